```python
import math
import jax
import jax.numpy as jnp
from jax import lax
import numpy as np

D_MODEL = 2048
BATCH = 16
SEQ = 256
DEPTH = 4
DEC_BATCH = 8
DEC_SEQ = 1024
PAST_LEN = 256

GRID_W = 64
D_GROUP = D_MODEL // 4
S5_CH = 16
S5_GROUPS = D_GROUP // S5_CH
S5_STATE = 64
RW_HEAD = 64
RW_HEADS = D_GROUP // RW_HEAD
RW_LORA_W = 64
RW_LORA_A = 64
RW_LORA_G = 128
RW_GN_EPS = 64e-5
NA_HEAD = 64
NA_HEADS = D_GROUP // NA_HEAD
NA_WIN_H = 8
NA_WIN_W = 16
NEG_INF = -1e30
ML_HEAD = 128
ML_HEADS = D_GROUP // ML_HEAD
ML_CHUNK = 64
ML_GN_EPS = 1e-5
ROPE_BASE = 10000.0
N_EXPERTS = 16
N_EXPERT_GROUPS = 4
EXPERTS_PER_GROUP = N_EXPERTS // N_EXPERT_GROUPS
TOP_K = 2
D_EXPERT = 512
DEEPNORM_ALPHA = (2 * DEPTH) ** 0.25
DEEPNORM_BETA = (8 * DEPTH) ** -0.25
LN_EPS = 1e-5
S5_IN = D_GROUP
RW_IN = 3 * D_GROUP + RW_LORA_W + RW_LORA_A + RW_LORA_G
NA_IN = 3 * D_GROUP
ML_IN = 4 * D_GROUP + 4 * ML_HEADS
D_IN = S5_IN + RW_IN + NA_IN + ML_IN
RW_SPLITS = [D_GROUP, 2 * D_GROUP, 3 * D_GROUP, 3 * D_GROUP + RW_LORA_W, 3 * D_GROUP + RW_LORA_W + RW_LORA_A]
ML_SPLITS = [D_GROUP, 2 * D_GROUP, 3 * D_GROUP, 4 * D_GROUP]

kernel_name = 'hybrid_s5_rwkv7_natten_mlstm_moe_dit_step'


def layer_norm(x, g, b):
    xf = x.astype(jnp.float32)
    mu = jnp.mean(xf, axis=-1, keepdims=True)
    var = jnp.mean(jnp.square(xf - mu), axis=-1, keepdims=True)
    return ((xf - mu) * lax.rsqrt(var + LN_EPS)).astype(x.dtype) * g + b


def head_norm(y, eps):
    mu = jnp.mean(y, axis=-1, keepdims=True)
    var = jnp.mean(jnp.square(y - mu), axis=-1, keepdims=True)
    return (y - mu) * lax.rsqrt(var + eps)


def centred_shift(x, mu_prev, mu_next):
    prev = jnp.pad(x[:, :-1], ((0, 0), (1, 0), (0, 0)))
    nxt = jnp.pad(x[:, 1:], ((0, 0), (0, 1), (0, 0)))
    return x + mu_prev * (prev - x) + mu_next * (nxt - x)


def _linear_combine(e1, e2):
    a1, b1 = e1
    a2, b2 = e2
    return a1 * a2, a2 * b1 + b2


def s5_scan(u, lam_bar, b_bar, c_mat, s0, reverse):
    bu = jnp.einsum('gph,blgh->blgp', b_bar, u.astype(b_bar.dtype))
    edge = -1 if reverse else 0
    bu = bu.at[:, edge].add(lam_bar * s0)
    a = jnp.broadcast_to(lam_bar, bu.shape)
    _, s = lax.associative_scan(_linear_combine, (a, bu), reverse=reverse, axis=1)
    y = jnp.einsum('ghp,blgp->blgh', c_mat, s).real
    s_last = s[:, 0] if reverse else s[:, -1]
    return y, s_last


def s5_mixer(u, lam_re, lam_im, log_step, b_re, b_im, c_re, c_im, d_skip, w_glu, s0_re, s0_im):
    bsz, L, _ = u.shape
    uf = u.astype(jnp.float32)
    lam = lax.complex(lam_re.astype(jnp.float32), lam_im.astype(jnp.float32))
    dt = jnp.exp(log_step.astype(jnp.float32))[..., None]
    lam_bar = jnp.exp(lam * dt)
    b_bar = ((lam_bar - 1.0) / lam)[..., None] * lax.complex(b_re.astype(jnp.float32), b_im.astype(jnp.float32))
    c_mat = lax.complex(c_re.astype(jnp.float32), c_im.astype(jnp.float32))
    s0 = lax.complex(s0_re.astype(jnp.float32), s0_im.astype(jnp.float32))
    ug = uf.reshape(bsz, L, S5_GROUPS, S5_CH)
    y_f, s_f = s5_scan(ug, lam_bar[0], b_bar[0], c_mat[0], s0[:, 0], False)
    y_b, s_b = s5_scan(ug, lam_bar[1], b_bar[1], c_mat[1], s0[:, 1], True)
    y = (y_f + y_b).reshape(bsz, L, D_GROUP) + d_skip.astype(jnp.float32) * uf
    y = jax.nn.gelu(y).astype(u.dtype)
    y = y * jax.nn.sigmoid(y @ w_glu)
    s_fin = jnp.stack([s_f, s_b], axis=1)
    return y, s_fin.real, s_fin.imag


def rwkv_scan(r, w, k, v, kk, a, s0, reverse):
    def step(S, inp):
        r_t, w_t, k_t, v_t, kk_t, a_t = inp
        sk = jnp.einsum('bhvk,bhk->bhv', S, kk_t)
        S = (S * w_t[:, :, None, :] - sk[..., None] * (kk_t * a_t)[:, :, None, :]
             + v_t[..., None] * k_t[:, :, None, :])
        return S, jnp.einsum('bhvk,bhk->bhv', S, r_t)
    xs = tuple(jnp.moveaxis(t, 1, 0) for t in (r, w, k, v, kk, a))
    S, ys = lax.scan(step, s0, xs, reverse=reverse)
    return jnp.moveaxis(ys, 0, 1), S


def rwkv_mixer(z, mu_prev, mu_next, w0, w2, a0, a2, g2, k_k, k_a, r_k, ln_g, ln_b, s0):
    bsz, L, _ = z.shape
    z = centred_shift(z, mu_prev, mu_next)
    r, k, v, zw, za, zg = jnp.split(z, RW_SPLITS, axis=-1)

    def heads(t):
        return t.astype(jnp.float32).reshape(bsz, L, RW_HEADS, RW_HEAD)

    rf, kf, vf = heads(r), heads(k), heads(v)
    kk = heads(k * k_k)
    kk = kk * lax.rsqrt(jnp.sum(jnp.square(kk), axis=-1, keepdims=True) + 1e-12)
    k_a_h = k_a.astype(jnp.float32).reshape(RW_HEADS, RW_HEAD)
    tw = jnp.tanh(zw)
    ys, ks, states = [], [], []
    for d in range(2):
        w_log = -jax.nn.softplus(-(w0[d] + tw @ w2[d]).astype(jnp.float32)) - 0.5
        decay = heads(jnp.exp(-jnp.exp(w_log)))
        a = heads(jax.nn.sigmoid((a0[d] + za @ a2[d]).astype(jnp.float32)))
        k_d = kf * (1.0 + (a - 1.0) * k_a_h)
        y_d, s_d = rwkv_scan(rf, decay, k_d, vf, kk, a, s0[:, d].astype(jnp.float32), d == 1)
        ys.append(y_d)
        ks.append(k_d)
        states.append(s_d)
    y = head_norm(ys[0] + ys[1], RW_GN_EPS).reshape(bsz, L, D_GROUP).astype(z.dtype) * ln_g + ln_b
    r_k_h = r_k.astype(jnp.float32).reshape(RW_HEADS, RW_HEAD)
    bonus = jnp.sum(rf * (0.5 * (ks[0] + ks[1])) * r_k_h, axis=-1, keepdims=True) * vf
    g = jax.nn.sigmoid(zg) @ g2
    out = (y + bonus.reshape(bsz, L, D_GROUP).astype(z.dtype)) * g
    return out, jnp.stack(states, axis=1)


def context_attention(q, k, v):
    s = jnp.einsum('bhqd,bhkd->bhqk', q, k).astype(jnp.float32) * (NA_HEAD ** -0.5)
    p = jax.nn.softmax(s, axis=-1).astype(v.dtype)
    return jnp.einsum('bhqk,bhkd->bhqd', p, v)


def neighbourhood_attention(q, k, v, k_ctx, v_ctx, rpb):
    bsz, H, L, d = q.shape
    rows = L // GRID_W
    kh = min(NA_WIN_H, rows)
    qr = jnp.arange(rows)
    row_start = jnp.clip(qr - kh // 2, 0, rows - kh)
    key_rows = row_start[:, None] + jnp.arange(kh)[None, :]
    qc = jnp.arange(GRID_W)
    col_start = jnp.clip(qc - NA_WIN_W // 2, 0, GRID_W - NA_WIN_W)
    kc = jnp.arange(GRID_W)
    col_in = (kc[None, :] >= col_start[:, None]) & (kc[None, :] < col_start[:, None] + NA_WIN_W)
    qg = q.reshape(bsz, H, rows, GRID_W, d)
    kg = k.reshape(bsz, H, rows, GRID_W, d)[:, :, key_rows]
    vg = v.reshape(bsz, H, rows, GRID_W, d)[:, :, key_rows]
    scale = NA_HEAD ** -0.5
    s_loc = jnp.einsum('bhrqd,bhrjkd->bhrqjk', qg, kg).astype(jnp.float32) * scale
    dr_i = key_rows - qr[:, None] + (NA_WIN_H - 1)
    dc_i = jnp.clip(kc[None, :] - qc[:, None], -(NA_WIN_W - 1), NA_WIN_W - 1) + (NA_WIN_W - 1)
    bias = rpb[:, dr_i[:, :, None, None], dc_i[None, None, :, :]]
    s_loc = s_loc + jnp.transpose(bias, (0, 1, 3, 2, 4))[None].astype(jnp.float32)
    s_loc = jnp.where(col_in[:, None, :], s_loc, NEG_INF)
    s_loc = s_loc.reshape(bsz, H, rows, GRID_W, kh * GRID_W)
    s_ctx = jnp.einsum('bhrqd,bhcd->bhrqc', qg, k_ctx).astype(jnp.float32) * scale
    p = jax.nn.softmax(jnp.concatenate([s_loc, s_ctx], axis=-1), axis=-1)
    p_loc = p[..., :kh * GRID_W].reshape(bsz, H, rows, GRID_W, kh, GRID_W).astype(v.dtype)
    p_ctx = p[..., kh * GRID_W:].astype(v.dtype)
    o = (jnp.einsum('bhrqjk,bhrjkd->bhrqd', p_loc, vg)
         + jnp.einsum('bhrqc,bhcd->bhrqd', p_ctx, v_ctx))
    return o.reshape(bsz, H, L, d)


def axial_rope(x):
    L, d = x.shape[2], x.shape[3]
    half = d // 2
    quarter = half // 2
    t = jnp.arange(L)
    inv_freq = ROPE_BASE ** (-jnp.arange(quarter, dtype=jnp.float32) / quarter)

    def rotate(xp, pos):
        ang = pos.astype(jnp.float32)[:, None] * inv_freq[None, :]
        cos, sin = jnp.cos(ang), jnp.sin(ang)
        x1, x2 = xp[..., :quarter], xp[..., quarter:]
        return jnp.concatenate([x1 * cos - x2 * sin, x1 * sin + x2 * cos], axis=-1)

    return jnp.concatenate([rotate(x[..., :half], t // GRID_W), rotate(x[..., half:], t % GRID_W)], axis=-1)


def mlstm_chunkwise(q, k, v, ig, fg, c0, n0, m0):
    bsz, H, L, d = q.shape
    nc = L // ML_CHUNK

    def chunks(t):
        return jnp.moveaxis(t.reshape(bsz, H, nc, ML_CHUNK, *t.shape[3:]), 2, 0)

    logf = jax.nn.log_sigmoid(fg)
    causal = jnp.tril(jnp.ones((ML_CHUNK, ML_CHUNK), dtype=bool))

    def step(carry, inp):
        C, n, m = carry
        qc, kc, vc, ic, lfc = inp
        b = jnp.cumsum(lfc, axis=-1)
        dmat = jnp.where(causal, b[..., :, None] - b[..., None, :] + ic[..., None, :], -jnp.inf)
        inter = b + m[..., None]
        m_t = jnp.maximum(inter, jnp.max(dmat, axis=-1))
        w_intra = jnp.exp(dmat - m_t[..., None])
        w_inter = jnp.exp(inter - m_t)
        qk = jnp.einsum('bhtd,bhsd->bhts', qc, kc) * w_intra
        num = (w_inter[..., None] * jnp.einsum('bhtd,bhde->bhte', qc, C)
               + jnp.einsum('bhts,bhse->bhte', qk, vc))
        den = w_inter * jnp.einsum('bhtd,bhd->bht', qc, n) + jnp.sum(qk, axis=-1)
        h = num / jnp.maximum(jnp.abs(den), jnp.exp(-m_t))[..., None]
        b_last = b[..., -1]
        g_s = b_last[..., None] - b + ic
        m_new = jnp.maximum(b_last + m, jnp.max(g_s, axis=-1))
        w_old = jnp.exp(b_last + m - m_new)
        w_s = jnp.exp(g_s - m_new[..., None])
        C = w_old[..., None, None] * C + jnp.einsum('bhs,bhsd,bhse->bhde', w_s, kc, vc)
        n = w_old[..., None] * n + jnp.einsum('bhs,bhsd->bhd', w_s, kc)
        return (C, n, m_new), h

    (C, n, m), hs = lax.scan(step, (c0, n0, m0), (chunks(q), chunks(k), chunks(v), chunks(ig), chunks(logf)))
    return jnp.moveaxis(hs, 0, 2).reshape(bsz, H, L, d), C, n, m


def mlstm_mixer(z, i_bias, f_bias, ln_g, ln_b, c0, n0, m0, rotary):
    bsz, L, _ = z.shape
    q, k, v, o, gates = jnp.split(z, ML_SPLITS, axis=-1)

    def heads(t):
        return t.astype(jnp.float32).reshape(bsz, L, ML_HEADS, ML_HEAD).transpose(0, 2, 1, 3)

    q, k, v = heads(q), heads(k), heads(v)
    if rotary:
        q, k = axial_rope(q), axial_rope(k)
    q = q * (ML_HEAD ** -0.5)
    pre = gates.astype(jnp.float32).reshape(bsz, L, 2, 2, ML_HEADS).transpose(2, 3, 0, 4, 1)
    hs, cs, ns, ms = [], [], [], []
    for d in range(2):
        ig = pre[0, d] + i_bias[d].astype(jnp.float32)[None, :, None]
        fg = pre[1, d] + f_bias[d].astype(jnp.float32)[None, :, None]
        args = (q, k, v, ig, fg)
        if d == 1:
            args = tuple(jnp.flip(t, axis=2) for t in args)
        h_d, c_d, n_d, m_d = mlstm_chunkwise(*args, c0[:, d].astype(jnp.float32),
                                             n0[:, d].astype(jnp.float32), m0[:, d].astype(jnp.float32))
        hs.append(h_d if d == 0 else jnp.flip(h_d, axis=2))
        cs.append(c_d)
        ns.append(n_d)
        ms.append(m_d)
    h = head_norm(hs[0] + hs[1], ML_GN_EPS).transpose(0, 2, 1, 3).reshape(bsz, L, D_GROUP)
    out = jax.nn.sigmoid(o) * (h.astype(z.dtype) * ln_g + ln_b)
    return out, jnp.stack(cs, axis=1), jnp.stack(ns, axis=1), jnp.stack(ms, axis=1)


def moe_ffn(h, w_router, b_router, w_gate, w_up, w_down):
    bsz, L, D = h.shape
    t = h.reshape(bsz * L, D)
    scores = jax.nn.sigmoid((t @ w_router).astype(jnp.float32))
    sel = scores + b_router.astype(jnp.float32)
    grp_score = jnp.sum(lax.top_k(sel.reshape(-1, N_EXPERT_GROUPS, EXPERTS_PER_GROUP), TOP_K)[0], axis=-1)
    best = jnp.argmax(grp_score, axis=-1)
    in_grp = jnp.repeat(jax.nn.one_hot(best, N_EXPERT_GROUPS, dtype=jnp.bool_), EXPERTS_PER_GROUP, axis=-1)
    _, idx = lax.top_k(jnp.where(in_grp, sel, -jnp.inf), TOP_K)
    g = jnp.take_along_axis(scores, idx, axis=-1)
    g = g / jnp.sum(g, axis=-1, keepdims=True)
    gates = jnp.sum(jax.nn.one_hot(idx, N_EXPERTS, dtype=jnp.float32) * g[..., None], axis=1)
    a = jnp.einsum('td,edf->tef', t, w_gate)
    u = jnp.einsum('td,edf->tef', t, w_up)
    hid = jax.nn.silu(a) * u * gates[:, :, None].astype(t.dtype)
    return jnp.einsum('tef,efd->td', hid, w_down).reshape(bsz, L, D)


def trunk_layer(x, mod, lw, st, ctx_kv):
    bsz, L, _ = x.shape
    shift1, scale1, gate1, shift2, scale2, gate2 = jnp.split(mod[:, None, :], 6, axis=-1)
    h = x * (1.0 + scale1) + shift1
    z = h @ lw['w_in']
    z_s5, z_rw, z_na, z_ml = jnp.split(z, [S5_IN, S5_IN + RW_IN, S5_IN + RW_IN + NA_IN], axis=-1)
    s5_re0, s5_im0, rw0, ml_c0, ml_n0, ml_m0 = st
    y_s5, s5_re, s5_im = s5_mixer(z_s5, lw['s5_lam_re'], lw['s5_lam_im'], lw['s5_log_step'],
                                  lw['s5_b_re'], lw['s5_b_im'], lw['s5_c_re'], lw['s5_c_im'],
                                  lw['s5_d'], lw['s5_w_glu'], s5_re0, s5_im0)
    y_rw, rw_s = rwkv_mixer(z_rw, lw['rw_mu_prev'], lw['rw_mu_next'], lw['rw_w0'], lw['rw_w2'],
                            lw['rw_a0'], lw['rw_a2'], lw['rw_g2'], lw['rw_k_k'], lw['rw_k_a'],
                            lw['rw_r_k'], lw['rw_ln_g'], lw['rw_ln_b'], rw0)
    q, k, v = [t.reshape(bsz, L, NA_HEADS, NA_HEAD).transpose(0, 2, 1, 3) for t in jnp.split(z_na, 3, axis=-1)]
    if ctx_kv is None:
        o = context_attention(q, k, v)
    else:
        o = neighbourhood_attention(q, k, v, ctx_kv[0], ctx_kv[1], lw['na_rpb'])
    y_na = o.transpose(0, 2, 1, 3).reshape(bsz, L, D_GROUP)
    y_ml, ml_c, ml_n, ml_m = mlstm_mixer(z_ml, lw['ml_i_bias'], lw['ml_f_bias'], lw['ml_ln_g'], lw['ml_ln_b'],
                                         ml_c0, ml_n0, ml_m0, ctx_kv is not None)
    y = jnp.concatenate([y_s5, y_rw, y_na, y_ml], axis=-1) @ lw['w_out']
    x = layer_norm(DEEPNORM_ALPHA * x + gate1 * y, lw['ln1_g'], lw['ln1_b'])
    h = x * (1.0 + scale2) + shift2
    f = moe_ffn(h, lw['w_router'], lw['b_router'], lw['moe_w_gate'], lw['moe_w_up'], lw['moe_w_down'])
    x = layer_norm(DEEPNORM_ALPHA * x + gate2 * f, lw['ln2_g'], lw['ln2_b'])
    return x, (k, v, s5_re, s5_im, rw_s, ml_c, ml_n, ml_m)


def setup_inputs(seed: int = 0) -> dict:
    key = jax.random.key(seed)
    keys = iter(jax.random.split(key, 64))

    def nrm(shape, scale):
        return jax.random.normal(next(keys), shape, jnp.float32) * scale

    def unif(shape, lo, hi):
        return jax.random.uniform(next(keys), shape, jnp.float32, lo, hi)

    D, G = D_MODEL, D_GROUP
    n_idx = jnp.arange(S5_STATE, dtype=jnp.float32)
    return {
        'x_prompt': nrm((BATCH, SEQ, D), 1.0),
        'x_sample': nrm((DEC_BATCH, DEC_SEQ, D), 1.0),
        'cache_nat_k': nrm((DEC_BATCH, DEPTH, NA_HEADS, PAST_LEN, NA_HEAD), 1.0),
        'cache_nat_v': nrm((DEC_BATCH, DEPTH, NA_HEADS, PAST_LEN, NA_HEAD), 1.0),
        'state_s5_re': nrm((DEC_BATCH, DEPTH, 2, S5_GROUPS, S5_STATE), 0.3),
        'state_s5_im': nrm((DEC_BATCH, DEPTH, 2, S5_GROUPS, S5_STATE), 0.3),
        'state_rwkv': nrm((DEC_BATCH, DEPTH, 2, RW_HEADS, RW_HEAD, RW_HEAD), 0.3),
        'state_mlstm_c': nrm((DEC_BATCH, DEPTH, 2, ML_HEADS, ML_HEAD, ML_HEAD), 0.3),
        'state_mlstm_n': nrm((DEC_BATCH, DEPTH, 2, ML_HEADS, ML_HEAD), 0.3),
        'state_mlstm_m': nrm((DEC_BATCH, DEPTH, 2, ML_HEADS), 1.0),
        'c': nrm((DEC_BATCH, D), 1.0),
        'c_ctx': nrm((D,), 1.0),
        'w_mod': nrm((DEPTH, D, 6 * D), 0.5 * D ** -0.5),
        'b_mod': nrm((DEPTH, 6 * D), 0.02),
        'w_in': nrm((DEPTH, D, D_IN), D ** -0.5),
        'w_out': nrm((DEPTH, D, D), D ** -0.5 * DEEPNORM_BETA),
        's5_lam_re': -0.5 + nrm((DEPTH, 2, S5_GROUPS, S5_STATE), 0.01),
        's5_lam_im': math.pi * n_idx + nrm((DEPTH, 2, S5_GROUPS, S5_STATE), 0.01),
        's5_log_step': unif((DEPTH, 2, S5_GROUPS), math.log(1e-3), math.log(1e-1)),
        's5_b_re': nrm((DEPTH, 2, S5_GROUPS, S5_STATE, S5_CH), (2 * S5_CH) ** -0.5),
        's5_b_im': nrm((DEPTH, 2, S5_GROUPS, S5_STATE, S5_CH), (2 * S5_CH) ** -0.5),
        's5_c_re': nrm((DEPTH, 2, S5_GROUPS, S5_CH, S5_STATE), (2 * S5_STATE) ** -0.5),
        's5_c_im': nrm((DEPTH, 2, S5_GROUPS, S5_CH, S5_STATE), (2 * S5_STATE) ** -0.5),
        's5_d': nrm((DEPTH, G), 1.0),
        's5_w_glu': nrm((DEPTH, G, G), G ** -0.5),
        'rw_mu_prev': unif((DEPTH, RW_IN), 0.0, 0.5),
        'rw_mu_next': unif((DEPTH, RW_IN), 0.0, 0.5),
        'rw_w0': unif((DEPTH, 2, G), -4.0, 1.0),
        'rw_w2': nrm((DEPTH, 2, RW_LORA_W, G), 0.1 * RW_LORA_W ** -0.5),
        'rw_a0': nrm((DEPTH, 2, G), 0.1),
        'rw_a2': nrm((DEPTH, 2, RW_LORA_A, G), 0.1 * RW_LORA_A ** -0.5),
        'rw_g2': nrm((DEPTH, RW_LORA_G, G), RW_LORA_G ** -0.5),
        'rw_k_k': 0.85 + nrm((DEPTH, G), 0.02),
        'rw_k_a': 1.0 + nrm((DEPTH, G), 0.02),
        'rw_r_k': nrm((DEPTH, G), 0.1),
        'rw_ln_g': 1.0 + nrm((DEPTH, G), 0.02),
        'rw_ln_b': nrm((DEPTH, G), 0.02),
        'na_rpb': nrm((DEPTH, NA_HEADS, 2 * NA_WIN_H - 1, 2 * NA_WIN_W - 1), 0.02),
        'ml_i_bias': nrm((DEPTH, 2, ML_HEADS), 0.1),
        'ml_f_bias': jnp.linspace(3.0, 6.0, ML_HEADS) + nrm((DEPTH, 2, ML_HEADS), 0.1),
        'ml_ln_g': 1.0 + nrm((DEPTH, G), 0.02),
        'ml_ln_b': nrm((DEPTH, G), 0.02),
        'ln1_g': 1.0 + nrm((DEPTH, D), 0.02),
        'ln1_b': nrm((DEPTH, D), 0.02),
        'ln2_g': 1.0 + nrm((DEPTH, D), 0.02),
        'ln2_b': nrm((DEPTH, D), 0.02),
        'w_router': nrm((D, N_EXPERTS), D ** -0.5),
        'b_router': nrm((N_EXPERTS,), 0.01),
        'moe_w_gate': nrm((DEPTH, N_EXPERTS, D, D_EXPERT), D ** -0.5),
        'moe_w_up': nrm((DEPTH, N_EXPERTS, D, D_EXPERT), D ** -0.5),
        'moe_w_down': nrm((DEPTH, N_EXPERTS, D_EXPERT, D), D_EXPERT ** -0.5 * DEEPNORM_BETA),
    }


def reference(x_prompt, x_sample, cache_nat_k, cache_nat_v, state_s5_re, state_s5_im, state_rwkv,
              state_mlstm_c, state_mlstm_n, state_mlstm_m, c, c_ctx, w_mod, b_mod, w_in, w_out,
              s5_lam_re, s5_lam_im, s5_log_step, s5_b_re, s5_b_im, s5_c_re, s5_c_im, s5_d, s5_w_glu,
              rw_mu_prev, rw_mu_next, rw_w0, rw_w2, rw_a0, rw_a2, rw_g2, rw_k_k, rw_k_a, rw_r_k,
              rw_ln_g, rw_ln_b, na_rpb, ml_i_bias, ml_f_bias, ml_ln_g, ml_ln_b,
              ln1_g, ln1_b, ln2_g, ln2_b, w_router, b_router, moe_w_gate, moe_w_up, moe_w_down):
    dt = x_prompt.dtype
    bp = x_prompt.shape[0]
    cond_ctx = jax.nn.silu(c_ctx)[None, :]
    cond_lat = jax.nn.silu(c)
    xp, xs = x_prompt, x_sample
    nk, nv, n_s5r, n_s5i, n_rw, n_mc, n_mn, n_mm = [], [], [], [], [], [], [], []
    for l in range(DEPTH):
        lw = {
            'w_in': w_in[l], 'w_out': w_out[l],
            's5_lam_re': s5_lam_re[l], 's5_lam_im': s5_lam_im[l], 's5_log_step': s5_log_step[l],
            's5_b_re': s5_b_re[l], 's5_b_im': s5_b_im[l], 's5_c_re': s5_c_re[l], 's5_c_im': s5_c_im[l],
            's5_d': s5_d[l], 's5_w_glu': s5_w_glu[l],
            'rw_mu_prev': rw_mu_prev[l], 'rw_mu_next': rw_mu_next[l], 'rw_w0': rw_w0[l], 'rw_w2': rw_w2[l],
            'rw_a0': rw_a0[l], 'rw_a2': rw_a2[l], 'rw_g2': rw_g2[l], 'rw_k_k': rw_k_k[l], 'rw_k_a': rw_k_a[l],
            'rw_r_k': rw_r_k[l], 'rw_ln_g': rw_ln_g[l], 'rw_ln_b': rw_ln_b[l],
            'na_rpb': na_rpb[l],
            'ml_i_bias': ml_i_bias[l], 'ml_f_bias': ml_f_bias[l], 'ml_ln_g': ml_ln_g[l], 'ml_ln_b': ml_ln_b[l],
            'ln1_g': ln1_g[l], 'ln1_b': ln1_b[l], 'ln2_g': ln2_g[l], 'ln2_b': ln2_b[l],
            'w_router': w_router, 'b_router': b_router,
            'moe_w_gate': moe_w_gate[l], 'moe_w_up': moe_w_up[l], 'moe_w_down': moe_w_down[l],
        }
        zero_st = (jnp.zeros((bp, 2, S5_GROUPS, S5_STATE), dt), jnp.zeros((bp, 2, S5_GROUPS, S5_STATE), dt),
                   jnp.zeros((bp, 2, RW_HEADS, RW_HEAD, RW_HEAD), dt),
                   jnp.zeros((bp, 2, ML_HEADS, ML_HEAD, ML_HEAD), dt),
                   jnp.zeros((bp, 2, ML_HEADS, ML_HEAD), dt), jnp.zeros((bp, 2, ML_HEADS), dt))
        xp, ctx_t = trunk_layer(xp, cond_ctx @ w_mod[l] + b_mod[l], lw, zero_st, None)
        nk.append(ctx_t[0])
        nv.append(ctx_t[1])
        n_s5r.append(ctx_t[2])
        n_s5i.append(ctx_t[3])
        n_rw.append(ctx_t[4])
        n_mc.append(ctx_t[5])
        n_mn.append(ctx_t[6])
        n_mm.append(ctx_t[7])
        lat_st = (state_s5_re[:, l], state_s5_im[:, l], state_rwkv[:, l],
                  state_mlstm_c[:, l], state_mlstm_n[:, l], state_mlstm_m[:, l])
        xs, _ = trunk_layer(xs, cond_lat @ w_mod[l] + b_mod[l], lw, lat_st,
                            (cache_nat_k[:, l], cache_nat_v[:, l]))
    return (xp, xs,
            jnp.stack(nk, axis=1), jnp.stack(nv, axis=1),
            jnp.stack(n_s5r, axis=1).astype(dt), jnp.stack(n_s5i, axis=1).astype(dt),
            jnp.stack(n_rw, axis=1).astype(dt), jnp.stack(n_mc, axis=1).astype(dt),
            jnp.stack(n_mn, axis=1).astype(dt), jnp.stack(n_mm, axis=1).astype(dt))
```

```python
import functools
import math

import numpy as np
import jax
import jax.numpy as jnp
from jax import lax
from jax.experimental import pallas as pl
from jax.experimental.pallas import tpu as pltpu

F32 = jnp.float32
BF16 = jnp.bfloat16

D_MODEL = 2048
BATCH = 16
SEQ = 256
DEPTH = 4
DEC_BATCH = 8
DEC_SEQ = 1024
PAST_LEN = 256
GRID_W = 64
D_GROUP = D_MODEL // 4
S5_CH = 16
S5_GROUPS = D_GROUP // S5_CH
S5_STATE = 64
RW_HEAD = 64
RW_HEADS = D_GROUP // RW_HEAD
RW_LORA_W = 64
RW_LORA_A = 64
RW_LORA_G = 128
RW_GN_EPS = 64e-5
NA_HEAD = 64
NA_HEADS = D_GROUP // NA_HEAD
NA_WIN_H = 8
NA_WIN_W = 16
NEG_INF = -1e30
ML_HEAD = 128
ML_HEADS = D_GROUP // ML_HEAD
ML_CHUNK = 64
ML_GN_EPS = 1e-5
ROPE_BASE = 10000.0
N_EXPERTS = 16
N_EXPERT_GROUPS = 4
EXPERTS_PER_GROUP = N_EXPERTS // N_EXPERT_GROUPS
D_EXPERT = 512
DEEPNORM_ALPHA = (2 * DEPTH) ** 0.25
LN_EPS = 1e-5
S5_IN = D_GROUP
RW_IN = 3 * D_GROUP + RW_LORA_W + RW_LORA_A + RW_LORA_G
NA_IN = 3 * D_GROUP
ML_IN = 4 * D_GROUP + 4 * ML_HEADS
D_IN = S5_IN + RW_IN + NA_IN + ML_IN
N_GATE = 4 * ML_HEADS
D_Z = D_IN - N_GATE
RW_OFF = S5_IN
NA_OFF = S5_IN + RW_IN
ML_OFF = NA_OFF + NA_IN
MOD_ROWS = 16
LANE = 128
SUB = 8

VMEM_LIMIT = 56 * 1024 * 1024


def _cparams(*sem):
    return pltpu.CompilerParams(dimension_semantics=sem, vmem_limit_bytes=VMEM_LIMIT)


def _dg(a, b, dims):
    return lax.dot_general(a, b, (dims, ((), ())), preferred_element_type=F32)


NN = ((1,), (0,))
NT = ((1,), (1,))
TN = ((0,), (0,))


def _dot1(a, b, dims=NN):
    return _dg(a.astype(BF16), b.astype(BF16), dims)


def _split(x):
    hi = x.astype(BF16)
    lo = (x - hi.astype(F32)).astype(BF16)
    return hi, lo


def _split3(x):
    hi = x.astype(BF16)
    r = x - hi.astype(F32)
    mid = r.astype(BF16)
    lo = (r - mid.astype(F32)).astype(BF16)
    return hi, mid, lo


def _dot3(a, b, dims=NN):
    ah, al = _split(a)
    bh, bl = _split(b)
    return _dg(ah, bh, dims) + (_dg(ah, bl, dims) + _dg(al, bh, dims))


def _dot_sel_l(sel, x, dims=NN):
    s = sel.astype(BF16)
    hi, mid, lo = _split3(x)
    return _dg(s, hi, dims) + (_dg(s, mid, dims) + _dg(s, lo, dims))


def _dot_sel_r(x, sel, dims=NN):
    s = sel.astype(BF16)
    hi, mid, lo = _split3(x)
    return _dg(hi, s, dims) + (_dg(mid, s, dims) + _dg(lo, s, dims))


def _sigmoid(x):
    return 1.0 / (1.0 + jnp.exp(-x))


def _silu(x):
    return x * _sigmoid(x)


def _softplus(x):
    return jnp.maximum(x, 0.0) + jnp.log(1.0 + jnp.exp(-jnp.abs(x)))


def _iota(shape, dim):
    return lax.broadcasted_iota(jnp.int32, shape, dim)


class _Stream:
    def __init__(self, nb, L, latent):
        self.nb, self.L, self.latent = nb, L, latent
        self.rows = nb * L

    def mod_row(self, tile, tile_rows):
        if not self.latent:
            return 0
        return 1 + (tile * tile_rows) // self.L


PROMPT = _Stream(BATCH, SEQ, False)
LATENT = _Stream(DEC_BATCH, DEC_SEQ, True)


MOD_TN = 768


def _mod_kernel(cond_ref, w_ref, b_ref, o_ref):
    c = _silu(cond_ref[...])
    o_ref[0] = _dot3(c, w_ref[0]) + b_ref[0]


def _modulation(cond, w_mod, b_mod):
    n = 6 * D_MODEL
    return pl.pallas_call(
        _mod_kernel,
        grid=(DEPTH, n // MOD_TN),
        in_specs=[pl.BlockSpec((MOD_ROWS, D_MODEL), lambda l, j: (0, 0)),
                  pl.BlockSpec((1, D_MODEL, MOD_TN), lambda l, j: (l, 0, j)),
                  pl.BlockSpec((1, 1, MOD_TN), lambda l, j: (l, 0, j))],
        out_specs=pl.BlockSpec((1, MOD_ROWS, MOD_TN), lambda l, j: (l, 0, j)),
        out_shape=jax.ShapeDtypeStruct((DEPTH, MOD_ROWS, n), F32),
        compiler_params=_cparams("parallel", "parallel"),
        name="modulation",
    )(cond, w_mod, b_mod.reshape(DEPTH, 1, n))


ZP_TM = 512
ZP_TN = 256


def _zproj_kernel(x_ref, mod_ref, w_ref, wg_ref, z_ref, zg_ref, h_scr):
    j = pl.program_id(1)

    @pl.when(j == 0)
    def _():
        m = mod_ref[0, 0]
        shift1 = m[:, 0:D_MODEL]
        scale1 = m[:, D_MODEL:2 * D_MODEL]
        h = x_ref[...] * (1.0 + scale1) + shift1
        h_scr[...] = h.astype(BF16)
        zg_ref[...] = _dot3(h, wg_ref[0])

    z_ref[...] = jnp.dot(h_scr[...], w_ref[0].astype(BF16), preferred_element_type=F32)


def _zproj(st, x, mod4, w_in, w_gate_cols, l):
    return pl.pallas_call(
        _zproj_kernel,
        grid=(st.rows // ZP_TM, D_Z // ZP_TN),
        in_specs=[pl.BlockSpec((ZP_TM, D_MODEL), lambda i, j: (i, 0)),
                  pl.BlockSpec((1, 1, 1, 6 * D_MODEL), lambda i, j: (l, st.mod_row(i, ZP_TM), 0, 0)),
                  pl.BlockSpec((1, D_MODEL, ZP_TN), lambda i, j: (l, 0, j)),
                  pl.BlockSpec((1, D_MODEL, N_GATE), lambda i, j: (l, 0, 0))],
        out_specs=[pl.BlockSpec((ZP_TM, ZP_TN), lambda i, j: (i, j)),
                   pl.BlockSpec((ZP_TM, N_GATE), lambda i, j: (i, 0))],
        out_shape=[jax.ShapeDtypeStruct((st.rows, D_Z), F32),
                   jax.ShapeDtypeStruct((st.rows, N_GATE), F32)],
        scratch_shapes=[pltpu.VMEM((ZP_TM, D_MODEL), BF16)],
        compiler_params=_cparams("parallel", "arbitrary"),
        name="zproj",
    )(x, mod4, w_in, w_gate_cols)


OP_TM = 512
OP_TN = 512


def _layer_norm(v, g, b):
    mu = jnp.mean(v, axis=-1, keepdims=True)
    d = v - mu
    var = jnp.mean(d * d, axis=-1, keepdims=True)
    return d * lax.rsqrt(var + LN_EPS) * g + b


def _route(scores, b_router):
    sel = scores + b_router
    s = [sel[:, e:e + 1] for e in range(N_EXPERTS)]
    in_top2 = []
    for g in range(N_EXPERT_GROUPS):
        for i in range(EXPERTS_PER_GROUP):
            e = g * EXPERTS_PER_GROUP + i
            cnt = jnp.zeros_like(s[e])
            for jj in range(EXPERTS_PER_GROUP):
                if jj == i:
                    continue
                o = g * EXPERTS_PER_GROUP + jj
                beats = (s[o] > s[e]) if jj > i else (s[o] >= s[e])
                cnt = cnt + jnp.where(beats, 1.0, 0.0)
            in_top2.append(cnt < 2.0)
    grp = []
    for g in range(N_EXPERT_GROUPS):
        tot = jnp.zeros_like(s[0])
        for i in range(EXPERTS_PER_GROUP):
            e = g * EXPERTS_PER_GROUP + i
            tot = tot + jnp.where(in_top2[e], s[e], 0.0)
        grp.append(tot)
    lane = _iota(scores.shape, 1)
    keep = jnp.zeros(scores.shape, F32)
    for g in range(N_EXPERT_GROUPS):
        cnt = jnp.zeros_like(s[0])
        for o in range(N_EXPERT_GROUPS):
            if o == g:
                continue
            beats = (grp[o] > grp[g]) if o > g else (grp[o] >= grp[g])
            cnt = cnt + jnp.where(beats, 1.0, 0.0)
        best = cnt < 1.0
        for i in range(EXPERTS_PER_GROUP):
            e = g * EXPERTS_PER_GROUP + i
            on = jnp.where(best, jnp.where(in_top2[e], 1.0, 0.0), 0.0)
            keep = keep + jnp.where(lane == e, on, 0.0)
    picked = scores * keep
    return picked / jnp.sum(picked, axis=-1, keepdims=True)


def _oproj_kernel(y0_ref, y1_ref, y2_ref, y3_ref, w_ref, x_ref, mod_ref, g_ref, b_ref, wr_ref, br_ref,
                  x1_ref, h2_ref, gate_ref, acc_scr):
    j = pl.program_id(1)
    acc = jnp.zeros((OP_TM, OP_TN), F32)
    for k, yr in enumerate((y0_ref, y1_ref, y2_ref, y3_ref)):
        wk = w_ref[0, k * D_GROUP:(k + 1) * D_GROUP, :].astype(BF16)
        acc = acc + jnp.dot(yr[...], wk, preferred_element_type=F32)
    for jj in range(D_MODEL // OP_TN):
        @pl.when(j == jj)
        def _(jj=jj):
            acc_scr[:, jj * OP_TN:(jj + 1) * OP_TN] = acc

    @pl.when(j == D_MODEL // OP_TN - 1)
    def _():
        m = mod_ref[0, 0]
        gate1 = m[:, 2 * D_MODEL:3 * D_MODEL]
        shift2 = m[:, 3 * D_MODEL:4 * D_MODEL]
        scale2 = m[:, 4 * D_MODEL:5 * D_MODEL]
        x1 = _layer_norm(DEEPNORM_ALPHA * x_ref[...] + gate1 * acc_scr[...], g_ref[0], b_ref[0])
        x1_ref[...] = x1
        h2 = x1 * (1.0 + scale2) + shift2
        h2_ref[...] = h2.astype(BF16)
        scores = _sigmoid(_dot3(h2, wr_ref[...]))
        gate_ref[...] = _route(scores, br_ref[...])


def _oproj(st, ys, x, mod4, w_out, ln_g, ln_b, w_router, b_router, l):
    row = lambda i, j: (i, 0)
    return pl.pallas_call(
        _oproj_kernel,
        grid=(st.rows // OP_TM, D_MODEL // OP_TN),
        in_specs=[pl.BlockSpec((OP_TM, D_GROUP), row)] * 4 + [
            pl.BlockSpec((1, D_MODEL, OP_TN), lambda i, j: (l, 0, j)),
            pl.BlockSpec((OP_TM, D_MODEL), row),
            pl.BlockSpec((1, 1, 1, 6 * D_MODEL), lambda i, j: (l, st.mod_row(i, OP_TM), 0, 0)),
            pl.BlockSpec((1, 1, D_MODEL), lambda i, j: (l, 0, 0)),
            pl.BlockSpec((1, 1, D_MODEL), lambda i, j: (l, 0, 0)),
            pl.BlockSpec((D_MODEL, N_EXPERTS), lambda i, j: (0, 0)),
            pl.BlockSpec((1, N_EXPERTS), lambda i, j: (0, 0))],
        out_specs=[pl.BlockSpec((OP_TM, D_MODEL), row),
                   pl.BlockSpec((OP_TM, D_MODEL), row),
                   pl.BlockSpec((OP_TM, N_EXPERTS), row)],
        out_shape=[jax.ShapeDtypeStruct((st.rows, D_MODEL), F32),
                   jax.ShapeDtypeStruct((st.rows, D_MODEL), BF16),
                   jax.ShapeDtypeStruct((st.rows, N_EXPERTS), F32)],
        scratch_shapes=[pltpu.VMEM((OP_TM, D_MODEL), F32)],
        compiler_params=_cparams("parallel", "arbitrary"),
        name="oproj_ln1_router",
    )(*ys, w_out, x, mod4, ln_g.reshape(DEPTH, 1, D_MODEL), ln_b.reshape(DEPTH, 1, D_MODEL),
      w_router, b_router.reshape(1, N_EXPERTS))


MOE_TM = 512


def _moe_kernel(h_ref, gate_ref, wg_ref, wu_ref, wd_ref, f_ref):
    e = pl.program_id(1)
    h = h_ref[...]
    a = jnp.dot(h, wg_ref[0, 0].astype(BF16), preferred_element_type=F32)
    u = jnp.dot(h, wu_ref[0, 0].astype(BF16), preferred_element_type=F32)
    gates = gate_ref[...]
    ge = jnp.sum(jnp.where(_iota(gates.shape, 1) == e, gates, 0.0), axis=-1, keepdims=True)
    hid = (_silu(a) * u * ge).astype(BF16)
    contrib = jnp.dot(hid, wd_ref[0, 0].astype(BF16), preferred_element_type=F32)

    @pl.when(e == 0)
    def _():
        f_ref[...] = contrib

    @pl.when(e != 0)
    def _():
        f_ref[...] += contrib


def _moe(st, h2, gates, w_gate, w_up, w_down, l):
    return pl.pallas_call(
        _moe_kernel,
        grid=(st.rows // MOE_TM, N_EXPERTS),
        in_specs=[pl.BlockSpec((MOE_TM, D_MODEL), lambda i, e: (i, 0)),
                  pl.BlockSpec((MOE_TM, N_EXPERTS), lambda i, e: (i, 0)),
                  pl.BlockSpec((1, 1, D_MODEL, D_EXPERT), lambda i, e: (l, e, 0, 0)),
                  pl.BlockSpec((1, 1, D_MODEL, D_EXPERT), lambda i, e: (l, e, 0, 0)),
                  pl.BlockSpec((1, 1, D_EXPERT, D_MODEL), lambda i, e: (l, e, 0, 0))],
        out_specs=pl.BlockSpec((MOE_TM, D_MODEL), lambda i, e: (i, 0)),
        out_shape=jax.ShapeDtypeStruct((st.rows, D_MODEL), F32),
        compiler_params=_cparams("parallel", "arbitrary"),
        name="moe_dense",
    )(h2, gates, w_gate, w_up, w_down)


LN_TM = 512


def _ln2_kernel(x_ref, f_ref, mod_ref, g_ref, b_ref, o_ref):
    gate2 = mod_ref[0, 0][:, 5 * D_MODEL:6 * D_MODEL]
    o_ref[...] = _layer_norm(DEEPNORM_ALPHA * x_ref[...] + gate2 * f_ref[...], g_ref[0], b_ref[0])


def _ln2(st, x1, f, mod4, ln_g, ln_b, l):
    row = lambda i: (i, 0)
    return pl.pallas_call(
        _ln2_kernel,
        grid=(st.rows // LN_TM,),
        in_specs=[pl.BlockSpec((LN_TM, D_MODEL), row),
                  pl.BlockSpec((LN_TM, D_MODEL), row),
                  pl.BlockSpec((1, 1, 1, 6 * D_MODEL), lambda i: (l, st.mod_row(i, LN_TM), 0, 0)),
                  pl.BlockSpec((1, 1, D_MODEL), lambda i: (l, 0, 0)),
                  pl.BlockSpec((1, 1, D_MODEL), lambda i: (l, 0, 0))],
        out_specs=pl.BlockSpec((LN_TM, D_MODEL), row),
        out_shape=jax.ShapeDtypeStruct((st.rows, D_MODEL), F32),
        compiler_params=_cparams("parallel"),
        name="ln2",
    )(x1, f, mod4, ln_g.reshape(DEPTH, 1, D_MODEL), ln_b.reshape(DEPTH, 1, D_MODEL))


S5_TC = 64
S5_SEQS = SUB
S5_Q = 4
S5_QS = S5_GROUPS // S5_Q * S5_STATE
N_S5 = S5_GROUPS * S5_STATE


def _s5_prep_kernel(lr_ref, li_ref, ls_ref, br_ref, bi_ref, e_ref, lbr_ref, lbi_ref, bbr_ref, bbi_ref):
    lr = lr_ref[0]
    li = li_ref[0]
    dt = jnp.exp(ls_ref[0])
    mag = jnp.exp(lr * dt)
    ang = li * dt
    ar = mag * jnp.cos(ang)
    ai = mag * jnp.sin(ang)
    lbr_ref[0] = ar
    lbi_ref[0] = ai
    den = lr * lr + li * li
    nr = ar - 1.0
    cr = (nr * lr + ai * li) / den
    ci = (ai * lr - nr * li) / den
    cr = _dot_sel_l(e_ref[...], cr)
    ci = _dot_sel_l(e_ref[...], ci)
    bre = br_ref[0]
    bim = bi_ref[0]
    bbr_ref[0] = cr * bre - ci * bim
    bbi_ref[0] = cr * bim + ci * bre


def _s5_prep(lam_re, lam_im, log_step, b_re, b_im):
    d2 = DEPTH * 2
    g, p, h = S5_GROUPS, S5_STATE, S5_CH
    bt_re = jnp.swapaxes(b_re, -1, -2).reshape(d2, g * h, p)
    bt_im = jnp.swapaxes(b_im, -1, -2).reshape(d2, g * h, p)
    expand = jnp.asarray(np.kron(np.eye(g, dtype=np.float32), np.ones((h, 1), np.float32)))
    spec_gp = pl.BlockSpec((1, g, p), lambda i: (i, 0, 0))
    spec_b = pl.BlockSpec((1, g * h, p), lambda i: (i, 0, 0))
    lbr, lbi, bbr, bbi = pl.pallas_call(
        _s5_prep_kernel,
        grid=(d2,),
        in_specs=[spec_gp, spec_gp, pl.BlockSpec((1, g, 1), lambda i: (i, 0, 0)), spec_b, spec_b,
                  pl.BlockSpec((g * h, g), lambda i: (0, 0))],
        out_specs=[spec_gp, spec_gp, spec_b, spec_b],
        out_shape=[jax.ShapeDtypeStruct((d2, g, p), F32)] * 2 + [jax.ShapeDtypeStruct((d2, g * h, p), F32)] * 2,
        compiler_params=_cparams("parallel"),
        name="s5_prep",
    )(lam_re.reshape(d2, g, p), lam_im.reshape(d2, g, p), log_step.reshape(d2, g, 1), bt_re, bt_im, expand)
    return lbr, lbi, bbr, bbi


def _s5_block_params(lbr, lbi, bbr, bbi, c_re, c_im):
    d2 = DEPTH * 2
    gq = S5_GROUPS // S5_Q
    eye = jnp.eye(gq, dtype=F32)

    def b_blocks(b):
        b = b.reshape(d2, S5_Q, gq, S5_CH, S5_STATE)
        return jnp.einsum('djghp,gk->djghkp', b, eye).reshape(d2, S5_Q, gq * S5_CH, gq * S5_STATE)

    def c_blocks(c):
        c = c.reshape(d2, S5_Q, gq, S5_CH, S5_STATE)
        return jnp.einsum('djghp,gk->djgpkh', c, eye).reshape(d2, S5_Q, gq * S5_STATE, gq * S5_CH)

    bq = jnp.concatenate([b_blocks(bbr), b_blocks(bbi)], axis=-1)
    cq = jnp.concatenate([c_blocks(c_re), -c_blocks(c_im)], axis=-2)
    return bq, cq, lbr.reshape(d2, 1, N_S5), lbi.reshape(d2, 1, N_S5)


def _s5_scan_kernel(u_ref, bq_ref, cq_ref, lr_ref, li_ref, s0r_ref, s0i_ref, y_ref, sfr_ref, sfi_ref,
                    utb, bur, bui, ytb, sr_scr, si_scr):
    d = pl.program_id(1)
    c = pl.program_id(2)
    nrow = S5_TC * S5_SEQS
    cw = D_GROUP // S5_Q

    @pl.when(c == 0)
    def _():
        sr_scr[...] = s0r_ref[0, 0]
        si_scr[...] = s0i_ref[0, 0]

    for b in range(S5_SEQS):
        for j in range(S5_Q):
            utb[j, pl.ds(b, S5_TC, stride=S5_SEQS), :] = u_ref[b, :, j * cw:(j + 1) * cw]
    for j in range(S5_Q):
        bu = _dot3(utb[j], bq_ref[0, j])
        bur[:, j * S5_QS:(j + 1) * S5_QS] = bu[:, :S5_QS]
        bui[:, j * S5_QS:(j + 1) * S5_QS] = bu[:, S5_QS:]
    for j in range(S5_Q):
        sl = slice(j * S5_QS, (j + 1) * S5_QS)
        lam_r = jnp.broadcast_to(lr_ref[0][:, sl], (S5_SEQS, S5_QS))
        lam_i = jnp.broadcast_to(li_ref[0][:, sl], (S5_SEQS, S5_QS))

        def step(t, carry, sl=sl, lam_r=lam_r, lam_i=lam_i):
            sr, si = carry
            te = jnp.where(d == 0, t, S5_TC - 1 - t)
            rows = pl.ds(pl.multiple_of(te * S5_SEQS, S5_SEQS), S5_SEQS)
            nr = lam_r * sr - lam_i * si + bur[rows, sl]
            ni = lam_r * si + lam_i * sr + bui[rows, sl]
            bur[rows, sl] = nr
            bui[rows, sl] = ni
            return nr, ni

        sr, si = lax.fori_loop(0, S5_TC, step, (sr_scr[:, sl], si_scr[:, sl]))
        sr_scr[:, sl] = sr
        si_scr[:, sl] = si
    for j in range(S5_Q):
        sl = slice(j * S5_QS, (j + 1) * S5_QS)
        yj = _dot3(bur[:, sl], cq_ref[0, j, :S5_QS, :]) + _dot3(bui[:, sl], cq_ref[0, j, S5_QS:, :])
        ytb[j] = yj
    for b in range(S5_SEQS):
        for j in range(S5_Q):
            y_ref[0, b, :, j * cw:(j + 1) * cw] = ytb[j, pl.ds(b, S5_TC, stride=S5_SEQS), :]

    @pl.when(c == pl.num_programs(2) - 1)
    def _():
        sfr_ref[0, 0] = sr_scr[...]
        sfi_ref[0, 0] = si_scr[...]


def _s5_scan(st, z3, bq, cq, lamr, lami, s0r, s0i, l):
    ng, nc = st.nb // S5_SEQS, st.L // S5_TC
    nrow = S5_TC * S5_SEQS
    chunk = lambda d, c: c + d * (nc - 1 - 2 * c)
    par = lambda g, d, c: (2 * l + d, 0, 0, 0)
    st_spec = pl.BlockSpec((1, 1, S5_SEQS, N_S5), lambda g, d, c: (g, d, 0, 0))
    return pl.pallas_call(
        _s5_scan_kernel,
        grid=(ng, 2, nc),
        in_specs=[pl.BlockSpec((S5_SEQS, S5_TC, D_GROUP), lambda g, d, c: (g, chunk(d, c), 0)),
                  pl.BlockSpec((1, S5_Q, D_GROUP // S5_Q, 2 * S5_QS), par),
                  pl.BlockSpec((1, S5_Q, 2 * S5_QS, D_GROUP // S5_Q), par),
                  pl.BlockSpec((1, 1, N_S5), lambda g, d, c: (2 * l + d, 0, 0)),
                  pl.BlockSpec((1, 1, N_S5), lambda g, d, c: (2 * l + d, 0, 0)),
                  st_spec, st_spec],
        out_specs=[pl.BlockSpec((1, S5_SEQS, S5_TC, D_GROUP), lambda g, d, c: (d, g, chunk(d, c), 0)),
                   st_spec, st_spec],
        out_shape=[jax.ShapeDtypeStruct((2, st.nb, st.L, D_GROUP), F32),
                   jax.ShapeDtypeStruct((ng, 2, S5_SEQS, N_S5), F32),
                   jax.ShapeDtypeStruct((ng, 2, S5_SEQS, N_S5), F32)],
        scratch_shapes=[pltpu.VMEM((S5_Q, nrow, LANE), F32), pltpu.VMEM((nrow, N_S5), F32),
                        pltpu.VMEM((nrow, N_S5), F32), pltpu.VMEM((S5_Q, nrow, LANE), F32),
                        pltpu.VMEM((S5_SEQS, N_S5), F32), pltpu.VMEM((S5_SEQS, N_S5), F32)],
        compiler_params=_cparams("parallel", "arbitrary", "arbitrary"),
        name="s5_scan",
    )(z3, bq, cq, lamr, lami, s0r, s0i)


S5_TM = 512


def _gelu_tanh(x):
    return 0.5 * x * (1.0 + jnp.tanh(math.sqrt(2.0 / math.pi) * (x + 0.044715 * (x * x * x))))


def _s5_out_kernel(u_ref, yf_ref, yb_ref, d_ref, w_ref, o_ref):
    y = yf_ref[0] + yb_ref[0] + d_ref[0] * u_ref[...]
    y = _gelu_tanh(y)
    o_ref[...] = (y * _sigmoid(_dot1(y, w_ref[0]))).astype(BF16)


def _s5_out(st, z, ydir, d_skip, w_glu, l):
    return pl.pallas_call(
        _s5_out_kernel,
        grid=(st.rows // S5_TM,),
        in_specs=[pl.BlockSpec((S5_TM, D_GROUP), lambda i: (i, 0)),
                  pl.BlockSpec((1, S5_TM, D_GROUP), lambda i: (0, i, 0)),
                  pl.BlockSpec((1, S5_TM, D_GROUP), lambda i: (1, i, 0)),
                  pl.BlockSpec((1, 1, D_GROUP), lambda i: (l, 0, 0)),
                  pl.BlockSpec((1, D_GROUP, D_GROUP), lambda i: (l, 0, 0))],
        out_specs=pl.BlockSpec((S5_TM, D_GROUP), lambda i: (i, 0)),
        out_shape=jax.ShapeDtypeStruct((st.rows, D_GROUP), BF16),
        compiler_params=_cparams("parallel"),
        name="s5_out",
    )(z, ydir, ydir, d_skip.reshape(DEPTH, 1, D_GROUP), w_glu)


NA_SCALE = NA_HEAD ** -0.5
NA_ROWS = DEC_SEQ // GRID_W
NA_KH = min(NA_WIN_H, NA_ROWS)
NA_QCOL = NA_OFF // LANE
NA_KCOL = (NA_OFF + D_GROUP) // LANE
NA_VCOL = (NA_OFF + 2 * D_GROUP) // LANE
NA_NDR = 2 * NA_WIN_H - 1


def _na_ctx_kernel(q_ref, k_ref, v_ref, o_ref, nk_ref, nv_ref):
    for hh in range(2):
        sl = slice(hh * NA_HEAD, (hh + 1) * NA_HEAD)
        q = q_ref[0, :, sl]
        k = k_ref[0, :, sl]
        v = v_ref[0, :, sl]
        nk_ref[0, hh] = k
        nv_ref[0, hh] = v
        s = _dot1(q, k, NT) * NA_SCALE
        e = jnp.exp(s - jnp.max(s, axis=-1, keepdims=True))
        o = _dot1(e, v) / jnp.sum(e, axis=-1, keepdims=True)
        o_ref[0, :, sl] = o.astype(BF16)


def _na_ctx(st, z3):
    blk = lambda col: pl.BlockSpec((1, st.L, LANE), lambda b, p: (b, 0, col + p))
    kv_spec = pl.BlockSpec((1, 2, st.L, NA_HEAD), lambda b, p: (b, p, 0, 0))
    kv_shape = jax.ShapeDtypeStruct((st.nb, NA_HEADS, st.L, NA_HEAD), F32)
    return pl.pallas_call(
        _na_ctx_kernel,
        grid=(st.nb, NA_HEADS // 2),
        in_specs=[blk(NA_QCOL), blk(NA_KCOL), blk(NA_VCOL)],
        out_specs=[pl.BlockSpec((1, st.L, LANE), lambda b, p: (b, 0, p)), kv_spec, kv_spec],
        out_shape=[jax.ShapeDtypeStruct((st.nb, st.L, D_GROUP), BF16), kv_shape, kv_shape],
        compiler_params=_cparams("parallel", "parallel"),
        name="na_context",
    )(z3, z3, z3)


def _na_nbr_kernel(q_ref, k_ref, v_ref, ck_ref, cv_ref, bias_ref, o_ref):
    nloc = NA_KH * GRID_W
    qc = _iota((GRID_W, nloc), 0)
    kc = _iota((GRID_W, nloc), 1) % GRID_W
    cs = jnp.clip(qc - NA_WIN_W // 2, 0, GRID_W - NA_WIN_W)
    col_in = (kc >= cs) & (kc < cs + NA_WIN_W)
    for hh in range(2):
        sl = slice(hh * NA_HEAD, (hh + 1) * NA_HEAD)
        kctx = ck_ref[0, 0, hh]
        vctx = cv_ref[0, 0, hh]
        for r in range(NA_ROWS):
            rs = min(max(r - NA_KH // 2, 0), NA_ROWS - NA_KH)
            q = q_ref[0, r * GRID_W:(r + 1) * GRID_W, sl]
            k = k_ref[0, rs * GRID_W:rs * GRID_W + nloc, sl]
            v = v_ref[0, rs * GRID_W:rs * GRID_W + nloc, sl]
            off = (rs - r + NA_WIN_H - 1) * GRID_W
            s_loc = _dot1(q, k, NT) * NA_SCALE + bias_ref[hh, :, off:off + nloc]
            s_loc = jnp.where(col_in, s_loc, NEG_INF)
            s_ctx = _dot1(q, kctx, NT) * NA_SCALE
            m = jnp.maximum(jnp.max(s_loc, axis=-1, keepdims=True), jnp.max(s_ctx, axis=-1, keepdims=True))
            e_loc = jnp.exp(s_loc - m)
            e_ctx = jnp.exp(s_ctx - m)
            den = jnp.sum(e_loc, axis=-1, keepdims=True) + jnp.sum(e_ctx, axis=-1, keepdims=True)
            o = (_dot1(e_loc, v) + _dot1(e_ctx, vctx)) / den
            o_ref[0, r * GRID_W:(r + 1) * GRID_W, sl] = o.astype(BF16)


def _na_bias_table(rpb):
    qc = np.arange(GRID_W)[:, None]
    kc = np.arange(GRID_W)[None, :]
    dc = np.clip(kc - qc, -(NA_WIN_W - 1), NA_WIN_W - 1) + (NA_WIN_W - 1)
    t = rpb[:, :, :, dc]
    return jnp.transpose(t, (0, 1, 3, 2, 4)).reshape(DEPTH, NA_HEADS, GRID_W, NA_NDR * GRID_W)


def _na_nbr(st, z3, cache_k, cache_v, bias_tab, l):
    blk = lambda col: pl.BlockSpec((1, st.L, LANE), lambda b, p: (b, 0, col + p))
    cspec = pl.BlockSpec((1, 1, 2, PAST_LEN, NA_HEAD), lambda b, p: (b, l, p, 0, 0))
    return pl.pallas_call(
        _na_nbr_kernel,
        grid=(st.nb, NA_HEADS // 2),
        in_specs=[blk(NA_QCOL), blk(NA_KCOL), blk(NA_VCOL), cspec, cspec,
                  pl.BlockSpec((None, 2, GRID_W, NA_NDR * GRID_W), lambda b, p: (l, p, 0, 0))],
        out_specs=pl.BlockSpec((1, st.L, LANE), lambda b, p: (b, 0, p)),
        out_shape=jax.ShapeDtypeStruct((st.nb, st.L, D_GROUP), BF16),
        compiler_params=_cparams("parallel", "parallel"),
        name="na_neighbourhood",
    )(z3, z3, z3, cache_k, cache_v, bias_tab)


ML_COL = ML_OFF // LANE
ML_SCALE = ML_HEAD ** -0.5


def _log_sigmoid(x):
    return -_softplus(-x)


def _rope_tables(L):
    half = ML_HEAD // 2
    quarter = half // 2
    t = np.arange(L)
    inv_freq = ROPE_BASE ** (-np.arange(quarter, dtype=np.float32) / quarter)

    def tabs(pos):
        ang = pos.astype(np.float32)[:, None] * inv_freq[None, :].astype(np.float32)
        c, s = np.cos(ang), np.sin(ang)
        return np.concatenate([c, c], axis=-1), np.concatenate([-s, s], axis=-1)

    c1, s1 = tabs(t // GRID_W)
    c2, s2 = tabs(t % GRID_W)
    return (jnp.asarray(np.concatenate([c1, c2], axis=-1), F32),
            jnp.asarray(np.concatenate([s1, s2], axis=-1), F32))


def _ml_kernel(ib_ref, fb_ref, q_ref, k_ref, v_ref, og_ref, g_ref, cos_ref, sin_ref, c0_ref, n0_ref, m0_ref,
               lng_ref, lnb_ref, y_ref, cf_ref, nf_ref, mf_ref, qs, ks, hf, hb, c_scr, n_scr, m_scr,
               *, L, rotary, layer):
    T = ML_CHUNK
    nc = L // T
    head = pl.program_id(1)

    q = q_ref[0]
    k = k_ref[0]
    if rotary:
        first = (_iota((L, ML_HEAD), 1) % (ML_HEAD // 2)) < ML_HEAD // 4

        def rope(x):
            quarter = ML_HEAD // 4
            partner = jnp.where(first, pltpu.roll(x, ML_HEAD - quarter, axis=1), pltpu.roll(x, quarter, axis=1))
            return x * cos_ref[...] + partner * sin_ref[...]

        q = rope(q)
        k = rope(k)
    qs[...] = q * ML_SCALE
    ks[...] = k
    for d in range(2):
        c_scr[d] = c0_ref[0, d, 0]
        n_scr[d] = n0_ref[0, 0, d:d + 1, :]
        m_scr[d] = m0_ref[0, 0, d:d + 1, :]

    tj = _iota((T, T), 0)
    ts = _iota((T, T), 1)
    ones = jnp.ones((T, T), F32)

    def chunk(ci, d):
        rows = pl.ds(pl.multiple_of(ci * T, T), T)
        qc = qs[rows, :]
        kc = ks[rows, :]
        vc = v_ref[0, rows, :]
        g = g_ref[0, 0, ci]
        ig = g[:, d:d + 1] + ib_ref[layer * 2 * ML_HEADS + d * ML_HEADS + head]
        fg = g[:, 2 + d:3 + d] + fb_ref[layer * 2 * ML_HEADS + d * ML_HEADS + head]
        lf = jnp.broadcast_to(_log_sigmoid(fg), (T, T))
        igb = jnp.broadcast_to(ig, (T, T))
        if d == 0:
            upto_row, upto_col, causal = tj <= ts, ts <= tj, ts <= tj
        else:
            upto_row, upto_col, causal = tj >= ts, ts >= tj, ts >= tj
        bcol = _dot_sel_l(jnp.where(upto_col, 1.0, 0.0), lf)
        brow_i = _dot_sel_l(ones, jnp.where(upto_row, lf, 0.0) - jnp.where(tj == ts, igb, 0.0))
        dmat = jnp.where(causal, bcol - brow_i, -jnp.inf)
        b1 = bcol[:, 0:1]
        m_prev = m_scr[d]
        inter = b1 + m_prev
        m_t = jnp.maximum(inter, jnp.max(dmat, axis=-1, keepdims=True))
        w_intra = jnp.exp(dmat - m_t)
        w_inter = jnp.exp(inter - m_t)
        qk = _dot1(qc, kc, NT) * w_intra
        c_old = c_scr[d]
        n_old = n_scr[d]
        num = w_inter * _dot1(qc, c_old) + _dot1(qk, vc)
        den = w_inter * jnp.sum(qc * n_old, axis=-1, keepdims=True) + jnp.sum(qk, axis=-1, keepdims=True)
        hcur = num / jnp.maximum(jnp.abs(den), jnp.exp(-m_t))
        if d == 0:
            hf[rows, :] = hcur
            b_last = b1[T - 1:T, :]
        else:
            hb[rows, :] = hcur
            b_last = b1[0:1, :]
        g_s = b_last - b1 + ig
        m_new = jnp.maximum(b_last + m_prev, jnp.max(g_s, axis=0, keepdims=True))
        w_old = jnp.exp(b_last + m_prev - m_new)
        w_s = jnp.exp(g_s - m_new)
        c_scr[d] = w_old * c_old + _dot1(kc, w_s * vc, TN)
        n_scr[d] = w_old * n_old + jnp.sum(w_s * kc, axis=0, keepdims=True)
        m_scr[d] = m_new

    def body(ci, carry):
        chunk(ci, 0)
        chunk(nc - 1 - ci, 1)
        return carry

    lax.fori_loop(0, nc, body, 0)

    hsum = hf[...] + hb[...]
    mu = jnp.mean(hsum, axis=-1, keepdims=True)
    dv = hsum - mu
    var = jnp.mean(dv * dv, axis=-1, keepdims=True)
    hn = dv * lax.rsqrt(var + ML_GN_EPS)
    y_ref[0] = (_sigmoid(og_ref[0]) * (hn * lng_ref[0] + lnb_ref[0])).astype(BF16)
    for d in range(2):
        cf_ref[0, d, 0] = c_scr[d]
        nf_ref[0, 0, d:d + 1, :] = n_scr[d]
        mf_ref[0, 0, d:d + 1, :] = m_scr[d]


def _ml_mixer(st, z3, zg, i_bias, f_bias, ln_g, ln_b, c0, n0, m0, l):
    L, nb = st.L, st.nb
    nc = L // ML_CHUNK
    gates = zg.reshape(nb, L, 2, 2, ML_HEADS).transpose(0, 4, 1, 2, 3).reshape(nb, ML_HEADS, nc, ML_CHUNK, 4)
    cos_t, sin_t = _rope_tables(L)
    blk = lambda col: pl.BlockSpec((1, L, LANE), lambda b, h: (b, 0, col + h))
    smem = pl.BlockSpec(memory_space=pltpu.SMEM)
    tab = pl.BlockSpec((L, ML_HEAD), lambda b, h: (0, 0))
    c_spec = pl.BlockSpec((1, 2, 1, ML_HEAD, ML_HEAD), lambda b, h: (b, 0, h, 0, 0))
    n_spec = pl.BlockSpec((1, 1, 2, ML_HEAD), lambda b, h: (b, h, 0, 0))
    m_spec = pl.BlockSpec((1, 1, 2, 1), lambda b, h: (b, h, 0, 0))
    par = pl.BlockSpec((1, 1, ML_HEAD), lambda b, h: (l * ML_HEADS + h, 0, 0))
    return pl.pallas_call(
        functools.partial(_ml_kernel, L=L, rotary=st.latent, layer=l),
        grid=(nb, ML_HEADS),
        in_specs=[smem, smem, blk(ML_COL), blk(ML_COL + ML_HEADS), blk(ML_COL + 2 * ML_HEADS),
                  blk(ML_COL + 3 * ML_HEADS),
                  pl.BlockSpec((1, 1, nc, ML_CHUNK, 4), lambda b, h: (b, h, 0, 0, 0)),
                  tab, tab, c_spec, n_spec, m_spec, par, par],
        out_specs=[pl.BlockSpec((1, L, LANE), lambda b, h: (b, 0, h)), c_spec, n_spec, m_spec],
        out_shape=[jax.ShapeDtypeStruct((nb, L, D_GROUP), BF16),
                   jax.ShapeDtypeStruct((nb, 2, ML_HEADS, ML_HEAD, ML_HEAD), F32),
                   jax.ShapeDtypeStruct((nb, ML_HEADS, 2, ML_HEAD), F32),
                   jax.ShapeDtypeStruct((nb, ML_HEADS, 2, 1), F32)],
        scratch_shapes=[pltpu.VMEM((L, ML_HEAD), F32), pltpu.VMEM((L, ML_HEAD), F32),
                        pltpu.VMEM((L, ML_HEAD), F32), pltpu.VMEM((L, ML_HEAD), F32),
                        pltpu.VMEM((2, ML_HEAD, ML_HEAD), F32), pltpu.VMEM((2, 1, ML_HEAD), F32),
                        pltpu.VMEM((2, 1, 1), F32)],
        compiler_params=_cparams("parallel", "parallel"),
        name="mlstm",
    )(i_bias.reshape(-1), f_bias.reshape(-1), z3, z3, z3, z3, gates, cos_t, sin_t, c0, n0, m0,
      ln_g.reshape(DEPTH * ML_HEADS, 1, ML_HEAD), ln_b.reshape(DEPTH * ML_HEADS, 1, ML_HEAD))


RW_TL = 256
RW_T = 64
RW_RCOL = RW_OFF // D_GROUP
RW_LCOL = (RW_OFF + 3 * D_GROUP) // (2 * LANE)
N_LORA = RW_LORA_W + RW_LORA_A + RW_LORA_G


def _head_ones():
    return jnp.asarray(np.kron(np.eye(RW_HEADS, dtype=np.float32), np.ones((RW_HEAD, RW_HEAD), np.float32)))


def _rw_front_kernel(r_ref, k_ref, v_ref, lo_ref, rp_ref, kp_ref, vp_ref, lp_ref, rn_ref, kn_ref, vn_ref, ln_ref,
                     mup_ref, mun_ref, w0_ref, w2_ref, a0_ref, a2_ref, g2_ref, kk_ref, ka_ref, hones_ref,
                     ro_ref, vo_ref, kko_ref, go_ref, kd_ref, bd_ref, lw_ref):
    t = pl.program_id(1)
    first = t == 0
    last = t == pl.num_programs(1) - 1
    row = _iota((RW_TL, 1), 0)

    def shift(x_ref, p_ref, n_ref, lo, hi):
        x = x_ref[0]
        prev_edge = jnp.where(first, 0.0, p_ref[0, SUB - 1:SUB, :])
        next_edge = jnp.where(last, 0.0, n_ref[0, 0:1, :])
        prev = jnp.where(row == 0, prev_edge, pltpu.roll(x, 1, axis=0))
        nxt = jnp.where(row == RW_TL - 1, next_edge, pltpu.roll(x, RW_TL - 1, axis=0))
        return x + mup_ref[0][:, lo:hi] * (prev - x) + mun_ref[0][:, lo:hi] * (nxt - x)

    g = D_GROUP
    r = shift(r_ref, rp_ref, rn_ref, 0, g)
    k = shift(k_ref, kp_ref, kn_ref, g, 2 * g)
    v = shift(v_ref, vp_ref, vn_ref, 2 * g, 3 * g)
    lo = shift(lo_ref, lp_ref, ln_ref, 3 * g, 3 * g + N_LORA)
    zw = lo[:, :RW_LORA_W]
    za = lo[:, RW_LORA_W:RW_LORA_W + RW_LORA_A]
    zg = lo[:, RW_LORA_W + RW_LORA_A:]
    ro_ref[0] = r
    vo_ref[0] = v
    kk = k * kk_ref[0]
    ssq = _dot_sel_r(kk * kk, hones_ref[...])
    kk = kk * lax.rsqrt(ssq + 1e-12)
    kko_ref[0] = kk
    go_ref[0] = _dot3(_sigmoid(zg), g2_ref[0])
    tw = jnp.tanh(zw)
    for d in range(2):
        w_log = -_softplus(-(w0_ref[0, d:d + 1, :] + _dot3(tw, w2_ref[0, d]))) - 0.5
        lw_ref[d, 0] = -jnp.exp(w_log)
        a = _sigmoid(a0_ref[0, d:d + 1, :] + _dot3(za, a2_ref[0, d]))
        kd_ref[d, 0] = k * (1.0 + (a - 1.0) * ka_ref[0])
        bd_ref[d, 0] = kk * a


def _rw_front(st, z3, mu_prev, mu_next, w0, w2, a0, a2, g2, k_k, k_a, l):
    nb, L = st.nb, st.L
    nt = L // RW_TL
    tpb = RW_TL // SUB
    main = lambda w, col: pl.BlockSpec((1, RW_TL, w), lambda b, t: (b, t, col))
    prev = lambda w, col: pl.BlockSpec((1, SUB, w), lambda b, t: (b, jnp.maximum(t * tpb - 1, 0), col))
    nxt = lambda w, col: pl.BlockSpec((1, SUB, w), lambda b, t: (b, jnp.minimum((t + 1) * tpb, L // SUB - 1), col))
    cols = [(D_GROUP, RW_RCOL), (D_GROUP, RW_RCOL + 1), (D_GROUP, RW_RCOL + 2), (2 * LANE, RW_LCOL)]
    lay = lambda *shape: pl.BlockSpec((1,) + shape, lambda b, t: (l,) + (0,) * len(shape))
    out1 = pl.BlockSpec((1, RW_TL, D_GROUP), lambda b, t: (b, t, 0))
    out2 = pl.BlockSpec((2, 1, RW_TL, D_GROUP), lambda b, t: (0, b, t, 0))
    s1 = jax.ShapeDtypeStruct((nb, L, D_GROUP), F32)
    s2 = jax.ShapeDtypeStruct((2, nb, L, D_GROUP), F32)
    return pl.pallas_call(
        _rw_front_kernel,
        grid=(nb, nt),
        in_specs=[main(*c) for c in cols] + [prev(*c) for c in cols] + [nxt(*c) for c in cols] + [
            lay(1, RW_IN), lay(1, RW_IN), lay(2, D_GROUP), lay(2, RW_LORA_W, D_GROUP), lay(2, D_GROUP),
            lay(2, RW_LORA_A, D_GROUP), lay(RW_LORA_G, D_GROUP), lay(1, D_GROUP), lay(1, D_GROUP),
            pl.BlockSpec((D_GROUP, D_GROUP), lambda b, t: (0, 0))],
        out_specs=[out1, out1, out1, out1, out2, out2, out2],
        out_shape=[s1, s1, s1, s1, s2, s2, s2],
        compiler_params=_cparams("parallel", "parallel"),
        name="rwkv_front",
    )(*([z3] * 12), mu_prev.reshape(DEPTH, 1, RW_IN), mu_next.reshape(DEPTH, 1, RW_IN), w0, w2, a0, a2, g2,
      k_k.reshape(DEPTH, 1, D_GROUP), k_a.reshape(DEPTH, 1, D_GROUP), _head_ones())


def _rw_core_kernel(rf_ref, vf_ref, kkf_ref, kdf_ref, bdf_ref, lwf_ref,
                    rb_ref, vb_ref, kkb_ref, kdb_ref, bdb_ref, lwb_ref, s0_ref,
                    yf_ref, yb_ref, sf_ref, s_scr):
    c = pl.program_id(1)
    T = RW_T

    @pl.when(c == 0)
    def _():
        s_scr[...] = s0_ref[0]

    tj = _iota((T, T), 0)
    ts = _iota((T, T), 1)
    dirs = ((rf_ref, vf_ref, kkf_ref, kdf_ref, bdf_ref, lwf_ref, yf_ref),
            (rb_ref, vb_ref, kkb_ref, kdb_ref, bdb_ref, lwb_ref, yb_ref))
    for d, (r_ref, v_ref, kk_ref, kd_ref, bd_ref, lw_ref, y_ref) in enumerate(dirs):
        if d == 0:
            incl = ts <= tj
            strict = ts < tj
            last = T - 1
        else:
            incl = ts >= tj
            strict = ts > tj
            last = 0
        lw = lw_ref[0, 0]
        cum = _dot_sel_l(jnp.where(incl, 1.0, 0.0), lw)
        w_in = jnp.exp(cum)
        w_inv = jnp.exp(-cum)
        w_ex = jnp.exp(cum - lw)
        kap_a = kk_ref[0] * w_ex
        bet_a = bd_ref[0, 0] * w_inv
        khat_a = kd_ref[0, 0] * w_inv
        rho_a = r_ref[0] * w_in
        w_tot = w_in[last:last + 1, :]
        v_a = v_ref[0]
        for h in range(RW_HEADS):
            sl = slice(h * RW_HEAD, (h + 1) * RW_HEAD)
            kap, bet, khat, rho, v = kap_a[:, sl], bet_a[:, sl], khat_a[:, sl], rho_a[:, sl], v_a[:, sl]
            kr = jnp.concatenate([kap, rho], axis=0)
            bk = jnp.concatenate([bet, khat], axis=0)
            gram = _dot3(kr, bk, NT)
            l_b = jnp.where(strict, gram[:T, :T], 0.0)
            l_k = jnp.where(strict, gram[:T, T:], 0.0)
            m_b = jnp.where(incl, gram[T:, :T], 0.0)
            m_k = jnp.where(incl, gram[T:, T:], 0.0)
            lmv = _dot3(jnp.concatenate([l_k, m_k], axis=0), v)
            x = jnp.concatenate([kap, lmv[:T]], axis=1)
            n = -l_b
            for lvl in range(6):
                if lvl < 5:
                    nx = _dot3(n, jnp.concatenate([n, x], axis=1))
                    n, x = nx[:, :T], x + nx[:, T:]
                else:
                    x = x + _dot3(n, x)
            p = _dot3(m_b, x)
            rho_p = rho - p[:, :RW_HEAD]
            y_v = lmv[T:] - p[:, RW_HEAD:]
            kap_p = x[:, :RW_HEAD]
            u_v = x[:, RW_HEAD:]
            a_m = _dot3(kap_p, bet, TN)
            d_m = _dot3(v, khat, TN) - _dot3(u_v, bet, TN)
            s_old = s_scr[d, h]
            y_ref[0, :, sl] = _dot3(rho_p, s_old, NT) + y_v
            s_scr[d, h] = (s_old - _dot3(s_old, a_m) + d_m) * w_tot[:, sl]

    @pl.when(c == pl.num_programs(1) - 1)
    def _():
        sf_ref[0] = s_scr[...]


def _rw_core(st, r, v, kk, kd, bd, lw, s0):
    nb, L = st.nb, st.L
    nc = L // RW_T
    f1 = pl.BlockSpec((1, RW_T, D_GROUP), lambda b, c: (b, c, 0))
    b1 = pl.BlockSpec((1, RW_T, D_GROUP), lambda b, c: (b, nc - 1 - c, 0))
    f2 = pl.BlockSpec((1, 1, RW_T, D_GROUP), lambda b, c: (0, b, c, 0))
    b2 = pl.BlockSpec((1, 1, RW_T, D_GROUP), lambda b, c: (1, b, nc - 1 - c, 0))
    s_spec = pl.BlockSpec((1, 2, RW_HEADS, RW_HEAD, RW_HEAD), lambda b, c: (b, 0, 0, 0, 0))
    ys = jax.ShapeDtypeStruct((nb, L, D_GROUP), F32)
    return pl.pallas_call(
        _rw_core_kernel,
        grid=(nb, nc),
        in_specs=[f1, f1, f1, f2, f2, f2, b1, b1, b1, b2, b2, b2, s_spec],
        out_specs=[f1, b1, s_spec],
        out_shape=[ys, ys, jax.ShapeDtypeStruct((nb, 2, RW_HEADS, RW_HEAD, RW_HEAD), F32)],
        scratch_shapes=[pltpu.VMEM((2, RW_HEADS, RW_HEAD, RW_HEAD), F32)],
        compiler_params=_cparams("parallel", "arbitrary"),
        name="rwkv_core",
    )(r, v, kk, kd, bd, lw, r, v, kk, kd, bd, lw, s0)


RWO_TM = 512


def _rw_out_kernel(yf_ref, yb_ref, r_ref, v_ref, kd0_ref, kd1_ref, g_ref, lng_ref, lnb_ref, rk_ref, hones_ref, o_ref):
    y = yf_ref[...] + yb_ref[...]
    inv = 1.0 / RW_HEAD
    mu = _dot_sel_r(y, hones_ref[...]) * inv
    dv = y - mu
    var = _dot_sel_r(dv * dv, hones_ref[...]) * inv
    yn = dv * lax.rsqrt(var + RW_GN_EPS) * lng_ref[0] + lnb_ref[0]
    kmean = 0.5 * (kd0_ref[0] + kd1_ref[0])
    bonus = _dot_sel_r(r_ref[...] * kmean * rk_ref[0], hones_ref[...]) * v_ref[...]
    o_ref[...] = ((yn + bonus) * g_ref[...]).astype(BF16)


def _rw_out(st, yf, yb, r, v, kd, g, ln_g, ln_b, r_k, l):
    rows = st.rows
    flat = lambda a: a.reshape(rows, D_GROUP)
    row = pl.BlockSpec((RWO_TM, D_GROUP), lambda i: (i, 0))
    lay = pl.BlockSpec((1, 1, D_GROUP), lambda i: (l, 0, 0))
    kd2 = kd.reshape(2, rows, D_GROUP)
    return pl.pallas_call(
        _rw_out_kernel,
        grid=(rows // RWO_TM,),
        in_specs=[row, row, row, row,
                  pl.BlockSpec((1, RWO_TM, D_GROUP), lambda i: (0, i, 0)),
                  pl.BlockSpec((1, RWO_TM, D_GROUP), lambda i: (1, i, 0)),
                  row, lay, lay, lay, pl.BlockSpec((D_GROUP, D_GROUP), lambda i: (0, 0))],
        out_specs=row,
        out_shape=jax.ShapeDtypeStruct((rows, D_GROUP), BF16),
        compiler_params=_cparams("parallel"),
        name="rwkv_out",
    )(flat(yf), flat(yb), flat(r), flat(v), kd2, kd2, flat(g), ln_g.reshape(DEPTH, 1, D_GROUP),
      ln_b.reshape(DEPTH, 1, D_GROUP), r_k.reshape(DEPTH, 1, D_GROUP), _head_ones())


def _trunk_layer(st, x, l, mod4, p, states, ctx_kv):
    nb, L = st.nb, st.L
    z, zg = _zproj(st, x, mod4, p['w_in'], p['w_gate_cols'], l)
    z3 = z.reshape(nb, L, D_Z)

    ydir, sfr, sfi = _s5_scan(st, z3, p['s5_bq'], p['s5_cq'], p['s5_lamr'], p['s5_lami'],
                              states['s5_re'], states['s5_im'], l)
    y_s5 = _s5_out(st, z, ydir.reshape(2, st.rows, D_GROUP), p['s5_d'], p['s5_w_glu'], l)

    r, v, kk, g, kd, bd, lw = _rw_front(st, z3, p['rw_mu_prev'], p['rw_mu_next'], p['rw_w0'], p['rw_w2'],
                                        p['rw_a0'], p['rw_a2'], p['rw_g2'], p['rw_k_k'], p['rw_k_a'], l)
    yf, yb, rw_s = _rw_core(st, r, v, kk, kd, bd, lw, states['rw'])
    y_rw = _rw_out(st, yf, yb, r, v, kd, g, p['rw_ln_g'], p['rw_ln_b'], p['rw_r_k'], l)

    if ctx_kv is None:
        y_na, nk, nv = _na_ctx(st, z3)
    else:
        y_na = _na_nbr(st, z3, ctx_kv[0], ctx_kv[1], p['na_bias'], l)
        nk = nv = None

    y_ml, ml_c, ml_n, ml_m = _ml_mixer(st, z3, zg, p['ml_i_bias'], p['ml_f_bias'], p['ml_ln_g'], p['ml_ln_b'],
                                       states['ml_c'], states['ml_n'], states['ml_m'], l)

    ys = (y_s5, y_rw, y_na.reshape(st.rows, D_GROUP), y_ml.reshape(st.rows, D_GROUP))
    x1, h2, gates = _oproj(st, ys, x, mod4, p['w_out'], p['ln1_g'], p['ln1_b'], p['w_router'], p['b_router'], l)
    f = _moe(st, h2, gates, p['moe_w_gate'], p['moe_w_up'], p['moe_w_down'], l)
    x2 = _ln2(st, x1, f, mod4, p['ln2_g'], p['ln2_b'], l)
    return x2, (nk, nv, sfr, sfi, rw_s, ml_c, ml_n, ml_m)


def kernel(x_prompt, x_sample, cache_nat_k, cache_nat_v, state_s5_re, state_s5_im, state_rwkv, state_mlstm_c, state_mlstm_n, state_mlstm_m, c, c_ctx, w_mod, b_mod, w_in, w_out, s5_lam_re, s5_lam_im, s5_log_step, s5_b_re, s5_b_im, s5_c_re, s5_c_im, s5_d, s5_w_glu, rw_mu_prev, rw_mu_next, rw_w0, rw_w2, rw_a0, rw_a2, rw_g2, rw_k_k, rw_k_a, rw_r_k, rw_ln_g, rw_ln_b, na_rpb, ml_i_bias, ml_f_bias, ml_ln_g, ml_ln_b, ln1_g, ln1_b, ln2_g, ln2_b, w_router, b_router, moe_w_gate, moe_w_up, moe_w_down):
    dt = x_prompt.dtype
    cond = jnp.concatenate([c_ctx[None, :], c, jnp.zeros((MOD_ROWS - 1 - DEC_BATCH, D_MODEL), F32)], axis=0)
    mod4 = _modulation(cond, w_mod, b_mod).reshape(DEPTH, MOD_ROWS, 1, 6 * D_MODEL)

    lbr, lbi, bbr, bbi = _s5_prep(s5_lam_re, s5_lam_im, s5_log_step, s5_b_re, s5_b_im)
    s5_bq, s5_cq, s5_lamr, s5_lami = _s5_block_params(lbr, lbi, bbr, bbi, s5_c_re, s5_c_im)
    p = dict(w_in=w_in, w_gate_cols=w_in[:, :, D_Z:], w_out=w_out,
             s5_bq=s5_bq, s5_cq=s5_cq, s5_lamr=s5_lamr, s5_lami=s5_lami, s5_d=s5_d, s5_w_glu=s5_w_glu,
             rw_mu_prev=rw_mu_prev, rw_mu_next=rw_mu_next, rw_w0=rw_w0, rw_w2=rw_w2, rw_a0=rw_a0, rw_a2=rw_a2,
             rw_g2=rw_g2, rw_k_k=rw_k_k, rw_k_a=rw_k_a, rw_r_k=rw_r_k, rw_ln_g=rw_ln_g, rw_ln_b=rw_ln_b,
             na_bias=_na_bias_table(na_rpb), ml_i_bias=ml_i_bias, ml_f_bias=ml_f_bias, ml_ln_g=ml_ln_g,
             ml_ln_b=ml_ln_b, ln1_g=ln1_g, ln1_b=ln1_b, ln2_g=ln2_g, ln2_b=ln2_b, w_router=w_router,
             b_router=b_router, moe_w_gate=moe_w_gate, moe_w_up=moe_w_up, moe_w_down=moe_w_down)

    gp = BATCH // S5_SEQS
    zero_states = dict(
        s5_re=jnp.zeros((gp, 2, S5_SEQS, N_S5), F32), s5_im=jnp.zeros((gp, 2, S5_SEQS, N_S5), F32),
        rw=jnp.zeros((BATCH, 2, RW_HEADS, RW_HEAD, RW_HEAD), F32),
        ml_c=jnp.zeros((BATCH, 2, ML_HEADS, ML_HEAD, ML_HEAD), F32),
        ml_n=jnp.zeros((BATCH, ML_HEADS, 2, ML_HEAD), F32), ml_m=jnp.zeros((BATCH, ML_HEADS, 2, 1), F32))

    xp = x_prompt.reshape(PROMPT.rows, D_MODEL)
    xs = x_sample.reshape(LATENT.rows, D_MODEL)
    outs = [[] for _ in range(8)]
    for l in range(DEPTH):
        xp, ctx_t = _trunk_layer(PROMPT, xp, l, mod4, p, zero_states, None)
        for acc, t in zip(outs, ctx_t):
            acc.append(t)
        lat_states = dict(
            s5_re=state_s5_re[:, l].reshape(DEC_BATCH, 2, N_S5).transpose(1, 0, 2)[None],
            s5_im=state_s5_im[:, l].reshape(DEC_BATCH, 2, N_S5).transpose(1, 0, 2)[None],
            rw=state_rwkv[:, l], ml_c=state_mlstm_c[:, l],
            ml_n=state_mlstm_n[:, l].transpose(0, 2, 1, 3), ml_m=state_mlstm_m[:, l].transpose(0, 2, 1)[..., None])
        xs, _ = _trunk_layer(LATENT, xs, l, mod4, p, lat_states, (cache_nat_k, cache_nat_v))

    nk, nv, s5r, s5i, rw, mc, mn, mm = [jnp.stack(t, axis=1) for t in outs]

    def s5_state(t):
        return t.transpose(0, 3, 1, 2, 4).reshape(BATCH, DEPTH, 2, S5_GROUPS, S5_STATE)

    return (xp.reshape(BATCH, SEQ, D_MODEL), xs.reshape(DEC_BATCH, DEC_SEQ, D_MODEL),
            nk, nv, s5_state(s5r).astype(dt), s5_state(s5i).astype(dt), rw.astype(dt), mc.astype(dt),
            mn.transpose(0, 1, 3, 2, 4).astype(dt), mm[..., 0].transpose(0, 1, 3, 2).astype(dt))
```

```python
import functools
import math

import numpy as np
import jax
import jax.numpy as jnp
from jax import lax
from jax.experimental import pallas as pl
from jax.experimental.pallas import tpu as pltpu

F32 = jnp.float32
BF16 = jnp.bfloat16

D_MODEL = 2048
BATCH = 16
SEQ = 256
DEPTH = 4
DEC_BATCH = 8
DEC_SEQ = 1024
PAST_LEN = 256
GRID_W = 64
D_GROUP = D_MODEL // 4
S5_CH = 16
S5_GROUPS = D_GROUP // S5_CH
S5_STATE = 64
RW_HEAD = 64
RW_HEADS = D_GROUP // RW_HEAD
RW_LORA_W = 64
RW_LORA_A = 64
RW_LORA_G = 128
RW_GN_EPS = 64e-5
NA_HEAD = 64
NA_HEADS = D_GROUP // NA_HEAD
NA_WIN_H = 8
NA_WIN_W = 16
NEG_INF = -1e30
ML_HEAD = 128
ML_HEADS = D_GROUP // ML_HEAD
ML_CHUNK = 64
ML_GN_EPS = 1e-5
ROPE_BASE = 10000.0
N_EXPERTS = 16
N_EXPERT_GROUPS = 4
EXPERTS_PER_GROUP = N_EXPERTS // N_EXPERT_GROUPS
D_EXPERT = 512
DEEPNORM_ALPHA = (2 * DEPTH) ** 0.25
LN_EPS = 1e-5
S5_IN = D_GROUP
RW_IN = 3 * D_GROUP + RW_LORA_W + RW_LORA_A + RW_LORA_G
NA_IN = 3 * D_GROUP
ML_IN = 4 * D_GROUP + 4 * ML_HEADS
D_IN = S5_IN + RW_IN + NA_IN + ML_IN
N_GATE = 4 * ML_HEADS
D_Z = D_IN - N_GATE
RW_OFF = S5_IN
NA_OFF = S5_IN + RW_IN
ML_OFF = NA_OFF + NA_IN
MOD_ROWS = 16
LANE = 128
SUB = 8

VMEM_LIMIT = 56 * 1024 * 1024


def _cparams(*sem):
    return pltpu.CompilerParams(dimension_semantics=sem, vmem_limit_bytes=VMEM_LIMIT)


def _dg(a, b, dims):
    return lax.dot_general(a, b, (dims, ((), ())), preferred_element_type=F32)


NN = ((1,), (0,))
NT = ((1,), (1,))
TN = ((0,), (0,))


def _dot1(a, b, dims=NN):
    return _dg(a.astype(BF16), b.astype(BF16), dims)


def _split(x):
    hi = x.astype(BF16)
    lo = (x - hi.astype(F32)).astype(BF16)
    return hi, lo


def _split3(x):
    hi = x.astype(BF16)
    r = x - hi.astype(F32)
    mid = r.astype(BF16)
    lo = (r - mid.astype(F32)).astype(BF16)
    return hi, mid, lo


def _dot3(a, b, dims=NN):
    ah, al = _split(a)
    bh, bl = _split(b)
    return _dg(ah, bh, dims) + (_dg(ah, bl, dims) + _dg(al, bh, dims))


def _dot_sel_l(sel, x, dims=NN):
    s = sel.astype(BF16)
    hi, mid, lo = _split3(x)
    return _dg(s, hi, dims) + (_dg(s, mid, dims) + _dg(s, lo, dims))


def _dot_sel_r(x, sel, dims=NN):
    s = sel.astype(BF16)
    hi, mid, lo = _split3(x)
    return _dg(hi, s, dims) + (_dg(mid, s, dims) + _dg(lo, s, dims))


def _sigmoid(x):
    return 1.0 / (1.0 + jnp.exp(-x))


def _silu(x):
    return x * _sigmoid(x)


def _softplus(x):
    return jnp.maximum(x, 0.0) + jnp.log(1.0 + jnp.exp(-jnp.abs(x)))


def _iota(shape, dim):
    return lax.broadcasted_iota(jnp.int32, shape, dim)


class _Stream:
    def __init__(self, nb, L, latent):
        self.nb, self.L, self.latent = nb, L, latent
        self.rows = nb * L

    def mod_row(self, tile, tile_rows):
        if not self.latent:
            return 0
        return 1 + (tile * tile_rows) // self.L


PROMPT = _Stream(BATCH, SEQ, False)
LATENT = _Stream(DEC_BATCH, DEC_SEQ, True)


MOD_TN = 768


def _mod_kernel(cond_ref, w_ref, b_ref, o_ref):
    c = _silu(cond_ref[...])
    o_ref[0] = _dot3(c, w_ref[0]) + b_ref[0]


def _modulation(cond, w_mod, b_mod):
    n = 6 * D_MODEL
    return pl.pallas_call(
        _mod_kernel,
        grid=(DEPTH, n // MOD_TN),
        in_specs=[pl.BlockSpec((MOD_ROWS, D_MODEL), lambda l, j: (0, 0)),
                  pl.BlockSpec((1, D_MODEL, MOD_TN), lambda l, j: (l, 0, j)),
                  pl.BlockSpec((1, 1, MOD_TN), lambda l, j: (l, 0, j))],
        out_specs=pl.BlockSpec((1, MOD_ROWS, MOD_TN), lambda l, j: (l, 0, j)),
        out_shape=jax.ShapeDtypeStruct((DEPTH, MOD_ROWS, n), F32),
        compiler_params=_cparams("parallel", "parallel"),
        name="modulation",
    )(cond, w_mod, b_mod.reshape(DEPTH, 1, n))


ZP_TM = 512
ZP_TN = 256


def _zproj_kernel(x_ref, mod_ref, w_ref, wg_ref, z_ref, zg_ref, h_scr):
    j = pl.program_id(1)

    @pl.when(j == 0)
    def _():
        m = mod_ref[0, 0]
        shift1 = m[:, 0:D_MODEL]
        scale1 = m[:, D_MODEL:2 * D_MODEL]
        h = x_ref[...] * (1.0 + scale1) + shift1
        h_scr[...] = h.astype(BF16)
        zg_ref[...] = _dot3(h, wg_ref[0])

    z_ref[...] = jnp.dot(h_scr[...], w_ref[0].astype(BF16), preferred_element_type=F32)


def _zproj(st, x, mod4, w_in, w_gate_cols, l):
    return pl.pallas_call(
        _zproj_kernel,
        grid=(st.rows // ZP_TM, D_Z // ZP_TN),
        in_specs=[pl.BlockSpec((ZP_TM, D_MODEL), lambda i, j: (i, 0)),
                  pl.BlockSpec((1, 1, 1, 6 * D_MODEL), lambda i, j: (l, st.mod_row(i, ZP_TM), 0, 0)),
                  pl.BlockSpec((1, D_MODEL, ZP_TN), lambda i, j: (l, 0, j)),
                  pl.BlockSpec((1, D_MODEL, N_GATE), lambda i, j: (l, 0, 0))],
        out_specs=[pl.BlockSpec((ZP_TM, ZP_TN), lambda i, j: (i, j)),
                   pl.BlockSpec((ZP_TM, N_GATE), lambda i, j: (i, 0))],
        out_shape=[jax.ShapeDtypeStruct((st.rows, D_Z), F32),
                   jax.ShapeDtypeStruct((st.rows, N_GATE), F32)],
        scratch_shapes=[pltpu.VMEM((ZP_TM, D_MODEL), BF16)],
        compiler_params=_cparams("parallel", "arbitrary"),
        name="zproj",
    )(x, mod4, w_in, w_gate_cols)


OP_TM = 512
OP_TN = 512


def _layer_norm(v, g, b):
    mu = jnp.mean(v, axis=-1, keepdims=True)
    d = v - mu
    var = jnp.mean(d * d, axis=-1, keepdims=True)
    return d * lax.rsqrt(var + LN_EPS) * g + b


def _route(scores, b_router):
    sel = scores + b_router
    s = [sel[:, e:e + 1] for e in range(N_EXPERTS)]
    in_top2 = []
    for g in range(N_EXPERT_GROUPS):
        for i in range(EXPERTS_PER_GROUP):
            e = g * EXPERTS_PER_GROUP + i
            cnt = jnp.zeros_like(s[e])
            for jj in range(EXPERTS_PER_GROUP):
                if jj == i:
                    continue
                o = g * EXPERTS_PER_GROUP + jj
                beats = (s[o] > s[e]) if jj > i else (s[o] >= s[e])
                cnt = cnt + jnp.where(beats, 1.0, 0.0)
            in_top2.append(cnt < 2.0)
    grp = []
    for g in range(N_EXPERT_GROUPS):
        tot = jnp.zeros_like(s[0])
        for i in range(EXPERTS_PER_GROUP):
            e = g * EXPERTS_PER_GROUP + i
            tot = tot + jnp.where(in_top2[e], s[e], 0.0)
        grp.append(tot)
    lane = _iota(scores.shape, 1)
    keep = jnp.zeros(scores.shape, F32)
    for g in range(N_EXPERT_GROUPS):
        cnt = jnp.zeros_like(s[0])
        for o in range(N_EXPERT_GROUPS):
            if o == g:
                continue
            beats = (grp[o] > grp[g]) if o > g else (grp[o] >= grp[g])
            cnt = cnt + jnp.where(beats, 1.0, 0.0)
        best = cnt < 1.0
        for i in range(EXPERTS_PER_GROUP):
            e = g * EXPERTS_PER_GROUP + i
            on = jnp.where(best, jnp.where(in_top2[e], 1.0, 0.0), 0.0)
            keep = keep + jnp.where(lane == e, on, 0.0)
    picked = scores * keep
    return picked / jnp.sum(picked, axis=-1, keepdims=True)


def _oproj_kernel(y0_ref, y1_ref, y2_ref, y3_ref, w_ref, x_ref, mod_ref, g_ref, b_ref, wr_ref, br_ref,
                  x1_ref, h2_ref, gate_ref, acc_scr):
    j = pl.program_id(1)
    acc = jnp.zeros((OP_TM, OP_TN), F32)
    for k, yr in enumerate((y0_ref, y1_ref, y2_ref, y3_ref)):
        wk = w_ref[0, k * D_GROUP:(k + 1) * D_GROUP, :].astype(BF16)
        acc = acc + jnp.dot(yr[...], wk, preferred_element_type=F32)
    for jj in range(D_MODEL // OP_TN):
        @pl.when(j == jj)
        def _(jj=jj):
            acc_scr[:, jj * OP_TN:(jj + 1) * OP_TN] = acc

    @pl.when(j == D_MODEL // OP_TN - 1)
    def _():
        m = mod_ref[0, 0]
        gate1 = m[:, 2 * D_MODEL:3 * D_MODEL]
        shift2 = m[:, 3 * D_MODEL:4 * D_MODEL]
        scale2 = m[:, 4 * D_MODEL:5 * D_MODEL]
        x1 = _layer_norm(DEEPNORM_ALPHA * x_ref[...] + gate1 * acc_scr[...], g_ref[0], b_ref[0])
        x1_ref[...] = x1
        h2 = x1 * (1.0 + scale2) + shift2
        h2_ref[...] = h2.astype(BF16)
        scores = _sigmoid(_dot3(h2, wr_ref[...]))
        gate_ref[...] = _route(scores, br_ref[...])


def _oproj(st, ys, x, mod4, w_out, ln_g, ln_b, w_router, b_router, l):
    row = lambda i, j: (i, 0)
    return pl.pallas_call(
        _oproj_kernel,
        grid=(st.rows // OP_TM, D_MODEL // OP_TN),
        in_specs=[pl.BlockSpec((OP_TM, D_GROUP), row)] * 4 + [
            pl.BlockSpec((1, D_MODEL, OP_TN), lambda i, j: (l, 0, j)),
            pl.BlockSpec((OP_TM, D_MODEL), row),
            pl.BlockSpec((1, 1, 1, 6 * D_MODEL), lambda i, j: (l, st.mod_row(i, OP_TM), 0, 0)),
            pl.BlockSpec((1, 1, D_MODEL), lambda i, j: (l, 0, 0)),
            pl.BlockSpec((1, 1, D_MODEL), lambda i, j: (l, 0, 0)),
            pl.BlockSpec((D_MODEL, N_EXPERTS), lambda i, j: (0, 0)),
            pl.BlockSpec((1, N_EXPERTS), lambda i, j: (0, 0))],
        out_specs=[pl.BlockSpec((OP_TM, D_MODEL), row),
                   pl.BlockSpec((OP_TM, D_MODEL), row),
                   pl.BlockSpec((OP_TM, N_EXPERTS), row)],
        out_shape=[jax.ShapeDtypeStruct((st.rows, D_MODEL), F32),
                   jax.ShapeDtypeStruct((st.rows, D_MODEL), BF16),
                   jax.ShapeDtypeStruct((st.rows, N_EXPERTS), F32)],
        scratch_shapes=[pltpu.VMEM((OP_TM, D_MODEL), F32)],
        compiler_params=_cparams("parallel", "arbitrary"),
        name="oproj_ln1_router",
    )(*ys, w_out, x, mod4, ln_g.reshape(DEPTH, 1, D_MODEL), ln_b.reshape(DEPTH, 1, D_MODEL),
      w_router, b_router.reshape(1, N_EXPERTS))


MOE_TM = 512


def _moe_kernel(h_ref, gate_ref, wg_ref, wu_ref, wd_ref, f_ref):
    e = pl.program_id(1)
    h = h_ref[...]
    a = jnp.dot(h, wg_ref[0, 0].astype(BF16), preferred_element_type=F32)
    u = jnp.dot(h, wu_ref[0, 0].astype(BF16), preferred_element_type=F32)
    gates = gate_ref[...]
    ge = jnp.sum(jnp.where(_iota(gates.shape, 1) == e, gates, 0.0), axis=-1, keepdims=True)
    hid = (_silu(a) * u * ge).astype(BF16)
    contrib = jnp.dot(hid, wd_ref[0, 0].astype(BF16), preferred_element_type=F32)

    @pl.when(e == 0)
    def _():
        f_ref[...] = contrib

    @pl.when(e != 0)
    def _():
        f_ref[...] += contrib


def _moe(st, h2, gates, w_gate, w_up, w_down, l):
    return pl.pallas_call(
        _moe_kernel,
        grid=(st.rows // MOE_TM, N_EXPERTS),
        in_specs=[pl.BlockSpec((MOE_TM, D_MODEL), lambda i, e: (i, 0)),
                  pl.BlockSpec((MOE_TM, N_EXPERTS), lambda i, e: (i, 0)),
                  pl.BlockSpec((1, 1, D_MODEL, D_EXPERT), lambda i, e: (l, e, 0, 0)),
                  pl.BlockSpec((1, 1, D_MODEL, D_EXPERT), lambda i, e: (l, e, 0, 0)),
                  pl.BlockSpec((1, 1, D_EXPERT, D_MODEL), lambda i, e: (l, e, 0, 0))],
        out_specs=pl.BlockSpec((MOE_TM, D_MODEL), lambda i, e: (i, 0)),
        out_shape=jax.ShapeDtypeStruct((st.rows, D_MODEL), F32),
        compiler_params=_cparams("parallel", "arbitrary"),
        name="moe_dense",
    )(h2, gates, w_gate, w_up, w_down)


LN_TM = 512


def _ln2_kernel(x_ref, f_ref, mod_ref, g_ref, b_ref, o_ref):
    gate2 = mod_ref[0, 0][:, 5 * D_MODEL:6 * D_MODEL]
    o_ref[...] = _layer_norm(DEEPNORM_ALPHA * x_ref[...] + gate2 * f_ref[...], g_ref[0], b_ref[0])


def _ln2(st, x1, f, mod4, ln_g, ln_b, l):
    row = lambda i: (i, 0)
    return pl.pallas_call(
        _ln2_kernel,
        grid=(st.rows // LN_TM,),
        in_specs=[pl.BlockSpec((LN_TM, D_MODEL), row),
                  pl.BlockSpec((LN_TM, D_MODEL), row),
                  pl.BlockSpec((1, 1, 1, 6 * D_MODEL), lambda i: (l, st.mod_row(i, LN_TM), 0, 0)),
                  pl.BlockSpec((1, 1, D_MODEL), lambda i: (l, 0, 0)),
                  pl.BlockSpec((1, 1, D_MODEL), lambda i: (l, 0, 0))],
        out_specs=pl.BlockSpec((LN_TM, D_MODEL), row),
        out_shape=jax.ShapeDtypeStruct((st.rows, D_MODEL), F32),
        compiler_params=_cparams("parallel"),
        name="ln2",
    )(x1, f, mod4, ln_g.reshape(DEPTH, 1, D_MODEL), ln_b.reshape(DEPTH, 1, D_MODEL))


S5_TC = 64
S5_SEQS = SUB
S5_Q = 4
S5_QS = S5_GROUPS // S5_Q * S5_STATE
N_S5 = S5_GROUPS * S5_STATE


def _s5_prep_kernel(lr_ref, li_ref, ls_ref, br_ref, bi_ref, e_ref, lbr_ref, lbi_ref, bbr_ref, bbi_ref):
    lr = lr_ref[0]
    li = li_ref[0]
    dt = jnp.exp(ls_ref[0])
    mag = jnp.exp(lr * dt)
    ang = li * dt
    ar = mag * jnp.cos(ang)
    ai = mag * jnp.sin(ang)
    lbr_ref[0] = ar
    lbi_ref[0] = ai
    den = lr * lr + li * li
    nr = ar - 1.0
    cr = (nr * lr + ai * li) / den
    ci = (ai * lr - nr * li) / den
    cr = _dot_sel_l(e_ref[...], cr)
    ci = _dot_sel_l(e_ref[...], ci)
    bre = br_ref[0]
    bim = bi_ref[0]
    bbr_ref[0] = cr * bre - ci * bim
    bbi_ref[0] = cr * bim + ci * bre


def _s5_prep(lam_re, lam_im, log_step, b_re, b_im):
    d2 = DEPTH * 2
    g, p, h = S5_GROUPS, S5_STATE, S5_CH
    bt_re = jnp.swapaxes(b_re, -1, -2).reshape(d2, g * h, p)
    bt_im = jnp.swapaxes(b_im, -1, -2).reshape(d2, g * h, p)
    expand = jnp.asarray(np.kron(np.eye(g, dtype=np.float32), np.ones((h, 1), np.float32)))
    spec_gp = pl.BlockSpec((1, g, p), lambda i: (i, 0, 0))
    spec_b = pl.BlockSpec((1, g * h, p), lambda i: (i, 0, 0))
    lbr, lbi, bbr, bbi = pl.pallas_call(
        _s5_prep_kernel,
        grid=(d2,),
        in_specs=[spec_gp, spec_gp, pl.BlockSpec((1, g, 1), lambda i: (i, 0, 0)), spec_b, spec_b,
                  pl.BlockSpec((g * h, g), lambda i: (0, 0))],
        out_specs=[spec_gp, spec_gp, spec_b, spec_b],
        out_shape=[jax.ShapeDtypeStruct((d2, g, p), F32)] * 2 + [jax.ShapeDtypeStruct((d2, g * h, p), F32)] * 2,
        compiler_params=_cparams("parallel"),
        name="s5_prep",
    )(lam_re.reshape(d2, g, p), lam_im.reshape(d2, g, p), log_step.reshape(d2, g, 1), bt_re, bt_im, expand)
    return lbr, lbi, bbr, bbi


def _s5_block_params(lbr, lbi, bbr, bbi, c_re, c_im):
    d2 = DEPTH * 2
    gq = S5_GROUPS // S5_Q
    eye = jnp.eye(gq, dtype=F32)

    def b_blocks(b):
        b = b.reshape(d2, S5_Q, gq, S5_CH, S5_STATE)
        return jnp.einsum('djghp,gk->djghkp', b, eye).reshape(d2, S5_Q, gq * S5_CH, gq * S5_STATE)

    def c_blocks(c):
        c = c.reshape(d2, S5_Q, gq, S5_CH, S5_STATE)
        return jnp.einsum('djghp,gk->djgpkh', c, eye).reshape(d2, S5_Q, gq * S5_STATE, gq * S5_CH)

    bq = jnp.concatenate([b_blocks(bbr), b_blocks(bbi)], axis=-1)
    cq = jnp.concatenate([c_blocks(c_re), -c_blocks(c_im)], axis=-2)
    return bq, cq, lbr.reshape(d2, 1, N_S5), lbi.reshape(d2, 1, N_S5)


def _s5_scan_kernel(u_ref, bq_ref, cq_ref, lr_ref, li_ref, s0r_ref, s0i_ref, y_ref, sfr_ref, sfi_ref,
                    utb, bur, bui, ytb, sr_scr, si_scr):
    d = pl.program_id(1)
    c = pl.program_id(2)
    nrow = S5_TC * S5_SEQS
    cw = D_GROUP // S5_Q

    @pl.when(c == 0)
    def _():
        sr_scr[...] = s0r_ref[0, 0]
        si_scr[...] = s0i_ref[0, 0]

    for b in range(S5_SEQS):
        for j in range(S5_Q):
            utb[j, pl.ds(b, S5_TC, stride=S5_SEQS), :] = u_ref[b, :, j * cw:(j + 1) * cw]
    for j in range(S5_Q):
        bu = _dot3(utb[j], bq_ref[0, j])
        bur[:, j * S5_QS:(j + 1) * S5_QS] = bu[:, :S5_QS]
        bui[:, j * S5_QS:(j + 1) * S5_QS] = bu[:, S5_QS:]
    for j in range(S5_Q):
        sl = slice(j * S5_QS, (j + 1) * S5_QS)
        lam_r = jnp.broadcast_to(lr_ref[0][:, sl], (S5_SEQS, S5_QS))
        lam_i = jnp.broadcast_to(li_ref[0][:, sl], (S5_SEQS, S5_QS))

        def step(t, carry, sl=sl, lam_r=lam_r, lam_i=lam_i):
            sr, si = carry
            te = jnp.where(d == 0, t, S5_TC - 1 - t)
            rows = pl.ds(pl.multiple_of(te * S5_SEQS, S5_SEQS), S5_SEQS)
            nr = lam_r * sr - lam_i * si + bur[rows, sl]
            ni = lam_r * si + lam_i * sr + bui[rows, sl]
            bur[rows, sl] = nr
            bui[rows, sl] = ni
            return nr, ni

        sr, si = lax.fori_loop(0, S5_TC, step, (sr_scr[:, sl], si_scr[:, sl]))
        sr_scr[:, sl] = sr
        si_scr[:, sl] = si
    for j in range(S5_Q):
        sl = slice(j * S5_QS, (j + 1) * S5_QS)
        yj = _dot3(bur[:, sl], cq_ref[0, j, :S5_QS, :]) + _dot3(bui[:, sl], cq_ref[0, j, S5_QS:, :])
        ytb[j] = yj
    for b in range(S5_SEQS):
        for j in range(S5_Q):
            y_ref[0, b, :, j * cw:(j + 1) * cw] = ytb[j, pl.ds(b, S5_TC, stride=S5_SEQS), :]

    @pl.when(c == pl.num_programs(2) - 1)
    def _():
        sfr_ref[0, 0] = sr_scr[...]
        sfi_ref[0, 0] = si_scr[...]


def _s5_scan(st, z3, bq, cq, lamr, lami, s0r, s0i, l):
    ng, nc = st.nb // S5_SEQS, st.L // S5_TC
    nrow = S5_TC * S5_SEQS
    chunk = lambda d, c: c + d * (nc - 1 - 2 * c)
    par = lambda g, d, c: (2 * l + d, 0, 0, 0)
    st_spec = pl.BlockSpec((1, 1, S5_SEQS, N_S5), lambda g, d, c: (g, d, 0, 0))
    return pl.pallas_call(
        _s5_scan_kernel,
        grid=(ng, 2, nc),
        in_specs=[pl.BlockSpec((S5_SEQS, S5_TC, D_GROUP), lambda g, d, c: (g, chunk(d, c), 0)),
                  pl.BlockSpec((1, S5_Q, D_GROUP // S5_Q, 2 * S5_QS), par),
                  pl.BlockSpec((1, S5_Q, 2 * S5_QS, D_GROUP // S5_Q), par),
                  pl.BlockSpec((1, 1, N_S5), lambda g, d, c: (2 * l + d, 0, 0)),
                  pl.BlockSpec((1, 1, N_S5), lambda g, d, c: (2 * l + d, 0, 0)),
                  st_spec, st_spec],
        out_specs=[pl.BlockSpec((1, S5_SEQS, S5_TC, D_GROUP), lambda g, d, c: (d, g, chunk(d, c), 0)),
                   st_spec, st_spec],
        out_shape=[jax.ShapeDtypeStruct((2, st.nb, st.L, D_GROUP), F32),
                   jax.ShapeDtypeStruct((ng, 2, S5_SEQS, N_S5), F32),
                   jax.ShapeDtypeStruct((ng, 2, S5_SEQS, N_S5), F32)],
        scratch_shapes=[pltpu.VMEM((S5_Q, nrow, LANE), F32), pltpu.VMEM((nrow, N_S5), F32),
                        pltpu.VMEM((nrow, N_S5), F32), pltpu.VMEM((S5_Q, nrow, LANE), F32),
                        pltpu.VMEM((S5_SEQS, N_S5), F32), pltpu.VMEM((S5_SEQS, N_S5), F32)],
        compiler_params=_cparams("parallel", "arbitrary", "arbitrary"),
        name="s5_scan",
    )(z3, bq, cq, lamr, lami, s0r, s0i)


S5_TM = 512


def _gelu_tanh(x):
    return 0.5 * x * (1.0 + jnp.tanh(math.sqrt(2.0 / math.pi) * (x + 0.044715 * (x * x * x))))


def _s5_out_kernel(u_ref, yf_ref, yb_ref, d_ref, w_ref, o_ref):
    y = yf_ref[0] + yb_ref[0] + d_ref[0] * u_ref[...]
    y = _gelu_tanh(y)
    o_ref[...] = (y * _sigmoid(_dot1(y, w_ref[0]))).astype(BF16)


def _s5_out(st, z, ydir, d_skip, w_glu, l):
    return pl.pallas_call(
        _s5_out_kernel,
        grid=(st.rows // S5_TM,),
        in_specs=[pl.BlockSpec((S5_TM, D_GROUP), lambda i: (i, 0)),
                  pl.BlockSpec((1, S5_TM, D_GROUP), lambda i: (0, i, 0)),
                  pl.BlockSpec((1, S5_TM, D_GROUP), lambda i: (1, i, 0)),
                  pl.BlockSpec((1, 1, D_GROUP), lambda i: (l, 0, 0)),
                  pl.BlockSpec((1, D_GROUP, D_GROUP), lambda i: (l, 0, 0))],
        out_specs=pl.BlockSpec((S5_TM, D_GROUP), lambda i: (i, 0)),
        out_shape=jax.ShapeDtypeStruct((st.rows, D_GROUP), BF16),
        compiler_params=_cparams("parallel"),
        name="s5_out",
    )(z, ydir, ydir, d_skip.reshape(DEPTH, 1, D_GROUP), w_glu)


NA_SCALE = NA_HEAD ** -0.5
NA_ROWS = DEC_SEQ // GRID_W
NA_KH = min(NA_WIN_H, NA_ROWS)
NA_QCOL = NA_OFF // LANE
NA_KCOL = (NA_OFF + D_GROUP) // LANE
NA_VCOL = (NA_OFF + 2 * D_GROUP) // LANE
NA_NDR = 2 * NA_WIN_H - 1


def _na_ctx_kernel(q_ref, k_ref, v_ref, o_ref, nk_ref, nv_ref):
    for hh in range(2):
        sl = slice(hh * NA_HEAD, (hh + 1) * NA_HEAD)
        q = q_ref[0, :, sl]
        k = k_ref[0, :, sl]
        v = v_ref[0, :, sl]
        nk_ref[0, hh] = k
        nv_ref[0, hh] = v
        s = _dot1(q, k, NT) * NA_SCALE
        e = jnp.exp(s - jnp.max(s, axis=-1, keepdims=True))
        o = _dot1(e, v) / jnp.sum(e, axis=-1, keepdims=True)
        o_ref[0, :, sl] = o.astype(BF16)


def _na_ctx(st, z3):
    blk = lambda col: pl.BlockSpec((1, st.L, LANE), lambda b, p: (b, 0, col + p))
    kv_spec = pl.BlockSpec((1, 2, st.L, NA_HEAD), lambda b, p: (b, p, 0, 0))
    kv_shape = jax.ShapeDtypeStruct((st.nb, NA_HEADS, st.L, NA_HEAD), F32)
    return pl.pallas_call(
        _na_ctx_kernel,
        grid=(st.nb, NA_HEADS // 2),
        in_specs=[blk(NA_QCOL), blk(NA_KCOL), blk(NA_VCOL)],
        out_specs=[pl.BlockSpec((1, st.L, LANE), lambda b, p: (b, 0, p)), kv_spec, kv_spec],
        out_shape=[jax.ShapeDtypeStruct((st.nb, st.L, D_GROUP), BF16), kv_shape, kv_shape],
        compiler_params=_cparams("parallel", "parallel"),
        name="na_context",
    )(z3, z3, z3)


def _na_nbr_kernel(q_ref, k_ref, v_ref, ck_ref, cv_ref, bias_ref, o_ref):
    nloc = NA_KH * GRID_W
    qc = _iota((GRID_W, nloc), 0)
    kc = _iota((GRID_W, nloc), 1) % GRID_W
    cs = jnp.clip(qc - NA_WIN_W // 2, 0, GRID_W - NA_WIN_W)
    col_in = (kc >= cs) & (kc < cs + NA_WIN_W)
    for hh in range(2):
        sl = slice(hh * NA_HEAD, (hh + 1) * NA_HEAD)
        kctx = ck_ref[0, 0, hh]
        vctx = cv_ref[0, 0, hh]
        for r in range(NA_ROWS):
            rs = min(max(r - NA_KH // 2, 0), NA_ROWS - NA_KH)
            q = q_ref[0, r * GRID_W:(r + 1) * GRID_W, sl]
            k = k_ref[0, rs * GRID_W:rs * GRID_W + nloc, sl]
            v = v_ref[0, rs * GRID_W:rs * GRID_W + nloc, sl]
            off = (rs - r + NA_WIN_H - 1) * GRID_W
            s_loc = _dot1(q, k, NT) * NA_SCALE + bias_ref[hh, :, off:off + nloc]
            s_loc = jnp.where(col_in, s_loc, NEG_INF)
            s_ctx = _dot1(q, kctx, NT) * NA_SCALE
            m = jnp.maximum(jnp.max(s_loc, axis=-1, keepdims=True), jnp.max(s_ctx, axis=-1, keepdims=True))
            e_loc = jnp.exp(s_loc - m)
            e_ctx = jnp.exp(s_ctx - m)
            den = jnp.sum(e_loc, axis=-1, keepdims=True) + jnp.sum(e_ctx, axis=-1, keepdims=True)
            o = (_dot1(e_loc, v) + _dot1(e_ctx, vctx)) / den
            o_ref[0, r * GRID_W:(r + 1) * GRID_W, sl] = o.astype(BF16)


def _na_bias_table(rpb):
    qc = np.arange(GRID_W)[:, None]
    kc = np.arange(GRID_W)[None, :]
    dc = np.clip(kc - qc, -(NA_WIN_W - 1), NA_WIN_W - 1) + (NA_WIN_W - 1)
    t = rpb[:, :, :, dc]
    return jnp.transpose(t, (0, 1, 3, 2, 4)).reshape(DEPTH, NA_HEADS, GRID_W, NA_NDR * GRID_W)


def _na_nbr(st, z3, cache_k, cache_v, bias_tab, l):
    blk = lambda col: pl.BlockSpec((1, st.L, LANE), lambda b, p: (b, 0, col + p))
    cspec = pl.BlockSpec((1, 1, 2, PAST_LEN, NA_HEAD), lambda b, p: (b, l, p, 0, 0))
    return pl.pallas_call(
        _na_nbr_kernel,
        grid=(st.nb, NA_HEADS // 2),
        in_specs=[blk(NA_QCOL), blk(NA_KCOL), blk(NA_VCOL), cspec, cspec,
                  pl.BlockSpec((None, 2, GRID_W, NA_NDR * GRID_W), lambda b, p: (l, p, 0, 0))],
        out_specs=pl.BlockSpec((1, st.L, LANE), lambda b, p: (b, 0, p)),
        out_shape=jax.ShapeDtypeStruct((st.nb, st.L, D_GROUP), BF16),
        compiler_params=_cparams("parallel", "parallel"),
        name="na_neighbourhood",
    )(z3, z3, z3, cache_k, cache_v, bias_tab)


ML_COL = ML_OFF // LANE
ML_SCALE = ML_HEAD ** -0.5


def _log_sigmoid(x):
    return -_softplus(-x)


def _rope_tables(L):
    half = ML_HEAD // 2
    quarter = half // 2
    t = np.arange(L)
    inv_freq = ROPE_BASE ** (-np.arange(quarter, dtype=np.float32) / quarter)

    def tabs(pos):
        ang = pos.astype(np.float32)[:, None] * inv_freq[None, :].astype(np.float32)
        c, s = np.cos(ang), np.sin(ang)
        return np.concatenate([c, c], axis=-1), np.concatenate([-s, s], axis=-1)

    c1, s1 = tabs(t // GRID_W)
    c2, s2 = tabs(t % GRID_W)
    return (jnp.asarray(np.concatenate([c1, c2], axis=-1), F32),
            jnp.asarray(np.concatenate([s1, s2], axis=-1), F32))


def _ml_kernel(*refs, L, rotary, layer):
    nh = ML_HEADS
    ib_ref, fb_ref = refs[0], refs[1]
    q_refs, k_refs, v_refs, o_refs = (refs[2 + i * nh:2 + (i + 1) * nh] for i in range(4))
    (g_ref, cos_ref, sin_ref, c0_ref, n0_ref, m0_ref, lng_ref, lnb_ref,
     y_ref, cf_ref, nf_ref, mf_ref, qs, ks, hf, hb, c_scr, n_scr, m_scr) = refs[2 + 4 * nh:]
    T = ML_CHUNK
    nc = L // T

    if rotary:
        first = (_iota((L, ML_HEAD), 1) % (ML_HEAD // 2)) < ML_HEAD // 4

        def rope(x):
            quarter = ML_HEAD // 4
            partner = jnp.where(first, pltpu.roll(x, ML_HEAD - quarter, axis=1), pltpu.roll(x, quarter, axis=1))
            return x * cos_ref[...] + partner * sin_ref[...]
    else:
        rope = lambda x: x
    for h in range(nh):
        qs[h] = rope(q_refs[h][0]) * ML_SCALE
        ks[h] = rope(k_refs[h][0])
        for d in range(2):
            c_scr[d, h] = c0_ref[0, d, h]
            n_scr[d, h] = n0_ref[0, h, d:d + 1, :]
            m_scr[d, h] = m0_ref[0, h, d:d + 1, :]

    tj = _iota((T, T), 0)
    ts = _iota((T, T), 1)
    ones = jnp.ones((T, T), F32)
    upto_row = (tj <= ts, tj >= ts)
    upto_col = (ts <= tj, ts >= tj)

    def body(ci, carry):
        ch = []
        for d in range(2):
            cd = ci if d == 0 else nc - 1 - ci
            rows = pl.ds(pl.multiple_of(cd * T, T), T)
            for h in range(nh):
                g = g_ref[0, h, cd]
                ig = g[:, d:d + 1] + ib_ref[layer * 2 * nh + d * nh + h]
                fg = g[:, 2 + d:3 + d] + fb_ref[layer * 2 * nh + d * nh + h]
                lf = jnp.broadcast_to(_log_sigmoid(fg), (T, T))
                igb = jnp.broadcast_to(ig, (T, T))
                q = dict(d=d, h=h, rows=rows, ig=ig, qc=qs[h, rows, :], kc=ks[h, rows, :], vc=v_refs[h][0, rows, :])
                q['bcol'] = _dot_sel_l(jnp.where(upto_col[d], 1.0, 0.0), lf)
                q['brow_i'] = _dot_sel_l(ones, jnp.where(upto_row[d], lf, 0.0) - jnp.where(tj == ts, igb, 0.0))
                ch.append(q)
        for q in ch:
            q['qk'] = _dot1(q['qc'], q['kc'], NT)
            q['c_old'] = c_scr[q['d'], q['h']]
            q['qc_c'] = _dot1(q['qc'], q['c_old'])
        for q in ch:
            d, h = q['d'], q['h']
            dmat = jnp.where(upto_col[d], q['bcol'] - q['brow_i'], -jnp.inf)
            b1 = q['bcol'][:, 0:1]
            m_prev = m_scr[d, h]
            inter = b1 + m_prev
            m_t = jnp.maximum(inter, jnp.max(dmat, axis=-1, keepdims=True))
            qk = q['qk'] * jnp.exp(dmat - m_t)
            w_inter = jnp.exp(inter - m_t)
            n_old = n_scr[d, h]
            den = w_inter * jnp.sum(q['qc'] * n_old, axis=-1, keepdims=True) + jnp.sum(qk, axis=-1, keepdims=True)
            q['scale'] = 1.0 / jnp.maximum(jnp.abs(den), jnp.exp(-m_t))
            q['inter_part'] = w_inter * q['qc_c']
            q['intra'] = _dot1(qk, q['vc'])
            b_last = b1[T - 1:T, :] if d == 0 else b1[0:1, :]
            g_s = b_last - b1 + q['ig']
            m_new = jnp.maximum(b_last + m_prev, jnp.max(g_s, axis=0, keepdims=True))
            w_old = jnp.exp(b_last + m_prev - m_new)
            w_s = jnp.exp(g_s - m_new)
            q['c_new'] = _dot1(q['kc'], w_s * q['vc'], TN)
            q['w_old'] = w_old
            n_scr[d, h] = w_old * n_old + jnp.sum(w_s * q['kc'], axis=0, keepdims=True)
            m_scr[d, h] = m_new
        for q in ch:
            d, h = q['d'], q['h']
            hcur = (q['inter_part'] + q['intra']) * q['scale']
            if d == 0:
                hf[h, q['rows'], :] = hcur
            else:
                hb[h, q['rows'], :] = hcur
            c_scr[d, h] = q['w_old'] * q['c_old'] + q['c_new']
        return carry

    lax.fori_loop(0, nc, body, 0)

    for h in range(nh):
        hsum = hf[h] + hb[h]
        mu = jnp.mean(hsum, axis=-1, keepdims=True)
        dv = hsum - mu
        var = jnp.mean(dv * dv, axis=-1, keepdims=True)
        hn = dv * lax.rsqrt(var + ML_GN_EPS)
        y_ref[0, :, h * ML_HEAD:(h + 1) * ML_HEAD] = (
            _sigmoid(o_refs[h][0]) * (hn * lng_ref[0, :, h * ML_HEAD:(h + 1) * ML_HEAD]
                                      + lnb_ref[0, :, h * ML_HEAD:(h + 1) * ML_HEAD])).astype(BF16)
        for d in range(2):
            cf_ref[0, d, h] = c_scr[d, h]
            nf_ref[0, h, d:d + 1, :] = n_scr[d, h]
            mf_ref[0, h, d:d + 1, :] = m_scr[d, h]


def _ml_mixer(st, z3, zg, i_bias, f_bias, ln_g, ln_b, c0, n0, m0, l):
    L, nb, nh = st.L, st.nb, ML_HEADS
    nc = L // ML_CHUNK
    gates = zg.reshape(nb, L, 2, 2, nh).transpose(0, 4, 1, 2, 3).reshape(nb, nh, nc, ML_CHUNK, 4)
    cos_t, sin_t = _rope_tables(L)
    blk = lambda col: pl.BlockSpec((1, L, LANE), lambda b: (b, 0, col))
    head_blocks = [blk(ML_COL + part * nh + h) for part in range(4) for h in range(nh)]
    smem = pl.BlockSpec(memory_space=pltpu.SMEM)
    tab = pl.BlockSpec((L, ML_HEAD), lambda b: (0, 0))
    c_spec = pl.BlockSpec((1, 2, nh, ML_HEAD, ML_HEAD), lambda b: (b, 0, 0, 0, 0))
    n_spec = pl.BlockSpec((1, nh, 2, ML_HEAD), lambda b: (b, 0, 0, 0))
    m_spec = pl.BlockSpec((1, nh, 2, 1), lambda b: (b, 0, 0, 0))
    par = pl.BlockSpec((1, 1, D_GROUP), lambda b: (l, 0, 0))
    return pl.pallas_call(
        functools.partial(_ml_kernel, L=L, rotary=st.latent, layer=l),
        grid=(nb,),
        in_specs=[smem, smem] + head_blocks + [
            pl.BlockSpec((1, nh, nc, ML_CHUNK, 4), lambda b: (b, 0, 0, 0, 0)),
            tab, tab, c_spec, n_spec, m_spec, par, par],
        out_specs=[pl.BlockSpec((1, L, D_GROUP), lambda b: (b, 0, 0)), c_spec, n_spec, m_spec],
        out_shape=[jax.ShapeDtypeStruct((nb, L, D_GROUP), BF16),
                   jax.ShapeDtypeStruct((nb, 2, nh, ML_HEAD, ML_HEAD), F32),
                   jax.ShapeDtypeStruct((nb, nh, 2, ML_HEAD), F32),
                   jax.ShapeDtypeStruct((nb, nh, 2, 1), F32)],
        scratch_shapes=[pltpu.VMEM((nh, L, ML_HEAD), F32), pltpu.VMEM((nh, L, ML_HEAD), F32),
                        pltpu.VMEM((nh, L, ML_HEAD), F32), pltpu.VMEM((nh, L, ML_HEAD), F32),
                        pltpu.VMEM((2, nh, ML_HEAD, ML_HEAD), F32), pltpu.VMEM((2, nh, 1, ML_HEAD), F32),
                        pltpu.VMEM((2, nh, 1, 1), F32)],
        compiler_params=_cparams("parallel"),
        name="mlstm",
    )(i_bias.reshape(-1), f_bias.reshape(-1), *([z3] * (4 * nh)), gates, cos_t, sin_t, c0, n0, m0,
      ln_g.reshape(DEPTH, 1, D_GROUP), ln_b.reshape(DEPTH, 1, D_GROUP))


RW_TL = 256
RW_T = 64
RW_RCOL = RW_OFF // D_GROUP
RW_LCOL = (RW_OFF + 3 * D_GROUP) // (2 * LANE)
N_LORA = RW_LORA_W + RW_LORA_A + RW_LORA_G


def _head_ones():
    return jnp.asarray(np.kron(np.eye(RW_HEADS, dtype=np.float32), np.ones((RW_HEAD, RW_HEAD), np.float32)))


def _rw_front_kernel(r_ref, k_ref, v_ref, lo_ref, rp_ref, kp_ref, vp_ref, lp_ref, rn_ref, kn_ref, vn_ref, ln_ref,
                     mup_ref, mun_ref, w0_ref, w2_ref, a0_ref, a2_ref, g2_ref, kk_ref, ka_ref, hones_ref,
                     ro_ref, vo_ref, kko_ref, go_ref, kd_ref, bd_ref, lw_ref):
    t = pl.program_id(1)
    first = t == 0
    last = t == pl.num_programs(1) - 1
    row = _iota((RW_TL, 1), 0)

    def shift(x_ref, p_ref, n_ref, lo, hi):
        x = x_ref[0]
        prev_edge = jnp.where(first, 0.0, p_ref[0, SUB - 1:SUB, :])
        next_edge = jnp.where(last, 0.0, n_ref[0, 0:1, :])
        prev = jnp.where(row == 0, prev_edge, pltpu.roll(x, 1, axis=0))
        nxt = jnp.where(row == RW_TL - 1, next_edge, pltpu.roll(x, RW_TL - 1, axis=0))
        return x + mup_ref[0][:, lo:hi] * (prev - x) + mun_ref[0][:, lo:hi] * (nxt - x)

    g = D_GROUP
    r = shift(r_ref, rp_ref, rn_ref, 0, g)
    k = shift(k_ref, kp_ref, kn_ref, g, 2 * g)
    v = shift(v_ref, vp_ref, vn_ref, 2 * g, 3 * g)
    lo = shift(lo_ref, lp_ref, ln_ref, 3 * g, 3 * g + N_LORA)
    zw = lo[:, :RW_LORA_W]
    za = lo[:, RW_LORA_W:RW_LORA_W + RW_LORA_A]
    zg = lo[:, RW_LORA_W + RW_LORA_A:]
    ro_ref[0] = r
    vo_ref[0] = v
    kk = k * kk_ref[0]
    ssq = _dot_sel_r(kk * kk, hones_ref[...])
    kk = kk * lax.rsqrt(ssq + 1e-12)
    kko_ref[0] = kk
    go_ref[0] = _dot3(_sigmoid(zg), g2_ref[0])
    tw = jnp.tanh(zw)
    for d in range(2):
        w_log = -_softplus(-(w0_ref[0, d:d + 1, :] + _dot3(tw, w2_ref[0, d]))) - 0.5
        lw_ref[d, 0] = -jnp.exp(w_log)
        a = _sigmoid(a0_ref[0, d:d + 1, :] + _dot3(za, a2_ref[0, d]))
        kd_ref[d, 0] = k * (1.0 + (a - 1.0) * ka_ref[0])
        bd_ref[d, 0] = kk * a


def _rw_front(st, z3, mu_prev, mu_next, w0, w2, a0, a2, g2, k_k, k_a, l):
    nb, L = st.nb, st.L
    nt = L // RW_TL
    tpb = RW_TL // SUB
    main = lambda w, col: pl.BlockSpec((1, RW_TL, w), lambda b, t: (b, t, col))
    prev = lambda w, col: pl.BlockSpec((1, SUB, w), lambda b, t: (b, jnp.maximum(t * tpb - 1, 0), col))
    nxt = lambda w, col: pl.BlockSpec((1, SUB, w), lambda b, t: (b, jnp.minimum((t + 1) * tpb, L // SUB - 1), col))
    cols = [(D_GROUP, RW_RCOL), (D_GROUP, RW_RCOL + 1), (D_GROUP, RW_RCOL + 2), (2 * LANE, RW_LCOL)]
    lay = lambda *shape: pl.BlockSpec((1,) + shape, lambda b, t: (l,) + (0,) * len(shape))
    out1 = pl.BlockSpec((1, RW_TL, D_GROUP), lambda b, t: (b, t, 0))
    out2 = pl.BlockSpec((2, 1, RW_TL, D_GROUP), lambda b, t: (0, b, t, 0))
    s1 = jax.ShapeDtypeStruct((nb, L, D_GROUP), F32)
    s2 = jax.ShapeDtypeStruct((2, nb, L, D_GROUP), F32)
    return pl.pallas_call(
        _rw_front_kernel,
        grid=(nb, nt),
        in_specs=[main(*c) for c in cols] + [prev(*c) for c in cols] + [nxt(*c) for c in cols] + [
            lay(1, RW_IN), lay(1, RW_IN), lay(2, D_GROUP), lay(2, RW_LORA_W, D_GROUP), lay(2, D_GROUP),
            lay(2, RW_LORA_A, D_GROUP), lay(RW_LORA_G, D_GROUP), lay(1, D_GROUP), lay(1, D_GROUP),
            pl.BlockSpec((D_GROUP, D_GROUP), lambda b, t: (0, 0))],
        out_specs=[out1, out1, out1, out1, out2, out2, out2],
        out_shape=[s1, s1, s1, s1, s2, s2, s2],
        compiler_params=_cparams("parallel", "parallel"),
        name="rwkv_front",
    )(*([z3] * 12), mu_prev.reshape(DEPTH, 1, RW_IN), mu_next.reshape(DEPTH, 1, RW_IN), w0, w2, a0, a2, g2,
      k_k.reshape(DEPTH, 1, D_GROUP), k_a.reshape(DEPTH, 1, D_GROUP), _head_ones())


def _rw_core_kernel(rf_ref, vf_ref, kkf_ref, kdf_ref, bdf_ref, lwf_ref,
                    rb_ref, vb_ref, kkb_ref, kdb_ref, bdb_ref, lwb_ref, s0_ref,
                    yf_ref, yb_ref, sf_ref, s_scr):
    c = pl.program_id(1)
    T = RW_T

    @pl.when(c == 0)
    def _():
        s_scr[...] = s0_ref[0]

    tj = _iota((T, T), 0)
    ts = _iota((T, T), 1)
    dirs = ((rf_ref, vf_ref, kkf_ref, kdf_ref, bdf_ref, lwf_ref, yf_ref),
            (rb_ref, vb_ref, kkb_ref, kdb_ref, bdb_ref, lwb_ref, yb_ref))
    ch = []
    for d, (r_ref, v_ref, kk_ref, kd_ref, bd_ref, lw_ref, y_ref) in enumerate(dirs):
        if d == 0:
            incl = ts <= tj
            strict = ts < tj
            last = T - 1
        else:
            incl = ts >= tj
            strict = ts > tj
            last = 0
        lw = lw_ref[0, 0]
        cum = _dot_sel_l(jnp.where(incl, 1.0, 0.0), lw)
        w_in = jnp.exp(cum)
        w_inv = jnp.exp(-cum)
        w_ex = jnp.exp(cum - lw)
        kap_a = kk_ref[0] * w_ex
        bet_a = bd_ref[0, 0] * w_inv
        khat_a = kd_ref[0, 0] * w_inv
        rho_a = r_ref[0] * w_in
        w_tot = w_in[last:last + 1, :]
        v_a = v_ref[0]
        for h in range(RW_HEADS):
            sl = slice(h * RW_HEAD, (h + 1) * RW_HEAD)
            ch.append(dict(d=d, h=h, sl=sl, incl=incl, strict=strict, y_ref=y_ref, w_tot=w_tot[:, sl],
                           kap=kap_a[:, sl], bet=bet_a[:, sl], khat=khat_a[:, sl], rho=rho_a[:, sl], v=v_a[:, sl]))
    for q in ch:
        kr = jnp.concatenate([q['kap'], q['rho']], axis=0)
        bk = jnp.concatenate([q['bet'], q['khat']], axis=0)
        q['gram'] = _dot1(kr, bk, NT)
    for q in ch:
        gram = q.pop('gram')
        q['n'] = -jnp.where(q['strict'], gram[:T, :T], 0.0)
        l_k = jnp.where(q['strict'], gram[:T, T:], 0.0)
        q['m_b'] = jnp.where(q['incl'], gram[T:, :T], 0.0)
        m_k = jnp.where(q['incl'], gram[T:, T:], 0.0)
        q['lmv'] = _dot1(jnp.concatenate([l_k, m_k], axis=0), q['v'])
    for q in ch:
        q['x'] = jnp.concatenate([q['kap'], q['lmv'][:T]], axis=1)
    for lvl in range(6):
        for q in ch:
            if lvl < 5:
                nx = _dot1(q['n'], jnp.concatenate([q['n'], q['x']], axis=1))
                q['n'], q['x'] = nx[:, :T], q['x'] + nx[:, T:]
            else:
                q['x'] = q['x'] + _dot1(q['n'], q['x'])
    for q in ch:
        q['p'] = _dot1(q['m_b'], q['x'])
        xb = _dot1(q['x'], q['bet'], TN)
        q['a_m'] = xb[:RW_HEAD]
        q['d_m'] = _dot1(q['v'], q['khat'], TN) - xb[RW_HEAD:]
    for q in ch:
        rho_p = q['rho'] - q['p'][:, :RW_HEAD]
        y_v = q['lmv'][T:] - q['p'][:, RW_HEAD:]
        s_old = s_scr[q['d'], q['h']]
        q['y_ref'][0, :, q['sl']] = _dot1(rho_p, s_old, NT) + y_v
        s_scr[q['d'], q['h']] = (s_old - _dot3(s_old, q['a_m']) + q['d_m']) * q['w_tot']

    @pl.when(c == pl.num_programs(1) - 1)
    def _():
        sf_ref[0] = s_scr[...]


def _rw_core(st, r, v, kk, kd, bd, lw, s0):
    nb, L = st.nb, st.L
    nc = L // RW_T
    f1 = pl.BlockSpec((1, RW_T, D_GROUP), lambda b, c: (b, c, 0))
    b1 = pl.BlockSpec((1, RW_T, D_GROUP), lambda b, c: (b, nc - 1 - c, 0))
    f2 = pl.BlockSpec((1, 1, RW_T, D_GROUP), lambda b, c: (0, b, c, 0))
    b2 = pl.BlockSpec((1, 1, RW_T, D_GROUP), lambda b, c: (1, b, nc - 1 - c, 0))
    s_spec = pl.BlockSpec((1, 2, RW_HEADS, RW_HEAD, RW_HEAD), lambda b, c: (b, 0, 0, 0, 0))
    ys = jax.ShapeDtypeStruct((nb, L, D_GROUP), F32)
    return pl.pallas_call(
        _rw_core_kernel,
        grid=(nb, nc),
        in_specs=[f1, f1, f1, f2, f2, f2, b1, b1, b1, b2, b2, b2, s_spec],
        out_specs=[f1, b1, s_spec],
        out_shape=[ys, ys, jax.ShapeDtypeStruct((nb, 2, RW_HEADS, RW_HEAD, RW_HEAD), F32)],
        scratch_shapes=[pltpu.VMEM((2, RW_HEADS, RW_HEAD, RW_HEAD), F32)],
        compiler_params=_cparams("parallel", "arbitrary"),
        name="rwkv_core",
    )(r, v, kk, kd, bd, lw, r, v, kk, kd, bd, lw, s0)


RWO_TM = 512


def _rw_out_kernel(yf_ref, yb_ref, r_ref, v_ref, kd0_ref, kd1_ref, g_ref, lng_ref, lnb_ref, rk_ref, hones_ref, o_ref):
    y = yf_ref[...] + yb_ref[...]
    inv = 1.0 / RW_HEAD
    mu = _dot_sel_r(y, hones_ref[...]) * inv
    dv = y - mu
    var = _dot_sel_r(dv * dv, hones_ref[...]) * inv
    yn = dv * lax.rsqrt(var + RW_GN_EPS) * lng_ref[0] + lnb_ref[0]
    kmean = 0.5 * (kd0_ref[0] + kd1_ref[0])
    bonus = _dot_sel_r(r_ref[...] * kmean * rk_ref[0], hones_ref[...]) * v_ref[...]
    o_ref[...] = ((yn + bonus) * g_ref[...]).astype(BF16)


def _rw_out(st, yf, yb, r, v, kd, g, ln_g, ln_b, r_k, l):
    rows = st.rows
    flat = lambda a: a.reshape(rows, D_GROUP)
    row = pl.BlockSpec((RWO_TM, D_GROUP), lambda i: (i, 0))
    lay = pl.BlockSpec((1, 1, D_GROUP), lambda i: (l, 0, 0))
    kd2 = kd.reshape(2, rows, D_GROUP)
    return pl.pallas_call(
        _rw_out_kernel,
        grid=(rows // RWO_TM,),
        in_specs=[row, row, row, row,
                  pl.BlockSpec((1, RWO_TM, D_GROUP), lambda i: (0, i, 0)),
                  pl.BlockSpec((1, RWO_TM, D_GROUP), lambda i: (1, i, 0)),
                  row, lay, lay, lay, pl.BlockSpec((D_GROUP, D_GROUP), lambda i: (0, 0))],
        out_specs=row,
        out_shape=jax.ShapeDtypeStruct((rows, D_GROUP), BF16),
        compiler_params=_cparams("parallel"),
        name="rwkv_out",
    )(flat(yf), flat(yb), flat(r), flat(v), kd2, kd2, flat(g), ln_g.reshape(DEPTH, 1, D_GROUP),
      ln_b.reshape(DEPTH, 1, D_GROUP), r_k.reshape(DEPTH, 1, D_GROUP), _head_ones())


def _trunk_layer(st, x, l, mod4, p, states, ctx_kv):
    nb, L = st.nb, st.L
    z, zg = _zproj(st, x, mod4, p['w_in'], p['w_gate_cols'], l)
    z3 = z.reshape(nb, L, D_Z)

    ydir, sfr, sfi = _s5_scan(st, z3, p['s5_bq'], p['s5_cq'], p['s5_lamr'], p['s5_lami'],
                              states['s5_re'], states['s5_im'], l)
    y_s5 = _s5_out(st, z, ydir.reshape(2, st.rows, D_GROUP), p['s5_d'], p['s5_w_glu'], l)

    r, v, kk, g, kd, bd, lw = _rw_front(st, z3, p['rw_mu_prev'], p['rw_mu_next'], p['rw_w0'], p['rw_w2'],
                                        p['rw_a0'], p['rw_a2'], p['rw_g2'], p['rw_k_k'], p['rw_k_a'], l)
    yf, yb, rw_s = _rw_core(st, r, v, kk, kd, bd, lw, states['rw'])
    y_rw = _rw_out(st, yf, yb, r, v, kd, g, p['rw_ln_g'], p['rw_ln_b'], p['rw_r_k'], l)

    if ctx_kv is None:
        y_na, nk, nv = _na_ctx(st, z3)
    else:
        y_na = _na_nbr(st, z3, ctx_kv[0], ctx_kv[1], p['na_bias'], l)
        nk = nv = None

    y_ml, ml_c, ml_n, ml_m = _ml_mixer(st, z3, zg, p['ml_i_bias'], p['ml_f_bias'], p['ml_ln_g'], p['ml_ln_b'],
                                       states['ml_c'], states['ml_n'], states['ml_m'], l)

    ys = (y_s5, y_rw, y_na.reshape(st.rows, D_GROUP), y_ml.reshape(st.rows, D_GROUP))
    x1, h2, gates = _oproj(st, ys, x, mod4, p['w_out'], p['ln1_g'], p['ln1_b'], p['w_router'], p['b_router'], l)
    f = _moe(st, h2, gates, p['moe_w_gate'], p['moe_w_up'], p['moe_w_down'], l)
    x2 = _ln2(st, x1, f, mod4, p['ln2_g'], p['ln2_b'], l)
    return x2, (nk, nv, sfr, sfi, rw_s, ml_c, ml_n, ml_m)


def kernel(x_prompt, x_sample, cache_nat_k, cache_nat_v, state_s5_re, state_s5_im, state_rwkv, state_mlstm_c, state_mlstm_n, state_mlstm_m, c, c_ctx, w_mod, b_mod, w_in, w_out, s5_lam_re, s5_lam_im, s5_log_step, s5_b_re, s5_b_im, s5_c_re, s5_c_im, s5_d, s5_w_glu, rw_mu_prev, rw_mu_next, rw_w0, rw_w2, rw_a0, rw_a2, rw_g2, rw_k_k, rw_k_a, rw_r_k, rw_ln_g, rw_ln_b, na_rpb, ml_i_bias, ml_f_bias, ml_ln_g, ml_ln_b, ln1_g, ln1_b, ln2_g, ln2_b, w_router, b_router, moe_w_gate, moe_w_up, moe_w_down):
    dt = x_prompt.dtype
    cond = jnp.concatenate([c_ctx[None, :], c, jnp.zeros((MOD_ROWS - 1 - DEC_BATCH, D_MODEL), F32)], axis=0)
    mod4 = _modulation(cond, w_mod, b_mod).reshape(DEPTH, MOD_ROWS, 1, 6 * D_MODEL)

    lbr, lbi, bbr, bbi = _s5_prep(s5_lam_re, s5_lam_im, s5_log_step, s5_b_re, s5_b_im)
    s5_bq, s5_cq, s5_lamr, s5_lami = _s5_block_params(lbr, lbi, bbr, bbi, s5_c_re, s5_c_im)
    p = dict(w_in=w_in, w_gate_cols=w_in[:, :, D_Z:], w_out=w_out,
             s5_bq=s5_bq, s5_cq=s5_cq, s5_lamr=s5_lamr, s5_lami=s5_lami, s5_d=s5_d, s5_w_glu=s5_w_glu,
             rw_mu_prev=rw_mu_prev, rw_mu_next=rw_mu_next, rw_w0=rw_w0, rw_w2=rw_w2, rw_a0=rw_a0, rw_a2=rw_a2,
             rw_g2=rw_g2, rw_k_k=rw_k_k, rw_k_a=rw_k_a, rw_r_k=rw_r_k, rw_ln_g=rw_ln_g, rw_ln_b=rw_ln_b,
             na_bias=_na_bias_table(na_rpb), ml_i_bias=ml_i_bias, ml_f_bias=ml_f_bias, ml_ln_g=ml_ln_g,
             ml_ln_b=ml_ln_b, ln1_g=ln1_g, ln1_b=ln1_b, ln2_g=ln2_g, ln2_b=ln2_b, w_router=w_router,
             b_router=b_router, moe_w_gate=moe_w_gate, moe_w_up=moe_w_up, moe_w_down=moe_w_down)

    gp = BATCH // S5_SEQS
    zero_states = dict(
        s5_re=jnp.zeros((gp, 2, S5_SEQS, N_S5), F32), s5_im=jnp.zeros((gp, 2, S5_SEQS, N_S5), F32),
        rw=jnp.zeros((BATCH, 2, RW_HEADS, RW_HEAD, RW_HEAD), F32),
        ml_c=jnp.zeros((BATCH, 2, ML_HEADS, ML_HEAD, ML_HEAD), F32),
        ml_n=jnp.zeros((BATCH, ML_HEADS, 2, ML_HEAD), F32), ml_m=jnp.zeros((BATCH, ML_HEADS, 2, 1), F32))

    xp = x_prompt.reshape(PROMPT.rows, D_MODEL)
    xs = x_sample.reshape(LATENT.rows, D_MODEL)
    outs = [[] for _ in range(8)]
    for l in range(DEPTH):
        xp, ctx_t = _trunk_layer(PROMPT, xp, l, mod4, p, zero_states, None)
        for acc, t in zip(outs, ctx_t):
            acc.append(t)
        lat_states = dict(
            s5_re=state_s5_re[:, l].reshape(DEC_BATCH, 2, N_S5).transpose(1, 0, 2)[None],
            s5_im=state_s5_im[:, l].reshape(DEC_BATCH, 2, N_S5).transpose(1, 0, 2)[None],
            rw=state_rwkv[:, l], ml_c=state_mlstm_c[:, l],
            ml_n=state_mlstm_n[:, l].transpose(0, 2, 1, 3), ml_m=state_mlstm_m[:, l].transpose(0, 2, 1)[..., None])
        xs, _ = _trunk_layer(LATENT, xs, l, mod4, p, lat_states, (cache_nat_k, cache_nat_v))

    nk, nv, s5r, s5i, rw, mc, mn, mm = [jnp.stack(t, axis=1) for t in outs]

    def s5_state(t):
        return t.transpose(0, 3, 1, 2, 4).reshape(BATCH, DEPTH, 2, S5_GROUPS, S5_STATE)

    return (xp.reshape(BATCH, SEQ, D_MODEL), xs.reshape(DEC_BATCH, DEC_SEQ, D_MODEL),
            nk, nv, s5_state(s5r).astype(dt), s5_state(s5i).astype(dt), rw.astype(dt), mc.astype(dt),
            mn.transpose(0, 1, 3, 2, 4).astype(dt), mm[..., 0].transpose(0, 1, 3, 2).astype(dt))
```

```python
import functools
import math

import numpy as np
import jax
import jax.numpy as jnp
from jax import lax
from jax.experimental import pallas as pl
from jax.experimental.pallas import tpu as pltpu

F32 = jnp.float32
BF16 = jnp.bfloat16

D_MODEL = 2048
BATCH = 16
SEQ = 256
DEPTH = 4
DEC_BATCH = 8
DEC_SEQ = 1024
PAST_LEN = 256
GRID_W = 64
D_GROUP = D_MODEL // 4
S5_CH = 16
S5_GROUPS = D_GROUP // S5_CH
S5_STATE = 64
RW_HEAD = 64
RW_HEADS = D_GROUP // RW_HEAD
RW_LORA_W = 64
RW_LORA_A = 64
RW_LORA_G = 128
RW_GN_EPS = 64e-5
NA_HEAD = 64
NA_HEADS = D_GROUP // NA_HEAD
NA_WIN_H = 8
NA_WIN_W = 16
NEG_INF = -1e30
ML_HEAD = 128
ML_HEADS = D_GROUP // ML_HEAD
ML_CHUNK = 64
ML_GN_EPS = 1e-5
ROPE_BASE = 10000.0
N_EXPERTS = 16
N_EXPERT_GROUPS = 4
EXPERTS_PER_GROUP = N_EXPERTS // N_EXPERT_GROUPS
D_EXPERT = 512
DEEPNORM_ALPHA = (2 * DEPTH) ** 0.25
LN_EPS = 1e-5
S5_IN = D_GROUP
RW_IN = 3 * D_GROUP + RW_LORA_W + RW_LORA_A + RW_LORA_G
NA_IN = 3 * D_GROUP
ML_IN = 4 * D_GROUP + 4 * ML_HEADS
D_IN = S5_IN + RW_IN + NA_IN + ML_IN
N_GATE = 4 * ML_HEADS
D_Z = D_IN - N_GATE
RW_OFF = S5_IN
NA_OFF = S5_IN + RW_IN
ML_OFF = NA_OFF + NA_IN
MOD_ROWS = 16
LANE = 128
SUB = 8

VMEM_LIMIT = 56 * 1024 * 1024


def _cparams(*sem):
    return pltpu.CompilerParams(dimension_semantics=sem, vmem_limit_bytes=VMEM_LIMIT)


def _dg(a, b, dims):
    return lax.dot_general(a, b, (dims, ((), ())), preferred_element_type=F32)


NN = ((1,), (0,))
NT = ((1,), (1,))
TN = ((0,), (0,))


def _dot1(a, b, dims=NN):
    return _dg(a.astype(BF16), b.astype(BF16), dims)


def _split(x):
    hi = x.astype(BF16)
    lo = (x - hi.astype(F32)).astype(BF16)
    return hi, lo


def _split3(x):
    hi = x.astype(BF16)
    r = x - hi.astype(F32)
    mid = r.astype(BF16)
    lo = (r - mid.astype(F32)).astype(BF16)
    return hi, mid, lo


def _dot3(a, b, dims=NN):
    ah, al = _split(a)
    bh, bl = _split(b)
    return _dg(ah, bh, dims) + (_dg(ah, bl, dims) + _dg(al, bh, dims))


def _dot_sel_l(sel, x, dims=NN):
    s = sel.astype(BF16)
    hi, mid, lo = _split3(x)
    return _dg(s, hi, dims) + (_dg(s, mid, dims) + _dg(s, lo, dims))


def _dot_sel_r(x, sel, dims=NN):
    s = sel.astype(BF16)
    hi, mid, lo = _split3(x)
    return _dg(hi, s, dims) + (_dg(mid, s, dims) + _dg(lo, s, dims))


def _sigmoid(x):
    return 1.0 / (1.0 + jnp.exp(-x))


def _silu(x):
    return x * _sigmoid(x)


def _softplus(x):
    return jnp.maximum(x, 0.0) + jnp.log(1.0 + jnp.exp(-jnp.abs(x)))


def _iota(shape, dim):
    return lax.broadcasted_iota(jnp.int32, shape, dim)


class _Stream:
    def __init__(self, nb, L, latent):
        self.nb, self.L, self.latent = nb, L, latent
        self.rows = nb * L

    def mod_row(self, tile, tile_rows):
        if not self.latent:
            return 0
        return 1 + (tile * tile_rows) // self.L


PROMPT = _Stream(BATCH, SEQ, False)
LATENT = _Stream(DEC_BATCH, DEC_SEQ, True)


MOD_TN = 768


def _mod_kernel(cond_ref, w_ref, b_ref, o_ref):
    c = _silu(cond_ref[...])
    o_ref[0] = _dot3(c, w_ref[0]) + b_ref[0]


def _modulation(cond, w_mod, b_mod):
    n = 6 * D_MODEL
    return pl.pallas_call(
        _mod_kernel,
        grid=(DEPTH, n // MOD_TN),
        in_specs=[pl.BlockSpec((MOD_ROWS, D_MODEL), lambda l, j: (0, 0)),
                  pl.BlockSpec((1, D_MODEL, MOD_TN), lambda l, j: (l, 0, j)),
                  pl.BlockSpec((1, 1, MOD_TN), lambda l, j: (l, 0, j))],
        out_specs=pl.BlockSpec((1, MOD_ROWS, MOD_TN), lambda l, j: (l, 0, j)),
        out_shape=jax.ShapeDtypeStruct((DEPTH, MOD_ROWS, n), F32),
        compiler_params=_cparams("parallel", "parallel"),
        name="modulation",
    )(cond, w_mod, b_mod.reshape(DEPTH, 1, n))


ZP_TM = 1024
ZP_TN = 256


def _zproj_kernel(x_ref, mod_ref, w_ref, wg_ref, z_ref, zg_ref, h_scr):
    j = pl.program_id(1)

    @pl.when(j == 0)
    def _():
        m = mod_ref[0, 0]
        shift1 = m[:, 0:D_MODEL]
        scale1 = m[:, D_MODEL:2 * D_MODEL]
        h = x_ref[...] * (1.0 + scale1) + shift1
        h_scr[...] = h.astype(BF16)
        zg_ref[...] = _dot3(h, wg_ref[0])

    z_ref[...] = jnp.dot(h_scr[...], w_ref[0].astype(BF16), preferred_element_type=F32)


def _zproj(st, x, mod4, w_in, w_gate_cols, l):
    return pl.pallas_call(
        _zproj_kernel,
        grid=(st.rows // ZP_TM, D_Z // ZP_TN),
        in_specs=[pl.BlockSpec((ZP_TM, D_MODEL), lambda i, j: (i, 0)),
                  pl.BlockSpec((1, 1, 1, 6 * D_MODEL), lambda i, j: (l, st.mod_row(i, ZP_TM), 0, 0)),
                  pl.BlockSpec((1, D_MODEL, ZP_TN), lambda i, j: (l, 0, j)),
                  pl.BlockSpec((1, D_MODEL, N_GATE), lambda i, j: (l, 0, 0))],
        out_specs=[pl.BlockSpec((ZP_TM, ZP_TN), lambda i, j: (i, j)),
                   pl.BlockSpec((ZP_TM, N_GATE), lambda i, j: (i, 0))],
        out_shape=[jax.ShapeDtypeStruct((st.rows, D_Z), F32),
                   jax.ShapeDtypeStruct((st.rows, N_GATE), F32)],
        scratch_shapes=[pltpu.VMEM((ZP_TM, D_MODEL), BF16)],
        compiler_params=_cparams("parallel", "arbitrary"),
        name="zproj",
    )(x, mod4, w_in, w_gate_cols)


OP_TM = 512
OP_TN = 512


def _layer_norm(v, g, b):
    mu = jnp.mean(v, axis=-1, keepdims=True)
    d = v - mu
    var = jnp.mean(d * d, axis=-1, keepdims=True)
    return d * lax.rsqrt(var + LN_EPS) * g + b


def _route(scores, b_router):
    sel = scores + b_router
    s = [sel[:, e:e + 1] for e in range(N_EXPERTS)]
    in_top2 = []
    for g in range(N_EXPERT_GROUPS):
        for i in range(EXPERTS_PER_GROUP):
            e = g * EXPERTS_PER_GROUP + i
            cnt = jnp.zeros_like(s[e])
            for jj in range(EXPERTS_PER_GROUP):
                if jj == i:
                    continue
                o = g * EXPERTS_PER_GROUP + jj
                beats = (s[o] > s[e]) if jj > i else (s[o] >= s[e])
                cnt = cnt + jnp.where(beats, 1.0, 0.0)
            in_top2.append(cnt < 2.0)
    grp = []
    for g in range(N_EXPERT_GROUPS):
        tot = jnp.zeros_like(s[0])
        for i in range(EXPERTS_PER_GROUP):
            e = g * EXPERTS_PER_GROUP + i
            tot = tot + jnp.where(in_top2[e], s[e], 0.0)
        grp.append(tot)
    lane = _iota(scores.shape, 1)
    keep = jnp.zeros(scores.shape, F32)
    for g in range(N_EXPERT_GROUPS):
        cnt = jnp.zeros_like(s[0])
        for o in range(N_EXPERT_GROUPS):
            if o == g:
                continue
            beats = (grp[o] > grp[g]) if o > g else (grp[o] >= grp[g])
            cnt = cnt + jnp.where(beats, 1.0, 0.0)
        best = cnt < 1.0
        for i in range(EXPERTS_PER_GROUP):
            e = g * EXPERTS_PER_GROUP + i
            on = jnp.where(best, jnp.where(in_top2[e], 1.0, 0.0), 0.0)
            keep = keep + jnp.where(lane == e, on, 0.0)
    picked = scores * keep
    return picked / jnp.sum(picked, axis=-1, keepdims=True), keep


def _oproj_kernel(y0_ref, y1_ref, y2_ref, y3_ref, w_ref, x_ref, mod_ref, g_ref, b_ref, wr_ref, br_ref,
                  x1_ref, h2_ref, gate_ref, keep_ref, acc_scr):
    j = pl.program_id(1)
    acc = jnp.zeros((OP_TM, OP_TN), F32)
    for k, yr in enumerate((y0_ref, y1_ref, y2_ref, y3_ref)):
        wk = w_ref[0, k * D_GROUP:(k + 1) * D_GROUP, :].astype(BF16)
        acc = acc + jnp.dot(yr[...], wk, preferred_element_type=F32)
    for jj in range(D_MODEL // OP_TN):
        @pl.when(j == jj)
        def _(jj=jj):
            acc_scr[:, jj * OP_TN:(jj + 1) * OP_TN] = acc

    @pl.when(j == D_MODEL // OP_TN - 1)
    def _():
        m = mod_ref[0, 0]
        gate1 = m[:, 2 * D_MODEL:3 * D_MODEL]
        shift2 = m[:, 3 * D_MODEL:4 * D_MODEL]
        scale2 = m[:, 4 * D_MODEL:5 * D_MODEL]
        x1 = _layer_norm(DEEPNORM_ALPHA * x_ref[...] + gate1 * acc_scr[...], g_ref[0], b_ref[0])
        x1_ref[...] = x1
        h2 = x1 * (1.0 + scale2) + shift2
        h2_ref[...] = h2
        scores = _sigmoid(_dot3(h2, wr_ref[...]))
        gate_ref[...], keep_ref[...] = _route(scores, br_ref[...])


def _oproj(st, ys, x, mod4, w_out, ln_g, ln_b, w_router, b_router, l):
    row = lambda i, j: (i, 0)
    return pl.pallas_call(
        _oproj_kernel,
        grid=(st.rows // OP_TM, D_MODEL // OP_TN),
        in_specs=[pl.BlockSpec((OP_TM, D_GROUP), row)] * 4 + [
            pl.BlockSpec((1, D_MODEL, OP_TN), lambda i, j: (l, 0, j)),
            pl.BlockSpec((OP_TM, D_MODEL), row),
            pl.BlockSpec((1, 1, 1, 6 * D_MODEL), lambda i, j: (l, st.mod_row(i, OP_TM), 0, 0)),
            pl.BlockSpec((1, 1, D_MODEL), lambda i, j: (l, 0, 0)),
            pl.BlockSpec((1, 1, D_MODEL), lambda i, j: (l, 0, 0)),
            pl.BlockSpec((D_MODEL, N_EXPERTS), lambda i, j: (0, 0)),
            pl.BlockSpec((1, N_EXPERTS), lambda i, j: (0, 0))],
        out_specs=[pl.BlockSpec((OP_TM, D_MODEL), row),
                   pl.BlockSpec((OP_TM, D_MODEL), row),
                   pl.BlockSpec((OP_TM, N_EXPERTS), row),
                   pl.BlockSpec((OP_TM, N_EXPERTS), row)],
        out_shape=[jax.ShapeDtypeStruct((st.rows, D_MODEL), F32),
                   jax.ShapeDtypeStruct((st.rows, D_MODEL), F32),
                   jax.ShapeDtypeStruct((st.rows, N_EXPERTS), F32),
                   jax.ShapeDtypeStruct((st.rows, N_EXPERTS), F32)],
        scratch_shapes=[pltpu.VMEM((OP_TM, D_MODEL), F32)],
        compiler_params=_cparams("parallel", "arbitrary"),
        name="oproj_ln1_router",
    )(*ys, w_out, x, mod4, ln_g.reshape(DEPTH, 1, D_MODEL), ln_b.reshape(DEPTH, 1, D_MODEL),
      w_router, b_router.reshape(1, N_EXPERTS))


N_TOK = BATCH * SEQ + DEC_BATCH * DEC_SEQ
N_ASSIGN = 2 * N_TOK
MOE_TM = 512
MOE_ROWS = N_ASSIGN + N_EXPERTS * MOE_TM
MOE_TILES = MOE_ROWS // MOE_TM
RC_BLK = 512


def _moe_plan(keep, gates):
    kb = keep > 0.5
    ki = kb.astype(jnp.int32)
    cnt = jnp.sum(ki, axis=0)
    padded = (cnt + MOE_TM - 1) // MOE_TM * MOE_TM
    ends = jnp.cumsum(padded)
    pos = (ends - padded)[None, :] + jnp.cumsum(ki, axis=0) - ki
    e0 = jnp.argmax(kb, axis=1)
    e1 = N_EXPERTS - 1 - jnp.argmax(kb[:, ::-1], axis=1)
    take = lambda a, e: jnp.take_along_axis(a, e[:, None], axis=1)[:, 0]
    dst = jnp.concatenate([take(pos, e0), take(pos, e1)]).astype(jnp.int32)
    w01 = jnp.stack([take(gates, e0), take(gates, e1)], axis=1)
    n_active = (ends[-1] // MOE_TM).astype(jnp.int32)
    tile = jnp.arange(MOE_TILES, dtype=jnp.int32)
    owner = jnp.searchsorted(ends, jnp.minimum(tile, n_active - 1) * MOE_TM, side='right').astype(jnp.int32)
    return dst, w01, jnp.minimum(owner, N_EXPERTS - 1), n_active.reshape(1)


def _row_copy_kernel(sidx_ref, didx_ref, src_ref, *rest):
    dst_ref, sem = rest[-2], rest[-1]
    base = pl.program_id(0) * RC_BLK

    def row_copy(s, d):
        return pltpu.make_async_copy(src_ref.at[pl.ds(s, 1)], dst_ref.at[pl.ds(d, 1)], sem)

    def issue(j, carry):
        row_copy(sidx_ref[base + j], didx_ref[base + j]).start()
        return carry

    def drain(j, carry):
        row_copy(0, 0).wait()
        return carry

    lax.fori_loop(0, RC_BLK, issue, 0)
    lax.fori_loop(0, RC_BLK, drain, 0)


def _row_copy(src, src_idx, dst_idx, n_dst, dst_init=None):
    n = src_idx.shape[0]
    any_spec = pl.BlockSpec(memory_space=pl.ANY)
    operands = (src,) if dst_init is None else (src, dst_init)
    return pl.pallas_call(
        _row_copy_kernel,
        grid_spec=pltpu.PrefetchScalarGridSpec(
            num_scalar_prefetch=2, grid=(n // RC_BLK,),
            in_specs=[any_spec] * len(operands), out_specs=any_spec,
            scratch_shapes=[pltpu.SemaphoreType.DMA(())]),
        out_shape=jax.ShapeDtypeStruct((n_dst, src.shape[1]), src.dtype),
        input_output_aliases={} if dst_init is None else {3: 0},
        compiler_params=_cparams("arbitrary"),
        name="moe_row_copy",
    )(src_idx, dst_idx, *operands)


def _moe_experts_kernel(owner_ref, nact_ref, xs_ref, wg_ref, wu_ref, wd_ref, y_ref, wg_b, wu_b, wd_b):
    i = pl.program_id(0)

    @pl.when(i < nact_ref[0])
    def _():
        @pl.when((i == 0) | (owner_ref[i] != owner_ref[jnp.maximum(i - 1, 0)]))
        def _():
            wg_b[...] = wg_ref[0, 0].astype(BF16)
            wu_b[...] = wu_ref[0, 0].astype(BF16)
            wd_b[...] = wd_ref[0, 0].astype(BF16)

        x = xs_ref[...].astype(BF16)
        a = jnp.dot(x, wg_b[...], preferred_element_type=F32)
        u = jnp.dot(x, wu_b[...], preferred_element_type=F32)
        hid = (_silu(a) * u).astype(BF16)
        y_ref[...] = jnp.dot(hid, wd_b[...], preferred_element_type=F32)

    @pl.when(i >= nact_ref[0])
    def _():
        y_ref[...] = jnp.zeros_like(y_ref)


def _moe_experts(xs, owner, n_active, w_gate, w_up, w_down, l):
    tile = lambda i, owner, nact: (jnp.minimum(i, nact[0] - 1), 0)
    wspec = lambda a, b: pl.BlockSpec((1, 1, a, b), lambda i, owner, nact: (l, owner[i], 0, 0))
    return pl.pallas_call(
        _moe_experts_kernel,
        grid_spec=pltpu.PrefetchScalarGridSpec(
            num_scalar_prefetch=2, grid=(MOE_TILES,),
            in_specs=[pl.BlockSpec((MOE_TM, D_MODEL), tile), wspec(D_MODEL, D_EXPERT), wspec(D_MODEL, D_EXPERT),
                      wspec(D_EXPERT, D_MODEL)],
            out_specs=pl.BlockSpec((MOE_TM, D_MODEL), lambda i, owner, nact: (i, 0)),
            scratch_shapes=[pltpu.VMEM((D_MODEL, D_EXPERT), BF16), pltpu.VMEM((D_MODEL, D_EXPERT), BF16),
                            pltpu.VMEM((D_EXPERT, D_MODEL), BF16)]),
        out_shape=jax.ShapeDtypeStruct((MOE_ROWS, D_MODEL), F32),
        compiler_params=_cparams("arbitrary"),
        name="moe_experts",
    )(owner, n_active, xs, w_gate, w_up, w_down)


def _moe_routed(h2_all, keep, gates, w_gate, w_up, w_down, l):
    dst, w01, owner, n_active = _moe_plan(keep, gates)
    tok = jnp.tile(jnp.arange(N_TOK, dtype=jnp.int32), 2)
    xs = _row_copy(h2_all, tok, dst, MOE_ROWS, dst_init=jnp.zeros((MOE_ROWS, D_MODEL), F32))
    ys = _moe_experts(xs, owner, n_active, w_gate, w_up, w_down, l)
    y01 = _row_copy(ys, dst, jnp.arange(N_ASSIGN, dtype=jnp.int32), N_ASSIGN)
    return y01, w01


LN_TM = 512


def _ln2_kernel(x_ref, y0_ref, y1_ref, w_ref, mod_ref, g_ref, b_ref, o_ref):
    gate2 = mod_ref[0, 0][:, 5 * D_MODEL:6 * D_MODEL]
    w = w_ref[...]
    f = w[:, 0:1] * y0_ref[...] + w[:, 1:2] * y1_ref[...]
    o_ref[...] = _layer_norm(DEEPNORM_ALPHA * x_ref[...] + gate2 * f, g_ref[0], b_ref[0])


def _ln2(st, x1, y01, w01, mod4, ln_g, ln_b, l, tok_off):
    row = lambda i: (i, 0)
    off = tok_off // LN_TM
    return pl.pallas_call(
        _ln2_kernel,
        grid=(st.rows // LN_TM,),
        in_specs=[pl.BlockSpec((LN_TM, D_MODEL), row),
                  pl.BlockSpec((LN_TM, D_MODEL), lambda i: (off + i, 0)),
                  pl.BlockSpec((LN_TM, D_MODEL), lambda i: (N_TOK // LN_TM + off + i, 0)),
                  pl.BlockSpec((LN_TM, 2), lambda i: (off + i, 0)),
                  pl.BlockSpec((1, 1, 1, 6 * D_MODEL), lambda i: (l, st.mod_row(i, LN_TM), 0, 0)),
                  pl.BlockSpec((1, 1, D_MODEL), lambda i: (l, 0, 0)),
                  pl.BlockSpec((1, 1, D_MODEL), lambda i: (l, 0, 0))],
        out_specs=pl.BlockSpec((LN_TM, D_MODEL), row),
        out_shape=jax.ShapeDtypeStruct((st.rows, D_MODEL), F32),
        compiler_params=_cparams("parallel"),
        name="ln2",
    )(x1, y01, y01, w01, mod4, ln_g.reshape(DEPTH, 1, D_MODEL), ln_b.reshape(DEPTH, 1, D_MODEL))


S5_TC = 64
S5_SEQS = SUB
S5_Q = 4
S5_QS = S5_GROUPS // S5_Q * S5_STATE
N_S5 = S5_GROUPS * S5_STATE


def _s5_prep_kernel(lr_ref, li_ref, ls_ref, br_ref, bi_ref, e_ref, lbr_ref, lbi_ref, bbr_ref, bbi_ref):
    lr = lr_ref[0]
    li = li_ref[0]
    dt = jnp.exp(ls_ref[0])
    mag = jnp.exp(lr * dt)
    ang = li * dt
    ar = mag * jnp.cos(ang)
    ai = mag * jnp.sin(ang)
    lbr_ref[0] = ar
    lbi_ref[0] = ai
    den = lr * lr + li * li
    nr = ar - 1.0
    cr = (nr * lr + ai * li) / den
    ci = (ai * lr - nr * li) / den
    cr = _dot_sel_l(e_ref[...], cr)
    ci = _dot_sel_l(e_ref[...], ci)
    bre = br_ref[0]
    bim = bi_ref[0]
    bbr_ref[0] = cr * bre - ci * bim
    bbi_ref[0] = cr * bim + ci * bre


def _s5_prep(lam_re, lam_im, log_step, b_re, b_im):
    d2 = DEPTH * 2
    g, p, h = S5_GROUPS, S5_STATE, S5_CH
    bt_re = jnp.swapaxes(b_re, -1, -2).reshape(d2, g * h, p)
    bt_im = jnp.swapaxes(b_im, -1, -2).reshape(d2, g * h, p)
    expand = jnp.asarray(np.kron(np.eye(g, dtype=np.float32), np.ones((h, 1), np.float32)))
    spec_gp = pl.BlockSpec((1, g, p), lambda i: (i, 0, 0))
    spec_b = pl.BlockSpec((1, g * h, p), lambda i: (i, 0, 0))
    lbr, lbi, bbr, bbi = pl.pallas_call(
        _s5_prep_kernel,
        grid=(d2,),
        in_specs=[spec_gp, spec_gp, pl.BlockSpec((1, g, 1), lambda i: (i, 0, 0)), spec_b, spec_b,
                  pl.BlockSpec((g * h, g), lambda i: (0, 0))],
        out_specs=[spec_gp, spec_gp, spec_b, spec_b],
        out_shape=[jax.ShapeDtypeStruct((d2, g, p), F32)] * 2 + [jax.ShapeDtypeStruct((d2, g * h, p), F32)] * 2,
        compiler_params=_cparams("parallel"),
        name="s5_prep",
    )(lam_re.reshape(d2, g, p), lam_im.reshape(d2, g, p), log_step.reshape(d2, g, 1), bt_re, bt_im, expand)
    return lbr, lbi, bbr, bbi


def _s5_block_params(lbr, lbi, bbr, bbi, c_re, c_im):
    d2 = DEPTH * 2
    gq = S5_GROUPS // S5_Q
    eye = jnp.eye(gq, dtype=F32)

    def b_blocks(b):
        b = b.reshape(d2, S5_Q, gq, S5_CH, S5_STATE)
        return jnp.einsum('djghp,gk->djghkp', b, eye).reshape(d2, S5_Q, gq * S5_CH, gq * S5_STATE)

    def c_blocks(c):
        c = c.reshape(d2, S5_Q, gq, S5_CH, S5_STATE)
        return jnp.einsum('djghp,gk->djgpkh', c, eye).reshape(d2, S5_Q, gq * S5_STATE, gq * S5_CH)

    bq = jnp.concatenate([b_blocks(bbr), b_blocks(bbi)], axis=-1)
    cq = jnp.concatenate([c_blocks(c_re), -c_blocks(c_im)], axis=-2)
    return bq, cq, lbr.reshape(d2, 1, N_S5), lbi.reshape(d2, 1, N_S5)


def _s5_scan_kernel(u_ref, bq_ref, cq_ref, lr_ref, li_ref, s0r_ref, s0i_ref, y_ref, sfr_ref, sfi_ref,
                    utb, bur, bui, ytb, sr_scr, si_scr):
    d = pl.program_id(1)
    c = pl.program_id(2)
    nrow = S5_TC * S5_SEQS
    cw = D_GROUP // S5_Q

    @pl.when(c == 0)
    def _():
        sr_scr[...] = s0r_ref[0, 0]
        si_scr[...] = s0i_ref[0, 0]

    for b in range(S5_SEQS):
        for j in range(S5_Q):
            utb[j, pl.ds(b, S5_TC, stride=S5_SEQS), :] = u_ref[b, :, j * cw:(j + 1) * cw]
    for j in range(S5_Q):
        bu = _dot3(utb[j], bq_ref[0, j])
        bur[:, j * S5_QS:(j + 1) * S5_QS] = bu[:, :S5_QS]
        bui[:, j * S5_QS:(j + 1) * S5_QS] = bu[:, S5_QS:]
    for j in range(S5_Q):
        sl = slice(j * S5_QS, (j + 1) * S5_QS)
        lam_r = jnp.broadcast_to(lr_ref[0][:, sl], (S5_SEQS, S5_QS))
        lam_i = jnp.broadcast_to(li_ref[0][:, sl], (S5_SEQS, S5_QS))

        def step(t, carry, sl=sl, lam_r=lam_r, lam_i=lam_i):
            sr, si = carry
            te = jnp.where(d == 0, t, S5_TC - 1 - t)
            rows = pl.ds(pl.multiple_of(te * S5_SEQS, S5_SEQS), S5_SEQS)
            nr = lam_r * sr - lam_i * si + bur[rows, sl]
            ni = lam_r * si + lam_i * sr + bui[rows, sl]
            bur[rows, sl] = nr
            bui[rows, sl] = ni
            return nr, ni

        sr, si = lax.fori_loop(0, S5_TC, step, (sr_scr[:, sl], si_scr[:, sl]))
        sr_scr[:, sl] = sr
        si_scr[:, sl] = si
    for j in range(S5_Q):
        sl = slice(j * S5_QS, (j + 1) * S5_QS)
        yj = _dot3(bur[:, sl], cq_ref[0, j, :S5_QS, :]) + _dot3(bui[:, sl], cq_ref[0, j, S5_QS:, :])
        ytb[j] = yj
    for b in range(S5_SEQS):
        for j in range(S5_Q):
            y_ref[0, b, :, j * cw:(j + 1) * cw] = ytb[j, pl.ds(b, S5_TC, stride=S5_SEQS), :]

    @pl.when(c == pl.num_programs(2) - 1)
    def _():
        sfr_ref[0, 0] = sr_scr[...]
        sfi_ref[0, 0] = si_scr[...]


def _s5_scan(st, z3, bq, cq, lamr, lami, s0r, s0i, l):
    ng, nc = st.nb // S5_SEQS, st.L // S5_TC
    nrow = S5_TC * S5_SEQS
    chunk = lambda d, c: c + d * (nc - 1 - 2 * c)
    par = lambda g, d, c: (2 * l + d, 0, 0, 0)
    st_spec = pl.BlockSpec((1, 1, S5_SEQS, N_S5), lambda g, d, c: (g, d, 0, 0))
    return pl.pallas_call(
        _s5_scan_kernel,
        grid=(ng, 2, nc),
        in_specs=[pl.BlockSpec((S5_SEQS, S5_TC, D_GROUP), lambda g, d, c: (g, chunk(d, c), 0)),
                  pl.BlockSpec((1, S5_Q, D_GROUP // S5_Q, 2 * S5_QS), par),
                  pl.BlockSpec((1, S5_Q, 2 * S5_QS, D_GROUP // S5_Q), par),
                  pl.BlockSpec((1, 1, N_S5), lambda g, d, c: (2 * l + d, 0, 0)),
                  pl.BlockSpec((1, 1, N_S5), lambda g, d, c: (2 * l + d, 0, 0)),
                  st_spec, st_spec],
        out_specs=[pl.BlockSpec((1, S5_SEQS, S5_TC, D_GROUP), lambda g, d, c: (d, g, chunk(d, c), 0)),
                   st_spec, st_spec],
        out_shape=[jax.ShapeDtypeStruct((2, st.nb, st.L, D_GROUP), F32),
                   jax.ShapeDtypeStruct((ng, 2, S5_SEQS, N_S5), F32),
                   jax.ShapeDtypeStruct((ng, 2, S5_SEQS, N_S5), F32)],
        scratch_shapes=[pltpu.VMEM((S5_Q, nrow, LANE), F32), pltpu.VMEM((nrow, N_S5), F32),
                        pltpu.VMEM((nrow, N_S5), F32), pltpu.VMEM((S5_Q, nrow, LANE), F32),
                        pltpu.VMEM((S5_SEQS, N_S5), F32), pltpu.VMEM((S5_SEQS, N_S5), F32)],
        compiler_params=_cparams("parallel", "arbitrary", "arbitrary"),
        name="s5_scan",
    )(z3, bq, cq, lamr, lami, s0r, s0i)


S5_TM = 512


def _gelu_tanh(x):
    return 0.5 * x * (1.0 + jnp.tanh(math.sqrt(2.0 / math.pi) * (x + 0.044715 * (x * x * x))))


def _s5_out_kernel(u_ref, yf_ref, yb_ref, d_ref, w_ref, o_ref):
    y = yf_ref[0] + yb_ref[0] + d_ref[0] * u_ref[...]
    y = _gelu_tanh(y)
    o_ref[...] = (y * _sigmoid(_dot1(y, w_ref[0]))).astype(BF16)


def _s5_out(st, z, ydir, d_skip, w_glu, l):
    return pl.pallas_call(
        _s5_out_kernel,
        grid=(st.rows // S5_TM,),
        in_specs=[pl.BlockSpec((S5_TM, D_GROUP), lambda i: (i, 0)),
                  pl.BlockSpec((1, S5_TM, D_GROUP), lambda i: (0, i, 0)),
                  pl.BlockSpec((1, S5_TM, D_GROUP), lambda i: (1, i, 0)),
                  pl.BlockSpec((1, 1, D_GROUP), lambda i: (l, 0, 0)),
                  pl.BlockSpec((1, D_GROUP, D_GROUP), lambda i: (l, 0, 0))],
        out_specs=pl.BlockSpec((S5_TM, D_GROUP), lambda i: (i, 0)),
        out_shape=jax.ShapeDtypeStruct((st.rows, D_GROUP), BF16),
        compiler_params=_cparams("parallel"),
        name="s5_out",
    )(z, ydir, ydir, d_skip.reshape(DEPTH, 1, D_GROUP), w_glu)


NA_SCALE = NA_HEAD ** -0.5
NA_ROWS = DEC_SEQ // GRID_W
NA_KH = min(NA_WIN_H, NA_ROWS)
NA_QCOL = NA_OFF // LANE
NA_KCOL = (NA_OFF + D_GROUP) // LANE
NA_VCOL = (NA_OFF + 2 * D_GROUP) // LANE
NA_NDR = 2 * NA_WIN_H - 1


def _na_ctx_kernel(q_ref, k_ref, v_ref, o_ref, nk_ref, nv_ref):
    for hh in range(2):
        sl = slice(hh * NA_HEAD, (hh + 1) * NA_HEAD)
        q = q_ref[0, :, sl]
        k = k_ref[0, :, sl]
        v = v_ref[0, :, sl]
        nk_ref[0, hh] = k
        nv_ref[0, hh] = v
        s = _dot1(q, k, NT) * NA_SCALE
        e = jnp.exp(s - jnp.max(s, axis=-1, keepdims=True))
        o = _dot1(e, v) / jnp.sum(e, axis=-1, keepdims=True)
        o_ref[0, :, sl] = o.astype(BF16)


def _na_ctx(st, z3):
    blk = lambda col: pl.BlockSpec((1, st.L, LANE), lambda b, p: (b, 0, col + p))
    kv_spec = pl.BlockSpec((1, 2, st.L, NA_HEAD), lambda b, p: (b, p, 0, 0))
    kv_shape = jax.ShapeDtypeStruct((st.nb, NA_HEADS, st.L, NA_HEAD), F32)
    return pl.pallas_call(
        _na_ctx_kernel,
        grid=(st.nb, NA_HEADS // 2),
        in_specs=[blk(NA_QCOL), blk(NA_KCOL), blk(NA_VCOL)],
        out_specs=[pl.BlockSpec((1, st.L, LANE), lambda b, p: (b, 0, p)), kv_spec, kv_spec],
        out_shape=[jax.ShapeDtypeStruct((st.nb, st.L, D_GROUP), BF16), kv_shape, kv_shape],
        compiler_params=_cparams("parallel", "parallel"),
        name="na_context",
    )(z3, z3, z3)


def _na_nbr_kernel(q_ref, k_ref, v_ref, ck_ref, cv_ref, bias_ref, o_ref):
    nloc = NA_KH * GRID_W
    qc = _iota((GRID_W, nloc), 0)
    kc = _iota((GRID_W, nloc), 1) % GRID_W
    cs = jnp.clip(qc - NA_WIN_W // 2, 0, GRID_W - NA_WIN_W)
    col_in = (kc >= cs) & (kc < cs + NA_WIN_W)
    for hh in range(2):
        sl = slice(hh * NA_HEAD, (hh + 1) * NA_HEAD)
        kctx = ck_ref[0, 0, hh]
        vctx = cv_ref[0, 0, hh]
        for r in range(NA_ROWS):
            rs = min(max(r - NA_KH // 2, 0), NA_ROWS - NA_KH)
            q = q_ref[0, r * GRID_W:(r + 1) * GRID_W, sl]
            k = k_ref[0, rs * GRID_W:rs * GRID_W + nloc, sl]
            v = v_ref[0, rs * GRID_W:rs * GRID_W + nloc, sl]
            off = (rs - r + NA_WIN_H - 1) * GRID_W
            s_loc = _dot1(q, k, NT) * NA_SCALE + bias_ref[hh, :, off:off + nloc]
            s_loc = jnp.where(col_in, s_loc, NEG_INF)
            s_ctx = _dot1(q, kctx, NT) * NA_SCALE
            m = jnp.maximum(jnp.max(s_loc, axis=-1, keepdims=True), jnp.max(s_ctx, axis=-1, keepdims=True))
            e_loc = jnp.exp(s_loc - m)
            e_ctx = jnp.exp(s_ctx - m)
            den = jnp.sum(e_loc, axis=-1, keepdims=True) + jnp.sum(e_ctx, axis=-1, keepdims=True)
            o = (_dot1(e_loc, v) + _dot1(e_ctx, vctx)) / den
            o_ref[0, r * GRID_W:(r + 1) * GRID_W, sl] = o.astype(BF16)


def _na_bias_table(rpb):
    qc = np.arange(GRID_W)[:, None]
    kc = np.arange(GRID_W)[None, :]
    dc = np.clip(kc - qc, -(NA_WIN_W - 1), NA_WIN_W - 1) + (NA_WIN_W - 1)
    t = rpb[:, :, :, dc]
    return jnp.transpose(t, (0, 1, 3, 2, 4)).reshape(DEPTH, NA_HEADS, GRID_W, NA_NDR * GRID_W)


def _na_nbr(st, z3, cache_k, cache_v, bias_tab, l):
    blk = lambda col: pl.BlockSpec((1, st.L, LANE), lambda b, p: (b, 0, col + p))
    cspec = pl.BlockSpec((1, 1, 2, PAST_LEN, NA_HEAD), lambda b, p: (b, l, p, 0, 0))
    return pl.pallas_call(
        _na_nbr_kernel,
        grid=(st.nb, NA_HEADS // 2),
        in_specs=[blk(NA_QCOL), blk(NA_KCOL), blk(NA_VCOL), cspec, cspec,
                  pl.BlockSpec((None, 2, GRID_W, NA_NDR * GRID_W), lambda b, p: (l, p, 0, 0))],
        out_specs=pl.BlockSpec((1, st.L, LANE), lambda b, p: (b, 0, p)),
        out_shape=jax.ShapeDtypeStruct((st.nb, st.L, D_GROUP), BF16),
        compiler_params=_cparams("parallel", "parallel"),
        name="na_neighbourhood",
    )(z3, z3, z3, cache_k, cache_v, bias_tab)


ML_COL = ML_OFF // LANE
ML_SCALE = ML_HEAD ** -0.5


def _log_sigmoid(x):
    return -_softplus(-x)


def _rope_tables(L):
    half = ML_HEAD // 2
    quarter = half // 2
    t = np.arange(L)
    inv_freq = ROPE_BASE ** (-np.arange(quarter, dtype=np.float32) / quarter)

    def tabs(pos):
        ang = pos.astype(np.float32)[:, None] * inv_freq[None, :].astype(np.float32)
        c, s = np.cos(ang), np.sin(ang)
        return np.concatenate([c, c], axis=-1), np.concatenate([-s, s], axis=-1)

    c1, s1 = tabs(t // GRID_W)
    c2, s2 = tabs(t % GRID_W)
    return (jnp.asarray(np.concatenate([c1, c2], axis=-1), F32),
            jnp.asarray(np.concatenate([s1, s2], axis=-1), F32))


def _ml_kernel(*refs, L, rotary, layer):
    nh = ML_HEADS
    ib_ref, fb_ref = refs[0], refs[1]
    q_refs, k_refs, v_refs, o_refs = (refs[2 + i * nh:2 + (i + 1) * nh] for i in range(4))
    (g_ref, cos_ref, sin_ref, c0_ref, n0_ref, m0_ref, lng_ref, lnb_ref,
     y_ref, cf_ref, nf_ref, mf_ref, qs, ks, hf, hb, c_scr, n_scr, m_scr) = refs[2 + 4 * nh:]
    T = ML_CHUNK
    nc = L // T

    if rotary:
        first = (_iota((L, ML_HEAD), 1) % (ML_HEAD // 2)) < ML_HEAD // 4

        def rope(x):
            quarter = ML_HEAD // 4
            partner = jnp.where(first, pltpu.roll(x, ML_HEAD - quarter, axis=1), pltpu.roll(x, quarter, axis=1))
            return x * cos_ref[...] + partner * sin_ref[...]
    else:
        rope = lambda x: x
    for h in range(nh):
        qs[h] = rope(q_refs[h][0]) * ML_SCALE
        ks[h] = rope(k_refs[h][0])
        for d in range(2):
            c_scr[d, h] = c0_ref[0, d, h]
            n_scr[d, h] = n0_ref[0, h, d:d + 1, :]
            m_scr[d, h] = m0_ref[0, h, d:d + 1, :]

    tj = _iota((T, T), 0)
    ts = _iota((T, T), 1)
    ones = jnp.ones((T, T), F32)
    upto_row = (tj <= ts, tj >= ts)
    upto_col = (ts <= tj, ts >= tj)

    def body(ci, carry):
        ch = []
        for d in range(2):
            cd = ci if d == 0 else nc - 1 - ci
            rows = pl.ds(pl.multiple_of(cd * T, T), T)
            for h in range(nh):
                g = g_ref[0, h, cd]
                ig = g[:, d:d + 1] + ib_ref[layer * 2 * nh + d * nh + h]
                fg = g[:, 2 + d:3 + d] + fb_ref[layer * 2 * nh + d * nh + h]
                lf = jnp.broadcast_to(_log_sigmoid(fg), (T, T))
                igb = jnp.broadcast_to(ig, (T, T))
                q = dict(d=d, h=h, rows=rows, ig=ig, qc=qs[h, rows, :], kc=ks[h, rows, :], vc=v_refs[h][0, rows, :])
                q['bcol'] = _dot_sel_l(jnp.where(upto_col[d], 1.0, 0.0), lf)
                q['brow_i'] = _dot_sel_l(ones, jnp.where(upto_row[d], lf, 0.0) - jnp.where(tj == ts, igb, 0.0))
                ch.append(q)
        for q in ch:
            q['qk'] = _dot1(q['qc'], q['kc'], NT)
            q['c_old'] = c_scr[q['d'], q['h']]
            q['qc_c'] = _dot1(q['qc'], q['c_old'])
        for q in ch:
            d, h = q['d'], q['h']
            dmat = jnp.where(upto_col[d], q['bcol'] - q['brow_i'], -jnp.inf)
            b1 = q['bcol'][:, 0:1]
            m_prev = m_scr[d, h]
            inter = b1 + m_prev
            m_t = jnp.maximum(inter, jnp.max(dmat, axis=-1, keepdims=True))
            qk = q['qk'] * jnp.exp(dmat - m_t)
            w_inter = jnp.exp(inter - m_t)
            n_old = n_scr[d, h]
            den = w_inter * jnp.sum(q['qc'] * n_old, axis=-1, keepdims=True) + jnp.sum(qk, axis=-1, keepdims=True)
            q['scale'] = 1.0 / jnp.maximum(jnp.abs(den), jnp.exp(-m_t))
            q['inter_part'] = w_inter * q['qc_c']
            q['intra'] = _dot1(qk, q['vc'])
            b_last = b1[T - 1:T, :] if d == 0 else b1[0:1, :]
            g_s = b_last - b1 + q['ig']
            m_new = jnp.maximum(b_last + m_prev, jnp.max(g_s, axis=0, keepdims=True))
            w_old = jnp.exp(b_last + m_prev - m_new)
            w_s = jnp.exp(g_s - m_new)
            q['c_new'] = _dot1(q['kc'], w_s * q['vc'], TN)
            q['w_old'] = w_old
            n_scr[d, h] = w_old * n_old + jnp.sum(w_s * q['kc'], axis=0, keepdims=True)
            m_scr[d, h] = m_new
        for q in ch:
            d, h = q['d'], q['h']
            hcur = (q['inter_part'] + q['intra']) * q['scale']
            if d == 0:
                hf[h, q['rows'], :] = hcur
            else:
                hb[h, q['rows'], :] = hcur
            c_scr[d, h] = q['w_old'] * q['c_old'] + q['c_new']
        return carry

    lax.fori_loop(0, nc, body, 0)

    for h in range(nh):
        hsum = hf[h] + hb[h]
        mu = jnp.mean(hsum, axis=-1, keepdims=True)
        dv = hsum - mu
        var = jnp.mean(dv * dv, axis=-1, keepdims=True)
        hn = dv * lax.rsqrt(var + ML_GN_EPS)
        y_ref[0, :, h * ML_HEAD:(h + 1) * ML_HEAD] = (
            _sigmoid(o_refs[h][0]) * (hn * lng_ref[0, :, h * ML_HEAD:(h + 1) * ML_HEAD]
                                      + lnb_ref[0, :, h * ML_HEAD:(h + 1) * ML_HEAD])).astype(BF16)
        for d in range(2):
            cf_ref[0, d, h] = c_scr[d, h]
            nf_ref[0, h, d:d + 1, :] = n_scr[d, h]
            mf_ref[0, h, d:d + 1, :] = m_scr[d, h]


def _ml_mixer(st, z3, zg, i_bias, f_bias, ln_g, ln_b, c0, n0, m0, l):
    L, nb, nh = st.L, st.nb, ML_HEADS
    nc = L // ML_CHUNK
    gates = zg.reshape(nb, L, 2, 2, nh).transpose(0, 4, 1, 2, 3).reshape(nb, nh, nc, ML_CHUNK, 4)
    cos_t, sin_t = _rope_tables(L)
    blk = lambda col: pl.BlockSpec((1, L, LANE), lambda b: (b, 0, col))
    head_blocks = [blk(ML_COL + part * nh + h) for part in range(4) for h in range(nh)]
    smem = pl.BlockSpec(memory_space=pltpu.SMEM)
    tab = pl.BlockSpec((L, ML_HEAD), lambda b: (0, 0))
    c_spec = pl.BlockSpec((1, 2, nh, ML_HEAD, ML_HEAD), lambda b: (b, 0, 0, 0, 0))
    n_spec = pl.BlockSpec((1, nh, 2, ML_HEAD), lambda b: (b, 0, 0, 0))
    m_spec = pl.BlockSpec((1, nh, 2, 1), lambda b: (b, 0, 0, 0))
    par = pl.BlockSpec((1, 1, D_GROUP), lambda b: (l, 0, 0))
    return pl.pallas_call(
        functools.partial(_ml_kernel, L=L, rotary=st.latent, layer=l),
        grid=(nb,),
        in_specs=[smem, smem] + head_blocks + [
            pl.BlockSpec((1, nh, nc, ML_CHUNK, 4), lambda b: (b, 0, 0, 0, 0)),
            tab, tab, c_spec, n_spec, m_spec, par, par],
        out_specs=[pl.BlockSpec((1, L, D_GROUP), lambda b: (b, 0, 0)), c_spec, n_spec, m_spec],
        out_shape=[jax.ShapeDtypeStruct((nb, L, D_GROUP), BF16),
                   jax.ShapeDtypeStruct((nb, 2, nh, ML_HEAD, ML_HEAD), F32),
                   jax.ShapeDtypeStruct((nb, nh, 2, ML_HEAD), F32),
                   jax.ShapeDtypeStruct((nb, nh, 2, 1), F32)],
        scratch_shapes=[pltpu.VMEM((nh, L, ML_HEAD), F32), pltpu.VMEM((nh, L, ML_HEAD), F32),
                        pltpu.VMEM((nh, L, ML_HEAD), F32), pltpu.VMEM((nh, L, ML_HEAD), F32),
                        pltpu.VMEM((2, nh, ML_HEAD, ML_HEAD), F32), pltpu.VMEM((2, nh, 1, ML_HEAD), F32),
                        pltpu.VMEM((2, nh, 1, 1), F32)],
        compiler_params=_cparams("parallel"),
        name="mlstm",
    )(i_bias.reshape(-1), f_bias.reshape(-1), *([z3] * (4 * nh)), gates, cos_t, sin_t, c0, n0, m0,
      ln_g.reshape(DEPTH, 1, D_GROUP), ln_b.reshape(DEPTH, 1, D_GROUP))


RW_TL = 256
RW_T = 64
RW_RCOL = RW_OFF // D_GROUP
RW_LCOL = (RW_OFF + 3 * D_GROUP) // (2 * LANE)
N_LORA = RW_LORA_W + RW_LORA_A + RW_LORA_G


def _head_ones():
    return jnp.asarray(np.kron(np.eye(RW_HEADS, dtype=np.float32), np.ones((RW_HEAD, RW_HEAD), np.float32)))


def _rw_front_kernel(r_ref, k_ref, v_ref, lo_ref, rp_ref, kp_ref, vp_ref, lp_ref, rn_ref, kn_ref, vn_ref, ln_ref,
                     mup_ref, mun_ref, w0_ref, w2_ref, a0_ref, a2_ref, g2_ref, kk_ref, ka_ref, hones_ref,
                     ro_ref, vo_ref, kko_ref, go_ref, kd_ref, bd_ref, lw_ref):
    t = pl.program_id(1)
    first = t == 0
    last = t == pl.num_programs(1) - 1
    row = _iota((RW_TL, 1), 0)

    def shift(x_ref, p_ref, n_ref, lo, hi):
        x = x_ref[0]
        prev_edge = jnp.where(first, 0.0, p_ref[0, SUB - 1:SUB, :])
        next_edge = jnp.where(last, 0.0, n_ref[0, 0:1, :])
        prev = jnp.where(row == 0, prev_edge, pltpu.roll(x, 1, axis=0))
        nxt = jnp.where(row == RW_TL - 1, next_edge, pltpu.roll(x, RW_TL - 1, axis=0))
        return x + mup_ref[0][:, lo:hi] * (prev - x) + mun_ref[0][:, lo:hi] * (nxt - x)

    g = D_GROUP
    r = shift(r_ref, rp_ref, rn_ref, 0, g)
    k = shift(k_ref, kp_ref, kn_ref, g, 2 * g)
    v = shift(v_ref, vp_ref, vn_ref, 2 * g, 3 * g)
    lo = shift(lo_ref, lp_ref, ln_ref, 3 * g, 3 * g + N_LORA)
    zw = lo[:, :RW_LORA_W]
    za = lo[:, RW_LORA_W:RW_LORA_W + RW_LORA_A]
    zg = lo[:, RW_LORA_W + RW_LORA_A:]
    ro_ref[0] = r
    vo_ref[0] = v
    kk = k * kk_ref[0]
    ssq = _dot_sel_r(kk * kk, hones_ref[...])
    kk = kk * lax.rsqrt(ssq + 1e-12)
    kko_ref[0] = kk
    go_ref[0] = _dot3(_sigmoid(zg), g2_ref[0])
    tw = jnp.tanh(zw)
    for d in range(2):
        w_log = -_softplus(-(w0_ref[0, d:d + 1, :] + _dot3(tw, w2_ref[0, d]))) - 0.5
        lw_ref[d, 0] = -jnp.exp(w_log)
        a = _sigmoid(a0_ref[0, d:d + 1, :] + _dot3(za, a2_ref[0, d]))
        kd_ref[d, 0] = k * (1.0 + (a - 1.0) * ka_ref[0])
        bd_ref[d, 0] = kk * a


def _rw_front(st, z3, mu_prev, mu_next, w0, w2, a0, a2, g2, k_k, k_a, l):
    nb, L = st.nb, st.L
    nt = L // RW_TL
    tpb = RW_TL // SUB
    main = lambda w, col: pl.BlockSpec((1, RW_TL, w), lambda b, t: (b, t, col))
    prev = lambda w, col: pl.BlockSpec((1, SUB, w), lambda b, t: (b, jnp.maximum(t * tpb - 1, 0), col))
    nxt = lambda w, col: pl.BlockSpec((1, SUB, w), lambda b, t: (b, jnp.minimum((t + 1) * tpb, L // SUB - 1), col))
    cols = [(D_GROUP, RW_RCOL), (D_GROUP, RW_RCOL + 1), (D_GROUP, RW_RCOL + 2), (2 * LANE, RW_LCOL)]
    lay = lambda *shape: pl.BlockSpec((1,) + shape, lambda b, t: (l,) + (0,) * len(shape))
    out1 = pl.BlockSpec((1, RW_TL, D_GROUP), lambda b, t: (b, t, 0))
    out2 = pl.BlockSpec((2, 1, RW_TL, D_GROUP), lambda b, t: (0, b, t, 0))
    s1 = jax.ShapeDtypeStruct((nb, L, D_GROUP), F32)
    s2 = jax.ShapeDtypeStruct((2, nb, L, D_GROUP), F32)
    return pl.pallas_call(
        _rw_front_kernel,
        grid=(nb, nt),
        in_specs=[main(*c) for c in cols] + [prev(*c) for c in cols] + [nxt(*c) for c in cols] + [
            lay(1, RW_IN), lay(1, RW_IN), lay(2, D_GROUP), lay(2, RW_LORA_W, D_GROUP), lay(2, D_GROUP),
            lay(2, RW_LORA_A, D_GROUP), lay(RW_LORA_G, D_GROUP), lay(1, D_GROUP), lay(1, D_GROUP),
            pl.BlockSpec((D_GROUP, D_GROUP), lambda b, t: (0, 0))],
        out_specs=[out1, out1, out1, out1, out2, out2, out2],
        out_shape=[s1, s1, s1, s1, s2, s2, s2],
        compiler_params=_cparams("parallel", "parallel"),
        name="rwkv_front",
    )(*([z3] * 12), mu_prev.reshape(DEPTH, 1, RW_IN), mu_next.reshape(DEPTH, 1, RW_IN), w0, w2, a0, a2, g2,
      k_k.reshape(DEPTH, 1, D_GROUP), k_a.reshape(DEPTH, 1, D_GROUP), _head_ones())


def _rw_core_kernel(rf_ref, vf_ref, kkf_ref, kdf_ref, bdf_ref, lwf_ref,
                    rb_ref, vb_ref, kkb_ref, kdb_ref, bdb_ref, lwb_ref, s0_ref,
                    yf_ref, yb_ref, sf_ref, s_scr):
    c = pl.program_id(1)
    T = RW_T

    @pl.when(c == 0)
    def _():
        s_scr[...] = s0_ref[0]

    tj = _iota((T, T), 0)
    ts = _iota((T, T), 1)
    dirs = ((rf_ref, vf_ref, kkf_ref, kdf_ref, bdf_ref, lwf_ref, yf_ref),
            (rb_ref, vb_ref, kkb_ref, kdb_ref, bdb_ref, lwb_ref, yb_ref))
    ch = []
    for d, (r_ref, v_ref, kk_ref, kd_ref, bd_ref, lw_ref, y_ref) in enumerate(dirs):
        if d == 0:
            incl = ts <= tj
            strict = ts < tj
            last = T - 1
        else:
            incl = ts >= tj
            strict = ts > tj
            last = 0
        lw = lw_ref[0, 0]
        cum = _dot_sel_l(jnp.where(incl, 1.0, 0.0), lw)
        w_in = jnp.exp(cum)
        w_inv = jnp.exp(-cum)
        w_ex = jnp.exp(cum - lw)
        kap_a = kk_ref[0] * w_ex
        bet_a = bd_ref[0, 0] * w_inv
        khat_a = kd_ref[0, 0] * w_inv
        rho_a = r_ref[0] * w_in
        w_tot = w_in[last:last + 1, :]
        v_a = v_ref[0]
        for h in range(RW_HEADS):
            sl = slice(h * RW_HEAD, (h + 1) * RW_HEAD)
            ch.append(dict(d=d, h=h, sl=sl, incl=incl, strict=strict, y_ref=y_ref, w_tot=w_tot[:, sl],
                           kap=kap_a[:, sl], bet=bet_a[:, sl], khat=khat_a[:, sl], rho=rho_a[:, sl], v=v_a[:, sl]))
    for q in ch:
        kr = jnp.concatenate([q['kap'], q['rho']], axis=0)
        bk = jnp.concatenate([q['bet'], q['khat']], axis=0)
        q['gram'] = _dot1(kr, bk, NT)
    for q in ch:
        gram = q.pop('gram')
        q['n'] = -jnp.where(q['strict'], gram[:T, :T], 0.0)
        l_k = jnp.where(q['strict'], gram[:T, T:], 0.0)
        q['m_b'] = jnp.where(q['incl'], gram[T:, :T], 0.0)
        m_k = jnp.where(q['incl'], gram[T:, T:], 0.0)
        q['lmv'] = _dot1(jnp.concatenate([l_k, m_k], axis=0), q['v'])
    for q in ch:
        q['x'] = jnp.concatenate([q['kap'], q['lmv'][:T]], axis=1)
    for lvl in range(6):
        for q in ch:
            if lvl < 5:
                nx = _dot1(q['n'], jnp.concatenate([q['n'], q['x']], axis=1))
                q['n'], q['x'] = nx[:, :T], q['x'] + nx[:, T:]
            else:
                q['x'] = q['x'] + _dot1(q['n'], q['x'])
    for q in ch:
        q['p'] = _dot1(q['m_b'], q['x'])
        xb = _dot1(q['x'], q['bet'], TN)
        q['a_m'] = xb[:RW_HEAD]
        q['d_m'] = _dot1(q['v'], q['khat'], TN) - xb[RW_HEAD:]
    for q in ch:
        rho_p = q['rho'] - q['p'][:, :RW_HEAD]
        y_v = q['lmv'][T:] - q['p'][:, RW_HEAD:]
        s_old = s_scr[q['d'], q['h']]
        q['y_ref'][0, :, q['sl']] = _dot1(rho_p, s_old, NT) + y_v
        s_scr[q['d'], q['h']] = (s_old - _dot3(s_old, q['a_m']) + q['d_m']) * q['w_tot']

    @pl.when(c == pl.num_programs(1) - 1)
    def _():
        sf_ref[0] = s_scr[...]


def _rw_core(st, r, v, kk, kd, bd, lw, s0):
    nb, L = st.nb, st.L
    nc = L // RW_T
    f1 = pl.BlockSpec((1, RW_T, D_GROUP), lambda b, c: (b, c, 0))
    b1 = pl.BlockSpec((1, RW_T, D_GROUP), lambda b, c: (b, nc - 1 - c, 0))
    f2 = pl.BlockSpec((1, 1, RW_T, D_GROUP), lambda b, c: (0, b, c, 0))
    b2 = pl.BlockSpec((1, 1, RW_T, D_GROUP), lambda b, c: (1, b, nc - 1 - c, 0))
    s_spec = pl.BlockSpec((1, 2, RW_HEADS, RW_HEAD, RW_HEAD), lambda b, c: (b, 0, 0, 0, 0))
    ys = jax.ShapeDtypeStruct((nb, L, D_GROUP), F32)
    return pl.pallas_call(
        _rw_core_kernel,
        grid=(nb, nc),
        in_specs=[f1, f1, f1, f2, f2, f2, b1, b1, b1, b2, b2, b2, s_spec],
        out_specs=[f1, b1, s_spec],
        out_shape=[ys, ys, jax.ShapeDtypeStruct((nb, 2, RW_HEADS, RW_HEAD, RW_HEAD), F32)],
        scratch_shapes=[pltpu.VMEM((2, RW_HEADS, RW_HEAD, RW_HEAD), F32)],
        compiler_params=_cparams("parallel", "arbitrary"),
        name="rwkv_core",
    )(r, v, kk, kd, bd, lw, r, v, kk, kd, bd, lw, s0)


RWO_TM = 512


def _rw_out_kernel(yf_ref, yb_ref, r_ref, v_ref, kd0_ref, kd1_ref, g_ref, lng_ref, lnb_ref, rk_ref, hones_ref, o_ref):
    y = yf_ref[...] + yb_ref[...]
    inv = 1.0 / RW_HEAD
    mu = _dot_sel_r(y, hones_ref[...]) * inv
    dv = y - mu
    var = _dot_sel_r(dv * dv, hones_ref[...]) * inv
    yn = dv * lax.rsqrt(var + RW_GN_EPS) * lng_ref[0] + lnb_ref[0]
    kmean = 0.5 * (kd0_ref[0] + kd1_ref[0])
    bonus = _dot_sel_r(r_ref[...] * kmean * rk_ref[0], hones_ref[...]) * v_ref[...]
    o_ref[...] = ((yn + bonus) * g_ref[...]).astype(BF16)


def _rw_out(st, yf, yb, r, v, kd, g, ln_g, ln_b, r_k, l):
    rows = st.rows
    flat = lambda a: a.reshape(rows, D_GROUP)
    row = pl.BlockSpec((RWO_TM, D_GROUP), lambda i: (i, 0))
    lay = pl.BlockSpec((1, 1, D_GROUP), lambda i: (l, 0, 0))
    kd2 = kd.reshape(2, rows, D_GROUP)
    return pl.pallas_call(
        _rw_out_kernel,
        grid=(rows // RWO_TM,),
        in_specs=[row, row, row, row,
                  pl.BlockSpec((1, RWO_TM, D_GROUP), lambda i: (0, i, 0)),
                  pl.BlockSpec((1, RWO_TM, D_GROUP), lambda i: (1, i, 0)),
                  row, lay, lay, lay, pl.BlockSpec((D_GROUP, D_GROUP), lambda i: (0, 0))],
        out_specs=row,
        out_shape=jax.ShapeDtypeStruct((rows, D_GROUP), BF16),
        compiler_params=_cparams("parallel"),
        name="rwkv_out",
    )(flat(yf), flat(yb), flat(r), flat(v), kd2, kd2, flat(g), ln_g.reshape(DEPTH, 1, D_GROUP),
      ln_b.reshape(DEPTH, 1, D_GROUP), r_k.reshape(DEPTH, 1, D_GROUP), _head_ones())


def _trunk_layer(st, x, l, mod4, p, states, ctx_kv):
    nb, L = st.nb, st.L
    z, zg = _zproj(st, x, mod4, p['w_in'], p['w_gate_cols'], l)
    z3 = z.reshape(nb, L, D_Z)

    ydir, sfr, sfi = _s5_scan(st, z3, p['s5_bq'], p['s5_cq'], p['s5_lamr'], p['s5_lami'],
                              states['s5_re'], states['s5_im'], l)
    y_s5 = _s5_out(st, z, ydir.reshape(2, st.rows, D_GROUP), p['s5_d'], p['s5_w_glu'], l)

    r, v, kk, g, kd, bd, lw = _rw_front(st, z3, p['rw_mu_prev'], p['rw_mu_next'], p['rw_w0'], p['rw_w2'],
                                        p['rw_a0'], p['rw_a2'], p['rw_g2'], p['rw_k_k'], p['rw_k_a'], l)
    yf, yb, rw_s = _rw_core(st, r, v, kk, kd, bd, lw, states['rw'])
    y_rw = _rw_out(st, yf, yb, r, v, kd, g, p['rw_ln_g'], p['rw_ln_b'], p['rw_r_k'], l)

    if ctx_kv is None:
        y_na, nk, nv = _na_ctx(st, z3)
    else:
        y_na = _na_nbr(st, z3, ctx_kv[0], ctx_kv[1], p['na_bias'], l)
        nk = nv = None

    y_ml, ml_c, ml_n, ml_m = _ml_mixer(st, z3, zg, p['ml_i_bias'], p['ml_f_bias'], p['ml_ln_g'], p['ml_ln_b'],
                                       states['ml_c'], states['ml_n'], states['ml_m'], l)

    ys = (y_s5, y_rw, y_na.reshape(st.rows, D_GROUP), y_ml.reshape(st.rows, D_GROUP))
    tail = _oproj(st, ys, x, mod4, p['w_out'], p['ln1_g'], p['ln1_b'], p['w_router'], p['b_router'], l)
    return tail, (nk, nv, sfr, sfi, rw_s, ml_c, ml_n, ml_m)


def _moe_and_ln2(tails, mod4, p, l):
    h2_all = jnp.concatenate([t[1] for t in tails], axis=0)
    gates = jnp.concatenate([t[2] for t in tails], axis=0)
    keep = jnp.concatenate([t[3] for t in tails], axis=0)
    y01, w01 = _moe_routed(h2_all, keep, gates, p['moe_w_gate'], p['moe_w_up'], p['moe_w_down'], l)
    xs, off = [], 0
    for st, t in zip((PROMPT, LATENT), tails):
        xs.append(_ln2(st, t[0], y01, w01, mod4, p['ln2_g'], p['ln2_b'], l, off))
        off += st.rows
    return xs


def kernel(x_prompt, x_sample, cache_nat_k, cache_nat_v, state_s5_re, state_s5_im, state_rwkv, state_mlstm_c, state_mlstm_n, state_mlstm_m, c, c_ctx, w_mod, b_mod, w_in, w_out, s5_lam_re, s5_lam_im, s5_log_step, s5_b_re, s5_b_im, s5_c_re, s5_c_im, s5_d, s5_w_glu, rw_mu_prev, rw_mu_next, rw_w0, rw_w2, rw_a0, rw_a2, rw_g2, rw_k_k, rw_k_a, rw_r_k, rw_ln_g, rw_ln_b, na_rpb, ml_i_bias, ml_f_bias, ml_ln_g, ml_ln_b, ln1_g, ln1_b, ln2_g, ln2_b, w_router, b_router, moe_w_gate, moe_w_up, moe_w_down):
    dt = x_prompt.dtype
    cond = jnp.concatenate([c_ctx[None, :], c, jnp.zeros((MOD_ROWS - 1 - DEC_BATCH, D_MODEL), F32)], axis=0)
    mod4 = _modulation(cond, w_mod, b_mod).reshape(DEPTH, MOD_ROWS, 1, 6 * D_MODEL)

    lbr, lbi, bbr, bbi = _s5_prep(s5_lam_re, s5_lam_im, s5_log_step, s5_b_re, s5_b_im)
    s5_bq, s5_cq, s5_lamr, s5_lami = _s5_block_params(lbr, lbi, bbr, bbi, s5_c_re, s5_c_im)
    p = dict(w_in=w_in, w_gate_cols=w_in[:, :, D_Z:], w_out=w_out,
             s5_bq=s5_bq, s5_cq=s5_cq, s5_lamr=s5_lamr, s5_lami=s5_lami, s5_d=s5_d, s5_w_glu=s5_w_glu,
             rw_mu_prev=rw_mu_prev, rw_mu_next=rw_mu_next, rw_w0=rw_w0, rw_w2=rw_w2, rw_a0=rw_a0, rw_a2=rw_a2,
             rw_g2=rw_g2, rw_k_k=rw_k_k, rw_k_a=rw_k_a, rw_r_k=rw_r_k, rw_ln_g=rw_ln_g, rw_ln_b=rw_ln_b,
             na_bias=_na_bias_table(na_rpb), ml_i_bias=ml_i_bias, ml_f_bias=ml_f_bias, ml_ln_g=ml_ln_g,
             ml_ln_b=ml_ln_b, ln1_g=ln1_g, ln1_b=ln1_b, ln2_g=ln2_g, ln2_b=ln2_b, w_router=w_router,
             b_router=b_router, moe_w_gate=moe_w_gate, moe_w_up=moe_w_up, moe_w_down=moe_w_down)

    gp = BATCH // S5_SEQS
    zero_states = dict(
        s5_re=jnp.zeros((gp, 2, S5_SEQS, N_S5), F32), s5_im=jnp.zeros((gp, 2, S5_SEQS, N_S5), F32),
        rw=jnp.zeros((BATCH, 2, RW_HEADS, RW_HEAD, RW_HEAD), F32),
        ml_c=jnp.zeros((BATCH, 2, ML_HEADS, ML_HEAD, ML_HEAD), F32),
        ml_n=jnp.zeros((BATCH, ML_HEADS, 2, ML_HEAD), F32), ml_m=jnp.zeros((BATCH, ML_HEADS, 2, 1), F32))

    xp = x_prompt.reshape(PROMPT.rows, D_MODEL)
    xs = x_sample.reshape(LATENT.rows, D_MODEL)
    outs = [[] for _ in range(8)]
    for l in range(DEPTH):
        tail_p, ctx_t = _trunk_layer(PROMPT, xp, l, mod4, p, zero_states, None)
        for acc, t in zip(outs, ctx_t):
            acc.append(t)
        lat_states = dict(
            s5_re=state_s5_re[:, l].reshape(DEC_BATCH, 2, N_S5).transpose(1, 0, 2)[None],
            s5_im=state_s5_im[:, l].reshape(DEC_BATCH, 2, N_S5).transpose(1, 0, 2)[None],
            rw=state_rwkv[:, l], ml_c=state_mlstm_c[:, l],
            ml_n=state_mlstm_n[:, l].transpose(0, 2, 1, 3), ml_m=state_mlstm_m[:, l].transpose(0, 2, 1)[..., None])
        tail_s, _ = _trunk_layer(LATENT, xs, l, mod4, p, lat_states, (cache_nat_k, cache_nat_v))
        xp, xs = _moe_and_ln2((tail_p, tail_s), mod4, p, l)

    nk, nv, s5r, s5i, rw, mc, mn, mm = [jnp.stack(t, axis=1) for t in outs]

    def s5_state(t):
        return t.transpose(0, 3, 1, 2, 4).reshape(BATCH, DEPTH, 2, S5_GROUPS, S5_STATE)

    return (xp.reshape(BATCH, SEQ, D_MODEL), xs.reshape(DEC_BATCH, DEC_SEQ, D_MODEL),
            nk, nv, s5_state(s5r).astype(dt), s5_state(s5i).astype(dt), rw.astype(dt), mc.astype(dt),
            mn.transpose(0, 1, 3, 2, 4).astype(dt), mm[..., 0].transpose(0, 1, 3, 2).astype(dt))
```

```python
import functools
import math

import numpy as np
import jax
import jax.numpy as jnp
from jax import lax
from jax.experimental import pallas as pl
from jax.experimental.pallas import tpu as pltpu

F32 = jnp.float32
BF16 = jnp.bfloat16

D_MODEL = 2048
BATCH = 16
SEQ = 256
DEPTH = 4
DEC_BATCH = 8
DEC_SEQ = 1024
PAST_LEN = 256
GRID_W = 64
D_GROUP = D_MODEL // 4
S5_CH = 16
S5_GROUPS = D_GROUP // S5_CH
S5_STATE = 64
RW_HEAD = 64
RW_HEADS = D_GROUP // RW_HEAD
RW_LORA_W = 64
RW_LORA_A = 64
RW_LORA_G = 128
RW_GN_EPS = 64e-5
NA_HEAD = 64
NA_HEADS = D_GROUP // NA_HEAD
NA_WIN_H = 8
NA_WIN_W = 16
NEG_INF = -1e30
ML_HEAD = 128
ML_HEADS = D_GROUP // ML_HEAD
ML_CHUNK = 64
ML_GN_EPS = 1e-5
ROPE_BASE = 10000.0
N_EXPERTS = 16
N_EXPERT_GROUPS = 4
EXPERTS_PER_GROUP = N_EXPERTS // N_EXPERT_GROUPS
D_EXPERT = 512
DEEPNORM_ALPHA = (2 * DEPTH) ** 0.25
LN_EPS = 1e-5
S5_IN = D_GROUP
RW_IN = 3 * D_GROUP + RW_LORA_W + RW_LORA_A + RW_LORA_G
NA_IN = 3 * D_GROUP
ML_IN = 4 * D_GROUP + 4 * ML_HEADS
D_IN = S5_IN + RW_IN + NA_IN + ML_IN
N_GATE = 4 * ML_HEADS
D_Z = D_IN - N_GATE
RW_OFF = S5_IN
NA_OFF = S5_IN + RW_IN
ML_OFF = NA_OFF + NA_IN
MOD_ROWS = 16
LANE = 128
SUB = 8

VMEM_LIMIT = 56 * 1024 * 1024


def _cparams(*sem):
    return pltpu.CompilerParams(dimension_semantics=sem, vmem_limit_bytes=VMEM_LIMIT)


def _dg(a, b, dims):
    return lax.dot_general(a, b, (dims, ((), ())), preferred_element_type=F32)


NN = ((1,), (0,))
NT = ((1,), (1,))
TN = ((0,), (0,))


def _dot1(a, b, dims=NN):
    return _dg(a.astype(BF16), b.astype(BF16), dims)


def _split(x):
    hi = x.astype(BF16)
    lo = (x - hi.astype(F32)).astype(BF16)
    return hi, lo


def _split3(x):
    hi = x.astype(BF16)
    r = x - hi.astype(F32)
    mid = r.astype(BF16)
    lo = (r - mid.astype(F32)).astype(BF16)
    return hi, mid, lo


def _dot3(a, b, dims=NN):
    ah, al = _split(a)
    bh, bl = _split(b)
    return _dg(ah, bh, dims) + (_dg(ah, bl, dims) + _dg(al, bh, dims))


def _dot_sel_l(sel, x, dims=NN):
    s = sel.astype(BF16)
    hi, mid, lo = _split3(x)
    return _dg(s, hi, dims) + (_dg(s, mid, dims) + _dg(s, lo, dims))


def _dot_sel_r(x, sel, dims=NN):
    s = sel.astype(BF16)
    hi, mid, lo = _split3(x)
    return _dg(hi, s, dims) + (_dg(mid, s, dims) + _dg(lo, s, dims))


def _sigmoid(x):
    return 1.0 / (1.0 + jnp.exp(-x))


def _silu(x):
    return x * _sigmoid(x)


def _softplus(x):
    return jnp.maximum(x, 0.0) + jnp.log(1.0 + jnp.exp(-jnp.abs(x)))


def _iota(shape, dim):
    return lax.broadcasted_iota(jnp.int32, shape, dim)


class _Stream:
    def __init__(self, nb, L, latent):
        self.nb, self.L, self.latent = nb, L, latent
        self.rows = nb * L

    def mod_row(self, tile, tile_rows):
        if not self.latent:
            return 0
        return 1 + (tile * tile_rows) // self.L


PROMPT = _Stream(BATCH, SEQ, False)
LATENT = _Stream(DEC_BATCH, DEC_SEQ, True)


MOD_TN = 768


def _mod_kernel(cond_ref, w_ref, b_ref, o_ref):
    c = _silu(cond_ref[...])
    o_ref[0] = _dot3(c, w_ref[0]) + b_ref[0]


def _modulation(cond, w_mod, b_mod):
    n = 6 * D_MODEL
    return pl.pallas_call(
        _mod_kernel,
        grid=(DEPTH, n // MOD_TN),
        in_specs=[pl.BlockSpec((MOD_ROWS, D_MODEL), lambda l, j: (0, 0)),
                  pl.BlockSpec((1, D_MODEL, MOD_TN), lambda l, j: (l, 0, j)),
                  pl.BlockSpec((1, 1, MOD_TN), lambda l, j: (l, 0, j))],
        out_specs=pl.BlockSpec((1, MOD_ROWS, MOD_TN), lambda l, j: (l, 0, j)),
        out_shape=jax.ShapeDtypeStruct((DEPTH, MOD_ROWS, n), F32),
        compiler_params=_cparams("parallel", "parallel"),
        name="modulation",
    )(cond, w_mod, b_mod.reshape(DEPTH, 1, n))


ZP_TM = 1024
ZP_TN = 256


def _zproj_kernel(x_ref, mod_ref, w_ref, wg_ref, z_ref, zg_ref, h_scr):
    j = pl.program_id(1)

    @pl.when(j == 0)
    def _():
        m = mod_ref[0, 0]
        shift1 = m[:, 0:D_MODEL]
        scale1 = m[:, D_MODEL:2 * D_MODEL]
        h = x_ref[...] * (1.0 + scale1) + shift1
        h_scr[...] = h.astype(BF16)
        zg_ref[...] = _dot3(h, wg_ref[0])

    z_ref[...] = jnp.dot(h_scr[...], w_ref[0].astype(BF16), preferred_element_type=F32)


def _zproj(st, x, mod4, w_in, w_gate_cols, l):
    return pl.pallas_call(
        _zproj_kernel,
        grid=(st.rows // ZP_TM, D_Z // ZP_TN),
        in_specs=[pl.BlockSpec((ZP_TM, D_MODEL), lambda i, j: (i, 0)),
                  pl.BlockSpec((1, 1, 1, 6 * D_MODEL), lambda i, j: (l, st.mod_row(i, ZP_TM), 0, 0)),
                  pl.BlockSpec((1, D_MODEL, ZP_TN), lambda i, j: (l, 0, j)),
                  pl.BlockSpec((1, D_MODEL, N_GATE), lambda i, j: (l, 0, 0))],
        out_specs=[pl.BlockSpec((ZP_TM, ZP_TN), lambda i, j: (i, j)),
                   pl.BlockSpec((ZP_TM, N_GATE), lambda i, j: (i, 0))],
        out_shape=[jax.ShapeDtypeStruct((st.rows, D_Z), F32),
                   jax.ShapeDtypeStruct((st.rows, N_GATE), F32)],
        scratch_shapes=[pltpu.VMEM((ZP_TM, D_MODEL), BF16)],
        compiler_params=_cparams("parallel", "arbitrary"),
        name="zproj",
    )(x, mod4, w_in, w_gate_cols)


OP_TM = 512
OP_TN = 512


def _layer_norm(v, g, b):
    mu = jnp.mean(v, axis=-1, keepdims=True)
    d = v - mu
    var = jnp.mean(d * d, axis=-1, keepdims=True)
    return d * lax.rsqrt(var + LN_EPS) * g + b


def _route(scores, b_router):
    sel = scores + b_router
    s = [sel[:, e:e + 1] for e in range(N_EXPERTS)]
    in_top2 = []
    for g in range(N_EXPERT_GROUPS):
        for i in range(EXPERTS_PER_GROUP):
            e = g * EXPERTS_PER_GROUP + i
            cnt = jnp.zeros_like(s[e])
            for jj in range(EXPERTS_PER_GROUP):
                if jj == i:
                    continue
                o = g * EXPERTS_PER_GROUP + jj
                beats = (s[o] > s[e]) if jj > i else (s[o] >= s[e])
                cnt = cnt + jnp.where(beats, 1.0, 0.0)
            in_top2.append(cnt < 2.0)
    grp = []
    for g in range(N_EXPERT_GROUPS):
        tot = jnp.zeros_like(s[0])
        for i in range(EXPERTS_PER_GROUP):
            e = g * EXPERTS_PER_GROUP + i
            tot = tot + jnp.where(in_top2[e], s[e], 0.0)
        grp.append(tot)
    lane = _iota(scores.shape, 1)
    keep = jnp.zeros(scores.shape, F32)
    for g in range(N_EXPERT_GROUPS):
        cnt = jnp.zeros_like(s[0])
        for o in range(N_EXPERT_GROUPS):
            if o == g:
                continue
            beats = (grp[o] > grp[g]) if o > g else (grp[o] >= grp[g])
            cnt = cnt + jnp.where(beats, 1.0, 0.0)
        best = cnt < 1.0
        for i in range(EXPERTS_PER_GROUP):
            e = g * EXPERTS_PER_GROUP + i
            on = jnp.where(best, jnp.where(in_top2[e], 1.0, 0.0), 0.0)
            keep = keep + jnp.where(lane == e, on, 0.0)
    picked = scores * keep
    return picked / jnp.sum(picked, axis=-1, keepdims=True), keep


def _oproj_kernel(y0_ref, y1_ref, y2_ref, y3_ref, w_ref, x_ref, mod_ref, g_ref, b_ref, wr_ref, br_ref,
                  x1_ref, h2_ref, gate_ref, keep_ref, acc_scr):
    j = pl.program_id(1)
    acc = jnp.zeros((OP_TM, OP_TN), F32)
    for k, yr in enumerate((y0_ref, y1_ref, y2_ref, y3_ref)):
        wk = w_ref[0, k * D_GROUP:(k + 1) * D_GROUP, :].astype(BF16)
        acc = acc + jnp.dot(yr[...], wk, preferred_element_type=F32)
    for jj in range(D_MODEL // OP_TN):
        @pl.when(j == jj)
        def _(jj=jj):
            acc_scr[:, jj * OP_TN:(jj + 1) * OP_TN] = acc

    @pl.when(j == D_MODEL // OP_TN - 1)
    def _():
        m = mod_ref[0, 0]
        gate1 = m[:, 2 * D_MODEL:3 * D_MODEL]
        shift2 = m[:, 3 * D_MODEL:4 * D_MODEL]
        scale2 = m[:, 4 * D_MODEL:5 * D_MODEL]
        x1 = _layer_norm(DEEPNORM_ALPHA * x_ref[...] + gate1 * acc_scr[...], g_ref[0], b_ref[0])
        x1_ref[...] = x1
        h2 = x1 * (1.0 + scale2) + shift2
        h2_ref[...] = h2.astype(BF16)
        scores = _sigmoid(_dot3(h2, wr_ref[...]))
        gate_ref[...], keep_ref[...] = _route(scores, br_ref[...])


def _oproj(st, ys, x, mod4, w_out, ln_g, ln_b, w_router, b_router, l):
    row = lambda i, j: (i, 0)
    return pl.pallas_call(
        _oproj_kernel,
        grid=(st.rows // OP_TM, D_MODEL // OP_TN),
        in_specs=[pl.BlockSpec((OP_TM, D_GROUP), row)] * 4 + [
            pl.BlockSpec((1, D_MODEL, OP_TN), lambda i, j: (l, 0, j)),
            pl.BlockSpec((OP_TM, D_MODEL), row),
            pl.BlockSpec((1, 1, 1, 6 * D_MODEL), lambda i, j: (l, st.mod_row(i, OP_TM), 0, 0)),
            pl.BlockSpec((1, 1, D_MODEL), lambda i, j: (l, 0, 0)),
            pl.BlockSpec((1, 1, D_MODEL), lambda i, j: (l, 0, 0)),
            pl.BlockSpec((D_MODEL, N_EXPERTS), lambda i, j: (0, 0)),
            pl.BlockSpec((1, N_EXPERTS), lambda i, j: (0, 0))],
        out_specs=[pl.BlockSpec((OP_TM, D_MODEL), row),
                   pl.BlockSpec((OP_TM, D_MODEL), row),
                   pl.BlockSpec((OP_TM, N_EXPERTS), row),
                   pl.BlockSpec((OP_TM, N_EXPERTS), row)],
        out_shape=[jax.ShapeDtypeStruct((st.rows, D_MODEL), F32),
                   jax.ShapeDtypeStruct((st.rows, D_MODEL), BF16),
                   jax.ShapeDtypeStruct((st.rows, N_EXPERTS), F32),
                   jax.ShapeDtypeStruct((st.rows, N_EXPERTS), F32)],
        scratch_shapes=[pltpu.VMEM((OP_TM, D_MODEL), F32)],
        compiler_params=_cparams("parallel", "arbitrary"),
        name="oproj_ln1_router",
    )(*ys, w_out, x, mod4, ln_g.reshape(DEPTH, 1, D_MODEL), ln_b.reshape(DEPTH, 1, D_MODEL),
      w_router, b_router.reshape(1, N_EXPERTS))


LN_TM = 512


GS_BLK = 1024
GS_CAP = 384
GS_ALIGN = 2 * SUB
GS_COLS = 512
GS_STEP = 256
GS_VMEM_LIMIT = 60 * 1024 * 1024
GS_ROWS = pl.cdiv(GS_BLK + N_EXPERT_GROUPS * GS_ALIGN + GS_CAP, GS_COLS) * GS_COLS


def _moe_sorted_kernel(h_ref, gate_ref, keep_ref, wg_ref, wu_ref, wd_ref, f_ref, xs_scr, gs_scr, pt_scr, acc_scr,
                       seg_smem):
    e = pl.program_id(1)

    @pl.when(e == 0)
    def _():
        keep = keep_ref[...]
        lane = _iota(keep.shape, 1)
        member = []
        for g in range(N_EXPERT_GROUPS):
            in_g = (lane >= g * EXPERTS_PER_GROUP) & (lane < (g + 1) * EXPERTS_PER_GROUP)
            member.append(jnp.minimum(jnp.sum(jnp.where(in_g, keep, 0.0), axis=-1, keepdims=True), 1.0))
        lane4 = _iota((GS_BLK, LANE), 1)
        onehot = jnp.zeros((GS_BLK, LANE), F32)
        for g in range(N_EXPERT_GROUPS):
            onehot = onehot + jnp.where(lane4 == g, member[g], 0.0)
        counts = jnp.sum(onehot, axis=0, keepdims=True)
        starts = []
        start = jnp.zeros((1, 1), F32)
        for g in range(N_EXPERT_GROUPS):
            cnt_g = counts[:, g:g + 1]
            starts.append(start)
            seg_smem[2 * g] = jnp.sum(start).astype(jnp.int32)
            seg_smem[2 * g + 1] = jnp.sum(cnt_g).astype(jnp.int32)
            start = start + jnp.ceil(cnt_g * (1.0 / GS_ALIGN)) * GS_ALIGN
        onehot_b = onehot.astype(BF16)
        for r0 in range(0, GS_BLK, GS_STEP):
            tri = jnp.where(_iota((GS_STEP, GS_BLK), 1) <= _iota((GS_STEP, GS_BLK), 0) + r0, 1.0, 0.0).astype(BF16)
            csum = jnp.dot(tri, onehot_b, preferred_element_type=F32)
            dest = jnp.zeros((GS_STEP, 1), F32)
            for g in range(N_EXPERT_GROUPS):
                dest = dest + member[g][r0:r0 + GS_STEP] * (starts[g] + csum[:, g:g + 1] - 1.0)
            pt_scr[r0:r0 + GS_STEP, :] = jnp.where(
                _iota((GS_STEP, GS_ROWS), 1) == dest.astype(jnp.int32), 1.0, 0.0).astype(BF16)
        for c0 in range(0, GS_ROWS, GS_COLS):
            pt_c = pt_scr[:, c0:c0 + GS_COLS]
            xs_scr[c0:c0 + GS_COLS, :] = _dg(pt_c, h_ref[...], TN).astype(BF16)
            gs_scr[c0:c0 + GS_COLS, :] = _dot_sel_l(pt_c, gate_ref[...], TN)
        acc_scr[...] = jnp.zeros_like(acc_scr)

    g = e // EXPERTS_PER_GROUP
    seg_start = seg_smem[2 * g]
    seg_len = seg_smem[2 * g + 1]
    wg = wg_ref[0, 0]
    wu = wu_ref[0, 0]
    wd = wd_ref[0, 0]

    def window(c, carry):
        rows = pl.ds(pl.multiple_of(seg_start + c * GS_CAP, GS_ALIGN), GS_CAP)
        x = xs_scr[rows, :]
        a = jnp.dot(x, wg, preferred_element_type=F32)
        u = jnp.dot(x, wu, preferred_element_type=F32)
        gates = gs_scr[rows, :]
        ge = jnp.sum(jnp.where(_iota(gates.shape, 1) == e, gates, 0.0), axis=-1, keepdims=True)
        hid = (_silu(a) * u * ge).astype(BF16)
        acc_scr[rows, :] += jnp.dot(hid, wd, preferred_element_type=F32)
        return carry

    lax.fori_loop(0, (seg_len + GS_CAP - 1) // GS_CAP, window, 0)

    @pl.when(e == N_EXPERTS - 1)
    def _():
        pt = pt_scr[...]
        for c0 in range(0, D_MODEL, GS_COLS):
            hi, lo = _split(acc_scr[:, c0:c0 + GS_COLS])
            f_ref[:, c0:c0 + GS_COLS] = (jnp.dot(pt, hi, preferred_element_type=F32)
                                         + jnp.dot(pt, lo, preferred_element_type=F32)).astype(BF16)


def _moe_sorted(st, h2, gates, keep, wg_b, wu_b, wd_b, l):
    row = lambda i, e: (i, 0)
    wspec = lambda a, b: pl.BlockSpec((1, 1, a, b), lambda i, e: (l, e, 0, 0))
    return pl.pallas_call(
        _moe_sorted_kernel,
        grid=(st.rows // GS_BLK, N_EXPERTS),
        in_specs=[pl.BlockSpec((GS_BLK, D_MODEL), row, pipeline_mode=pl.Buffered(1)),
                  pl.BlockSpec((GS_BLK, N_EXPERTS), row), pl.BlockSpec((GS_BLK, N_EXPERTS), row),
                  wspec(D_MODEL, D_EXPERT), wspec(D_MODEL, D_EXPERT), wspec(D_EXPERT, D_MODEL)],
        out_specs=pl.BlockSpec((GS_BLK, D_MODEL), row),
        out_shape=jax.ShapeDtypeStruct((st.rows, D_MODEL), BF16),
        scratch_shapes=[pltpu.VMEM((GS_ROWS, D_MODEL), BF16), pltpu.VMEM((GS_ROWS, N_EXPERTS), F32),
                        pltpu.VMEM((GS_BLK, GS_ROWS), BF16), pltpu.VMEM((GS_ROWS, D_MODEL), F32),
                        pltpu.SMEM((2 * N_EXPERT_GROUPS,), jnp.int32)],
        compiler_params=pltpu.CompilerParams(dimension_semantics=("parallel", "arbitrary"),
                                             vmem_limit_bytes=GS_VMEM_LIMIT),
        name="moe_group_sorted",
    )(h2, gates, keep, wg_b, wu_b, wd_b)


def _ln2f_kernel(x_ref, f_ref, mod_ref, g_ref, b_ref, o_ref):
    gate2 = mod_ref[0, 0][:, 5 * D_MODEL:6 * D_MODEL]
    o_ref[...] = _layer_norm(DEEPNORM_ALPHA * x_ref[...] + gate2 * f_ref[...].astype(F32), g_ref[0], b_ref[0])


def _ln2f(st, x1, f, mod4, ln_g, ln_b, l):
    row = lambda i: (i, 0)
    return pl.pallas_call(
        _ln2f_kernel,
        grid=(st.rows // LN_TM,),
        in_specs=[pl.BlockSpec((LN_TM, D_MODEL), row),
                  pl.BlockSpec((LN_TM, D_MODEL), row),
                  pl.BlockSpec((1, 1, 1, 6 * D_MODEL), lambda i: (l, st.mod_row(i, LN_TM), 0, 0)),
                  pl.BlockSpec((1, 1, D_MODEL), lambda i: (l, 0, 0)),
                  pl.BlockSpec((1, 1, D_MODEL), lambda i: (l, 0, 0))],
        out_specs=pl.BlockSpec((LN_TM, D_MODEL), row),
        out_shape=jax.ShapeDtypeStruct((st.rows, D_MODEL), F32),
        compiler_params=_cparams("parallel"),
        name="ln2",
    )(x1, f, mod4, ln_g.reshape(DEPTH, 1, D_MODEL), ln_b.reshape(DEPTH, 1, D_MODEL))


S5_TC = 64
S5_SEQS = SUB
S5_Q = 4
S5_QS = S5_GROUPS // S5_Q * S5_STATE
N_S5 = S5_GROUPS * S5_STATE


def _s5_prep_kernel(lr_ref, li_ref, ls_ref, br_ref, bi_ref, e_ref, lbr_ref, lbi_ref, bbr_ref, bbi_ref):
    lr = lr_ref[0]
    li = li_ref[0]
    dt = jnp.exp(ls_ref[0])
    mag = jnp.exp(lr * dt)
    ang = li * dt
    ar = mag * jnp.cos(ang)
    ai = mag * jnp.sin(ang)
    lbr_ref[0] = ar
    lbi_ref[0] = ai
    den = lr * lr + li * li
    nr = ar - 1.0
    cr = (nr * lr + ai * li) / den
    ci = (ai * lr - nr * li) / den
    cr = _dot_sel_l(e_ref[...], cr)
    ci = _dot_sel_l(e_ref[...], ci)
    bre = br_ref[0]
    bim = bi_ref[0]
    bbr_ref[0] = cr * bre - ci * bim
    bbi_ref[0] = cr * bim + ci * bre


def _s5_prep(lam_re, lam_im, log_step, b_re, b_im):
    d2 = DEPTH * 2
    g, p, h = S5_GROUPS, S5_STATE, S5_CH
    bt_re = jnp.swapaxes(b_re, -1, -2).reshape(d2, g * h, p)
    bt_im = jnp.swapaxes(b_im, -1, -2).reshape(d2, g * h, p)
    expand = jnp.asarray(np.kron(np.eye(g, dtype=np.float32), np.ones((h, 1), np.float32)))
    spec_gp = pl.BlockSpec((1, g, p), lambda i: (i, 0, 0))
    spec_b = pl.BlockSpec((1, g * h, p), lambda i: (i, 0, 0))
    lbr, lbi, bbr, bbi = pl.pallas_call(
        _s5_prep_kernel,
        grid=(d2,),
        in_specs=[spec_gp, spec_gp, pl.BlockSpec((1, g, 1), lambda i: (i, 0, 0)), spec_b, spec_b,
                  pl.BlockSpec((g * h, g), lambda i: (0, 0))],
        out_specs=[spec_gp, spec_gp, spec_b, spec_b],
        out_shape=[jax.ShapeDtypeStruct((d2, g, p), F32)] * 2 + [jax.ShapeDtypeStruct((d2, g * h, p), F32)] * 2,
        compiler_params=_cparams("parallel"),
        name="s5_prep",
    )(lam_re.reshape(d2, g, p), lam_im.reshape(d2, g, p), log_step.reshape(d2, g, 1), bt_re, bt_im, expand)
    return lbr, lbi, bbr, bbi


def _s5_block_params(lbr, lbi, bbr, bbi, c_re, c_im):
    d2 = DEPTH * 2
    gq = S5_GROUPS // S5_Q
    eye = jnp.eye(gq, dtype=F32)

    def b_blocks(b):
        b = b.reshape(d2, S5_Q, gq, S5_CH, S5_STATE)
        return jnp.einsum('djghp,gk->djghkp', b, eye).reshape(d2, S5_Q, gq * S5_CH, gq * S5_STATE)

    def c_blocks(c):
        c = c.reshape(d2, S5_Q, gq, S5_CH, S5_STATE)
        return jnp.einsum('djghp,gk->djgpkh', c, eye).reshape(d2, S5_Q, gq * S5_STATE, gq * S5_CH)

    bq = jnp.concatenate([b_blocks(bbr), b_blocks(bbi)], axis=-1)
    cq = jnp.concatenate([c_blocks(c_re), -c_blocks(c_im)], axis=-2)
    return bq, cq, lbr.reshape(d2, 1, N_S5), lbi.reshape(d2, 1, N_S5)


def _s5_scan_kernel(u_ref, bq_ref, cq_ref, lr_ref, li_ref, s0r_ref, s0i_ref, y_ref, sfr_ref, sfi_ref,
                    utb, bur, bui, ytb, sr_scr, si_scr):
    d = pl.program_id(1)
    c = pl.program_id(2)
    nrow = S5_TC * S5_SEQS
    cw = D_GROUP // S5_Q

    @pl.when(c == 0)
    def _():
        sr_scr[...] = s0r_ref[0, 0]
        si_scr[...] = s0i_ref[0, 0]

    for b in range(S5_SEQS):
        for j in range(S5_Q):
            utb[j, pl.ds(b, S5_TC, stride=S5_SEQS), :] = u_ref[b, :, j * cw:(j + 1) * cw]
    for j in range(S5_Q):
        bu = _dot3(utb[j], bq_ref[0, j])
        bur[:, j * S5_QS:(j + 1) * S5_QS] = bu[:, :S5_QS]
        bui[:, j * S5_QS:(j + 1) * S5_QS] = bu[:, S5_QS:]
    for j in range(S5_Q):
        sl = slice(j * S5_QS, (j + 1) * S5_QS)
        lam_r = jnp.broadcast_to(lr_ref[0][:, sl], (S5_SEQS, S5_QS))
        lam_i = jnp.broadcast_to(li_ref[0][:, sl], (S5_SEQS, S5_QS))

        def step(t, carry, sl=sl, lam_r=lam_r, lam_i=lam_i):
            sr, si = carry
            te = jnp.where(d == 0, t, S5_TC - 1 - t)
            rows = pl.ds(pl.multiple_of(te * S5_SEQS, S5_SEQS), S5_SEQS)
            nr = lam_r * sr - lam_i * si + bur[rows, sl]
            ni = lam_r * si + lam_i * sr + bui[rows, sl]
            bur[rows, sl] = nr
            bui[rows, sl] = ni
            return nr, ni

        sr, si = lax.fori_loop(0, S5_TC, step, (sr_scr[:, sl], si_scr[:, sl]))
        sr_scr[:, sl] = sr
        si_scr[:, sl] = si
    for j in range(S5_Q):
        sl = slice(j * S5_QS, (j + 1) * S5_QS)
        yj = _dot3(bur[:, sl], cq_ref[0, j, :S5_QS, :]) + _dot3(bui[:, sl], cq_ref[0, j, S5_QS:, :])
        ytb[j] = yj
    for b in range(S5_SEQS):
        for j in range(S5_Q):
            y_ref[0, b, :, j * cw:(j + 1) * cw] = ytb[j, pl.ds(b, S5_TC, stride=S5_SEQS), :]

    @pl.when(c == pl.num_programs(2) - 1)
    def _():
        sfr_ref[0, 0] = sr_scr[...]
        sfi_ref[0, 0] = si_scr[...]


def _s5_scan(st, z3, bq, cq, lamr, lami, s0r, s0i, l):
    ng, nc = st.nb // S5_SEQS, st.L // S5_TC
    nrow = S5_TC * S5_SEQS
    chunk = lambda d, c: c + d * (nc - 1 - 2 * c)
    par = lambda g, d, c: (2 * l + d, 0, 0, 0)
    st_spec = pl.BlockSpec((1, 1, S5_SEQS, N_S5), lambda g, d, c: (g, d, 0, 0))
    return pl.pallas_call(
        _s5_scan_kernel,
        grid=(ng, 2, nc),
        in_specs=[pl.BlockSpec((S5_SEQS, S5_TC, D_GROUP), lambda g, d, c: (g, chunk(d, c), 0)),
                  pl.BlockSpec((1, S5_Q, D_GROUP // S5_Q, 2 * S5_QS), par),
                  pl.BlockSpec((1, S5_Q, 2 * S5_QS, D_GROUP // S5_Q), par),
                  pl.BlockSpec((1, 1, N_S5), lambda g, d, c: (2 * l + d, 0, 0)),
                  pl.BlockSpec((1, 1, N_S5), lambda g, d, c: (2 * l + d, 0, 0)),
                  st_spec, st_spec],
        out_specs=[pl.BlockSpec((1, S5_SEQS, S5_TC, D_GROUP), lambda g, d, c: (d, g, chunk(d, c), 0)),
                   st_spec, st_spec],
        out_shape=[jax.ShapeDtypeStruct((2, st.nb, st.L, D_GROUP), F32),
                   jax.ShapeDtypeStruct((ng, 2, S5_SEQS, N_S5), F32),
                   jax.ShapeDtypeStruct((ng, 2, S5_SEQS, N_S5), F32)],
        scratch_shapes=[pltpu.VMEM((S5_Q, nrow, LANE), F32), pltpu.VMEM((nrow, N_S5), F32),
                        pltpu.VMEM((nrow, N_S5), F32), pltpu.VMEM((S5_Q, nrow, LANE), F32),
                        pltpu.VMEM((S5_SEQS, N_S5), F32), pltpu.VMEM((S5_SEQS, N_S5), F32)],
        compiler_params=_cparams("parallel", "arbitrary", "arbitrary"),
        name="s5_scan",
    )(z3, bq, cq, lamr, lami, s0r, s0i)


S5_TM = 512


def _gelu_tanh(x):
    return 0.5 * x * (1.0 + jnp.tanh(math.sqrt(2.0 / math.pi) * (x + 0.044715 * (x * x * x))))


def _s5_out_kernel(u_ref, yf_ref, yb_ref, d_ref, w_ref, o_ref):
    y = yf_ref[0] + yb_ref[0] + d_ref[0] * u_ref[...]
    y = _gelu_tanh(y)
    o_ref[...] = (y * _sigmoid(_dot1(y, w_ref[0]))).astype(BF16)


def _s5_out(st, z, ydir, d_skip, w_glu, l):
    return pl.pallas_call(
        _s5_out_kernel,
        grid=(st.rows // S5_TM,),
        in_specs=[pl.BlockSpec((S5_TM, D_GROUP), lambda i: (i, 0)),
                  pl.BlockSpec((1, S5_TM, D_GROUP), lambda i: (0, i, 0)),
                  pl.BlockSpec((1, S5_TM, D_GROUP), lambda i: (1, i, 0)),
                  pl.BlockSpec((1, 1, D_GROUP), lambda i: (l, 0, 0)),
                  pl.BlockSpec((1, D_GROUP, D_GROUP), lambda i: (l, 0, 0))],
        out_specs=pl.BlockSpec((S5_TM, D_GROUP), lambda i: (i, 0)),
        out_shape=jax.ShapeDtypeStruct((st.rows, D_GROUP), BF16),
        compiler_params=_cparams("parallel"),
        name="s5_out",
    )(z, ydir, ydir, d_skip.reshape(DEPTH, 1, D_GROUP), w_glu)


NA_SCALE = NA_HEAD ** -0.5
NA_ROWS = DEC_SEQ // GRID_W
NA_KH = min(NA_WIN_H, NA_ROWS)
NA_QCOL = NA_OFF // LANE
NA_KCOL = (NA_OFF + D_GROUP) // LANE
NA_VCOL = (NA_OFF + 2 * D_GROUP) // LANE
NA_NDR = 2 * NA_WIN_H - 1


def _na_ctx_kernel(q_ref, k_ref, v_ref, o_ref, nk_ref, nv_ref):
    for hh in range(2):
        sl = slice(hh * NA_HEAD, (hh + 1) * NA_HEAD)
        q = q_ref[0, :, sl]
        k = k_ref[0, :, sl]
        v = v_ref[0, :, sl]
        nk_ref[0, hh] = k
        nv_ref[0, hh] = v
        s = _dot1(q, k, NT) * NA_SCALE
        e = jnp.exp(s - jnp.max(s, axis=-1, keepdims=True))
        o = _dot1(e, v) / jnp.sum(e, axis=-1, keepdims=True)
        o_ref[0, :, sl] = o.astype(BF16)


def _na_ctx(st, z3):
    blk = lambda col: pl.BlockSpec((1, st.L, LANE), lambda b, p: (b, 0, col + p))
    kv_spec = pl.BlockSpec((1, 2, st.L, NA_HEAD), lambda b, p: (b, p, 0, 0))
    kv_shape = jax.ShapeDtypeStruct((st.nb, NA_HEADS, st.L, NA_HEAD), F32)
    return pl.pallas_call(
        _na_ctx_kernel,
        grid=(st.nb, NA_HEADS // 2),
        in_specs=[blk(NA_QCOL), blk(NA_KCOL), blk(NA_VCOL)],
        out_specs=[pl.BlockSpec((1, st.L, LANE), lambda b, p: (b, 0, p)), kv_spec, kv_spec],
        out_shape=[jax.ShapeDtypeStruct((st.nb, st.L, D_GROUP), BF16), kv_shape, kv_shape],
        compiler_params=_cparams("parallel", "parallel"),
        name="na_context",
    )(z3, z3, z3)


def _na_nbr_kernel(q_ref, k_ref, v_ref, ck_ref, cv_ref, bias_ref, o_ref):
    nloc = NA_KH * GRID_W
    qc = _iota((GRID_W, nloc), 0)
    kc = _iota((GRID_W, nloc), 1) % GRID_W
    cs = jnp.clip(qc - NA_WIN_W // 2, 0, GRID_W - NA_WIN_W)
    col_in = (kc >= cs) & (kc < cs + NA_WIN_W)
    for hh in range(2):
        sl = slice(hh * NA_HEAD, (hh + 1) * NA_HEAD)
        kctx = ck_ref[0, 0, hh]
        vctx = cv_ref[0, 0, hh]
        for r in range(NA_ROWS):
            rs = min(max(r - NA_KH // 2, 0), NA_ROWS - NA_KH)
            q = q_ref[0, r * GRID_W:(r + 1) * GRID_W, sl]
            k = k_ref[0, rs * GRID_W:rs * GRID_W + nloc, sl]
            v = v_ref[0, rs * GRID_W:rs * GRID_W + nloc, sl]
            off = (rs - r + NA_WIN_H - 1) * GRID_W
            s_loc = _dot1(q, k, NT) * NA_SCALE + bias_ref[hh, :, off:off + nloc]
            s_loc = jnp.where(col_in, s_loc, NEG_INF)
            s_ctx = _dot1(q, kctx, NT) * NA_SCALE
            m = jnp.maximum(jnp.max(s_loc, axis=-1, keepdims=True), jnp.max(s_ctx, axis=-1, keepdims=True))
            e_loc = jnp.exp(s_loc - m)
            e_ctx = jnp.exp(s_ctx - m)
            den = jnp.sum(e_loc, axis=-1, keepdims=True) + jnp.sum(e_ctx, axis=-1, keepdims=True)
            o = (_dot1(e_loc, v) + _dot1(e_ctx, vctx)) / den
            o_ref[0, r * GRID_W:(r + 1) * GRID_W, sl] = o.astype(BF16)


def _na_bias_table(rpb):
    qc = np.arange(GRID_W)[:, None]
    kc = np.arange(GRID_W)[None, :]
    dc = np.clip(kc - qc, -(NA_WIN_W - 1), NA_WIN_W - 1) + (NA_WIN_W - 1)
    t = rpb[:, :, :, dc]
    return jnp.transpose(t, (0, 1, 3, 2, 4)).reshape(DEPTH, NA_HEADS, GRID_W, NA_NDR * GRID_W)


def _na_nbr(st, z3, cache_k, cache_v, bias_tab, l):
    blk = lambda col: pl.BlockSpec((1, st.L, LANE), lambda b, p: (b, 0, col + p))
    cspec = pl.BlockSpec((1, 1, 2, PAST_LEN, NA_HEAD), lambda b, p: (b, l, p, 0, 0))
    return pl.pallas_call(
        _na_nbr_kernel,
        grid=(st.nb, NA_HEADS // 2),
        in_specs=[blk(NA_QCOL), blk(NA_KCOL), blk(NA_VCOL), cspec, cspec,
                  pl.BlockSpec((None, 2, GRID_W, NA_NDR * GRID_W), lambda b, p: (l, p, 0, 0))],
        out_specs=pl.BlockSpec((1, st.L, LANE), lambda b, p: (b, 0, p)),
        out_shape=jax.ShapeDtypeStruct((st.nb, st.L, D_GROUP), BF16),
        compiler_params=_cparams("parallel", "parallel"),
        name="na_neighbourhood",
    )(z3, z3, z3, cache_k, cache_v, bias_tab)


ML_COL = ML_OFF // LANE
ML_SCALE = ML_HEAD ** -0.5


def _log_sigmoid(x):
    return -_softplus(-x)


def _rope_tables(L):
    half = ML_HEAD // 2
    quarter = half // 2
    t = np.arange(L)
    inv_freq = ROPE_BASE ** (-np.arange(quarter, dtype=np.float32) / quarter)

    def tabs(pos):
        ang = pos.astype(np.float32)[:, None] * inv_freq[None, :].astype(np.float32)
        c, s = np.cos(ang), np.sin(ang)
        return np.concatenate([c, c], axis=-1), np.concatenate([-s, s], axis=-1)

    c1, s1 = tabs(t // GRID_W)
    c2, s2 = tabs(t % GRID_W)
    return (jnp.asarray(np.concatenate([c1, c2], axis=-1), F32),
            jnp.asarray(np.concatenate([s1, s2], axis=-1), F32))


def _ml_kernel(*refs, L, rotary, layer):
    nh = ML_HEADS
    ib_ref, fb_ref = refs[0], refs[1]
    q_refs, k_refs, v_refs, o_refs = (refs[2 + i * nh:2 + (i + 1) * nh] for i in range(4))
    (g_ref, cos_ref, sin_ref, c0_ref, n0_ref, m0_ref, lng_ref, lnb_ref,
     y_ref, cf_ref, nf_ref, mf_ref, qs, ks, hf, hb, c_scr, n_scr, m_scr) = refs[2 + 4 * nh:]
    T = ML_CHUNK
    nc = L // T

    if rotary:
        first = (_iota((L, ML_HEAD), 1) % (ML_HEAD // 2)) < ML_HEAD // 4

        def rope(x):
            quarter = ML_HEAD // 4
            partner = jnp.where(first, pltpu.roll(x, ML_HEAD - quarter, axis=1), pltpu.roll(x, quarter, axis=1))
            return x * cos_ref[...] + partner * sin_ref[...]
    else:
        rope = lambda x: x
    for h in range(nh):
        qs[h] = rope(q_refs[h][0]) * ML_SCALE
        ks[h] = rope(k_refs[h][0])
        for d in range(2):
            c_scr[d, h] = c0_ref[0, d, h]
            n_scr[d, h] = n0_ref[0, h, d:d + 1, :]
            m_scr[d, h] = m0_ref[0, h, d:d + 1, :]

    tj = _iota((T, T), 0)
    ts = _iota((T, T), 1)
    ones = jnp.ones((T, T), F32)
    upto_row = (tj <= ts, tj >= ts)
    upto_col = (ts <= tj, ts >= tj)

    def body(ci, carry):
        ch = []
        for d in range(2):
            cd = ci if d == 0 else nc - 1 - ci
            rows = pl.ds(pl.multiple_of(cd * T, T), T)
            for h in range(nh):
                g = g_ref[0, h, cd]
                ig = g[:, d:d + 1] + ib_ref[layer * 2 * nh + d * nh + h]
                fg = g[:, 2 + d:3 + d] + fb_ref[layer * 2 * nh + d * nh + h]
                lf = jnp.broadcast_to(_log_sigmoid(fg), (T, T))
                igb = jnp.broadcast_to(ig, (T, T))
                q = dict(d=d, h=h, rows=rows, ig=ig, qc=qs[h, rows, :], kc=ks[h, rows, :], vc=v_refs[h][0, rows, :])
                q['bcol'] = _dot_sel_l(jnp.where(upto_col[d], 1.0, 0.0), lf)
                q['brow_i'] = _dot_sel_l(ones, jnp.where(upto_row[d], lf, 0.0) - jnp.where(tj == ts, igb, 0.0))
                ch.append(q)
        for q in ch:
            q['qk'] = _dot1(q['qc'], q['kc'], NT)
            q['c_old'] = c_scr[q['d'], q['h']]
            q['qc_c'] = _dot1(q['qc'], q['c_old'])
        for q in ch:
            d, h = q['d'], q['h']
            dmat = jnp.where(upto_col[d], q['bcol'] - q['brow_i'], -jnp.inf)
            b1 = q['bcol'][:, 0:1]
            m_prev = m_scr[d, h]
            inter = b1 + m_prev
            m_t = jnp.maximum(inter, jnp.max(dmat, axis=-1, keepdims=True))
            qk = q['qk'] * jnp.exp(dmat - m_t)
            w_inter = jnp.exp(inter - m_t)
            n_old = n_scr[d, h]
            den = w_inter * jnp.sum(q['qc'] * n_old, axis=-1, keepdims=True) + jnp.sum(qk, axis=-1, keepdims=True)
            q['scale'] = 1.0 / jnp.maximum(jnp.abs(den), jnp.exp(-m_t))
            q['inter_part'] = w_inter * q['qc_c']
            q['intra'] = _dot1(qk, q['vc'])
            b_last = b1[T - 1:T, :] if d == 0 else b1[0:1, :]
            g_s = b_last - b1 + q['ig']
            m_new = jnp.maximum(b_last + m_prev, jnp.max(g_s, axis=0, keepdims=True))
            w_old = jnp.exp(b_last + m_prev - m_new)
            w_s = jnp.exp(g_s - m_new)
            q['c_new'] = _dot1(q['kc'], w_s * q['vc'], TN)
            q['w_old'] = w_old
            n_scr[d, h] = w_old * n_old + jnp.sum(w_s * q['kc'], axis=0, keepdims=True)
            m_scr[d, h] = m_new
        for q in ch:
            d, h = q['d'], q['h']
            hcur = (q['inter_part'] + q['intra']) * q['scale']
            if d == 0:
                hf[h, q['rows'], :] = hcur
            else:
                hb[h, q['rows'], :] = hcur
            c_scr[d, h] = q['w_old'] * q['c_old'] + q['c_new']
        return carry

    lax.fori_loop(0, nc, body, 0)

    for h in range(nh):
        hsum = hf[h] + hb[h]
        mu = jnp.mean(hsum, axis=-1, keepdims=True)
        dv = hsum - mu
        var = jnp.mean(dv * dv, axis=-1, keepdims=True)
        hn = dv * lax.rsqrt(var + ML_GN_EPS)
        y_ref[0, :, h * ML_HEAD:(h + 1) * ML_HEAD] = (
            _sigmoid(o_refs[h][0]) * (hn * lng_ref[0, :, h * ML_HEAD:(h + 1) * ML_HEAD]
                                      + lnb_ref[0, :, h * ML_HEAD:(h + 1) * ML_HEAD])).astype(BF16)
        for d in range(2):
            cf_ref[0, d, h] = c_scr[d, h]
            nf_ref[0, h, d:d + 1, :] = n_scr[d, h]
            mf_ref[0, h, d:d + 1, :] = m_scr[d, h]


def _ml_mixer(st, z3, zg, i_bias, f_bias, ln_g, ln_b, c0, n0, m0, l):
    L, nb, nh = st.L, st.nb, ML_HEADS
    nc = L // ML_CHUNK
    gates = zg.reshape(nb, L, 2, 2, nh).transpose(0, 4, 1, 2, 3).reshape(nb, nh, nc, ML_CHUNK, 4)
    cos_t, sin_t = _rope_tables(L)
    blk = lambda col: pl.BlockSpec((1, L, LANE), lambda b: (b, 0, col))
    head_blocks = [blk(ML_COL + part * nh + h) for part in range(4) for h in range(nh)]
    smem = pl.BlockSpec(memory_space=pltpu.SMEM)
    tab = pl.BlockSpec((L, ML_HEAD), lambda b: (0, 0))
    c_spec = pl.BlockSpec((1, 2, nh, ML_HEAD, ML_HEAD), lambda b: (b, 0, 0, 0, 0))
    n_spec = pl.BlockSpec((1, nh, 2, ML_HEAD), lambda b: (b, 0, 0, 0))
    m_spec = pl.BlockSpec((1, nh, 2, 1), lambda b: (b, 0, 0, 0))
    par = pl.BlockSpec((1, 1, D_GROUP), lambda b: (l, 0, 0))
    return pl.pallas_call(
        functools.partial(_ml_kernel, L=L, rotary=st.latent, layer=l),
        grid=(nb,),
        in_specs=[smem, smem] + head_blocks + [
            pl.BlockSpec((1, nh, nc, ML_CHUNK, 4), lambda b: (b, 0, 0, 0, 0)),
            tab, tab, c_spec, n_spec, m_spec, par, par],
        out_specs=[pl.BlockSpec((1, L, D_GROUP), lambda b: (b, 0, 0)), c_spec, n_spec, m_spec],
        out_shape=[jax.ShapeDtypeStruct((nb, L, D_GROUP), BF16),
                   jax.ShapeDtypeStruct((nb, 2, nh, ML_HEAD, ML_HEAD), F32),
                   jax.ShapeDtypeStruct((nb, nh, 2, ML_HEAD), F32),
                   jax.ShapeDtypeStruct((nb, nh, 2, 1), F32)],
        scratch_shapes=[pltpu.VMEM((nh, L, ML_HEAD), F32), pltpu.VMEM((nh, L, ML_HEAD), F32),
                        pltpu.VMEM((nh, L, ML_HEAD), F32), pltpu.VMEM((nh, L, ML_HEAD), F32),
                        pltpu.VMEM((2, nh, ML_HEAD, ML_HEAD), F32), pltpu.VMEM((2, nh, 1, ML_HEAD), F32),
                        pltpu.VMEM((2, nh, 1, 1), F32)],
        compiler_params=_cparams("parallel"),
        name="mlstm",
    )(i_bias.reshape(-1), f_bias.reshape(-1), *([z3] * (4 * nh)), gates, cos_t, sin_t, c0, n0, m0,
      ln_g.reshape(DEPTH, 1, D_GROUP), ln_b.reshape(DEPTH, 1, D_GROUP))


RW_TL = 256
RW_T = 64
RW_RCOL = RW_OFF // D_GROUP
RW_LCOL = (RW_OFF + 3 * D_GROUP) // (2 * LANE)
N_LORA = RW_LORA_W + RW_LORA_A + RW_LORA_G


def _head_ones():
    return jnp.asarray(np.kron(np.eye(RW_HEADS, dtype=np.float32), np.ones((RW_HEAD, RW_HEAD), np.float32)))


def _rw_front_kernel(r_ref, k_ref, v_ref, lo_ref, rp_ref, kp_ref, vp_ref, lp_ref, rn_ref, kn_ref, vn_ref, ln_ref,
                     mup_ref, mun_ref, w0_ref, w2_ref, a0_ref, a2_ref, g2_ref, kk_ref, ka_ref, hones_ref,
                     ro_ref, vo_ref, kko_ref, go_ref, kd_ref, bd_ref, lw_ref):
    t = pl.program_id(1)
    first = t == 0
    last = t == pl.num_programs(1) - 1
    row = _iota((RW_TL, 1), 0)

    def shift(x_ref, p_ref, n_ref, lo, hi):
        x = x_ref[0]
        prev_edge = jnp.where(first, 0.0, p_ref[0, SUB - 1:SUB, :])
        next_edge = jnp.where(last, 0.0, n_ref[0, 0:1, :])
        prev = jnp.where(row == 0, prev_edge, pltpu.roll(x, 1, axis=0))
        nxt = jnp.where(row == RW_TL - 1, next_edge, pltpu.roll(x, RW_TL - 1, axis=0))
        return x + mup_ref[0][:, lo:hi] * (prev - x) + mun_ref[0][:, lo:hi] * (nxt - x)

    g = D_GROUP
    r = shift(r_ref, rp_ref, rn_ref, 0, g)
    k = shift(k_ref, kp_ref, kn_ref, g, 2 * g)
    v = shift(v_ref, vp_ref, vn_ref, 2 * g, 3 * g)
    lo = shift(lo_ref, lp_ref, ln_ref, 3 * g, 3 * g + N_LORA)
    zw = lo[:, :RW_LORA_W]
    za = lo[:, RW_LORA_W:RW_LORA_W + RW_LORA_A]
    zg = lo[:, RW_LORA_W + RW_LORA_A:]
    ro_ref[0] = r
    vo_ref[0] = v
    kk = k * kk_ref[0]
    ssq = _dot_sel_r(kk * kk, hones_ref[...])
    kk = kk * lax.rsqrt(ssq + 1e-12)
    kko_ref[0] = kk
    go_ref[0] = _dot3(_sigmoid(zg), g2_ref[0])
    tw = jnp.tanh(zw)
    for d in range(2):
        w_log = -_softplus(-(w0_ref[0, d:d + 1, :] + _dot3(tw, w2_ref[0, d]))) - 0.5
        lw_ref[d, 0] = -jnp.exp(w_log)
        a = _sigmoid(a0_ref[0, d:d + 1, :] + _dot3(za, a2_ref[0, d]))
        kd_ref[d, 0] = k * (1.0 + (a - 1.0) * ka_ref[0])
        bd_ref[d, 0] = kk * a


def _rw_front(st, z3, mu_prev, mu_next, w0, w2, a0, a2, g2, k_k, k_a, l):
    nb, L = st.nb, st.L
    nt = L // RW_TL
    tpb = RW_TL // SUB
    main = lambda w, col: pl.BlockSpec((1, RW_TL, w), lambda b, t: (b, t, col))
    prev = lambda w, col: pl.BlockSpec((1, SUB, w), lambda b, t: (b, jnp.maximum(t * tpb - 1, 0), col))
    nxt = lambda w, col: pl.BlockSpec((1, SUB, w), lambda b, t: (b, jnp.minimum((t + 1) * tpb, L // SUB - 1), col))
    cols = [(D_GROUP, RW_RCOL), (D_GROUP, RW_RCOL + 1), (D_GROUP, RW_RCOL + 2), (2 * LANE, RW_LCOL)]
    lay = lambda *shape: pl.BlockSpec((1,) + shape, lambda b, t: (l,) + (0,) * len(shape))
    out1 = pl.BlockSpec((1, RW_TL, D_GROUP), lambda b, t: (b, t, 0))
    out2 = pl.BlockSpec((2, 1, RW_TL, D_GROUP), lambda b, t: (0, b, t, 0))
    s1 = jax.ShapeDtypeStruct((nb, L, D_GROUP), F32)
    s2 = jax.ShapeDtypeStruct((2, nb, L, D_GROUP), F32)
    return pl.pallas_call(
        _rw_front_kernel,
        grid=(nb, nt),
        in_specs=[main(*c) for c in cols] + [prev(*c) for c in cols] + [nxt(*c) for c in cols] + [
            lay(1, RW_IN), lay(1, RW_IN), lay(2, D_GROUP), lay(2, RW_LORA_W, D_GROUP), lay(2, D_GROUP),
            lay(2, RW_LORA_A, D_GROUP), lay(RW_LORA_G, D_GROUP), lay(1, D_GROUP), lay(1, D_GROUP),
            pl.BlockSpec((D_GROUP, D_GROUP), lambda b, t: (0, 0))],
        out_specs=[out1, out1, out1, out1, out2, out2, out2],
        out_shape=[s1, s1, s1, s1, s2, s2, s2],
        compiler_params=_cparams("parallel", "parallel"),
        name="rwkv_front",
    )(*([z3] * 12), mu_prev.reshape(DEPTH, 1, RW_IN), mu_next.reshape(DEPTH, 1, RW_IN), w0, w2, a0, a2, g2,
      k_k.reshape(DEPTH, 1, D_GROUP), k_a.reshape(DEPTH, 1, D_GROUP), _head_ones())


def _rw_core_kernel(rf_ref, vf_ref, kkf_ref, kdf_ref, bdf_ref, lwf_ref,
                    rb_ref, vb_ref, kkb_ref, kdb_ref, bdb_ref, lwb_ref, s0_ref,
                    yf_ref, yb_ref, sf_ref, s_scr):
    c = pl.program_id(1)
    T = RW_T

    @pl.when(c == 0)
    def _():
        s_scr[...] = s0_ref[0]

    tj = _iota((T, T), 0)
    ts = _iota((T, T), 1)
    dirs = ((rf_ref, vf_ref, kkf_ref, kdf_ref, bdf_ref, lwf_ref, yf_ref),
            (rb_ref, vb_ref, kkb_ref, kdb_ref, bdb_ref, lwb_ref, yb_ref))
    ch = []
    for d, (r_ref, v_ref, kk_ref, kd_ref, bd_ref, lw_ref, y_ref) in enumerate(dirs):
        if d == 0:
            incl = ts <= tj
            strict = ts < tj
            last = T - 1
        else:
            incl = ts >= tj
            strict = ts > tj
            last = 0
        lw = lw_ref[0, 0]
        cum = _dot_sel_l(jnp.where(incl, 1.0, 0.0), lw)
        w_in = jnp.exp(cum)
        w_inv = jnp.exp(-cum)
        w_ex = jnp.exp(cum - lw)
        kap_a = kk_ref[0] * w_ex
        bet_a = bd_ref[0, 0] * w_inv
        khat_a = kd_ref[0, 0] * w_inv
        rho_a = r_ref[0] * w_in
        w_tot = w_in[last:last + 1, :]
        v_a = v_ref[0]
        for h in range(RW_HEADS):
            sl = slice(h * RW_HEAD, (h + 1) * RW_HEAD)
            ch.append(dict(d=d, h=h, sl=sl, incl=incl, strict=strict, y_ref=y_ref, w_tot=w_tot[:, sl],
                           kap=kap_a[:, sl], bet=bet_a[:, sl], khat=khat_a[:, sl], rho=rho_a[:, sl], v=v_a[:, sl]))
    for q in ch:
        kr = jnp.concatenate([q['kap'], q['rho']], axis=0)
        bk = jnp.concatenate([q['bet'], q['khat']], axis=0)
        q['gram'] = _dot1(kr, bk, NT)
    for q in ch:
        gram = q.pop('gram')
        q['n'] = -jnp.where(q['strict'], gram[:T, :T], 0.0)
        l_k = jnp.where(q['strict'], gram[:T, T:], 0.0)
        q['m_b'] = jnp.where(q['incl'], gram[T:, :T], 0.0)
        m_k = jnp.where(q['incl'], gram[T:, T:], 0.0)
        q['lmv'] = _dot1(jnp.concatenate([l_k, m_k], axis=0), q['v'])
    for q in ch:
        q['x'] = jnp.concatenate([q['kap'], q['lmv'][:T]], axis=1)
    for lvl in range(6):
        mm = _dot3 if lvl < 3 else _dot1
        for q in ch:
            if lvl < 5:
                nx = mm(q['n'], jnp.concatenate([q['n'], q['x']], axis=1))
                q['n'], q['x'] = nx[:, :T], q['x'] + nx[:, T:]
            else:
                q['x'] = q['x'] + _dot1(q['n'], q['x'])
    for q in ch:
        q['p'] = _dot1(q['m_b'], q['x'])
        xb = _dot1(q['x'], q['bet'], TN)
        q['a_m'] = xb[:RW_HEAD]
        q['d_m'] = _dot1(q['v'], q['khat'], TN) - xb[RW_HEAD:]
    for q in ch:
        rho_p = q['rho'] - q['p'][:, :RW_HEAD]
        y_v = q['lmv'][T:] - q['p'][:, RW_HEAD:]
        s_old = s_scr[q['d'], q['h']]
        q['y_ref'][0, :, q['sl']] = _dot1(rho_p, s_old, NT) + y_v
        s_scr[q['d'], q['h']] = (s_old - _dot3(s_old, q['a_m']) + q['d_m']) * q['w_tot']

    @pl.when(c == pl.num_programs(1) - 1)
    def _():
        sf_ref[0] = s_scr[...]


def _rw_core(st, r, v, kk, kd, bd, lw, s0):
    nb, L = st.nb, st.L
    nc = L // RW_T
    f1 = pl.BlockSpec((1, RW_T, D_GROUP), lambda b, c: (b, c, 0))
    b1 = pl.BlockSpec((1, RW_T, D_GROUP), lambda b, c: (b, nc - 1 - c, 0))
    f2 = pl.BlockSpec((1, 1, RW_T, D_GROUP), lambda b, c: (0, b, c, 0))
    b2 = pl.BlockSpec((1, 1, RW_T, D_GROUP), lambda b, c: (1, b, nc - 1 - c, 0))
    s_spec = pl.BlockSpec((1, 2, RW_HEADS, RW_HEAD, RW_HEAD), lambda b, c: (b, 0, 0, 0, 0))
    ys = jax.ShapeDtypeStruct((nb, L, D_GROUP), F32)
    return pl.pallas_call(
        _rw_core_kernel,
        grid=(nb, nc),
        in_specs=[f1, f1, f1, f2, f2, f2, b1, b1, b1, b2, b2, b2, s_spec],
        out_specs=[f1, b1, s_spec],
        out_shape=[ys, ys, jax.ShapeDtypeStruct((nb, 2, RW_HEADS, RW_HEAD, RW_HEAD), F32)],
        scratch_shapes=[pltpu.VMEM((2, RW_HEADS, RW_HEAD, RW_HEAD), F32)],
        compiler_params=_cparams("parallel", "arbitrary"),
        name="rwkv_core",
    )(r, v, kk, kd, bd, lw, r, v, kk, kd, bd, lw, s0)


RWO_TM = 512


def _rw_out_kernel(yf_ref, yb_ref, r_ref, v_ref, kd0_ref, kd1_ref, g_ref, lng_ref, lnb_ref, rk_ref, hones_ref, o_ref):
    y = yf_ref[...] + yb_ref[...]
    inv = 1.0 / RW_HEAD
    mu = _dot_sel_r(y, hones_ref[...]) * inv
    dv = y - mu
    var = _dot_sel_r(dv * dv, hones_ref[...]) * inv
    yn = dv * lax.rsqrt(var + RW_GN_EPS) * lng_ref[0] + lnb_ref[0]
    kmean = 0.5 * (kd0_ref[0] + kd1_ref[0])
    bonus = _dot_sel_r(r_ref[...] * kmean * rk_ref[0], hones_ref[...]) * v_ref[...]
    o_ref[...] = ((yn + bonus) * g_ref[...]).astype(BF16)


def _rw_out(st, yf, yb, r, v, kd, g, ln_g, ln_b, r_k, l):
    rows = st.rows
    flat = lambda a: a.reshape(rows, D_GROUP)
    row = pl.BlockSpec((RWO_TM, D_GROUP), lambda i: (i, 0))
    lay = pl.BlockSpec((1, 1, D_GROUP), lambda i: (l, 0, 0))
    kd2 = kd.reshape(2, rows, D_GROUP)
    return pl.pallas_call(
        _rw_out_kernel,
        grid=(rows // RWO_TM,),
        in_specs=[row, row, row, row,
                  pl.BlockSpec((1, RWO_TM, D_GROUP), lambda i: (0, i, 0)),
                  pl.BlockSpec((1, RWO_TM, D_GROUP), lambda i: (1, i, 0)),
                  row, lay, lay, lay, pl.BlockSpec((D_GROUP, D_GROUP), lambda i: (0, 0))],
        out_specs=row,
        out_shape=jax.ShapeDtypeStruct((rows, D_GROUP), BF16),
        compiler_params=_cparams("parallel"),
        name="rwkv_out",
    )(flat(yf), flat(yb), flat(r), flat(v), kd2, kd2, flat(g), ln_g.reshape(DEPTH, 1, D_GROUP),
      ln_b.reshape(DEPTH, 1, D_GROUP), r_k.reshape(DEPTH, 1, D_GROUP), _head_ones())


def _trunk_layer(st, x, l, mod4, p, states, ctx_kv):
    nb, L = st.nb, st.L
    z, zg = _zproj(st, x, mod4, p['w_in'], p['w_gate_cols'], l)
    z3 = z.reshape(nb, L, D_Z)

    ydir, sfr, sfi = _s5_scan(st, z3, p['s5_bq'], p['s5_cq'], p['s5_lamr'], p['s5_lami'],
                              states['s5_re'], states['s5_im'], l)
    y_s5 = _s5_out(st, z, ydir.reshape(2, st.rows, D_GROUP), p['s5_d'], p['s5_w_glu'], l)

    r, v, kk, g, kd, bd, lw = _rw_front(st, z3, p['rw_mu_prev'], p['rw_mu_next'], p['rw_w0'], p['rw_w2'],
                                        p['rw_a0'], p['rw_a2'], p['rw_g2'], p['rw_k_k'], p['rw_k_a'], l)
    yf, yb, rw_s = _rw_core(st, r, v, kk, kd, bd, lw, states['rw'])
    y_rw = _rw_out(st, yf, yb, r, v, kd, g, p['rw_ln_g'], p['rw_ln_b'], p['rw_r_k'], l)

    if ctx_kv is None:
        y_na, nk, nv = _na_ctx(st, z3)
    else:
        y_na = _na_nbr(st, z3, ctx_kv[0], ctx_kv[1], p['na_bias'], l)
        nk = nv = None

    y_ml, ml_c, ml_n, ml_m = _ml_mixer(st, z3, zg, p['ml_i_bias'], p['ml_f_bias'], p['ml_ln_g'], p['ml_ln_b'],
                                       states['ml_c'], states['ml_n'], states['ml_m'], l)

    ys = (y_s5, y_rw, y_na.reshape(st.rows, D_GROUP), y_ml.reshape(st.rows, D_GROUP))
    tail = _oproj(st, ys, x, mod4, p['w_out'], p['ln1_g'], p['ln1_b'], p['w_router'], p['b_router'], l)
    return tail, (nk, nv, sfr, sfi, rw_s, ml_c, ml_n, ml_m)


def _moe_and_ln2(tails, mod4, p, l):
    xs = []
    for st, (x1, h2, gates, keep) in zip((PROMPT, LATENT), tails):
        f = _moe_sorted(st, h2, gates, keep, p['moe_w_gate'], p['moe_w_up'], p['moe_w_down'], l)
        xs.append(_ln2f(st, x1, f, mod4, p['ln2_g'], p['ln2_b'], l))
    return xs


def kernel(x_prompt, x_sample, cache_nat_k, cache_nat_v, state_s5_re, state_s5_im, state_rwkv, state_mlstm_c, state_mlstm_n, state_mlstm_m, c, c_ctx, w_mod, b_mod, w_in, w_out, s5_lam_re, s5_lam_im, s5_log_step, s5_b_re, s5_b_im, s5_c_re, s5_c_im, s5_d, s5_w_glu, rw_mu_prev, rw_mu_next, rw_w0, rw_w2, rw_a0, rw_a2, rw_g2, rw_k_k, rw_k_a, rw_r_k, rw_ln_g, rw_ln_b, na_rpb, ml_i_bias, ml_f_bias, ml_ln_g, ml_ln_b, ln1_g, ln1_b, ln2_g, ln2_b, w_router, b_router, moe_w_gate, moe_w_up, moe_w_down):
    dt = x_prompt.dtype
    cond = jnp.concatenate([c_ctx[None, :], c, jnp.zeros((MOD_ROWS - 1 - DEC_BATCH, D_MODEL), F32)], axis=0)
    mod4 = _modulation(cond, w_mod, b_mod).reshape(DEPTH, MOD_ROWS, 1, 6 * D_MODEL)

    lbr, lbi, bbr, bbi = _s5_prep(s5_lam_re, s5_lam_im, s5_log_step, s5_b_re, s5_b_im)
    s5_bq, s5_cq, s5_lamr, s5_lami = _s5_block_params(lbr, lbi, bbr, bbi, s5_c_re, s5_c_im)
    p = dict(w_in=w_in, w_gate_cols=w_in[:, :, D_Z:], w_out=w_out,
             s5_bq=s5_bq, s5_cq=s5_cq, s5_lamr=s5_lamr, s5_lami=s5_lami, s5_d=s5_d, s5_w_glu=s5_w_glu,
             rw_mu_prev=rw_mu_prev, rw_mu_next=rw_mu_next, rw_w0=rw_w0, rw_w2=rw_w2, rw_a0=rw_a0, rw_a2=rw_a2,
             rw_g2=rw_g2, rw_k_k=rw_k_k, rw_k_a=rw_k_a, rw_r_k=rw_r_k, rw_ln_g=rw_ln_g, rw_ln_b=rw_ln_b,
             na_bias=_na_bias_table(na_rpb), ml_i_bias=ml_i_bias, ml_f_bias=ml_f_bias, ml_ln_g=ml_ln_g,
             ml_ln_b=ml_ln_b, ln1_g=ln1_g, ln1_b=ln1_b, ln2_g=ln2_g, ln2_b=ln2_b, w_router=w_router,
             b_router=b_router, moe_w_gate=moe_w_gate.astype(BF16), moe_w_up=moe_w_up.astype(BF16),
             moe_w_down=moe_w_down.astype(BF16))

    gp = BATCH // S5_SEQS
    zero_states = dict(
        s5_re=jnp.zeros((gp, 2, S5_SEQS, N_S5), F32), s5_im=jnp.zeros((gp, 2, S5_SEQS, N_S5), F32),
        rw=jnp.zeros((BATCH, 2, RW_HEADS, RW_HEAD, RW_HEAD), F32),
        ml_c=jnp.zeros((BATCH, 2, ML_HEADS, ML_HEAD, ML_HEAD), F32),
        ml_n=jnp.zeros((BATCH, ML_HEADS, 2, ML_HEAD), F32), ml_m=jnp.zeros((BATCH, ML_HEADS, 2, 1), F32))

    xp = x_prompt.reshape(PROMPT.rows, D_MODEL)
    xs = x_sample.reshape(LATENT.rows, D_MODEL)
    outs = [[] for _ in range(8)]
    for l in range(DEPTH):
        tail_p, ctx_t = _trunk_layer(PROMPT, xp, l, mod4, p, zero_states, None)
        for acc, t in zip(outs, ctx_t):
            acc.append(t)
        lat_states = dict(
            s5_re=state_s5_re[:, l].reshape(DEC_BATCH, 2, N_S5).transpose(1, 0, 2)[None],
            s5_im=state_s5_im[:, l].reshape(DEC_BATCH, 2, N_S5).transpose(1, 0, 2)[None],
            rw=state_rwkv[:, l], ml_c=state_mlstm_c[:, l],
            ml_n=state_mlstm_n[:, l].transpose(0, 2, 1, 3), ml_m=state_mlstm_m[:, l].transpose(0, 2, 1)[..., None])
        tail_s, _ = _trunk_layer(LATENT, xs, l, mod4, p, lat_states, (cache_nat_k, cache_nat_v))
        xp, xs = _moe_and_ln2((tail_p, tail_s), mod4, p, l)

    nk, nv, s5r, s5i, rw, mc, mn, mm = [jnp.stack(t, axis=1) for t in outs]

    def s5_state(t):
        return t.transpose(0, 3, 1, 2, 4).reshape(BATCH, DEPTH, 2, S5_GROUPS, S5_STATE)

    return (xp.reshape(BATCH, SEQ, D_MODEL), xs.reshape(DEC_BATCH, DEC_SEQ, D_MODEL),
            nk, nv, s5_state(s5r).astype(dt), s5_state(s5i).astype(dt), rw.astype(dt), mc.astype(dt),
            mn.transpose(0, 1, 3, 2, 4).astype(dt), mm[..., 0].transpose(0, 1, 3, 2).astype(dt))
```

```python
import functools
import math

import numpy as np
import jax
import jax.numpy as jnp
from jax import lax
from jax.experimental import pallas as pl
from jax.experimental.pallas import tpu as pltpu

F32 = jnp.float32
BF16 = jnp.bfloat16

D_MODEL = 2048
BATCH = 16
SEQ = 256
DEPTH = 4
DEC_BATCH = 8
DEC_SEQ = 1024
PAST_LEN = 256
GRID_W = 64
D_GROUP = D_MODEL // 4
S5_CH = 16
S5_GROUPS = D_GROUP // S5_CH
S5_STATE = 64
RW_HEAD = 64
RW_HEADS = D_GROUP // RW_HEAD
RW_LORA_W = 64
RW_LORA_A = 64
RW_LORA_G = 128
RW_GN_EPS = 64e-5
NA_HEAD = 64
NA_HEADS = D_GROUP // NA_HEAD
NA_WIN_H = 8
NA_WIN_W = 16
NEG_INF = -1e30
ML_HEAD = 128
ML_HEADS = D_GROUP // ML_HEAD
ML_CHUNK = 64
ML_GN_EPS = 1e-5
ROPE_BASE = 10000.0
N_EXPERTS = 16
N_EXPERT_GROUPS = 4
EXPERTS_PER_GROUP = N_EXPERTS // N_EXPERT_GROUPS
D_EXPERT = 512
DEEPNORM_ALPHA = (2 * DEPTH) ** 0.25
LN_EPS = 1e-5
S5_IN = D_GROUP
RW_IN = 3 * D_GROUP + RW_LORA_W + RW_LORA_A + RW_LORA_G
NA_IN = 3 * D_GROUP
ML_IN = 4 * D_GROUP + 4 * ML_HEADS
D_IN = S5_IN + RW_IN + NA_IN + ML_IN
N_GATE = 4 * ML_HEADS
D_Z = D_IN - N_GATE
RW_OFF = S5_IN
NA_OFF = S5_IN + RW_IN
ML_OFF = NA_OFF + NA_IN
MOD_ROWS = 16
LANE = 128
SUB = 8

VMEM_LIMIT = 56 * 1024 * 1024


def _cparams(*sem):
    return pltpu.CompilerParams(dimension_semantics=sem, vmem_limit_bytes=VMEM_LIMIT)


def _dg(a, b, dims):
    return lax.dot_general(a, b, (dims, ((), ())), preferred_element_type=F32)


NN = ((1,), (0,))
NT = ((1,), (1,))
TN = ((0,), (0,))


def _dot1(a, b, dims=NN):
    return _dg(a.astype(BF16), b.astype(BF16), dims)


def _split(x):
    hi = x.astype(BF16)
    lo = (x - hi.astype(F32)).astype(BF16)
    return hi, lo


def _split3(x):
    hi = x.astype(BF16)
    r = x - hi.astype(F32)
    mid = r.astype(BF16)
    lo = (r - mid.astype(F32)).astype(BF16)
    return hi, mid, lo


def _dot3(a, b, dims=NN):
    ah, al = _split(a)
    bh, bl = _split(b)
    return _dg(ah, bh, dims) + (_dg(ah, bl, dims) + _dg(al, bh, dims))


def _dot_sel_l(sel, x, dims=NN):
    s = sel.astype(BF16)
    hi, mid, lo = _split3(x)
    return _dg(s, hi, dims) + (_dg(s, mid, dims) + _dg(s, lo, dims))


def _dot_sel_r(x, sel, dims=NN):
    s = sel.astype(BF16)
    hi, mid, lo = _split3(x)
    return _dg(hi, s, dims) + (_dg(mid, s, dims) + _dg(lo, s, dims))


def _sigmoid(x):
    return 1.0 / (1.0 + jnp.exp(-x))


def _silu(x):
    return x * _sigmoid(x)


def _softplus(x):
    return jnp.maximum(x, 0.0) + jnp.log(1.0 + jnp.exp(-jnp.abs(x)))


def _iota(shape, dim):
    return lax.broadcasted_iota(jnp.int32, shape, dim)


class _Stream:
    def __init__(self, nb, L, latent):
        self.nb, self.L, self.latent = nb, L, latent
        self.rows = nb * L

    def mod_row(self, tile, tile_rows):
        if not self.latent:
            return 0
        return 1 + (tile * tile_rows) // self.L


PROMPT = _Stream(BATCH, SEQ, False)
LATENT = _Stream(DEC_BATCH, DEC_SEQ, True)


MOD_TN = 768


def _mod_kernel(cond_ref, w_ref, b_ref, o_ref):
    c = _silu(cond_ref[...])
    o_ref[0] = _dot3(c, w_ref[0]) + b_ref[0]


def _modulation(cond, w_mod, b_mod):
    n = 6 * D_MODEL
    return pl.pallas_call(
        _mod_kernel,
        grid=(DEPTH, n // MOD_TN),
        in_specs=[pl.BlockSpec((MOD_ROWS, D_MODEL), lambda l, j: (0, 0)),
                  pl.BlockSpec((1, D_MODEL, MOD_TN), lambda l, j: (l, 0, j)),
                  pl.BlockSpec((1, 1, MOD_TN), lambda l, j: (l, 0, j))],
        out_specs=pl.BlockSpec((1, MOD_ROWS, MOD_TN), lambda l, j: (l, 0, j)),
        out_shape=jax.ShapeDtypeStruct((DEPTH, MOD_ROWS, n), F32),
        compiler_params=_cparams("parallel", "parallel"),
        name="modulation",
    )(cond, w_mod, b_mod.reshape(DEPTH, 1, n))


ZP_TM = 1024
ZP_TN = 256


def _zproj_kernel(x_ref, mod_ref, w_ref, wg_ref, z_ref, zg_ref, h_scr):
    j = pl.program_id(1)

    @pl.when(j == 0)
    def _():
        m = mod_ref[0, 0]
        shift1 = m[:, 0:D_MODEL]
        scale1 = m[:, D_MODEL:2 * D_MODEL]
        h = x_ref[...] * (1.0 + scale1) + shift1
        h_scr[...] = h.astype(BF16)
        zg_ref[...] = _dot3(h, wg_ref[0])

    z_ref[...] = jnp.dot(h_scr[...], w_ref[0].astype(BF16), preferred_element_type=F32)


def _zproj(st, x, mod4, w_in, w_gate_cols, l):
    return pl.pallas_call(
        _zproj_kernel,
        grid=(st.rows // ZP_TM, D_Z // ZP_TN),
        in_specs=[pl.BlockSpec((ZP_TM, D_MODEL), lambda i, j: (i, 0)),
                  pl.BlockSpec((1, 1, 1, 6 * D_MODEL), lambda i, j: (l, st.mod_row(i, ZP_TM), 0, 0)),
                  pl.BlockSpec((1, D_MODEL, ZP_TN), lambda i, j: (l, 0, j)),
                  pl.BlockSpec((1, D_MODEL, N_GATE), lambda i, j: (l, 0, 0))],
        out_specs=[pl.BlockSpec((ZP_TM, ZP_TN), lambda i, j: (i, j)),
                   pl.BlockSpec((ZP_TM, N_GATE), lambda i, j: (i, 0))],
        out_shape=[jax.ShapeDtypeStruct((st.rows, D_Z), F32),
                   jax.ShapeDtypeStruct((st.rows, N_GATE), F32)],
        scratch_shapes=[pltpu.VMEM((ZP_TM, D_MODEL), BF16)],
        compiler_params=_cparams("parallel", "arbitrary"),
        name="zproj",
    )(x, mod4, w_in, w_gate_cols)


OP_TM = 512


def _layer_norm(v, g, b):
    mu = jnp.mean(v, axis=-1, keepdims=True)
    d = v - mu
    var = jnp.mean(d * d, axis=-1, keepdims=True)
    return d * lax.rsqrt(var + LN_EPS) * g + b


def _route(scores, b_router):
    sel = scores + b_router
    s = [sel[:, e:e + 1] for e in range(N_EXPERTS)]
    in_top2 = []
    for g in range(N_EXPERT_GROUPS):
        for i in range(EXPERTS_PER_GROUP):
            e = g * EXPERTS_PER_GROUP + i
            cnt = jnp.zeros_like(s[e])
            for jj in range(EXPERTS_PER_GROUP):
                if jj == i:
                    continue
                o = g * EXPERTS_PER_GROUP + jj
                beats = (s[o] > s[e]) if jj > i else (s[o] >= s[e])
                cnt = cnt + jnp.where(beats, 1.0, 0.0)
            in_top2.append(cnt < 2.0)
    grp = []
    for g in range(N_EXPERT_GROUPS):
        tot = jnp.zeros_like(s[0])
        for i in range(EXPERTS_PER_GROUP):
            e = g * EXPERTS_PER_GROUP + i
            tot = tot + jnp.where(in_top2[e], s[e], 0.0)
        grp.append(tot)
    lane = _iota(scores.shape, 1)
    keep = jnp.zeros(scores.shape, F32)
    for g in range(N_EXPERT_GROUPS):
        cnt = jnp.zeros_like(s[0])
        for o in range(N_EXPERT_GROUPS):
            if o == g:
                continue
            beats = (grp[o] > grp[g]) if o > g else (grp[o] >= grp[g])
            cnt = cnt + jnp.where(beats, 1.0, 0.0)
        best = cnt < 1.0
        for i in range(EXPERTS_PER_GROUP):
            e = g * EXPERTS_PER_GROUP + i
            on = jnp.where(best, jnp.where(in_top2[e], 1.0, 0.0), 0.0)
            keep = keep + jnp.where(lane == e, on, 0.0)
    picked = scores * keep
    return picked / jnp.sum(picked, axis=-1, keepdims=True), keep


def _oproj_kernel(y0_ref, y1_ref, y2_ref, y3_ref, w_ref, x_ref, mod_ref, g_ref, b_ref, wr_ref, br_ref,
                  x1_ref, h2_ref, gate_ref, keep_ref):
    acc = None
    for k, yr in enumerate((y0_ref, y1_ref, y2_ref, y3_ref)):
        part = jnp.dot(yr[...], w_ref[0, k * D_GROUP:(k + 1) * D_GROUP, :], preferred_element_type=F32)
        acc = part if acc is None else acc + part
    m = mod_ref[0, 0]
    gate1 = m[:, 2 * D_MODEL:3 * D_MODEL]
    shift2 = m[:, 3 * D_MODEL:4 * D_MODEL]
    scale2 = m[:, 4 * D_MODEL:5 * D_MODEL]
    x1 = _layer_norm(DEEPNORM_ALPHA * x_ref[...] + gate1 * acc, g_ref[0], b_ref[0])
    x1_ref[...] = x1
    h2 = x1 * (1.0 + scale2) + shift2
    h2_ref[...] = h2.astype(BF16)
    scores = _sigmoid(_dot3(h2, wr_ref[...]))
    gate_ref[...], keep_ref[...] = _route(scores, br_ref[...])


def _oproj(st, ys, x, mod4, w_out_b, ln_g, ln_b, w_router, b_router, l):
    row = lambda i: (i, 0)
    return pl.pallas_call(
        _oproj_kernel,
        grid=(st.rows // OP_TM,),
        in_specs=[pl.BlockSpec((OP_TM, D_GROUP), row)] * 4 + [
            pl.BlockSpec((1, D_MODEL, D_MODEL), lambda i: (l, 0, 0)),
            pl.BlockSpec((OP_TM, D_MODEL), row),
            pl.BlockSpec((1, 1, 1, 6 * D_MODEL), lambda i: (l, st.mod_row(i, OP_TM), 0, 0)),
            pl.BlockSpec((1, 1, D_MODEL), lambda i: (l, 0, 0)),
            pl.BlockSpec((1, 1, D_MODEL), lambda i: (l, 0, 0)),
            pl.BlockSpec((D_MODEL, N_EXPERTS), lambda i: (0, 0)),
            pl.BlockSpec((1, N_EXPERTS), lambda i: (0, 0))],
        out_specs=[pl.BlockSpec((OP_TM, D_MODEL), row),
                   pl.BlockSpec((OP_TM, D_MODEL), row),
                   pl.BlockSpec((OP_TM, N_EXPERTS), row),
                   pl.BlockSpec((OP_TM, N_EXPERTS), row)],
        out_shape=[jax.ShapeDtypeStruct((st.rows, D_MODEL), F32),
                   jax.ShapeDtypeStruct((st.rows, D_MODEL), BF16),
                   jax.ShapeDtypeStruct((st.rows, N_EXPERTS), F32),
                   jax.ShapeDtypeStruct((st.rows, N_EXPERTS), F32)],
        compiler_params=_cparams("parallel"),
        name="oproj_ln1_router",
    )(*ys, w_out_b, x, mod4, ln_g.reshape(DEPTH, 1, D_MODEL), ln_b.reshape(DEPTH, 1, D_MODEL),
      w_router, b_router.reshape(1, N_EXPERTS))


LN_TM = 512


GS_BLK = 1024
GS_CAP = 320
GS_ALIGN = 2 * SUB
GS_COLS = 512
GS_STEP = 256
GS_VMEM_LIMIT = 60 * 1024 * 1024
GS_ROWS = pl.cdiv(GS_BLK + N_EXPERT_GROUPS * GS_ALIGN + GS_CAP, GS_COLS) * GS_COLS


def _moe_sorted_kernel(h_ref, gate_ref, keep_ref, wg_ref, wu_ref, wd_ref, f_ref, xs_scr, gs_scr, pt_scr, acc_scr,
                       seg_smem):
    e = pl.program_id(1)

    @pl.when(e == 0)
    def _():
        keep = keep_ref[...]
        lane = _iota(keep.shape, 1)
        member = []
        for g in range(N_EXPERT_GROUPS):
            in_g = (lane >= g * EXPERTS_PER_GROUP) & (lane < (g + 1) * EXPERTS_PER_GROUP)
            member.append(jnp.minimum(jnp.sum(jnp.where(in_g, keep, 0.0), axis=-1, keepdims=True), 1.0))
        lane4 = _iota((GS_BLK, LANE), 1)
        onehot = jnp.zeros((GS_BLK, LANE), F32)
        for g in range(N_EXPERT_GROUPS):
            onehot = onehot + jnp.where(lane4 == g, member[g], 0.0)
        counts = jnp.sum(onehot, axis=0, keepdims=True)
        starts = []
        start = jnp.zeros((1, 1), F32)
        for g in range(N_EXPERT_GROUPS):
            cnt_g = counts[:, g:g + 1]
            starts.append(start)
            seg_smem[2 * g] = jnp.sum(start).astype(jnp.int32)
            seg_smem[2 * g + 1] = jnp.sum(cnt_g).astype(jnp.int32)
            start = start + jnp.ceil(cnt_g * (1.0 / GS_ALIGN)) * GS_ALIGN
        onehot_b = onehot.astype(BF16)
        for r0 in range(0, GS_BLK, GS_STEP):
            tri = jnp.where(_iota((GS_STEP, GS_BLK), 1) <= _iota((GS_STEP, GS_BLK), 0) + r0, 1.0, 0.0).astype(BF16)
            csum = jnp.dot(tri, onehot_b, preferred_element_type=F32)
            dest = jnp.zeros((GS_STEP, 1), F32)
            for g in range(N_EXPERT_GROUPS):
                dest = dest + member[g][r0:r0 + GS_STEP] * (starts[g] + csum[:, g:g + 1] - 1.0)
            pt_scr[r0:r0 + GS_STEP, :] = jnp.where(
                _iota((GS_STEP, GS_ROWS), 1) == dest.astype(jnp.int32), 1.0, 0.0).astype(BF16)
        for c0 in range(0, GS_ROWS, GS_COLS):
            pt_c = pt_scr[:, c0:c0 + GS_COLS]
            xs_scr[c0:c0 + GS_COLS, :] = _dg(pt_c, h_ref[...], TN).astype(BF16)
            gs_scr[c0:c0 + GS_COLS, :] = _dot_sel_l(pt_c, gate_ref[...], TN)
        acc_scr[...] = jnp.zeros_like(acc_scr)

    g = e // EXPERTS_PER_GROUP
    seg_start = seg_smem[2 * g]
    seg_len = seg_smem[2 * g + 1]
    wg = wg_ref[0, 0]
    wu = wu_ref[0, 0]
    wd = wd_ref[0, 0]

    def window(c, carry):
        rows = pl.ds(pl.multiple_of(seg_start + c * GS_CAP, GS_ALIGN), GS_CAP)
        x = xs_scr[rows, :]
        a = jnp.dot(x, wg, preferred_element_type=F32)
        u = jnp.dot(x, wu, preferred_element_type=F32)
        gates = gs_scr[rows, :]
        ge = jnp.sum(jnp.where(_iota(gates.shape, 1) == e, gates, 0.0), axis=-1, keepdims=True)
        hid = (_silu(a) * u * ge).astype(BF16)
        acc_scr[rows, :] += jnp.dot(hid, wd, preferred_element_type=F32)
        return carry

    lax.fori_loop(0, (seg_len + GS_CAP - 1) // GS_CAP, window, 0)

    @pl.when(e == N_EXPERTS - 1)
    def _():
        pt = pt_scr[...]
        for c0 in range(0, D_MODEL, GS_COLS):
            acc_b = acc_scr[:, c0:c0 + GS_COLS].astype(BF16)
            f_ref[:, c0:c0 + GS_COLS] = jnp.dot(pt, acc_b, preferred_element_type=F32).astype(BF16)


def _moe_sorted(st, h2, gates, keep, wg_b, wu_b, wd_b, l):
    row = lambda i, e: (i, 0)
    wspec = lambda a, b: pl.BlockSpec((1, 1, a, b), lambda i, e: (l, e, 0, 0))
    return pl.pallas_call(
        _moe_sorted_kernel,
        grid=(st.rows // GS_BLK, N_EXPERTS),
        in_specs=[pl.BlockSpec((GS_BLK, D_MODEL), row, pipeline_mode=pl.Buffered(1)),
                  pl.BlockSpec((GS_BLK, N_EXPERTS), row), pl.BlockSpec((GS_BLK, N_EXPERTS), row),
                  wspec(D_MODEL, D_EXPERT), wspec(D_MODEL, D_EXPERT), wspec(D_EXPERT, D_MODEL)],
        out_specs=pl.BlockSpec((GS_BLK, D_MODEL), row),
        out_shape=jax.ShapeDtypeStruct((st.rows, D_MODEL), BF16),
        scratch_shapes=[pltpu.VMEM((GS_ROWS, D_MODEL), BF16), pltpu.VMEM((GS_ROWS, N_EXPERTS), F32),
                        pltpu.VMEM((GS_BLK, GS_ROWS), BF16), pltpu.VMEM((GS_ROWS, D_MODEL), F32),
                        pltpu.SMEM((2 * N_EXPERT_GROUPS,), jnp.int32)],
        compiler_params=pltpu.CompilerParams(dimension_semantics=("parallel", "arbitrary"),
                                             vmem_limit_bytes=GS_VMEM_LIMIT),
        name="moe_group_sorted",
    )(h2, gates, keep, wg_b, wu_b, wd_b)


def _ln2f_kernel(x_ref, f_ref, mod_ref, g_ref, b_ref, o_ref):
    gate2 = mod_ref[0, 0][:, 5 * D_MODEL:6 * D_MODEL]
    o_ref[...] = _layer_norm(DEEPNORM_ALPHA * x_ref[...] + gate2 * f_ref[...].astype(F32), g_ref[0], b_ref[0])


def _ln2f(st, x1, f, mod4, ln_g, ln_b, l):
    row = lambda i: (i, 0)
    return pl.pallas_call(
        _ln2f_kernel,
        grid=(st.rows // LN_TM,),
        in_specs=[pl.BlockSpec((LN_TM, D_MODEL), row),
                  pl.BlockSpec((LN_TM, D_MODEL), row),
                  pl.BlockSpec((1, 1, 1, 6 * D_MODEL), lambda i: (l, st.mod_row(i, LN_TM), 0, 0)),
                  pl.BlockSpec((1, 1, D_MODEL), lambda i: (l, 0, 0)),
                  pl.BlockSpec((1, 1, D_MODEL), lambda i: (l, 0, 0))],
        out_specs=pl.BlockSpec((LN_TM, D_MODEL), row),
        out_shape=jax.ShapeDtypeStruct((st.rows, D_MODEL), F32),
        compiler_params=_cparams("parallel"),
        name="ln2",
    )(x1, f, mod4, ln_g.reshape(DEPTH, 1, D_MODEL), ln_b.reshape(DEPTH, 1, D_MODEL))


S5_TC = 64
S5_SEQS = SUB
S5_Q = 4
S5_QS = S5_GROUPS // S5_Q * S5_STATE
N_S5 = S5_GROUPS * S5_STATE


def _s5_prep_kernel(lr_ref, li_ref, ls_ref, br_ref, bi_ref, e_ref, lbr_ref, lbi_ref, bbr_ref, bbi_ref):
    lr = lr_ref[0]
    li = li_ref[0]
    dt = jnp.exp(ls_ref[0])
    mag = jnp.exp(lr * dt)
    ang = li * dt
    ar = mag * jnp.cos(ang)
    ai = mag * jnp.sin(ang)
    lbr_ref[0] = ar
    lbi_ref[0] = ai
    den = lr * lr + li * li
    nr = ar - 1.0
    cr = (nr * lr + ai * li) / den
    ci = (ai * lr - nr * li) / den
    cr = _dot_sel_l(e_ref[...], cr)
    ci = _dot_sel_l(e_ref[...], ci)
    bre = br_ref[0]
    bim = bi_ref[0]
    bbr_ref[0] = cr * bre - ci * bim
    bbi_ref[0] = cr * bim + ci * bre


def _s5_prep(lam_re, lam_im, log_step, b_re, b_im):
    d2 = DEPTH * 2
    g, p, h = S5_GROUPS, S5_STATE, S5_CH
    bt_re = jnp.swapaxes(b_re, -1, -2).reshape(d2, g * h, p)
    bt_im = jnp.swapaxes(b_im, -1, -2).reshape(d2, g * h, p)
    expand = jnp.asarray(np.kron(np.eye(g, dtype=np.float32), np.ones((h, 1), np.float32)))
    spec_gp = pl.BlockSpec((1, g, p), lambda i: (i, 0, 0))
    spec_b = pl.BlockSpec((1, g * h, p), lambda i: (i, 0, 0))
    lbr, lbi, bbr, bbi = pl.pallas_call(
        _s5_prep_kernel,
        grid=(d2,),
        in_specs=[spec_gp, spec_gp, pl.BlockSpec((1, g, 1), lambda i: (i, 0, 0)), spec_b, spec_b,
                  pl.BlockSpec((g * h, g), lambda i: (0, 0))],
        out_specs=[spec_gp, spec_gp, spec_b, spec_b],
        out_shape=[jax.ShapeDtypeStruct((d2, g, p), F32)] * 2 + [jax.ShapeDtypeStruct((d2, g * h, p), F32)] * 2,
        compiler_params=_cparams("parallel"),
        name="s5_prep",
    )(lam_re.reshape(d2, g, p), lam_im.reshape(d2, g, p), log_step.reshape(d2, g, 1), bt_re, bt_im, expand)
    return lbr, lbi, bbr, bbi


def _s5_block_params(lbr, lbi, bbr, bbi, c_re, c_im):
    d2 = DEPTH * 2
    gq = S5_GROUPS // S5_Q
    eye = jnp.eye(gq, dtype=F32)

    def b_blocks(b):
        b = b.reshape(d2, S5_Q, gq, S5_CH, S5_STATE)
        return jnp.einsum('djghp,gk->djghkp', b, eye).reshape(d2, S5_Q, gq * S5_CH, gq * S5_STATE)

    def c_blocks(c):
        c = c.reshape(d2, S5_Q, gq, S5_CH, S5_STATE)
        return jnp.einsum('djghp,gk->djgpkh', c, eye).reshape(d2, S5_Q, gq * S5_STATE, gq * S5_CH)

    bq = jnp.concatenate([b_blocks(bbr), b_blocks(bbi)], axis=-1)
    cq = jnp.concatenate([c_blocks(c_re), -c_blocks(c_im)], axis=-2)
    return bq, cq, lbr.reshape(d2, 1, N_S5), lbi.reshape(d2, 1, N_S5)


def _s5_scan_kernel(u_ref, bq_ref, cq_ref, lr_ref, li_ref, s0r_ref, s0i_ref, y_ref, sfr_ref, sfi_ref,
                    utb, bur, bui, ytb, sr_scr, si_scr):
    d = pl.program_id(1)
    c = pl.program_id(2)
    nrow = S5_TC * S5_SEQS
    cw = D_GROUP // S5_Q

    @pl.when(c == 0)
    def _():
        sr_scr[...] = s0r_ref[0, 0]
        si_scr[...] = s0i_ref[0, 0]

    for b in range(S5_SEQS):
        for j in range(S5_Q):
            utb[j, pl.ds(b, S5_TC, stride=S5_SEQS), :] = u_ref[b, :, j * cw:(j + 1) * cw]
    for j in range(S5_Q):
        bu = _dot1(utb[j], bq_ref[0, j])
        bur[:, j * S5_QS:(j + 1) * S5_QS] = bu[:, :S5_QS]
        bui[:, j * S5_QS:(j + 1) * S5_QS] = bu[:, S5_QS:]
    for j in range(S5_Q):
        sl = slice(j * S5_QS, (j + 1) * S5_QS)
        lam_r = jnp.broadcast_to(lr_ref[0][:, sl], (S5_SEQS, S5_QS))
        lam_i = jnp.broadcast_to(li_ref[0][:, sl], (S5_SEQS, S5_QS))

        def step(t, carry, sl=sl, lam_r=lam_r, lam_i=lam_i):
            sr, si = carry
            te = jnp.where(d == 0, t, S5_TC - 1 - t)
            rows = pl.ds(pl.multiple_of(te * S5_SEQS, S5_SEQS), S5_SEQS)
            nr = lam_r * sr - lam_i * si + bur[rows, sl]
            ni = lam_r * si + lam_i * sr + bui[rows, sl]
            bur[rows, sl] = nr
            bui[rows, sl] = ni
            return nr, ni

        sr, si = lax.fori_loop(0, S5_TC, step, (sr_scr[:, sl], si_scr[:, sl]))
        sr_scr[:, sl] = sr
        si_scr[:, sl] = si
    for j in range(S5_Q):
        sl = slice(j * S5_QS, (j + 1) * S5_QS)
        yj = _dot1(bur[:, sl], cq_ref[0, j, :S5_QS, :]) + _dot1(bui[:, sl], cq_ref[0, j, S5_QS:, :])
        ytb[j] = yj
    for b in range(S5_SEQS):
        for j in range(S5_Q):
            y_ref[0, b, :, j * cw:(j + 1) * cw] = ytb[j, pl.ds(b, S5_TC, stride=S5_SEQS), :]

    @pl.when(c == pl.num_programs(2) - 1)
    def _():
        sfr_ref[0, 0] = sr_scr[...]
        sfi_ref[0, 0] = si_scr[...]


def _s5_scan(st, z3, bq, cq, lamr, lami, s0r, s0i, l):
    ng, nc = st.nb // S5_SEQS, st.L // S5_TC
    nrow = S5_TC * S5_SEQS
    chunk = lambda d, c: c + d * (nc - 1 - 2 * c)
    par = lambda g, d, c: (2 * l + d, 0, 0, 0)
    st_spec = pl.BlockSpec((1, 1, S5_SEQS, N_S5), lambda g, d, c: (g, d, 0, 0))
    return pl.pallas_call(
        _s5_scan_kernel,
        grid=(ng, 2, nc),
        in_specs=[pl.BlockSpec((S5_SEQS, S5_TC, D_GROUP), lambda g, d, c: (g, chunk(d, c), 0)),
                  pl.BlockSpec((1, S5_Q, D_GROUP // S5_Q, 2 * S5_QS), par),
                  pl.BlockSpec((1, S5_Q, 2 * S5_QS, D_GROUP // S5_Q), par),
                  pl.BlockSpec((1, 1, N_S5), lambda g, d, c: (2 * l + d, 0, 0)),
                  pl.BlockSpec((1, 1, N_S5), lambda g, d, c: (2 * l + d, 0, 0)),
                  st_spec, st_spec],
        out_specs=[pl.BlockSpec((1, S5_SEQS, S5_TC, D_GROUP), lambda g, d, c: (d, g, chunk(d, c), 0)),
                   st_spec, st_spec],
        out_shape=[jax.ShapeDtypeStruct((2, st.nb, st.L, D_GROUP), F32),
                   jax.ShapeDtypeStruct((ng, 2, S5_SEQS, N_S5), F32),
                   jax.ShapeDtypeStruct((ng, 2, S5_SEQS, N_S5), F32)],
        scratch_shapes=[pltpu.VMEM((S5_Q, nrow, LANE), F32), pltpu.VMEM((nrow, N_S5), F32),
                        pltpu.VMEM((nrow, N_S5), F32), pltpu.VMEM((S5_Q, nrow, LANE), F32),
                        pltpu.VMEM((S5_SEQS, N_S5), F32), pltpu.VMEM((S5_SEQS, N_S5), F32)],
        compiler_params=_cparams("parallel", "arbitrary", "arbitrary"),
        name="s5_scan",
    )(z3, bq, cq, lamr, lami, s0r, s0i)


S5_TM = 512


def _gelu_tanh(x):
    return 0.5 * x * (1.0 + jnp.tanh(math.sqrt(2.0 / math.pi) * (x + 0.044715 * (x * x * x))))


def _s5_out_kernel(u_ref, yf_ref, yb_ref, d_ref, w_ref, o_ref):
    y = yf_ref[0] + yb_ref[0] + d_ref[0] * u_ref[...]
    y = _gelu_tanh(y)
    o_ref[...] = (y * _sigmoid(_dot1(y, w_ref[0]))).astype(BF16)


def _s5_out(st, z, ydir, d_skip, w_glu, l):
    return pl.pallas_call(
        _s5_out_kernel,
        grid=(st.rows // S5_TM,),
        in_specs=[pl.BlockSpec((S5_TM, D_GROUP), lambda i: (i, 0)),
                  pl.BlockSpec((1, S5_TM, D_GROUP), lambda i: (0, i, 0)),
                  pl.BlockSpec((1, S5_TM, D_GROUP), lambda i: (1, i, 0)),
                  pl.BlockSpec((1, 1, D_GROUP), lambda i: (l, 0, 0)),
                  pl.BlockSpec((1, D_GROUP, D_GROUP), lambda i: (l, 0, 0))],
        out_specs=pl.BlockSpec((S5_TM, D_GROUP), lambda i: (i, 0)),
        out_shape=jax.ShapeDtypeStruct((st.rows, D_GROUP), BF16),
        compiler_params=_cparams("parallel"),
        name="s5_out",
    )(z, ydir, ydir, d_skip.reshape(DEPTH, 1, D_GROUP), w_glu)


NA_SCALE = NA_HEAD ** -0.5
NA_ROWS = DEC_SEQ // GRID_W
NA_KH = min(NA_WIN_H, NA_ROWS)
NA_QCOL = NA_OFF // LANE
NA_KCOL = (NA_OFF + D_GROUP) // LANE
NA_VCOL = (NA_OFF + 2 * D_GROUP) // LANE
NA_NDR = 2 * NA_WIN_H - 1


def _na_ctx_kernel(q_ref, k_ref, v_ref, o_ref, nk_ref, nv_ref):
    for hh in range(2):
        sl = slice(hh * NA_HEAD, (hh + 1) * NA_HEAD)
        q = q_ref[0, :, sl]
        k = k_ref[0, :, sl]
        v = v_ref[0, :, sl]
        nk_ref[0, hh] = k
        nv_ref[0, hh] = v
        s = _dot1(q, k, NT) * NA_SCALE
        e = jnp.exp(s - jnp.max(s, axis=-1, keepdims=True))
        o = _dot1(e, v) / jnp.sum(e, axis=-1, keepdims=True)
        o_ref[0, :, sl] = o.astype(BF16)


def _na_ctx(st, z3):
    blk = lambda col: pl.BlockSpec((1, st.L, LANE), lambda b, p: (b, 0, col + p))
    kv_spec = pl.BlockSpec((1, 2, st.L, NA_HEAD), lambda b, p: (b, p, 0, 0))
    kv_shape = jax.ShapeDtypeStruct((st.nb, NA_HEADS, st.L, NA_HEAD), F32)
    return pl.pallas_call(
        _na_ctx_kernel,
        grid=(st.nb, NA_HEADS // 2),
        in_specs=[blk(NA_QCOL), blk(NA_KCOL), blk(NA_VCOL)],
        out_specs=[pl.BlockSpec((1, st.L, LANE), lambda b, p: (b, 0, p)), kv_spec, kv_spec],
        out_shape=[jax.ShapeDtypeStruct((st.nb, st.L, D_GROUP), BF16), kv_shape, kv_shape],
        compiler_params=_cparams("parallel", "parallel"),
        name="na_context",
    )(z3, z3, z3)


def _na_nbr_kernel(q_ref, k_ref, v_ref, ck_ref, cv_ref, bias_ref, o_ref):
    nloc = NA_KH * GRID_W
    qc = _iota((GRID_W, nloc), 0)
    kc = _iota((GRID_W, nloc), 1) % GRID_W
    cs = jnp.clip(qc - NA_WIN_W // 2, 0, GRID_W - NA_WIN_W)
    col_in = (kc >= cs) & (kc < cs + NA_WIN_W)
    for hh in range(2):
        sl = slice(hh * NA_HEAD, (hh + 1) * NA_HEAD)
        kctx = ck_ref[0, 0, hh]
        vctx = cv_ref[0, 0, hh]
        for r in range(NA_ROWS):
            rs = min(max(r - NA_KH // 2, 0), NA_ROWS - NA_KH)
            q = q_ref[0, r * GRID_W:(r + 1) * GRID_W, sl]
            k = k_ref[0, rs * GRID_W:rs * GRID_W + nloc, sl]
            v = v_ref[0, rs * GRID_W:rs * GRID_W + nloc, sl]
            off = (rs - r + NA_WIN_H - 1) * GRID_W
            s_loc = _dot1(q, k, NT) * NA_SCALE + bias_ref[hh, :, off:off + nloc]
            s_loc = jnp.where(col_in, s_loc, NEG_INF)
            s_ctx = _dot1(q, kctx, NT) * NA_SCALE
            m = jnp.maximum(jnp.max(s_loc, axis=-1, keepdims=True), jnp.max(s_ctx, axis=-1, keepdims=True))
            e_loc = jnp.exp(s_loc - m)
            e_ctx = jnp.exp(s_ctx - m)
            den = jnp.sum(e_loc, axis=-1, keepdims=True) + jnp.sum(e_ctx, axis=-1, keepdims=True)
            o = (_dot1(e_loc, v) + _dot1(e_ctx, vctx)) / den
            o_ref[0, r * GRID_W:(r + 1) * GRID_W, sl] = o.astype(BF16)


def _na_bias_table(rpb):
    qc = np.arange(GRID_W)[:, None]
    kc = np.arange(GRID_W)[None, :]
    dc = np.clip(kc - qc, -(NA_WIN_W - 1), NA_WIN_W - 1) + (NA_WIN_W - 1)
    t = rpb[:, :, :, dc]
    return jnp.transpose(t, (0, 1, 3, 2, 4)).reshape(DEPTH, NA_HEADS, GRID_W, NA_NDR * GRID_W)


def _na_nbr(st, z3, cache_k, cache_v, bias_tab, l):
    blk = lambda col: pl.BlockSpec((1, st.L, LANE), lambda b, p: (b, 0, col + p))
    cspec = pl.BlockSpec((1, 1, 2, PAST_LEN, NA_HEAD), lambda b, p: (b, l, p, 0, 0))
    return pl.pallas_call(
        _na_nbr_kernel,
        grid=(st.nb, NA_HEADS // 2),
        in_specs=[blk(NA_QCOL), blk(NA_KCOL), blk(NA_VCOL), cspec, cspec,
                  pl.BlockSpec((None, 2, GRID_W, NA_NDR * GRID_W), lambda b, p: (l, p, 0, 0))],
        out_specs=pl.BlockSpec((1, st.L, LANE), lambda b, p: (b, 0, p)),
        out_shape=jax.ShapeDtypeStruct((st.nb, st.L, D_GROUP), BF16),
        compiler_params=_cparams("parallel", "parallel"),
        name="na_neighbourhood",
    )(z3, z3, z3, cache_k, cache_v, bias_tab)


ML_COL = ML_OFF // LANE
ML_SCALE = ML_HEAD ** -0.5


def _log_sigmoid(x):
    return -_softplus(-x)


def _rope_tables(L):
    half = ML_HEAD // 2
    quarter = half // 2
    t = np.arange(L)
    inv_freq = ROPE_BASE ** (-np.arange(quarter, dtype=np.float32) / quarter)

    def tabs(pos):
        ang = pos.astype(np.float32)[:, None] * inv_freq[None, :].astype(np.float32)
        c, s = np.cos(ang), np.sin(ang)
        return np.concatenate([c, c], axis=-1), np.concatenate([-s, s], axis=-1)

    c1, s1 = tabs(t // GRID_W)
    c2, s2 = tabs(t % GRID_W)
    return (jnp.asarray(np.concatenate([c1, c2], axis=-1), F32),
            jnp.asarray(np.concatenate([s1, s2], axis=-1), F32))


def _ml_kernel(*refs, L, rotary, layer):
    nh = ML_HEADS
    ib_ref, fb_ref = refs[0], refs[1]
    q_refs, k_refs, v_refs, o_refs = (refs[2 + i * nh:2 + (i + 1) * nh] for i in range(4))
    (g_ref, cos_ref, sin_ref, c0_ref, n0_ref, m0_ref, lng_ref, lnb_ref,
     y_ref, cf_ref, nf_ref, mf_ref, qs, ks, hf, hb, c_scr, n_scr, m_scr) = refs[2 + 4 * nh:]
    T = ML_CHUNK
    nc = L // T

    if rotary:
        first = (_iota((L, ML_HEAD), 1) % (ML_HEAD // 2)) < ML_HEAD // 4

        def rope(x):
            quarter = ML_HEAD // 4
            partner = jnp.where(first, pltpu.roll(x, ML_HEAD - quarter, axis=1), pltpu.roll(x, quarter, axis=1))
            return x * cos_ref[...] + partner * sin_ref[...]
    else:
        rope = lambda x: x
    for h in range(nh):
        qs[h] = rope(q_refs[h][0]) * ML_SCALE
        ks[h] = rope(k_refs[h][0])
        for d in range(2):
            c_scr[d, h] = c0_ref[0, d, h]
            n_scr[d, h] = n0_ref[0, h, d:d + 1, :]
            m_scr[d, h] = m0_ref[0, h, d:d + 1, :]

    tj = _iota((T, T), 0)
    ts = _iota((T, T), 1)
    ones = jnp.ones((T, T), F32)
    upto_row = (tj <= ts, tj >= ts)
    upto_col = (ts <= tj, ts >= tj)

    def body(ci, carry):
        ch = []
        for d in range(2):
            cd = ci if d == 0 else nc - 1 - ci
            rows = pl.ds(pl.multiple_of(cd * T, T), T)
            for h in range(nh):
                g = g_ref[0, h, cd]
                ig = g[:, d:d + 1] + ib_ref[layer * 2 * nh + d * nh + h]
                fg = g[:, 2 + d:3 + d] + fb_ref[layer * 2 * nh + d * nh + h]
                lf = jnp.broadcast_to(_log_sigmoid(fg), (T, T))
                igb = jnp.broadcast_to(ig, (T, T))
                q = dict(d=d, h=h, rows=rows, ig=ig, qc=qs[h, rows, :], kc=ks[h, rows, :], vc=v_refs[h][0, rows, :])
                q['bcol'] = _dot_sel_l(jnp.where(upto_col[d], 1.0, 0.0), lf)
                q['brow_i'] = _dot_sel_l(ones, jnp.where(upto_row[d], lf, 0.0) - jnp.where(tj == ts, igb, 0.0))
                ch.append(q)
        for q in ch:
            q['qk'] = _dot1(q['qc'], q['kc'], NT)
            q['c_old'] = c_scr[q['d'], q['h']]
            q['qc_c'] = _dot1(q['qc'], q['c_old'])
        for q in ch:
            d, h = q['d'], q['h']
            dmat = jnp.where(upto_col[d], q['bcol'] - q['brow_i'], -jnp.inf)
            b1 = q['bcol'][:, 0:1]
            m_prev = m_scr[d, h]
            inter = b1 + m_prev
            m_t = jnp.maximum(inter, jnp.max(dmat, axis=-1, keepdims=True))
            qk = q['qk'] * jnp.exp(dmat - m_t)
            w_inter = jnp.exp(inter - m_t)
            n_old = n_scr[d, h]
            den = w_inter * jnp.sum(q['qc'] * n_old, axis=-1, keepdims=True) + jnp.sum(qk, axis=-1, keepdims=True)
            q['scale'] = 1.0 / jnp.maximum(jnp.abs(den), jnp.exp(-m_t))
            q['inter_part'] = w_inter * q['qc_c']
            q['intra'] = _dot1(qk, q['vc'])
            b_last = b1[T - 1:T, :] if d == 0 else b1[0:1, :]
            g_s = b_last - b1 + q['ig']
            m_new = jnp.maximum(b_last + m_prev, jnp.max(g_s, axis=0, keepdims=True))
            w_old = jnp.exp(b_last + m_prev - m_new)
            w_s = jnp.exp(g_s - m_new)
            q['c_new'] = _dot1(q['kc'], w_s * q['vc'], TN)
            q['w_old'] = w_old
            n_scr[d, h] = w_old * n_old + jnp.sum(w_s * q['kc'], axis=0, keepdims=True)
            m_scr[d, h] = m_new
        for q in ch:
            d, h = q['d'], q['h']
            hcur = (q['inter_part'] + q['intra']) * q['scale']
            if d == 0:
                hf[h, q['rows'], :] = hcur
            else:
                hb[h, q['rows'], :] = hcur
            c_scr[d, h] = q['w_old'] * q['c_old'] + q['c_new']
        return carry

    lax.fori_loop(0, nc, body, 0)

    for h in range(nh):
        hsum = hf[h] + hb[h]
        mu = jnp.mean(hsum, axis=-1, keepdims=True)
        dv = hsum - mu
        var = jnp.mean(dv * dv, axis=-1, keepdims=True)
        hn = dv * lax.rsqrt(var + ML_GN_EPS)
        y_ref[0, :, h * ML_HEAD:(h + 1) * ML_HEAD] = (
            _sigmoid(o_refs[h][0]) * (hn * lng_ref[0, :, h * ML_HEAD:(h + 1) * ML_HEAD]
                                      + lnb_ref[0, :, h * ML_HEAD:(h + 1) * ML_HEAD])).astype(BF16)
        for d in range(2):
            cf_ref[0, d, h] = c_scr[d, h]
            nf_ref[0, h, d:d + 1, :] = n_scr[d, h]
            mf_ref[0, h, d:d + 1, :] = m_scr[d, h]


def _ml_mixer(st, z3, zg, i_bias, f_bias, ln_g, ln_b, c0, n0, m0, l):
    L, nb, nh = st.L, st.nb, ML_HEADS
    nc = L // ML_CHUNK
    gates = zg.reshape(nb, L, 2, 2, nh).transpose(0, 4, 1, 2, 3).reshape(nb, nh, nc, ML_CHUNK, 4)
    cos_t, sin_t = _rope_tables(L)
    blk = lambda col: pl.BlockSpec((1, L, LANE), lambda b: (b, 0, col))
    head_blocks = [blk(ML_COL + part * nh + h) for part in range(4) for h in range(nh)]
    smem = pl.BlockSpec(memory_space=pltpu.SMEM)
    tab = pl.BlockSpec((L, ML_HEAD), lambda b: (0, 0))
    c_spec = pl.BlockSpec((1, 2, nh, ML_HEAD, ML_HEAD), lambda b: (b, 0, 0, 0, 0))
    n_spec = pl.BlockSpec((1, nh, 2, ML_HEAD), lambda b: (b, 0, 0, 0))
    m_spec = pl.BlockSpec((1, nh, 2, 1), lambda b: (b, 0, 0, 0))
    par = pl.BlockSpec((1, 1, D_GROUP), lambda b: (l, 0, 0))
    return pl.pallas_call(
        functools.partial(_ml_kernel, L=L, rotary=st.latent, layer=l),
        grid=(nb,),
        in_specs=[smem, smem] + head_blocks + [
            pl.BlockSpec((1, nh, nc, ML_CHUNK, 4), lambda b: (b, 0, 0, 0, 0)),
            tab, tab, c_spec, n_spec, m_spec, par, par],
        out_specs=[pl.BlockSpec((1, L, D_GROUP), lambda b: (b, 0, 0)), c_spec, n_spec, m_spec],
        out_shape=[jax.ShapeDtypeStruct((nb, L, D_GROUP), BF16),
                   jax.ShapeDtypeStruct((nb, 2, nh, ML_HEAD, ML_HEAD), F32),
                   jax.ShapeDtypeStruct((nb, nh, 2, ML_HEAD), F32),
                   jax.ShapeDtypeStruct((nb, nh, 2, 1), F32)],
        scratch_shapes=[pltpu.VMEM((nh, L, ML_HEAD), F32), pltpu.VMEM((nh, L, ML_HEAD), F32),
                        pltpu.VMEM((nh, L, ML_HEAD), F32), pltpu.VMEM((nh, L, ML_HEAD), F32),
                        pltpu.VMEM((2, nh, ML_HEAD, ML_HEAD), F32), pltpu.VMEM((2, nh, 1, ML_HEAD), F32),
                        pltpu.VMEM((2, nh, 1, 1), F32)],
        compiler_params=_cparams("parallel"),
        name="mlstm",
    )(i_bias.reshape(-1), f_bias.reshape(-1), *([z3] * (4 * nh)), gates, cos_t, sin_t, c0, n0, m0,
      ln_g.reshape(DEPTH, 1, D_GROUP), ln_b.reshape(DEPTH, 1, D_GROUP))


RW_TL = 256
RW_T = 64
RW_RCOL = RW_OFF // D_GROUP
RW_LCOL = (RW_OFF + 3 * D_GROUP) // (2 * LANE)
N_LORA = RW_LORA_W + RW_LORA_A + RW_LORA_G


def _head_ones():
    return jnp.asarray(np.kron(np.eye(RW_HEADS, dtype=np.float32), np.ones((RW_HEAD, RW_HEAD), np.float32)))


def _rw_front_kernel(r_ref, k_ref, v_ref, lo_ref, rp_ref, kp_ref, vp_ref, lp_ref, rn_ref, kn_ref, vn_ref, ln_ref,
                     mup_ref, mun_ref, w0_ref, w2_ref, a0_ref, a2_ref, g2_ref, kk_ref, ka_ref, hones_ref,
                     ro_ref, vo_ref, kko_ref, go_ref, kd_ref, bd_ref, lw_ref):
    t = pl.program_id(1)
    first = t == 0
    last = t == pl.num_programs(1) - 1
    row = _iota((RW_TL, 1), 0)

    def shift(x_ref, p_ref, n_ref, lo, hi):
        x = x_ref[0]
        prev_edge = jnp.where(first, 0.0, p_ref[0, SUB - 1:SUB, :])
        next_edge = jnp.where(last, 0.0, n_ref[0, 0:1, :])
        prev = jnp.where(row == 0, prev_edge, pltpu.roll(x, 1, axis=0))
        nxt = jnp.where(row == RW_TL - 1, next_edge, pltpu.roll(x, RW_TL - 1, axis=0))
        return x + mup_ref[0][:, lo:hi] * (prev - x) + mun_ref[0][:, lo:hi] * (nxt - x)

    g = D_GROUP
    r = shift(r_ref, rp_ref, rn_ref, 0, g)
    k = shift(k_ref, kp_ref, kn_ref, g, 2 * g)
    v = shift(v_ref, vp_ref, vn_ref, 2 * g, 3 * g)
    lo = shift(lo_ref, lp_ref, ln_ref, 3 * g, 3 * g + N_LORA)
    zw = lo[:, :RW_LORA_W]
    za = lo[:, RW_LORA_W:RW_LORA_W + RW_LORA_A]
    zg = lo[:, RW_LORA_W + RW_LORA_A:]
    ro_ref[0] = r
    vo_ref[0] = v
    kk = k * kk_ref[0]
    ssq = _dot_sel_r(kk * kk, hones_ref[...])
    kk = kk * lax.rsqrt(ssq + 1e-12)
    kko_ref[0] = kk
    go_ref[0] = _dot3(_sigmoid(zg), g2_ref[0])
    tw = jnp.tanh(zw)
    for d in range(2):
        w_log = -_softplus(-(w0_ref[0, d:d + 1, :] + _dot3(tw, w2_ref[0, d]))) - 0.5
        lw_ref[d, 0] = -jnp.exp(w_log)
        a = _sigmoid(a0_ref[0, d:d + 1, :] + _dot3(za, a2_ref[0, d]))
        kd_ref[d, 0] = k * (1.0 + (a - 1.0) * ka_ref[0])
        bd_ref[d, 0] = kk * a


def _rw_front(st, z3, mu_prev, mu_next, w0, w2, a0, a2, g2, k_k, k_a, l):
    nb, L = st.nb, st.L
    nt = L // RW_TL
    tpb = RW_TL // SUB
    main = lambda w, col: pl.BlockSpec((1, RW_TL, w), lambda b, t: (b, t, col))
    prev = lambda w, col: pl.BlockSpec((1, SUB, w), lambda b, t: (b, jnp.maximum(t * tpb - 1, 0), col))
    nxt = lambda w, col: pl.BlockSpec((1, SUB, w), lambda b, t: (b, jnp.minimum((t + 1) * tpb, L // SUB - 1), col))
    cols = [(D_GROUP, RW_RCOL), (D_GROUP, RW_RCOL + 1), (D_GROUP, RW_RCOL + 2), (2 * LANE, RW_LCOL)]
    lay = lambda *shape: pl.BlockSpec((1,) + shape, lambda b, t: (l,) + (0,) * len(shape))
    out1 = pl.BlockSpec((1, RW_TL, D_GROUP), lambda b, t: (b, t, 0))
    out2 = pl.BlockSpec((2, 1, RW_TL, D_GROUP), lambda b, t: (0, b, t, 0))
    s1 = jax.ShapeDtypeStruct((nb, L, D_GROUP), F32)
    s2 = jax.ShapeDtypeStruct((2, nb, L, D_GROUP), F32)
    return pl.pallas_call(
        _rw_front_kernel,
        grid=(nb, nt),
        in_specs=[main(*c) for c in cols] + [prev(*c) for c in cols] + [nxt(*c) for c in cols] + [
            lay(1, RW_IN), lay(1, RW_IN), lay(2, D_GROUP), lay(2, RW_LORA_W, D_GROUP), lay(2, D_GROUP),
            lay(2, RW_LORA_A, D_GROUP), lay(RW_LORA_G, D_GROUP), lay(1, D_GROUP), lay(1, D_GROUP),
            pl.BlockSpec((D_GROUP, D_GROUP), lambda b, t: (0, 0))],
        out_specs=[out1, out1, out1, out1, out2, out2, out2],
        out_shape=[s1, s1, s1, s1, s2, s2, s2],
        compiler_params=_cparams("parallel", "parallel"),
        name="rwkv_front",
    )(*([z3] * 12), mu_prev.reshape(DEPTH, 1, RW_IN), mu_next.reshape(DEPTH, 1, RW_IN), w0, w2, a0, a2, g2,
      k_k.reshape(DEPTH, 1, D_GROUP), k_a.reshape(DEPTH, 1, D_GROUP), _head_ones())


def _rw_core_kernel(rf_ref, vf_ref, kkf_ref, kdf_ref, bdf_ref, lwf_ref,
                    rb_ref, vb_ref, kkb_ref, kdb_ref, bdb_ref, lwb_ref, s0_ref,
                    yf_ref, yb_ref, sf_ref, s_scr):
    c = pl.program_id(1)
    T = RW_T

    @pl.when(c == 0)
    def _():
        s_scr[...] = s0_ref[0]

    tj = _iota((T, T), 0)
    ts = _iota((T, T), 1)
    dirs = ((rf_ref, vf_ref, kkf_ref, kdf_ref, bdf_ref, lwf_ref, yf_ref),
            (rb_ref, vb_ref, kkb_ref, kdb_ref, bdb_ref, lwb_ref, yb_ref))
    ch = []
    for d, (r_ref, v_ref, kk_ref, kd_ref, bd_ref, lw_ref, y_ref) in enumerate(dirs):
        if d == 0:
            incl = ts <= tj
            strict = ts < tj
            last = T - 1
        else:
            incl = ts >= tj
            strict = ts > tj
            last = 0
        lw = lw_ref[0, 0]
        cum = _dot_sel_l(jnp.where(incl, 1.0, 0.0), lw)
        w_in = jnp.exp(cum)
        w_inv = jnp.exp(-cum)
        w_ex = jnp.exp(cum - lw)
        kap_a = kk_ref[0] * w_ex
        bet_a = bd_ref[0, 0] * w_inv
        khat_a = kd_ref[0, 0] * w_inv
        rho_a = r_ref[0] * w_in
        w_tot = w_in[last:last + 1, :]
        v_a = v_ref[0]
        for h in range(RW_HEADS):
            sl = slice(h * RW_HEAD, (h + 1) * RW_HEAD)
            ch.append(dict(d=d, h=h, sl=sl, incl=incl, strict=strict, y_ref=y_ref, w_tot=w_tot[:, sl],
                           kap=kap_a[:, sl], bet=bet_a[:, sl], khat=khat_a[:, sl], rho=rho_a[:, sl], v=v_a[:, sl]))
    for q in ch:
        kr = jnp.concatenate([q['kap'], q['rho']], axis=0)
        bk = jnp.concatenate([q['bet'], q['khat']], axis=0)
        q['gram'] = _dot1(kr, bk, NT)
    for q in ch:
        gram = q.pop('gram')
        q['n'] = -jnp.where(q['strict'], gram[:T, :T], 0.0)
        l_k = jnp.where(q['strict'], gram[:T, T:], 0.0)
        q['m_b'] = jnp.where(q['incl'], gram[T:, :T], 0.0)
        m_k = jnp.where(q['incl'], gram[T:, T:], 0.0)
        q['lmv'] = _dot1(jnp.concatenate([l_k, m_k], axis=0), q['v'])
    for q in ch:
        q['x'] = jnp.concatenate([q['kap'], q['lmv'][:T]], axis=1)
    for lvl in range(6):
        mm = _dot3 if lvl < 3 else _dot1
        for q in ch:
            if lvl < 5:
                nx = mm(q['n'], jnp.concatenate([q['n'], q['x']], axis=1))
                q['n'], q['x'] = nx[:, :T], q['x'] + nx[:, T:]
            else:
                q['x'] = q['x'] + _dot1(q['n'], q['x'])
    for q in ch:
        q['p'] = _dot1(q['m_b'], q['x'])
        xb = _dot1(q['x'], q['bet'], TN)
        q['a_m'] = xb[:RW_HEAD]
        q['d_m'] = _dot1(q['v'], q['khat'], TN) - xb[RW_HEAD:]
    for q in ch:
        rho_p = q['rho'] - q['p'][:, :RW_HEAD]
        y_v = q['lmv'][T:] - q['p'][:, RW_HEAD:]
        s_old = s_scr[q['d'], q['h']]
        q['y_ref'][0, :, q['sl']] = _dot1(rho_p, s_old, NT) + y_v
        s_scr[q['d'], q['h']] = (s_old - _dot3(s_old, q['a_m']) + q['d_m']) * q['w_tot']

    @pl.when(c == pl.num_programs(1) - 1)
    def _():
        sf_ref[0] = s_scr[...]


def _rw_core(st, r, v, kk, kd, bd, lw, s0):
    nb, L = st.nb, st.L
    nc = L // RW_T
    f1 = pl.BlockSpec((1, RW_T, D_GROUP), lambda b, c: (b, c, 0))
    b1 = pl.BlockSpec((1, RW_T, D_GROUP), lambda b, c: (b, nc - 1 - c, 0))
    f2 = pl.BlockSpec((1, 1, RW_T, D_GROUP), lambda b, c: (0, b, c, 0))
    b2 = pl.BlockSpec((1, 1, RW_T, D_GROUP), lambda b, c: (1, b, nc - 1 - c, 0))
    s_spec = pl.BlockSpec((1, 2, RW_HEADS, RW_HEAD, RW_HEAD), lambda b, c: (b, 0, 0, 0, 0))
    ys = jax.ShapeDtypeStruct((nb, L, D_GROUP), F32)
    return pl.pallas_call(
        _rw_core_kernel,
        grid=(nb, nc),
        in_specs=[f1, f1, f1, f2, f2, f2, b1, b1, b1, b2, b2, b2, s_spec],
        out_specs=[f1, b1, s_spec],
        out_shape=[ys, ys, jax.ShapeDtypeStruct((nb, 2, RW_HEADS, RW_HEAD, RW_HEAD), F32)],
        scratch_shapes=[pltpu.VMEM((2, RW_HEADS, RW_HEAD, RW_HEAD), F32)],
        compiler_params=_cparams("parallel", "arbitrary"),
        name="rwkv_core",
    )(r, v, kk, kd, bd, lw, r, v, kk, kd, bd, lw, s0)


RWO_TM = 512


def _rw_out_kernel(yf_ref, yb_ref, r_ref, v_ref, kd0_ref, kd1_ref, g_ref, lng_ref, lnb_ref, rk_ref, hones_ref, o_ref):
    y = yf_ref[...] + yb_ref[...]
    inv = 1.0 / RW_HEAD
    mu = _dot_sel_r(y, hones_ref[...]) * inv
    dv = y - mu
    var = _dot_sel_r(dv * dv, hones_ref[...]) * inv
    yn = dv * lax.rsqrt(var + RW_GN_EPS) * lng_ref[0] + lnb_ref[0]
    kmean = 0.5 * (kd0_ref[0] + kd1_ref[0])
    bonus = _dot_sel_r(r_ref[...] * kmean * rk_ref[0], hones_ref[...]) * v_ref[...]
    o_ref[...] = ((yn + bonus) * g_ref[...]).astype(BF16)


def _rw_out(st, yf, yb, r, v, kd, g, ln_g, ln_b, r_k, l):
    rows = st.rows
    flat = lambda a: a.reshape(rows, D_GROUP)
    row = pl.BlockSpec((RWO_TM, D_GROUP), lambda i: (i, 0))
    lay = pl.BlockSpec((1, 1, D_GROUP), lambda i: (l, 0, 0))
    kd2 = kd.reshape(2, rows, D_GROUP)
    return pl.pallas_call(
        _rw_out_kernel,
        grid=(rows // RWO_TM,),
        in_specs=[row, row, row, row,
                  pl.BlockSpec((1, RWO_TM, D_GROUP), lambda i: (0, i, 0)),
                  pl.BlockSpec((1, RWO_TM, D_GROUP), lambda i: (1, i, 0)),
                  row, lay, lay, lay, pl.BlockSpec((D_GROUP, D_GROUP), lambda i: (0, 0))],
        out_specs=row,
        out_shape=jax.ShapeDtypeStruct((rows, D_GROUP), BF16),
        compiler_params=_cparams("parallel"),
        name="rwkv_out",
    )(flat(yf), flat(yb), flat(r), flat(v), kd2, kd2, flat(g), ln_g.reshape(DEPTH, 1, D_GROUP),
      ln_b.reshape(DEPTH, 1, D_GROUP), r_k.reshape(DEPTH, 1, D_GROUP), _head_ones())


def _trunk_layer(st, x, l, mod4, p, states, ctx_kv):
    nb, L = st.nb, st.L
    z, zg = _zproj(st, x, mod4, p['w_in'], p['w_gate_cols'], l)
    z3 = z.reshape(nb, L, D_Z)

    ydir, sfr, sfi = _s5_scan(st, z3, p['s5_bq'], p['s5_cq'], p['s5_lamr'], p['s5_lami'],
                              states['s5_re'], states['s5_im'], l)
    y_s5 = _s5_out(st, z, ydir.reshape(2, st.rows, D_GROUP), p['s5_d'], p['s5_w_glu'], l)

    r, v, kk, g, kd, bd, lw = _rw_front(st, z3, p['rw_mu_prev'], p['rw_mu_next'], p['rw_w0'], p['rw_w2'],
                                        p['rw_a0'], p['rw_a2'], p['rw_g2'], p['rw_k_k'], p['rw_k_a'], l)
    yf, yb, rw_s = _rw_core(st, r, v, kk, kd, bd, lw, states['rw'])
    y_rw = _rw_out(st, yf, yb, r, v, kd, g, p['rw_ln_g'], p['rw_ln_b'], p['rw_r_k'], l)

    if ctx_kv is None:
        y_na, nk, nv = _na_ctx(st, z3)
    else:
        y_na = _na_nbr(st, z3, ctx_kv[0], ctx_kv[1], p['na_bias'], l)
        nk = nv = None

    y_ml, ml_c, ml_n, ml_m = _ml_mixer(st, z3, zg, p['ml_i_bias'], p['ml_f_bias'], p['ml_ln_g'], p['ml_ln_b'],
                                       states['ml_c'], states['ml_n'], states['ml_m'], l)

    ys = (y_s5, y_rw, y_na.reshape(st.rows, D_GROUP), y_ml.reshape(st.rows, D_GROUP))
    tail = _oproj(st, ys, x, mod4, p['w_out'], p['ln1_g'], p['ln1_b'], p['w_router'], p['b_router'], l)
    return tail, (nk, nv, sfr, sfi, rw_s, ml_c, ml_n, ml_m)


def _moe_and_ln2(tails, mod4, p, l):
    xs = []
    for st, (x1, h2, gates, keep) in zip((PROMPT, LATENT), tails):
        f = _moe_sorted(st, h2, gates, keep, p['moe_w_gate'], p['moe_w_up'], p['moe_w_down'], l)
        xs.append(_ln2f(st, x1, f, mod4, p['ln2_g'], p['ln2_b'], l))
    return xs


def kernel(x_prompt, x_sample, cache_nat_k, cache_nat_v, state_s5_re, state_s5_im, state_rwkv, state_mlstm_c, state_mlstm_n, state_mlstm_m, c, c_ctx, w_mod, b_mod, w_in, w_out, s5_lam_re, s5_lam_im, s5_log_step, s5_b_re, s5_b_im, s5_c_re, s5_c_im, s5_d, s5_w_glu, rw_mu_prev, rw_mu_next, rw_w0, rw_w2, rw_a0, rw_a2, rw_g2, rw_k_k, rw_k_a, rw_r_k, rw_ln_g, rw_ln_b, na_rpb, ml_i_bias, ml_f_bias, ml_ln_g, ml_ln_b, ln1_g, ln1_b, ln2_g, ln2_b, w_router, b_router, moe_w_gate, moe_w_up, moe_w_down):
    dt = x_prompt.dtype
    cond = jnp.concatenate([c_ctx[None, :], c, jnp.zeros((MOD_ROWS - 1 - DEC_BATCH, D_MODEL), F32)], axis=0)
    mod4 = _modulation(cond, w_mod, b_mod).reshape(DEPTH, MOD_ROWS, 1, 6 * D_MODEL)

    lbr, lbi, bbr, bbi = _s5_prep(s5_lam_re, s5_lam_im, s5_log_step, s5_b_re, s5_b_im)
    s5_bq, s5_cq, s5_lamr, s5_lami = _s5_block_params(lbr, lbi, bbr, bbi, s5_c_re, s5_c_im)
    p = dict(w_in=w_in, w_gate_cols=w_in[:, :, D_Z:], w_out=w_out.astype(BF16),
             s5_bq=s5_bq, s5_cq=s5_cq, s5_lamr=s5_lamr, s5_lami=s5_lami, s5_d=s5_d, s5_w_glu=s5_w_glu,
             rw_mu_prev=rw_mu_prev, rw_mu_next=rw_mu_next, rw_w0=rw_w0, rw_w2=rw_w2, rw_a0=rw_a0, rw_a2=rw_a2,
             rw_g2=rw_g2, rw_k_k=rw_k_k, rw_k_a=rw_k_a, rw_r_k=rw_r_k, rw_ln_g=rw_ln_g, rw_ln_b=rw_ln_b,
             na_bias=_na_bias_table(na_rpb), ml_i_bias=ml_i_bias, ml_f_bias=ml_f_bias, ml_ln_g=ml_ln_g,
             ml_ln_b=ml_ln_b, ln1_g=ln1_g, ln1_b=ln1_b, ln2_g=ln2_g, ln2_b=ln2_b, w_router=w_router,
             b_router=b_router, moe_w_gate=moe_w_gate.astype(BF16), moe_w_up=moe_w_up.astype(BF16),
             moe_w_down=moe_w_down.astype(BF16))

    gp = BATCH // S5_SEQS
    zero_states = dict(
        s5_re=jnp.zeros((gp, 2, S5_SEQS, N_S5), F32), s5_im=jnp.zeros((gp, 2, S5_SEQS, N_S5), F32),
        rw=jnp.zeros((BATCH, 2, RW_HEADS, RW_HEAD, RW_HEAD), F32),
        ml_c=jnp.zeros((BATCH, 2, ML_HEADS, ML_HEAD, ML_HEAD), F32),
        ml_n=jnp.zeros((BATCH, ML_HEADS, 2, ML_HEAD), F32), ml_m=jnp.zeros((BATCH, ML_HEADS, 2, 1), F32))

    xp = x_prompt.reshape(PROMPT.rows, D_MODEL)
    xs = x_sample.reshape(LATENT.rows, D_MODEL)
    outs = [[] for _ in range(8)]
    for l in range(DEPTH):
        tail_p, ctx_t = _trunk_layer(PROMPT, xp, l, mod4, p, zero_states, None)
        for acc, t in zip(outs, ctx_t):
            acc.append(t)
        lat_states = dict(
            s5_re=state_s5_re[:, l].reshape(DEC_BATCH, 2, N_S5).transpose(1, 0, 2)[None],
            s5_im=state_s5_im[:, l].reshape(DEC_BATCH, 2, N_S5).transpose(1, 0, 2)[None],
            rw=state_rwkv[:, l], ml_c=state_mlstm_c[:, l],
            ml_n=state_mlstm_n[:, l].transpose(0, 2, 1, 3), ml_m=state_mlstm_m[:, l].transpose(0, 2, 1)[..., None])
        tail_s, _ = _trunk_layer(LATENT, xs, l, mod4, p, lat_states, (cache_nat_k, cache_nat_v))
        xp, xs = _moe_and_ln2((tail_p, tail_s), mod4, p, l)

    nk, nv, s5r, s5i, rw, mc, mn, mm = [jnp.stack(t, axis=1) for t in outs]

    def s5_state(t):
        return t.transpose(0, 3, 1, 2, 4).reshape(BATCH, DEPTH, 2, S5_GROUPS, S5_STATE)

    return (xp.reshape(BATCH, SEQ, D_MODEL), xs.reshape(DEC_BATCH, DEC_SEQ, D_MODEL),
            nk, nv, s5_state(s5r).astype(dt), s5_state(s5i).astype(dt), rw.astype(dt), mc.astype(dt),
            mn.transpose(0, 1, 3, 2, 4).astype(dt), mm[..., 0].transpose(0, 1, 3, 2).astype(dt))
```

```python
import functools
import math

import numpy as np
import jax
import jax.numpy as jnp
from jax import lax
from jax.experimental import pallas as pl
from jax.experimental.pallas import tpu as pltpu

F32 = jnp.float32
BF16 = jnp.bfloat16

D_MODEL = 2048
BATCH = 16
SEQ = 256
DEPTH = 4
DEC_BATCH = 8
DEC_SEQ = 1024
PAST_LEN = 256
GRID_W = 64
D_GROUP = D_MODEL // 4
S5_CH = 16
S5_GROUPS = D_GROUP // S5_CH
S5_STATE = 64
RW_HEAD = 64
RW_HEADS = D_GROUP // RW_HEAD
RW_LORA_W = 64
RW_LORA_A = 64
RW_LORA_G = 128
RW_GN_EPS = 64e-5
NA_HEAD = 64
NA_HEADS = D_GROUP // NA_HEAD
NA_WIN_H = 8
NA_WIN_W = 16
NEG_INF = -1e30
ML_HEAD = 128
ML_HEADS = D_GROUP // ML_HEAD
ML_CHUNK = 256
ML_GN_EPS = 1e-5
ROPE_BASE = 10000.0
N_EXPERTS = 16
N_EXPERT_GROUPS = 4
EXPERTS_PER_GROUP = N_EXPERTS // N_EXPERT_GROUPS
D_EXPERT = 512
DEEPNORM_ALPHA = (2 * DEPTH) ** 0.25
LN_EPS = 1e-5
S5_IN = D_GROUP
RW_IN = 3 * D_GROUP + RW_LORA_W + RW_LORA_A + RW_LORA_G
NA_IN = 3 * D_GROUP
ML_IN = 4 * D_GROUP + 4 * ML_HEADS
D_IN = S5_IN + RW_IN + NA_IN + ML_IN
N_GATE = 4 * ML_HEADS
D_Z = D_IN - N_GATE
RW_OFF = S5_IN
NA_OFF = S5_IN + RW_IN
ML_OFF = NA_OFF + NA_IN
MOD_ROWS = 16
LANE = 128
SUB = 8

VMEM_LIMIT = 56 * 1024 * 1024


def _cparams(*sem):
    return pltpu.CompilerParams(dimension_semantics=sem, vmem_limit_bytes=VMEM_LIMIT)


def _dg(a, b, dims):
    return lax.dot_general(a, b, (dims, ((), ())), preferred_element_type=F32)


NN = ((1,), (0,))
NT = ((1,), (1,))
TN = ((0,), (0,))


def _dot1(a, b, dims=NN):
    return _dg(a.astype(BF16), b.astype(BF16), dims)


def _split(x):
    hi = x.astype(BF16)
    lo = (x - hi.astype(F32)).astype(BF16)
    return hi, lo


def _split3(x):
    hi = x.astype(BF16)
    r = x - hi.astype(F32)
    mid = r.astype(BF16)
    lo = (r - mid.astype(F32)).astype(BF16)
    return hi, mid, lo


def _dot3(a, b, dims=NN):
    ah, al = _split(a)
    bh, bl = _split(b)
    return _dg(ah, bh, dims) + (_dg(ah, bl, dims) + _dg(al, bh, dims))


def _dot_sel_l(sel, x, dims=NN):
    s = sel.astype(BF16)
    hi, mid, lo = _split3(x)
    return _dg(s, hi, dims) + (_dg(s, mid, dims) + _dg(s, lo, dims))


def _dot_sel_r(x, sel, dims=NN):
    s = sel.astype(BF16)
    hi, mid, lo = _split3(x)
    return _dg(hi, s, dims) + (_dg(mid, s, dims) + _dg(lo, s, dims))


def _sigmoid(x):
    return 1.0 / (1.0 + jnp.exp(-x))


def _silu(x):
    return x * _sigmoid(x)


def _softplus(x):
    return jnp.maximum(x, 0.0) + jnp.log(1.0 + jnp.exp(-jnp.abs(x)))


def _iota(shape, dim):
    return lax.broadcasted_iota(jnp.int32, shape, dim)


class _Stream:
    def __init__(self, nb, L, latent):
        self.nb, self.L, self.latent = nb, L, latent
        self.rows = nb * L

    def mod_row(self, tile, tile_rows):
        if not self.latent:
            return 0
        return 1 + (tile * tile_rows) // self.L


PROMPT = _Stream(BATCH, SEQ, False)
LATENT = _Stream(DEC_BATCH, DEC_SEQ, True)


MOD_TN = 768


def _mod_kernel(cond_ref, w_ref, b_ref, o_ref):
    c = _silu(cond_ref[...])
    o_ref[0] = _dot3(c, w_ref[0]) + b_ref[0]


def _modulation(cond, w_mod, b_mod):
    n = 6 * D_MODEL
    return pl.pallas_call(
        _mod_kernel,
        grid=(DEPTH, n // MOD_TN),
        in_specs=[pl.BlockSpec((MOD_ROWS, D_MODEL), lambda l, j: (0, 0)),
                  pl.BlockSpec((1, D_MODEL, MOD_TN), lambda l, j: (l, 0, j)),
                  pl.BlockSpec((1, 1, MOD_TN), lambda l, j: (l, 0, j))],
        out_specs=pl.BlockSpec((1, MOD_ROWS, MOD_TN), lambda l, j: (l, 0, j)),
        out_shape=jax.ShapeDtypeStruct((DEPTH, MOD_ROWS, n), F32),
        compiler_params=_cparams("parallel", "parallel"),
        name="modulation",
    )(cond, w_mod, b_mod.reshape(DEPTH, 1, n))


ZP_TM = 1024
ZP_TN = 256


def _zproj_kernel(x_ref, mod_ref, w_ref, wg_ref, z_ref, zg_ref, h_scr):
    j = pl.program_id(1)

    @pl.when(j == 0)
    def _():
        m = mod_ref[0, 0]
        shift1 = m[:, 0:D_MODEL]
        scale1 = m[:, D_MODEL:2 * D_MODEL]
        h = x_ref[...] * (1.0 + scale1) + shift1
        h_scr[...] = h.astype(BF16)
        zg_ref[...] = _dot3(h, wg_ref[0])

    z_ref[...] = jnp.dot(h_scr[...], w_ref[0].astype(BF16), preferred_element_type=F32)


def _zproj(st, x, mod4, w_in, w_gate_cols, l):
    return pl.pallas_call(
        _zproj_kernel,
        grid=(st.rows // ZP_TM, D_Z // ZP_TN),
        in_specs=[pl.BlockSpec((ZP_TM, D_MODEL), lambda i, j: (i, 0)),
                  pl.BlockSpec((1, 1, 1, 6 * D_MODEL), lambda i, j: (l, st.mod_row(i, ZP_TM), 0, 0)),
                  pl.BlockSpec((1, D_MODEL, ZP_TN), lambda i, j: (l, 0, j)),
                  pl.BlockSpec((1, D_MODEL, N_GATE), lambda i, j: (l, 0, 0))],
        out_specs=[pl.BlockSpec((ZP_TM, ZP_TN), lambda i, j: (i, j)),
                   pl.BlockSpec((ZP_TM, N_GATE), lambda i, j: (i, 0))],
        out_shape=[jax.ShapeDtypeStruct((st.rows, D_Z), F32),
                   jax.ShapeDtypeStruct((st.rows, N_GATE), F32)],
        scratch_shapes=[pltpu.VMEM((ZP_TM, D_MODEL), BF16)],
        compiler_params=_cparams("parallel", "arbitrary"),
        name="zproj",
    )(x, mod4, w_in, w_gate_cols)


OP_TM = 512


def _layer_norm(v, g, b):
    mu = jnp.mean(v, axis=-1, keepdims=True)
    d = v - mu
    var = jnp.mean(d * d, axis=-1, keepdims=True)
    return d * lax.rsqrt(var + LN_EPS) * g + b


def _route(scores, b_router):
    sel = scores + b_router
    s = [sel[:, e:e + 1] for e in range(N_EXPERTS)]
    in_top2 = []
    for g in range(N_EXPERT_GROUPS):
        for i in range(EXPERTS_PER_GROUP):
            e = g * EXPERTS_PER_GROUP + i
            cnt = jnp.zeros_like(s[e])
            for jj in range(EXPERTS_PER_GROUP):
                if jj == i:
                    continue
                o = g * EXPERTS_PER_GROUP + jj
                beats = (s[o] > s[e]) if jj > i else (s[o] >= s[e])
                cnt = cnt + jnp.where(beats, 1.0, 0.0)
            in_top2.append(cnt < 2.0)
    grp = []
    for g in range(N_EXPERT_GROUPS):
        tot = jnp.zeros_like(s[0])
        for i in range(EXPERTS_PER_GROUP):
            e = g * EXPERTS_PER_GROUP + i
            tot = tot + jnp.where(in_top2[e], s[e], 0.0)
        grp.append(tot)
    lane = _iota(scores.shape, 1)
    keep = jnp.zeros(scores.shape, F32)
    for g in range(N_EXPERT_GROUPS):
        cnt = jnp.zeros_like(s[0])
        for o in range(N_EXPERT_GROUPS):
            if o == g:
                continue
            beats = (grp[o] > grp[g]) if o > g else (grp[o] >= grp[g])
            cnt = cnt + jnp.where(beats, 1.0, 0.0)
        best = cnt < 1.0
        for i in range(EXPERTS_PER_GROUP):
            e = g * EXPERTS_PER_GROUP + i
            on = jnp.where(best, jnp.where(in_top2[e], 1.0, 0.0), 0.0)
            keep = keep + jnp.where(lane == e, on, 0.0)
    picked = scores * keep
    return picked / jnp.sum(picked, axis=-1, keepdims=True), keep


def _oproj_kernel(y0_ref, y1_ref, y2_ref, y3_ref, w_ref, x_ref, mod_ref, g_ref, b_ref, wr_ref, br_ref,
                  x1_ref, h2_ref, gate_ref, keep_ref):
    acc = None
    for k, yr in enumerate((y0_ref, y1_ref, y2_ref, y3_ref)):
        part = jnp.dot(yr[...], w_ref[0, k * D_GROUP:(k + 1) * D_GROUP, :], preferred_element_type=F32)
        acc = part if acc is None else acc + part
    m = mod_ref[0, 0]
    gate1 = m[:, 2 * D_MODEL:3 * D_MODEL]
    shift2 = m[:, 3 * D_MODEL:4 * D_MODEL]
    scale2 = m[:, 4 * D_MODEL:5 * D_MODEL]
    x1 = _layer_norm(DEEPNORM_ALPHA * x_ref[...] + gate1 * acc, g_ref[0], b_ref[0])
    x1_ref[...] = x1
    h2 = x1 * (1.0 + scale2) + shift2
    h2_ref[...] = h2.astype(BF16)
    scores = _sigmoid(_dot3(h2, wr_ref[...]))
    gate_ref[...], keep_ref[...] = _route(scores, br_ref[...])


def _oproj(st, ys, x, mod4, w_out_b, ln_g, ln_b, w_router, b_router, l):
    row = lambda i: (i, 0)
    return pl.pallas_call(
        _oproj_kernel,
        grid=(st.rows // OP_TM,),
        in_specs=[pl.BlockSpec((OP_TM, D_GROUP), row)] * 4 + [
            pl.BlockSpec((1, D_MODEL, D_MODEL), lambda i: (l, 0, 0)),
            pl.BlockSpec((OP_TM, D_MODEL), row),
            pl.BlockSpec((1, 1, 1, 6 * D_MODEL), lambda i: (l, st.mod_row(i, OP_TM), 0, 0)),
            pl.BlockSpec((1, 1, D_MODEL), lambda i: (l, 0, 0)),
            pl.BlockSpec((1, 1, D_MODEL), lambda i: (l, 0, 0)),
            pl.BlockSpec((D_MODEL, N_EXPERTS), lambda i: (0, 0)),
            pl.BlockSpec((1, N_EXPERTS), lambda i: (0, 0))],
        out_specs=[pl.BlockSpec((OP_TM, D_MODEL), row),
                   pl.BlockSpec((OP_TM, D_MODEL), row),
                   pl.BlockSpec((OP_TM, N_EXPERTS), row),
                   pl.BlockSpec((OP_TM, N_EXPERTS), row)],
        out_shape=[jax.ShapeDtypeStruct((st.rows, D_MODEL), F32),
                   jax.ShapeDtypeStruct((st.rows, D_MODEL), BF16),
                   jax.ShapeDtypeStruct((st.rows, N_EXPERTS), F32),
                   jax.ShapeDtypeStruct((st.rows, N_EXPERTS), F32)],
        compiler_params=_cparams("parallel"),
        name="oproj_ln1_router",
    )(*ys, w_out_b, x, mod4, ln_g.reshape(DEPTH, 1, D_MODEL), ln_b.reshape(DEPTH, 1, D_MODEL),
      w_router, b_router.reshape(1, N_EXPERTS))


LN_TM = 512


GS_BLK = 1024
GS_CAP = 320
GS_ALIGN = 2 * SUB
GS_COLS = 512
GS_STEP = 256
GS_VMEM_LIMIT = 60 * 1024 * 1024
GS_ROWS = pl.cdiv(GS_BLK + N_EXPERT_GROUPS * GS_ALIGN + GS_CAP, GS_COLS) * GS_COLS


def _moe_sorted_kernel(h_ref, gate_ref, keep_ref, wg_ref, wu_ref, wd_ref, f_ref, xs_scr, gs_scr, pt_scr, acc_scr,
                       seg_smem):
    e = pl.program_id(1)

    @pl.when(e == 0)
    def _():
        keep = keep_ref[...]
        lane = _iota(keep.shape, 1)
        member = []
        for g in range(N_EXPERT_GROUPS):
            in_g = (lane >= g * EXPERTS_PER_GROUP) & (lane < (g + 1) * EXPERTS_PER_GROUP)
            member.append(jnp.minimum(jnp.sum(jnp.where(in_g, keep, 0.0), axis=-1, keepdims=True), 1.0))
        lane4 = _iota((GS_BLK, LANE), 1)
        onehot = jnp.zeros((GS_BLK, LANE), F32)
        for g in range(N_EXPERT_GROUPS):
            onehot = onehot + jnp.where(lane4 == g, member[g], 0.0)
        counts = jnp.sum(onehot, axis=0, keepdims=True)
        starts = []
        start = jnp.zeros((1, 1), F32)
        for g in range(N_EXPERT_GROUPS):
            cnt_g = counts[:, g:g + 1]
            starts.append(start)
            seg_smem[2 * g] = jnp.sum(start).astype(jnp.int32)
            seg_smem[2 * g + 1] = jnp.sum(cnt_g).astype(jnp.int32)
            start = start + jnp.ceil(cnt_g * (1.0 / GS_ALIGN)) * GS_ALIGN
        onehot_b = onehot.astype(BF16)
        for r0 in range(0, GS_BLK, GS_STEP):
            tri = jnp.where(_iota((GS_STEP, GS_BLK), 1) <= _iota((GS_STEP, GS_BLK), 0) + r0, 1.0, 0.0).astype(BF16)
            csum = jnp.dot(tri, onehot_b, preferred_element_type=F32)
            dest = jnp.zeros((GS_STEP, 1), F32)
            for g in range(N_EXPERT_GROUPS):
                dest = dest + member[g][r0:r0 + GS_STEP] * (starts[g] + csum[:, g:g + 1] - 1.0)
            pt_scr[r0:r0 + GS_STEP, :] = jnp.where(
                _iota((GS_STEP, GS_ROWS), 1) == dest.astype(jnp.int32), 1.0, 0.0).astype(BF16)
        for c0 in range(0, GS_ROWS, GS_COLS):
            pt_c = pt_scr[:, c0:c0 + GS_COLS]
            xs_scr[c0:c0 + GS_COLS, :] = _dg(pt_c, h_ref[...], TN).astype(BF16)
            gs_scr[c0:c0 + GS_COLS, :] = _dot_sel_l(pt_c, gate_ref[...], TN)
        acc_scr[...] = jnp.zeros_like(acc_scr)

    g = e // EXPERTS_PER_GROUP
    seg_start = seg_smem[2 * g]
    seg_len = seg_smem[2 * g + 1]
    wg = wg_ref[0, 0]
    wu = wu_ref[0, 0]
    wd = wd_ref[0, 0]

    def window(c, carry):
        rows = pl.ds(pl.multiple_of(seg_start + c * GS_CAP, GS_ALIGN), GS_CAP)
        x = xs_scr[rows, :]
        a = jnp.dot(x, wg, preferred_element_type=F32)
        u = jnp.dot(x, wu, preferred_element_type=F32)
        gates = gs_scr[rows, :]
        ge = jnp.sum(jnp.where(_iota(gates.shape, 1) == e, gates, 0.0), axis=-1, keepdims=True)
        hid = (_silu(a) * u * ge).astype(BF16)
        acc_scr[rows, :] += jnp.dot(hid, wd, preferred_element_type=F32)
        return carry

    lax.fori_loop(0, (seg_len + GS_CAP - 1) // GS_CAP, window, 0)

    @pl.when(e == N_EXPERTS - 1)
    def _():
        pt = pt_scr[...]
        for c0 in range(0, D_MODEL, GS_COLS):
            acc_b = acc_scr[:, c0:c0 + GS_COLS].astype(BF16)
            f_ref[:, c0:c0 + GS_COLS] = jnp.dot(pt, acc_b, preferred_element_type=F32).astype(BF16)


def _moe_sorted(st, h2, gates, keep, wg_b, wu_b, wd_b, l):
    row = lambda i, e: (i, 0)
    wspec = lambda a, b: pl.BlockSpec((1, 1, a, b), lambda i, e: (l, e, 0, 0))
    return pl.pallas_call(
        _moe_sorted_kernel,
        grid=(st.rows // GS_BLK, N_EXPERTS),
        in_specs=[pl.BlockSpec((GS_BLK, D_MODEL), row, pipeline_mode=pl.Buffered(1)),
                  pl.BlockSpec((GS_BLK, N_EXPERTS), row), pl.BlockSpec((GS_BLK, N_EXPERTS), row),
                  wspec(D_MODEL, D_EXPERT), wspec(D_MODEL, D_EXPERT), wspec(D_EXPERT, D_MODEL)],
        out_specs=pl.BlockSpec((GS_BLK, D_MODEL), row),
        out_shape=jax.ShapeDtypeStruct((st.rows, D_MODEL), BF16),
        scratch_shapes=[pltpu.VMEM((GS_ROWS, D_MODEL), BF16), pltpu.VMEM((GS_ROWS, N_EXPERTS), F32),
                        pltpu.VMEM((GS_BLK, GS_ROWS), BF16), pltpu.VMEM((GS_ROWS, D_MODEL), F32),
                        pltpu.SMEM((2 * N_EXPERT_GROUPS,), jnp.int32)],
        compiler_params=pltpu.CompilerParams(dimension_semantics=("parallel", "arbitrary"),
                                             vmem_limit_bytes=GS_VMEM_LIMIT),
        name="moe_group_sorted",
    )(h2, gates, keep, wg_b, wu_b, wd_b)


def _ln2f_kernel(x_ref, f_ref, mod_ref, g_ref, b_ref, o_ref):
    gate2 = mod_ref[0, 0][:, 5 * D_MODEL:6 * D_MODEL]
    o_ref[...] = _layer_norm(DEEPNORM_ALPHA * x_ref[...] + gate2 * f_ref[...].astype(F32), g_ref[0], b_ref[0])


def _ln2f(st, x1, f, mod4, ln_g, ln_b, l):
    row = lambda i: (i, 0)
    return pl.pallas_call(
        _ln2f_kernel,
        grid=(st.rows // LN_TM,),
        in_specs=[pl.BlockSpec((LN_TM, D_MODEL), row),
                  pl.BlockSpec((LN_TM, D_MODEL), row),
                  pl.BlockSpec((1, 1, 1, 6 * D_MODEL), lambda i: (l, st.mod_row(i, LN_TM), 0, 0)),
                  pl.BlockSpec((1, 1, D_MODEL), lambda i: (l, 0, 0)),
                  pl.BlockSpec((1, 1, D_MODEL), lambda i: (l, 0, 0))],
        out_specs=pl.BlockSpec((LN_TM, D_MODEL), row),
        out_shape=jax.ShapeDtypeStruct((st.rows, D_MODEL), F32),
        compiler_params=_cparams("parallel"),
        name="ln2",
    )(x1, f, mod4, ln_g.reshape(DEPTH, 1, D_MODEL), ln_b.reshape(DEPTH, 1, D_MODEL))


S5_TC = 128
S5_SEQS = SUB
S5_Q = 4
S5_QS = S5_GROUPS // S5_Q * S5_STATE
N_S5 = S5_GROUPS * S5_STATE


def _s5_prep_kernel(lr_ref, li_ref, ls_ref, br_ref, bi_ref, e_ref, lbr_ref, lbi_ref, bbr_ref, bbi_ref):
    lr = lr_ref[0]
    li = li_ref[0]
    dt = jnp.exp(ls_ref[0])
    mag = jnp.exp(lr * dt)
    ang = li * dt
    ar = mag * jnp.cos(ang)
    ai = mag * jnp.sin(ang)
    lbr_ref[0] = ar
    lbi_ref[0] = ai
    den = lr * lr + li * li
    nr = ar - 1.0
    cr = (nr * lr + ai * li) / den
    ci = (ai * lr - nr * li) / den
    cr = _dot_sel_l(e_ref[...], cr)
    ci = _dot_sel_l(e_ref[...], ci)
    bre = br_ref[0]
    bim = bi_ref[0]
    bbr_ref[0] = cr * bre - ci * bim
    bbi_ref[0] = cr * bim + ci * bre


def _s5_prep(lam_re, lam_im, log_step, b_re, b_im):
    d2 = DEPTH * 2
    g, p, h = S5_GROUPS, S5_STATE, S5_CH
    bt_re = jnp.swapaxes(b_re, -1, -2).reshape(d2, g * h, p)
    bt_im = jnp.swapaxes(b_im, -1, -2).reshape(d2, g * h, p)
    expand = jnp.asarray(np.kron(np.eye(g, dtype=np.float32), np.ones((h, 1), np.float32)))
    spec_gp = pl.BlockSpec((1, g, p), lambda i: (i, 0, 0))
    spec_b = pl.BlockSpec((1, g * h, p), lambda i: (i, 0, 0))
    lbr, lbi, bbr, bbi = pl.pallas_call(
        _s5_prep_kernel,
        grid=(d2,),
        in_specs=[spec_gp, spec_gp, pl.BlockSpec((1, g, 1), lambda i: (i, 0, 0)), spec_b, spec_b,
                  pl.BlockSpec((g * h, g), lambda i: (0, 0))],
        out_specs=[spec_gp, spec_gp, spec_b, spec_b],
        out_shape=[jax.ShapeDtypeStruct((d2, g, p), F32)] * 2 + [jax.ShapeDtypeStruct((d2, g * h, p), F32)] * 2,
        compiler_params=_cparams("parallel"),
        name="s5_prep",
    )(lam_re.reshape(d2, g, p), lam_im.reshape(d2, g, p), log_step.reshape(d2, g, 1), bt_re, bt_im, expand)
    return lbr, lbi, bbr, bbi


def _s5_block_params(lbr, lbi, bbr, bbi, c_re, c_im):
    d2 = DEPTH * 2
    gq = S5_GROUPS // S5_Q
    eye = jnp.eye(gq, dtype=F32)

    def b_blocks(b):
        b = b.reshape(d2, S5_Q, gq, S5_CH, S5_STATE)
        return jnp.einsum('djghp,gk->djghkp', b, eye).reshape(d2, S5_Q, gq * S5_CH, gq * S5_STATE)

    def c_blocks(c):
        c = c.reshape(d2, S5_Q, gq, S5_CH, S5_STATE)
        return jnp.einsum('djghp,gk->djgpkh', c, eye).reshape(d2, S5_Q, gq * S5_STATE, gq * S5_CH)

    bq = jnp.concatenate([b_blocks(bbr), b_blocks(bbi)], axis=-1)
    cq = jnp.concatenate([c_blocks(c_re), -c_blocks(c_im)], axis=-2)
    return bq, cq, lbr.reshape(d2, 1, N_S5), lbi.reshape(d2, 1, N_S5)


def _s5_scan_kernel(u_ref, bq_ref, cq_ref, lr_ref, li_ref, s0r_ref, s0i_ref, y_ref, sfr_ref, sfi_ref,
                    utb, bur, bui, ytb, sr_scr, si_scr):
    d = pl.program_id(1)
    c = pl.program_id(2)
    nrow = S5_TC * S5_SEQS
    cw = D_GROUP // S5_Q

    @pl.when(c == 0)
    def _():
        sr_scr[...] = s0r_ref[0, 0]
        si_scr[...] = s0i_ref[0, 0]

    for b in range(S5_SEQS):
        for j in range(S5_Q):
            utb[j, pl.ds(b, S5_TC, stride=S5_SEQS), :] = u_ref[b, :, j * cw:(j + 1) * cw]
    for j in range(S5_Q):
        bu = _dot1(utb[j], bq_ref[0, j])
        bur[:, j * S5_QS:(j + 1) * S5_QS] = bu[:, :S5_QS]
        bui[:, j * S5_QS:(j + 1) * S5_QS] = bu[:, S5_QS:]
    for j in range(S5_Q):
        sl = slice(j * S5_QS, (j + 1) * S5_QS)
        lam_r = jnp.broadcast_to(lr_ref[0][:, sl], (S5_SEQS, S5_QS))
        lam_i = jnp.broadcast_to(li_ref[0][:, sl], (S5_SEQS, S5_QS))

        def step(t, carry, sl=sl, lam_r=lam_r, lam_i=lam_i):
            sr, si = carry
            te = jnp.where(d == 0, t, S5_TC - 1 - t)
            rows = pl.ds(pl.multiple_of(te * S5_SEQS, S5_SEQS), S5_SEQS)
            nr = lam_r * sr - lam_i * si + bur[rows, sl]
            ni = lam_r * si + lam_i * sr + bui[rows, sl]
            bur[rows, sl] = nr
            bui[rows, sl] = ni
            return nr, ni

        sr, si = lax.fori_loop(0, S5_TC, step, (sr_scr[:, sl], si_scr[:, sl]))
        sr_scr[:, sl] = sr
        si_scr[:, sl] = si
    for j in range(S5_Q):
        sl = slice(j * S5_QS, (j + 1) * S5_QS)
        yj = _dot1(bur[:, sl], cq_ref[0, j, :S5_QS, :]) + _dot1(bui[:, sl], cq_ref[0, j, S5_QS:, :])
        ytb[j] = yj
    for b in range(S5_SEQS):
        for j in range(S5_Q):
            y_ref[0, b, :, j * cw:(j + 1) * cw] = ytb[j, pl.ds(b, S5_TC, stride=S5_SEQS), :]

    @pl.when(c == pl.num_programs(2) - 1)
    def _():
        sfr_ref[0, 0] = sr_scr[...]
        sfi_ref[0, 0] = si_scr[...]


def _s5_scan(st, z3, bq, cq, lamr, lami, s0r, s0i, l):
    ng, nc = st.nb // S5_SEQS, st.L // S5_TC
    nrow = S5_TC * S5_SEQS
    chunk = lambda d, c: c + d * (nc - 1 - 2 * c)
    par = lambda g, d, c: (2 * l + d, 0, 0, 0)
    st_spec = pl.BlockSpec((1, 1, S5_SEQS, N_S5), lambda g, d, c: (g, d, 0, 0))
    return pl.pallas_call(
        _s5_scan_kernel,
        grid=(ng, 2, nc),
        in_specs=[pl.BlockSpec((S5_SEQS, S5_TC, D_GROUP), lambda g, d, c: (g, chunk(d, c), 0)),
                  pl.BlockSpec((1, S5_Q, D_GROUP // S5_Q, 2 * S5_QS), par),
                  pl.BlockSpec((1, S5_Q, 2 * S5_QS, D_GROUP // S5_Q), par),
                  pl.BlockSpec((1, 1, N_S5), lambda g, d, c: (2 * l + d, 0, 0)),
                  pl.BlockSpec((1, 1, N_S5), lambda g, d, c: (2 * l + d, 0, 0)),
                  st_spec, st_spec],
        out_specs=[pl.BlockSpec((1, S5_SEQS, S5_TC, D_GROUP), lambda g, d, c: (d, g, chunk(d, c), 0)),
                   st_spec, st_spec],
        out_shape=[jax.ShapeDtypeStruct((2, st.nb, st.L, D_GROUP), F32),
                   jax.ShapeDtypeStruct((ng, 2, S5_SEQS, N_S5), F32),
                   jax.ShapeDtypeStruct((ng, 2, S5_SEQS, N_S5), F32)],
        scratch_shapes=[pltpu.VMEM((S5_Q, nrow, LANE), F32), pltpu.VMEM((nrow, N_S5), F32),
                        pltpu.VMEM((nrow, N_S5), F32), pltpu.VMEM((S5_Q, nrow, LANE), F32),
                        pltpu.VMEM((S5_SEQS, N_S5), F32), pltpu.VMEM((S5_SEQS, N_S5), F32)],
        compiler_params=_cparams("parallel", "arbitrary", "arbitrary"),
        name="s5_scan",
    )(z3, bq, cq, lamr, lami, s0r, s0i)


S5_TM = 512


def _gelu_tanh(x):
    return 0.5 * x * (1.0 + jnp.tanh(math.sqrt(2.0 / math.pi) * (x + 0.044715 * (x * x * x))))


def _s5_out_kernel(u_ref, yf_ref, yb_ref, d_ref, w_ref, o_ref):
    y = yf_ref[0] + yb_ref[0] + d_ref[0] * u_ref[...]
    y = _gelu_tanh(y)
    o_ref[...] = (y * _sigmoid(_dot1(y, w_ref[0]))).astype(BF16)


def _s5_out(st, z, ydir, d_skip, w_glu, l):
    return pl.pallas_call(
        _s5_out_kernel,
        grid=(st.rows // S5_TM,),
        in_specs=[pl.BlockSpec((S5_TM, D_GROUP), lambda i: (i, 0)),
                  pl.BlockSpec((1, S5_TM, D_GROUP), lambda i: (0, i, 0)),
                  pl.BlockSpec((1, S5_TM, D_GROUP), lambda i: (1, i, 0)),
                  pl.BlockSpec((1, 1, D_GROUP), lambda i: (l, 0, 0)),
                  pl.BlockSpec((1, D_GROUP, D_GROUP), lambda i: (l, 0, 0))],
        out_specs=pl.BlockSpec((S5_TM, D_GROUP), lambda i: (i, 0)),
        out_shape=jax.ShapeDtypeStruct((st.rows, D_GROUP), BF16),
        compiler_params=_cparams("parallel"),
        name="s5_out",
    )(z, ydir, ydir, d_skip.reshape(DEPTH, 1, D_GROUP), w_glu)


NA_SCALE = NA_HEAD ** -0.5
NA_ROWS = DEC_SEQ // GRID_W
NA_KH = min(NA_WIN_H, NA_ROWS)
NA_QCOL = NA_OFF // LANE
NA_KCOL = (NA_OFF + D_GROUP) // LANE
NA_VCOL = (NA_OFF + 2 * D_GROUP) // LANE
NA_NDR = 2 * NA_WIN_H - 1
NA_AHEAD = 1


def _na_ctx_kernel(q_ref, k_ref, v_ref, o_ref, nk_ref, nv_ref):
    for hh in range(2):
        sl = slice(hh * NA_HEAD, (hh + 1) * NA_HEAD)
        q = q_ref[0, :, sl]
        k = k_ref[0, :, sl]
        v = v_ref[0, :, sl]
        nk_ref[0, hh] = k
        nv_ref[0, hh] = v
        s = _dot1(q, k, NT) * NA_SCALE
        e = jnp.exp(s - jnp.max(s, axis=-1, keepdims=True))
        o = _dot1(e, v) / jnp.sum(e, axis=-1, keepdims=True)
        o_ref[0, :, sl] = o.astype(BF16)


def _na_ctx(st, z3):
    blk = lambda col: pl.BlockSpec((1, st.L, LANE), lambda b, p: (b, 0, col + p))
    kv_spec = pl.BlockSpec((1, 2, st.L, NA_HEAD), lambda b, p: (b, p, 0, 0))
    kv_shape = jax.ShapeDtypeStruct((st.nb, NA_HEADS, st.L, NA_HEAD), F32)
    return pl.pallas_call(
        _na_ctx_kernel,
        grid=(st.nb, NA_HEADS // 2),
        in_specs=[blk(NA_QCOL), blk(NA_KCOL), blk(NA_VCOL)],
        out_specs=[pl.BlockSpec((1, st.L, LANE), lambda b, p: (b, 0, p)), kv_spec, kv_spec],
        out_shape=[jax.ShapeDtypeStruct((st.nb, st.L, D_GROUP), BF16), kv_shape, kv_shape],
        compiler_params=_cparams("parallel", "parallel"),
        name="na_context",
    )(z3, z3, z3)


def _na_nbr_kernel(q_ref, k_ref, v_ref, ck_ref, cv_ref, bias_ref, o_ref):
    nloc = NA_KH * GRID_W
    qc = _iota((GRID_W, nloc), 0)
    kc = _iota((GRID_W, nloc), 1) % GRID_W
    cs = jnp.clip(qc - NA_WIN_W // 2, 0, GRID_W - NA_WIN_W)
    col_in = (kc >= cs) & (kc < cs + NA_WIN_W)

    def scores(hh, r):
        sl = slice(hh * NA_HEAD, (hh + 1) * NA_HEAD)
        rs = min(max(r - NA_KH // 2, 0), NA_ROWS - NA_KH)
        q = q_ref[0, r * GRID_W:(r + 1) * GRID_W, sl]
        k = k_ref[0, rs * GRID_W:rs * GRID_W + nloc, sl]
        off = (rs - r + NA_WIN_H - 1) * GRID_W
        s_loc = _dot1(q, k, NT) * NA_SCALE + bias_ref[hh, :, off:off + nloc]
        s_loc = jnp.where(col_in, s_loc, NEG_INF)
        s_ctx = _dot1(q, ck_ref[0, 0, hh], NT) * NA_SCALE
        return s_loc, s_ctx

    def finish(hh, r, s_loc, s_ctx):
        sl = slice(hh * NA_HEAD, (hh + 1) * NA_HEAD)
        rs = min(max(r - NA_KH // 2, 0), NA_ROWS - NA_KH)
        v = v_ref[0, rs * GRID_W:rs * GRID_W + nloc, sl]
        m = jnp.maximum(jnp.max(s_loc, axis=-1, keepdims=True), jnp.max(s_ctx, axis=-1, keepdims=True))
        e_loc = jnp.exp(s_loc - m)
        e_ctx = jnp.exp(s_ctx - m)
        den = jnp.sum(e_loc, axis=-1, keepdims=True) + jnp.sum(e_ctx, axis=-1, keepdims=True)
        o = (_dot1(e_loc, v) + _dot1(e_ctx, cv_ref[0, 0, hh])) / den
        o_ref[0, r * GRID_W:(r + 1) * GRID_W, sl] = o.astype(BF16)

    blocks = [(hh, r) for hh in range(2) for r in range(NA_ROWS)]
    pending = []
    for i, blk in enumerate(blocks):
        pending.append(scores(*blk))
        if i >= NA_AHEAD:
            finish(*blocks[i - NA_AHEAD], *pending.pop(0))
    for j in range(len(blocks) - NA_AHEAD, len(blocks)):
        finish(*blocks[j], *pending.pop(0))


def _na_bias_table(rpb):
    qc = np.arange(GRID_W)[:, None]
    kc = np.arange(GRID_W)[None, :]
    dc = np.clip(kc - qc, -(NA_WIN_W - 1), NA_WIN_W - 1) + (NA_WIN_W - 1)
    t = rpb[:, :, :, dc]
    return jnp.transpose(t, (0, 1, 3, 2, 4)).reshape(DEPTH, NA_HEADS, GRID_W, NA_NDR * GRID_W)


def _na_nbr(st, z3, cache_k, cache_v, bias_tab, l):
    blk = lambda col: pl.BlockSpec((1, st.L, LANE), lambda b, p: (b, 0, col + p))
    cspec = pl.BlockSpec((1, 1, 2, PAST_LEN, NA_HEAD), lambda b, p: (b, l, p, 0, 0))
    return pl.pallas_call(
        _na_nbr_kernel,
        grid=(st.nb, NA_HEADS // 2),
        in_specs=[blk(NA_QCOL), blk(NA_KCOL), blk(NA_VCOL), cspec, cspec,
                  pl.BlockSpec((None, 2, GRID_W, NA_NDR * GRID_W), lambda b, p: (l, p, 0, 0))],
        out_specs=pl.BlockSpec((1, st.L, LANE), lambda b, p: (b, 0, p)),
        out_shape=jax.ShapeDtypeStruct((st.nb, st.L, D_GROUP), BF16),
        compiler_params=_cparams("parallel", "parallel"),
        name="na_neighbourhood",
    )(z3, z3, z3, cache_k, cache_v, bias_tab)


ML_COL = ML_OFF // LANE
ML_SCALE = ML_HEAD ** -0.5


def _log_sigmoid(x):
    return -_softplus(-x)


def _rope_tables(L):
    half = ML_HEAD // 2
    quarter = half // 2
    t = np.arange(L)
    inv_freq = ROPE_BASE ** (-np.arange(quarter, dtype=np.float32) / quarter)

    def tabs(pos):
        ang = pos.astype(np.float32)[:, None] * inv_freq[None, :].astype(np.float32)
        c, s = np.cos(ang), np.sin(ang)
        return np.concatenate([c, c], axis=-1), np.concatenate([-s, s], axis=-1)

    c1, s1 = tabs(t // GRID_W)
    c2, s2 = tabs(t % GRID_W)
    return (jnp.asarray(np.concatenate([c1, c2], axis=-1), F32),
            jnp.asarray(np.concatenate([s1, s2], axis=-1), F32))


def _ml_kernel(*refs, L, rotary, layer):
    nh = ML_HEADS
    ib_ref, fb_ref = refs[0], refs[1]
    q_refs, k_refs, v_refs, o_refs = (refs[2 + i * nh:2 + (i + 1) * nh] for i in range(4))
    (g_ref, cos_ref, sin_ref, c0_ref, n0_ref, m0_ref, lng_ref, lnb_ref,
     y_ref, cf_ref, nf_ref, mf_ref, qs, ks, hf, hb, c_scr, n_scr, m_scr) = refs[2 + 4 * nh:]
    T = ML_CHUNK
    nc = L // T

    if rotary:
        first = (_iota((L, ML_HEAD), 1) % (ML_HEAD // 2)) < ML_HEAD // 4

        def rope(x):
            quarter = ML_HEAD // 4
            partner = jnp.where(first, pltpu.roll(x, ML_HEAD - quarter, axis=1), pltpu.roll(x, quarter, axis=1))
            return x * cos_ref[...] + partner * sin_ref[...]
    else:
        rope = lambda x: x
    for h in range(nh):
        qs[h] = rope(q_refs[h][0]) * ML_SCALE
        ks[h] = rope(k_refs[h][0])
        for d in range(2):
            c_scr[d, h] = c0_ref[0, d, h]
            n_scr[d, h] = n0_ref[0, h, d:d + 1, :]
            m_scr[d, h] = m0_ref[0, h, d:d + 1, :]

    tj = _iota((T, T), 0)
    ts = _iota((T, T), 1)
    ones = jnp.ones((T, T), F32)
    upto_row = (tj <= ts, tj >= ts)
    upto_col = (ts <= tj, ts >= tj)

    def body(ci, carry):
        ch = []
        for d in range(2):
            cd = ci if d == 0 else nc - 1 - ci
            rows = pl.ds(pl.multiple_of(cd * T, T), T)
            for h in range(nh):
                g = g_ref[0, h, cd]
                ig = g[:, d:d + 1] + ib_ref[layer * 2 * nh + d * nh + h]
                fg = g[:, 2 + d:3 + d] + fb_ref[layer * 2 * nh + d * nh + h]
                lf = jnp.broadcast_to(_log_sigmoid(fg), (T, T))
                igb = jnp.broadcast_to(ig, (T, T))
                q = dict(d=d, h=h, rows=rows, ig=ig, qc=qs[h, rows, :], kc=ks[h, rows, :], vc=v_refs[h][0, rows, :])
                q['bcol'] = _dot_sel_l(jnp.where(upto_col[d], 1.0, 0.0), lf)
                q['brow_i'] = _dot_sel_l(ones, jnp.where(upto_row[d], lf, 0.0) - jnp.where(tj == ts, igb, 0.0))
                ch.append(q)
        for q in ch:
            q['qk'] = _dot1(q['qc'], q['kc'], NT)
            q['c_old'] = c_scr[q['d'], q['h']]
            q['qc_c'] = _dot1(q['qc'], q['c_old'])
        for q in ch:
            d, h = q['d'], q['h']
            dmat = jnp.where(upto_col[d], q['bcol'] - q['brow_i'], -jnp.inf)
            b1 = q['bcol'][:, 0:1]
            m_prev = m_scr[d, h]
            inter = b1 + m_prev
            m_t = jnp.maximum(inter, jnp.max(dmat, axis=-1, keepdims=True))
            qk = q['qk'] * jnp.exp(dmat - m_t)
            w_inter = jnp.exp(inter - m_t)
            n_old = n_scr[d, h]
            den = w_inter * jnp.sum(q['qc'] * n_old, axis=-1, keepdims=True) + jnp.sum(qk, axis=-1, keepdims=True)
            q['scale'] = 1.0 / jnp.maximum(jnp.abs(den), jnp.exp(-m_t))
            q['inter_part'] = w_inter * q['qc_c']
            q['intra'] = _dot1(qk, q['vc'])
            b_last = b1[T - 1:T, :] if d == 0 else b1[0:1, :]
            g_s = b_last - b1 + q['ig']
            m_new = jnp.maximum(b_last + m_prev, jnp.max(g_s, axis=0, keepdims=True))
            w_old = jnp.exp(b_last + m_prev - m_new)
            w_s = jnp.exp(g_s - m_new)
            q['c_new'] = _dot1(q['kc'], w_s * q['vc'], TN)
            q['w_old'] = w_old
            n_scr[d, h] = w_old * n_old + jnp.sum(w_s * q['kc'], axis=0, keepdims=True)
            m_scr[d, h] = m_new
        for q in ch:
            d, h = q['d'], q['h']
            hcur = (q['inter_part'] + q['intra']) * q['scale']
            if d == 0:
                hf[h, q['rows'], :] = hcur
            else:
                hb[h, q['rows'], :] = hcur
            c_scr[d, h] = q['w_old'] * q['c_old'] + q['c_new']
        return carry

    lax.fori_loop(0, nc, body, 0)

    for h in range(nh):
        hsum = hf[h] + hb[h]
        mu = jnp.mean(hsum, axis=-1, keepdims=True)
        dv = hsum - mu
        var = jnp.mean(dv * dv, axis=-1, keepdims=True)
        hn = dv * lax.rsqrt(var + ML_GN_EPS)
        y_ref[0, :, h * ML_HEAD:(h + 1) * ML_HEAD] = (
            _sigmoid(o_refs[h][0]) * (hn * lng_ref[0, :, h * ML_HEAD:(h + 1) * ML_HEAD]
                                      + lnb_ref[0, :, h * ML_HEAD:(h + 1) * ML_HEAD])).astype(BF16)
        for d in range(2):
            cf_ref[0, d, h] = c_scr[d, h]
            nf_ref[0, h, d:d + 1, :] = n_scr[d, h]
            mf_ref[0, h, d:d + 1, :] = m_scr[d, h]


def _ml_mixer(st, z3, zg, i_bias, f_bias, ln_g, ln_b, c0, n0, m0, l):
    L, nb, nh = st.L, st.nb, ML_HEADS
    nc = L // ML_CHUNK
    gates = zg.reshape(nb, L, 2, 2, nh).transpose(0, 4, 1, 2, 3).reshape(nb, nh, nc, ML_CHUNK, 4)
    cos_t, sin_t = _rope_tables(L)
    blk = lambda col: pl.BlockSpec((1, L, LANE), lambda b: (b, 0, col))
    head_blocks = [blk(ML_COL + part * nh + h) for part in range(4) for h in range(nh)]
    smem = pl.BlockSpec(memory_space=pltpu.SMEM)
    tab = pl.BlockSpec((L, ML_HEAD), lambda b: (0, 0))
    c_spec = pl.BlockSpec((1, 2, nh, ML_HEAD, ML_HEAD), lambda b: (b, 0, 0, 0, 0))
    n_spec = pl.BlockSpec((1, nh, 2, ML_HEAD), lambda b: (b, 0, 0, 0))
    m_spec = pl.BlockSpec((1, nh, 2, 1), lambda b: (b, 0, 0, 0))
    par = pl.BlockSpec((1, 1, D_GROUP), lambda b: (l, 0, 0))
    return pl.pallas_call(
        functools.partial(_ml_kernel, L=L, rotary=st.latent, layer=l),
        grid=(nb,),
        in_specs=[smem, smem] + head_blocks + [
            pl.BlockSpec((1, nh, nc, ML_CHUNK, 4), lambda b: (b, 0, 0, 0, 0)),
            tab, tab, c_spec, n_spec, m_spec, par, par],
        out_specs=[pl.BlockSpec((1, L, D_GROUP), lambda b: (b, 0, 0)), c_spec, n_spec, m_spec],
        out_shape=[jax.ShapeDtypeStruct((nb, L, D_GROUP), BF16),
                   jax.ShapeDtypeStruct((nb, 2, nh, ML_HEAD, ML_HEAD), F32),
                   jax.ShapeDtypeStruct((nb, nh, 2, ML_HEAD), F32),
                   jax.ShapeDtypeStruct((nb, nh, 2, 1), F32)],
        scratch_shapes=[pltpu.VMEM((nh, L, ML_HEAD), F32), pltpu.VMEM((nh, L, ML_HEAD), F32),
                        pltpu.VMEM((nh, L, ML_HEAD), F32), pltpu.VMEM((nh, L, ML_HEAD), F32),
                        pltpu.VMEM((2, nh, ML_HEAD, ML_HEAD), F32), pltpu.VMEM((2, nh, 1, ML_HEAD), F32),
                        pltpu.VMEM((2, nh, 1, 1), F32)],
        compiler_params=_cparams("parallel"),
        name="mlstm",
    )(i_bias.reshape(-1), f_bias.reshape(-1), *([z3] * (4 * nh)), gates, cos_t, sin_t, c0, n0, m0,
      ln_g.reshape(DEPTH, 1, D_GROUP), ln_b.reshape(DEPTH, 1, D_GROUP))


RW_TL = 256
RW_T = 64
RW_GROUP = 16
RW_RCOL = RW_OFF // D_GROUP
RW_LCOL = (RW_OFF + 3 * D_GROUP) // (2 * LANE)
N_LORA = RW_LORA_W + RW_LORA_A + RW_LORA_G


def _head_ones():
    return jnp.asarray(np.kron(np.eye(RW_HEADS, dtype=np.float32), np.ones((RW_HEAD, RW_HEAD), np.float32)))


def _rw_front_kernel(r_ref, k_ref, v_ref, lo_ref, rp_ref, kp_ref, vp_ref, lp_ref, rn_ref, kn_ref, vn_ref, ln_ref,
                     mup_ref, mun_ref, w0_ref, w2_ref, a0_ref, a2_ref, g2_ref, kk_ref, ka_ref, hones_ref,
                     ro_ref, vo_ref, kko_ref, go_ref, kd_ref, bd_ref, lw_ref):
    t = pl.program_id(1)
    first = t == 0
    last = t == pl.num_programs(1) - 1
    row = _iota((RW_TL, 1), 0)

    def shift(x_ref, p_ref, n_ref, lo, hi):
        x = x_ref[0]
        prev_edge = jnp.where(first, 0.0, p_ref[0, SUB - 1:SUB, :])
        next_edge = jnp.where(last, 0.0, n_ref[0, 0:1, :])
        prev = jnp.where(row == 0, prev_edge, pltpu.roll(x, 1, axis=0))
        nxt = jnp.where(row == RW_TL - 1, next_edge, pltpu.roll(x, RW_TL - 1, axis=0))
        return x + mup_ref[0][:, lo:hi] * (prev - x) + mun_ref[0][:, lo:hi] * (nxt - x)

    g = D_GROUP
    r = shift(r_ref, rp_ref, rn_ref, 0, g)
    k = shift(k_ref, kp_ref, kn_ref, g, 2 * g)
    v = shift(v_ref, vp_ref, vn_ref, 2 * g, 3 * g)
    lo = shift(lo_ref, lp_ref, ln_ref, 3 * g, 3 * g + N_LORA)
    zw = lo[:, :RW_LORA_W]
    za = lo[:, RW_LORA_W:RW_LORA_W + RW_LORA_A]
    zg = lo[:, RW_LORA_W + RW_LORA_A:]
    ro_ref[0] = r
    vo_ref[0] = v
    kk = k * kk_ref[0]
    ssq = _dot_sel_r(kk * kk, hones_ref[...])
    kk = kk * lax.rsqrt(ssq + 1e-12)
    kko_ref[0] = kk
    go_ref[0] = _dot3(_sigmoid(zg), g2_ref[0])
    tw = jnp.tanh(zw)
    for d in range(2):
        w_log = -_softplus(-(w0_ref[0, d:d + 1, :] + _dot3(tw, w2_ref[0, d]))) - 0.5
        lw_ref[d, 0] = -jnp.exp(w_log)
        a = _sigmoid(a0_ref[0, d:d + 1, :] + _dot3(za, a2_ref[0, d]))
        kd_ref[d, 0] = k * (1.0 + (a - 1.0) * ka_ref[0])
        bd_ref[d, 0] = kk * a


def _rw_front(st, z3, mu_prev, mu_next, w0, w2, a0, a2, g2, k_k, k_a, l):
    nb, L = st.nb, st.L
    nt = L // RW_TL
    tpb = RW_TL // SUB
    main = lambda w, col: pl.BlockSpec((1, RW_TL, w), lambda b, t: (b, t, col))
    prev = lambda w, col: pl.BlockSpec((1, SUB, w), lambda b, t: (b, jnp.maximum(t * tpb - 1, 0), col))
    nxt = lambda w, col: pl.BlockSpec((1, SUB, w), lambda b, t: (b, jnp.minimum((t + 1) * tpb, L // SUB - 1), col))
    cols = [(D_GROUP, RW_RCOL), (D_GROUP, RW_RCOL + 1), (D_GROUP, RW_RCOL + 2), (2 * LANE, RW_LCOL)]
    lay = lambda *shape: pl.BlockSpec((1,) + shape, lambda b, t: (l,) + (0,) * len(shape))
    out1 = pl.BlockSpec((1, RW_TL, D_GROUP), lambda b, t: (b, t, 0))
    out2 = pl.BlockSpec((2, 1, RW_TL, D_GROUP), lambda b, t: (0, b, t, 0))
    s1 = jax.ShapeDtypeStruct((nb, L, D_GROUP), F32)
    s2 = jax.ShapeDtypeStruct((2, nb, L, D_GROUP), F32)
    return pl.pallas_call(
        _rw_front_kernel,
        grid=(nb, nt),
        in_specs=[main(*c) for c in cols] + [prev(*c) for c in cols] + [nxt(*c) for c in cols] + [
            lay(1, RW_IN), lay(1, RW_IN), lay(2, D_GROUP), lay(2, RW_LORA_W, D_GROUP), lay(2, D_GROUP),
            lay(2, RW_LORA_A, D_GROUP), lay(RW_LORA_G, D_GROUP), lay(1, D_GROUP), lay(1, D_GROUP),
            pl.BlockSpec((D_GROUP, D_GROUP), lambda b, t: (0, 0))],
        out_specs=[out1, out1, out1, out1, out2, out2, out2],
        out_shape=[s1, s1, s1, s1, s2, s2, s2],
        compiler_params=_cparams("parallel", "parallel"),
        name="rwkv_front",
    )(*([z3] * 12), mu_prev.reshape(DEPTH, 1, RW_IN), mu_next.reshape(DEPTH, 1, RW_IN), w0, w2, a0, a2, g2,
      k_k.reshape(DEPTH, 1, D_GROUP), k_a.reshape(DEPTH, 1, D_GROUP), _head_ones())


def _rw_core_kernel(rf_ref, vf_ref, kkf_ref, kdf_ref, bdf_ref, lwf_ref,
                    rb_ref, vb_ref, kkb_ref, kdb_ref, bdb_ref, lwb_ref, s0_ref,
                    yf_ref, yb_ref, sf_ref, s_scr):
    c = pl.program_id(1)
    T = RW_T

    @pl.when(c == 0)
    def _():
        s_scr[...] = s0_ref[0]

    tj = _iota((T, T), 0)
    ts = _iota((T, T), 1)
    dirs = ((rf_ref, vf_ref, kkf_ref, kdf_ref, bdf_ref, lwf_ref, yf_ref),
            (rb_ref, vb_ref, kkb_ref, kdb_ref, bdb_ref, lwb_ref, yb_ref))
    ch = []
    for d, (r_ref, v_ref, kk_ref, kd_ref, bd_ref, lw_ref, y_ref) in enumerate(dirs):
        if d == 0:
            incl = ts <= tj
            strict = ts < tj
            last = T - 1
        else:
            incl = ts >= tj
            strict = ts > tj
            last = 0
        lw = lw_ref[0, 0]
        cum = _dot_sel_l(jnp.where(incl, 1.0, 0.0), lw)
        w_in = jnp.exp(cum)
        w_inv = jnp.exp(-cum)
        w_ex = jnp.exp(cum - lw)
        kap_a = kk_ref[0] * w_ex
        bet_a = bd_ref[0, 0] * w_inv
        khat_a = kd_ref[0, 0] * w_inv
        rho_a = r_ref[0] * w_in
        w_tot = w_in[last:last + 1, :]
        v_a = v_ref[0]
        for h in range(RW_HEADS):
            sl = slice(h * RW_HEAD, (h + 1) * RW_HEAD)
            ch.append(dict(d=d, h=h, sl=sl, incl=incl, strict=strict, y_ref=y_ref, w_tot=w_tot[:, sl],
                           kap=kap_a[:, sl], bet=bet_a[:, sl], khat=khat_a[:, sl], rho=rho_a[:, sl], v=v_a[:, sl]))
    def solve(chs):
        for q in chs:
            kr = jnp.concatenate([q['kap'], q['rho']], axis=0)
            bk = jnp.concatenate([q['bet'], q['khat']], axis=0)
            q['gram'] = _dot1(kr, bk, NT)
        for q in chs:
            gram = q.pop('gram')
            q['n'] = -jnp.where(q['strict'], gram[:T, :T], 0.0)
            l_k = jnp.where(q['strict'], gram[:T, T:], 0.0)
            q['m_b'] = jnp.where(q['incl'], gram[T:, :T], 0.0)
            m_k = jnp.where(q['incl'], gram[T:, T:], 0.0)
            q['lmv'] = _dot1(jnp.concatenate([l_k, m_k], axis=0), q['v'])
        for q in chs:
            q['x'] = jnp.concatenate([q['kap'], q['lmv'][:T]], axis=1)
        for lvl in range(6):
            mm = _dot3 if lvl < 3 else _dot1
            for q in chs:
                if lvl < 5:
                    nx = mm(q['n'], jnp.concatenate([q['n'], q['x']], axis=1))
                    q['n'], q['x'] = nx[:, :T], q['x'] + nx[:, T:]
                else:
                    q['x'] = q['x'] + _dot1(q['n'], q['x'])
        for q in chs:
            q['p'] = _dot1(q['m_b'], q['x'])
            xb = _dot1(q['x'], q['bet'], TN)
            q['a_m'] = xb[:RW_HEAD]
            q['d_m'] = _dot1(q['v'], q['khat'], TN) - xb[RW_HEAD:]
        for q in chs:
            rho_p = q['rho'] - q['p'][:, :RW_HEAD]
            y_v = q['lmv'][T:] - q['p'][:, RW_HEAD:]
            s_old = s_scr[q['d'], q['h']]
            q['y_ref'][0, :, q['sl']] = _dot1(rho_p, s_old, NT) + y_v
            s_scr[q['d'], q['h']] = (s_old - _dot3(s_old, q['a_m']) + q['d_m']) * q['w_tot']

    for g0 in range(0, len(ch), RW_GROUP):
        solve(ch[g0:g0 + RW_GROUP])

    @pl.when(c == pl.num_programs(1) - 1)
    def _():
        sf_ref[0] = s_scr[...]


def _rw_core(st, r, v, kk, kd, bd, lw, s0):
    nb, L = st.nb, st.L
    nc = L // RW_T
    f1 = pl.BlockSpec((1, RW_T, D_GROUP), lambda b, c: (b, c, 0))
    b1 = pl.BlockSpec((1, RW_T, D_GROUP), lambda b, c: (b, nc - 1 - c, 0))
    f2 = pl.BlockSpec((1, 1, RW_T, D_GROUP), lambda b, c: (0, b, c, 0))
    b2 = pl.BlockSpec((1, 1, RW_T, D_GROUP), lambda b, c: (1, b, nc - 1 - c, 0))
    s_spec = pl.BlockSpec((1, 2, RW_HEADS, RW_HEAD, RW_HEAD), lambda b, c: (b, 0, 0, 0, 0))
    ys = jax.ShapeDtypeStruct((nb, L, D_GROUP), F32)
    return pl.pallas_call(
        _rw_core_kernel,
        grid=(nb, nc),
        in_specs=[f1, f1, f1, f2, f2, f2, b1, b1, b1, b2, b2, b2, s_spec],
        out_specs=[f1, b1, s_spec],
        out_shape=[ys, ys, jax.ShapeDtypeStruct((nb, 2, RW_HEADS, RW_HEAD, RW_HEAD), F32)],
        scratch_shapes=[pltpu.VMEM((2, RW_HEADS, RW_HEAD, RW_HEAD), F32)],
        compiler_params=_cparams("parallel", "arbitrary"),
        name="rwkv_core",
    )(r, v, kk, kd, bd, lw, r, v, kk, kd, bd, lw, s0)


RWO_TM = 512


def _rw_out_kernel(yf_ref, yb_ref, r_ref, v_ref, kd0_ref, kd1_ref, g_ref, lng_ref, lnb_ref, rk_ref, hones_ref, o_ref):
    y = yf_ref[...] + yb_ref[...]
    inv = 1.0 / RW_HEAD
    mu = _dot_sel_r(y, hones_ref[...]) * inv
    dv = y - mu
    var = _dot_sel_r(dv * dv, hones_ref[...]) * inv
    yn = dv * lax.rsqrt(var + RW_GN_EPS) * lng_ref[0] + lnb_ref[0]
    kmean = 0.5 * (kd0_ref[0] + kd1_ref[0])
    bonus = _dot_sel_r(r_ref[...] * kmean * rk_ref[0], hones_ref[...]) * v_ref[...]
    o_ref[...] = ((yn + bonus) * g_ref[...]).astype(BF16)


def _rw_out(st, yf, yb, r, v, kd, g, ln_g, ln_b, r_k, l):
    rows = st.rows
    flat = lambda a: a.reshape(rows, D_GROUP)
    row = pl.BlockSpec((RWO_TM, D_GROUP), lambda i: (i, 0))
    lay = pl.BlockSpec((1, 1, D_GROUP), lambda i: (l, 0, 0))
    kd2 = kd.reshape(2, rows, D_GROUP)
    return pl.pallas_call(
        _rw_out_kernel,
        grid=(rows // RWO_TM,),
        in_specs=[row, row, row, row,
                  pl.BlockSpec((1, RWO_TM, D_GROUP), lambda i: (0, i, 0)),
                  pl.BlockSpec((1, RWO_TM, D_GROUP), lambda i: (1, i, 0)),
                  row, lay, lay, lay, pl.BlockSpec((D_GROUP, D_GROUP), lambda i: (0, 0))],
        out_specs=row,
        out_shape=jax.ShapeDtypeStruct((rows, D_GROUP), BF16),
        compiler_params=_cparams("parallel"),
        name="rwkv_out",
    )(flat(yf), flat(yb), flat(r), flat(v), kd2, kd2, flat(g), ln_g.reshape(DEPTH, 1, D_GROUP),
      ln_b.reshape(DEPTH, 1, D_GROUP), r_k.reshape(DEPTH, 1, D_GROUP), _head_ones())


def _trunk_layer(st, x, l, mod4, p, states, ctx_kv):
    nb, L = st.nb, st.L
    z, zg = _zproj(st, x, mod4, p['w_in'], p['w_gate_cols'], l)
    z3 = z.reshape(nb, L, D_Z)

    ydir, sfr, sfi = _s5_scan(st, z3, p['s5_bq'], p['s5_cq'], p['s5_lamr'], p['s5_lami'],
                              states['s5_re'], states['s5_im'], l)
    y_s5 = _s5_out(st, z, ydir.reshape(2, st.rows, D_GROUP), p['s5_d'], p['s5_w_glu'], l)

    r, v, kk, g, kd, bd, lw = _rw_front(st, z3, p['rw_mu_prev'], p['rw_mu_next'], p['rw_w0'], p['rw_w2'],
                                        p['rw_a0'], p['rw_a2'], p['rw_g2'], p['rw_k_k'], p['rw_k_a'], l)
    yf, yb, rw_s = _rw_core(st, r, v, kk, kd, bd, lw, states['rw'])
    y_rw = _rw_out(st, yf, yb, r, v, kd, g, p['rw_ln_g'], p['rw_ln_b'], p['rw_r_k'], l)

    if ctx_kv is None:
        y_na, nk, nv = _na_ctx(st, z3)
    else:
        y_na = _na_nbr(st, z3, ctx_kv[0], ctx_kv[1], p['na_bias'], l)
        nk = nv = None

    y_ml, ml_c, ml_n, ml_m = _ml_mixer(st, z3, zg, p['ml_i_bias'], p['ml_f_bias'], p['ml_ln_g'], p['ml_ln_b'],
                                       states['ml_c'], states['ml_n'], states['ml_m'], l)

    ys = (y_s5, y_rw, y_na.reshape(st.rows, D_GROUP), y_ml.reshape(st.rows, D_GROUP))
    tail = _oproj(st, ys, x, mod4, p['w_out'], p['ln1_g'], p['ln1_b'], p['w_router'], p['b_router'], l)
    return tail, (nk, nv, sfr, sfi, rw_s, ml_c, ml_n, ml_m)


def _moe_and_ln2(tails, mod4, p, l):
    xs = []
    for st, (x1, h2, gates, keep) in zip((PROMPT, LATENT), tails):
        f = _moe_sorted(st, h2, gates, keep, p['moe_w_gate'], p['moe_w_up'], p['moe_w_down'], l)
        xs.append(_ln2f(st, x1, f, mod4, p['ln2_g'], p['ln2_b'], l))
    return xs


def kernel(x_prompt, x_sample, cache_nat_k, cache_nat_v, state_s5_re, state_s5_im, state_rwkv, state_mlstm_c, state_mlstm_n, state_mlstm_m, c, c_ctx, w_mod, b_mod, w_in, w_out, s5_lam_re, s5_lam_im, s5_log_step, s5_b_re, s5_b_im, s5_c_re, s5_c_im, s5_d, s5_w_glu, rw_mu_prev, rw_mu_next, rw_w0, rw_w2, rw_a0, rw_a2, rw_g2, rw_k_k, rw_k_a, rw_r_k, rw_ln_g, rw_ln_b, na_rpb, ml_i_bias, ml_f_bias, ml_ln_g, ml_ln_b, ln1_g, ln1_b, ln2_g, ln2_b, w_router, b_router, moe_w_gate, moe_w_up, moe_w_down):
    dt = x_prompt.dtype
    cond = jnp.concatenate([c_ctx[None, :], c, jnp.zeros((MOD_ROWS - 1 - DEC_BATCH, D_MODEL), F32)], axis=0)
    mod4 = _modulation(cond, w_mod, b_mod).reshape(DEPTH, MOD_ROWS, 1, 6 * D_MODEL)

    lbr, lbi, bbr, bbi = _s5_prep(s5_lam_re, s5_lam_im, s5_log_step, s5_b_re, s5_b_im)
    s5_bq, s5_cq, s5_lamr, s5_lami = _s5_block_params(lbr, lbi, bbr, bbi, s5_c_re, s5_c_im)
    p = dict(w_in=w_in, w_gate_cols=w_in[:, :, D_Z:], w_out=w_out.astype(BF16),
             s5_bq=s5_bq, s5_cq=s5_cq, s5_lamr=s5_lamr, s5_lami=s5_lami, s5_d=s5_d, s5_w_glu=s5_w_glu,
             rw_mu_prev=rw_mu_prev, rw_mu_next=rw_mu_next, rw_w0=rw_w0, rw_w2=rw_w2, rw_a0=rw_a0, rw_a2=rw_a2,
             rw_g2=rw_g2, rw_k_k=rw_k_k, rw_k_a=rw_k_a, rw_r_k=rw_r_k, rw_ln_g=rw_ln_g, rw_ln_b=rw_ln_b,
             na_bias=_na_bias_table(na_rpb), ml_i_bias=ml_i_bias, ml_f_bias=ml_f_bias, ml_ln_g=ml_ln_g,
             ml_ln_b=ml_ln_b, ln1_g=ln1_g, ln1_b=ln1_b, ln2_g=ln2_g, ln2_b=ln2_b, w_router=w_router,
             b_router=b_router, moe_w_gate=moe_w_gate.astype(BF16), moe_w_up=moe_w_up.astype(BF16),
             moe_w_down=moe_w_down.astype(BF16))

    gp = BATCH // S5_SEQS
    zero_states = dict(
        s5_re=jnp.zeros((gp, 2, S5_SEQS, N_S5), F32), s5_im=jnp.zeros((gp, 2, S5_SEQS, N_S5), F32),
        rw=jnp.zeros((BATCH, 2, RW_HEADS, RW_HEAD, RW_HEAD), F32),
        ml_c=jnp.zeros((BATCH, 2, ML_HEADS, ML_HEAD, ML_HEAD), F32),
        ml_n=jnp.zeros((BATCH, ML_HEADS, 2, ML_HEAD), F32), ml_m=jnp.zeros((BATCH, ML_HEADS, 2, 1), F32))

    xp = x_prompt.reshape(PROMPT.rows, D_MODEL)
    xs = x_sample.reshape(LATENT.rows, D_MODEL)
    outs = [[] for _ in range(8)]
    for l in range(DEPTH):
        tail_p, ctx_t = _trunk_layer(PROMPT, xp, l, mod4, p, zero_states, None)
        for acc, t in zip(outs, ctx_t):
            acc.append(t)
        lat_states = dict(
            s5_re=state_s5_re[:, l].reshape(DEC_BATCH, 2, N_S5).transpose(1, 0, 2)[None],
            s5_im=state_s5_im[:, l].reshape(DEC_BATCH, 2, N_S5).transpose(1, 0, 2)[None],
            rw=state_rwkv[:, l], ml_c=state_mlstm_c[:, l],
            ml_n=state_mlstm_n[:, l].transpose(0, 2, 1, 3), ml_m=state_mlstm_m[:, l].transpose(0, 2, 1)[..., None])
        tail_s, _ = _trunk_layer(LATENT, xs, l, mod4, p, lat_states, (cache_nat_k, cache_nat_v))
        xp, xs = _moe_and_ln2((tail_p, tail_s), mod4, p, l)

    nk, nv, s5r, s5i, rw, mc, mn, mm = [jnp.stack(t, axis=1) for t in outs]

    def s5_state(t):
        return t.transpose(0, 3, 1, 2, 4).reshape(BATCH, DEPTH, 2, S5_GROUPS, S5_STATE)

    return (xp.reshape(BATCH, SEQ, D_MODEL), xs.reshape(DEC_BATCH, DEC_SEQ, D_MODEL),
            nk, nv, s5_state(s5r).astype(dt), s5_state(s5i).astype(dt), rw.astype(dt), mc.astype(dt),
            mn.transpose(0, 1, 3, 2, 4).astype(dt), mm[..., 0].transpose(0, 1, 3, 2).astype(dt))
```

```python
import functools
import math

import numpy as np
import jax
import jax.numpy as jnp
from jax import lax
from jax.experimental import pallas as pl
from jax.experimental.pallas import tpu as pltpu

F32 = jnp.float32
BF16 = jnp.bfloat16

D_MODEL = 2048
BATCH = 16
SEQ = 256
DEPTH = 4
DEC_BATCH = 8
DEC_SEQ = 1024
PAST_LEN = 256
GRID_W = 64
D_GROUP = D_MODEL // 4
S5_CH = 16
S5_GROUPS = D_GROUP // S5_CH
S5_STATE = 64
RW_HEAD = 64
RW_HEADS = D_GROUP // RW_HEAD
RW_LORA_W = 64
RW_LORA_A = 64
RW_LORA_G = 128
RW_GN_EPS = 64e-5
NA_HEAD = 64
NA_HEADS = D_GROUP // NA_HEAD
NA_WIN_H = 8
NA_WIN_W = 16
NEG_INF = -1e30
ML_HEAD = 128
ML_HEADS = D_GROUP // ML_HEAD
ML_CHUNK = 256
ML_GN_EPS = 1e-5
ROPE_BASE = 10000.0
N_EXPERTS = 16
N_EXPERT_GROUPS = 4
EXPERTS_PER_GROUP = N_EXPERTS // N_EXPERT_GROUPS
D_EXPERT = 512
DEEPNORM_ALPHA = (2 * DEPTH) ** 0.25
LN_EPS = 1e-5
S5_IN = D_GROUP
RW_IN = 3 * D_GROUP + RW_LORA_W + RW_LORA_A + RW_LORA_G
NA_IN = 3 * D_GROUP
ML_IN = 4 * D_GROUP + 4 * ML_HEADS
D_IN = S5_IN + RW_IN + NA_IN + ML_IN
N_GATE = 4 * ML_HEADS
D_Z = D_IN - N_GATE
RW_OFF = S5_IN
NA_OFF = S5_IN + RW_IN
ML_OFF = NA_OFF + NA_IN
MOD_ROWS = 16
LANE = 128
SUB = 8

VMEM_LIMIT = 56 * 1024 * 1024


def _cparams(*sem):
    return pltpu.CompilerParams(dimension_semantics=sem, vmem_limit_bytes=VMEM_LIMIT)


def _dg(a, b, dims):
    return lax.dot_general(a, b, (dims, ((), ())), preferred_element_type=F32)


NN = ((1,), (0,))
NT = ((1,), (1,))
TN = ((0,), (0,))


def _dot1(a, b, dims=NN):
    return _dg(a.astype(BF16), b.astype(BF16), dims)


def _split(x):
    hi = x.astype(BF16)
    lo = (x - hi.astype(F32)).astype(BF16)
    return hi, lo


def _split3(x):
    hi = x.astype(BF16)
    r = x - hi.astype(F32)
    mid = r.astype(BF16)
    lo = (r - mid.astype(F32)).astype(BF16)
    return hi, mid, lo


def _dot3(a, b, dims=NN):
    ah, al = _split(a)
    bh, bl = _split(b)
    return _dg(ah, bh, dims) + (_dg(ah, bl, dims) + _dg(al, bh, dims))


def _dot_sel_l(sel, x, dims=NN):
    s = sel.astype(BF16)
    hi, mid, lo = _split3(x)
    return _dg(s, hi, dims) + (_dg(s, mid, dims) + _dg(s, lo, dims))


def _dot_sel_r(x, sel, dims=NN):
    s = sel.astype(BF16)
    hi, mid, lo = _split3(x)
    return _dg(hi, s, dims) + (_dg(mid, s, dims) + _dg(lo, s, dims))


def _sigmoid(x):
    return 1.0 / (1.0 + jnp.exp(-x))


def _silu(x):
    return x * _sigmoid(x)


def _softplus(x):
    return jnp.maximum(x, 0.0) + jnp.log(1.0 + jnp.exp(-jnp.abs(x)))


def _iota(shape, dim):
    return lax.broadcasted_iota(jnp.int32, shape, dim)


class _Stream:
    def __init__(self, nb, L, latent):
        self.nb, self.L, self.latent = nb, L, latent
        self.rows = nb * L

    def mod_row(self, tile, tile_rows):
        if not self.latent:
            return 0
        return 1 + (tile * tile_rows) // self.L


PROMPT = _Stream(BATCH, SEQ, False)
LATENT = _Stream(DEC_BATCH, DEC_SEQ, True)


MOD_TN = 768


def _mod_kernel(cond_ref, w_ref, b_ref, o_ref):
    c = _silu(cond_ref[...])
    o_ref[0] = _dot3(c, w_ref[0]) + b_ref[0]


def _modulation(cond, w_mod, b_mod):
    n = 6 * D_MODEL
    return pl.pallas_call(
        _mod_kernel,
        grid=(DEPTH, n // MOD_TN),
        in_specs=[pl.BlockSpec((MOD_ROWS, D_MODEL), lambda l, j: (0, 0)),
                  pl.BlockSpec((1, D_MODEL, MOD_TN), lambda l, j: (l, 0, j)),
                  pl.BlockSpec((1, 1, MOD_TN), lambda l, j: (l, 0, j))],
        out_specs=pl.BlockSpec((1, MOD_ROWS, MOD_TN), lambda l, j: (l, 0, j)),
        out_shape=jax.ShapeDtypeStruct((DEPTH, MOD_ROWS, n), F32),
        compiler_params=_cparams("parallel", "parallel"),
        name="modulation",
    )(cond, w_mod, b_mod.reshape(DEPTH, 1, n))


ZP_TM = 1024
ZP_TN = 256


def _zproj_kernel(x_ref, mod_ref, w_ref, wg_ref, z_ref, zg_ref, h_scr):
    j = pl.program_id(1)

    @pl.when(j == 0)
    def _():
        m = mod_ref[0, 0]
        shift1 = m[:, 0:D_MODEL]
        scale1 = m[:, D_MODEL:2 * D_MODEL]
        h = (x_ref[...] * (1.0 + scale1) + shift1).astype(BF16)
        h_scr[...] = h
        zg_ref[...] = jnp.dot(h, wg_ref[0], preferred_element_type=F32)

    z_ref[...] = jnp.dot(h_scr[...], w_ref[0], preferred_element_type=F32)


def _zproj(st, x, mod4, w_in, w_gate_cols, l):
    return pl.pallas_call(
        _zproj_kernel,
        grid=(st.rows // ZP_TM, D_Z // ZP_TN),
        in_specs=[pl.BlockSpec((ZP_TM, D_MODEL), lambda i, j: (i, 0)),
                  pl.BlockSpec((1, 1, 1, 6 * D_MODEL), lambda i, j: (l, st.mod_row(i, ZP_TM), 0, 0)),
                  pl.BlockSpec((1, D_MODEL, ZP_TN), lambda i, j: (l, 0, j)),
                  pl.BlockSpec((1, D_MODEL, N_GATE), lambda i, j: (l, 0, 0))],
        out_specs=[pl.BlockSpec((ZP_TM, ZP_TN), lambda i, j: (i, j)),
                   pl.BlockSpec((ZP_TM, N_GATE), lambda i, j: (i, 0))],
        out_shape=[jax.ShapeDtypeStruct((st.rows, D_Z), F32),
                   jax.ShapeDtypeStruct((st.rows, N_GATE), F32)],
        scratch_shapes=[pltpu.VMEM((ZP_TM, D_MODEL), BF16)],
        compiler_params=_cparams("parallel", "arbitrary"),
        name="zproj",
    )(x, mod4, w_in, w_gate_cols)


OP_TM = 512


def _layer_norm(v, g, b):
    mu = jnp.mean(v, axis=-1, keepdims=True)
    d = v - mu
    var = jnp.mean(d * d, axis=-1, keepdims=True)
    return d * lax.rsqrt(var + LN_EPS) * g + b


def _route(scores, b_router):
    sel = scores + b_router
    s = [sel[:, e:e + 1] for e in range(N_EXPERTS)]
    in_top2 = []
    for g in range(N_EXPERT_GROUPS):
        for i in range(EXPERTS_PER_GROUP):
            e = g * EXPERTS_PER_GROUP + i
            cnt = jnp.zeros_like(s[e])
            for jj in range(EXPERTS_PER_GROUP):
                if jj == i:
                    continue
                o = g * EXPERTS_PER_GROUP + jj
                beats = (s[o] > s[e]) if jj > i else (s[o] >= s[e])
                cnt = cnt + jnp.where(beats, 1.0, 0.0)
            in_top2.append(cnt < 2.0)
    grp = []
    for g in range(N_EXPERT_GROUPS):
        tot = jnp.zeros_like(s[0])
        for i in range(EXPERTS_PER_GROUP):
            e = g * EXPERTS_PER_GROUP + i
            tot = tot + jnp.where(in_top2[e], s[e], 0.0)
        grp.append(tot)
    lane = _iota(scores.shape, 1)
    keep = jnp.zeros(scores.shape, F32)
    for g in range(N_EXPERT_GROUPS):
        cnt = jnp.zeros_like(s[0])
        for o in range(N_EXPERT_GROUPS):
            if o == g:
                continue
            beats = (grp[o] > grp[g]) if o > g else (grp[o] >= grp[g])
            cnt = cnt + jnp.where(beats, 1.0, 0.0)
        best = cnt < 1.0
        for i in range(EXPERTS_PER_GROUP):
            e = g * EXPERTS_PER_GROUP + i
            on = jnp.where(best, jnp.where(in_top2[e], 1.0, 0.0), 0.0)
            keep = keep + jnp.where(lane == e, on, 0.0)
    picked = scores * keep
    return picked / jnp.sum(picked, axis=-1, keepdims=True), keep


def _oproj_kernel(y0_ref, y1_ref, y2_ref, y3_ref, w_ref, x_ref, mod_ref, g_ref, b_ref, wr_ref, br_ref,
                  x1_ref, h2_ref, gate_ref, keep_ref):
    acc = None
    for k, yr in enumerate((y0_ref, y1_ref, y2_ref, y3_ref)):
        part = jnp.dot(yr[...], w_ref[0, k * D_GROUP:(k + 1) * D_GROUP, :], preferred_element_type=F32)
        acc = part if acc is None else acc + part
    m = mod_ref[0, 0]
    gate1 = m[:, 2 * D_MODEL:3 * D_MODEL]
    shift2 = m[:, 3 * D_MODEL:4 * D_MODEL]
    scale2 = m[:, 4 * D_MODEL:5 * D_MODEL]
    x1 = _layer_norm(DEEPNORM_ALPHA * x_ref[...] + gate1 * acc, g_ref[0], b_ref[0])
    x1_ref[...] = x1
    h2 = x1 * (1.0 + scale2) + shift2
    h2_ref[...] = h2.astype(BF16)
    scores = _sigmoid(_dot3(h2, wr_ref[...]))
    gate_ref[...], keep_ref[...] = _route(scores, br_ref[...])


def _oproj(st, ys, x, mod4, w_out_b, ln_g, ln_b, w_router, b_router, l):
    row = lambda i: (i, 0)
    return pl.pallas_call(
        _oproj_kernel,
        grid=(st.rows // OP_TM,),
        in_specs=[pl.BlockSpec((OP_TM, D_GROUP), row)] * 4 + [
            pl.BlockSpec((1, D_MODEL, D_MODEL), lambda i: (l, 0, 0)),
            pl.BlockSpec((OP_TM, D_MODEL), row),
            pl.BlockSpec((1, 1, 1, 6 * D_MODEL), lambda i: (l, st.mod_row(i, OP_TM), 0, 0)),
            pl.BlockSpec((1, 1, D_MODEL), lambda i: (l, 0, 0)),
            pl.BlockSpec((1, 1, D_MODEL), lambda i: (l, 0, 0)),
            pl.BlockSpec((D_MODEL, N_EXPERTS), lambda i: (0, 0)),
            pl.BlockSpec((1, N_EXPERTS), lambda i: (0, 0))],
        out_specs=[pl.BlockSpec((OP_TM, D_MODEL), row),
                   pl.BlockSpec((OP_TM, D_MODEL), row),
                   pl.BlockSpec((OP_TM, N_EXPERTS), row),
                   pl.BlockSpec((OP_TM, N_EXPERTS), row)],
        out_shape=[jax.ShapeDtypeStruct((st.rows, D_MODEL), F32),
                   jax.ShapeDtypeStruct((st.rows, D_MODEL), BF16),
                   jax.ShapeDtypeStruct((st.rows, N_EXPERTS), F32),
                   jax.ShapeDtypeStruct((st.rows, N_EXPERTS), F32)],
        compiler_params=_cparams("parallel"),
        name="oproj_ln1_router",
    )(*ys, w_out_b, x, mod4, ln_g.reshape(DEPTH, 1, D_MODEL), ln_b.reshape(DEPTH, 1, D_MODEL),
      w_router, b_router.reshape(1, N_EXPERTS))


LN_TM = 512


GS_BLK = 1024
GS_CAP = 320
GS_ALIGN = 2 * SUB
GS_COLS = 512
GS_STEP = 256
GS_VMEM_LIMIT = 60 * 1024 * 1024
GS_ROWS = pl.cdiv(GS_BLK + N_EXPERT_GROUPS * GS_ALIGN + GS_CAP, GS_COLS) * GS_COLS


def _moe_sorted_kernel(h_ref, gate_ref, keep_ref, wg_ref, wu_ref, wd_ref, f_ref, xs_scr, gs_scr, pt_scr, acc_scr,
                       seg_smem):
    e = pl.program_id(1)

    @pl.when(e == 0)
    def _():
        keep = keep_ref[...]
        lane = _iota(keep.shape, 1)
        member = []
        for g in range(N_EXPERT_GROUPS):
            in_g = (lane >= g * EXPERTS_PER_GROUP) & (lane < (g + 1) * EXPERTS_PER_GROUP)
            member.append(jnp.minimum(jnp.sum(jnp.where(in_g, keep, 0.0), axis=-1, keepdims=True), 1.0))
        lane4 = _iota((GS_BLK, LANE), 1)
        onehot = jnp.zeros((GS_BLK, LANE), F32)
        for g in range(N_EXPERT_GROUPS):
            onehot = onehot + jnp.where(lane4 == g, member[g], 0.0)
        counts = jnp.sum(onehot, axis=0, keepdims=True)
        starts = []
        start = jnp.zeros((1, 1), F32)
        for g in range(N_EXPERT_GROUPS):
            cnt_g = counts[:, g:g + 1]
            starts.append(start)
            seg_smem[2 * g] = jnp.sum(start).astype(jnp.int32)
            seg_smem[2 * g + 1] = jnp.sum(cnt_g).astype(jnp.int32)
            start = start + jnp.ceil(cnt_g * (1.0 / GS_ALIGN)) * GS_ALIGN
        onehot_b = onehot.astype(BF16)
        for r0 in range(0, GS_BLK, GS_STEP):
            tri = jnp.where(_iota((GS_STEP, GS_BLK), 1) <= _iota((GS_STEP, GS_BLK), 0) + r0, 1.0, 0.0).astype(BF16)
            csum = jnp.dot(tri, onehot_b, preferred_element_type=F32)
            dest = jnp.zeros((GS_STEP, 1), F32)
            for g in range(N_EXPERT_GROUPS):
                dest = dest + member[g][r0:r0 + GS_STEP] * (starts[g] + csum[:, g:g + 1] - 1.0)
            pt_scr[r0:r0 + GS_STEP, :] = jnp.where(
                _iota((GS_STEP, GS_ROWS), 1) == dest.astype(jnp.int32), 1.0, 0.0).astype(BF16)
        for c0 in range(0, GS_ROWS, GS_COLS):
            pt_c = pt_scr[:, c0:c0 + GS_COLS]
            xs_scr[c0:c0 + GS_COLS, :] = _dg(pt_c, h_ref[...], TN).astype(BF16)
            gs_scr[c0:c0 + GS_COLS, :] = _dot_sel_l(pt_c, gate_ref[...], TN)
        acc_scr[...] = jnp.zeros_like(acc_scr)

    g = e // EXPERTS_PER_GROUP
    seg_start = seg_smem[2 * g]
    seg_len = seg_smem[2 * g + 1]
    wg = wg_ref[0, 0]
    wu = wu_ref[0, 0]
    wd = wd_ref[0, 0]

    def window(c, carry):
        rows = pl.ds(pl.multiple_of(seg_start + c * GS_CAP, GS_ALIGN), GS_CAP)
        x = xs_scr[rows, :]
        a = jnp.dot(x, wg, preferred_element_type=F32)
        u = jnp.dot(x, wu, preferred_element_type=F32)
        gates = gs_scr[rows, :]
        ge = jnp.sum(jnp.where(_iota(gates.shape, 1) == e, gates, 0.0), axis=-1, keepdims=True)
        hid = (_silu(a) * u * ge).astype(BF16)
        acc_scr[rows, :] += jnp.dot(hid, wd, preferred_element_type=F32)
        return carry

    lax.fori_loop(0, (seg_len + GS_CAP - 1) // GS_CAP, window, 0)

    @pl.when(e == N_EXPERTS - 1)
    def _():
        pt = pt_scr[...]
        for c0 in range(0, D_MODEL, GS_COLS):
            acc_b = acc_scr[:, c0:c0 + GS_COLS].astype(BF16)
            f_ref[:, c0:c0 + GS_COLS] = jnp.dot(pt, acc_b, preferred_element_type=F32).astype(BF16)


def _moe_sorted(st, h2, gates, keep, wg_b, wu_b, wd_b, l):
    row = lambda i, e: (i, 0)
    wspec = lambda a, b: pl.BlockSpec((1, 1, a, b), lambda i, e: (l, e, 0, 0))
    return pl.pallas_call(
        _moe_sorted_kernel,
        grid=(st.rows // GS_BLK, N_EXPERTS),
        in_specs=[pl.BlockSpec((GS_BLK, D_MODEL), row, pipeline_mode=pl.Buffered(1)),
                  pl.BlockSpec((GS_BLK, N_EXPERTS), row), pl.BlockSpec((GS_BLK, N_EXPERTS), row),
                  wspec(D_MODEL, D_EXPERT), wspec(D_MODEL, D_EXPERT), wspec(D_EXPERT, D_MODEL)],
        out_specs=pl.BlockSpec((GS_BLK, D_MODEL), row),
        out_shape=jax.ShapeDtypeStruct((st.rows, D_MODEL), BF16),
        scratch_shapes=[pltpu.VMEM((GS_ROWS, D_MODEL), BF16), pltpu.VMEM((GS_ROWS, N_EXPERTS), F32),
                        pltpu.VMEM((GS_BLK, GS_ROWS), BF16), pltpu.VMEM((GS_ROWS, D_MODEL), F32),
                        pltpu.SMEM((2 * N_EXPERT_GROUPS,), jnp.int32)],
        compiler_params=pltpu.CompilerParams(dimension_semantics=("parallel", "arbitrary"),
                                             vmem_limit_bytes=GS_VMEM_LIMIT),
        name="moe_group_sorted",
    )(h2, gates, keep, wg_b, wu_b, wd_b)


def _ln2f_kernel(x_ref, f_ref, mod_ref, g_ref, b_ref, o_ref):
    gate2 = mod_ref[0, 0][:, 5 * D_MODEL:6 * D_MODEL]
    o_ref[...] = _layer_norm(DEEPNORM_ALPHA * x_ref[...] + gate2 * f_ref[...].astype(F32), g_ref[0], b_ref[0])


def _ln2f(st, x1, f, mod4, ln_g, ln_b, l):
    row = lambda i: (i, 0)
    return pl.pallas_call(
        _ln2f_kernel,
        grid=(st.rows // LN_TM,),
        in_specs=[pl.BlockSpec((LN_TM, D_MODEL), row),
                  pl.BlockSpec((LN_TM, D_MODEL), row),
                  pl.BlockSpec((1, 1, 1, 6 * D_MODEL), lambda i: (l, st.mod_row(i, LN_TM), 0, 0)),
                  pl.BlockSpec((1, 1, D_MODEL), lambda i: (l, 0, 0)),
                  pl.BlockSpec((1, 1, D_MODEL), lambda i: (l, 0, 0))],
        out_specs=pl.BlockSpec((LN_TM, D_MODEL), row),
        out_shape=jax.ShapeDtypeStruct((st.rows, D_MODEL), F32),
        compiler_params=_cparams("parallel"),
        name="ln2",
    )(x1, f, mod4, ln_g.reshape(DEPTH, 1, D_MODEL), ln_b.reshape(DEPTH, 1, D_MODEL))


S5_TC = 128
S5_SEQS = SUB
S5_Q = 4
S5_QS = S5_GROUPS // S5_Q * S5_STATE
N_S5 = S5_GROUPS * S5_STATE


def _s5_prep_kernel(lr_ref, li_ref, ls_ref, br_ref, bi_ref, e_ref, lbr_ref, lbi_ref, bbr_ref, bbi_ref):
    lr = lr_ref[0]
    li = li_ref[0]
    dt = jnp.exp(ls_ref[0])
    mag = jnp.exp(lr * dt)
    ang = li * dt
    ar = mag * jnp.cos(ang)
    ai = mag * jnp.sin(ang)
    lbr_ref[0] = ar
    lbi_ref[0] = ai
    den = lr * lr + li * li
    nr = ar - 1.0
    cr = (nr * lr + ai * li) / den
    ci = (ai * lr - nr * li) / den
    cr = _dot_sel_l(e_ref[...], cr)
    ci = _dot_sel_l(e_ref[...], ci)
    bre = br_ref[0]
    bim = bi_ref[0]
    bbr_ref[0] = cr * bre - ci * bim
    bbi_ref[0] = cr * bim + ci * bre


def _s5_prep(lam_re, lam_im, log_step, b_re, b_im):
    d2 = DEPTH * 2
    g, p, h = S5_GROUPS, S5_STATE, S5_CH
    bt_re = jnp.swapaxes(b_re, -1, -2).reshape(d2, g * h, p)
    bt_im = jnp.swapaxes(b_im, -1, -2).reshape(d2, g * h, p)
    expand = jnp.asarray(np.kron(np.eye(g, dtype=np.float32), np.ones((h, 1), np.float32)))
    spec_gp = pl.BlockSpec((1, g, p), lambda i: (i, 0, 0))
    spec_b = pl.BlockSpec((1, g * h, p), lambda i: (i, 0, 0))
    lbr, lbi, bbr, bbi = pl.pallas_call(
        _s5_prep_kernel,
        grid=(d2,),
        in_specs=[spec_gp, spec_gp, pl.BlockSpec((1, g, 1), lambda i: (i, 0, 0)), spec_b, spec_b,
                  pl.BlockSpec((g * h, g), lambda i: (0, 0))],
        out_specs=[spec_gp, spec_gp, spec_b, spec_b],
        out_shape=[jax.ShapeDtypeStruct((d2, g, p), F32)] * 2 + [jax.ShapeDtypeStruct((d2, g * h, p), F32)] * 2,
        compiler_params=_cparams("parallel"),
        name="s5_prep",
    )(lam_re.reshape(d2, g, p), lam_im.reshape(d2, g, p), log_step.reshape(d2, g, 1), bt_re, bt_im, expand)
    return lbr, lbi, bbr, bbi


def _s5_block_params(lbr, lbi, bbr, bbi, c_re, c_im):
    d2 = DEPTH * 2
    gq = S5_GROUPS // S5_Q
    eye = jnp.eye(gq, dtype=F32)

    def b_blocks(b):
        b = b.reshape(d2, S5_Q, gq, S5_CH, S5_STATE)
        return jnp.einsum('djghp,gk->djghkp', b, eye).reshape(d2, S5_Q, gq * S5_CH, gq * S5_STATE)

    def c_blocks(c):
        c = c.reshape(d2, S5_Q, gq, S5_CH, S5_STATE)
        return jnp.einsum('djghp,gk->djgpkh', c, eye).reshape(d2, S5_Q, gq * S5_STATE, gq * S5_CH)

    bq = jnp.concatenate([b_blocks(bbr), b_blocks(bbi)], axis=-1)
    cq = jnp.concatenate([c_blocks(c_re), -c_blocks(c_im)], axis=-2)
    return bq, cq, lbr.reshape(d2, 1, N_S5), lbi.reshape(d2, 1, N_S5)


def _s5_scan_kernel(u_ref, bq_ref, cq_ref, lr_ref, li_ref, s0r_ref, s0i_ref, y_ref, sfr_ref, sfi_ref,
                    utb, bur, bui, ytb, sr_scr, si_scr):
    d = pl.program_id(1)
    c = pl.program_id(2)
    nrow = S5_TC * S5_SEQS
    cw = D_GROUP // S5_Q

    @pl.when(c == 0)
    def _():
        sr_scr[...] = s0r_ref[0, 0]
        si_scr[...] = s0i_ref[0, 0]

    for b in range(S5_SEQS):
        for j in range(S5_Q):
            utb[j, pl.ds(b, S5_TC, stride=S5_SEQS), :] = u_ref[b, :, j * cw:(j + 1) * cw]
    for j in range(S5_Q):
        bu = _dot1(utb[j], bq_ref[0, j])
        bur[:, j * S5_QS:(j + 1) * S5_QS] = bu[:, :S5_QS]
        bui[:, j * S5_QS:(j + 1) * S5_QS] = bu[:, S5_QS:]
    for j in range(S5_Q):
        sl = slice(j * S5_QS, (j + 1) * S5_QS)
        lam_r = jnp.broadcast_to(lr_ref[0][:, sl], (S5_SEQS, S5_QS))
        lam_i = jnp.broadcast_to(li_ref[0][:, sl], (S5_SEQS, S5_QS))

        def step(t, carry, sl=sl, lam_r=lam_r, lam_i=lam_i):
            sr, si = carry
            te = jnp.where(d == 0, t, S5_TC - 1 - t)
            rows = pl.ds(pl.multiple_of(te * S5_SEQS, S5_SEQS), S5_SEQS)
            nr = lam_r * sr - lam_i * si + bur[rows, sl]
            ni = lam_r * si + lam_i * sr + bui[rows, sl]
            bur[rows, sl] = nr
            bui[rows, sl] = ni
            return nr, ni

        sr, si = lax.fori_loop(0, S5_TC, step, (sr_scr[:, sl], si_scr[:, sl]))
        sr_scr[:, sl] = sr
        si_scr[:, sl] = si
    for j in range(S5_Q):
        sl = slice(j * S5_QS, (j + 1) * S5_QS)
        yj = _dot1(bur[:, sl], cq_ref[0, j, :S5_QS, :]) + _dot1(bui[:, sl], cq_ref[0, j, S5_QS:, :])
        ytb[j] = yj
    for b in range(S5_SEQS):
        for j in range(S5_Q):
            y_ref[0, b, :, j * cw:(j + 1) * cw] = ytb[j, pl.ds(b, S5_TC, stride=S5_SEQS), :]

    @pl.when(c == pl.num_programs(2) - 1)
    def _():
        sfr_ref[0, 0] = sr_scr[...]
        sfi_ref[0, 0] = si_scr[...]


def _s5_scan(st, z3, bq, cq, lamr, lami, s0r, s0i, l):
    ng, nc = st.nb // S5_SEQS, st.L // S5_TC
    nrow = S5_TC * S5_SEQS
    chunk = lambda d, c: c + d * (nc - 1 - 2 * c)
    par = lambda g, d, c: (2 * l + d, 0, 0, 0)
    st_spec = pl.BlockSpec((1, 1, S5_SEQS, N_S5), lambda g, d, c: (g, d, 0, 0))
    return pl.pallas_call(
        _s5_scan_kernel,
        grid=(ng, 2, nc),
        in_specs=[pl.BlockSpec((S5_SEQS, S5_TC, D_GROUP), lambda g, d, c: (g, chunk(d, c), 0)),
                  pl.BlockSpec((1, S5_Q, D_GROUP // S5_Q, 2 * S5_QS), par),
                  pl.BlockSpec((1, S5_Q, 2 * S5_QS, D_GROUP // S5_Q), par),
                  pl.BlockSpec((1, 1, N_S5), lambda g, d, c: (2 * l + d, 0, 0)),
                  pl.BlockSpec((1, 1, N_S5), lambda g, d, c: (2 * l + d, 0, 0)),
                  st_spec, st_spec],
        out_specs=[pl.BlockSpec((1, S5_SEQS, S5_TC, D_GROUP), lambda g, d, c: (d, g, chunk(d, c), 0)),
                   st_spec, st_spec],
        out_shape=[jax.ShapeDtypeStruct((2, st.nb, st.L, D_GROUP), F32),
                   jax.ShapeDtypeStruct((ng, 2, S5_SEQS, N_S5), F32),
                   jax.ShapeDtypeStruct((ng, 2, S5_SEQS, N_S5), F32)],
        scratch_shapes=[pltpu.VMEM((S5_Q, nrow, LANE), F32), pltpu.VMEM((nrow, N_S5), F32),
                        pltpu.VMEM((nrow, N_S5), F32), pltpu.VMEM((S5_Q, nrow, LANE), F32),
                        pltpu.VMEM((S5_SEQS, N_S5), F32), pltpu.VMEM((S5_SEQS, N_S5), F32)],
        compiler_params=_cparams("parallel", "arbitrary", "arbitrary"),
        name="s5_scan",
    )(z3, bq, cq, lamr, lami, s0r, s0i)


S5_TM = 512


def _gelu_tanh(x):
    return 0.5 * x * (1.0 + jnp.tanh(math.sqrt(2.0 / math.pi) * (x + 0.044715 * (x * x * x))))


def _s5_out_kernel(u_ref, yf_ref, yb_ref, d_ref, w_ref, o_ref):
    y = yf_ref[0] + yb_ref[0] + d_ref[0] * u_ref[...]
    y = _gelu_tanh(y)
    o_ref[...] = (y * _sigmoid(_dot1(y, w_ref[0]))).astype(BF16)


def _s5_out(st, z, ydir, d_skip, w_glu, l):
    return pl.pallas_call(
        _s5_out_kernel,
        grid=(st.rows // S5_TM,),
        in_specs=[pl.BlockSpec((S5_TM, D_GROUP), lambda i: (i, 0)),
                  pl.BlockSpec((1, S5_TM, D_GROUP), lambda i: (0, i, 0)),
                  pl.BlockSpec((1, S5_TM, D_GROUP), lambda i: (1, i, 0)),
                  pl.BlockSpec((1, 1, D_GROUP), lambda i: (l, 0, 0)),
                  pl.BlockSpec((1, D_GROUP, D_GROUP), lambda i: (l, 0, 0))],
        out_specs=pl.BlockSpec((S5_TM, D_GROUP), lambda i: (i, 0)),
        out_shape=jax.ShapeDtypeStruct((st.rows, D_GROUP), BF16),
        compiler_params=_cparams("parallel"),
        name="s5_out",
    )(z, ydir, ydir, d_skip.reshape(DEPTH, 1, D_GROUP), w_glu)


NA_SCALE = NA_HEAD ** -0.5
NA_ROWS = DEC_SEQ // GRID_W
NA_KH = min(NA_WIN_H, NA_ROWS)
NA_QCOL = NA_OFF // LANE
NA_KCOL = (NA_OFF + D_GROUP) // LANE
NA_VCOL = (NA_OFF + 2 * D_GROUP) // LANE
NA_NDR = 2 * NA_WIN_H - 1
NA_AHEAD = 1


def _na_ctx_kernel(q_ref, k_ref, v_ref, o_ref, nk_ref, nv_ref):
    for hh in range(2):
        sl = slice(hh * NA_HEAD, (hh + 1) * NA_HEAD)
        q = q_ref[0, :, sl]
        k = k_ref[0, :, sl]
        v = v_ref[0, :, sl]
        nk_ref[0, hh] = k
        nv_ref[0, hh] = v
        s = _dot1(q, k, NT) * NA_SCALE
        e = jnp.exp(s - jnp.max(s, axis=-1, keepdims=True))
        o = _dot1(e, v) / jnp.sum(e, axis=-1, keepdims=True)
        o_ref[0, :, sl] = o.astype(BF16)


def _na_ctx(st, z3):
    blk = lambda col: pl.BlockSpec((1, st.L, LANE), lambda b, p: (b, 0, col + p))
    kv_spec = pl.BlockSpec((1, 2, st.L, NA_HEAD), lambda b, p: (b, p, 0, 0))
    kv_shape = jax.ShapeDtypeStruct((st.nb, NA_HEADS, st.L, NA_HEAD), F32)
    return pl.pallas_call(
        _na_ctx_kernel,
        grid=(st.nb, NA_HEADS // 2),
        in_specs=[blk(NA_QCOL), blk(NA_KCOL), blk(NA_VCOL)],
        out_specs=[pl.BlockSpec((1, st.L, LANE), lambda b, p: (b, 0, p)), kv_spec, kv_spec],
        out_shape=[jax.ShapeDtypeStruct((st.nb, st.L, D_GROUP), BF16), kv_shape, kv_shape],
        compiler_params=_cparams("parallel", "parallel"),
        name="na_context",
    )(z3, z3, z3)


def _na_nbr_kernel(q_ref, k_ref, v_ref, ck_ref, cv_ref, bias_ref, o_ref):
    nloc = NA_KH * GRID_W
    qc = _iota((GRID_W, nloc), 0)
    kc = _iota((GRID_W, nloc), 1) % GRID_W
    cs = jnp.clip(qc - NA_WIN_W // 2, 0, GRID_W - NA_WIN_W)
    col_in = (kc >= cs) & (kc < cs + NA_WIN_W)

    def scores(hh, r):
        sl = slice(hh * NA_HEAD, (hh + 1) * NA_HEAD)
        rs = min(max(r - NA_KH // 2, 0), NA_ROWS - NA_KH)
        q = q_ref[0, r * GRID_W:(r + 1) * GRID_W, sl]
        k = k_ref[0, rs * GRID_W:rs * GRID_W + nloc, sl]
        off = (rs - r + NA_WIN_H - 1) * GRID_W
        s_loc = _dot1(q, k, NT) * NA_SCALE + bias_ref[hh, :, off:off + nloc]
        s_loc = jnp.where(col_in, s_loc, NEG_INF)
        s_ctx = _dot1(q, ck_ref[0, 0, hh], NT) * NA_SCALE
        return s_loc, s_ctx

    def finish(hh, r, s_loc, s_ctx):
        sl = slice(hh * NA_HEAD, (hh + 1) * NA_HEAD)
        rs = min(max(r - NA_KH // 2, 0), NA_ROWS - NA_KH)
        v = v_ref[0, rs * GRID_W:rs * GRID_W + nloc, sl]
        m = jnp.maximum(jnp.max(s_loc, axis=-1, keepdims=True), jnp.max(s_ctx, axis=-1, keepdims=True))
        e_loc = jnp.exp(s_loc - m)
        e_ctx = jnp.exp(s_ctx - m)
        den = jnp.sum(e_loc, axis=-1, keepdims=True) + jnp.sum(e_ctx, axis=-1, keepdims=True)
        o = (_dot1(e_loc, v) + _dot1(e_ctx, cv_ref[0, 0, hh])) / den
        o_ref[0, r * GRID_W:(r + 1) * GRID_W, sl] = o.astype(BF16)

    blocks = [(hh, r) for hh in range(2) for r in range(NA_ROWS)]
    pending = []
    for i, blk in enumerate(blocks):
        pending.append(scores(*blk))
        if i >= NA_AHEAD:
            finish(*blocks[i - NA_AHEAD], *pending.pop(0))
    for j in range(len(blocks) - NA_AHEAD, len(blocks)):
        finish(*blocks[j], *pending.pop(0))


def _na_bias_table(rpb):
    qc = np.arange(GRID_W)[:, None]
    kc = np.arange(GRID_W)[None, :]
    dc = np.clip(kc - qc, -(NA_WIN_W - 1), NA_WIN_W - 1) + (NA_WIN_W - 1)
    t = rpb[:, :, :, dc]
    return jnp.transpose(t, (0, 1, 3, 2, 4)).reshape(DEPTH, NA_HEADS, GRID_W, NA_NDR * GRID_W)


def _na_nbr(st, z3, cache_k, cache_v, bias_tab, l):
    blk = lambda col: pl.BlockSpec((1, st.L, LANE), lambda b, p: (b, 0, col + p))
    cspec = pl.BlockSpec((1, 1, 2, PAST_LEN, NA_HEAD), lambda b, p: (b, l, p, 0, 0))
    return pl.pallas_call(
        _na_nbr_kernel,
        grid=(st.nb, NA_HEADS // 2),
        in_specs=[blk(NA_QCOL), blk(NA_KCOL), blk(NA_VCOL), cspec, cspec,
                  pl.BlockSpec((None, 2, GRID_W, NA_NDR * GRID_W), lambda b, p: (l, p, 0, 0))],
        out_specs=pl.BlockSpec((1, st.L, LANE), lambda b, p: (b, 0, p)),
        out_shape=jax.ShapeDtypeStruct((st.nb, st.L, D_GROUP), BF16),
        compiler_params=_cparams("parallel", "parallel"),
        name="na_neighbourhood",
    )(z3, z3, z3, cache_k, cache_v, bias_tab)


ML_COL = ML_OFF // LANE
ML_SCALE = ML_HEAD ** -0.5


def _log_sigmoid(x):
    return -_softplus(-x)


def _rope_tables(L):
    half = ML_HEAD // 2
    quarter = half // 2
    t = np.arange(L)
    inv_freq = ROPE_BASE ** (-np.arange(quarter, dtype=np.float32) / quarter)

    def tabs(pos):
        ang = pos.astype(np.float32)[:, None] * inv_freq[None, :].astype(np.float32)
        c, s = np.cos(ang), np.sin(ang)
        return np.concatenate([c, c], axis=-1), np.concatenate([-s, s], axis=-1)

    c1, s1 = tabs(t // GRID_W)
    c2, s2 = tabs(t % GRID_W)
    return (jnp.asarray(np.concatenate([c1, c2], axis=-1), F32),
            jnp.asarray(np.concatenate([s1, s2], axis=-1), F32))


def _ml_kernel(*refs, L, rotary, layer):
    nh = ML_HEADS
    ib_ref, fb_ref = refs[0], refs[1]
    q_refs, k_refs, v_refs, o_refs = (refs[2 + i * nh:2 + (i + 1) * nh] for i in range(4))
    (g_ref, cos_ref, sin_ref, c0_ref, n0_ref, m0_ref, lng_ref, lnb_ref,
     y_ref, cf_ref, nf_ref, mf_ref, qs, ks, hf, hb, c_scr, n_scr, m_scr) = refs[2 + 4 * nh:]
    T = ML_CHUNK
    nc = L // T

    if rotary:
        first = (_iota((L, ML_HEAD), 1) % (ML_HEAD // 2)) < ML_HEAD // 4

        def rope(x):
            quarter = ML_HEAD // 4
            partner = jnp.where(first, pltpu.roll(x, ML_HEAD - quarter, axis=1), pltpu.roll(x, quarter, axis=1))
            return x * cos_ref[...] + partner * sin_ref[...]
    else:
        rope = lambda x: x
    for h in range(nh):
        qs[h] = rope(q_refs[h][0]) * ML_SCALE
        ks[h] = rope(k_refs[h][0])
        for d in range(2):
            c_scr[d, h] = c0_ref[0, d, h]
            n_scr[d, h] = n0_ref[0, h, d:d + 1, :]
            m_scr[d, h] = m0_ref[0, h, d:d + 1, :]

    tj = _iota((T, T), 0)
    ts = _iota((T, T), 1)
    ones = jnp.ones((T, T), F32)
    upto_row = (tj <= ts, tj >= ts)
    upto_col = (ts <= tj, ts >= tj)

    def body(ci, carry):
        ch = []
        for d in range(2):
            cd = ci if d == 0 else nc - 1 - ci
            rows = pl.ds(pl.multiple_of(cd * T, T), T)
            for h in range(nh):
                g = g_ref[0, h, cd]
                ig = g[:, d:d + 1] + ib_ref[layer * 2 * nh + d * nh + h]
                fg = g[:, 2 + d:3 + d] + fb_ref[layer * 2 * nh + d * nh + h]
                lf = jnp.broadcast_to(_log_sigmoid(fg), (T, T))
                igb = jnp.broadcast_to(ig, (T, T))
                q = dict(d=d, h=h, rows=rows, ig=ig, qc=qs[h, rows, :], kc=ks[h, rows, :], vc=v_refs[h][0, rows, :])
                q['bcol'] = _dot_sel_l(jnp.where(upto_col[d], 1.0, 0.0), lf)
                q['brow_i'] = _dot_sel_l(ones, jnp.where(upto_row[d], lf, 0.0) - jnp.where(tj == ts, igb, 0.0))
                ch.append(q)
        for q in ch:
            q['qk'] = _dot1(q['qc'], q['kc'], NT)
            q['c_old'] = c_scr[q['d'], q['h']]
            q['qc_c'] = _dot1(q['qc'], q['c_old'])
        for q in ch:
            d, h = q['d'], q['h']
            dmat = jnp.where(upto_col[d], q['bcol'] - q['brow_i'], -jnp.inf)
            b1 = q['bcol'][:, 0:1]
            m_prev = m_scr[d, h]
            inter = b1 + m_prev
            m_t = jnp.maximum(inter, jnp.max(dmat, axis=-1, keepdims=True))
            qk = q['qk'] * jnp.exp(dmat - m_t)
            w_inter = jnp.exp(inter - m_t)
            n_old = n_scr[d, h]
            den = w_inter * jnp.sum(q['qc'] * n_old, axis=-1, keepdims=True) + jnp.sum(qk, axis=-1, keepdims=True)
            q['scale'] = 1.0 / jnp.maximum(jnp.abs(den), jnp.exp(-m_t))
            q['inter_part'] = w_inter * q['qc_c']
            q['intra'] = _dot1(qk, q['vc'])
            b_last = b1[T - 1:T, :] if d == 0 else b1[0:1, :]
            g_s = b_last - b1 + q['ig']
            m_new = jnp.maximum(b_last + m_prev, jnp.max(g_s, axis=0, keepdims=True))
            w_old = jnp.exp(b_last + m_prev - m_new)
            w_s = jnp.exp(g_s - m_new)
            q['c_new'] = _dot1(q['kc'], w_s * q['vc'], TN)
            q['w_old'] = w_old
            n_scr[d, h] = w_old * n_old + jnp.sum(w_s * q['kc'], axis=0, keepdims=True)
            m_scr[d, h] = m_new
        for q in ch:
            d, h = q['d'], q['h']
            hcur = (q['inter_part'] + q['intra']) * q['scale']
            if d == 0:
                hf[h, q['rows'], :] = hcur
            else:
                hb[h, q['rows'], :] = hcur
            c_scr[d, h] = q['w_old'] * q['c_old'] + q['c_new']
        return carry

    lax.fori_loop(0, nc, body, 0)

    for h in range(nh):
        hsum = hf[h] + hb[h]
        mu = jnp.mean(hsum, axis=-1, keepdims=True)
        dv = hsum - mu
        var = jnp.mean(dv * dv, axis=-1, keepdims=True)
        hn = dv * lax.rsqrt(var + ML_GN_EPS)
        y_ref[0, :, h * ML_HEAD:(h + 1) * ML_HEAD] = (
            _sigmoid(o_refs[h][0]) * (hn * lng_ref[0, :, h * ML_HEAD:(h + 1) * ML_HEAD]
                                      + lnb_ref[0, :, h * ML_HEAD:(h + 1) * ML_HEAD])).astype(BF16)
        for d in range(2):
            cf_ref[0, d, h] = c_scr[d, h]
            nf_ref[0, h, d:d + 1, :] = n_scr[d, h]
            mf_ref[0, h, d:d + 1, :] = m_scr[d, h]


def _ml_mixer(st, z3, zg, i_bias, f_bias, ln_g, ln_b, c0, n0, m0, l):
    L, nb, nh = st.L, st.nb, ML_HEADS
    nc = L // ML_CHUNK
    gates = zg.reshape(nb, L, 2, 2, nh).transpose(0, 4, 1, 2, 3).reshape(nb, nh, nc, ML_CHUNK, 4)
    cos_t, sin_t = _rope_tables(L)
    blk = lambda col: pl.BlockSpec((1, L, LANE), lambda b: (b, 0, col))
    head_blocks = [blk(ML_COL + part * nh + h) for part in range(4) for h in range(nh)]
    smem = pl.BlockSpec(memory_space=pltpu.SMEM)
    tab = pl.BlockSpec((L, ML_HEAD), lambda b: (0, 0))
    c_spec = pl.BlockSpec((1, 2, nh, ML_HEAD, ML_HEAD), lambda b: (b, 0, 0, 0, 0))
    n_spec = pl.BlockSpec((1, nh, 2, ML_HEAD), lambda b: (b, 0, 0, 0))
    m_spec = pl.BlockSpec((1, nh, 2, 1), lambda b: (b, 0, 0, 0))
    par = pl.BlockSpec((1, 1, D_GROUP), lambda b: (l, 0, 0))
    return pl.pallas_call(
        functools.partial(_ml_kernel, L=L, rotary=st.latent, layer=l),
        grid=(nb,),
        in_specs=[smem, smem] + head_blocks + [
            pl.BlockSpec((1, nh, nc, ML_CHUNK, 4), lambda b: (b, 0, 0, 0, 0)),
            tab, tab, c_spec, n_spec, m_spec, par, par],
        out_specs=[pl.BlockSpec((1, L, D_GROUP), lambda b: (b, 0, 0)), c_spec, n_spec, m_spec],
        out_shape=[jax.ShapeDtypeStruct((nb, L, D_GROUP), BF16),
                   jax.ShapeDtypeStruct((nb, 2, nh, ML_HEAD, ML_HEAD), F32),
                   jax.ShapeDtypeStruct((nb, nh, 2, ML_HEAD), F32),
                   jax.ShapeDtypeStruct((nb, nh, 2, 1), F32)],
        scratch_shapes=[pltpu.VMEM((nh, L, ML_HEAD), F32), pltpu.VMEM((nh, L, ML_HEAD), F32),
                        pltpu.VMEM((nh, L, ML_HEAD), F32), pltpu.VMEM((nh, L, ML_HEAD), F32),
                        pltpu.VMEM((2, nh, ML_HEAD, ML_HEAD), F32), pltpu.VMEM((2, nh, 1, ML_HEAD), F32),
                        pltpu.VMEM((2, nh, 1, 1), F32)],
        compiler_params=_cparams("parallel"),
        name="mlstm",
    )(i_bias.reshape(-1), f_bias.reshape(-1), *([z3] * (4 * nh)), gates, cos_t, sin_t, c0, n0, m0,
      ln_g.reshape(DEPTH, 1, D_GROUP), ln_b.reshape(DEPTH, 1, D_GROUP))


RW_TL = 256
RW_T = 64
RW_GROUP = 16
RW_RCOL = RW_OFF // D_GROUP
RW_LCOL = (RW_OFF + 3 * D_GROUP) // (2 * LANE)
N_LORA = RW_LORA_W + RW_LORA_A + RW_LORA_G


def _head_ones():
    return jnp.asarray(np.kron(np.eye(RW_HEADS, dtype=np.float32), np.ones((RW_HEAD, RW_HEAD), np.float32)))


def _rw_front_kernel(r_ref, k_ref, v_ref, lo_ref, rp_ref, kp_ref, vp_ref, lp_ref, rn_ref, kn_ref, vn_ref, ln_ref,
                     mup_ref, mun_ref, w0_ref, w2_ref, a0_ref, a2_ref, g2_ref, kk_ref, ka_ref, hones_ref,
                     ro_ref, vo_ref, kko_ref, go_ref, kd_ref, bd_ref, lw_ref):
    t = pl.program_id(1)
    first = t == 0
    last = t == pl.num_programs(1) - 1
    row = _iota((RW_TL, 1), 0)

    def shift(x_ref, p_ref, n_ref, lo, hi):
        x = x_ref[0]
        prev_edge = jnp.where(first, 0.0, p_ref[0, SUB - 1:SUB, :])
        next_edge = jnp.where(last, 0.0, n_ref[0, 0:1, :])
        prev = jnp.where(row == 0, prev_edge, pltpu.roll(x, 1, axis=0))
        nxt = jnp.where(row == RW_TL - 1, next_edge, pltpu.roll(x, RW_TL - 1, axis=0))
        return x + mup_ref[0][:, lo:hi] * (prev - x) + mun_ref[0][:, lo:hi] * (nxt - x)

    g = D_GROUP
    r = shift(r_ref, rp_ref, rn_ref, 0, g)
    k = shift(k_ref, kp_ref, kn_ref, g, 2 * g)
    v = shift(v_ref, vp_ref, vn_ref, 2 * g, 3 * g)
    lo = shift(lo_ref, lp_ref, ln_ref, 3 * g, 3 * g + N_LORA)
    zw = lo[:, :RW_LORA_W]
    za = lo[:, RW_LORA_W:RW_LORA_W + RW_LORA_A]
    zg = lo[:, RW_LORA_W + RW_LORA_A:]
    ro_ref[0] = r
    vo_ref[0] = v
    kk = k * kk_ref[0]
    ssq = _dot_sel_r(kk * kk, hones_ref[...])
    kk = kk * lax.rsqrt(ssq + 1e-12)
    kko_ref[0] = kk
    go_ref[0] = _dot3(_sigmoid(zg), g2_ref[0])
    tw = jnp.tanh(zw)
    for d in range(2):
        w_log = -_softplus(-(w0_ref[0, d:d + 1, :] + _dot3(tw, w2_ref[0, d]))) - 0.5
        lw_ref[d, 0] = -jnp.exp(w_log)
        a = _sigmoid(a0_ref[0, d:d + 1, :] + _dot3(za, a2_ref[0, d]))
        kd_ref[d, 0] = k * (1.0 + (a - 1.0) * ka_ref[0])
        bd_ref[d, 0] = kk * a


def _rw_front(st, z3, mu_prev, mu_next, w0, w2, a0, a2, g2, k_k, k_a, l):
    nb, L = st.nb, st.L
    nt = L // RW_TL
    tpb = RW_TL // SUB
    main = lambda w, col: pl.BlockSpec((1, RW_TL, w), lambda b, t: (b, t, col))
    prev = lambda w, col: pl.BlockSpec((1, SUB, w), lambda b, t: (b, jnp.maximum(t * tpb - 1, 0), col))
    nxt = lambda w, col: pl.BlockSpec((1, SUB, w), lambda b, t: (b, jnp.minimum((t + 1) * tpb, L // SUB - 1), col))
    cols = [(D_GROUP, RW_RCOL), (D_GROUP, RW_RCOL + 1), (D_GROUP, RW_RCOL + 2), (2 * LANE, RW_LCOL)]
    lay = lambda *shape: pl.BlockSpec((1,) + shape, lambda b, t: (l,) + (0,) * len(shape))
    out1 = pl.BlockSpec((1, RW_TL, D_GROUP), lambda b, t: (b, t, 0))
    out2 = pl.BlockSpec((2, 1, RW_TL, D_GROUP), lambda b, t: (0, b, t, 0))
    s1 = jax.ShapeDtypeStruct((nb, L, D_GROUP), F32)
    s2 = jax.ShapeDtypeStruct((2, nb, L, D_GROUP), F32)
    return pl.pallas_call(
        _rw_front_kernel,
        grid=(nb, nt),
        in_specs=[main(*c) for c in cols] + [prev(*c) for c in cols] + [nxt(*c) for c in cols] + [
            lay(1, RW_IN), lay(1, RW_IN), lay(2, D_GROUP), lay(2, RW_LORA_W, D_GROUP), lay(2, D_GROUP),
            lay(2, RW_LORA_A, D_GROUP), lay(RW_LORA_G, D_GROUP), lay(1, D_GROUP), lay(1, D_GROUP),
            pl.BlockSpec((D_GROUP, D_GROUP), lambda b, t: (0, 0))],
        out_specs=[out1, out1, out1, out1, out2, out2, out2],
        out_shape=[s1, s1, s1, s1, s2, s2, s2],
        compiler_params=_cparams("parallel", "parallel"),
        name="rwkv_front",
    )(*([z3] * 12), mu_prev.reshape(DEPTH, 1, RW_IN), mu_next.reshape(DEPTH, 1, RW_IN), w0, w2, a0, a2, g2,
      k_k.reshape(DEPTH, 1, D_GROUP), k_a.reshape(DEPTH, 1, D_GROUP), _head_ones())


def _rw_core_kernel(rf_ref, vf_ref, kkf_ref, kdf_ref, bdf_ref, lwf_ref,
                    rb_ref, vb_ref, kkb_ref, kdb_ref, bdb_ref, lwb_ref, s0_ref,
                    yf_ref, yb_ref, sf_ref, s_scr):
    c = pl.program_id(1)
    T = RW_T

    @pl.when(c == 0)
    def _():
        s_scr[...] = s0_ref[0]

    tj = _iota((T, T), 0)
    ts = _iota((T, T), 1)
    dirs = ((rf_ref, vf_ref, kkf_ref, kdf_ref, bdf_ref, lwf_ref, yf_ref),
            (rb_ref, vb_ref, kkb_ref, kdb_ref, bdb_ref, lwb_ref, yb_ref))
    ch = []
    for d, (r_ref, v_ref, kk_ref, kd_ref, bd_ref, lw_ref, y_ref) in enumerate(dirs):
        if d == 0:
            incl = ts <= tj
            strict = ts < tj
            last = T - 1
        else:
            incl = ts >= tj
            strict = ts > tj
            last = 0
        lw = lw_ref[0, 0]
        cum = _dot_sel_l(jnp.where(incl, 1.0, 0.0), lw)
        w_in = jnp.exp(cum)
        w_inv = jnp.exp(-cum)
        w_ex = jnp.exp(cum - lw)
        kap_a = kk_ref[0] * w_ex
        bet_a = bd_ref[0, 0] * w_inv
        khat_a = kd_ref[0, 0] * w_inv
        rho_a = r_ref[0] * w_in
        w_tot = w_in[last:last + 1, :]
        v_a = v_ref[0]
        for h in range(RW_HEADS):
            sl = slice(h * RW_HEAD, (h + 1) * RW_HEAD)
            ch.append(dict(d=d, h=h, sl=sl, incl=incl, strict=strict, y_ref=y_ref, w_tot=w_tot[:, sl],
                           kap=kap_a[:, sl], bet=bet_a[:, sl], khat=khat_a[:, sl], rho=rho_a[:, sl], v=v_a[:, sl]))
    def solve(chs):
        for q in chs:
            kr = jnp.concatenate([q['kap'], q['rho']], axis=0)
            bk = jnp.concatenate([q['bet'], q['khat']], axis=0)
            q['gram'] = _dot1(kr, bk, NT)
        for q in chs:
            gram = q.pop('gram')
            q['n'] = -jnp.where(q['strict'], gram[:T, :T], 0.0)
            l_k = jnp.where(q['strict'], gram[:T, T:], 0.0)
            q['m_b'] = jnp.where(q['incl'], gram[T:, :T], 0.0)
            m_k = jnp.where(q['incl'], gram[T:, T:], 0.0)
            q['lmv'] = _dot1(jnp.concatenate([l_k, m_k], axis=0), q['v'])
        for q in chs:
            q['x'] = jnp.concatenate([q['kap'], q['lmv'][:T]], axis=1)
        for lvl in range(6):
            mm = _dot3 if lvl < 3 else _dot1
            for q in chs:
                if lvl < 5:
                    nx = mm(q['n'], jnp.concatenate([q['n'], q['x']], axis=1))
                    q['n'], q['x'] = nx[:, :T], q['x'] + nx[:, T:]
                else:
                    q['x'] = q['x'] + _dot1(q['n'], q['x'])
        for q in chs:
            q['p'] = _dot1(q['m_b'], q['x'])
            xb = _dot1(q['x'], q['bet'], TN)
            q['a_m'] = xb[:RW_HEAD]
            q['d_m'] = _dot1(q['v'], q['khat'], TN) - xb[RW_HEAD:]
        for q in chs:
            rho_p = q['rho'] - q['p'][:, :RW_HEAD]
            y_v = q['lmv'][T:] - q['p'][:, RW_HEAD:]
            s_old = s_scr[q['d'], q['h']]
            q['y_ref'][0, :, q['sl']] = _dot1(rho_p, s_old, NT) + y_v
            s_scr[q['d'], q['h']] = (s_old - _dot3(s_old, q['a_m']) + q['d_m']) * q['w_tot']

    for g0 in range(0, len(ch), RW_GROUP):
        solve(ch[g0:g0 + RW_GROUP])

    @pl.when(c == pl.num_programs(1) - 1)
    def _():
        sf_ref[0] = s_scr[...]


def _rw_core(st, r, v, kk, kd, bd, lw, s0):
    nb, L = st.nb, st.L
    nc = L // RW_T
    f1 = pl.BlockSpec((1, RW_T, D_GROUP), lambda b, c: (b, c, 0))
    b1 = pl.BlockSpec((1, RW_T, D_GROUP), lambda b, c: (b, nc - 1 - c, 0))
    f2 = pl.BlockSpec((1, 1, RW_T, D_GROUP), lambda b, c: (0, b, c, 0))
    b2 = pl.BlockSpec((1, 1, RW_T, D_GROUP), lambda b, c: (1, b, nc - 1 - c, 0))
    s_spec = pl.BlockSpec((1, 2, RW_HEADS, RW_HEAD, RW_HEAD), lambda b, c: (b, 0, 0, 0, 0))
    ys = jax.ShapeDtypeStruct((nb, L, D_GROUP), F32)
    return pl.pallas_call(
        _rw_core_kernel,
        grid=(nb, nc),
        in_specs=[f1, f1, f1, f2, f2, f2, b1, b1, b1, b2, b2, b2, s_spec],
        out_specs=[f1, b1, s_spec],
        out_shape=[ys, ys, jax.ShapeDtypeStruct((nb, 2, RW_HEADS, RW_HEAD, RW_HEAD), F32)],
        scratch_shapes=[pltpu.VMEM((2, RW_HEADS, RW_HEAD, RW_HEAD), F32)],
        compiler_params=_cparams("parallel", "arbitrary"),
        name="rwkv_core",
    )(r, v, kk, kd, bd, lw, r, v, kk, kd, bd, lw, s0)


RWO_TM = 512


def _rw_out_kernel(yf_ref, yb_ref, r_ref, v_ref, kd0_ref, kd1_ref, g_ref, lng_ref, lnb_ref, rk_ref, hones_ref, o_ref):
    y = yf_ref[...] + yb_ref[...]
    inv = 1.0 / RW_HEAD
    mu = _dot_sel_r(y, hones_ref[...]) * inv
    dv = y - mu
    var = _dot_sel_r(dv * dv, hones_ref[...]) * inv
    yn = dv * lax.rsqrt(var + RW_GN_EPS) * lng_ref[0] + lnb_ref[0]
    kmean = 0.5 * (kd0_ref[0] + kd1_ref[0])
    bonus = _dot_sel_r(r_ref[...] * kmean * rk_ref[0], hones_ref[...]) * v_ref[...]
    o_ref[...] = ((yn + bonus) * g_ref[...]).astype(BF16)


def _rw_out(st, yf, yb, r, v, kd, g, ln_g, ln_b, r_k, l):
    rows = st.rows
    flat = lambda a: a.reshape(rows, D_GROUP)
    row = pl.BlockSpec((RWO_TM, D_GROUP), lambda i: (i, 0))
    lay = pl.BlockSpec((1, 1, D_GROUP), lambda i: (l, 0, 0))
    kd2 = kd.reshape(2, rows, D_GROUP)
    return pl.pallas_call(
        _rw_out_kernel,
        grid=(rows // RWO_TM,),
        in_specs=[row, row, row, row,
                  pl.BlockSpec((1, RWO_TM, D_GROUP), lambda i: (0, i, 0)),
                  pl.BlockSpec((1, RWO_TM, D_GROUP), lambda i: (1, i, 0)),
                  row, lay, lay, lay, pl.BlockSpec((D_GROUP, D_GROUP), lambda i: (0, 0))],
        out_specs=row,
        out_shape=jax.ShapeDtypeStruct((rows, D_GROUP), BF16),
        compiler_params=_cparams("parallel"),
        name="rwkv_out",
    )(flat(yf), flat(yb), flat(r), flat(v), kd2, kd2, flat(g), ln_g.reshape(DEPTH, 1, D_GROUP),
      ln_b.reshape(DEPTH, 1, D_GROUP), r_k.reshape(DEPTH, 1, D_GROUP), _head_ones())


def _trunk_layer(st, x, l, mod4, p, states, ctx_kv):
    nb, L = st.nb, st.L
    z, zg = _zproj(st, x, mod4, p['w_in'], p['w_gate_cols'], l)
    z3 = z.reshape(nb, L, D_Z)

    ydir, sfr, sfi = _s5_scan(st, z3, p['s5_bq'], p['s5_cq'], p['s5_lamr'], p['s5_lami'],
                              states['s5_re'], states['s5_im'], l)
    y_s5 = _s5_out(st, z, ydir.reshape(2, st.rows, D_GROUP), p['s5_d'], p['s5_w_glu'], l)

    r, v, kk, g, kd, bd, lw = _rw_front(st, z3, p['rw_mu_prev'], p['rw_mu_next'], p['rw_w0'], p['rw_w2'],
                                        p['rw_a0'], p['rw_a2'], p['rw_g2'], p['rw_k_k'], p['rw_k_a'], l)
    yf, yb, rw_s = _rw_core(st, r, v, kk, kd, bd, lw, states['rw'])
    y_rw = _rw_out(st, yf, yb, r, v, kd, g, p['rw_ln_g'], p['rw_ln_b'], p['rw_r_k'], l)

    if ctx_kv is None:
        y_na, nk, nv = _na_ctx(st, z3)
    else:
        y_na = _na_nbr(st, z3, ctx_kv[0], ctx_kv[1], p['na_bias'], l)
        nk = nv = None

    y_ml, ml_c, ml_n, ml_m = _ml_mixer(st, z3, zg, p['ml_i_bias'], p['ml_f_bias'], p['ml_ln_g'], p['ml_ln_b'],
                                       states['ml_c'], states['ml_n'], states['ml_m'], l)

    ys = (y_s5, y_rw, y_na.reshape(st.rows, D_GROUP), y_ml.reshape(st.rows, D_GROUP))
    tail = _oproj(st, ys, x, mod4, p['w_out'], p['ln1_g'], p['ln1_b'], p['w_router'], p['b_router'], l)
    return tail, (nk, nv, sfr, sfi, rw_s, ml_c, ml_n, ml_m)


def _moe_and_ln2(tails, mod4, p, l):
    xs = []
    for st, (x1, h2, gates, keep) in zip((PROMPT, LATENT), tails):
        f = _moe_sorted(st, h2, gates, keep, p['moe_w_gate'], p['moe_w_up'], p['moe_w_down'], l)
        xs.append(_ln2f(st, x1, f, mod4, p['ln2_g'], p['ln2_b'], l))
    return xs


def kernel(x_prompt, x_sample, cache_nat_k, cache_nat_v, state_s5_re, state_s5_im, state_rwkv, state_mlstm_c, state_mlstm_n, state_mlstm_m, c, c_ctx, w_mod, b_mod, w_in, w_out, s5_lam_re, s5_lam_im, s5_log_step, s5_b_re, s5_b_im, s5_c_re, s5_c_im, s5_d, s5_w_glu, rw_mu_prev, rw_mu_next, rw_w0, rw_w2, rw_a0, rw_a2, rw_g2, rw_k_k, rw_k_a, rw_r_k, rw_ln_g, rw_ln_b, na_rpb, ml_i_bias, ml_f_bias, ml_ln_g, ml_ln_b, ln1_g, ln1_b, ln2_g, ln2_b, w_router, b_router, moe_w_gate, moe_w_up, moe_w_down):
    dt = x_prompt.dtype
    cond = jnp.concatenate([c_ctx[None, :], c, jnp.zeros((MOD_ROWS - 1 - DEC_BATCH, D_MODEL), F32)], axis=0)
    mod4 = _modulation(cond, w_mod, b_mod).reshape(DEPTH, MOD_ROWS, 1, 6 * D_MODEL)

    lbr, lbi, bbr, bbi = _s5_prep(s5_lam_re, s5_lam_im, s5_log_step, s5_b_re, s5_b_im)
    s5_bq, s5_cq, s5_lamr, s5_lami = _s5_block_params(lbr, lbi, bbr, bbi, s5_c_re, s5_c_im)
    w_in_b = w_in.astype(BF16)
    p = dict(w_in=w_in_b, w_gate_cols=w_in_b[:, :, D_Z:], w_out=w_out.astype(BF16),
             s5_bq=s5_bq, s5_cq=s5_cq, s5_lamr=s5_lamr, s5_lami=s5_lami, s5_d=s5_d, s5_w_glu=s5_w_glu,
             rw_mu_prev=rw_mu_prev, rw_mu_next=rw_mu_next, rw_w0=rw_w0, rw_w2=rw_w2, rw_a0=rw_a0, rw_a2=rw_a2,
             rw_g2=rw_g2, rw_k_k=rw_k_k, rw_k_a=rw_k_a, rw_r_k=rw_r_k, rw_ln_g=rw_ln_g, rw_ln_b=rw_ln_b,
             na_bias=_na_bias_table(na_rpb), ml_i_bias=ml_i_bias, ml_f_bias=ml_f_bias, ml_ln_g=ml_ln_g,
             ml_ln_b=ml_ln_b, ln1_g=ln1_g, ln1_b=ln1_b, ln2_g=ln2_g, ln2_b=ln2_b, w_router=w_router,
             b_router=b_router, moe_w_gate=moe_w_gate.astype(BF16), moe_w_up=moe_w_up.astype(BF16),
             moe_w_down=moe_w_down.astype(BF16))

    gp = BATCH // S5_SEQS
    zero_states = dict(
        s5_re=jnp.zeros((gp, 2, S5_SEQS, N_S5), F32), s5_im=jnp.zeros((gp, 2, S5_SEQS, N_S5), F32),
        rw=jnp.zeros((BATCH, 2, RW_HEADS, RW_HEAD, RW_HEAD), F32),
        ml_c=jnp.zeros((BATCH, 2, ML_HEADS, ML_HEAD, ML_HEAD), F32),
        ml_n=jnp.zeros((BATCH, ML_HEADS, 2, ML_HEAD), F32), ml_m=jnp.zeros((BATCH, ML_HEADS, 2, 1), F32))

    xp = x_prompt.reshape(PROMPT.rows, D_MODEL)
    xs = x_sample.reshape(LATENT.rows, D_MODEL)
    outs = [[] for _ in range(8)]
    for l in range(DEPTH):
        tail_p, ctx_t = _trunk_layer(PROMPT, xp, l, mod4, p, zero_states, None)
        for acc, t in zip(outs, ctx_t):
            acc.append(t)
        lat_states = dict(
            s5_re=state_s5_re[:, l].reshape(DEC_BATCH, 2, N_S5).transpose(1, 0, 2)[None],
            s5_im=state_s5_im[:, l].reshape(DEC_BATCH, 2, N_S5).transpose(1, 0, 2)[None],
            rw=state_rwkv[:, l], ml_c=state_mlstm_c[:, l],
            ml_n=state_mlstm_n[:, l].transpose(0, 2, 1, 3), ml_m=state_mlstm_m[:, l].transpose(0, 2, 1)[..., None])
        tail_s, _ = _trunk_layer(LATENT, xs, l, mod4, p, lat_states, (cache_nat_k, cache_nat_v))
        xp, xs = _moe_and_ln2((tail_p, tail_s), mod4, p, l)

    nk, nv, s5r, s5i, rw, mc, mn, mm = [jnp.stack(t, axis=1) for t in outs]

    def s5_state(t):
        return t.transpose(0, 3, 1, 2, 4).reshape(BATCH, DEPTH, 2, S5_GROUPS, S5_STATE)

    return (xp.reshape(BATCH, SEQ, D_MODEL), xs.reshape(DEC_BATCH, DEC_SEQ, D_MODEL),
            nk, nv, s5_state(s5r).astype(dt), s5_state(s5i).astype(dt), rw.astype(dt), mc.astype(dt),
            mn.transpose(0, 1, 3, 2, 4).astype(dt), mm[..., 0].transpose(0, 1, 3, 2).astype(dt))
```

```python
import functools
import math

import numpy as np
import jax
import jax.numpy as jnp
from jax import lax
from jax.experimental import pallas as pl
from jax.experimental.pallas import tpu as pltpu

F32 = jnp.float32
BF16 = jnp.bfloat16

D_MODEL = 2048
BATCH = 16
SEQ = 256
DEPTH = 4
DEC_BATCH = 8
DEC_SEQ = 1024
PAST_LEN = 256
GRID_W = 64
D_GROUP = D_MODEL // 4
S5_CH = 16
S5_GROUPS = D_GROUP // S5_CH
S5_STATE = 64
RW_HEAD = 64
RW_HEADS = D_GROUP // RW_HEAD
RW_LORA_W = 64
RW_LORA_A = 64
RW_LORA_G = 128
RW_GN_EPS = 64e-5
NA_HEAD = 64
NA_HEADS = D_GROUP // NA_HEAD
NA_WIN_H = 8
NA_WIN_W = 16
NEG_INF = -1e30
ML_HEAD = 128
ML_HEADS = D_GROUP // ML_HEAD
ML_CHUNK = 256
ML_GN_EPS = 1e-5
ROPE_BASE = 10000.0
N_EXPERTS = 16
N_EXPERT_GROUPS = 4
EXPERTS_PER_GROUP = N_EXPERTS // N_EXPERT_GROUPS
D_EXPERT = 512
DEEPNORM_ALPHA = (2 * DEPTH) ** 0.25
LN_EPS = 1e-5
S5_IN = D_GROUP
RW_IN = 3 * D_GROUP + RW_LORA_W + RW_LORA_A + RW_LORA_G
NA_IN = 3 * D_GROUP
ML_IN = 4 * D_GROUP + 4 * ML_HEADS
D_IN = S5_IN + RW_IN + NA_IN + ML_IN
N_GATE = 4 * ML_HEADS
D_Z = D_IN - N_GATE
RW_OFF = S5_IN
NA_OFF = S5_IN + RW_IN
ML_OFF = NA_OFF + NA_IN
MOD_ROWS = 16
LANE = 128
SUB = 8

VMEM_LIMIT = 56 * 1024 * 1024


def _cparams(*sem):
    return pltpu.CompilerParams(dimension_semantics=sem, vmem_limit_bytes=VMEM_LIMIT)


def _dg(a, b, dims):
    return lax.dot_general(a, b, (dims, ((), ())), preferred_element_type=F32)


NN = ((1,), (0,))
NT = ((1,), (1,))
TN = ((0,), (0,))


def _dot1(a, b, dims=NN):
    return _dg(a.astype(BF16), b.astype(BF16), dims)


def _split(x):
    hi = x.astype(BF16)
    lo = (x - hi.astype(F32)).astype(BF16)
    return hi, lo


def _split3(x):
    hi = x.astype(BF16)
    r = x - hi.astype(F32)
    mid = r.astype(BF16)
    lo = (r - mid.astype(F32)).astype(BF16)
    return hi, mid, lo


def _dot3(a, b, dims=NN):
    ah, al = _split(a)
    bh, bl = _split(b)
    return _dg(ah, bh, dims) + (_dg(ah, bl, dims) + _dg(al, bh, dims))


def _dot_sel_l(sel, x, dims=NN):
    s = sel.astype(BF16)
    hi, mid, lo = _split3(x)
    return _dg(s, hi, dims) + (_dg(s, mid, dims) + _dg(s, lo, dims))


def _dot_sel_r(x, sel, dims=NN):
    s = sel.astype(BF16)
    hi, mid, lo = _split3(x)
    return _dg(hi, s, dims) + (_dg(mid, s, dims) + _dg(lo, s, dims))


def _sigmoid(x):
    return 1.0 / (1.0 + jnp.exp(-x))


def _silu(x):
    return x * _sigmoid(x)


def _softplus(x):
    return jnp.maximum(x, 0.0) + jnp.log(1.0 + jnp.exp(-jnp.abs(x)))


def _iota(shape, dim):
    return lax.broadcasted_iota(jnp.int32, shape, dim)


class _Stream:
    def __init__(self, nb, L, latent):
        self.nb, self.L, self.latent = nb, L, latent
        self.rows = nb * L

    def mod_row(self, tile, tile_rows):
        if not self.latent:
            return 0
        return 1 + (tile * tile_rows) // self.L


PROMPT = _Stream(BATCH, SEQ, False)
LATENT = _Stream(DEC_BATCH, DEC_SEQ, True)


MOD_TN = 768


def _mod_kernel(cond_ref, w_ref, b_ref, o_ref):
    c = _silu(cond_ref[...])
    o_ref[0] = _dot3(c, w_ref[0]) + b_ref[0]


def _modulation(cond, w_mod, b_mod):
    n = 6 * D_MODEL
    return pl.pallas_call(
        _mod_kernel,
        grid=(DEPTH, n // MOD_TN),
        in_specs=[pl.BlockSpec((MOD_ROWS, D_MODEL), lambda l, j: (0, 0)),
                  pl.BlockSpec((1, D_MODEL, MOD_TN), lambda l, j: (l, 0, j)),
                  pl.BlockSpec((1, 1, MOD_TN), lambda l, j: (l, 0, j))],
        out_specs=pl.BlockSpec((1, MOD_ROWS, MOD_TN), lambda l, j: (l, 0, j)),
        out_shape=jax.ShapeDtypeStruct((DEPTH, MOD_ROWS, n), F32),
        compiler_params=_cparams("parallel", "parallel"),
        name="modulation",
    )(cond, w_mod, b_mod.reshape(DEPTH, 1, n))


ZP_TM = 1024
ZP_TN = 256


def _zproj_kernel(x_ref, mod_ref, w_ref, wg_ref, z_ref, zg_ref, h_scr):
    j = pl.program_id(1)

    @pl.when(j == 0)
    def _():
        m = mod_ref[0, 0]
        shift1 = m[:, 0:D_MODEL]
        scale1 = m[:, D_MODEL:2 * D_MODEL]
        h = (x_ref[...] * (1.0 + scale1) + shift1).astype(BF16)
        h_scr[...] = h
        zg_ref[...] = jnp.dot(h, wg_ref[0], preferred_element_type=F32)

    z_ref[...] = jnp.dot(h_scr[...], w_ref[0], preferred_element_type=F32)


def _zproj(st, x, mod4, w_in, w_gate_cols, l):
    return pl.pallas_call(
        _zproj_kernel,
        grid=(st.rows // ZP_TM, D_Z // ZP_TN),
        in_specs=[pl.BlockSpec((ZP_TM, D_MODEL), lambda i, j: (i, 0)),
                  pl.BlockSpec((1, 1, 1, 6 * D_MODEL), lambda i, j: (l, st.mod_row(i, ZP_TM), 0, 0)),
                  pl.BlockSpec((1, D_MODEL, ZP_TN), lambda i, j: (l, 0, j)),
                  pl.BlockSpec((1, D_MODEL, N_GATE), lambda i, j: (l, 0, 0))],
        out_specs=[pl.BlockSpec((ZP_TM, ZP_TN), lambda i, j: (i, j)),
                   pl.BlockSpec((ZP_TM, N_GATE), lambda i, j: (i, 0))],
        out_shape=[jax.ShapeDtypeStruct((st.rows, D_Z), F32),
                   jax.ShapeDtypeStruct((st.rows, N_GATE), F32)],
        scratch_shapes=[pltpu.VMEM((ZP_TM, D_MODEL), BF16)],
        compiler_params=_cparams("parallel", "arbitrary"),
        name="zproj",
    )(x, mod4, w_in, w_gate_cols)


OP_TM = 512


def _layer_norm(v, g, b):
    mu = jnp.mean(v, axis=-1, keepdims=True)
    d = v - mu
    var = jnp.mean(d * d, axis=-1, keepdims=True)
    return d * lax.rsqrt(var + LN_EPS) * g + b


def _route(scores, b_router):
    sel = scores + b_router
    s = [sel[:, e:e + 1] for e in range(N_EXPERTS)]
    in_top2 = []
    for g in range(N_EXPERT_GROUPS):
        for i in range(EXPERTS_PER_GROUP):
            e = g * EXPERTS_PER_GROUP + i
            cnt = jnp.zeros_like(s[e])
            for jj in range(EXPERTS_PER_GROUP):
                if jj == i:
                    continue
                o = g * EXPERTS_PER_GROUP + jj
                beats = (s[o] > s[e]) if jj > i else (s[o] >= s[e])
                cnt = cnt + jnp.where(beats, 1.0, 0.0)
            in_top2.append(cnt < 2.0)
    grp = []
    for g in range(N_EXPERT_GROUPS):
        tot = jnp.zeros_like(s[0])
        for i in range(EXPERTS_PER_GROUP):
            e = g * EXPERTS_PER_GROUP + i
            tot = tot + jnp.where(in_top2[e], s[e], 0.0)
        grp.append(tot)
    lane = _iota(scores.shape, 1)
    keep = jnp.zeros(scores.shape, F32)
    for g in range(N_EXPERT_GROUPS):
        cnt = jnp.zeros_like(s[0])
        for o in range(N_EXPERT_GROUPS):
            if o == g:
                continue
            beats = (grp[o] > grp[g]) if o > g else (grp[o] >= grp[g])
            cnt = cnt + jnp.where(beats, 1.0, 0.0)
        best = cnt < 1.0
        for i in range(EXPERTS_PER_GROUP):
            e = g * EXPERTS_PER_GROUP + i
            on = jnp.where(best, jnp.where(in_top2[e], 1.0, 0.0), 0.0)
            keep = keep + jnp.where(lane == e, on, 0.0)
    picked = scores * keep
    return picked / jnp.sum(picked, axis=-1, keepdims=True), keep


def _oproj_kernel(y0_ref, y1_ref, y2_ref, y3_ref, w_ref, x_ref, mod_ref, g_ref, b_ref, wr_ref, br_ref,
                  x1_ref, h2_ref, gate_ref, keep_ref):
    acc = None
    for k, yr in enumerate((y0_ref, y1_ref, y2_ref, y3_ref)):
        part = jnp.dot(yr[...], w_ref[0, k * D_GROUP:(k + 1) * D_GROUP, :], preferred_element_type=F32)
        acc = part if acc is None else acc + part
    m = mod_ref[0, 0]
    gate1 = m[:, 2 * D_MODEL:3 * D_MODEL]
    shift2 = m[:, 3 * D_MODEL:4 * D_MODEL]
    scale2 = m[:, 4 * D_MODEL:5 * D_MODEL]
    x1 = _layer_norm(DEEPNORM_ALPHA * x_ref[...] + gate1 * acc, g_ref[0], b_ref[0])
    x1_ref[...] = x1
    h2 = x1 * (1.0 + scale2) + shift2
    h2_ref[...] = h2.astype(BF16)
    scores = _sigmoid(_dot3(h2, wr_ref[...]))
    gate_ref[...], keep_ref[...] = _route(scores, br_ref[...])


def _oproj(st, ys, x, mod4, w_out_b, ln_g, ln_b, w_router, b_router, l):
    row = lambda i: (i, 0)
    return pl.pallas_call(
        _oproj_kernel,
        grid=(st.rows // OP_TM,),
        in_specs=[pl.BlockSpec((OP_TM, D_GROUP), row)] * 4 + [
            pl.BlockSpec((1, D_MODEL, D_MODEL), lambda i: (l, 0, 0)),
            pl.BlockSpec((OP_TM, D_MODEL), row),
            pl.BlockSpec((1, 1, 1, 6 * D_MODEL), lambda i: (l, st.mod_row(i, OP_TM), 0, 0)),
            pl.BlockSpec((1, 1, D_MODEL), lambda i: (l, 0, 0)),
            pl.BlockSpec((1, 1, D_MODEL), lambda i: (l, 0, 0)),
            pl.BlockSpec((D_MODEL, N_EXPERTS), lambda i: (0, 0)),
            pl.BlockSpec((1, N_EXPERTS), lambda i: (0, 0))],
        out_specs=[pl.BlockSpec((OP_TM, D_MODEL), row),
                   pl.BlockSpec((OP_TM, D_MODEL), row),
                   pl.BlockSpec((OP_TM, N_EXPERTS), row),
                   pl.BlockSpec((OP_TM, N_EXPERTS), row)],
        out_shape=[jax.ShapeDtypeStruct((st.rows, D_MODEL), F32),
                   jax.ShapeDtypeStruct((st.rows, D_MODEL), BF16),
                   jax.ShapeDtypeStruct((st.rows, N_EXPERTS), F32),
                   jax.ShapeDtypeStruct((st.rows, N_EXPERTS), F32)],
        compiler_params=_cparams("parallel"),
        name="oproj_ln1_router",
    )(*ys, w_out_b, x, mod4, ln_g.reshape(DEPTH, 1, D_MODEL), ln_b.reshape(DEPTH, 1, D_MODEL),
      w_router, b_router.reshape(1, N_EXPERTS))


LN_TM = 512


GS_BLK = 1024
GS_CAP = 320
GS_ALIGN = 2 * SUB
GS_COLS = 512
GS_STEP = 256
GS_VMEM_LIMIT = 60 * 1024 * 1024
GS_ROWS = pl.cdiv(GS_BLK + N_EXPERT_GROUPS * GS_ALIGN + GS_CAP, GS_COLS) * GS_COLS


def _moe_sorted_kernel(h_ref, gate_ref, keep_ref, wg_ref, wu_ref, wd_ref, f_ref, xs_scr, gs_scr, pt_scr, acc_scr,
                       seg_smem):
    e = pl.program_id(1)

    @pl.when(e == 0)
    def _():
        keep = keep_ref[...]
        lane = _iota(keep.shape, 1)
        member = []
        for g in range(N_EXPERT_GROUPS):
            in_g = (lane >= g * EXPERTS_PER_GROUP) & (lane < (g + 1) * EXPERTS_PER_GROUP)
            member.append(jnp.minimum(jnp.sum(jnp.where(in_g, keep, 0.0), axis=-1, keepdims=True), 1.0))
        lane4 = _iota((GS_BLK, LANE), 1)
        onehot = jnp.zeros((GS_BLK, LANE), F32)
        for g in range(N_EXPERT_GROUPS):
            onehot = onehot + jnp.where(lane4 == g, member[g], 0.0)
        counts = jnp.sum(onehot, axis=0, keepdims=True)
        starts = []
        start = jnp.zeros((1, 1), F32)
        for g in range(N_EXPERT_GROUPS):
            cnt_g = counts[:, g:g + 1]
            starts.append(start)
            seg_smem[2 * g] = jnp.sum(start).astype(jnp.int32)
            seg_smem[2 * g + 1] = jnp.sum(cnt_g).astype(jnp.int32)
            start = start + jnp.ceil(cnt_g * (1.0 / GS_ALIGN)) * GS_ALIGN
        onehot_b = onehot.astype(BF16)
        for r0 in range(0, GS_BLK, GS_STEP):
            tri = jnp.where(_iota((GS_STEP, GS_BLK), 1) <= _iota((GS_STEP, GS_BLK), 0) + r0, 1.0, 0.0).astype(BF16)
            csum = jnp.dot(tri, onehot_b, preferred_element_type=F32)
            dest = jnp.zeros((GS_STEP, 1), F32)
            for g in range(N_EXPERT_GROUPS):
                dest = dest + member[g][r0:r0 + GS_STEP] * (starts[g] + csum[:, g:g + 1] - 1.0)
            pt_scr[r0:r0 + GS_STEP, :] = jnp.where(
                _iota((GS_STEP, GS_ROWS), 1) == dest.astype(jnp.int32), 1.0, 0.0).astype(BF16)
        for c0 in range(0, GS_ROWS, GS_COLS):
            pt_c = pt_scr[:, c0:c0 + GS_COLS]
            xs_scr[c0:c0 + GS_COLS, :] = _dg(pt_c, h_ref[...], TN).astype(BF16)
            gs_scr[c0:c0 + GS_COLS, :] = _dot_sel_l(pt_c, gate_ref[...], TN)
        acc_scr[...] = jnp.zeros_like(acc_scr)

    g = e // EXPERTS_PER_GROUP
    seg_start = seg_smem[2 * g]
    seg_len = seg_smem[2 * g + 1]
    wg = wg_ref[0, 0]
    wu = wu_ref[0, 0]
    wd = wd_ref[0, 0]

    def window(c, carry):
        rows = pl.ds(pl.multiple_of(seg_start + c * GS_CAP, GS_ALIGN), GS_CAP)
        x = xs_scr[rows, :]
        a = jnp.dot(x, wg, preferred_element_type=F32)
        u = jnp.dot(x, wu, preferred_element_type=F32)
        gates = gs_scr[rows, :]
        ge = jnp.sum(jnp.where(_iota(gates.shape, 1) == e, gates, 0.0), axis=-1, keepdims=True)
        hid = (_silu(a) * u * ge).astype(BF16)
        acc_scr[rows, :] += jnp.dot(hid, wd, preferred_element_type=F32)
        return carry

    lax.fori_loop(0, (seg_len + GS_CAP - 1) // GS_CAP, window, 0)

    @pl.when(e == N_EXPERTS - 1)
    def _():
        pt = pt_scr[...]
        for c0 in range(0, D_MODEL, GS_COLS):
            acc_b = acc_scr[:, c0:c0 + GS_COLS].astype(BF16)
            f_ref[:, c0:c0 + GS_COLS] = jnp.dot(pt, acc_b, preferred_element_type=F32).astype(BF16)


def _moe_sorted(st, h2, gates, keep, wg_b, wu_b, wd_b, l):
    row = lambda i, e: (i, 0)
    wspec = lambda a, b: pl.BlockSpec((1, 1, a, b), lambda i, e: (l, e, 0, 0))
    return pl.pallas_call(
        _moe_sorted_kernel,
        grid=(st.rows // GS_BLK, N_EXPERTS),
        in_specs=[pl.BlockSpec((GS_BLK, D_MODEL), row, pipeline_mode=pl.Buffered(1)),
                  pl.BlockSpec((GS_BLK, N_EXPERTS), row), pl.BlockSpec((GS_BLK, N_EXPERTS), row),
                  wspec(D_MODEL, D_EXPERT), wspec(D_MODEL, D_EXPERT), wspec(D_EXPERT, D_MODEL)],
        out_specs=pl.BlockSpec((GS_BLK, D_MODEL), row),
        out_shape=jax.ShapeDtypeStruct((st.rows, D_MODEL), BF16),
        scratch_shapes=[pltpu.VMEM((GS_ROWS, D_MODEL), BF16), pltpu.VMEM((GS_ROWS, N_EXPERTS), F32),
                        pltpu.VMEM((GS_BLK, GS_ROWS), BF16), pltpu.VMEM((GS_ROWS, D_MODEL), F32),
                        pltpu.SMEM((2 * N_EXPERT_GROUPS,), jnp.int32)],
        compiler_params=pltpu.CompilerParams(dimension_semantics=("parallel", "arbitrary"),
                                             vmem_limit_bytes=GS_VMEM_LIMIT),
        name="moe_group_sorted",
    )(h2, gates, keep, wg_b, wu_b, wd_b)


def _ln2f_kernel(x_ref, f_ref, mod_ref, g_ref, b_ref, o_ref):
    gate2 = mod_ref[0, 0][:, 5 * D_MODEL:6 * D_MODEL]
    o_ref[...] = _layer_norm(DEEPNORM_ALPHA * x_ref[...] + gate2 * f_ref[...].astype(F32), g_ref[0], b_ref[0])


def _ln2f(st, x1, f, mod4, ln_g, ln_b, l):
    row = lambda i: (i, 0)
    return pl.pallas_call(
        _ln2f_kernel,
        grid=(st.rows // LN_TM,),
        in_specs=[pl.BlockSpec((LN_TM, D_MODEL), row),
                  pl.BlockSpec((LN_TM, D_MODEL), row),
                  pl.BlockSpec((1, 1, 1, 6 * D_MODEL), lambda i: (l, st.mod_row(i, LN_TM), 0, 0)),
                  pl.BlockSpec((1, 1, D_MODEL), lambda i: (l, 0, 0)),
                  pl.BlockSpec((1, 1, D_MODEL), lambda i: (l, 0, 0))],
        out_specs=pl.BlockSpec((LN_TM, D_MODEL), row),
        out_shape=jax.ShapeDtypeStruct((st.rows, D_MODEL), F32),
        compiler_params=_cparams("parallel"),
        name="ln2",
    )(x1, f, mod4, ln_g.reshape(DEPTH, 1, D_MODEL), ln_b.reshape(DEPTH, 1, D_MODEL))


S5_TC = 128
S5_SEQS = SUB
S5_Q = 4
S5_QS = S5_GROUPS // S5_Q * S5_STATE
N_S5 = S5_GROUPS * S5_STATE


def _s5_prep_kernel(lr_ref, li_ref, ls_ref, br_ref, bi_ref, e_ref, lbr_ref, lbi_ref, bbr_ref, bbi_ref):
    lr = lr_ref[0]
    li = li_ref[0]
    dt = jnp.exp(ls_ref[0])
    mag = jnp.exp(lr * dt)
    ang = li * dt
    ar = mag * jnp.cos(ang)
    ai = mag * jnp.sin(ang)
    lbr_ref[0] = ar
    lbi_ref[0] = ai
    den = lr * lr + li * li
    nr = ar - 1.0
    cr = (nr * lr + ai * li) / den
    ci = (ai * lr - nr * li) / den
    cr = _dot_sel_l(e_ref[...], cr)
    ci = _dot_sel_l(e_ref[...], ci)
    bre = br_ref[0]
    bim = bi_ref[0]
    bbr_ref[0] = cr * bre - ci * bim
    bbi_ref[0] = cr * bim + ci * bre


def _s5_prep(lam_re, lam_im, log_step, b_re, b_im):
    d2 = DEPTH * 2
    g, p, h = S5_GROUPS, S5_STATE, S5_CH
    bt_re = jnp.swapaxes(b_re, -1, -2).reshape(d2, g * h, p)
    bt_im = jnp.swapaxes(b_im, -1, -2).reshape(d2, g * h, p)
    expand = jnp.asarray(np.kron(np.eye(g, dtype=np.float32), np.ones((h, 1), np.float32)))
    spec_gp = pl.BlockSpec((1, g, p), lambda i: (i, 0, 0))
    spec_b = pl.BlockSpec((1, g * h, p), lambda i: (i, 0, 0))
    lbr, lbi, bbr, bbi = pl.pallas_call(
        _s5_prep_kernel,
        grid=(d2,),
        in_specs=[spec_gp, spec_gp, pl.BlockSpec((1, g, 1), lambda i: (i, 0, 0)), spec_b, spec_b,
                  pl.BlockSpec((g * h, g), lambda i: (0, 0))],
        out_specs=[spec_gp, spec_gp, spec_b, spec_b],
        out_shape=[jax.ShapeDtypeStruct((d2, g, p), F32)] * 2 + [jax.ShapeDtypeStruct((d2, g * h, p), F32)] * 2,
        compiler_params=_cparams("parallel"),
        name="s5_prep",
    )(lam_re.reshape(d2, g, p), lam_im.reshape(d2, g, p), log_step.reshape(d2, g, 1), bt_re, bt_im, expand)
    return lbr, lbi, bbr, bbi


def _s5_block_params(lbr, lbi, bbr, bbi, c_re, c_im):
    d2 = DEPTH * 2
    gq = S5_GROUPS // S5_Q
    eye = jnp.eye(gq, dtype=F32)

    def b_blocks(b):
        b = b.reshape(d2, S5_Q, gq, S5_CH, S5_STATE)
        return jnp.einsum('djghp,gk->djghkp', b, eye).reshape(d2, S5_Q, gq * S5_CH, gq * S5_STATE)

    def c_blocks(c):
        c = c.reshape(d2, S5_Q, gq, S5_CH, S5_STATE)
        return jnp.einsum('djghp,gk->djgpkh', c, eye).reshape(d2, S5_Q, gq * S5_STATE, gq * S5_CH)

    bq = jnp.concatenate([b_blocks(bbr), b_blocks(bbi)], axis=-1)
    cq = jnp.concatenate([c_blocks(c_re), -c_blocks(c_im)], axis=-2)
    return bq, cq, lbr.reshape(d2, 1, N_S5), lbi.reshape(d2, 1, N_S5)


def _s5_scan_kernel(u_ref, bq_ref, cq_ref, lr_ref, li_ref, s0r_ref, s0i_ref, y_ref, sfr_ref, sfi_ref,
                    utb, bur, bui, ytb, sr_scr, si_scr):
    d = pl.program_id(1)
    c = pl.program_id(2)
    nrow = S5_TC * S5_SEQS
    cw = D_GROUP // S5_Q

    @pl.when(c == 0)
    def _():
        sr_scr[...] = s0r_ref[0, 0]
        si_scr[...] = s0i_ref[0, 0]

    for b in range(S5_SEQS):
        for j in range(S5_Q):
            utb[j, pl.ds(b, S5_TC, stride=S5_SEQS), :] = u_ref[b, :, j * cw:(j + 1) * cw]
    for j in range(S5_Q):
        bu = _dot1(utb[j], bq_ref[0, j])
        bur[:, j * S5_QS:(j + 1) * S5_QS] = bu[:, :S5_QS]
        bui[:, j * S5_QS:(j + 1) * S5_QS] = bu[:, S5_QS:]
    for j in range(S5_Q):
        sl = slice(j * S5_QS, (j + 1) * S5_QS)
        lam_r = jnp.broadcast_to(lr_ref[0][:, sl], (S5_SEQS, S5_QS))
        lam_i = jnp.broadcast_to(li_ref[0][:, sl], (S5_SEQS, S5_QS))

        def step(t, carry, sl=sl, lam_r=lam_r, lam_i=lam_i):
            sr, si = carry
            te = jnp.where(d == 0, t, S5_TC - 1 - t)
            rows = pl.ds(pl.multiple_of(te * S5_SEQS, S5_SEQS), S5_SEQS)
            nr = lam_r * sr - lam_i * si + bur[rows, sl]
            ni = lam_r * si + lam_i * sr + bui[rows, sl]
            bur[rows, sl] = nr
            bui[rows, sl] = ni
            return nr, ni

        sr, si = lax.fori_loop(0, S5_TC, step, (sr_scr[:, sl], si_scr[:, sl]))
        sr_scr[:, sl] = sr
        si_scr[:, sl] = si
    for j in range(S5_Q):
        sl = slice(j * S5_QS, (j + 1) * S5_QS)
        yj = _dot1(bur[:, sl], cq_ref[0, j, :S5_QS, :]) + _dot1(bui[:, sl], cq_ref[0, j, S5_QS:, :])
        ytb[j] = yj
    for b in range(S5_SEQS):
        for j in range(S5_Q):
            y_ref[0, b, :, j * cw:(j + 1) * cw] = ytb[j, pl.ds(b, S5_TC, stride=S5_SEQS), :]

    @pl.when(c == pl.num_programs(2) - 1)
    def _():
        sfr_ref[0, 0] = sr_scr[...]
        sfi_ref[0, 0] = si_scr[...]


def _s5_scan(st, z3, bq, cq, lamr, lami, s0r, s0i, l):
    ng, nc = st.nb // S5_SEQS, st.L // S5_TC
    nrow = S5_TC * S5_SEQS
    chunk = lambda d, c: c + d * (nc - 1 - 2 * c)
    par = lambda g, d, c: (2 * l + d, 0, 0, 0)
    st_spec = pl.BlockSpec((1, 1, S5_SEQS, N_S5), lambda g, d, c: (g, d, 0, 0))
    return pl.pallas_call(
        _s5_scan_kernel,
        grid=(ng, 2, nc),
        in_specs=[pl.BlockSpec((S5_SEQS, S5_TC, D_GROUP), lambda g, d, c: (g, chunk(d, c), 0)),
                  pl.BlockSpec((1, S5_Q, D_GROUP // S5_Q, 2 * S5_QS), par),
                  pl.BlockSpec((1, S5_Q, 2 * S5_QS, D_GROUP // S5_Q), par),
                  pl.BlockSpec((1, 1, N_S5), lambda g, d, c: (2 * l + d, 0, 0)),
                  pl.BlockSpec((1, 1, N_S5), lambda g, d, c: (2 * l + d, 0, 0)),
                  st_spec, st_spec],
        out_specs=[pl.BlockSpec((1, S5_SEQS, S5_TC, D_GROUP), lambda g, d, c: (d, g, chunk(d, c), 0)),
                   st_spec, st_spec],
        out_shape=[jax.ShapeDtypeStruct((2, st.nb, st.L, D_GROUP), F32),
                   jax.ShapeDtypeStruct((ng, 2, S5_SEQS, N_S5), F32),
                   jax.ShapeDtypeStruct((ng, 2, S5_SEQS, N_S5), F32)],
        scratch_shapes=[pltpu.VMEM((S5_Q, nrow, LANE), F32), pltpu.VMEM((nrow, N_S5), F32),
                        pltpu.VMEM((nrow, N_S5), F32), pltpu.VMEM((S5_Q, nrow, LANE), F32),
                        pltpu.VMEM((S5_SEQS, N_S5), F32), pltpu.VMEM((S5_SEQS, N_S5), F32)],
        compiler_params=_cparams("parallel", "arbitrary", "arbitrary"),
        name="s5_scan",
    )(z3, bq, cq, lamr, lami, s0r, s0i)


S5_TM = 512


def _gelu_tanh(x):
    return 0.5 * x * (1.0 + jnp.tanh(math.sqrt(2.0 / math.pi) * (x + 0.044715 * (x * x * x))))


def _s5_out_kernel(u_ref, yf_ref, yb_ref, d_ref, w_ref, o_ref):
    y = yf_ref[0] + yb_ref[0] + d_ref[0] * u_ref[...]
    y = _gelu_tanh(y)
    o_ref[...] = (y * _sigmoid(_dot1(y, w_ref[0]))).astype(BF16)


def _s5_out(st, z, ydir, d_skip, w_glu, l):
    return pl.pallas_call(
        _s5_out_kernel,
        grid=(st.rows // S5_TM,),
        in_specs=[pl.BlockSpec((S5_TM, D_GROUP), lambda i: (i, 0)),
                  pl.BlockSpec((1, S5_TM, D_GROUP), lambda i: (0, i, 0)),
                  pl.BlockSpec((1, S5_TM, D_GROUP), lambda i: (1, i, 0)),
                  pl.BlockSpec((1, 1, D_GROUP), lambda i: (l, 0, 0)),
                  pl.BlockSpec((1, D_GROUP, D_GROUP), lambda i: (l, 0, 0))],
        out_specs=pl.BlockSpec((S5_TM, D_GROUP), lambda i: (i, 0)),
        out_shape=jax.ShapeDtypeStruct((st.rows, D_GROUP), BF16),
        compiler_params=_cparams("parallel"),
        name="s5_out",
    )(z, ydir, ydir, d_skip.reshape(DEPTH, 1, D_GROUP), w_glu)


NA_SCALE = NA_HEAD ** -0.5
NA_ROWS = DEC_SEQ // GRID_W
NA_KH = min(NA_WIN_H, NA_ROWS)
NA_QCOL = NA_OFF // LANE
NA_KCOL = (NA_OFF + D_GROUP) // LANE
NA_VCOL = (NA_OFF + 2 * D_GROUP) // LANE
NA_NDR = 2 * NA_WIN_H - 1
NA_AHEAD = 1


def _na_ctx_kernel(q_ref, k_ref, v_ref, o_ref, nk_ref, nv_ref):
    for hh in range(2):
        sl = slice(hh * NA_HEAD, (hh + 1) * NA_HEAD)
        q = q_ref[0, :, sl]
        k = k_ref[0, :, sl]
        v = v_ref[0, :, sl]
        nk_ref[0, hh] = k
        nv_ref[0, hh] = v
        s = _dot1(q, k, NT) * NA_SCALE
        e = jnp.exp(s - jnp.max(s, axis=-1, keepdims=True))
        o = _dot1(e, v) / jnp.sum(e, axis=-1, keepdims=True)
        o_ref[0, :, sl] = o.astype(BF16)


def _na_ctx(st, z3):
    blk = lambda col: pl.BlockSpec((1, st.L, LANE), lambda b, p: (b, 0, col + p))
    kv_spec = pl.BlockSpec((1, 2, st.L, NA_HEAD), lambda b, p: (b, p, 0, 0))
    kv_shape = jax.ShapeDtypeStruct((st.nb, NA_HEADS, st.L, NA_HEAD), F32)
    return pl.pallas_call(
        _na_ctx_kernel,
        grid=(st.nb, NA_HEADS // 2),
        in_specs=[blk(NA_QCOL), blk(NA_KCOL), blk(NA_VCOL)],
        out_specs=[pl.BlockSpec((1, st.L, LANE), lambda b, p: (b, 0, p)), kv_spec, kv_spec],
        out_shape=[jax.ShapeDtypeStruct((st.nb, st.L, D_GROUP), BF16), kv_shape, kv_shape],
        compiler_params=_cparams("parallel", "parallel"),
        name="na_context",
    )(z3, z3, z3)


def _na_nbr_kernel(q_ref, k_ref, v_ref, ck_ref, cv_ref, bias_ref, o_ref):
    nloc = NA_KH * GRID_W
    qc = _iota((GRID_W, nloc), 0)
    kc = _iota((GRID_W, nloc), 1) % GRID_W
    cs = jnp.clip(qc - NA_WIN_W // 2, 0, GRID_W - NA_WIN_W)
    col_in = (kc >= cs) & (kc < cs + NA_WIN_W)

    def scores(hh, r):
        sl = slice(hh * NA_HEAD, (hh + 1) * NA_HEAD)
        rs = min(max(r - NA_KH // 2, 0), NA_ROWS - NA_KH)
        q = q_ref[0, r * GRID_W:(r + 1) * GRID_W, sl]
        k = k_ref[0, rs * GRID_W:rs * GRID_W + nloc, sl]
        off = (rs - r + NA_WIN_H - 1) * GRID_W
        s_loc = _dot1(q, k, NT) * NA_SCALE + bias_ref[hh, :, off:off + nloc]
        s_loc = jnp.where(col_in, s_loc, NEG_INF)
        s_ctx = _dot1(q, ck_ref[0, 0, hh], NT) * NA_SCALE
        return s_loc, s_ctx

    def finish(hh, r, s_loc, s_ctx):
        sl = slice(hh * NA_HEAD, (hh + 1) * NA_HEAD)
        rs = min(max(r - NA_KH // 2, 0), NA_ROWS - NA_KH)
        v = v_ref[0, rs * GRID_W:rs * GRID_W + nloc, sl]
        m = jnp.maximum(jnp.max(s_loc, axis=-1, keepdims=True), jnp.max(s_ctx, axis=-1, keepdims=True))
        e_loc = jnp.exp(s_loc - m)
        e_ctx = jnp.exp(s_ctx - m)
        den = jnp.sum(e_loc, axis=-1, keepdims=True) + jnp.sum(e_ctx, axis=-1, keepdims=True)
        o = (_dot1(e_loc, v) + _dot1(e_ctx, cv_ref[0, 0, hh])) / den
        o_ref[0, r * GRID_W:(r + 1) * GRID_W, sl] = o.astype(BF16)

    blocks = [(hh, r) for hh in range(2) for r in range(NA_ROWS)]
    pending = []
    for i, blk in enumerate(blocks):
        pending.append(scores(*blk))
        if i >= NA_AHEAD:
            finish(*blocks[i - NA_AHEAD], *pending.pop(0))
    for j in range(len(blocks) - NA_AHEAD, len(blocks)):
        finish(*blocks[j], *pending.pop(0))


def _na_bias_table(rpb):
    qc = np.arange(GRID_W)[:, None]
    kc = np.arange(GRID_W)[None, :]
    dc = np.clip(kc - qc, -(NA_WIN_W - 1), NA_WIN_W - 1) + (NA_WIN_W - 1)
    t = rpb[:, :, :, dc]
    return jnp.transpose(t, (0, 1, 3, 2, 4)).reshape(DEPTH, NA_HEADS, GRID_W, NA_NDR * GRID_W)


def _na_nbr(st, z3, cache_k, cache_v, bias_tab, l):
    blk = lambda col: pl.BlockSpec((1, st.L, LANE), lambda b, p: (b, 0, col + p))
    cspec = pl.BlockSpec((1, 1, 2, PAST_LEN, NA_HEAD), lambda b, p: (b, l, p, 0, 0))
    return pl.pallas_call(
        _na_nbr_kernel,
        grid=(st.nb, NA_HEADS // 2),
        in_specs=[blk(NA_QCOL), blk(NA_KCOL), blk(NA_VCOL), cspec, cspec,
                  pl.BlockSpec((None, 2, GRID_W, NA_NDR * GRID_W), lambda b, p: (l, p, 0, 0))],
        out_specs=pl.BlockSpec((1, st.L, LANE), lambda b, p: (b, 0, p)),
        out_shape=jax.ShapeDtypeStruct((st.nb, st.L, D_GROUP), BF16),
        compiler_params=_cparams("parallel", "parallel"),
        name="na_neighbourhood",
    )(z3, z3, z3, cache_k, cache_v, bias_tab)


ML_COL = ML_OFF // LANE
ML_SCALE = ML_HEAD ** -0.5


def _log_sigmoid(x):
    return -_softplus(-x)


def _rope_tables(L):
    half = ML_HEAD // 2
    quarter = half // 2
    t = np.arange(L)
    inv_freq = ROPE_BASE ** (-np.arange(quarter, dtype=np.float32) / quarter)

    def tabs(pos):
        ang = pos.astype(np.float32)[:, None] * inv_freq[None, :].astype(np.float32)
        c, s = np.cos(ang), np.sin(ang)
        return np.concatenate([c, c], axis=-1), np.concatenate([-s, s], axis=-1)

    c1, s1 = tabs(t // GRID_W)
    c2, s2 = tabs(t % GRID_W)
    return (jnp.asarray(np.concatenate([c1, c2], axis=-1), F32),
            jnp.asarray(np.concatenate([s1, s2], axis=-1), F32))


def _ml_kernel(*refs, L, rotary, layer):
    nh = ML_HEADS
    ib_ref, fb_ref = refs[0], refs[1]
    q_refs, k_refs, v_refs, o_refs = (refs[2 + i * nh:2 + (i + 1) * nh] for i in range(4))
    (g_ref, cos_ref, sin_ref, c0_ref, n0_ref, m0_ref, lng_ref, lnb_ref,
     y_ref, cf_ref, nf_ref, mf_ref, qs, ks, hf, hb, c_scr, n_scr, m_scr) = refs[2 + 4 * nh:]
    T = ML_CHUNK
    nc = L // T

    if rotary:
        first = (_iota((L, ML_HEAD), 1) % (ML_HEAD // 2)) < ML_HEAD // 4

        def rope(x):
            quarter = ML_HEAD // 4
            partner = jnp.where(first, pltpu.roll(x, ML_HEAD - quarter, axis=1), pltpu.roll(x, quarter, axis=1))
            return x * cos_ref[...] + partner * sin_ref[...]
    else:
        rope = lambda x: x
    for h in range(nh):
        qs[h] = rope(q_refs[h][0]) * ML_SCALE
        ks[h] = rope(k_refs[h][0])
        for d in range(2):
            c_scr[d, h] = c0_ref[0, d, h]
            n_scr[d, h] = n0_ref[0, h, d:d + 1, :]
            m_scr[d, h] = m0_ref[0, h, d:d + 1, :]

    tj = _iota((T, T), 0)
    ts = _iota((T, T), 1)
    ones = jnp.ones((T, T), F32)
    upto_row = (tj <= ts, tj >= ts)
    upto_col = (ts <= tj, ts >= tj)

    def body(ci, carry):
        ch = []
        for d in range(2):
            cd = ci if d == 0 else nc - 1 - ci
            rows = pl.ds(pl.multiple_of(cd * T, T), T)
            for h in range(nh):
                g = g_ref[0, h, cd]
                ig = g[:, d:d + 1] + ib_ref[layer * 2 * nh + d * nh + h]
                fg = g[:, 2 + d:3 + d] + fb_ref[layer * 2 * nh + d * nh + h]
                lf = jnp.broadcast_to(_log_sigmoid(fg), (T, T))
                igb = jnp.broadcast_to(ig, (T, T))
                q = dict(d=d, h=h, rows=rows, ig=ig, qc=qs[h, rows, :], kc=ks[h, rows, :], vc=v_refs[h][0, rows, :])
                q['bcol'] = _dot_sel_l(jnp.where(upto_col[d], 1.0, 0.0), lf)
                q['brow_i'] = _dot_sel_l(ones, jnp.where(upto_row[d], lf, 0.0) - jnp.where(tj == ts, igb, 0.0))
                ch.append(q)
        for q in ch:
            q['qk'] = _dot1(q['qc'], q['kc'], NT)
            q['c_old'] = c_scr[q['d'], q['h']]
            q['qc_c'] = _dot1(q['qc'], q['c_old'])
        for q in ch:
            d, h = q['d'], q['h']
            dmat = jnp.where(upto_col[d], q['bcol'] - q['brow_i'], -jnp.inf)
            b1 = q['bcol'][:, 0:1]
            m_prev = m_scr[d, h]
            inter = b1 + m_prev
            m_t = jnp.maximum(inter, jnp.max(dmat, axis=-1, keepdims=True))
            qk = q['qk'] * jnp.exp(dmat - m_t)
            w_inter = jnp.exp(inter - m_t)
            n_old = n_scr[d, h]
            den = w_inter * jnp.sum(q['qc'] * n_old, axis=-1, keepdims=True) + jnp.sum(qk, axis=-1, keepdims=True)
            q['scale'] = 1.0 / jnp.maximum(jnp.abs(den), jnp.exp(-m_t))
            q['inter_part'] = w_inter * q['qc_c']
            q['intra'] = _dot1(qk, q['vc'])
            b_last = b1[T - 1:T, :] if d == 0 else b1[0:1, :]
            g_s = b_last - b1 + q['ig']
            m_new = jnp.maximum(b_last + m_prev, jnp.max(g_s, axis=0, keepdims=True))
            w_old = jnp.exp(b_last + m_prev - m_new)
            w_s = jnp.exp(g_s - m_new)
            q['c_new'] = _dot1(q['kc'], w_s * q['vc'], TN)
            q['w_old'] = w_old
            n_scr[d, h] = w_old * n_old + jnp.sum(w_s * q['kc'], axis=0, keepdims=True)
            m_scr[d, h] = m_new
        for q in ch:
            d, h = q['d'], q['h']
            hcur = (q['inter_part'] + q['intra']) * q['scale']
            if d == 0:
                hf[h, q['rows'], :] = hcur
            else:
                hb[h, q['rows'], :] = hcur
            c_scr[d, h] = q['w_old'] * q['c_old'] + q['c_new']
        return carry

    lax.fori_loop(0, nc, body, 0)

    for h in range(nh):
        hsum = hf[h] + hb[h]
        mu = jnp.mean(hsum, axis=-1, keepdims=True)
        dv = hsum - mu
        var = jnp.mean(dv * dv, axis=-1, keepdims=True)
        hn = dv * lax.rsqrt(var + ML_GN_EPS)
        y_ref[0, :, h * ML_HEAD:(h + 1) * ML_HEAD] = (
            _sigmoid(o_refs[h][0]) * (hn * lng_ref[0, :, h * ML_HEAD:(h + 1) * ML_HEAD]
                                      + lnb_ref[0, :, h * ML_HEAD:(h + 1) * ML_HEAD])).astype(BF16)
        for d in range(2):
            cf_ref[0, d, h] = c_scr[d, h]
            nf_ref[0, h, d:d + 1, :] = n_scr[d, h]
            mf_ref[0, h, d:d + 1, :] = m_scr[d, h]


def _ml_mixer(st, z3, zg, i_bias, f_bias, ln_g, ln_b, c0, n0, m0, l):
    L, nb, nh = st.L, st.nb, ML_HEADS
    nc = L // ML_CHUNK
    gates = zg.reshape(nb, L, 2, 2, nh).transpose(0, 4, 1, 2, 3).reshape(nb, nh, nc, ML_CHUNK, 4)
    cos_t, sin_t = _rope_tables(L)
    blk = lambda col: pl.BlockSpec((1, L, LANE), lambda b: (b, 0, col))
    head_blocks = [blk(ML_COL + part * nh + h) for part in range(4) for h in range(nh)]
    smem = pl.BlockSpec(memory_space=pltpu.SMEM)
    tab = pl.BlockSpec((L, ML_HEAD), lambda b: (0, 0))
    c_spec = pl.BlockSpec((1, 2, nh, ML_HEAD, ML_HEAD), lambda b: (b, 0, 0, 0, 0))
    n_spec = pl.BlockSpec((1, nh, 2, ML_HEAD), lambda b: (b, 0, 0, 0))
    m_spec = pl.BlockSpec((1, nh, 2, 1), lambda b: (b, 0, 0, 0))
    par = pl.BlockSpec((1, 1, D_GROUP), lambda b: (l, 0, 0))
    return pl.pallas_call(
        functools.partial(_ml_kernel, L=L, rotary=st.latent, layer=l),
        grid=(nb,),
        in_specs=[smem, smem] + head_blocks + [
            pl.BlockSpec((1, nh, nc, ML_CHUNK, 4), lambda b: (b, 0, 0, 0, 0)),
            tab, tab, c_spec, n_spec, m_spec, par, par],
        out_specs=[pl.BlockSpec((1, L, D_GROUP), lambda b: (b, 0, 0)), c_spec, n_spec, m_spec],
        out_shape=[jax.ShapeDtypeStruct((nb, L, D_GROUP), BF16),
                   jax.ShapeDtypeStruct((nb, 2, nh, ML_HEAD, ML_HEAD), F32),
                   jax.ShapeDtypeStruct((nb, nh, 2, ML_HEAD), F32),
                   jax.ShapeDtypeStruct((nb, nh, 2, 1), F32)],
        scratch_shapes=[pltpu.VMEM((nh, L, ML_HEAD), F32), pltpu.VMEM((nh, L, ML_HEAD), F32),
                        pltpu.VMEM((nh, L, ML_HEAD), F32), pltpu.VMEM((nh, L, ML_HEAD), F32),
                        pltpu.VMEM((2, nh, ML_HEAD, ML_HEAD), F32), pltpu.VMEM((2, nh, 1, ML_HEAD), F32),
                        pltpu.VMEM((2, nh, 1, 1), F32)],
        compiler_params=_cparams("parallel"),
        name="mlstm",
    )(i_bias.reshape(-1), f_bias.reshape(-1), *([z3] * (4 * nh)), gates, cos_t, sin_t, c0, n0, m0,
      ln_g.reshape(DEPTH, 1, D_GROUP), ln_b.reshape(DEPTH, 1, D_GROUP))


RW_TL = 256
RW_T = 64
RW_GROUP = 16
RW_RCOL = RW_OFF // D_GROUP
RW_LCOL = (RW_OFF + 3 * D_GROUP) // (2 * LANE)
N_LORA = RW_LORA_W + RW_LORA_A + RW_LORA_G


def _head_ones():
    return jnp.asarray(np.kron(np.eye(RW_HEADS, dtype=np.float32), np.ones((RW_HEAD, RW_HEAD), np.float32)))


def _rw_front_kernel(r_ref, k_ref, v_ref, lo_ref, rp_ref, kp_ref, vp_ref, lp_ref, rn_ref, kn_ref, vn_ref, ln_ref,
                     mup_ref, mun_ref, w0_ref, w2_ref, a0_ref, a2_ref, g2_ref, kk_ref, ka_ref, hones_ref,
                     ro_ref, vo_ref, kko_ref, go_ref, kd_ref, bd_ref, lw_ref):
    t = pl.program_id(1)
    first = t == 0
    last = t == pl.num_programs(1) - 1
    row = _iota((RW_TL, 1), 0)

    def shift(x_ref, p_ref, n_ref, lo, hi):
        x = x_ref[0]
        prev_edge = jnp.where(first, 0.0, p_ref[0, SUB - 1:SUB, :])
        next_edge = jnp.where(last, 0.0, n_ref[0, 0:1, :])
        prev = jnp.where(row == 0, prev_edge, pltpu.roll(x, 1, axis=0))
        nxt = jnp.where(row == RW_TL - 1, next_edge, pltpu.roll(x, RW_TL - 1, axis=0))
        return x + mup_ref[0][:, lo:hi] * (prev - x) + mun_ref[0][:, lo:hi] * (nxt - x)

    g = D_GROUP
    r = shift(r_ref, rp_ref, rn_ref, 0, g)
    k = shift(k_ref, kp_ref, kn_ref, g, 2 * g)
    v = shift(v_ref, vp_ref, vn_ref, 2 * g, 3 * g)
    lo = shift(lo_ref, lp_ref, ln_ref, 3 * g, 3 * g + N_LORA)
    zw = lo[:, :RW_LORA_W]
    za = lo[:, RW_LORA_W:RW_LORA_W + RW_LORA_A]
    zg = lo[:, RW_LORA_W + RW_LORA_A:]
    ro_ref[0] = r
    vo_ref[0] = v
    kk = k * kk_ref[0]
    ssq = _dot_sel_r(kk * kk, hones_ref[...])
    kk = kk * lax.rsqrt(ssq + 1e-12)
    kko_ref[0] = kk
    go_ref[0] = _dot3(_sigmoid(zg), g2_ref[0])
    tw = jnp.tanh(zw)
    for d in range(2):
        w_log = -_softplus(-(w0_ref[0, d:d + 1, :] + _dot3(tw, w2_ref[0, d]))) - 0.5
        lw_ref[d, 0] = -jnp.exp(w_log)
        a = _sigmoid(a0_ref[0, d:d + 1, :] + _dot3(za, a2_ref[0, d]))
        kd_ref[d, 0] = k * (1.0 + (a - 1.0) * ka_ref[0])
        bd_ref[d, 0] = kk * a


def _rw_front(st, z3, mu_prev, mu_next, w0, w2, a0, a2, g2, k_k, k_a, l):
    nb, L = st.nb, st.L
    nt = L // RW_TL
    tpb = RW_TL // SUB
    main = lambda w, col: pl.BlockSpec((1, RW_TL, w), lambda b, t: (b, t, col))
    prev = lambda w, col: pl.BlockSpec((1, SUB, w), lambda b, t: (b, jnp.maximum(t * tpb - 1, 0), col))
    nxt = lambda w, col: pl.BlockSpec((1, SUB, w), lambda b, t: (b, jnp.minimum((t + 1) * tpb, L // SUB - 1), col))
    cols = [(D_GROUP, RW_RCOL), (D_GROUP, RW_RCOL + 1), (D_GROUP, RW_RCOL + 2), (2 * LANE, RW_LCOL)]
    lay = lambda *shape: pl.BlockSpec((1,) + shape, lambda b, t: (l,) + (0,) * len(shape))
    out1 = pl.BlockSpec((1, RW_TL, D_GROUP), lambda b, t: (b, t, 0))
    out2 = pl.BlockSpec((2, 1, RW_TL, D_GROUP), lambda b, t: (0, b, t, 0))
    s1 = jax.ShapeDtypeStruct((nb, L, D_GROUP), F32)
    s2 = jax.ShapeDtypeStruct((2, nb, L, D_GROUP), F32)
    return pl.pallas_call(
        _rw_front_kernel,
        grid=(nb, nt),
        in_specs=[main(*c) for c in cols] + [prev(*c) for c in cols] + [nxt(*c) for c in cols] + [
            lay(1, RW_IN), lay(1, RW_IN), lay(2, D_GROUP), lay(2, RW_LORA_W, D_GROUP), lay(2, D_GROUP),
            lay(2, RW_LORA_A, D_GROUP), lay(RW_LORA_G, D_GROUP), lay(1, D_GROUP), lay(1, D_GROUP),
            pl.BlockSpec((D_GROUP, D_GROUP), lambda b, t: (0, 0))],
        out_specs=[out1, out1, out1, out1, out2, out2, out2],
        out_shape=[s1, s1, s1, s1, s2, s2, s2],
        compiler_params=_cparams("parallel", "parallel"),
        name="rwkv_front",
    )(*([z3] * 12), mu_prev.reshape(DEPTH, 1, RW_IN), mu_next.reshape(DEPTH, 1, RW_IN), w0, w2, a0, a2, g2,
      k_k.reshape(DEPTH, 1, D_GROUP), k_a.reshape(DEPTH, 1, D_GROUP), _head_ones())


def _rw_core_kernel(rf_ref, vf_ref, kkf_ref, kdf_ref, bdf_ref, lwf_ref,
                    rb_ref, vb_ref, kkb_ref, kdb_ref, bdb_ref, lwb_ref, s0_ref,
                    yf_ref, yb_ref, sf_ref, s_scr):
    c = pl.program_id(1)
    T = RW_T

    @pl.when(c == 0)
    def _():
        s_scr[...] = s0_ref[0]

    tj = _iota((T, T), 0)
    ts = _iota((T, T), 1)
    dirs = ((rf_ref, vf_ref, kkf_ref, kdf_ref, bdf_ref, lwf_ref, yf_ref),
            (rb_ref, vb_ref, kkb_ref, kdb_ref, bdb_ref, lwb_ref, yb_ref))
    ch = []
    for d, (r_ref, v_ref, kk_ref, kd_ref, bd_ref, lw_ref, y_ref) in enumerate(dirs):
        if d == 0:
            incl = ts <= tj
            strict = ts < tj
            last = T - 1
        else:
            incl = ts >= tj
            strict = ts > tj
            last = 0
        lw = lw_ref[0, 0]
        cum = _dot_sel_l(jnp.where(incl, 1.0, 0.0), lw)
        w_in = jnp.exp(cum)
        w_inv = jnp.exp(-cum)
        w_ex = jnp.exp(cum - lw)
        kap_a = kk_ref[0] * w_ex
        bet_a = bd_ref[0, 0] * w_inv
        khat_a = kd_ref[0, 0] * w_inv
        rho_a = r_ref[0] * w_in
        w_tot = w_in[last:last + 1, :]
        v_a = v_ref[0]
        for h in range(RW_HEADS):
            sl = slice(h * RW_HEAD, (h + 1) * RW_HEAD)
            ch.append(dict(d=d, h=h, sl=sl, incl=incl, strict=strict, y_ref=y_ref, w_tot=w_tot[:, sl],
                           kap=kap_a[:, sl], bet=bet_a[:, sl], khat=khat_a[:, sl], rho=rho_a[:, sl], v=v_a[:, sl]))
    def solve(chs):
        for q in chs:
            kr = jnp.concatenate([q['kap'], q['rho']], axis=0)
            bk = jnp.concatenate([q['bet'], q['khat']], axis=0)
            q['gram'] = _dot1(kr, bk, NT)
        for q in chs:
            gram = q.pop('gram')
            q['n'] = -jnp.where(q['strict'], gram[:T, :T], 0.0)
            l_k = jnp.where(q['strict'], gram[:T, T:], 0.0)
            q['m_b'] = jnp.where(q['incl'], gram[T:, :T], 0.0)
            m_k = jnp.where(q['incl'], gram[T:, T:], 0.0)
            q['lmv'] = _dot1(jnp.concatenate([l_k, m_k], axis=0), q['v'])
        for q in chs:
            q['x'] = jnp.concatenate([q['kap'], q['lmv'][:T]], axis=1)
        for lvl in range(6):
            mm = _dot3 if lvl < 3 else _dot1
            for q in chs:
                if lvl < 5:
                    nx = mm(q['n'], jnp.concatenate([q['x'], q['n']], axis=1))
                    q['n'], q['x'] = nx[:, 2 * RW_HEAD:], q['x'] + nx[:, :2 * RW_HEAD]
                else:
                    q['x'] = q['x'] + _dot1(q['n'], q['x'])
        for q in chs:
            q['p'] = _dot1(q['m_b'], q['x'])
            xb = _dot1(q['x'], q['bet'], TN)
            q['a_m'] = xb[:RW_HEAD]
            q['d_m'] = _dot1(q['v'], q['khat'], TN) - xb[RW_HEAD:]
        for q in chs:
            rho_p = q['rho'] - q['p'][:, :RW_HEAD]
            y_v = q['lmv'][T:] - q['p'][:, RW_HEAD:]
            s_old = s_scr[q['d'], q['h']]
            q['y_ref'][0, :, q['sl']] = _dot1(rho_p, s_old, NT) + y_v
            s_scr[q['d'], q['h']] = (s_old - _dot3(s_old, q['a_m']) + q['d_m']) * q['w_tot']

    for g0 in range(0, len(ch), RW_GROUP):
        solve(ch[g0:g0 + RW_GROUP])

    @pl.when(c == pl.num_programs(1) - 1)
    def _():
        sf_ref[0] = s_scr[...]


def _rw_core(st, r, v, kk, kd, bd, lw, s0):
    nb, L = st.nb, st.L
    nc = L // RW_T
    f1 = pl.BlockSpec((1, RW_T, D_GROUP), lambda b, c: (b, c, 0))
    b1 = pl.BlockSpec((1, RW_T, D_GROUP), lambda b, c: (b, nc - 1 - c, 0))
    f2 = pl.BlockSpec((1, 1, RW_T, D_GROUP), lambda b, c: (0, b, c, 0))
    b2 = pl.BlockSpec((1, 1, RW_T, D_GROUP), lambda b, c: (1, b, nc - 1 - c, 0))
    s_spec = pl.BlockSpec((1, 2, RW_HEADS, RW_HEAD, RW_HEAD), lambda b, c: (b, 0, 0, 0, 0))
    ys = jax.ShapeDtypeStruct((nb, L, D_GROUP), F32)
    return pl.pallas_call(
        _rw_core_kernel,
        grid=(nb, nc),
        in_specs=[f1, f1, f1, f2, f2, f2, b1, b1, b1, b2, b2, b2, s_spec],
        out_specs=[f1, b1, s_spec],
        out_shape=[ys, ys, jax.ShapeDtypeStruct((nb, 2, RW_HEADS, RW_HEAD, RW_HEAD), F32)],
        scratch_shapes=[pltpu.VMEM((2, RW_HEADS, RW_HEAD, RW_HEAD), F32)],
        compiler_params=_cparams("parallel", "arbitrary"),
        name="rwkv_core",
    )(r, v, kk, kd, bd, lw, r, v, kk, kd, bd, lw, s0)


RWO_TM = 512


def _rw_out_kernel(yf_ref, yb_ref, r_ref, v_ref, kd0_ref, kd1_ref, g_ref, lng_ref, lnb_ref, rk_ref, hones_ref, o_ref):
    y = yf_ref[...] + yb_ref[...]
    inv = 1.0 / RW_HEAD
    mu = _dot_sel_r(y, hones_ref[...]) * inv
    dv = y - mu
    var = _dot_sel_r(dv * dv, hones_ref[...]) * inv
    yn = dv * lax.rsqrt(var + RW_GN_EPS) * lng_ref[0] + lnb_ref[0]
    kmean = 0.5 * (kd0_ref[0] + kd1_ref[0])
    bonus = _dot_sel_r(r_ref[...] * kmean * rk_ref[0], hones_ref[...]) * v_ref[...]
    o_ref[...] = ((yn + bonus) * g_ref[...]).astype(BF16)


def _rw_out(st, yf, yb, r, v, kd, g, ln_g, ln_b, r_k, l):
    rows = st.rows
    flat = lambda a: a.reshape(rows, D_GROUP)
    row = pl.BlockSpec((RWO_TM, D_GROUP), lambda i: (i, 0))
    lay = pl.BlockSpec((1, 1, D_GROUP), lambda i: (l, 0, 0))
    kd2 = kd.reshape(2, rows, D_GROUP)
    return pl.pallas_call(
        _rw_out_kernel,
        grid=(rows // RWO_TM,),
        in_specs=[row, row, row, row,
                  pl.BlockSpec((1, RWO_TM, D_GROUP), lambda i: (0, i, 0)),
                  pl.BlockSpec((1, RWO_TM, D_GROUP), lambda i: (1, i, 0)),
                  row, lay, lay, lay, pl.BlockSpec((D_GROUP, D_GROUP), lambda i: (0, 0))],
        out_specs=row,
        out_shape=jax.ShapeDtypeStruct((rows, D_GROUP), BF16),
        compiler_params=_cparams("parallel"),
        name="rwkv_out",
    )(flat(yf), flat(yb), flat(r), flat(v), kd2, kd2, flat(g), ln_g.reshape(DEPTH, 1, D_GROUP),
      ln_b.reshape(DEPTH, 1, D_GROUP), r_k.reshape(DEPTH, 1, D_GROUP), _head_ones())


def _trunk_layer(st, x, l, mod4, p, states, ctx_kv):
    nb, L = st.nb, st.L
    z, zg = _zproj(st, x, mod4, p['w_in'], p['w_gate_cols'], l)
    z3 = z.reshape(nb, L, D_Z)

    ydir, sfr, sfi = _s5_scan(st, z3, p['s5_bq'], p['s5_cq'], p['s5_lamr'], p['s5_lami'],
                              states['s5_re'], states['s5_im'], l)
    y_s5 = _s5_out(st, z, ydir.reshape(2, st.rows, D_GROUP), p['s5_d'], p['s5_w_glu'], l)

    r, v, kk, g, kd, bd, lw = _rw_front(st, z3, p['rw_mu_prev'], p['rw_mu_next'], p['rw_w0'], p['rw_w2'],
                                        p['rw_a0'], p['rw_a2'], p['rw_g2'], p['rw_k_k'], p['rw_k_a'], l)
    yf, yb, rw_s = _rw_core(st, r, v, kk, kd, bd, lw, states['rw'])
    y_rw = _rw_out(st, yf, yb, r, v, kd, g, p['rw_ln_g'], p['rw_ln_b'], p['rw_r_k'], l)

    if ctx_kv is None:
        y_na, nk, nv = _na_ctx(st, z3)
    else:
        y_na = _na_nbr(st, z3, ctx_kv[0], ctx_kv[1], p['na_bias'], l)
        nk = nv = None

    y_ml, ml_c, ml_n, ml_m = _ml_mixer(st, z3, zg, p['ml_i_bias'], p['ml_f_bias'], p['ml_ln_g'], p['ml_ln_b'],
                                       states['ml_c'], states['ml_n'], states['ml_m'], l)

    ys = (y_s5, y_rw, y_na.reshape(st.rows, D_GROUP), y_ml.reshape(st.rows, D_GROUP))
    tail = _oproj(st, ys, x, mod4, p['w_out'], p['ln1_g'], p['ln1_b'], p['w_router'], p['b_router'], l)
    return tail, (nk, nv, sfr, sfi, rw_s, ml_c, ml_n, ml_m)


def _moe_and_ln2(tails, mod4, p, l):
    xs = []
    for st, (x1, h2, gates, keep) in zip((PROMPT, LATENT), tails):
        f = _moe_sorted(st, h2, gates, keep, p['moe_w_gate'], p['moe_w_up'], p['moe_w_down'], l)
        xs.append(_ln2f(st, x1, f, mod4, p['ln2_g'], p['ln2_b'], l))
    return xs


def kernel(x_prompt, x_sample, cache_nat_k, cache_nat_v, state_s5_re, state_s5_im, state_rwkv, state_mlstm_c, state_mlstm_n, state_mlstm_m, c, c_ctx, w_mod, b_mod, w_in, w_out, s5_lam_re, s5_lam_im, s5_log_step, s5_b_re, s5_b_im, s5_c_re, s5_c_im, s5_d, s5_w_glu, rw_mu_prev, rw_mu_next, rw_w0, rw_w2, rw_a0, rw_a2, rw_g2, rw_k_k, rw_k_a, rw_r_k, rw_ln_g, rw_ln_b, na_rpb, ml_i_bias, ml_f_bias, ml_ln_g, ml_ln_b, ln1_g, ln1_b, ln2_g, ln2_b, w_router, b_router, moe_w_gate, moe_w_up, moe_w_down):
    dt = x_prompt.dtype
    cond = jnp.concatenate([c_ctx[None, :], c, jnp.zeros((MOD_ROWS - 1 - DEC_BATCH, D_MODEL), F32)], axis=0)
    mod4 = _modulation(cond, w_mod, b_mod).reshape(DEPTH, MOD_ROWS, 1, 6 * D_MODEL)

    lbr, lbi, bbr, bbi = _s5_prep(s5_lam_re, s5_lam_im, s5_log_step, s5_b_re, s5_b_im)
    s5_bq, s5_cq, s5_lamr, s5_lami = _s5_block_params(lbr, lbi, bbr, bbi, s5_c_re, s5_c_im)
    w_in_b = w_in.astype(BF16)
    p = dict(w_in=w_in_b, w_gate_cols=w_in_b[:, :, D_Z:], w_out=w_out.astype(BF16),
             s5_bq=s5_bq, s5_cq=s5_cq, s5_lamr=s5_lamr, s5_lami=s5_lami, s5_d=s5_d, s5_w_glu=s5_w_glu,
             rw_mu_prev=rw_mu_prev, rw_mu_next=rw_mu_next, rw_w0=rw_w0, rw_w2=rw_w2, rw_a0=rw_a0, rw_a2=rw_a2,
             rw_g2=rw_g2, rw_k_k=rw_k_k, rw_k_a=rw_k_a, rw_r_k=rw_r_k, rw_ln_g=rw_ln_g, rw_ln_b=rw_ln_b,
             na_bias=_na_bias_table(na_rpb), ml_i_bias=ml_i_bias, ml_f_bias=ml_f_bias, ml_ln_g=ml_ln_g,
             ml_ln_b=ml_ln_b, ln1_g=ln1_g, ln1_b=ln1_b, ln2_g=ln2_g, ln2_b=ln2_b, w_router=w_router,
             b_router=b_router, moe_w_gate=moe_w_gate.astype(BF16), moe_w_up=moe_w_up.astype(BF16),
             moe_w_down=moe_w_down.astype(BF16))

    gp = BATCH // S5_SEQS
    zero_states = dict(
        s5_re=jnp.zeros((gp, 2, S5_SEQS, N_S5), F32), s5_im=jnp.zeros((gp, 2, S5_SEQS, N_S5), F32),
        rw=jnp.zeros((BATCH, 2, RW_HEADS, RW_HEAD, RW_HEAD), F32),
        ml_c=jnp.zeros((BATCH, 2, ML_HEADS, ML_HEAD, ML_HEAD), F32),
        ml_n=jnp.zeros((BATCH, ML_HEADS, 2, ML_HEAD), F32), ml_m=jnp.zeros((BATCH, ML_HEADS, 2, 1), F32))

    xp = x_prompt.reshape(PROMPT.rows, D_MODEL)
    xs = x_sample.reshape(LATENT.rows, D_MODEL)
    outs = [[] for _ in range(8)]
    for l in range(DEPTH):
        tail_p, ctx_t = _trunk_layer(PROMPT, xp, l, mod4, p, zero_states, None)
        for acc, t in zip(outs, ctx_t):
            acc.append(t)
        lat_states = dict(
            s5_re=state_s5_re[:, l].reshape(DEC_BATCH, 2, N_S5).transpose(1, 0, 2)[None],
            s5_im=state_s5_im[:, l].reshape(DEC_BATCH, 2, N_S5).transpose(1, 0, 2)[None],
            rw=state_rwkv[:, l], ml_c=state_mlstm_c[:, l],
            ml_n=state_mlstm_n[:, l].transpose(0, 2, 1, 3), ml_m=state_mlstm_m[:, l].transpose(0, 2, 1)[..., None])
        tail_s, _ = _trunk_layer(LATENT, xs, l, mod4, p, lat_states, (cache_nat_k, cache_nat_v))
        xp, xs = _moe_and_ln2((tail_p, tail_s), mod4, p, l)

    nk, nv, s5r, s5i, rw, mc, mn, mm = [jnp.stack(t, axis=1) for t in outs]

    def s5_state(t):
        return t.transpose(0, 3, 1, 2, 4).reshape(BATCH, DEPTH, 2, S5_GROUPS, S5_STATE)

    return (xp.reshape(BATCH, SEQ, D_MODEL), xs.reshape(DEC_BATCH, DEC_SEQ, D_MODEL),
            nk, nv, s5_state(s5r).astype(dt), s5_state(s5i).astype(dt), rw.astype(dt), mc.astype(dt),
            mn.transpose(0, 1, 3, 2, 4).astype(dt), mm[..., 0].transpose(0, 1, 3, 2).astype(dt))
```

```python
import functools
import math

import numpy as np
import jax
import jax.numpy as jnp
from jax import lax
from jax.experimental import pallas as pl
from jax.experimental.pallas import tpu as pltpu

F32 = jnp.float32
BF16 = jnp.bfloat16

D_MODEL = 2048
BATCH = 16
SEQ = 256
DEPTH = 4
DEC_BATCH = 8
DEC_SEQ = 1024
PAST_LEN = 256
GRID_W = 64
D_GROUP = D_MODEL // 4
S5_CH = 16
S5_GROUPS = D_GROUP // S5_CH
S5_STATE = 64
RW_HEAD = 64
RW_HEADS = D_GROUP // RW_HEAD
RW_LORA_W = 64
RW_LORA_A = 64
RW_LORA_G = 128
RW_GN_EPS = 64e-5
NA_HEAD = 64
NA_HEADS = D_GROUP // NA_HEAD
NA_WIN_H = 8
NA_WIN_W = 16
NEG_INF = -1e30
ML_HEAD = 128
ML_HEADS = D_GROUP // ML_HEAD
ML_CHUNK = 256
ML_GN_EPS = 1e-5
ROPE_BASE = 10000.0
N_EXPERTS = 16
N_EXPERT_GROUPS = 4
EXPERTS_PER_GROUP = N_EXPERTS // N_EXPERT_GROUPS
D_EXPERT = 512
DEEPNORM_ALPHA = (2 * DEPTH) ** 0.25
LN_EPS = 1e-5
S5_IN = D_GROUP
RW_IN = 3 * D_GROUP + RW_LORA_W + RW_LORA_A + RW_LORA_G
NA_IN = 3 * D_GROUP
ML_IN = 4 * D_GROUP + 4 * ML_HEADS
D_IN = S5_IN + RW_IN + NA_IN + ML_IN
N_GATE = 4 * ML_HEADS
D_Z = D_IN - N_GATE
RW_OFF = S5_IN
NA_OFF = S5_IN + RW_IN
ML_OFF = NA_OFF + NA_IN
MOD_ROWS = 16
LANE = 128
SUB = 8

VMEM_LIMIT = 56 * 1024 * 1024


def _cparams(*sem):
    return pltpu.CompilerParams(dimension_semantics=sem, vmem_limit_bytes=VMEM_LIMIT)


def _dg(a, b, dims):
    return lax.dot_general(a, b, (dims, ((), ())), preferred_element_type=F32)


NN = ((1,), (0,))
NT = ((1,), (1,))
TN = ((0,), (0,))


def _dot1(a, b, dims=NN):
    return _dg(a.astype(BF16), b.astype(BF16), dims)


def _split(x):
    hi = x.astype(BF16)
    lo = (x - hi.astype(F32)).astype(BF16)
    return hi, lo


def _split3(x):
    hi = x.astype(BF16)
    r = x - hi.astype(F32)
    mid = r.astype(BF16)
    lo = (r - mid.astype(F32)).astype(BF16)
    return hi, mid, lo


def _dot3(a, b, dims=NN):
    ah, al = _split(a)
    bh, bl = _split(b)
    return _dg(ah, bh, dims) + (_dg(ah, bl, dims) + _dg(al, bh, dims))


def _dot_sel_l(sel, x, dims=NN):
    s = sel.astype(BF16)
    hi, mid, lo = _split3(x)
    return _dg(s, hi, dims) + (_dg(s, mid, dims) + _dg(s, lo, dims))


def _dot_sel_r(x, sel, dims=NN):
    s = sel.astype(BF16)
    hi, mid, lo = _split3(x)
    return _dg(hi, s, dims) + (_dg(mid, s, dims) + _dg(lo, s, dims))


def _sigmoid(x):
    return 1.0 / (1.0 + jnp.exp(-x))


def _silu(x):
    return x * _sigmoid(x)


def _softplus(x):
    return jnp.maximum(x, 0.0) + jnp.log(1.0 + jnp.exp(-jnp.abs(x)))


def _iota(shape, dim):
    return lax.broadcasted_iota(jnp.int32, shape, dim)


class _Stream:
    def __init__(self, nb, L, latent):
        self.nb, self.L, self.latent = nb, L, latent
        self.rows = nb * L

    def mod_row(self, tile, tile_rows):
        if not self.latent:
            return 0
        return 1 + (tile * tile_rows) // self.L


PROMPT = _Stream(BATCH, SEQ, False)
LATENT = _Stream(DEC_BATCH, DEC_SEQ, True)


MOD_TN = 768


def _mod_kernel(cond_ref, w_ref, b_ref, o_ref):
    c = _silu(cond_ref[...])
    o_ref[0] = _dot3(c, w_ref[0]) + b_ref[0]


def _modulation(cond, w_mod, b_mod):
    n = 6 * D_MODEL
    return pl.pallas_call(
        _mod_kernel,
        grid=(DEPTH, n // MOD_TN),
        in_specs=[pl.BlockSpec((MOD_ROWS, D_MODEL), lambda l, j: (0, 0)),
                  pl.BlockSpec((1, D_MODEL, MOD_TN), lambda l, j: (l, 0, j)),
                  pl.BlockSpec((1, 1, MOD_TN), lambda l, j: (l, 0, j))],
        out_specs=pl.BlockSpec((1, MOD_ROWS, MOD_TN), lambda l, j: (l, 0, j)),
        out_shape=jax.ShapeDtypeStruct((DEPTH, MOD_ROWS, n), F32),
        compiler_params=_cparams("parallel", "parallel"),
        name="modulation",
    )(cond, w_mod, b_mod.reshape(DEPTH, 1, n))


ZP_TM = 1024
ZP_TN = 256


def _zproj_kernel(x_ref, mod_ref, w_ref, wg_ref, z_ref, zg_ref, h_scr):
    j = pl.program_id(1)

    @pl.when(j == 0)
    def _():
        m = mod_ref[0, 0]
        shift1 = m[:, 0:D_MODEL]
        scale1 = m[:, D_MODEL:2 * D_MODEL]
        h = (x_ref[...] * (1.0 + scale1) + shift1).astype(BF16)
        h_scr[...] = h
        zg_ref[...] = jnp.dot(h, wg_ref[0], preferred_element_type=F32)

    z_ref[...] = jnp.dot(h_scr[...], w_ref[0], preferred_element_type=F32)


def _zproj(st, x, mod4, w_in, w_gate_cols, l):
    return pl.pallas_call(
        _zproj_kernel,
        grid=(st.rows // ZP_TM, D_Z // ZP_TN),
        in_specs=[pl.BlockSpec((ZP_TM, D_MODEL), lambda i, j: (i, 0)),
                  pl.BlockSpec((1, 1, 1, 6 * D_MODEL), lambda i, j: (l, st.mod_row(i, ZP_TM), 0, 0)),
                  pl.BlockSpec((1, D_MODEL, ZP_TN), lambda i, j: (l, 0, j)),
                  pl.BlockSpec((1, D_MODEL, N_GATE), lambda i, j: (l, 0, 0))],
        out_specs=[pl.BlockSpec((ZP_TM, ZP_TN), lambda i, j: (i, j)),
                   pl.BlockSpec((ZP_TM, N_GATE), lambda i, j: (i, 0))],
        out_shape=[jax.ShapeDtypeStruct((st.rows, D_Z), F32),
                   jax.ShapeDtypeStruct((st.rows, N_GATE), F32)],
        scratch_shapes=[pltpu.VMEM((ZP_TM, D_MODEL), BF16)],
        compiler_params=_cparams("parallel", "arbitrary"),
        name="zproj",
    )(x, mod4, w_in, w_gate_cols)


OP_TM = 512


def _layer_norm(v, g, b):
    mu = jnp.mean(v, axis=-1, keepdims=True)
    d = v - mu
    var = jnp.mean(d * d, axis=-1, keepdims=True)
    return d * lax.rsqrt(var + LN_EPS) * g + b


def _route(scores, b_router):
    sel = scores + b_router
    s = [sel[:, e:e + 1] for e in range(N_EXPERTS)]
    in_top2 = []
    for g in range(N_EXPERT_GROUPS):
        for i in range(EXPERTS_PER_GROUP):
            e = g * EXPERTS_PER_GROUP + i
            cnt = jnp.zeros_like(s[e])
            for jj in range(EXPERTS_PER_GROUP):
                if jj == i:
                    continue
                o = g * EXPERTS_PER_GROUP + jj
                beats = (s[o] > s[e]) if jj > i else (s[o] >= s[e])
                cnt = cnt + jnp.where(beats, 1.0, 0.0)
            in_top2.append(cnt < 2.0)
    grp = []
    for g in range(N_EXPERT_GROUPS):
        tot = jnp.zeros_like(s[0])
        for i in range(EXPERTS_PER_GROUP):
            e = g * EXPERTS_PER_GROUP + i
            tot = tot + jnp.where(in_top2[e], s[e], 0.0)
        grp.append(tot)
    lane = _iota(scores.shape, 1)
    keep = jnp.zeros(scores.shape, F32)
    for g in range(N_EXPERT_GROUPS):
        cnt = jnp.zeros_like(s[0])
        for o in range(N_EXPERT_GROUPS):
            if o == g:
                continue
            beats = (grp[o] > grp[g]) if o > g else (grp[o] >= grp[g])
            cnt = cnt + jnp.where(beats, 1.0, 0.0)
        best = cnt < 1.0
        for i in range(EXPERTS_PER_GROUP):
            e = g * EXPERTS_PER_GROUP + i
            on = jnp.where(best, jnp.where(in_top2[e], 1.0, 0.0), 0.0)
            keep = keep + jnp.where(lane == e, on, 0.0)
    picked = scores * keep
    return picked / jnp.sum(picked, axis=-1, keepdims=True), keep


def _oproj_kernel(y0_ref, y1_ref, y2_ref, y3_ref, w_ref, x_ref, mod_ref, g_ref, b_ref, wr_ref, br_ref,
                  x1_ref, h2_ref, gate_ref, keep_ref):
    acc = None
    for k, yr in enumerate((y0_ref, y1_ref, y2_ref, y3_ref)):
        part = jnp.dot(yr[...], w_ref[0, k * D_GROUP:(k + 1) * D_GROUP, :], preferred_element_type=F32)
        acc = part if acc is None else acc + part
    m = mod_ref[0, 0]
    gate1 = m[:, 2 * D_MODEL:3 * D_MODEL]
    shift2 = m[:, 3 * D_MODEL:4 * D_MODEL]
    scale2 = m[:, 4 * D_MODEL:5 * D_MODEL]
    x1 = _layer_norm(DEEPNORM_ALPHA * x_ref[...] + gate1 * acc, g_ref[0], b_ref[0])
    x1_ref[...] = x1
    h2 = x1 * (1.0 + scale2) + shift2
    h2_ref[...] = h2.astype(BF16)
    scores = _sigmoid(_dot3(h2, wr_ref[...]))
    gate_ref[...], keep_ref[...] = _route(scores, br_ref[...])


def _oproj(st, ys, x, mod4, w_out_b, ln_g, ln_b, w_router, b_router, l):
    row = lambda i: (i, 0)
    return pl.pallas_call(
        _oproj_kernel,
        grid=(st.rows // OP_TM,),
        in_specs=[pl.BlockSpec((OP_TM, D_GROUP), row)] * 4 + [
            pl.BlockSpec((1, D_MODEL, D_MODEL), lambda i: (l, 0, 0)),
            pl.BlockSpec((OP_TM, D_MODEL), row),
            pl.BlockSpec((1, 1, 1, 6 * D_MODEL), lambda i: (l, st.mod_row(i, OP_TM), 0, 0)),
            pl.BlockSpec((1, 1, D_MODEL), lambda i: (l, 0, 0)),
            pl.BlockSpec((1, 1, D_MODEL), lambda i: (l, 0, 0)),
            pl.BlockSpec((D_MODEL, N_EXPERTS), lambda i: (0, 0)),
            pl.BlockSpec((1, N_EXPERTS), lambda i: (0, 0))],
        out_specs=[pl.BlockSpec((OP_TM, D_MODEL), row),
                   pl.BlockSpec((OP_TM, D_MODEL), row),
                   pl.BlockSpec((OP_TM, N_EXPERTS), row),
                   pl.BlockSpec((OP_TM, N_EXPERTS), row)],
        out_shape=[jax.ShapeDtypeStruct((st.rows, D_MODEL), F32),
                   jax.ShapeDtypeStruct((st.rows, D_MODEL), BF16),
                   jax.ShapeDtypeStruct((st.rows, N_EXPERTS), F32),
                   jax.ShapeDtypeStruct((st.rows, N_EXPERTS), F32)],
        compiler_params=_cparams("parallel"),
        name="oproj_ln1_router",
    )(*ys, w_out_b, x, mod4, ln_g.reshape(DEPTH, 1, D_MODEL), ln_b.reshape(DEPTH, 1, D_MODEL),
      w_router, b_router.reshape(1, N_EXPERTS))


LN_TM = 512


GS_BLK = 1024
GS_CAP = 320
GS_ALIGN = 2 * SUB
GS_COLS = 512
GS_STEP = 256
GS_VMEM_LIMIT = 60 * 1024 * 1024
GS_ROWS = pl.cdiv(GS_BLK + N_EXPERT_GROUPS * GS_ALIGN + GS_CAP, GS_COLS) * GS_COLS


def _moe_sorted_kernel(h_ref, gate_ref, keep_ref, wg_ref, wu_ref, wd_ref, f_ref, xs_scr, gs_scr, pt_scr, acc_scr,
                       seg_smem):
    e = pl.program_id(1)

    @pl.when(e == 0)
    def _():
        keep = keep_ref[...]
        lane = _iota(keep.shape, 1)
        member = []
        for g in range(N_EXPERT_GROUPS):
            in_g = (lane >= g * EXPERTS_PER_GROUP) & (lane < (g + 1) * EXPERTS_PER_GROUP)
            member.append(jnp.minimum(jnp.sum(jnp.where(in_g, keep, 0.0), axis=-1, keepdims=True), 1.0))
        lane4 = _iota((GS_BLK, LANE), 1)
        onehot = jnp.zeros((GS_BLK, LANE), F32)
        for g in range(N_EXPERT_GROUPS):
            onehot = onehot + jnp.where(lane4 == g, member[g], 0.0)
        counts = jnp.sum(onehot, axis=0, keepdims=True)
        starts = []
        start = jnp.zeros((1, 1), F32)
        for g in range(N_EXPERT_GROUPS):
            cnt_g = counts[:, g:g + 1]
            starts.append(start)
            seg_smem[2 * g] = jnp.sum(start).astype(jnp.int32)
            seg_smem[2 * g + 1] = jnp.sum(cnt_g).astype(jnp.int32)
            start = start + jnp.ceil(cnt_g * (1.0 / GS_ALIGN)) * GS_ALIGN
        onehot_b = onehot.astype(BF16)
        for r0 in range(0, GS_BLK, GS_STEP):
            tri = jnp.where(_iota((GS_STEP, GS_BLK), 1) <= _iota((GS_STEP, GS_BLK), 0) + r0, 1.0, 0.0).astype(BF16)
            csum = jnp.dot(tri, onehot_b, preferred_element_type=F32)
            dest = jnp.zeros((GS_STEP, 1), F32)
            for g in range(N_EXPERT_GROUPS):
                dest = dest + member[g][r0:r0 + GS_STEP] * (starts[g] + csum[:, g:g + 1] - 1.0)
            pt_scr[r0:r0 + GS_STEP, :] = jnp.where(
                _iota((GS_STEP, GS_ROWS), 1) == dest.astype(jnp.int32), 1.0, 0.0).astype(BF16)
        for c0 in range(0, GS_ROWS, GS_COLS):
            pt_c = pt_scr[:, c0:c0 + GS_COLS]
            xs_scr[c0:c0 + GS_COLS, :] = _dg(pt_c, h_ref[...], TN).astype(BF16)
            gs_scr[c0:c0 + GS_COLS, :] = _dot_sel_l(pt_c, gate_ref[...], TN)
        acc_scr[...] = jnp.zeros_like(acc_scr)

    g = e // EXPERTS_PER_GROUP
    seg_start = seg_smem[2 * g]
    seg_len = seg_smem[2 * g + 1]
    wg = wg_ref[0, 0]
    wu = wu_ref[0, 0]
    wd = wd_ref[0, 0]

    def window(c, carry):
        rows = pl.ds(pl.multiple_of(seg_start + c * GS_CAP, GS_ALIGN), GS_CAP)
        x = xs_scr[rows, :]
        a = jnp.dot(x, wg, preferred_element_type=F32)
        u = jnp.dot(x, wu, preferred_element_type=F32)
        gates = gs_scr[rows, :]
        ge = jnp.sum(jnp.where(_iota(gates.shape, 1) == e, gates, 0.0), axis=-1, keepdims=True)
        hid = (_silu(a) * u * ge).astype(BF16)
        acc_scr[rows, :] += jnp.dot(hid, wd, preferred_element_type=F32)
        return carry

    lax.fori_loop(0, (seg_len + GS_CAP - 1) // GS_CAP, window, 0)

    @pl.when(e == N_EXPERTS - 1)
    def _():
        pt = pt_scr[...]
        for c0 in range(0, D_MODEL, GS_COLS):
            acc_b = acc_scr[:, c0:c0 + GS_COLS].astype(BF16)
            f_ref[:, c0:c0 + GS_COLS] = jnp.dot(pt, acc_b, preferred_element_type=F32).astype(BF16)


def _moe_sorted(st, h2, gates, keep, wg_b, wu_b, wd_b, l):
    row = lambda i, e: (i, 0)
    wspec = lambda a, b: pl.BlockSpec((1, 1, a, b), lambda i, e: (l, e, 0, 0))
    return pl.pallas_call(
        _moe_sorted_kernel,
        grid=(st.rows // GS_BLK, N_EXPERTS),
        in_specs=[pl.BlockSpec((GS_BLK, D_MODEL), row, pipeline_mode=pl.Buffered(1)),
                  pl.BlockSpec((GS_BLK, N_EXPERTS), row), pl.BlockSpec((GS_BLK, N_EXPERTS), row),
                  wspec(D_MODEL, D_EXPERT), wspec(D_MODEL, D_EXPERT), wspec(D_EXPERT, D_MODEL)],
        out_specs=pl.BlockSpec((GS_BLK, D_MODEL), row),
        out_shape=jax.ShapeDtypeStruct((st.rows, D_MODEL), BF16),
        scratch_shapes=[pltpu.VMEM((GS_ROWS, D_MODEL), BF16), pltpu.VMEM((GS_ROWS, N_EXPERTS), F32),
                        pltpu.VMEM((GS_BLK, GS_ROWS), BF16), pltpu.VMEM((GS_ROWS, D_MODEL), F32),
                        pltpu.SMEM((2 * N_EXPERT_GROUPS,), jnp.int32)],
        compiler_params=pltpu.CompilerParams(dimension_semantics=("parallel", "arbitrary"),
                                             vmem_limit_bytes=GS_VMEM_LIMIT),
        name="moe_group_sorted",
    )(h2, gates, keep, wg_b, wu_b, wd_b)


def _ln2f_kernel(x_ref, f_ref, mod_ref, g_ref, b_ref, o_ref):
    gate2 = mod_ref[0, 0][:, 5 * D_MODEL:6 * D_MODEL]
    o_ref[...] = _layer_norm(DEEPNORM_ALPHA * x_ref[...] + gate2 * f_ref[...].astype(F32), g_ref[0], b_ref[0])


def _ln2f(st, x1, f, mod4, ln_g, ln_b, l):
    row = lambda i: (i, 0)
    return pl.pallas_call(
        _ln2f_kernel,
        grid=(st.rows // LN_TM,),
        in_specs=[pl.BlockSpec((LN_TM, D_MODEL), row),
                  pl.BlockSpec((LN_TM, D_MODEL), row),
                  pl.BlockSpec((1, 1, 1, 6 * D_MODEL), lambda i: (l, st.mod_row(i, LN_TM), 0, 0)),
                  pl.BlockSpec((1, 1, D_MODEL), lambda i: (l, 0, 0)),
                  pl.BlockSpec((1, 1, D_MODEL), lambda i: (l, 0, 0))],
        out_specs=pl.BlockSpec((LN_TM, D_MODEL), row),
        out_shape=jax.ShapeDtypeStruct((st.rows, D_MODEL), F32),
        compiler_params=_cparams("parallel"),
        name="ln2",
    )(x1, f, mod4, ln_g.reshape(DEPTH, 1, D_MODEL), ln_b.reshape(DEPTH, 1, D_MODEL))


S5_TC = 128
S5_SEQS = SUB
S5_Q = 4
S5_QS = S5_GROUPS // S5_Q * S5_STATE
N_S5 = S5_GROUPS * S5_STATE


def _s5_prep_kernel(lr_ref, li_ref, ls_ref, br_ref, bi_ref, e_ref, lbr_ref, lbi_ref, bbr_ref, bbi_ref):
    lr = lr_ref[0]
    li = li_ref[0]
    dt = jnp.exp(ls_ref[0])
    mag = jnp.exp(lr * dt)
    ang = li * dt
    ar = mag * jnp.cos(ang)
    ai = mag * jnp.sin(ang)
    lbr_ref[0] = ar
    lbi_ref[0] = ai
    den = lr * lr + li * li
    nr = ar - 1.0
    cr = (nr * lr + ai * li) / den
    ci = (ai * lr - nr * li) / den
    cr = _dot_sel_l(e_ref[...], cr)
    ci = _dot_sel_l(e_ref[...], ci)
    bre = br_ref[0]
    bim = bi_ref[0]
    bbr_ref[0] = cr * bre - ci * bim
    bbi_ref[0] = cr * bim + ci * bre


def _s5_prep(lam_re, lam_im, log_step, b_re, b_im):
    d2 = DEPTH * 2
    g, p, h = S5_GROUPS, S5_STATE, S5_CH
    bt_re = jnp.swapaxes(b_re, -1, -2).reshape(d2, g * h, p)
    bt_im = jnp.swapaxes(b_im, -1, -2).reshape(d2, g * h, p)
    expand = jnp.asarray(np.kron(np.eye(g, dtype=np.float32), np.ones((h, 1), np.float32)))
    spec_gp = pl.BlockSpec((1, g, p), lambda i: (i, 0, 0))
    spec_b = pl.BlockSpec((1, g * h, p), lambda i: (i, 0, 0))
    lbr, lbi, bbr, bbi = pl.pallas_call(
        _s5_prep_kernel,
        grid=(d2,),
        in_specs=[spec_gp, spec_gp, pl.BlockSpec((1, g, 1), lambda i: (i, 0, 0)), spec_b, spec_b,
                  pl.BlockSpec((g * h, g), lambda i: (0, 0))],
        out_specs=[spec_gp, spec_gp, spec_b, spec_b],
        out_shape=[jax.ShapeDtypeStruct((d2, g, p), F32)] * 2 + [jax.ShapeDtypeStruct((d2, g * h, p), F32)] * 2,
        compiler_params=_cparams("parallel"),
        name="s5_prep",
    )(lam_re.reshape(d2, g, p), lam_im.reshape(d2, g, p), log_step.reshape(d2, g, 1), bt_re, bt_im, expand)
    return lbr, lbi, bbr, bbi


def _s5_block_params(lbr, lbi, bbr, bbi, c_re, c_im):
    d2 = DEPTH * 2
    gq = S5_GROUPS // S5_Q
    eye = jnp.eye(gq, dtype=F32)

    def b_blocks(b):
        b = b.reshape(d2, S5_Q, gq, S5_CH, S5_STATE)
        return jnp.einsum('djghp,gk->djghkp', b, eye).reshape(d2, S5_Q, gq * S5_CH, gq * S5_STATE)

    def c_blocks(c):
        c = c.reshape(d2, S5_Q, gq, S5_CH, S5_STATE)
        return jnp.einsum('djghp,gk->djgpkh', c, eye).reshape(d2, S5_Q, gq * S5_STATE, gq * S5_CH)

    bq = jnp.concatenate([b_blocks(bbr), b_blocks(bbi)], axis=-1)
    cq = jnp.concatenate([c_blocks(c_re), -c_blocks(c_im)], axis=-2)
    return bq, cq, lbr.reshape(d2, 1, N_S5), lbi.reshape(d2, 1, N_S5)


def _s5_scan_kernel(u_ref, bq_ref, cq_ref, lr_ref, li_ref, s0r_ref, s0i_ref, y_ref, sfr_ref, sfi_ref,
                    utb, bur, bui, ytb, sr_scr, si_scr):
    d = pl.program_id(1)
    c = pl.program_id(2)
    nrow = S5_TC * S5_SEQS
    cw = D_GROUP // S5_Q

    @pl.when(c == 0)
    def _():
        sr_scr[...] = s0r_ref[0, 0]
        si_scr[...] = s0i_ref[0, 0]

    for b in range(S5_SEQS):
        for j in range(S5_Q):
            utb[j, pl.ds(b, S5_TC, stride=S5_SEQS), :] = u_ref[b, :, j * cw:(j + 1) * cw]
    for j in range(S5_Q):
        bu = _dot1(utb[j], bq_ref[0, j])
        bur[:, j * S5_QS:(j + 1) * S5_QS] = bu[:, :S5_QS]
        bui[:, j * S5_QS:(j + 1) * S5_QS] = bu[:, S5_QS:]
    for j in range(S5_Q):
        sl = slice(j * S5_QS, (j + 1) * S5_QS)
        lam_r = jnp.broadcast_to(lr_ref[0][:, sl], (S5_SEQS, S5_QS))
        lam_i = jnp.broadcast_to(li_ref[0][:, sl], (S5_SEQS, S5_QS))

        def step(t, carry, sl=sl, lam_r=lam_r, lam_i=lam_i):
            sr, si = carry
            te = jnp.where(d == 0, t, S5_TC - 1 - t)
            rows = pl.ds(pl.multiple_of(te * S5_SEQS, S5_SEQS), S5_SEQS)
            nr = lam_r * sr - lam_i * si + bur[rows, sl]
            ni = lam_r * si + lam_i * sr + bui[rows, sl]
            bur[rows, sl] = nr
            bui[rows, sl] = ni
            return nr, ni

        sr, si = lax.fori_loop(0, S5_TC, step, (sr_scr[:, sl], si_scr[:, sl]))
        sr_scr[:, sl] = sr
        si_scr[:, sl] = si
    for j in range(S5_Q):
        sl = slice(j * S5_QS, (j + 1) * S5_QS)
        yj = _dot1(bur[:, sl], cq_ref[0, j, :S5_QS, :]) + _dot1(bui[:, sl], cq_ref[0, j, S5_QS:, :])
        ytb[j] = yj
    for b in range(S5_SEQS):
        for j in range(S5_Q):
            y_ref[0, b, :, j * cw:(j + 1) * cw] = ytb[j, pl.ds(b, S5_TC, stride=S5_SEQS), :]

    @pl.when(c == pl.num_programs(2) - 1)
    def _():
        sfr_ref[0, 0] = sr_scr[...]
        sfi_ref[0, 0] = si_scr[...]


def _s5_scan(st, z3, bq, cq, lamr, lami, s0r, s0i, l):
    ng, nc = st.nb // S5_SEQS, st.L // S5_TC
    nrow = S5_TC * S5_SEQS
    chunk = lambda d, c: c + d * (nc - 1 - 2 * c)
    par = lambda g, d, c: (2 * l + d, 0, 0, 0)
    st_spec = pl.BlockSpec((1, 1, S5_SEQS, N_S5), lambda g, d, c: (g, d, 0, 0))
    return pl.pallas_call(
        _s5_scan_kernel,
        grid=(ng, 2, nc),
        in_specs=[pl.BlockSpec((S5_SEQS, S5_TC, D_GROUP), lambda g, d, c: (g, chunk(d, c), 0)),
                  pl.BlockSpec((1, S5_Q, D_GROUP // S5_Q, 2 * S5_QS), par),
                  pl.BlockSpec((1, S5_Q, 2 * S5_QS, D_GROUP // S5_Q), par),
                  pl.BlockSpec((1, 1, N_S5), lambda g, d, c: (2 * l + d, 0, 0)),
                  pl.BlockSpec((1, 1, N_S5), lambda g, d, c: (2 * l + d, 0, 0)),
                  st_spec, st_spec],
        out_specs=[pl.BlockSpec((1, S5_SEQS, S5_TC, D_GROUP), lambda g, d, c: (d, g, chunk(d, c), 0)),
                   st_spec, st_spec],
        out_shape=[jax.ShapeDtypeStruct((2, st.nb, st.L, D_GROUP), F32),
                   jax.ShapeDtypeStruct((ng, 2, S5_SEQS, N_S5), F32),
                   jax.ShapeDtypeStruct((ng, 2, S5_SEQS, N_S5), F32)],
        scratch_shapes=[pltpu.VMEM((S5_Q, nrow, LANE), F32), pltpu.VMEM((nrow, N_S5), F32),
                        pltpu.VMEM((nrow, N_S5), F32), pltpu.VMEM((S5_Q, nrow, LANE), F32),
                        pltpu.VMEM((S5_SEQS, N_S5), F32), pltpu.VMEM((S5_SEQS, N_S5), F32)],
        compiler_params=_cparams("parallel", "arbitrary", "arbitrary"),
        name="s5_scan",
    )(z3, bq, cq, lamr, lami, s0r, s0i)


S5_TM = 512


def _gelu_tanh(x):
    return 0.5 * x * (1.0 + jnp.tanh(math.sqrt(2.0 / math.pi) * (x + 0.044715 * (x * x * x))))


def _s5_out_kernel(u_ref, yf_ref, yb_ref, d_ref, w_ref, o_ref):
    y = yf_ref[0] + yb_ref[0] + d_ref[0] * u_ref[...]
    y = _gelu_tanh(y)
    o_ref[...] = (y * _sigmoid(_dot1(y, w_ref[0]))).astype(BF16)


def _s5_out(st, z, ydir, d_skip, w_glu, l):
    return pl.pallas_call(
        _s5_out_kernel,
        grid=(st.rows // S5_TM,),
        in_specs=[pl.BlockSpec((S5_TM, D_GROUP), lambda i: (i, 0)),
                  pl.BlockSpec((1, S5_TM, D_GROUP), lambda i: (0, i, 0)),
                  pl.BlockSpec((1, S5_TM, D_GROUP), lambda i: (1, i, 0)),
                  pl.BlockSpec((1, 1, D_GROUP), lambda i: (l, 0, 0)),
                  pl.BlockSpec((1, D_GROUP, D_GROUP), lambda i: (l, 0, 0))],
        out_specs=pl.BlockSpec((S5_TM, D_GROUP), lambda i: (i, 0)),
        out_shape=jax.ShapeDtypeStruct((st.rows, D_GROUP), BF16),
        compiler_params=_cparams("parallel"),
        name="s5_out",
    )(z, ydir, ydir, d_skip.reshape(DEPTH, 1, D_GROUP), w_glu)


NA_SCALE = NA_HEAD ** -0.5
NA_ROWS = DEC_SEQ // GRID_W
NA_KH = min(NA_WIN_H, NA_ROWS)
NA_QCOL = NA_OFF // LANE
NA_KCOL = (NA_OFF + D_GROUP) // LANE
NA_VCOL = (NA_OFF + 2 * D_GROUP) // LANE
NA_NDR = 2 * NA_WIN_H - 1
NA_AHEAD = 1


def _na_ctx_kernel(q_ref, k_ref, v_ref, o_ref, nk_ref, nv_ref):
    for hh in range(2):
        sl = slice(hh * NA_HEAD, (hh + 1) * NA_HEAD)
        q = q_ref[0, :, sl]
        k = k_ref[0, :, sl]
        v = v_ref[0, :, sl]
        nk_ref[0, hh] = k
        nv_ref[0, hh] = v
        s = _dot1(q, k, NT) * NA_SCALE
        e = jnp.exp(s - jnp.max(s, axis=-1, keepdims=True))
        o = _dot1(e, v) / jnp.sum(e, axis=-1, keepdims=True)
        o_ref[0, :, sl] = o.astype(BF16)


def _na_ctx(st, z3):
    blk = lambda col: pl.BlockSpec((1, st.L, LANE), lambda b, p: (b, 0, col + p))
    kv_spec = pl.BlockSpec((1, 2, st.L, NA_HEAD), lambda b, p: (b, p, 0, 0))
    kv_shape = jax.ShapeDtypeStruct((st.nb, NA_HEADS, st.L, NA_HEAD), F32)
    return pl.pallas_call(
        _na_ctx_kernel,
        grid=(st.nb, NA_HEADS // 2),
        in_specs=[blk(NA_QCOL), blk(NA_KCOL), blk(NA_VCOL)],
        out_specs=[pl.BlockSpec((1, st.L, LANE), lambda b, p: (b, 0, p)), kv_spec, kv_spec],
        out_shape=[jax.ShapeDtypeStruct((st.nb, st.L, D_GROUP), BF16), kv_shape, kv_shape],
        compiler_params=_cparams("parallel", "parallel"),
        name="na_context",
    )(z3, z3, z3)


def _na_nbr_kernel(q_ref, k_ref, v_ref, ck_ref, cv_ref, bias_ref, o_ref):
    nloc = NA_KH * GRID_W
    qc = _iota((GRID_W, nloc), 0)
    kc = _iota((GRID_W, nloc), 1) % GRID_W
    cs = jnp.clip(qc - NA_WIN_W // 2, 0, GRID_W - NA_WIN_W)
    col_in = (kc >= cs) & (kc < cs + NA_WIN_W)

    def scores(hh, r):
        sl = slice(hh * NA_HEAD, (hh + 1) * NA_HEAD)
        rs = min(max(r - NA_KH // 2, 0), NA_ROWS - NA_KH)
        q = q_ref[0, r * GRID_W:(r + 1) * GRID_W, sl]
        k = k_ref[0, rs * GRID_W:rs * GRID_W + nloc, sl]
        off = (rs - r + NA_WIN_H - 1) * GRID_W
        s_loc = _dot1(q, k, NT) * NA_SCALE + bias_ref[hh, :, off:off + nloc]
        s_loc = jnp.where(col_in, s_loc, NEG_INF)
        s_ctx = _dot1(q, ck_ref[0, 0, hh], NT) * NA_SCALE
        return s_loc, s_ctx

    def finish(hh, r, s_loc, s_ctx):
        sl = slice(hh * NA_HEAD, (hh + 1) * NA_HEAD)
        rs = min(max(r - NA_KH // 2, 0), NA_ROWS - NA_KH)
        v = v_ref[0, rs * GRID_W:rs * GRID_W + nloc, sl]
        m = jnp.maximum(jnp.max(s_loc, axis=-1, keepdims=True), jnp.max(s_ctx, axis=-1, keepdims=True))
        e_loc = jnp.exp(s_loc - m)
        e_ctx = jnp.exp(s_ctx - m)
        den = jnp.sum(e_loc, axis=-1, keepdims=True) + jnp.sum(e_ctx, axis=-1, keepdims=True)
        o = (_dot1(e_loc, v) + _dot1(e_ctx, cv_ref[0, 0, hh])) / den
        o_ref[0, r * GRID_W:(r + 1) * GRID_W, sl] = o.astype(BF16)

    blocks = [(hh, r) for hh in range(2) for r in range(NA_ROWS)]
    pending = []
    for i, blk in enumerate(blocks):
        pending.append(scores(*blk))
        if i >= NA_AHEAD:
            finish(*blocks[i - NA_AHEAD], *pending.pop(0))
    for j in range(len(blocks) - NA_AHEAD, len(blocks)):
        finish(*blocks[j], *pending.pop(0))


def _na_bias_table(rpb):
    qc = np.arange(GRID_W)[:, None]
    kc = np.arange(GRID_W)[None, :]
    dc = np.clip(kc - qc, -(NA_WIN_W - 1), NA_WIN_W - 1) + (NA_WIN_W - 1)
    t = rpb[:, :, :, dc]
    return jnp.transpose(t, (0, 1, 3, 2, 4)).reshape(DEPTH, NA_HEADS, GRID_W, NA_NDR * GRID_W)


def _na_nbr(st, z3, cache_k, cache_v, bias_tab, l):
    blk = lambda col: pl.BlockSpec((1, st.L, LANE), lambda b, p: (b, 0, col + p))
    cspec = pl.BlockSpec((1, 1, 2, PAST_LEN, NA_HEAD), lambda b, p: (b, l, p, 0, 0))
    return pl.pallas_call(
        _na_nbr_kernel,
        grid=(st.nb, NA_HEADS // 2),
        in_specs=[blk(NA_QCOL), blk(NA_KCOL), blk(NA_VCOL), cspec, cspec,
                  pl.BlockSpec((None, 2, GRID_W, NA_NDR * GRID_W), lambda b, p: (l, p, 0, 0))],
        out_specs=pl.BlockSpec((1, st.L, LANE), lambda b, p: (b, 0, p)),
        out_shape=jax.ShapeDtypeStruct((st.nb, st.L, D_GROUP), BF16),
        compiler_params=_cparams("parallel", "parallel"),
        name="na_neighbourhood",
    )(z3, z3, z3, cache_k, cache_v, bias_tab)


ML_COL = ML_OFF // LANE
ML_SCALE = ML_HEAD ** -0.5


def _log_sigmoid(x):
    return -_softplus(-x)


def _rope_tables(L):
    half = ML_HEAD // 2
    quarter = half // 2
    t = np.arange(L)
    inv_freq = ROPE_BASE ** (-np.arange(quarter, dtype=np.float32) / quarter)

    def tabs(pos):
        ang = pos.astype(np.float32)[:, None] * inv_freq[None, :].astype(np.float32)
        c, s = np.cos(ang), np.sin(ang)
        return np.concatenate([c, c], axis=-1), np.concatenate([-s, s], axis=-1)

    c1, s1 = tabs(t // GRID_W)
    c2, s2 = tabs(t % GRID_W)
    return (jnp.asarray(np.concatenate([c1, c2], axis=-1), F32),
            jnp.asarray(np.concatenate([s1, s2], axis=-1), F32))


def _ml_kernel(*refs, L, rotary, layer):
    nh = ML_HEADS
    ib_ref, fb_ref = refs[0], refs[1]
    q_refs, k_refs, v_refs, o_refs = (refs[2 + i * nh:2 + (i + 1) * nh] for i in range(4))
    (g_ref, cos_ref, sin_ref, c0_ref, n0_ref, m0_ref, lng_ref, lnb_ref,
     y_ref, cf_ref, nf_ref, mf_ref, qs, ks, hf, hb, c_scr, n_scr, m_scr) = refs[2 + 4 * nh:]
    T = ML_CHUNK
    nc = L // T

    if rotary:
        first = (_iota((L, ML_HEAD), 1) % (ML_HEAD // 2)) < ML_HEAD // 4

        def rope(x):
            quarter = ML_HEAD // 4
            partner = jnp.where(first, pltpu.roll(x, ML_HEAD - quarter, axis=1), pltpu.roll(x, quarter, axis=1))
            return x * cos_ref[...] + partner * sin_ref[...]
    else:
        rope = lambda x: x
    for h in range(nh):
        qs[h] = rope(q_refs[h][0]) * ML_SCALE
        ks[h] = rope(k_refs[h][0])
        for d in range(2):
            c_scr[d, h] = c0_ref[0, d, h]
            n_scr[d, h] = n0_ref[0, h, d:d + 1, :]
            m_scr[d, h] = m0_ref[0, h, d:d + 1, :]

    tj = _iota((T, T), 0)
    ts = _iota((T, T), 1)
    ones = jnp.ones((T, T), F32)
    upto_row = (tj <= ts, tj >= ts)
    upto_col = (ts <= tj, ts >= tj)

    def body(ci, carry):
        ch = []
        for d in range(2):
            cd = ci if d == 0 else nc - 1 - ci
            rows = pl.ds(pl.multiple_of(cd * T, T), T)
            for h in range(nh):
                g = g_ref[0, h, cd]
                ig = g[:, d:d + 1] + ib_ref[layer * 2 * nh + d * nh + h]
                fg = g[:, 2 + d:3 + d] + fb_ref[layer * 2 * nh + d * nh + h]
                lf = jnp.broadcast_to(_log_sigmoid(fg), (T, T))
                igb = jnp.broadcast_to(ig, (T, T))
                q = dict(d=d, h=h, rows=rows, ig=ig, qc=qs[h, rows, :], kc=ks[h, rows, :], vc=v_refs[h][0, rows, :])
                q['bcol'] = _dot_sel_l(jnp.where(upto_col[d], 1.0, 0.0), lf)
                q['brow_i'] = _dot_sel_l(ones, jnp.where(upto_row[d], lf, 0.0) - jnp.where(tj == ts, igb, 0.0))
                ch.append(q)
        for q in ch:
            q['qk'] = _dot1(q['qc'], q['kc'], NT)
            q['c_old'] = c_scr[q['d'], q['h']]
            q['qc_c'] = _dot1(q['qc'], q['c_old'])
        for q in ch:
            d, h = q['d'], q['h']
            dmat = jnp.where(upto_col[d], q['bcol'] - q['brow_i'], -jnp.inf)
            b1 = q['bcol'][:, 0:1]
            m_prev = m_scr[d, h]
            inter = b1 + m_prev
            m_t = jnp.maximum(inter, jnp.max(dmat, axis=-1, keepdims=True))
            qk = q['qk'] * jnp.exp(dmat - m_t)
            w_inter = jnp.exp(inter - m_t)
            n_old = n_scr[d, h]
            den = w_inter * jnp.sum(q['qc'] * n_old, axis=-1, keepdims=True) + jnp.sum(qk, axis=-1, keepdims=True)
            q['scale'] = 1.0 / jnp.maximum(jnp.abs(den), jnp.exp(-m_t))
            q['inter_part'] = w_inter * q['qc_c']
            q['intra'] = _dot1(qk, q['vc'])
            b_last = b1[T - 1:T, :] if d == 0 else b1[0:1, :]
            g_s = b_last - b1 + q['ig']
            m_new = jnp.maximum(b_last + m_prev, jnp.max(g_s, axis=0, keepdims=True))
            w_old = jnp.exp(b_last + m_prev - m_new)
            w_s = jnp.exp(g_s - m_new)
            q['c_new'] = _dot1(q['kc'], w_s * q['vc'], TN)
            q['w_old'] = w_old
            n_scr[d, h] = w_old * n_old + jnp.sum(w_s * q['kc'], axis=0, keepdims=True)
            m_scr[d, h] = m_new
        for q in ch:
            d, h = q['d'], q['h']
            hcur = (q['inter_part'] + q['intra']) * q['scale']
            if d == 0:
                hf[h, q['rows'], :] = hcur
            else:
                hb[h, q['rows'], :] = hcur
            c_scr[d, h] = q['w_old'] * q['c_old'] + q['c_new']
        return carry

    lax.fori_loop(0, nc, body, 0)

    for h in range(nh):
        hsum = hf[h] + hb[h]
        mu = jnp.mean(hsum, axis=-1, keepdims=True)
        dv = hsum - mu
        var = jnp.mean(dv * dv, axis=-1, keepdims=True)
        hn = dv * lax.rsqrt(var + ML_GN_EPS)
        y_ref[0, :, h * ML_HEAD:(h + 1) * ML_HEAD] = (
            _sigmoid(o_refs[h][0]) * (hn * lng_ref[0, :, h * ML_HEAD:(h + 1) * ML_HEAD]
                                      + lnb_ref[0, :, h * ML_HEAD:(h + 1) * ML_HEAD])).astype(BF16)
        for d in range(2):
            cf_ref[0, d, h] = c_scr[d, h]
            nf_ref[0, h, d:d + 1, :] = n_scr[d, h]
            mf_ref[0, h, d:d + 1, :] = m_scr[d, h]


def _ml_mixer(st, z3, zg, i_bias, f_bias, ln_g, ln_b, c0, n0, m0, l):
    L, nb, nh = st.L, st.nb, ML_HEADS
    nc = L // ML_CHUNK
    gates = zg.reshape(nb, L, 2, 2, nh).transpose(0, 4, 1, 2, 3).reshape(nb, nh, nc, ML_CHUNK, 4)
    cos_t, sin_t = _rope_tables(L)
    blk = lambda col: pl.BlockSpec((1, L, LANE), lambda b: (b, 0, col))
    head_blocks = [blk(ML_COL + part * nh + h) for part in range(4) for h in range(nh)]
    smem = pl.BlockSpec(memory_space=pltpu.SMEM)
    tab = pl.BlockSpec((L, ML_HEAD), lambda b: (0, 0))
    c_spec = pl.BlockSpec((1, 2, nh, ML_HEAD, ML_HEAD), lambda b: (b, 0, 0, 0, 0))
    n_spec = pl.BlockSpec((1, nh, 2, ML_HEAD), lambda b: (b, 0, 0, 0))
    m_spec = pl.BlockSpec((1, nh, 2, 1), lambda b: (b, 0, 0, 0))
    par = pl.BlockSpec((1, 1, D_GROUP), lambda b: (l, 0, 0))
    return pl.pallas_call(
        functools.partial(_ml_kernel, L=L, rotary=st.latent, layer=l),
        grid=(nb,),
        in_specs=[smem, smem] + head_blocks + [
            pl.BlockSpec((1, nh, nc, ML_CHUNK, 4), lambda b: (b, 0, 0, 0, 0)),
            tab, tab, c_spec, n_spec, m_spec, par, par],
        out_specs=[pl.BlockSpec((1, L, D_GROUP), lambda b: (b, 0, 0)), c_spec, n_spec, m_spec],
        out_shape=[jax.ShapeDtypeStruct((nb, L, D_GROUP), BF16),
                   jax.ShapeDtypeStruct((nb, 2, nh, ML_HEAD, ML_HEAD), F32),
                   jax.ShapeDtypeStruct((nb, nh, 2, ML_HEAD), F32),
                   jax.ShapeDtypeStruct((nb, nh, 2, 1), F32)],
        scratch_shapes=[pltpu.VMEM((nh, L, ML_HEAD), F32), pltpu.VMEM((nh, L, ML_HEAD), F32),
                        pltpu.VMEM((nh, L, ML_HEAD), F32), pltpu.VMEM((nh, L, ML_HEAD), F32),
                        pltpu.VMEM((2, nh, ML_HEAD, ML_HEAD), F32), pltpu.VMEM((2, nh, 1, ML_HEAD), F32),
                        pltpu.VMEM((2, nh, 1, 1), F32)],
        compiler_params=_cparams("parallel"),
        name="mlstm",
    )(i_bias.reshape(-1), f_bias.reshape(-1), *([z3] * (4 * nh)), gates, cos_t, sin_t, c0, n0, m0,
      ln_g.reshape(DEPTH, 1, D_GROUP), ln_b.reshape(DEPTH, 1, D_GROUP))


RW_TL = 256
RW_T = 64
RW_GROUP = 16
RW_RCOL = RW_OFF // D_GROUP
RW_LCOL = (RW_OFF + 3 * D_GROUP) // (2 * LANE)
N_LORA = RW_LORA_W + RW_LORA_A + RW_LORA_G


def _head_ones():
    return jnp.asarray(np.kron(np.eye(RW_HEADS, dtype=np.float32), np.ones((RW_HEAD, RW_HEAD), np.float32)))


def _rw_front_kernel(r_ref, k_ref, v_ref, lo_ref, rp_ref, kp_ref, vp_ref, lp_ref, rn_ref, kn_ref, vn_ref, ln_ref,
                     mup_ref, mun_ref, w0_ref, w2_ref, a0_ref, a2_ref, g2_ref, kk_ref, ka_ref, hones_ref,
                     ro_ref, vo_ref, kko_ref, go_ref, kd_ref, bd_ref, lw_ref):
    t = pl.program_id(1)
    first = t == 0
    last = t == pl.num_programs(1) - 1
    row = _iota((RW_TL, 1), 0)

    def shift(x_ref, p_ref, n_ref, lo, hi):
        x = x_ref[0]
        prev_edge = jnp.where(first, 0.0, p_ref[0, SUB - 1:SUB, :])
        next_edge = jnp.where(last, 0.0, n_ref[0, 0:1, :])
        prev = jnp.where(row == 0, prev_edge, pltpu.roll(x, 1, axis=0))
        nxt = jnp.where(row == RW_TL - 1, next_edge, pltpu.roll(x, RW_TL - 1, axis=0))
        return x + mup_ref[0][:, lo:hi] * (prev - x) + mun_ref[0][:, lo:hi] * (nxt - x)

    g = D_GROUP
    r = shift(r_ref, rp_ref, rn_ref, 0, g)
    k = shift(k_ref, kp_ref, kn_ref, g, 2 * g)
    v = shift(v_ref, vp_ref, vn_ref, 2 * g, 3 * g)
    lo = shift(lo_ref, lp_ref, ln_ref, 3 * g, 3 * g + N_LORA)
    zw = lo[:, :RW_LORA_W]
    za = lo[:, RW_LORA_W:RW_LORA_W + RW_LORA_A]
    zg = lo[:, RW_LORA_W + RW_LORA_A:]
    ro_ref[0] = r
    vo_ref[0] = v
    kk = k * kk_ref[0]
    ssq = _dot_sel_r(kk * kk, hones_ref[...])
    kk = kk * lax.rsqrt(ssq + 1e-12)
    kko_ref[0] = kk
    go_ref[0] = _dot3(_sigmoid(zg), g2_ref[0])
    tw = jnp.tanh(zw)
    for d in range(2):
        w_log = -_softplus(-(w0_ref[0, d:d + 1, :] + _dot3(tw, w2_ref[0, d]))) - 0.5
        lw_ref[d, 0] = -jnp.exp(w_log)
        a = _sigmoid(a0_ref[0, d:d + 1, :] + _dot3(za, a2_ref[0, d]))
        kd_ref[d, 0] = k * (1.0 + (a - 1.0) * ka_ref[0])
        bd_ref[d, 0] = kk * a


def _rw_front(st, z3, mu_prev, mu_next, w0, w2, a0, a2, g2, k_k, k_a, l):
    nb, L = st.nb, st.L
    nt = L // RW_TL
    tpb = RW_TL // SUB
    main = lambda w, col: pl.BlockSpec((1, RW_TL, w), lambda b, t: (b, t, col))
    prev = lambda w, col: pl.BlockSpec((1, SUB, w), lambda b, t: (b, jnp.maximum(t * tpb - 1, 0), col))
    nxt = lambda w, col: pl.BlockSpec((1, SUB, w), lambda b, t: (b, jnp.minimum((t + 1) * tpb, L // SUB - 1), col))
    cols = [(D_GROUP, RW_RCOL), (D_GROUP, RW_RCOL + 1), (D_GROUP, RW_RCOL + 2), (2 * LANE, RW_LCOL)]
    lay = lambda *shape: pl.BlockSpec((1,) + shape, lambda b, t: (l,) + (0,) * len(shape))
    out1 = pl.BlockSpec((1, RW_TL, D_GROUP), lambda b, t: (b, t, 0))
    out2 = pl.BlockSpec((2, 1, RW_TL, D_GROUP), lambda b, t: (0, b, t, 0))
    s1 = jax.ShapeDtypeStruct((nb, L, D_GROUP), F32)
    s2 = jax.ShapeDtypeStruct((2, nb, L, D_GROUP), F32)
    return pl.pallas_call(
        _rw_front_kernel,
        grid=(nb, nt),
        in_specs=[main(*c) for c in cols] + [prev(*c) for c in cols] + [nxt(*c) for c in cols] + [
            lay(1, RW_IN), lay(1, RW_IN), lay(2, D_GROUP), lay(2, RW_LORA_W, D_GROUP), lay(2, D_GROUP),
            lay(2, RW_LORA_A, D_GROUP), lay(RW_LORA_G, D_GROUP), lay(1, D_GROUP), lay(1, D_GROUP),
            pl.BlockSpec((D_GROUP, D_GROUP), lambda b, t: (0, 0))],
        out_specs=[out1, out1, out1, out1, out2, out2, out2],
        out_shape=[s1, s1, s1, s1, s2, s2, s2],
        compiler_params=_cparams("parallel", "parallel"),
        name="rwkv_front",
    )(*([z3] * 12), mu_prev.reshape(DEPTH, 1, RW_IN), mu_next.reshape(DEPTH, 1, RW_IN), w0, w2, a0, a2, g2,
      k_k.reshape(DEPTH, 1, D_GROUP), k_a.reshape(DEPTH, 1, D_GROUP), _head_ones())


def _rw_core_kernel(rf_ref, vf_ref, kkf_ref, kdf_ref, bdf_ref, lwf_ref,
                    rb_ref, vb_ref, kkb_ref, kdb_ref, bdb_ref, lwb_ref, s0_ref,
                    yf_ref, yb_ref, sf_ref, s_scr):
    c = pl.program_id(1)
    T = RW_T

    @pl.when(c == 0)
    def _():
        s_scr[...] = s0_ref[0]

    tj = _iota((T, T), 0)
    ts = _iota((T, T), 1)
    dirs = ((rf_ref, vf_ref, kkf_ref, kdf_ref, bdf_ref, lwf_ref, yf_ref),
            (rb_ref, vb_ref, kkb_ref, kdb_ref, bdb_ref, lwb_ref, yb_ref))
    ch = []
    for d, (r_ref, v_ref, kk_ref, kd_ref, bd_ref, lw_ref, y_ref) in enumerate(dirs):
        if d == 0:
            incl = ts <= tj
            strict = ts < tj
            last = T - 1
        else:
            incl = ts >= tj
            strict = ts > tj
            last = 0
        lw = lw_ref[0, 0]
        cum = _dot_sel_l(jnp.where(incl, 1.0, 0.0), lw)
        w_in = jnp.exp(cum)
        w_inv = jnp.exp(-cum)
        w_ex = jnp.exp(cum - lw)
        kap_a = kk_ref[0] * w_ex
        bet_a = bd_ref[0, 0] * w_inv
        khat_a = kd_ref[0, 0] * w_inv
        rho_a = r_ref[0] * w_in
        w_tot = w_in[last:last + 1, :]
        v_a = v_ref[0]
        for h in range(RW_HEADS):
            sl = slice(h * RW_HEAD, (h + 1) * RW_HEAD)
            ch.append(dict(d=d, h=h, sl=sl, incl=incl, strict=strict, y_ref=y_ref, w_tot=w_tot[:, sl],
                           kap=kap_a[:, sl], bet=bet_a[:, sl], khat=khat_a[:, sl], rho=rho_a[:, sl], v=v_a[:, sl]))
    def solve(chs):
        for q in chs:
            kr = jnp.concatenate([q['kap'], q['rho']], axis=0)
            bk = jnp.concatenate([q['bet'], q['khat']], axis=0)
            q['gram'] = _dot1(kr, bk, NT)
        for q in chs:
            gram = q.pop('gram')
            q['n'] = -jnp.where(q['strict'], gram[:T, :T], 0.0)
            l_k = jnp.where(q['strict'], gram[:T, T:], 0.0)
            q['m_b'] = jnp.where(q['incl'], gram[T:, :T], 0.0)
            m_k = jnp.where(q['incl'], gram[T:, T:], 0.0)
            q['lmv'] = _dot1(jnp.concatenate([l_k, m_k], axis=0), q['v'])
        for q in chs:
            q['x'] = jnp.concatenate([q['kap'], q['lmv'][:T]], axis=1)
        for lvl in range(6):
            mm = _dot3 if lvl < 2 else _dot1
            for q in chs:
                if lvl < 5:
                    nx = mm(q['n'], jnp.concatenate([q['x'], q['n']], axis=1))
                    q['n'], q['x'] = nx[:, 2 * RW_HEAD:], q['x'] + nx[:, :2 * RW_HEAD]
                else:
                    q['x'] = q['x'] + _dot1(q['n'], q['x'])
        for q in chs:
            q['p'] = _dot1(q['m_b'], q['x'])
            xb = _dot1(q['x'], q['bet'], TN)
            q['a_m'] = xb[:RW_HEAD]
            q['d_m'] = _dot1(q['v'], q['khat'], TN) - xb[RW_HEAD:]
        for q in chs:
            rho_p = q['rho'] - q['p'][:, :RW_HEAD]
            y_v = q['lmv'][T:] - q['p'][:, RW_HEAD:]
            s_old = s_scr[q['d'], q['h']]
            q['y_ref'][0, :, q['sl']] = _dot1(rho_p, s_old, NT) + y_v
            s_scr[q['d'], q['h']] = (s_old - _dot1(s_old, q['a_m']) + q['d_m']) * q['w_tot']

    for g0 in range(0, len(ch), RW_GROUP):
        solve(ch[g0:g0 + RW_GROUP])

    @pl.when(c == pl.num_programs(1) - 1)
    def _():
        sf_ref[0] = s_scr[...]


def _rw_core(st, r, v, kk, kd, bd, lw, s0):
    nb, L = st.nb, st.L
    nc = L // RW_T
    f1 = pl.BlockSpec((1, RW_T, D_GROUP), lambda b, c: (b, c, 0))
    b1 = pl.BlockSpec((1, RW_T, D_GROUP), lambda b, c: (b, nc - 1 - c, 0))
    f2 = pl.BlockSpec((1, 1, RW_T, D_GROUP), lambda b, c: (0, b, c, 0))
    b2 = pl.BlockSpec((1, 1, RW_T, D_GROUP), lambda b, c: (1, b, nc - 1 - c, 0))
    s_spec = pl.BlockSpec((1, 2, RW_HEADS, RW_HEAD, RW_HEAD), lambda b, c: (b, 0, 0, 0, 0))
    ys = jax.ShapeDtypeStruct((nb, L, D_GROUP), F32)
    return pl.pallas_call(
        _rw_core_kernel,
        grid=(nb, nc),
        in_specs=[f1, f1, f1, f2, f2, f2, b1, b1, b1, b2, b2, b2, s_spec],
        out_specs=[f1, b1, s_spec],
        out_shape=[ys, ys, jax.ShapeDtypeStruct((nb, 2, RW_HEADS, RW_HEAD, RW_HEAD), F32)],
        scratch_shapes=[pltpu.VMEM((2, RW_HEADS, RW_HEAD, RW_HEAD), F32)],
        compiler_params=_cparams("parallel", "arbitrary"),
        name="rwkv_core",
    )(r, v, kk, kd, bd, lw, r, v, kk, kd, bd, lw, s0)


RWO_TM = 512


def _rw_out_kernel(yf_ref, yb_ref, r_ref, v_ref, kd0_ref, kd1_ref, g_ref, lng_ref, lnb_ref, rk_ref, hones_ref, o_ref):
    y = yf_ref[...] + yb_ref[...]
    inv = 1.0 / RW_HEAD
    mu = _dot_sel_r(y, hones_ref[...]) * inv
    dv = y - mu
    var = _dot_sel_r(dv * dv, hones_ref[...]) * inv
    yn = dv * lax.rsqrt(var + RW_GN_EPS) * lng_ref[0] + lnb_ref[0]
    kmean = 0.5 * (kd0_ref[0] + kd1_ref[0])
    bonus = _dot_sel_r(r_ref[...] * kmean * rk_ref[0], hones_ref[...]) * v_ref[...]
    o_ref[...] = ((yn + bonus) * g_ref[...]).astype(BF16)


def _rw_out(st, yf, yb, r, v, kd, g, ln_g, ln_b, r_k, l):
    rows = st.rows
    flat = lambda a: a.reshape(rows, D_GROUP)
    row = pl.BlockSpec((RWO_TM, D_GROUP), lambda i: (i, 0))
    lay = pl.BlockSpec((1, 1, D_GROUP), lambda i: (l, 0, 0))
    kd2 = kd.reshape(2, rows, D_GROUP)
    return pl.pallas_call(
        _rw_out_kernel,
        grid=(rows // RWO_TM,),
        in_specs=[row, row, row, row,
                  pl.BlockSpec((1, RWO_TM, D_GROUP), lambda i: (0, i, 0)),
                  pl.BlockSpec((1, RWO_TM, D_GROUP), lambda i: (1, i, 0)),
                  row, lay, lay, lay, pl.BlockSpec((D_GROUP, D_GROUP), lambda i: (0, 0))],
        out_specs=row,
        out_shape=jax.ShapeDtypeStruct((rows, D_GROUP), BF16),
        compiler_params=_cparams("parallel"),
        name="rwkv_out",
    )(flat(yf), flat(yb), flat(r), flat(v), kd2, kd2, flat(g), ln_g.reshape(DEPTH, 1, D_GROUP),
      ln_b.reshape(DEPTH, 1, D_GROUP), r_k.reshape(DEPTH, 1, D_GROUP), _head_ones())


def _trunk_layer(st, x, l, mod4, p, states, ctx_kv):
    nb, L = st.nb, st.L
    z, zg = _zproj(st, x, mod4, p['w_in'], p['w_gate_cols'], l)
    z3 = z.reshape(nb, L, D_Z)

    ydir, sfr, sfi = _s5_scan(st, z3, p['s5_bq'], p['s5_cq'], p['s5_lamr'], p['s5_lami'],
                              states['s5_re'], states['s5_im'], l)
    y_s5 = _s5_out(st, z, ydir.reshape(2, st.rows, D_GROUP), p['s5_d'], p['s5_w_glu'], l)

    r, v, kk, g, kd, bd, lw = _rw_front(st, z3, p['rw_mu_prev'], p['rw_mu_next'], p['rw_w0'], p['rw_w2'],
                                        p['rw_a0'], p['rw_a2'], p['rw_g2'], p['rw_k_k'], p['rw_k_a'], l)
    yf, yb, rw_s = _rw_core(st, r, v, kk, kd, bd, lw, states['rw'])
    y_rw = _rw_out(st, yf, yb, r, v, kd, g, p['rw_ln_g'], p['rw_ln_b'], p['rw_r_k'], l)

    if ctx_kv is None:
        y_na, nk, nv = _na_ctx(st, z3)
    else:
        y_na = _na_nbr(st, z3, ctx_kv[0], ctx_kv[1], p['na_bias'], l)
        nk = nv = None

    y_ml, ml_c, ml_n, ml_m = _ml_mixer(st, z3, zg, p['ml_i_bias'], p['ml_f_bias'], p['ml_ln_g'], p['ml_ln_b'],
                                       states['ml_c'], states['ml_n'], states['ml_m'], l)

    ys = (y_s5, y_rw, y_na.reshape(st.rows, D_GROUP), y_ml.reshape(st.rows, D_GROUP))
    tail = _oproj(st, ys, x, mod4, p['w_out'], p['ln1_g'], p['ln1_b'], p['w_router'], p['b_router'], l)
    return tail, (nk, nv, sfr, sfi, rw_s, ml_c, ml_n, ml_m)


def _moe_and_ln2(tails, mod4, p, l):
    xs = []
    for st, (x1, h2, gates, keep) in zip((PROMPT, LATENT), tails):
        f = _moe_sorted(st, h2, gates, keep, p['moe_w_gate'], p['moe_w_up'], p['moe_w_down'], l)
        xs.append(_ln2f(st, x1, f, mod4, p['ln2_g'], p['ln2_b'], l))
    return xs


def kernel(x_prompt, x_sample, cache_nat_k, cache_nat_v, state_s5_re, state_s5_im, state_rwkv, state_mlstm_c, state_mlstm_n, state_mlstm_m, c, c_ctx, w_mod, b_mod, w_in, w_out, s5_lam_re, s5_lam_im, s5_log_step, s5_b_re, s5_b_im, s5_c_re, s5_c_im, s5_d, s5_w_glu, rw_mu_prev, rw_mu_next, rw_w0, rw_w2, rw_a0, rw_a2, rw_g2, rw_k_k, rw_k_a, rw_r_k, rw_ln_g, rw_ln_b, na_rpb, ml_i_bias, ml_f_bias, ml_ln_g, ml_ln_b, ln1_g, ln1_b, ln2_g, ln2_b, w_router, b_router, moe_w_gate, moe_w_up, moe_w_down):
    dt = x_prompt.dtype
    cond = jnp.concatenate([c_ctx[None, :], c, jnp.zeros((MOD_ROWS - 1 - DEC_BATCH, D_MODEL), F32)], axis=0)
    mod4 = _modulation(cond, w_mod, b_mod).reshape(DEPTH, MOD_ROWS, 1, 6 * D_MODEL)

    lbr, lbi, bbr, bbi = _s5_prep(s5_lam_re, s5_lam_im, s5_log_step, s5_b_re, s5_b_im)
    s5_bq, s5_cq, s5_lamr, s5_lami = _s5_block_params(lbr, lbi, bbr, bbi, s5_c_re, s5_c_im)
    w_in_b = w_in.astype(BF16)
    p = dict(w_in=w_in_b, w_gate_cols=w_in_b[:, :, D_Z:], w_out=w_out.astype(BF16),
             s5_bq=s5_bq, s5_cq=s5_cq, s5_lamr=s5_lamr, s5_lami=s5_lami, s5_d=s5_d, s5_w_glu=s5_w_glu,
             rw_mu_prev=rw_mu_prev, rw_mu_next=rw_mu_next, rw_w0=rw_w0, rw_w2=rw_w2, rw_a0=rw_a0, rw_a2=rw_a2,
             rw_g2=rw_g2, rw_k_k=rw_k_k, rw_k_a=rw_k_a, rw_r_k=rw_r_k, rw_ln_g=rw_ln_g, rw_ln_b=rw_ln_b,
             na_bias=_na_bias_table(na_rpb), ml_i_bias=ml_i_bias, ml_f_bias=ml_f_bias, ml_ln_g=ml_ln_g,
             ml_ln_b=ml_ln_b, ln1_g=ln1_g, ln1_b=ln1_b, ln2_g=ln2_g, ln2_b=ln2_b, w_router=w_router,
             b_router=b_router, moe_w_gate=moe_w_gate.astype(BF16), moe_w_up=moe_w_up.astype(BF16),
             moe_w_down=moe_w_down.astype(BF16))

    gp = BATCH // S5_SEQS
    zero_states = dict(
        s5_re=jnp.zeros((gp, 2, S5_SEQS, N_S5), F32), s5_im=jnp.zeros((gp, 2, S5_SEQS, N_S5), F32),
        rw=jnp.zeros((BATCH, 2, RW_HEADS, RW_HEAD, RW_HEAD), F32),
        ml_c=jnp.zeros((BATCH, 2, ML_HEADS, ML_HEAD, ML_HEAD), F32),
        ml_n=jnp.zeros((BATCH, ML_HEADS, 2, ML_HEAD), F32), ml_m=jnp.zeros((BATCH, ML_HEADS, 2, 1), F32))

    xp = x_prompt.reshape(PROMPT.rows, D_MODEL)
    xs = x_sample.reshape(LATENT.rows, D_MODEL)
    outs = [[] for _ in range(8)]
    for l in range(DEPTH):
        tail_p, ctx_t = _trunk_layer(PROMPT, xp, l, mod4, p, zero_states, None)
        for acc, t in zip(outs, ctx_t):
            acc.append(t)
        lat_states = dict(
            s5_re=state_s5_re[:, l].reshape(DEC_BATCH, 2, N_S5).transpose(1, 0, 2)[None],
            s5_im=state_s5_im[:, l].reshape(DEC_BATCH, 2, N_S5).transpose(1, 0, 2)[None],
            rw=state_rwkv[:, l], ml_c=state_mlstm_c[:, l],
            ml_n=state_mlstm_n[:, l].transpose(0, 2, 1, 3), ml_m=state_mlstm_m[:, l].transpose(0, 2, 1)[..., None])
        tail_s, _ = _trunk_layer(LATENT, xs, l, mod4, p, lat_states, (cache_nat_k, cache_nat_v))
        xp, xs = _moe_and_ln2((tail_p, tail_s), mod4, p, l)

    nk, nv, s5r, s5i, rw, mc, mn, mm = [jnp.stack(t, axis=1) for t in outs]

    def s5_state(t):
        return t.transpose(0, 3, 1, 2, 4).reshape(BATCH, DEPTH, 2, S5_GROUPS, S5_STATE)

    return (xp.reshape(BATCH, SEQ, D_MODEL), xs.reshape(DEC_BATCH, DEC_SEQ, D_MODEL),
            nk, nv, s5_state(s5r).astype(dt), s5_state(s5i).astype(dt), rw.astype(dt), mc.astype(dt),
            mn.transpose(0, 1, 3, 2, 4).astype(dt), mm[..., 0].transpose(0, 1, 3, 2).astype(dt))
```

```python
import functools
import math

import numpy as np
import jax
import jax.numpy as jnp
from jax import lax
from jax.experimental import pallas as pl
from jax.experimental.pallas import tpu as pltpu

F32 = jnp.float32
BF16 = jnp.bfloat16

D_MODEL = 2048
BATCH = 16
SEQ = 256
DEPTH = 4
DEC_BATCH = 8
DEC_SEQ = 1024
PAST_LEN = 256
GRID_W = 64
D_GROUP = D_MODEL // 4
S5_CH = 16
S5_GROUPS = D_GROUP // S5_CH
S5_STATE = 64
RW_HEAD = 64
RW_HEADS = D_GROUP // RW_HEAD
RW_LORA_W = 64
RW_LORA_A = 64
RW_LORA_G = 128
RW_GN_EPS = 64e-5
NA_HEAD = 64
NA_HEADS = D_GROUP // NA_HEAD
NA_WIN_H = 8
NA_WIN_W = 16
NEG_INF = -1e30
ML_HEAD = 128
ML_HEADS = D_GROUP // ML_HEAD
ML_CHUNK = 256
ML_GN_EPS = 1e-5
ROPE_BASE = 10000.0
N_EXPERTS = 16
N_EXPERT_GROUPS = 4
EXPERTS_PER_GROUP = N_EXPERTS // N_EXPERT_GROUPS
D_EXPERT = 512
DEEPNORM_ALPHA = (2 * DEPTH) ** 0.25
LN_EPS = 1e-5
S5_IN = D_GROUP
RW_IN = 3 * D_GROUP + RW_LORA_W + RW_LORA_A + RW_LORA_G
NA_IN = 3 * D_GROUP
ML_IN = 4 * D_GROUP + 4 * ML_HEADS
D_IN = S5_IN + RW_IN + NA_IN + ML_IN
N_GATE = 4 * ML_HEADS
D_Z = D_IN - N_GATE
RW_OFF = S5_IN
NA_OFF = S5_IN + RW_IN
ML_OFF = NA_OFF + NA_IN
MOD_ROWS = 16
LANE = 128
SUB = 8

VMEM_LIMIT = 56 * 1024 * 1024


def _cparams(*sem):
    return pltpu.CompilerParams(dimension_semantics=sem, vmem_limit_bytes=VMEM_LIMIT)


def _dg(a, b, dims):
    return lax.dot_general(a, b, (dims, ((), ())), preferred_element_type=F32)


NN = ((1,), (0,))
NT = ((1,), (1,))
TN = ((0,), (0,))


def _dot1(a, b, dims=NN):
    return _dg(a.astype(BF16), b.astype(BF16), dims)


def _split(x):
    hi = x.astype(BF16)
    lo = (x - hi.astype(F32)).astype(BF16)
    return hi, lo


def _split3(x):
    hi = x.astype(BF16)
    r = x - hi.astype(F32)
    mid = r.astype(BF16)
    lo = (r - mid.astype(F32)).astype(BF16)
    return hi, mid, lo


def _dot3(a, b, dims=NN):
    ah, al = _split(a)
    bh, bl = _split(b)
    return _dg(ah, bh, dims) + (_dg(ah, bl, dims) + _dg(al, bh, dims))


def _dot_sel_l(sel, x, dims=NN):
    s = sel.astype(BF16)
    hi, mid, lo = _split3(x)
    return _dg(s, hi, dims) + (_dg(s, mid, dims) + _dg(s, lo, dims))


def _dot_sel_r(x, sel, dims=NN):
    s = sel.astype(BF16)
    hi, mid, lo = _split3(x)
    return _dg(hi, s, dims) + (_dg(mid, s, dims) + _dg(lo, s, dims))


def _sigmoid(x):
    return 1.0 / (1.0 + jnp.exp(-x))


def _silu(x):
    return x * _sigmoid(x)


def _softplus(x):
    return jnp.maximum(x, 0.0) + jnp.log(1.0 + jnp.exp(-jnp.abs(x)))


def _iota(shape, dim):
    return lax.broadcasted_iota(jnp.int32, shape, dim)


class _Stream:
    def __init__(self, nb, L, latent):
        self.nb, self.L, self.latent = nb, L, latent
        self.rows = nb * L

    def mod_row(self, tile, tile_rows):
        if not self.latent:
            return 0
        return 1 + (tile * tile_rows) // self.L


PROMPT = _Stream(BATCH, SEQ, False)
LATENT = _Stream(DEC_BATCH, DEC_SEQ, True)


MOD_TN = 768


def _mod_kernel(cond_ref, w_ref, b_ref, o_ref):
    c = _silu(cond_ref[...])
    o_ref[0] = _dot3(c, w_ref[0]) + b_ref[0]


def _modulation(cond, w_mod, b_mod):
    n = 6 * D_MODEL
    return pl.pallas_call(
        _mod_kernel,
        grid=(DEPTH, n // MOD_TN),
        in_specs=[pl.BlockSpec((MOD_ROWS, D_MODEL), lambda l, j: (0, 0)),
                  pl.BlockSpec((1, D_MODEL, MOD_TN), lambda l, j: (l, 0, j)),
                  pl.BlockSpec((1, 1, MOD_TN), lambda l, j: (l, 0, j))],
        out_specs=pl.BlockSpec((1, MOD_ROWS, MOD_TN), lambda l, j: (l, 0, j)),
        out_shape=jax.ShapeDtypeStruct((DEPTH, MOD_ROWS, n), F32),
        compiler_params=_cparams("parallel", "parallel"),
        name="modulation",
    )(cond, w_mod, b_mod.reshape(DEPTH, 1, n))


ZP_TM = 1024
ZP_TN = 256


def _zproj_kernel(x_ref, mod_ref, w_ref, wg_ref, z_ref, zg_ref, h_scr):
    j = pl.program_id(1)

    @pl.when(j == 0)
    def _():
        m = mod_ref[0, 0]
        shift1 = m[:, 0:D_MODEL]
        scale1 = m[:, D_MODEL:2 * D_MODEL]
        h = (x_ref[...] * (1.0 + scale1) + shift1).astype(BF16)
        h_scr[...] = h
        zg_ref[...] = jnp.dot(h, wg_ref[0], preferred_element_type=F32)

    z_ref[...] = jnp.dot(h_scr[...], w_ref[0], preferred_element_type=F32)


def _zproj(st, x, mod4, w_in, w_gate_cols, l):
    return pl.pallas_call(
        _zproj_kernel,
        grid=(st.rows // ZP_TM, D_Z // ZP_TN),
        in_specs=[pl.BlockSpec((ZP_TM, D_MODEL), lambda i, j: (i, 0)),
                  pl.BlockSpec((1, 1, 1, 6 * D_MODEL), lambda i, j: (l, st.mod_row(i, ZP_TM), 0, 0)),
                  pl.BlockSpec((1, D_MODEL, ZP_TN), lambda i, j: (l, 0, j)),
                  pl.BlockSpec((1, D_MODEL, N_GATE), lambda i, j: (l, 0, 0))],
        out_specs=[pl.BlockSpec((ZP_TM, ZP_TN), lambda i, j: (i, j)),
                   pl.BlockSpec((ZP_TM, N_GATE), lambda i, j: (i, 0))],
        out_shape=[jax.ShapeDtypeStruct((st.rows, D_Z), F32),
                   jax.ShapeDtypeStruct((st.rows, N_GATE), F32)],
        scratch_shapes=[pltpu.VMEM((ZP_TM, D_MODEL), BF16)],
        compiler_params=_cparams("parallel", "arbitrary"),
        name="zproj",
    )(x, mod4, w_in, w_gate_cols)


OP_TM = 512


def _layer_norm(v, g, b):
    mu = jnp.mean(v, axis=-1, keepdims=True)
    d = v - mu
    var = jnp.mean(d * d, axis=-1, keepdims=True)
    return d * lax.rsqrt(var + LN_EPS) * g + b


def _route(scores, b_router):
    sel = scores + b_router
    s = [sel[e:e + 1, :] for e in range(N_EXPERTS)]
    in_top2 = []
    for g in range(N_EXPERT_GROUPS):
        for i in range(EXPERTS_PER_GROUP):
            e = g * EXPERTS_PER_GROUP + i
            cnt = jnp.zeros_like(s[e])
            for jj in range(EXPERTS_PER_GROUP):
                if jj == i:
                    continue
                o = g * EXPERTS_PER_GROUP + jj
                beats = (s[o] > s[e]) if jj > i else (s[o] >= s[e])
                cnt = cnt + jnp.where(beats, 1.0, 0.0)
            in_top2.append(cnt < 2.0)
    grp = []
    for g in range(N_EXPERT_GROUPS):
        tot = jnp.zeros_like(s[0])
        for i in range(EXPERTS_PER_GROUP):
            e = g * EXPERTS_PER_GROUP + i
            tot = tot + jnp.where(in_top2[e], s[e], 0.0)
        grp.append(tot)
    keep_rows = []
    for g in range(N_EXPERT_GROUPS):
        cnt = jnp.zeros_like(s[0])
        for o in range(N_EXPERT_GROUPS):
            if o == g:
                continue
            beats = (grp[o] > grp[g]) if o > g else (grp[o] >= grp[g])
            cnt = cnt + jnp.where(beats, 1.0, 0.0)
        best = cnt < 1.0
        for i in range(EXPERTS_PER_GROUP):
            e = g * EXPERTS_PER_GROUP + i
            keep_rows.append(jnp.where(best, jnp.where(in_top2[e], 1.0, 0.0), 0.0))
    keep = jnp.concatenate(keep_rows, axis=0)
    picked = scores * keep
    return picked / jnp.sum(picked, axis=0, keepdims=True), keep


def _oproj_kernel(y0_ref, y1_ref, y2_ref, y3_ref, w_ref, x_ref, mod_ref, g_ref, b_ref, wr_ref, br_ref,
                  x1_ref, h2_ref, gate_ref, keep_ref):
    acc = None
    for k, yr in enumerate((y0_ref, y1_ref, y2_ref, y3_ref)):
        part = jnp.dot(yr[...], w_ref[0, k * D_GROUP:(k + 1) * D_GROUP, :], preferred_element_type=F32)
        acc = part if acc is None else acc + part
    m = mod_ref[0, 0]
    gate1 = m[:, 2 * D_MODEL:3 * D_MODEL]
    shift2 = m[:, 3 * D_MODEL:4 * D_MODEL]
    scale2 = m[:, 4 * D_MODEL:5 * D_MODEL]
    x1 = _layer_norm(DEEPNORM_ALPHA * x_ref[...] + gate1 * acc, g_ref[0], b_ref[0])
    x1_ref[...] = x1
    h2 = x1 * (1.0 + scale2) + shift2
    h2_ref[...] = h2.astype(BF16)
    scores_t = _sigmoid(_dot3(wr_ref[...], h2, NT))
    gates_t, keep_t = _route(scores_t, br_ref[...])
    gate_ref[...] = gates_t.T
    keep_ref[...] = keep_t.T


def _oproj(st, ys, x, mod4, w_out_b, ln_g, ln_b, w_router, b_router, l):
    row = lambda i: (i, 0)
    return pl.pallas_call(
        _oproj_kernel,
        grid=(st.rows // OP_TM,),
        in_specs=[pl.BlockSpec((OP_TM, D_GROUP), row)] * 4 + [
            pl.BlockSpec((1, D_MODEL, D_MODEL), lambda i: (l, 0, 0)),
            pl.BlockSpec((OP_TM, D_MODEL), row),
            pl.BlockSpec((1, 1, 1, 6 * D_MODEL), lambda i: (l, st.mod_row(i, OP_TM), 0, 0)),
            pl.BlockSpec((1, 1, D_MODEL), lambda i: (l, 0, 0)),
            pl.BlockSpec((1, 1, D_MODEL), lambda i: (l, 0, 0)),
            pl.BlockSpec((N_EXPERTS, D_MODEL), lambda i: (0, 0)),
            pl.BlockSpec((N_EXPERTS, 1), lambda i: (0, 0))],
        out_specs=[pl.BlockSpec((OP_TM, D_MODEL), row),
                   pl.BlockSpec((OP_TM, D_MODEL), row),
                   pl.BlockSpec((OP_TM, N_EXPERTS), row),
                   pl.BlockSpec((OP_TM, N_EXPERTS), row)],
        out_shape=[jax.ShapeDtypeStruct((st.rows, D_MODEL), F32),
                   jax.ShapeDtypeStruct((st.rows, D_MODEL), BF16),
                   jax.ShapeDtypeStruct((st.rows, N_EXPERTS), F32),
                   jax.ShapeDtypeStruct((st.rows, N_EXPERTS), F32)],
        compiler_params=_cparams("parallel"),
        name="oproj_ln1_router",
    )(*ys, w_out_b, x, mod4, ln_g.reshape(DEPTH, 1, D_MODEL), ln_b.reshape(DEPTH, 1, D_MODEL),
      w_router.T, b_router.reshape(N_EXPERTS, 1))


LN_TM = 512


GS_BLK = 1024
GS_CAP = 320
GS_ALIGN = 2 * SUB
GS_COLS = 512
GS_STEP = 256
GS_VMEM_LIMIT = 60 * 1024 * 1024
GS_ROWS = pl.cdiv(GS_BLK + N_EXPERT_GROUPS * GS_ALIGN + GS_CAP, GS_COLS) * GS_COLS


def _moe_sorted_kernel(h_ref, gate_ref, keep_ref, wg_ref, wu_ref, wd_ref, f_ref, xs_scr, gs_scr, pt_scr, acc_scr,
                       seg_smem):
    e = pl.program_id(1)

    @pl.when(e == 0)
    def _():
        keep = keep_ref[...]
        lane = _iota(keep.shape, 1)
        member = []
        for g in range(N_EXPERT_GROUPS):
            in_g = (lane >= g * EXPERTS_PER_GROUP) & (lane < (g + 1) * EXPERTS_PER_GROUP)
            member.append(jnp.minimum(jnp.sum(jnp.where(in_g, keep, 0.0), axis=-1, keepdims=True), 1.0))
        lane4 = _iota((GS_BLK, LANE), 1)
        onehot = jnp.zeros((GS_BLK, LANE), F32)
        for g in range(N_EXPERT_GROUPS):
            onehot = onehot + jnp.where(lane4 == g, member[g], 0.0)
        counts = jnp.sum(onehot, axis=0, keepdims=True)
        starts = []
        start = jnp.zeros((1, 1), F32)
        for g in range(N_EXPERT_GROUPS):
            cnt_g = counts[:, g:g + 1]
            starts.append(start)
            seg_smem[2 * g] = jnp.sum(start).astype(jnp.int32)
            seg_smem[2 * g + 1] = jnp.sum(cnt_g).astype(jnp.int32)
            start = start + jnp.ceil(cnt_g * (1.0 / GS_ALIGN)) * GS_ALIGN
        onehot_b = onehot.astype(BF16)
        for r0 in range(0, GS_BLK, GS_STEP):
            tri = jnp.where(_iota((GS_STEP, GS_BLK), 1) <= _iota((GS_STEP, GS_BLK), 0) + r0, 1.0, 0.0).astype(BF16)
            csum = jnp.dot(tri, onehot_b, preferred_element_type=F32)
            dest = jnp.zeros((GS_STEP, 1), F32)
            for g in range(N_EXPERT_GROUPS):
                dest = dest + member[g][r0:r0 + GS_STEP] * (starts[g] + csum[:, g:g + 1] - 1.0)
            pt_scr[r0:r0 + GS_STEP, :] = jnp.where(
                _iota((GS_STEP, GS_ROWS), 1) == dest.astype(jnp.int32), 1.0, 0.0).astype(BF16)
        for c0 in range(0, GS_ROWS, GS_COLS):
            pt_c = pt_scr[:, c0:c0 + GS_COLS]
            xs_scr[c0:c0 + GS_COLS, :] = _dg(pt_c, h_ref[...], TN).astype(BF16)
            gs_scr[c0:c0 + GS_COLS, :] = _dot_sel_l(pt_c, gate_ref[...], TN)
        acc_scr[...] = jnp.zeros_like(acc_scr)

    g = e // EXPERTS_PER_GROUP
    seg_start = seg_smem[2 * g]
    seg_len = seg_smem[2 * g + 1]
    wg = wg_ref[0, 0]
    wu = wu_ref[0, 0]
    wd = wd_ref[0, 0]

    def window(c, carry):
        rows = pl.ds(pl.multiple_of(seg_start + c * GS_CAP, GS_ALIGN), GS_CAP)
        x = xs_scr[rows, :]
        a = jnp.dot(x, wg, preferred_element_type=F32)
        u = jnp.dot(x, wu, preferred_element_type=F32)
        gates = gs_scr[rows, :]
        ge = jnp.sum(jnp.where(_iota(gates.shape, 1) == e, gates, 0.0), axis=-1, keepdims=True)
        hid = (_silu(a) * u * ge).astype(BF16)
        acc_scr[rows, :] += jnp.dot(hid, wd, preferred_element_type=F32)
        return carry

    lax.fori_loop(0, (seg_len + GS_CAP - 1) // GS_CAP, window, 0)

    @pl.when(e == N_EXPERTS - 1)
    def _():
        pt = pt_scr[...]
        for c0 in range(0, D_MODEL, GS_COLS):
            acc_b = acc_scr[:, c0:c0 + GS_COLS].astype(BF16)
            f_ref[:, c0:c0 + GS_COLS] = jnp.dot(pt, acc_b, preferred_element_type=F32).astype(BF16)


def _moe_sorted(st, h2, gates, keep, wg_b, wu_b, wd_b, l):
    row = lambda i, e: (i, 0)
    wspec = lambda a, b: pl.BlockSpec((1, 1, a, b), lambda i, e: (l, e, 0, 0))
    return pl.pallas_call(
        _moe_sorted_kernel,
        grid=(st.rows // GS_BLK, N_EXPERTS),
        in_specs=[pl.BlockSpec((GS_BLK, D_MODEL), row, pipeline_mode=pl.Buffered(1)),
                  pl.BlockSpec((GS_BLK, N_EXPERTS), row), pl.BlockSpec((GS_BLK, N_EXPERTS), row),
                  wspec(D_MODEL, D_EXPERT), wspec(D_MODEL, D_EXPERT), wspec(D_EXPERT, D_MODEL)],
        out_specs=pl.BlockSpec((GS_BLK, D_MODEL), row),
        out_shape=jax.ShapeDtypeStruct((st.rows, D_MODEL), BF16),
        scratch_shapes=[pltpu.VMEM((GS_ROWS, D_MODEL), BF16), pltpu.VMEM((GS_ROWS, N_EXPERTS), F32),
                        pltpu.VMEM((GS_BLK, GS_ROWS), BF16), pltpu.VMEM((GS_ROWS, D_MODEL), F32),
                        pltpu.SMEM((2 * N_EXPERT_GROUPS,), jnp.int32)],
        compiler_params=pltpu.CompilerParams(dimension_semantics=("parallel", "arbitrary"),
                                             vmem_limit_bytes=GS_VMEM_LIMIT),
        name="moe_group_sorted",
    )(h2, gates, keep, wg_b, wu_b, wd_b)


def _ln2f_kernel(x_ref, f_ref, mod_ref, g_ref, b_ref, o_ref):
    gate2 = mod_ref[0, 0][:, 5 * D_MODEL:6 * D_MODEL]
    o_ref[...] = _layer_norm(DEEPNORM_ALPHA * x_ref[...] + gate2 * f_ref[...].astype(F32), g_ref[0], b_ref[0])


def _ln2f(st, x1, f, mod4, ln_g, ln_b, l):
    row = lambda i: (i, 0)
    return pl.pallas_call(
        _ln2f_kernel,
        grid=(st.rows // LN_TM,),
        in_specs=[pl.BlockSpec((LN_TM, D_MODEL), row),
                  pl.BlockSpec((LN_TM, D_MODEL), row),
                  pl.BlockSpec((1, 1, 1, 6 * D_MODEL), lambda i: (l, st.mod_row(i, LN_TM), 0, 0)),
                  pl.BlockSpec((1, 1, D_MODEL), lambda i: (l, 0, 0)),
                  pl.BlockSpec((1, 1, D_MODEL), lambda i: (l, 0, 0))],
        out_specs=pl.BlockSpec((LN_TM, D_MODEL), row),
        out_shape=jax.ShapeDtypeStruct((st.rows, D_MODEL), F32),
        compiler_params=_cparams("parallel"),
        name="ln2",
    )(x1, f, mod4, ln_g.reshape(DEPTH, 1, D_MODEL), ln_b.reshape(DEPTH, 1, D_MODEL))


S5_TC = 128
S5_SEQS = SUB
S5_Q = 4
S5_QS = S5_GROUPS // S5_Q * S5_STATE
N_S5 = S5_GROUPS * S5_STATE


def _s5_prep_kernel(lr_ref, li_ref, ls_ref, br_ref, bi_ref, e_ref, lbr_ref, lbi_ref, bbr_ref, bbi_ref):
    lr = lr_ref[0]
    li = li_ref[0]
    dt = jnp.exp(ls_ref[0])
    mag = jnp.exp(lr * dt)
    ang = li * dt
    ar = mag * jnp.cos(ang)
    ai = mag * jnp.sin(ang)
    lbr_ref[0] = ar
    lbi_ref[0] = ai
    den = lr * lr + li * li
    nr = ar - 1.0
    cr = (nr * lr + ai * li) / den
    ci = (ai * lr - nr * li) / den
    cr = _dot_sel_l(e_ref[...], cr)
    ci = _dot_sel_l(e_ref[...], ci)
    bre = br_ref[0]
    bim = bi_ref[0]
    bbr_ref[0] = cr * bre - ci * bim
    bbi_ref[0] = cr * bim + ci * bre


def _s5_prep(lam_re, lam_im, log_step, b_re, b_im):
    d2 = DEPTH * 2
    g, p, h = S5_GROUPS, S5_STATE, S5_CH
    bt_re = jnp.swapaxes(b_re, -1, -2).reshape(d2, g * h, p)
    bt_im = jnp.swapaxes(b_im, -1, -2).reshape(d2, g * h, p)
    expand = jnp.asarray(np.kron(np.eye(g, dtype=np.float32), np.ones((h, 1), np.float32)))
    spec_gp = pl.BlockSpec((1, g, p), lambda i: (i, 0, 0))
    spec_b = pl.BlockSpec((1, g * h, p), lambda i: (i, 0, 0))
    lbr, lbi, bbr, bbi = pl.pallas_call(
        _s5_prep_kernel,
        grid=(d2,),
        in_specs=[spec_gp, spec_gp, pl.BlockSpec((1, g, 1), lambda i: (i, 0, 0)), spec_b, spec_b,
                  pl.BlockSpec((g * h, g), lambda i: (0, 0))],
        out_specs=[spec_gp, spec_gp, spec_b, spec_b],
        out_shape=[jax.ShapeDtypeStruct((d2, g, p), F32)] * 2 + [jax.ShapeDtypeStruct((d2, g * h, p), F32)] * 2,
        compiler_params=_cparams("parallel"),
        name="s5_prep",
    )(lam_re.reshape(d2, g, p), lam_im.reshape(d2, g, p), log_step.reshape(d2, g, 1), bt_re, bt_im, expand)
    return lbr, lbi, bbr, bbi


def _s5_block_params(lbr, lbi, bbr, bbi, c_re, c_im):
    d2 = DEPTH * 2
    gq = S5_GROUPS // S5_Q
    eye = jnp.eye(gq, dtype=F32)

    def b_blocks(b):
        b = b.reshape(d2, S5_Q, gq, S5_CH, S5_STATE)
        return jnp.einsum('djghp,gk->djghkp', b, eye).reshape(d2, S5_Q, gq * S5_CH, gq * S5_STATE)

    def c_blocks(c):
        c = c.reshape(d2, S5_Q, gq, S5_CH, S5_STATE)
        return jnp.einsum('djghp,gk->djgpkh', c, eye).reshape(d2, S5_Q, gq * S5_STATE, gq * S5_CH)

    bq = jnp.concatenate([b_blocks(bbr), b_blocks(bbi)], axis=-1)
    cq = jnp.concatenate([c_blocks(c_re), -c_blocks(c_im)], axis=-2)
    return bq, cq, lbr.reshape(d2, 1, N_S5), lbi.reshape(d2, 1, N_S5)


def _s5_scan_kernel(u_ref, bq_ref, cq_ref, lr_ref, li_ref, s0r_ref, s0i_ref, y_ref, sfr_ref, sfi_ref,
                    utb, bur, bui, ytb, sr_scr, si_scr):
    d = pl.program_id(1)
    c = pl.program_id(2)
    nrow = S5_TC * S5_SEQS
    cw = D_GROUP // S5_Q

    @pl.when(c == 0)
    def _():
        sr_scr[...] = s0r_ref[0, 0]
        si_scr[...] = s0i_ref[0, 0]

    for b in range(S5_SEQS):
        for j in range(S5_Q):
            utb[j, pl.ds(b, S5_TC, stride=S5_SEQS), :] = u_ref[b, :, j * cw:(j + 1) * cw]
    for j in range(S5_Q):
        bu = _dot1(utb[j], bq_ref[0, j])
        bur[:, j * S5_QS:(j + 1) * S5_QS] = bu[:, :S5_QS]
        bui[:, j * S5_QS:(j + 1) * S5_QS] = bu[:, S5_QS:]
    for j in range(S5_Q):
        sl = slice(j * S5_QS, (j + 1) * S5_QS)
        lam_r = jnp.broadcast_to(lr_ref[0][:, sl], (S5_SEQS, S5_QS))
        lam_i = jnp.broadcast_to(li_ref[0][:, sl], (S5_SEQS, S5_QS))

        def step(t, carry, sl=sl, lam_r=lam_r, lam_i=lam_i):
            sr, si = carry
            te = jnp.where(d == 0, t, S5_TC - 1 - t)
            rows = pl.ds(pl.multiple_of(te * S5_SEQS, S5_SEQS), S5_SEQS)
            nr = lam_r * sr - lam_i * si + bur[rows, sl]
            ni = lam_r * si + lam_i * sr + bui[rows, sl]
            bur[rows, sl] = nr
            bui[rows, sl] = ni
            return nr, ni

        sr, si = lax.fori_loop(0, S5_TC, step, (sr_scr[:, sl], si_scr[:, sl]))
        sr_scr[:, sl] = sr
        si_scr[:, sl] = si
    for j in range(S5_Q):
        sl = slice(j * S5_QS, (j + 1) * S5_QS)
        yj = _dot1(bur[:, sl], cq_ref[0, j, :S5_QS, :]) + _dot1(bui[:, sl], cq_ref[0, j, S5_QS:, :])
        ytb[j] = yj
    for b in range(S5_SEQS):
        for j in range(S5_Q):
            y_ref[0, b, :, j * cw:(j + 1) * cw] = ytb[j, pl.ds(b, S5_TC, stride=S5_SEQS), :]

    @pl.when(c == pl.num_programs(2) - 1)
    def _():
        sfr_ref[0, 0] = sr_scr[...]
        sfi_ref[0, 0] = si_scr[...]


def _s5_scan(st, z3, bq, cq, lamr, lami, s0r, s0i, l):
    ng, nc = st.nb // S5_SEQS, st.L // S5_TC
    nrow = S5_TC * S5_SEQS
    chunk = lambda d, c: c + d * (nc - 1 - 2 * c)
    par = lambda g, d, c: (2 * l + d, 0, 0, 0)
    st_spec = pl.BlockSpec((1, 1, S5_SEQS, N_S5), lambda g, d, c: (g, d, 0, 0))
    return pl.pallas_call(
        _s5_scan_kernel,
        grid=(ng, 2, nc),
        in_specs=[pl.BlockSpec((S5_SEQS, S5_TC, D_GROUP), lambda g, d, c: (g, chunk(d, c), 0)),
                  pl.BlockSpec((1, S5_Q, D_GROUP // S5_Q, 2 * S5_QS), par),
                  pl.BlockSpec((1, S5_Q, 2 * S5_QS, D_GROUP // S5_Q), par),
                  pl.BlockSpec((1, 1, N_S5), lambda g, d, c: (2 * l + d, 0, 0)),
                  pl.BlockSpec((1, 1, N_S5), lambda g, d, c: (2 * l + d, 0, 0)),
                  st_spec, st_spec],
        out_specs=[pl.BlockSpec((1, S5_SEQS, S5_TC, D_GROUP), lambda g, d, c: (d, g, chunk(d, c), 0)),
                   st_spec, st_spec],
        out_shape=[jax.ShapeDtypeStruct((2, st.nb, st.L, D_GROUP), F32),
                   jax.ShapeDtypeStruct((ng, 2, S5_SEQS, N_S5), F32),
                   jax.ShapeDtypeStruct((ng, 2, S5_SEQS, N_S5), F32)],
        scratch_shapes=[pltpu.VMEM((S5_Q, nrow, LANE), F32), pltpu.VMEM((nrow, N_S5), F32),
                        pltpu.VMEM((nrow, N_S5), F32), pltpu.VMEM((S5_Q, nrow, LANE), F32),
                        pltpu.VMEM((S5_SEQS, N_S5), F32), pltpu.VMEM((S5_SEQS, N_S5), F32)],
        compiler_params=_cparams("parallel", "arbitrary", "arbitrary"),
        name="s5_scan",
    )(z3, bq, cq, lamr, lami, s0r, s0i)


S5_TM = 512


def _gelu_tanh(x):
    return 0.5 * x * (1.0 + jnp.tanh(math.sqrt(2.0 / math.pi) * (x + 0.044715 * (x * x * x))))


def _s5_out_kernel(u_ref, yf_ref, yb_ref, d_ref, w_ref, o_ref):
    y = yf_ref[0] + yb_ref[0] + d_ref[0] * u_ref[...]
    y = _gelu_tanh(y)
    o_ref[...] = (y * _sigmoid(_dot1(y, w_ref[0]))).astype(BF16)


def _s5_out(st, z, ydir, d_skip, w_glu, l):
    return pl.pallas_call(
        _s5_out_kernel,
        grid=(st.rows // S5_TM,),
        in_specs=[pl.BlockSpec((S5_TM, D_GROUP), lambda i: (i, 0)),
                  pl.BlockSpec((1, S5_TM, D_GROUP), lambda i: (0, i, 0)),
                  pl.BlockSpec((1, S5_TM, D_GROUP), lambda i: (1, i, 0)),
                  pl.BlockSpec((1, 1, D_GROUP), lambda i: (l, 0, 0)),
                  pl.BlockSpec((1, D_GROUP, D_GROUP), lambda i: (l, 0, 0))],
        out_specs=pl.BlockSpec((S5_TM, D_GROUP), lambda i: (i, 0)),
        out_shape=jax.ShapeDtypeStruct((st.rows, D_GROUP), BF16),
        compiler_params=_cparams("parallel"),
        name="s5_out",
    )(z, ydir, ydir, d_skip.reshape(DEPTH, 1, D_GROUP), w_glu)


NA_SCALE = NA_HEAD ** -0.5
NA_ROWS = DEC_SEQ // GRID_W
NA_KH = min(NA_WIN_H, NA_ROWS)
NA_QCOL = NA_OFF // LANE
NA_KCOL = (NA_OFF + D_GROUP) // LANE
NA_VCOL = (NA_OFF + 2 * D_GROUP) // LANE
NA_NDR = 2 * NA_WIN_H - 1
NA_AHEAD = 1


def _na_ctx_kernel(q_ref, k_ref, v_ref, o_ref, nk_ref, nv_ref):
    for hh in range(2):
        sl = slice(hh * NA_HEAD, (hh + 1) * NA_HEAD)
        q = q_ref[0, :, sl]
        k = k_ref[0, :, sl]
        v = v_ref[0, :, sl]
        nk_ref[0, hh] = k
        nv_ref[0, hh] = v
        s = _dot1(q, k, NT) * NA_SCALE
        e = jnp.exp(s - jnp.max(s, axis=-1, keepdims=True))
        o = _dot1(e, v) / jnp.sum(e, axis=-1, keepdims=True)
        o_ref[0, :, sl] = o.astype(BF16)


def _na_ctx(st, z3):
    blk = lambda col: pl.BlockSpec((1, st.L, LANE), lambda b, p: (b, 0, col + p))
    kv_spec = pl.BlockSpec((1, 2, st.L, NA_HEAD), lambda b, p: (b, p, 0, 0))
    kv_shape = jax.ShapeDtypeStruct((st.nb, NA_HEADS, st.L, NA_HEAD), F32)
    return pl.pallas_call(
        _na_ctx_kernel,
        grid=(st.nb, NA_HEADS // 2),
        in_specs=[blk(NA_QCOL), blk(NA_KCOL), blk(NA_VCOL)],
        out_specs=[pl.BlockSpec((1, st.L, LANE), lambda b, p: (b, 0, p)), kv_spec, kv_spec],
        out_shape=[jax.ShapeDtypeStruct((st.nb, st.L, D_GROUP), BF16), kv_shape, kv_shape],
        compiler_params=_cparams("parallel", "parallel"),
        name="na_context",
    )(z3, z3, z3)


def _na_nbr_kernel(q_ref, k_ref, v_ref, ck_ref, cv_ref, bias_ref, o_ref):
    nloc = NA_KH * GRID_W
    qc = _iota((GRID_W, nloc), 0)
    kc = _iota((GRID_W, nloc), 1) % GRID_W
    cs = jnp.clip(qc - NA_WIN_W // 2, 0, GRID_W - NA_WIN_W)
    col_in = (kc >= cs) & (kc < cs + NA_WIN_W)

    def scores(hh, r):
        sl = slice(hh * NA_HEAD, (hh + 1) * NA_HEAD)
        rs = min(max(r - NA_KH // 2, 0), NA_ROWS - NA_KH)
        q = q_ref[0, r * GRID_W:(r + 1) * GRID_W, sl]
        k = k_ref[0, rs * GRID_W:rs * GRID_W + nloc, sl]
        off = (rs - r + NA_WIN_H - 1) * GRID_W
        s_loc = _dot1(q, k, NT) * NA_SCALE + bias_ref[hh, :, off:off + nloc]
        s_loc = jnp.where(col_in, s_loc, NEG_INF)
        s_ctx = _dot1(q, ck_ref[0, 0, hh], NT) * NA_SCALE
        return s_loc, s_ctx

    def finish(hh, r, s_loc, s_ctx):
        sl = slice(hh * NA_HEAD, (hh + 1) * NA_HEAD)
        rs = min(max(r - NA_KH // 2, 0), NA_ROWS - NA_KH)
        v = v_ref[0, rs * GRID_W:rs * GRID_W + nloc, sl]
        m = jnp.maximum(jnp.max(s_loc, axis=-1, keepdims=True), jnp.max(s_ctx, axis=-1, keepdims=True))
        e_loc = jnp.exp(s_loc - m)
        e_ctx = jnp.exp(s_ctx - m)
        den = jnp.sum(e_loc, axis=-1, keepdims=True) + jnp.sum(e_ctx, axis=-1, keepdims=True)
        o = (_dot1(e_loc, v) + _dot1(e_ctx, cv_ref[0, 0, hh])) / den
        o_ref[0, r * GRID_W:(r + 1) * GRID_W, sl] = o.astype(BF16)

    blocks = [(hh, r) for hh in range(2) for r in range(NA_ROWS)]
    pending = []
    for i, blk in enumerate(blocks):
        pending.append(scores(*blk))
        if i >= NA_AHEAD:
            finish(*blocks[i - NA_AHEAD], *pending.pop(0))
    for j in range(len(blocks) - NA_AHEAD, len(blocks)):
        finish(*blocks[j], *pending.pop(0))


def _na_bias_table(rpb):
    qc = np.arange(GRID_W)[:, None]
    kc = np.arange(GRID_W)[None, :]
    dc = np.clip(kc - qc, -(NA_WIN_W - 1), NA_WIN_W - 1) + (NA_WIN_W - 1)
    t = rpb[:, :, :, dc]
    return jnp.transpose(t, (0, 1, 3, 2, 4)).reshape(DEPTH, NA_HEADS, GRID_W, NA_NDR * GRID_W)


def _na_nbr(st, z3, cache_k, cache_v, bias_tab, l):
    blk = lambda col: pl.BlockSpec((1, st.L, LANE), lambda b, p: (b, 0, col + p))
    cspec = pl.BlockSpec((1, 1, 2, PAST_LEN, NA_HEAD), lambda b, p: (b, l, p, 0, 0))
    return pl.pallas_call(
        _na_nbr_kernel,
        grid=(st.nb, NA_HEADS // 2),
        in_specs=[blk(NA_QCOL), blk(NA_KCOL), blk(NA_VCOL), cspec, cspec,
                  pl.BlockSpec((None, 2, GRID_W, NA_NDR * GRID_W), lambda b, p: (l, p, 0, 0))],
        out_specs=pl.BlockSpec((1, st.L, LANE), lambda b, p: (b, 0, p)),
        out_shape=jax.ShapeDtypeStruct((st.nb, st.L, D_GROUP), BF16),
        compiler_params=_cparams("parallel", "parallel"),
        name="na_neighbourhood",
    )(z3, z3, z3, cache_k, cache_v, bias_tab)


ML_COL = ML_OFF // LANE
ML_SCALE = ML_HEAD ** -0.5


def _log_sigmoid(x):
    return -_softplus(-x)


def _rope_tables(L):
    half = ML_HEAD // 2
    quarter = half // 2
    t = np.arange(L)
    inv_freq = ROPE_BASE ** (-np.arange(quarter, dtype=np.float32) / quarter)

    def tabs(pos):
        ang = pos.astype(np.float32)[:, None] * inv_freq[None, :].astype(np.float32)
        c, s = np.cos(ang), np.sin(ang)
        return np.concatenate([c, c], axis=-1), np.concatenate([-s, s], axis=-1)

    c1, s1 = tabs(t // GRID_W)
    c2, s2 = tabs(t % GRID_W)
    return (jnp.asarray(np.concatenate([c1, c2], axis=-1), F32),
            jnp.asarray(np.concatenate([s1, s2], axis=-1), F32))


def _ml_kernel(*refs, L, rotary, layer):
    nh = ML_HEADS
    ib_ref, fb_ref = refs[0], refs[1]
    q_refs, k_refs, v_refs, o_refs = (refs[2 + i * nh:2 + (i + 1) * nh] for i in range(4))
    (g_ref, cos_ref, sin_ref, c0_ref, n0_ref, m0_ref, lng_ref, lnb_ref,
     y_ref, cf_ref, nf_ref, mf_ref, qs, ks, hf, hb, c_scr, n_scr, m_scr) = refs[2 + 4 * nh:]
    T = ML_CHUNK
    nc = L // T

    if rotary:
        first = (_iota((L, ML_HEAD), 1) % (ML_HEAD // 2)) < ML_HEAD // 4

        def rope(x):
            quarter = ML_HEAD // 4
            partner = jnp.where(first, pltpu.roll(x, ML_HEAD - quarter, axis=1), pltpu.roll(x, quarter, axis=1))
            return x * cos_ref[...] + partner * sin_ref[...]
    else:
        rope = lambda x: x
    for h in range(nh):
        qs[h] = rope(q_refs[h][0]) * ML_SCALE
        ks[h] = rope(k_refs[h][0])
        for d in range(2):
            c_scr[d, h] = c0_ref[0, d, h]
            n_scr[d, h] = n0_ref[0, h, d:d + 1, :]
            m_scr[d, h] = m0_ref[0, h, d:d + 1, :]

    tj = _iota((T, T), 0)
    ts = _iota((T, T), 1)
    ones = jnp.ones((T, T), F32)
    upto_row = (tj <= ts, tj >= ts)
    upto_col = (ts <= tj, ts >= tj)

    def body(ci, carry):
        ch = []
        for d in range(2):
            cd = ci if d == 0 else nc - 1 - ci
            rows = pl.ds(pl.multiple_of(cd * T, T), T)
            for h in range(nh):
                g = g_ref[0, h, cd]
                ig = g[:, d:d + 1] + ib_ref[layer * 2 * nh + d * nh + h]
                fg = g[:, 2 + d:3 + d] + fb_ref[layer * 2 * nh + d * nh + h]
                lf = jnp.broadcast_to(_log_sigmoid(fg), (T, T))
                igb = jnp.broadcast_to(ig, (T, T))
                q = dict(d=d, h=h, rows=rows, ig=ig, qc=qs[h, rows, :], kc=ks[h, rows, :], vc=v_refs[h][0, rows, :])
                q['bcol'] = _dot_sel_l(jnp.where(upto_col[d], 1.0, 0.0), lf)
                q['brow_i'] = _dot_sel_l(ones, jnp.where(upto_row[d], lf, 0.0) - jnp.where(tj == ts, igb, 0.0))
                ch.append(q)
        for q in ch:
            q['qk'] = _dot1(q['qc'], q['kc'], NT)
            q['c_old'] = c_scr[q['d'], q['h']]
            q['qc_c'] = _dot1(q['qc'], q['c_old'])
        for q in ch:
            d, h = q['d'], q['h']
            dmat = jnp.where(upto_col[d], q['bcol'] - q['brow_i'], -jnp.inf)
            b1 = q['bcol'][:, 0:1]
            m_prev = m_scr[d, h]
            inter = b1 + m_prev
            m_t = jnp.maximum(inter, jnp.max(dmat, axis=-1, keepdims=True))
            qk = q['qk'] * jnp.exp(dmat - m_t)
            w_inter = jnp.exp(inter - m_t)
            n_old = n_scr[d, h]
            den = w_inter * jnp.sum(q['qc'] * n_old, axis=-1, keepdims=True) + jnp.sum(qk, axis=-1, keepdims=True)
            q['scale'] = 1.0 / jnp.maximum(jnp.abs(den), jnp.exp(-m_t))
            q['inter_part'] = w_inter * q['qc_c']
            q['intra'] = _dot1(qk, q['vc'])
            b_last = b1[T - 1:T, :] if d == 0 else b1[0:1, :]
            g_s = b_last - b1 + q['ig']
            m_new = jnp.maximum(b_last + m_prev, jnp.max(g_s, axis=0, keepdims=True))
            w_old = jnp.exp(b_last + m_prev - m_new)
            w_s = jnp.exp(g_s - m_new)
            q['c_new'] = _dot1(q['kc'], w_s * q['vc'], TN)
            q['w_old'] = w_old
            n_scr[d, h] = w_old * n_old + jnp.sum(w_s * q['kc'], axis=0, keepdims=True)
            m_scr[d, h] = m_new
        for q in ch:
            d, h = q['d'], q['h']
            hcur = (q['inter_part'] + q['intra']) * q['scale']
            if d == 0:
                hf[h, q['rows'], :] = hcur
            else:
                hb[h, q['rows'], :] = hcur
            c_scr[d, h] = q['w_old'] * q['c_old'] + q['c_new']
        return carry

    lax.fori_loop(0, nc, body, 0)

    for h in range(nh):
        hsum = hf[h] + hb[h]
        mu = jnp.mean(hsum, axis=-1, keepdims=True)
        dv = hsum - mu
        var = jnp.mean(dv * dv, axis=-1, keepdims=True)
        hn = dv * lax.rsqrt(var + ML_GN_EPS)
        y_ref[0, :, h * ML_HEAD:(h + 1) * ML_HEAD] = (
            _sigmoid(o_refs[h][0]) * (hn * lng_ref[0, :, h * ML_HEAD:(h + 1) * ML_HEAD]
                                      + lnb_ref[0, :, h * ML_HEAD:(h + 1) * ML_HEAD])).astype(BF16)
        for d in range(2):
            cf_ref[0, d, h] = c_scr[d, h]
            nf_ref[0, h, d:d + 1, :] = n_scr[d, h]
            mf_ref[0, h, d:d + 1, :] = m_scr[d, h]


def _ml_mixer(st, z3, zg, i_bias, f_bias, ln_g, ln_b, c0, n0, m0, l):
    L, nb, nh = st.L, st.nb, ML_HEADS
    nc = L // ML_CHUNK
    gates = zg.reshape(nb, L, 2, 2, nh).transpose(0, 4, 1, 2, 3).reshape(nb, nh, nc, ML_CHUNK, 4)
    cos_t, sin_t = _rope_tables(L)
    blk = lambda col: pl.BlockSpec((1, L, LANE), lambda b: (b, 0, col))
    head_blocks = [blk(ML_COL + part * nh + h) for part in range(4) for h in range(nh)]
    smem = pl.BlockSpec(memory_space=pltpu.SMEM)
    tab = pl.BlockSpec((L, ML_HEAD), lambda b: (0, 0))
    c_spec = pl.BlockSpec((1, 2, nh, ML_HEAD, ML_HEAD), lambda b: (b, 0, 0, 0, 0))
    n_spec = pl.BlockSpec((1, nh, 2, ML_HEAD), lambda b: (b, 0, 0, 0))
    m_spec = pl.BlockSpec((1, nh, 2, 1), lambda b: (b, 0, 0, 0))
    par = pl.BlockSpec((1, 1, D_GROUP), lambda b: (l, 0, 0))
    return pl.pallas_call(
        functools.partial(_ml_kernel, L=L, rotary=st.latent, layer=l),
        grid=(nb,),
        in_specs=[smem, smem] + head_blocks + [
            pl.BlockSpec((1, nh, nc, ML_CHUNK, 4), lambda b: (b, 0, 0, 0, 0)),
            tab, tab, c_spec, n_spec, m_spec, par, par],
        out_specs=[pl.BlockSpec((1, L, D_GROUP), lambda b: (b, 0, 0)), c_spec, n_spec, m_spec],
        out_shape=[jax.ShapeDtypeStruct((nb, L, D_GROUP), BF16),
                   jax.ShapeDtypeStruct((nb, 2, nh, ML_HEAD, ML_HEAD), F32),
                   jax.ShapeDtypeStruct((nb, nh, 2, ML_HEAD), F32),
                   jax.ShapeDtypeStruct((nb, nh, 2, 1), F32)],
        scratch_shapes=[pltpu.VMEM((nh, L, ML_HEAD), F32), pltpu.VMEM((nh, L, ML_HEAD), F32),
                        pltpu.VMEM((nh, L, ML_HEAD), F32), pltpu.VMEM((nh, L, ML_HEAD), F32),
                        pltpu.VMEM((2, nh, ML_HEAD, ML_HEAD), F32), pltpu.VMEM((2, nh, 1, ML_HEAD), F32),
                        pltpu.VMEM((2, nh, 1, 1), F32)],
        compiler_params=_cparams("parallel"),
        name="mlstm",
    )(i_bias.reshape(-1), f_bias.reshape(-1), *([z3] * (4 * nh)), gates, cos_t, sin_t, c0, n0, m0,
      ln_g.reshape(DEPTH, 1, D_GROUP), ln_b.reshape(DEPTH, 1, D_GROUP))


RW_TL = 256
RW_T = 64
RW_GROUP = 16
RW_RCOL = RW_OFF // D_GROUP
RW_LCOL = (RW_OFF + 3 * D_GROUP) // (2 * LANE)
N_LORA = RW_LORA_W + RW_LORA_A + RW_LORA_G


def _head_ones():
    return jnp.asarray(np.kron(np.eye(RW_HEADS, dtype=np.float32), np.ones((RW_HEAD, RW_HEAD), np.float32)))


def _rw_front_kernel(r_ref, k_ref, v_ref, lo_ref, rp_ref, kp_ref, vp_ref, lp_ref, rn_ref, kn_ref, vn_ref, ln_ref,
                     mup_ref, mun_ref, w0_ref, w2_ref, a0_ref, a2_ref, g2_ref, kk_ref, ka_ref, hones_ref,
                     ro_ref, vo_ref, kko_ref, go_ref, kd_ref, bd_ref, lw_ref):
    t = pl.program_id(1)
    first = t == 0
    last = t == pl.num_programs(1) - 1
    row = _iota((RW_TL, 1), 0)

    def shift(x_ref, p_ref, n_ref, lo, hi):
        x = x_ref[0]
        prev_edge = jnp.where(first, 0.0, p_ref[0, SUB - 1:SUB, :])
        next_edge = jnp.where(last, 0.0, n_ref[0, 0:1, :])
        prev = jnp.where(row == 0, prev_edge, pltpu.roll(x, 1, axis=0))
        nxt = jnp.where(row == RW_TL - 1, next_edge, pltpu.roll(x, RW_TL - 1, axis=0))
        return x + mup_ref[0][:, lo:hi] * (prev - x) + mun_ref[0][:, lo:hi] * (nxt - x)

    g = D_GROUP
    r = shift(r_ref, rp_ref, rn_ref, 0, g)
    k = shift(k_ref, kp_ref, kn_ref, g, 2 * g)
    v = shift(v_ref, vp_ref, vn_ref, 2 * g, 3 * g)
    lo = shift(lo_ref, lp_ref, ln_ref, 3 * g, 3 * g + N_LORA)
    zw = lo[:, :RW_LORA_W]
    za = lo[:, RW_LORA_W:RW_LORA_W + RW_LORA_A]
    zg = lo[:, RW_LORA_W + RW_LORA_A:]
    ro_ref[0] = r
    vo_ref[0] = v
    kk = k * kk_ref[0]
    ssq = _dot_sel_r(kk * kk, hones_ref[...])
    kk = kk * lax.rsqrt(ssq + 1e-12)
    kko_ref[0] = kk
    go_ref[0] = _dot3(_sigmoid(zg), g2_ref[0])
    tw = jnp.tanh(zw)
    for d in range(2):
        w_log = -_softplus(-(w0_ref[0, d:d + 1, :] + _dot3(tw, w2_ref[0, d]))) - 0.5
        lw_ref[d, 0] = -jnp.exp(w_log)
        a = _sigmoid(a0_ref[0, d:d + 1, :] + _dot3(za, a2_ref[0, d]))
        kd_ref[d, 0] = k * (1.0 + (a - 1.0) * ka_ref[0])
        bd_ref[d, 0] = kk * a


def _rw_front(st, z3, mu_prev, mu_next, w0, w2, a0, a2, g2, k_k, k_a, l):
    nb, L = st.nb, st.L
    nt = L // RW_TL
    tpb = RW_TL // SUB
    main = lambda w, col: pl.BlockSpec((1, RW_TL, w), lambda b, t: (b, t, col))
    prev = lambda w, col: pl.BlockSpec((1, SUB, w), lambda b, t: (b, jnp.maximum(t * tpb - 1, 0), col))
    nxt = lambda w, col: pl.BlockSpec((1, SUB, w), lambda b, t: (b, jnp.minimum((t + 1) * tpb, L // SUB - 1), col))
    cols = [(D_GROUP, RW_RCOL), (D_GROUP, RW_RCOL + 1), (D_GROUP, RW_RCOL + 2), (2 * LANE, RW_LCOL)]
    lay = lambda *shape: pl.BlockSpec((1,) + shape, lambda b, t: (l,) + (0,) * len(shape))
    out1 = pl.BlockSpec((1, RW_TL, D_GROUP), lambda b, t: (b, t, 0))
    out2 = pl.BlockSpec((2, 1, RW_TL, D_GROUP), lambda b, t: (0, b, t, 0))
    s1 = jax.ShapeDtypeStruct((nb, L, D_GROUP), F32)
    s2 = jax.ShapeDtypeStruct((2, nb, L, D_GROUP), F32)
    return pl.pallas_call(
        _rw_front_kernel,
        grid=(nb, nt),
        in_specs=[main(*c) for c in cols] + [prev(*c) for c in cols] + [nxt(*c) for c in cols] + [
            lay(1, RW_IN), lay(1, RW_IN), lay(2, D_GROUP), lay(2, RW_LORA_W, D_GROUP), lay(2, D_GROUP),
            lay(2, RW_LORA_A, D_GROUP), lay(RW_LORA_G, D_GROUP), lay(1, D_GROUP), lay(1, D_GROUP),
            pl.BlockSpec((D_GROUP, D_GROUP), lambda b, t: (0, 0))],
        out_specs=[out1, out1, out1, out1, out2, out2, out2],
        out_shape=[s1, s1, s1, s1, s2, s2, s2],
        compiler_params=_cparams("parallel", "parallel"),
        name="rwkv_front",
    )(*([z3] * 12), mu_prev.reshape(DEPTH, 1, RW_IN), mu_next.reshape(DEPTH, 1, RW_IN), w0, w2, a0, a2, g2,
      k_k.reshape(DEPTH, 1, D_GROUP), k_a.reshape(DEPTH, 1, D_GROUP), _head_ones())


def _rw_core_kernel(rf_ref, vf_ref, kkf_ref, kdf_ref, bdf_ref, lwf_ref,
                    rb_ref, vb_ref, kkb_ref, kdb_ref, bdb_ref, lwb_ref, s0_ref,
                    yf_ref, yb_ref, sf_ref, s_scr):
    c = pl.program_id(1)
    T = RW_T

    @pl.when(c == 0)
    def _():
        s_scr[...] = s0_ref[0]

    tj = _iota((T, T), 0)
    ts = _iota((T, T), 1)
    dirs = ((rf_ref, vf_ref, kkf_ref, kdf_ref, bdf_ref, lwf_ref, yf_ref),
            (rb_ref, vb_ref, kkb_ref, kdb_ref, bdb_ref, lwb_ref, yb_ref))
    ch = []
    for d, (r_ref, v_ref, kk_ref, kd_ref, bd_ref, lw_ref, y_ref) in enumerate(dirs):
        if d == 0:
            incl = ts <= tj
            strict = ts < tj
            last = T - 1
        else:
            incl = ts >= tj
            strict = ts > tj
            last = 0
        lw = lw_ref[0, 0]
        cum = _dot_sel_l(jnp.where(incl, 1.0, 0.0), lw)
        w_in = jnp.exp(cum)
        w_inv = jnp.exp(-cum)
        w_ex = jnp.exp(cum - lw)
        kap_a = kk_ref[0] * w_ex
        bet_a = bd_ref[0, 0] * w_inv
        khat_a = kd_ref[0, 0] * w_inv
        rho_a = r_ref[0] * w_in
        w_tot = w_in[last:last + 1, :]
        v_a = v_ref[0]
        for h in range(RW_HEADS):
            sl = slice(h * RW_HEAD, (h + 1) * RW_HEAD)
            ch.append(dict(d=d, h=h, sl=sl, incl=incl, strict=strict, y_ref=y_ref, w_tot=w_tot[:, sl],
                           kap=kap_a[:, sl], bet=bet_a[:, sl], khat=khat_a[:, sl], rho=rho_a[:, sl], v=v_a[:, sl]))
    def solve(chs):
        for q in chs:
            kr = jnp.concatenate([q['kap'], q['rho']], axis=0)
            bk = jnp.concatenate([q['bet'], q['khat']], axis=0)
            q['gram'] = _dot1(kr, bk, NT)
        for q in chs:
            gram = q.pop('gram')
            q['n'] = -jnp.where(q['strict'], gram[:T, :T], 0.0)
            l_k = jnp.where(q['strict'], gram[:T, T:], 0.0)
            q['m_b'] = jnp.where(q['incl'], gram[T:, :T], 0.0)
            m_k = jnp.where(q['incl'], gram[T:, T:], 0.0)
            q['lmv'] = _dot1(jnp.concatenate([l_k, m_k], axis=0), q['v'])
        for q in chs:
            q['x'] = jnp.concatenate([q['kap'], q['lmv'][:T]], axis=1)
        for lvl in range(6):
            mm = _dot3 if lvl < 2 else _dot1
            for q in chs:
                if lvl < 5:
                    nx = mm(q['n'], jnp.concatenate([q['x'], q['n']], axis=1))
                    q['n'], q['x'] = nx[:, 2 * RW_HEAD:], q['x'] + nx[:, :2 * RW_HEAD]
                else:
                    q['x'] = q['x'] + _dot1(q['n'], q['x'])
        for q in chs:
            q['p'] = _dot1(q['m_b'], q['x'])
            xb = _dot1(q['x'], q['bet'], TN)
            q['a_m'] = xb[:RW_HEAD]
            q['d_m'] = _dot1(q['v'], q['khat'], TN) - xb[RW_HEAD:]
        for q in chs:
            rho_p = q['rho'] - q['p'][:, :RW_HEAD]
            y_v = q['lmv'][T:] - q['p'][:, RW_HEAD:]
            s_old = s_scr[q['d'], q['h']]
            q['y_ref'][0, :, q['sl']] = _dot1(rho_p, s_old, NT) + y_v
            s_scr[q['d'], q['h']] = (s_old - _dot1(s_old, q['a_m']) + q['d_m']) * q['w_tot']

    for g0 in range(0, len(ch), RW_GROUP):
        solve(ch[g0:g0 + RW_GROUP])

    @pl.when(c == pl.num_programs(1) - 1)
    def _():
        sf_ref[0] = s_scr[...]


def _rw_core(st, r, v, kk, kd, bd, lw, s0):
    nb, L = st.nb, st.L
    nc = L // RW_T
    f1 = pl.BlockSpec((1, RW_T, D_GROUP), lambda b, c: (b, c, 0))
    b1 = pl.BlockSpec((1, RW_T, D_GROUP), lambda b, c: (b, nc - 1 - c, 0))
    f2 = pl.BlockSpec((1, 1, RW_T, D_GROUP), lambda b, c: (0, b, c, 0))
    b2 = pl.BlockSpec((1, 1, RW_T, D_GROUP), lambda b, c: (1, b, nc - 1 - c, 0))
    s_spec = pl.BlockSpec((1, 2, RW_HEADS, RW_HEAD, RW_HEAD), lambda b, c: (b, 0, 0, 0, 0))
    ys = jax.ShapeDtypeStruct((nb, L, D_GROUP), F32)
    return pl.pallas_call(
        _rw_core_kernel,
        grid=(nb, nc),
        in_specs=[f1, f1, f1, f2, f2, f2, b1, b1, b1, b2, b2, b2, s_spec],
        out_specs=[f1, b1, s_spec],
        out_shape=[ys, ys, jax.ShapeDtypeStruct((nb, 2, RW_HEADS, RW_HEAD, RW_HEAD), F32)],
        scratch_shapes=[pltpu.VMEM((2, RW_HEADS, RW_HEAD, RW_HEAD), F32)],
        compiler_params=_cparams("parallel", "arbitrary"),
        name="rwkv_core",
    )(r, v, kk, kd, bd, lw, r, v, kk, kd, bd, lw, s0)


RWO_TM = 512


def _rw_out_kernel(yf_ref, yb_ref, r_ref, v_ref, kd0_ref, kd1_ref, g_ref, lng_ref, lnb_ref, rk_ref, hones_ref, o_ref):
    y = yf_ref[...] + yb_ref[...]
    inv = 1.0 / RW_HEAD
    mu = _dot_sel_r(y, hones_ref[...]) * inv
    dv = y - mu
    var = _dot_sel_r(dv * dv, hones_ref[...]) * inv
    yn = dv * lax.rsqrt(var + RW_GN_EPS) * lng_ref[0] + lnb_ref[0]
    kmean = 0.5 * (kd0_ref[0] + kd1_ref[0])
    bonus = _dot_sel_r(r_ref[...] * kmean * rk_ref[0], hones_ref[...]) * v_ref[...]
    o_ref[...] = ((yn + bonus) * g_ref[...]).astype(BF16)


def _rw_out(st, yf, yb, r, v, kd, g, ln_g, ln_b, r_k, l):
    rows = st.rows
    flat = lambda a: a.reshape(rows, D_GROUP)
    row = pl.BlockSpec((RWO_TM, D_GROUP), lambda i: (i, 0))
    lay = pl.BlockSpec((1, 1, D_GROUP), lambda i: (l, 0, 0))
    kd2 = kd.reshape(2, rows, D_GROUP)
    return pl.pallas_call(
        _rw_out_kernel,
        grid=(rows // RWO_TM,),
        in_specs=[row, row, row, row,
                  pl.BlockSpec((1, RWO_TM, D_GROUP), lambda i: (0, i, 0)),
                  pl.BlockSpec((1, RWO_TM, D_GROUP), lambda i: (1, i, 0)),
                  row, lay, lay, lay, pl.BlockSpec((D_GROUP, D_GROUP), lambda i: (0, 0))],
        out_specs=row,
        out_shape=jax.ShapeDtypeStruct((rows, D_GROUP), BF16),
        compiler_params=_cparams("parallel"),
        name="rwkv_out",
    )(flat(yf), flat(yb), flat(r), flat(v), kd2, kd2, flat(g), ln_g.reshape(DEPTH, 1, D_GROUP),
      ln_b.reshape(DEPTH, 1, D_GROUP), r_k.reshape(DEPTH, 1, D_GROUP), _head_ones())


def _trunk_layer(st, x, l, mod4, p, states, ctx_kv):
    nb, L = st.nb, st.L
    z, zg = _zproj(st, x, mod4, p['w_in'], p['w_gate_cols'], l)
    z3 = z.reshape(nb, L, D_Z)

    ydir, sfr, sfi = _s5_scan(st, z3, p['s5_bq'], p['s5_cq'], p['s5_lamr'], p['s5_lami'],
                              states['s5_re'], states['s5_im'], l)
    y_s5 = _s5_out(st, z, ydir.reshape(2, st.rows, D_GROUP), p['s5_d'], p['s5_w_glu'], l)

    r, v, kk, g, kd, bd, lw = _rw_front(st, z3, p['rw_mu_prev'], p['rw_mu_next'], p['rw_w0'], p['rw_w2'],
                                        p['rw_a0'], p['rw_a2'], p['rw_g2'], p['rw_k_k'], p['rw_k_a'], l)
    yf, yb, rw_s = _rw_core(st, r, v, kk, kd, bd, lw, states['rw'])
    y_rw = _rw_out(st, yf, yb, r, v, kd, g, p['rw_ln_g'], p['rw_ln_b'], p['rw_r_k'], l)

    if ctx_kv is None:
        y_na, nk, nv = _na_ctx(st, z3)
    else:
        y_na = _na_nbr(st, z3, ctx_kv[0], ctx_kv[1], p['na_bias'], l)
        nk = nv = None

    y_ml, ml_c, ml_n, ml_m = _ml_mixer(st, z3, zg, p['ml_i_bias'], p['ml_f_bias'], p['ml_ln_g'], p['ml_ln_b'],
                                       states['ml_c'], states['ml_n'], states['ml_m'], l)

    ys = (y_s5, y_rw, y_na.reshape(st.rows, D_GROUP), y_ml.reshape(st.rows, D_GROUP))
    tail = _oproj(st, ys, x, mod4, p['w_out'], p['ln1_g'], p['ln1_b'], p['w_router'], p['b_router'], l)
    return tail, (nk, nv, sfr, sfi, rw_s, ml_c, ml_n, ml_m)


def _moe_and_ln2(tails, mod4, p, l):
    xs = []
    for st, (x1, h2, gates, keep) in zip((PROMPT, LATENT), tails):
        f = _moe_sorted(st, h2, gates, keep, p['moe_w_gate'], p['moe_w_up'], p['moe_w_down'], l)
        xs.append(_ln2f(st, x1, f, mod4, p['ln2_g'], p['ln2_b'], l))
    return xs


def kernel(x_prompt, x_sample, cache_nat_k, cache_nat_v, state_s5_re, state_s5_im, state_rwkv, state_mlstm_c, state_mlstm_n, state_mlstm_m, c, c_ctx, w_mod, b_mod, w_in, w_out, s5_lam_re, s5_lam_im, s5_log_step, s5_b_re, s5_b_im, s5_c_re, s5_c_im, s5_d, s5_w_glu, rw_mu_prev, rw_mu_next, rw_w0, rw_w2, rw_a0, rw_a2, rw_g2, rw_k_k, rw_k_a, rw_r_k, rw_ln_g, rw_ln_b, na_rpb, ml_i_bias, ml_f_bias, ml_ln_g, ml_ln_b, ln1_g, ln1_b, ln2_g, ln2_b, w_router, b_router, moe_w_gate, moe_w_up, moe_w_down):
    dt = x_prompt.dtype
    cond = jnp.concatenate([c_ctx[None, :], c, jnp.zeros((MOD_ROWS - 1 - DEC_BATCH, D_MODEL), F32)], axis=0)
    mod4 = _modulation(cond, w_mod, b_mod).reshape(DEPTH, MOD_ROWS, 1, 6 * D_MODEL)

    lbr, lbi, bbr, bbi = _s5_prep(s5_lam_re, s5_lam_im, s5_log_step, s5_b_re, s5_b_im)
    s5_bq, s5_cq, s5_lamr, s5_lami = _s5_block_params(lbr, lbi, bbr, bbi, s5_c_re, s5_c_im)
    w_in_b = w_in.astype(BF16)
    p = dict(w_in=w_in_b, w_gate_cols=w_in_b[:, :, D_Z:], w_out=w_out.astype(BF16),
             s5_bq=s5_bq, s5_cq=s5_cq, s5_lamr=s5_lamr, s5_lami=s5_lami, s5_d=s5_d, s5_w_glu=s5_w_glu,
             rw_mu_prev=rw_mu_prev, rw_mu_next=rw_mu_next, rw_w0=rw_w0, rw_w2=rw_w2, rw_a0=rw_a0, rw_a2=rw_a2,
             rw_g2=rw_g2, rw_k_k=rw_k_k, rw_k_a=rw_k_a, rw_r_k=rw_r_k, rw_ln_g=rw_ln_g, rw_ln_b=rw_ln_b,
             na_bias=_na_bias_table(na_rpb), ml_i_bias=ml_i_bias, ml_f_bias=ml_f_bias, ml_ln_g=ml_ln_g,
             ml_ln_b=ml_ln_b, ln1_g=ln1_g, ln1_b=ln1_b, ln2_g=ln2_g, ln2_b=ln2_b, w_router=w_router,
             b_router=b_router, moe_w_gate=moe_w_gate.astype(BF16), moe_w_up=moe_w_up.astype(BF16),
             moe_w_down=moe_w_down.astype(BF16))

    gp = BATCH // S5_SEQS
    zero_states = dict(
        s5_re=jnp.zeros((gp, 2, S5_SEQS, N_S5), F32), s5_im=jnp.zeros((gp, 2, S5_SEQS, N_S5), F32),
        rw=jnp.zeros((BATCH, 2, RW_HEADS, RW_HEAD, RW_HEAD), F32),
        ml_c=jnp.zeros((BATCH, 2, ML_HEADS, ML_HEAD, ML_HEAD), F32),
        ml_n=jnp.zeros((BATCH, ML_HEADS, 2, ML_HEAD), F32), ml_m=jnp.zeros((BATCH, ML_HEADS, 2, 1), F32))

    xp = x_prompt.reshape(PROMPT.rows, D_MODEL)
    xs = x_sample.reshape(LATENT.rows, D_MODEL)
    outs = [[] for _ in range(8)]
    for l in range(DEPTH):
        tail_p, ctx_t = _trunk_layer(PROMPT, xp, l, mod4, p, zero_states, None)
        for acc, t in zip(outs, ctx_t):
            acc.append(t)
        lat_states = dict(
            s5_re=state_s5_re[:, l].reshape(DEC_BATCH, 2, N_S5).transpose(1, 0, 2)[None],
            s5_im=state_s5_im[:, l].reshape(DEC_BATCH, 2, N_S5).transpose(1, 0, 2)[None],
            rw=state_rwkv[:, l], ml_c=state_mlstm_c[:, l],
            ml_n=state_mlstm_n[:, l].transpose(0, 2, 1, 3), ml_m=state_mlstm_m[:, l].transpose(0, 2, 1)[..., None])
        tail_s, _ = _trunk_layer(LATENT, xs, l, mod4, p, lat_states, (cache_nat_k, cache_nat_v))
        xp, xs = _moe_and_ln2((tail_p, tail_s), mod4, p, l)

    nk, nv, s5r, s5i, rw, mc, mn, mm = [jnp.stack(t, axis=1) for t in outs]

    def s5_state(t):
        return t.transpose(0, 3, 1, 2, 4).reshape(BATCH, DEPTH, 2, S5_GROUPS, S5_STATE)

    return (xp.reshape(BATCH, SEQ, D_MODEL), xs.reshape(DEC_BATCH, DEC_SEQ, D_MODEL),
            nk, nv, s5_state(s5r).astype(dt), s5_state(s5i).astype(dt), rw.astype(dt), mc.astype(dt),
            mn.transpose(0, 1, 3, 2, 4).astype(dt), mm[..., 0].transpose(0, 1, 3, 2).astype(dt))
```

```python
import functools
import math

import numpy as np
import jax
import jax.numpy as jnp
from jax import lax
from jax.experimental import pallas as pl
from jax.experimental.pallas import tpu as pltpu

F32 = jnp.float32
BF16 = jnp.bfloat16

D_MODEL = 2048
BATCH = 16
SEQ = 256
DEPTH = 4
DEC_BATCH = 8
DEC_SEQ = 1024
PAST_LEN = 256
GRID_W = 64
D_GROUP = D_MODEL // 4
S5_CH = 16
S5_GROUPS = D_GROUP // S5_CH
S5_STATE = 64
RW_HEAD = 64
RW_HEADS = D_GROUP // RW_HEAD
RW_LORA_W = 64
RW_LORA_A = 64
RW_LORA_G = 128
RW_GN_EPS = 64e-5
NA_HEAD = 64
NA_HEADS = D_GROUP // NA_HEAD
NA_WIN_H = 8
NA_WIN_W = 16
NEG_INF = -1e30
ML_HEAD = 128
ML_HEADS = D_GROUP // ML_HEAD
ML_CHUNK = 256
ML_GN_EPS = 1e-5
ROPE_BASE = 10000.0
N_EXPERTS = 16
N_EXPERT_GROUPS = 4
EXPERTS_PER_GROUP = N_EXPERTS // N_EXPERT_GROUPS
D_EXPERT = 512
DEEPNORM_ALPHA = (2 * DEPTH) ** 0.25
LN_EPS = 1e-5
S5_IN = D_GROUP
RW_IN = 3 * D_GROUP + RW_LORA_W + RW_LORA_A + RW_LORA_G
NA_IN = 3 * D_GROUP
ML_IN = 4 * D_GROUP + 4 * ML_HEADS
D_IN = S5_IN + RW_IN + NA_IN + ML_IN
N_GATE = 4 * ML_HEADS
D_Z = D_IN - N_GATE
RW_OFF = S5_IN
NA_OFF = S5_IN + RW_IN
ML_OFF = NA_OFF + NA_IN
MOD_ROWS = 16
LANE = 128
SUB = 8

VMEM_LIMIT = 56 * 1024 * 1024


def _cparams(*sem):
    return pltpu.CompilerParams(dimension_semantics=sem, vmem_limit_bytes=VMEM_LIMIT)


def _dg(a, b, dims):
    return lax.dot_general(a, b, (dims, ((), ())), preferred_element_type=F32)


NN = ((1,), (0,))
NT = ((1,), (1,))
TN = ((0,), (0,))


def _dot1(a, b, dims=NN):
    return _dg(a.astype(BF16), b.astype(BF16), dims)


def _split(x):
    hi = x.astype(BF16)
    lo = (x - hi.astype(F32)).astype(BF16)
    return hi, lo


def _split3(x):
    hi = x.astype(BF16)
    r = x - hi.astype(F32)
    mid = r.astype(BF16)
    lo = (r - mid.astype(F32)).astype(BF16)
    return hi, mid, lo


def _dot3(a, b, dims=NN):
    ah, al = _split(a)
    bh, bl = _split(b)
    return _dg(ah, bh, dims) + (_dg(ah, bl, dims) + _dg(al, bh, dims))


def _dot_sel_l(sel, x, dims=NN):
    s = sel.astype(BF16)
    hi, mid, lo = _split3(x)
    return _dg(s, hi, dims) + (_dg(s, mid, dims) + _dg(s, lo, dims))


def _dot_sel_r(x, sel, dims=NN):
    s = sel.astype(BF16)
    hi, mid, lo = _split3(x)
    return _dg(hi, s, dims) + (_dg(mid, s, dims) + _dg(lo, s, dims))


def _sigmoid(x):
    return 1.0 / (1.0 + jnp.exp(-x))


def _silu(x):
    return x * _sigmoid(x)


def _softplus(x):
    return jnp.maximum(x, 0.0) + jnp.log(1.0 + jnp.exp(-jnp.abs(x)))


def _iota(shape, dim):
    return lax.broadcasted_iota(jnp.int32, shape, dim)


class _Stream:
    def __init__(self, nb, L, latent):
        self.nb, self.L, self.latent = nb, L, latent
        self.rows = nb * L

    def mod_row(self, tile, tile_rows):
        if not self.latent:
            return 0
        return 1 + (tile * tile_rows) // self.L


PROMPT = _Stream(BATCH, SEQ, False)
LATENT = _Stream(DEC_BATCH, DEC_SEQ, True)


MOD_TN = 768


def _mod_kernel(cond_ref, w_ref, b_ref, o_ref):
    c = _silu(cond_ref[...])
    o_ref[0] = _dot3(c, w_ref[0]) + b_ref[0]


def _modulation(cond, w_mod, b_mod):
    n = 6 * D_MODEL
    return pl.pallas_call(
        _mod_kernel,
        grid=(DEPTH, n // MOD_TN),
        in_specs=[pl.BlockSpec((MOD_ROWS, D_MODEL), lambda l, j: (0, 0)),
                  pl.BlockSpec((1, D_MODEL, MOD_TN), lambda l, j: (l, 0, j)),
                  pl.BlockSpec((1, 1, MOD_TN), lambda l, j: (l, 0, j))],
        out_specs=pl.BlockSpec((1, MOD_ROWS, MOD_TN), lambda l, j: (l, 0, j)),
        out_shape=jax.ShapeDtypeStruct((DEPTH, MOD_ROWS, n), F32),
        compiler_params=_cparams("parallel", "parallel"),
        name="modulation",
    )(cond, w_mod, b_mod.reshape(DEPTH, 1, n))


ZP_TM = 1024
ZP_TN = 256


def _zproj_kernel(x_ref, mod_ref, w_ref, wg_ref, z_ref, zg_ref, h_scr):
    j = pl.program_id(1)

    @pl.when(j == 0)
    def _():
        m = mod_ref[0, 0]
        shift1 = m[:, 0:D_MODEL]
        scale1 = m[:, D_MODEL:2 * D_MODEL]
        h = (x_ref[...] * (1.0 + scale1) + shift1).astype(BF16)
        h_scr[...] = h
        zg_ref[...] = jnp.dot(h, wg_ref[0], preferred_element_type=F32)

    z_ref[...] = jnp.dot(h_scr[...], w_ref[0], preferred_element_type=F32)


def _zproj(st, x, mod4, w_in, w_gate_cols, l):
    return pl.pallas_call(
        _zproj_kernel,
        grid=(st.rows // ZP_TM, D_Z // ZP_TN),
        in_specs=[pl.BlockSpec((ZP_TM, D_MODEL), lambda i, j: (i, 0)),
                  pl.BlockSpec((1, 1, 1, 6 * D_MODEL), lambda i, j: (l, st.mod_row(i, ZP_TM), 0, 0)),
                  pl.BlockSpec((1, D_MODEL, ZP_TN), lambda i, j: (l, 0, j)),
                  pl.BlockSpec((1, D_MODEL, N_GATE), lambda i, j: (l, 0, 0))],
        out_specs=[pl.BlockSpec((ZP_TM, ZP_TN), lambda i, j: (i, j)),
                   pl.BlockSpec((ZP_TM, N_GATE), lambda i, j: (i, 0))],
        out_shape=[jax.ShapeDtypeStruct((st.rows, D_Z), F32),
                   jax.ShapeDtypeStruct((st.rows, N_GATE), F32)],
        scratch_shapes=[pltpu.VMEM((ZP_TM, D_MODEL), BF16)],
        compiler_params=_cparams("parallel", "arbitrary"),
        name="zproj",
    )(x, mod4, w_in, w_gate_cols)


OP_TM = 512


def _layer_norm(v, g, b):
    mu = jnp.mean(v, axis=-1, keepdims=True)
    d = v - mu
    var = jnp.mean(d * d, axis=-1, keepdims=True)
    return d * lax.rsqrt(var + LN_EPS) * g + b


def _route(scores, b_router):
    sel = scores + b_router
    s = [sel[e:e + 1, :] for e in range(N_EXPERTS)]
    in_top2 = []
    for g in range(N_EXPERT_GROUPS):
        for i in range(EXPERTS_PER_GROUP):
            e = g * EXPERTS_PER_GROUP + i
            cnt = jnp.zeros_like(s[e])
            for jj in range(EXPERTS_PER_GROUP):
                if jj == i:
                    continue
                o = g * EXPERTS_PER_GROUP + jj
                beats = (s[o] > s[e]) if jj > i else (s[o] >= s[e])
                cnt = cnt + jnp.where(beats, 1.0, 0.0)
            in_top2.append(cnt < 2.0)
    grp = []
    for g in range(N_EXPERT_GROUPS):
        tot = jnp.zeros_like(s[0])
        for i in range(EXPERTS_PER_GROUP):
            e = g * EXPERTS_PER_GROUP + i
            tot = tot + jnp.where(in_top2[e], s[e], 0.0)
        grp.append(tot)
    keep_rows = []
    for g in range(N_EXPERT_GROUPS):
        cnt = jnp.zeros_like(s[0])
        for o in range(N_EXPERT_GROUPS):
            if o == g:
                continue
            beats = (grp[o] > grp[g]) if o > g else (grp[o] >= grp[g])
            cnt = cnt + jnp.where(beats, 1.0, 0.0)
        best = cnt < 1.0
        for i in range(EXPERTS_PER_GROUP):
            e = g * EXPERTS_PER_GROUP + i
            keep_rows.append(jnp.where(best, jnp.where(in_top2[e], 1.0, 0.0), 0.0))
    keep = jnp.concatenate(keep_rows, axis=0)
    picked = scores * keep
    return picked / jnp.sum(picked, axis=0, keepdims=True), keep


def _oproj_kernel(y0_ref, y1_ref, y2_ref, y3_ref, w_ref, x_ref, mod_ref, g_ref, b_ref, wr_ref, br_ref,
                  x1_ref, h2_ref, gate_ref, keep_ref):
    acc = None
    for k, yr in enumerate((y0_ref, y1_ref, y2_ref, y3_ref)):
        part = jnp.dot(yr[...], w_ref[0, k * D_GROUP:(k + 1) * D_GROUP, :], preferred_element_type=F32)
        acc = part if acc is None else acc + part
    m = mod_ref[0, 0]
    gate1 = m[:, 2 * D_MODEL:3 * D_MODEL]
    shift2 = m[:, 3 * D_MODEL:4 * D_MODEL]
    scale2 = m[:, 4 * D_MODEL:5 * D_MODEL]
    x1 = _layer_norm(DEEPNORM_ALPHA * x_ref[...] + gate1 * acc, g_ref[0], b_ref[0])
    x1_ref[...] = x1
    h2 = x1 * (1.0 + scale2) + shift2
    h2_ref[...] = h2.astype(BF16)
    scores_t = _sigmoid(_dot3(wr_ref[...], h2, NT))
    gates_t, keep_t = _route(scores_t, br_ref[...])
    gate_ref[...] = gates_t.T
    keep_ref[...] = keep_t.T


def _oproj(st, ys, x, mod4, w_out_b, ln_g, ln_b, w_router, b_router, l):
    row = lambda i: (i, 0)
    return pl.pallas_call(
        _oproj_kernel,
        grid=(st.rows // OP_TM,),
        in_specs=[pl.BlockSpec((OP_TM, D_GROUP), row)] * 4 + [
            pl.BlockSpec((1, D_MODEL, D_MODEL), lambda i: (l, 0, 0)),
            pl.BlockSpec((OP_TM, D_MODEL), row),
            pl.BlockSpec((1, 1, 1, 6 * D_MODEL), lambda i: (l, st.mod_row(i, OP_TM), 0, 0)),
            pl.BlockSpec((1, 1, D_MODEL), lambda i: (l, 0, 0)),
            pl.BlockSpec((1, 1, D_MODEL), lambda i: (l, 0, 0)),
            pl.BlockSpec((N_EXPERTS, D_MODEL), lambda i: (0, 0)),
            pl.BlockSpec((N_EXPERTS, 1), lambda i: (0, 0))],
        out_specs=[pl.BlockSpec((OP_TM, D_MODEL), row),
                   pl.BlockSpec((OP_TM, D_MODEL), row),
                   pl.BlockSpec((OP_TM, N_EXPERTS), row),
                   pl.BlockSpec((OP_TM, N_EXPERTS), row)],
        out_shape=[jax.ShapeDtypeStruct((st.rows, D_MODEL), F32),
                   jax.ShapeDtypeStruct((st.rows, D_MODEL), BF16),
                   jax.ShapeDtypeStruct((st.rows, N_EXPERTS), F32),
                   jax.ShapeDtypeStruct((st.rows, N_EXPERTS), F32)],
        compiler_params=_cparams("parallel"),
        name="oproj_ln1_router",
    )(*ys, w_out_b, x, mod4, ln_g.reshape(DEPTH, 1, D_MODEL), ln_b.reshape(DEPTH, 1, D_MODEL),
      w_router.T, b_router.reshape(N_EXPERTS, 1))


LN_TM = 512


GS_BLK = 1024
GS_CAP = 320
GS_ALIGN = 2 * SUB
GS_COLS = 512
GS_STEP = 256
GS_VMEM_LIMIT = 60 * 1024 * 1024
GS_RCH = 3 * LANE
GS_ROWS = pl.cdiv(GS_BLK + N_EXPERT_GROUPS * GS_ALIGN, GS_RCH) * GS_RCH


def _moe_sorted_kernel(h_ref, gate_ref, keep_ref, wg_ref, wu_ref, wd_ref, f_ref, xs_scr, gs_scr, pt_scr, acc_scr,
                       seg_smem):
    e = pl.program_id(1)

    @pl.when(e == 0)
    def _():
        keep = keep_ref[...]
        lane = _iota(keep.shape, 1)
        member = []
        for g in range(N_EXPERT_GROUPS):
            in_g = (lane >= g * EXPERTS_PER_GROUP) & (lane < (g + 1) * EXPERTS_PER_GROUP)
            member.append(jnp.minimum(jnp.sum(jnp.where(in_g, keep, 0.0), axis=-1, keepdims=True), 1.0))
        lane4 = _iota((GS_BLK, LANE), 1)
        onehot = jnp.zeros((GS_BLK, LANE), F32)
        for g in range(N_EXPERT_GROUPS):
            onehot = onehot + jnp.where(lane4 == g, member[g], 0.0)
        counts = jnp.sum(onehot, axis=0, keepdims=True)
        starts = []
        start = jnp.zeros((1, 1), F32)
        for g in range(N_EXPERT_GROUPS):
            cnt_g = counts[:, g:g + 1]
            starts.append(start)
            seg_smem[2 * g] = jnp.sum(start).astype(jnp.int32)
            seg_smem[2 * g + 1] = jnp.sum(cnt_g).astype(jnp.int32)
            start = start + jnp.ceil(cnt_g * (1.0 / GS_ALIGN)) * GS_ALIGN
        onehot_b = onehot.astype(BF16)
        for r0 in range(0, GS_BLK, GS_STEP):
            tri = jnp.where(_iota((GS_STEP, GS_BLK), 1) <= _iota((GS_STEP, GS_BLK), 0) + r0, 1.0, 0.0).astype(BF16)
            csum = jnp.dot(tri, onehot_b, preferred_element_type=F32)
            dest = jnp.zeros((GS_STEP, 1), F32)
            for g in range(N_EXPERT_GROUPS):
                dest = dest + member[g][r0:r0 + GS_STEP] * (starts[g] + csum[:, g:g + 1] - 1.0)
            pt_scr[r0:r0 + GS_STEP, :] = jnp.where(
                _iota((GS_STEP, GS_ROWS), 1) == dest.astype(jnp.int32), 1.0, 0.0).astype(BF16)
        for c0 in range(0, GS_ROWS, GS_RCH):
            pt_c = pt_scr[:, c0:c0 + GS_RCH]
            xs_scr[c0:c0 + GS_RCH, :] = _dg(pt_c, h_ref[...], TN).astype(BF16)
            gs_scr[c0:c0 + GS_RCH, :] = _dot_sel_l(pt_c, gate_ref[...], TN)
        acc_scr[...] = jnp.zeros_like(acc_scr)

    g = e // EXPERTS_PER_GROUP
    seg_start = seg_smem[2 * g]
    seg_len = seg_smem[2 * g + 1]
    wg = wg_ref[0, 0]
    wu = wu_ref[0, 0]
    wd = wd_ref[0, 0]

    def window(c, carry):
        first = seg_start + c * GS_CAP
        start = jnp.minimum(first, GS_ROWS - GS_CAP)
        rows = pl.ds(pl.multiple_of(start, GS_ALIGN), GS_CAP)
        x = xs_scr[rows, :]
        a = jnp.dot(x, wg, preferred_element_type=F32)
        u = jnp.dot(x, wu, preferred_element_type=F32)
        gates = gs_scr[rows, :]
        ge = jnp.sum(jnp.where(_iota(gates.shape, 1) == e, gates, 0.0), axis=-1, keepdims=True)
        ge = jnp.where(_iota((GS_CAP, 1), 0) + start >= first, ge, 0.0)
        hid = (_silu(a) * u * ge).astype(BF16)
        acc_scr[rows, :] += jnp.dot(hid, wd, preferred_element_type=F32)
        return carry

    lax.fori_loop(0, (seg_len + GS_CAP - 1) // GS_CAP, window, 0)

    @pl.when(e == N_EXPERTS - 1)
    def _():
        pt = pt_scr[...]
        for c0 in range(0, D_MODEL, GS_COLS):
            acc_b = acc_scr[:, c0:c0 + GS_COLS].astype(BF16)
            f_ref[:, c0:c0 + GS_COLS] = jnp.dot(pt, acc_b, preferred_element_type=F32).astype(BF16)


def _moe_sorted(st, h2, gates, keep, wg_b, wu_b, wd_b, l):
    row = lambda i, e: (i, 0)
    wspec = lambda a, b: pl.BlockSpec((1, 1, a, b), lambda i, e: (l, e, 0, 0))
    return pl.pallas_call(
        _moe_sorted_kernel,
        grid=(st.rows // GS_BLK, N_EXPERTS),
        in_specs=[pl.BlockSpec((GS_BLK, D_MODEL), row, pipeline_mode=pl.Buffered(1)),
                  pl.BlockSpec((GS_BLK, N_EXPERTS), row), pl.BlockSpec((GS_BLK, N_EXPERTS), row),
                  wspec(D_MODEL, D_EXPERT), wspec(D_MODEL, D_EXPERT), wspec(D_EXPERT, D_MODEL)],
        out_specs=pl.BlockSpec((GS_BLK, D_MODEL), row),
        out_shape=jax.ShapeDtypeStruct((st.rows, D_MODEL), BF16),
        scratch_shapes=[pltpu.VMEM((GS_ROWS, D_MODEL), BF16), pltpu.VMEM((GS_ROWS, N_EXPERTS), F32),
                        pltpu.VMEM((GS_BLK, GS_ROWS), BF16), pltpu.VMEM((GS_ROWS, D_MODEL), F32),
                        pltpu.SMEM((2 * N_EXPERT_GROUPS,), jnp.int32)],
        compiler_params=pltpu.CompilerParams(dimension_semantics=("parallel", "arbitrary"),
                                             vmem_limit_bytes=GS_VMEM_LIMIT),
        name="moe_group_sorted",
    )(h2, gates, keep, wg_b, wu_b, wd_b)


def _ln2f_kernel(x_ref, f_ref, mod_ref, g_ref, b_ref, o_ref):
    gate2 = mod_ref[0, 0][:, 5 * D_MODEL:6 * D_MODEL]
    o_ref[...] = _layer_norm(DEEPNORM_ALPHA * x_ref[...] + gate2 * f_ref[...].astype(F32), g_ref[0], b_ref[0])


def _ln2f(st, x1, f, mod4, ln_g, ln_b, l):
    row = lambda i: (i, 0)
    return pl.pallas_call(
        _ln2f_kernel,
        grid=(st.rows // LN_TM,),
        in_specs=[pl.BlockSpec((LN_TM, D_MODEL), row),
                  pl.BlockSpec((LN_TM, D_MODEL), row),
                  pl.BlockSpec((1, 1, 1, 6 * D_MODEL), lambda i: (l, st.mod_row(i, LN_TM), 0, 0)),
                  pl.BlockSpec((1, 1, D_MODEL), lambda i: (l, 0, 0)),
                  pl.BlockSpec((1, 1, D_MODEL), lambda i: (l, 0, 0))],
        out_specs=pl.BlockSpec((LN_TM, D_MODEL), row),
        out_shape=jax.ShapeDtypeStruct((st.rows, D_MODEL), F32),
        compiler_params=_cparams("parallel"),
        name="ln2",
    )(x1, f, mod4, ln_g.reshape(DEPTH, 1, D_MODEL), ln_b.reshape(DEPTH, 1, D_MODEL))


S5_TC = 128
S5_SEQS = SUB
S5_Q = 4
S5_QS = S5_GROUPS // S5_Q * S5_STATE
N_S5 = S5_GROUPS * S5_STATE


def _s5_prep_kernel(lr_ref, li_ref, ls_ref, br_ref, bi_ref, e_ref, lbr_ref, lbi_ref, bbr_ref, bbi_ref):
    lr = lr_ref[0]
    li = li_ref[0]
    dt = jnp.exp(ls_ref[0])
    mag = jnp.exp(lr * dt)
    ang = li * dt
    ar = mag * jnp.cos(ang)
    ai = mag * jnp.sin(ang)
    lbr_ref[0] = ar
    lbi_ref[0] = ai
    den = lr * lr + li * li
    nr = ar - 1.0
    cr = (nr * lr + ai * li) / den
    ci = (ai * lr - nr * li) / den
    cr = _dot_sel_l(e_ref[...], cr)
    ci = _dot_sel_l(e_ref[...], ci)
    bre = br_ref[0]
    bim = bi_ref[0]
    bbr_ref[0] = cr * bre - ci * bim
    bbi_ref[0] = cr * bim + ci * bre


def _s5_prep(lam_re, lam_im, log_step, b_re, b_im):
    d2 = DEPTH * 2
    g, p, h = S5_GROUPS, S5_STATE, S5_CH
    bt_re = jnp.swapaxes(b_re, -1, -2).reshape(d2, g * h, p)
    bt_im = jnp.swapaxes(b_im, -1, -2).reshape(d2, g * h, p)
    expand = jnp.asarray(np.kron(np.eye(g, dtype=np.float32), np.ones((h, 1), np.float32)))
    spec_gp = pl.BlockSpec((1, g, p), lambda i: (i, 0, 0))
    spec_b = pl.BlockSpec((1, g * h, p), lambda i: (i, 0, 0))
    lbr, lbi, bbr, bbi = pl.pallas_call(
        _s5_prep_kernel,
        grid=(d2,),
        in_specs=[spec_gp, spec_gp, pl.BlockSpec((1, g, 1), lambda i: (i, 0, 0)), spec_b, spec_b,
                  pl.BlockSpec((g * h, g), lambda i: (0, 0))],
        out_specs=[spec_gp, spec_gp, spec_b, spec_b],
        out_shape=[jax.ShapeDtypeStruct((d2, g, p), F32)] * 2 + [jax.ShapeDtypeStruct((d2, g * h, p), F32)] * 2,
        compiler_params=_cparams("parallel"),
        name="s5_prep",
    )(lam_re.reshape(d2, g, p), lam_im.reshape(d2, g, p), log_step.reshape(d2, g, 1), bt_re, bt_im, expand)
    return lbr, lbi, bbr, bbi


def _s5_block_params(lbr, lbi, bbr, bbi, c_re, c_im):
    d2 = DEPTH * 2
    gq = S5_GROUPS // S5_Q
    eye = jnp.eye(gq, dtype=F32)

    def b_blocks(b):
        b = b.reshape(d2, S5_Q, gq, S5_CH, S5_STATE)
        return jnp.einsum('djghp,gk->djghkp', b, eye).reshape(d2, S5_Q, gq * S5_CH, gq * S5_STATE)

    def c_blocks(c):
        c = c.reshape(d2, S5_Q, gq, S5_CH, S5_STATE)
        return jnp.einsum('djghp,gk->djgpkh', c, eye).reshape(d2, S5_Q, gq * S5_STATE, gq * S5_CH)

    bq = jnp.concatenate([b_blocks(bbr), b_blocks(bbi)], axis=-1)
    cq = jnp.concatenate([c_blocks(c_re), -c_blocks(c_im)], axis=-2)
    return bq, cq, lbr.reshape(d2, 1, N_S5), lbi.reshape(d2, 1, N_S5)


def _s5_scan_kernel(u_ref, bq_ref, cq_ref, lr_ref, li_ref, s0r_ref, s0i_ref, y_ref, sfr_ref, sfi_ref,
                    utb, bur, bui, ytb, sr_scr, si_scr):
    d = pl.program_id(1)
    c = pl.program_id(2)
    nrow = S5_TC * S5_SEQS
    cw = D_GROUP // S5_Q

    @pl.when(c == 0)
    def _():
        sr_scr[...] = s0r_ref[0, 0]
        si_scr[...] = s0i_ref[0, 0]

    for b in range(S5_SEQS):
        for j in range(S5_Q):
            utb[j, pl.ds(b, S5_TC, stride=S5_SEQS), :] = u_ref[b, :, j * cw:(j + 1) * cw]
    for j in range(S5_Q):
        bu = _dot1(utb[j], bq_ref[0, j])
        bur[:, j * S5_QS:(j + 1) * S5_QS] = bu[:, :S5_QS]
        bui[:, j * S5_QS:(j + 1) * S5_QS] = bu[:, S5_QS:]
    for j in range(S5_Q):
        sl = slice(j * S5_QS, (j + 1) * S5_QS)
        lam_r = jnp.broadcast_to(lr_ref[0][:, sl], (S5_SEQS, S5_QS))
        lam_i = jnp.broadcast_to(li_ref[0][:, sl], (S5_SEQS, S5_QS))

        def step(t, carry, sl=sl, lam_r=lam_r, lam_i=lam_i):
            sr, si = carry
            te = jnp.where(d == 0, t, S5_TC - 1 - t)
            rows = pl.ds(pl.multiple_of(te * S5_SEQS, S5_SEQS), S5_SEQS)
            nr = lam_r * sr - lam_i * si + bur[rows, sl]
            ni = lam_r * si + lam_i * sr + bui[rows, sl]
            bur[rows, sl] = nr
            bui[rows, sl] = ni
            return nr, ni

        sr, si = lax.fori_loop(0, S5_TC, step, (sr_scr[:, sl], si_scr[:, sl]))
        sr_scr[:, sl] = sr
        si_scr[:, sl] = si
    for j in range(S5_Q):
        sl = slice(j * S5_QS, (j + 1) * S5_QS)
        yj = _dot1(bur[:, sl], cq_ref[0, j, :S5_QS, :]) + _dot1(bui[:, sl], cq_ref[0, j, S5_QS:, :])
        ytb[j] = yj
    for b in range(S5_SEQS):
        for j in range(S5_Q):
            y_ref[0, b, :, j * cw:(j + 1) * cw] = ytb[j, pl.ds(b, S5_TC, stride=S5_SEQS), :]

    @pl.when(c == pl.num_programs(2) - 1)
    def _():
        sfr_ref[0, 0] = sr_scr[...]
        sfi_ref[0, 0] = si_scr[...]


def _s5_scan(st, z3, bq, cq, lamr, lami, s0r, s0i, l):
    ng, nc = st.nb // S5_SEQS, st.L // S5_TC
    nrow = S5_TC * S5_SEQS
    chunk = lambda d, c: c + d * (nc - 1 - 2 * c)
    par = lambda g, d, c: (2 * l + d, 0, 0, 0)
    st_spec = pl.BlockSpec((1, 1, S5_SEQS, N_S5), lambda g, d, c: (g, d, 0, 0))
    return pl.pallas_call(
        _s5_scan_kernel,
        grid=(ng, 2, nc),
        in_specs=[pl.BlockSpec((S5_SEQS, S5_TC, D_GROUP), lambda g, d, c: (g, chunk(d, c), 0)),
                  pl.BlockSpec((1, S5_Q, D_GROUP // S5_Q, 2 * S5_QS), par),
                  pl.BlockSpec((1, S5_Q, 2 * S5_QS, D_GROUP // S5_Q), par),
                  pl.BlockSpec((1, 1, N_S5), lambda g, d, c: (2 * l + d, 0, 0)),
                  pl.BlockSpec((1, 1, N_S5), lambda g, d, c: (2 * l + d, 0, 0)),
                  st_spec, st_spec],
        out_specs=[pl.BlockSpec((1, S5_SEQS, S5_TC, D_GROUP), lambda g, d, c: (d, g, chunk(d, c), 0)),
                   st_spec, st_spec],
        out_shape=[jax.ShapeDtypeStruct((2, st.nb, st.L, D_GROUP), F32),
                   jax.ShapeDtypeStruct((ng, 2, S5_SEQS, N_S5), F32),
                   jax.ShapeDtypeStruct((ng, 2, S5_SEQS, N_S5), F32)],
        scratch_shapes=[pltpu.VMEM((S5_Q, nrow, LANE), F32), pltpu.VMEM((nrow, N_S5), F32),
                        pltpu.VMEM((nrow, N_S5), F32), pltpu.VMEM((S5_Q, nrow, LANE), F32),
                        pltpu.VMEM((S5_SEQS, N_S5), F32), pltpu.VMEM((S5_SEQS, N_S5), F32)],
        compiler_params=_cparams("parallel", "arbitrary", "arbitrary"),
        name="s5_scan",
    )(z3, bq, cq, lamr, lami, s0r, s0i)


S5_TM = 512


def _gelu_tanh(x):
    return 0.5 * x * (1.0 + jnp.tanh(math.sqrt(2.0 / math.pi) * (x + 0.044715 * (x * x * x))))


def _s5_out_kernel(u_ref, yf_ref, yb_ref, d_ref, w_ref, o_ref):
    y = yf_ref[0] + yb_ref[0] + d_ref[0] * u_ref[...]
    y = _gelu_tanh(y)
    o_ref[...] = (y * _sigmoid(_dot1(y, w_ref[0]))).astype(BF16)


def _s5_out(st, z, ydir, d_skip, w_glu, l):
    return pl.pallas_call(
        _s5_out_kernel,
        grid=(st.rows // S5_TM,),
        in_specs=[pl.BlockSpec((S5_TM, D_GROUP), lambda i: (i, 0)),
                  pl.BlockSpec((1, S5_TM, D_GROUP), lambda i: (0, i, 0)),
                  pl.BlockSpec((1, S5_TM, D_GROUP), lambda i: (1, i, 0)),
                  pl.BlockSpec((1, 1, D_GROUP), lambda i: (l, 0, 0)),
                  pl.BlockSpec((1, D_GROUP, D_GROUP), lambda i: (l, 0, 0))],
        out_specs=pl.BlockSpec((S5_TM, D_GROUP), lambda i: (i, 0)),
        out_shape=jax.ShapeDtypeStruct((st.rows, D_GROUP), BF16),
        compiler_params=_cparams("parallel"),
        name="s5_out",
    )(z, ydir, ydir, d_skip.reshape(DEPTH, 1, D_GROUP), w_glu)


NA_SCALE = NA_HEAD ** -0.5
NA_ROWS = DEC_SEQ // GRID_W
NA_KH = min(NA_WIN_H, NA_ROWS)
NA_QCOL = NA_OFF // LANE
NA_KCOL = (NA_OFF + D_GROUP) // LANE
NA_VCOL = (NA_OFF + 2 * D_GROUP) // LANE
NA_NDR = 2 * NA_WIN_H - 1
NA_AHEAD = 1


def _na_ctx_kernel(q_ref, k_ref, v_ref, o_ref, nk_ref, nv_ref):
    for hh in range(2):
        sl = slice(hh * NA_HEAD, (hh + 1) * NA_HEAD)
        q = q_ref[0, :, sl]
        k = k_ref[0, :, sl]
        v = v_ref[0, :, sl]
        nk_ref[0, hh] = k
        nv_ref[0, hh] = v
        s = _dot1(q, k, NT) * NA_SCALE
        e = jnp.exp(s - jnp.max(s, axis=-1, keepdims=True))
        o = _dot1(e, v) / jnp.sum(e, axis=-1, keepdims=True)
        o_ref[0, :, sl] = o.astype(BF16)


def _na_ctx(st, z3):
    blk = lambda col: pl.BlockSpec((1, st.L, LANE), lambda b, p: (b, 0, col + p))
    kv_spec = pl.BlockSpec((1, 2, st.L, NA_HEAD), lambda b, p: (b, p, 0, 0))
    kv_shape = jax.ShapeDtypeStruct((st.nb, NA_HEADS, st.L, NA_HEAD), F32)
    return pl.pallas_call(
        _na_ctx_kernel,
        grid=(st.nb, NA_HEADS // 2),
        in_specs=[blk(NA_QCOL), blk(NA_KCOL), blk(NA_VCOL)],
        out_specs=[pl.BlockSpec((1, st.L, LANE), lambda b, p: (b, 0, p)), kv_spec, kv_spec],
        out_shape=[jax.ShapeDtypeStruct((st.nb, st.L, D_GROUP), BF16), kv_shape, kv_shape],
        compiler_params=_cparams("parallel", "parallel"),
        name="na_context",
    )(z3, z3, z3)


def _na_nbr_kernel(q_ref, k_ref, v_ref, ck_ref, cv_ref, bias_ref, o_ref):
    nloc = NA_KH * GRID_W
    qc = _iota((GRID_W, nloc), 0)
    kc = _iota((GRID_W, nloc), 1) % GRID_W
    cs = jnp.clip(qc - NA_WIN_W // 2, 0, GRID_W - NA_WIN_W)
    col_in = (kc >= cs) & (kc < cs + NA_WIN_W)

    def scores(hh, r):
        sl = slice(hh * NA_HEAD, (hh + 1) * NA_HEAD)
        rs = min(max(r - NA_KH // 2, 0), NA_ROWS - NA_KH)
        q = q_ref[0, r * GRID_W:(r + 1) * GRID_W, sl]
        k = k_ref[0, rs * GRID_W:rs * GRID_W + nloc, sl]
        off = (rs - r + NA_WIN_H - 1) * GRID_W
        s_loc = _dot1(q, k, NT) * NA_SCALE + bias_ref[hh, :, off:off + nloc]
        s_loc = jnp.where(col_in, s_loc, NEG_INF)
        s_ctx = _dot1(q, ck_ref[0, 0, hh], NT) * NA_SCALE
        return s_loc, s_ctx

    def finish(hh, r, s_loc, s_ctx):
        sl = slice(hh * NA_HEAD, (hh + 1) * NA_HEAD)
        rs = min(max(r - NA_KH // 2, 0), NA_ROWS - NA_KH)
        v = v_ref[0, rs * GRID_W:rs * GRID_W + nloc, sl]
        m = jnp.maximum(jnp.max(s_loc, axis=-1, keepdims=True), jnp.max(s_ctx, axis=-1, keepdims=True))
        e_loc = jnp.exp(s_loc - m)
        e_ctx = jnp.exp(s_ctx - m)
        den = jnp.sum(e_loc, axis=-1, keepdims=True) + jnp.sum(e_ctx, axis=-1, keepdims=True)
        o = (_dot1(e_loc, v) + _dot1(e_ctx, cv_ref[0, 0, hh])) / den
        o_ref[0, r * GRID_W:(r + 1) * GRID_W, sl] = o.astype(BF16)

    blocks = [(hh, r) for hh in range(2) for r in range(NA_ROWS)]
    pending = []
    for i, blk in enumerate(blocks):
        pending.append(scores(*blk))
        if i >= NA_AHEAD:
            finish(*blocks[i - NA_AHEAD], *pending.pop(0))
    for j in range(len(blocks) - NA_AHEAD, len(blocks)):
        finish(*blocks[j], *pending.pop(0))


def _na_bias_table(rpb):
    qc = np.arange(GRID_W)[:, None]
    kc = np.arange(GRID_W)[None, :]
    dc = np.clip(kc - qc, -(NA_WIN_W - 1), NA_WIN_W - 1) + (NA_WIN_W - 1)
    t = rpb[:, :, :, dc]
    return jnp.transpose(t, (0, 1, 3, 2, 4)).reshape(DEPTH, NA_HEADS, GRID_W, NA_NDR * GRID_W)


def _na_nbr(st, z3, cache_k, cache_v, bias_tab, l):
    blk = lambda col: pl.BlockSpec((1, st.L, LANE), lambda b, p: (b, 0, col + p))
    cspec = pl.BlockSpec((1, 1, 2, PAST_LEN, NA_HEAD), lambda b, p: (b, l, p, 0, 0))
    return pl.pallas_call(
        _na_nbr_kernel,
        grid=(st.nb, NA_HEADS // 2),
        in_specs=[blk(NA_QCOL), blk(NA_KCOL), blk(NA_VCOL), cspec, cspec,
                  pl.BlockSpec((None, 2, GRID_W, NA_NDR * GRID_W), lambda b, p: (l, p, 0, 0))],
        out_specs=pl.BlockSpec((1, st.L, LANE), lambda b, p: (b, 0, p)),
        out_shape=jax.ShapeDtypeStruct((st.nb, st.L, D_GROUP), BF16),
        compiler_params=_cparams("parallel", "parallel"),
        name="na_neighbourhood",
    )(z3, z3, z3, cache_k, cache_v, bias_tab)


ML_COL = ML_OFF // LANE
ML_SCALE = ML_HEAD ** -0.5


def _log_sigmoid(x):
    return -_softplus(-x)


def _rope_tables(L):
    half = ML_HEAD // 2
    quarter = half // 2
    t = np.arange(L)
    inv_freq = ROPE_BASE ** (-np.arange(quarter, dtype=np.float32) / quarter)

    def tabs(pos):
        ang = pos.astype(np.float32)[:, None] * inv_freq[None, :].astype(np.float32)
        c, s = np.cos(ang), np.sin(ang)
        return np.concatenate([c, c], axis=-1), np.concatenate([-s, s], axis=-1)

    c1, s1 = tabs(t // GRID_W)
    c2, s2 = tabs(t % GRID_W)
    return (jnp.asarray(np.concatenate([c1, c2], axis=-1), F32),
            jnp.asarray(np.concatenate([s1, s2], axis=-1), F32))


def _ml_kernel(*refs, L, rotary, layer):
    nh = ML_HEADS
    ib_ref, fb_ref = refs[0], refs[1]
    q_refs, k_refs, v_refs, o_refs = (refs[2 + i * nh:2 + (i + 1) * nh] for i in range(4))
    (g_ref, cos_ref, sin_ref, c0_ref, n0_ref, m0_ref, lng_ref, lnb_ref,
     y_ref, cf_ref, nf_ref, mf_ref, qs, ks, hf, hb, c_scr, n_scr, m_scr) = refs[2 + 4 * nh:]
    T = ML_CHUNK
    nc = L // T

    if rotary:
        first = (_iota((L, ML_HEAD), 1) % (ML_HEAD // 2)) < ML_HEAD // 4

        def rope(x):
            quarter = ML_HEAD // 4
            partner = jnp.where(first, pltpu.roll(x, ML_HEAD - quarter, axis=1), pltpu.roll(x, quarter, axis=1))
            return x * cos_ref[...] + partner * sin_ref[...]
    else:
        rope = lambda x: x
    for h in range(nh):
        qs[h] = rope(q_refs[h][0]) * ML_SCALE
        ks[h] = rope(k_refs[h][0])
        for d in range(2):
            c_scr[d, h] = c0_ref[0, d, h]
            n_scr[d, h] = n0_ref[0, h, d:d + 1, :]
            m_scr[d, h] = m0_ref[0, h, d:d + 1, :]

    tj = _iota((T, T), 0)
    ts = _iota((T, T), 1)
    ones = jnp.ones((T, T), F32)
    upto_row = (tj <= ts, tj >= ts)
    upto_col = (ts <= tj, ts >= tj)

    def body(ci, carry):
        ch = []
        for d in range(2):
            cd = ci if d == 0 else nc - 1 - ci
            rows = pl.ds(pl.multiple_of(cd * T, T), T)
            for h in range(nh):
                g = g_ref[0, h, cd]
                ig = g[:, d:d + 1] + ib_ref[layer * 2 * nh + d * nh + h]
                fg = g[:, 2 + d:3 + d] + fb_ref[layer * 2 * nh + d * nh + h]
                lf = jnp.broadcast_to(_log_sigmoid(fg), (T, T))
                igb = jnp.broadcast_to(ig, (T, T))
                q = dict(d=d, h=h, rows=rows, ig=ig, qc=qs[h, rows, :], kc=ks[h, rows, :], vc=v_refs[h][0, rows, :])
                q['bcol'] = _dot_sel_l(jnp.where(upto_col[d], 1.0, 0.0), lf)
                q['brow_i'] = _dot_sel_l(ones, jnp.where(upto_row[d], lf, 0.0) - jnp.where(tj == ts, igb, 0.0))
                ch.append(q)
        for q in ch:
            q['qk'] = _dot1(q['qc'], q['kc'], NT)
            q['c_old'] = c_scr[q['d'], q['h']]
            q['qc_c'] = _dot1(q['qc'], q['c_old'])
        for q in ch:
            d, h = q['d'], q['h']
            dmat = jnp.where(upto_col[d], q['bcol'] - q['brow_i'], -jnp.inf)
            b1 = q['bcol'][:, 0:1]
            m_prev = m_scr[d, h]
            inter = b1 + m_prev
            m_t = jnp.maximum(inter, jnp.max(dmat, axis=-1, keepdims=True))
            qk = q['qk'] * jnp.exp(dmat - m_t)
            w_inter = jnp.exp(inter - m_t)
            n_old = n_scr[d, h]
            den = w_inter * jnp.sum(q['qc'] * n_old, axis=-1, keepdims=True) + jnp.sum(qk, axis=-1, keepdims=True)
            q['scale'] = 1.0 / jnp.maximum(jnp.abs(den), jnp.exp(-m_t))
            q['inter_part'] = w_inter * q['qc_c']
            q['intra'] = _dot1(qk, q['vc'])
            b_last = b1[T - 1:T, :] if d == 0 else b1[0:1, :]
            g_s = b_last - b1 + q['ig']
            m_new = jnp.maximum(b_last + m_prev, jnp.max(g_s, axis=0, keepdims=True))
            w_old = jnp.exp(b_last + m_prev - m_new)
            w_s = jnp.exp(g_s - m_new)
            q['c_new'] = _dot1(q['kc'], w_s * q['vc'], TN)
            q['w_old'] = w_old
            n_scr[d, h] = w_old * n_old + jnp.sum(w_s * q['kc'], axis=0, keepdims=True)
            m_scr[d, h] = m_new
        for q in ch:
            d, h = q['d'], q['h']
            hcur = (q['inter_part'] + q['intra']) * q['scale']
            if d == 0:
                hf[h, q['rows'], :] = hcur
            else:
                hb[h, q['rows'], :] = hcur
            c_scr[d, h] = q['w_old'] * q['c_old'] + q['c_new']
        return carry

    lax.fori_loop(0, nc, body, 0)

    for h in range(nh):
        hsum = hf[h] + hb[h]
        mu = jnp.mean(hsum, axis=-1, keepdims=True)
        dv = hsum - mu
        var = jnp.mean(dv * dv, axis=-1, keepdims=True)
        hn = dv * lax.rsqrt(var + ML_GN_EPS)
        y_ref[0, :, h * ML_HEAD:(h + 1) * ML_HEAD] = (
            _sigmoid(o_refs[h][0]) * (hn * lng_ref[0, :, h * ML_HEAD:(h + 1) * ML_HEAD]
                                      + lnb_ref[0, :, h * ML_HEAD:(h + 1) * ML_HEAD])).astype(BF16)
        for d in range(2):
            cf_ref[0, d, h] = c_scr[d, h]
            nf_ref[0, h, d:d + 1, :] = n_scr[d, h]
            mf_ref[0, h, d:d + 1, :] = m_scr[d, h]


def _ml_mixer(st, z3, zg, i_bias, f_bias, ln_g, ln_b, c0, n0, m0, l):
    L, nb, nh = st.L, st.nb, ML_HEADS
    nc = L // ML_CHUNK
    gates = zg.reshape(nb, L, 2, 2, nh).transpose(0, 4, 1, 2, 3).reshape(nb, nh, nc, ML_CHUNK, 4)
    cos_t, sin_t = _rope_tables(L)
    blk = lambda col: pl.BlockSpec((1, L, LANE), lambda b: (b, 0, col))
    head_blocks = [blk(ML_COL + part * nh + h) for part in range(4) for h in range(nh)]
    smem = pl.BlockSpec(memory_space=pltpu.SMEM)
    tab = pl.BlockSpec((L, ML_HEAD), lambda b: (0, 0))
    c_spec = pl.BlockSpec((1, 2, nh, ML_HEAD, ML_HEAD), lambda b: (b, 0, 0, 0, 0))
    n_spec = pl.BlockSpec((1, nh, 2, ML_HEAD), lambda b: (b, 0, 0, 0))
    m_spec = pl.BlockSpec((1, nh, 2, 1), lambda b: (b, 0, 0, 0))
    par = pl.BlockSpec((1, 1, D_GROUP), lambda b: (l, 0, 0))
    return pl.pallas_call(
        functools.partial(_ml_kernel, L=L, rotary=st.latent, layer=l),
        grid=(nb,),
        in_specs=[smem, smem] + head_blocks + [
            pl.BlockSpec((1, nh, nc, ML_CHUNK, 4), lambda b: (b, 0, 0, 0, 0)),
            tab, tab, c_spec, n_spec, m_spec, par, par],
        out_specs=[pl.BlockSpec((1, L, D_GROUP), lambda b: (b, 0, 0)), c_spec, n_spec, m_spec],
        out_shape=[jax.ShapeDtypeStruct((nb, L, D_GROUP), BF16),
                   jax.ShapeDtypeStruct((nb, 2, nh, ML_HEAD, ML_HEAD), F32),
                   jax.ShapeDtypeStruct((nb, nh, 2, ML_HEAD), F32),
                   jax.ShapeDtypeStruct((nb, nh, 2, 1), F32)],
        scratch_shapes=[pltpu.VMEM((nh, L, ML_HEAD), F32), pltpu.VMEM((nh, L, ML_HEAD), F32),
                        pltpu.VMEM((nh, L, ML_HEAD), F32), pltpu.VMEM((nh, L, ML_HEAD), F32),
                        pltpu.VMEM((2, nh, ML_HEAD, ML_HEAD), F32), pltpu.VMEM((2, nh, 1, ML_HEAD), F32),
                        pltpu.VMEM((2, nh, 1, 1), F32)],
        compiler_params=_cparams("parallel"),
        name="mlstm",
    )(i_bias.reshape(-1), f_bias.reshape(-1), *([z3] * (4 * nh)), gates, cos_t, sin_t, c0, n0, m0,
      ln_g.reshape(DEPTH, 1, D_GROUP), ln_b.reshape(DEPTH, 1, D_GROUP))


RW_TL = 256
RW_T = 64
RW_GROUP = 16
RW_RCOL = RW_OFF // D_GROUP
RW_LCOL = (RW_OFF + 3 * D_GROUP) // (2 * LANE)
N_LORA = RW_LORA_W + RW_LORA_A + RW_LORA_G


def _head_ones():
    return jnp.asarray(np.kron(np.eye(RW_HEADS, dtype=np.float32), np.ones((RW_HEAD, RW_HEAD), np.float32)))


def _rw_front_kernel(r_ref, k_ref, v_ref, lo_ref, rp_ref, kp_ref, vp_ref, lp_ref, rn_ref, kn_ref, vn_ref, ln_ref,
                     mup_ref, mun_ref, w0_ref, w2_ref, a0_ref, a2_ref, g2_ref, kk_ref, ka_ref, hones_ref,
                     ro_ref, vo_ref, kko_ref, go_ref, kd_ref, bd_ref, lw_ref):
    t = pl.program_id(1)
    first = t == 0
    last = t == pl.num_programs(1) - 1
    row = _iota((RW_TL, 1), 0)

    def shift(x_ref, p_ref, n_ref, lo, hi):
        x = x_ref[0]
        prev_edge = jnp.where(first, 0.0, p_ref[0, SUB - 1:SUB, :])
        next_edge = jnp.where(last, 0.0, n_ref[0, 0:1, :])
        prev = jnp.where(row == 0, prev_edge, pltpu.roll(x, 1, axis=0))
        nxt = jnp.where(row == RW_TL - 1, next_edge, pltpu.roll(x, RW_TL - 1, axis=0))
        return x + mup_ref[0][:, lo:hi] * (prev - x) + mun_ref[0][:, lo:hi] * (nxt - x)

    g = D_GROUP
    r = shift(r_ref, rp_ref, rn_ref, 0, g)
    k = shift(k_ref, kp_ref, kn_ref, g, 2 * g)
    v = shift(v_ref, vp_ref, vn_ref, 2 * g, 3 * g)
    lo = shift(lo_ref, lp_ref, ln_ref, 3 * g, 3 * g + N_LORA)
    zw = lo[:, :RW_LORA_W]
    za = lo[:, RW_LORA_W:RW_LORA_W + RW_LORA_A]
    zg = lo[:, RW_LORA_W + RW_LORA_A:]
    ro_ref[0] = r
    vo_ref[0] = v
    kk = k * kk_ref[0]
    ssq = _dot_sel_r(kk * kk, hones_ref[...])
    kk = kk * lax.rsqrt(ssq + 1e-12)
    kko_ref[0] = kk
    go_ref[0] = _dot3(_sigmoid(zg), g2_ref[0])
    tw = jnp.tanh(zw)
    for d in range(2):
        w_log = -_softplus(-(w0_ref[0, d:d + 1, :] + _dot3(tw, w2_ref[0, d]))) - 0.5
        lw_ref[d, 0] = -jnp.exp(w_log)
        a = _sigmoid(a0_ref[0, d:d + 1, :] + _dot3(za, a2_ref[0, d]))
        kd_ref[d, 0] = k * (1.0 + (a - 1.0) * ka_ref[0])
        bd_ref[d, 0] = kk * a


def _rw_front(st, z3, mu_prev, mu_next, w0, w2, a0, a2, g2, k_k, k_a, l):
    nb, L = st.nb, st.L
    nt = L // RW_TL
    tpb = RW_TL // SUB
    main = lambda w, col: pl.BlockSpec((1, RW_TL, w), lambda b, t: (b, t, col))
    prev = lambda w, col: pl.BlockSpec((1, SUB, w), lambda b, t: (b, jnp.maximum(t * tpb - 1, 0), col))
    nxt = lambda w, col: pl.BlockSpec((1, SUB, w), lambda b, t: (b, jnp.minimum((t + 1) * tpb, L // SUB - 1), col))
    cols = [(D_GROUP, RW_RCOL), (D_GROUP, RW_RCOL + 1), (D_GROUP, RW_RCOL + 2), (2 * LANE, RW_LCOL)]
    lay = lambda *shape: pl.BlockSpec((1,) + shape, lambda b, t: (l,) + (0,) * len(shape))
    out1 = pl.BlockSpec((1, RW_TL, D_GROUP), lambda b, t: (b, t, 0))
    out2 = pl.BlockSpec((2, 1, RW_TL, D_GROUP), lambda b, t: (0, b, t, 0))
    s1 = jax.ShapeDtypeStruct((nb, L, D_GROUP), F32)
    s2 = jax.ShapeDtypeStruct((2, nb, L, D_GROUP), F32)
    return pl.pallas_call(
        _rw_front_kernel,
        grid=(nb, nt),
        in_specs=[main(*c) for c in cols] + [prev(*c) for c in cols] + [nxt(*c) for c in cols] + [
            lay(1, RW_IN), lay(1, RW_IN), lay(2, D_GROUP), lay(2, RW_LORA_W, D_GROUP), lay(2, D_GROUP),
            lay(2, RW_LORA_A, D_GROUP), lay(RW_LORA_G, D_GROUP), lay(1, D_GROUP), lay(1, D_GROUP),
            pl.BlockSpec((D_GROUP, D_GROUP), lambda b, t: (0, 0))],
        out_specs=[out1, out1, out1, out1, out2, out2, out2],
        out_shape=[s1, s1, s1, s1, s2, s2, s2],
        compiler_params=_cparams("parallel", "parallel"),
        name="rwkv_front",
    )(*([z3] * 12), mu_prev.reshape(DEPTH, 1, RW_IN), mu_next.reshape(DEPTH, 1, RW_IN), w0, w2, a0, a2, g2,
      k_k.reshape(DEPTH, 1, D_GROUP), k_a.reshape(DEPTH, 1, D_GROUP), _head_ones())


def _rw_core_kernel(rf_ref, vf_ref, kkf_ref, kdf_ref, bdf_ref, lwf_ref,
                    rb_ref, vb_ref, kkb_ref, kdb_ref, bdb_ref, lwb_ref, s0_ref,
                    yf_ref, yb_ref, sf_ref, s_scr):
    c = pl.program_id(1)
    T = RW_T

    @pl.when(c == 0)
    def _():
        s_scr[...] = s0_ref[0]

    tj = _iota((T, T), 0)
    ts = _iota((T, T), 1)
    dirs = ((rf_ref, vf_ref, kkf_ref, kdf_ref, bdf_ref, lwf_ref, yf_ref),
            (rb_ref, vb_ref, kkb_ref, kdb_ref, bdb_ref, lwb_ref, yb_ref))
    ch = []
    for d, (r_ref, v_ref, kk_ref, kd_ref, bd_ref, lw_ref, y_ref) in enumerate(dirs):
        if d == 0:
            incl = ts <= tj
            strict = ts < tj
            last = T - 1
        else:
            incl = ts >= tj
            strict = ts > tj
            last = 0
        lw = lw_ref[0, 0]
        cum = _dot_sel_l(jnp.where(incl, 1.0, 0.0), lw)
        w_in = jnp.exp(cum)
        w_inv = jnp.exp(-cum)
        w_ex = jnp.exp(cum - lw)
        kap_a = kk_ref[0] * w_ex
        bet_a = bd_ref[0, 0] * w_inv
        khat_a = kd_ref[0, 0] * w_inv
        rho_a = r_ref[0] * w_in
        w_tot = w_in[last:last + 1, :]
        v_a = v_ref[0]
        for h in range(RW_HEADS):
            sl = slice(h * RW_HEAD, (h + 1) * RW_HEAD)
            ch.append(dict(d=d, h=h, sl=sl, incl=incl, strict=strict, y_ref=y_ref, w_tot=w_tot[:, sl],
                           kap=kap_a[:, sl], bet=bet_a[:, sl], khat=khat_a[:, sl], rho=rho_a[:, sl], v=v_a[:, sl]))
    def solve(chs):
        for q in chs:
            kr = jnp.concatenate([q['kap'], q['rho']], axis=0)
            bk = jnp.concatenate([q['bet'], q['khat']], axis=0)
            q['gram'] = _dot1(kr, bk, NT)
        for q in chs:
            gram = q.pop('gram')
            q['n'] = -jnp.where(q['strict'], gram[:T, :T], 0.0)
            l_k = jnp.where(q['strict'], gram[:T, T:], 0.0)
            q['m_b'] = jnp.where(q['incl'], gram[T:, :T], 0.0)
            m_k = jnp.where(q['incl'], gram[T:, T:], 0.0)
            q['lmv'] = _dot1(jnp.concatenate([l_k, m_k], axis=0), q['v'])
        for q in chs:
            q['x'] = jnp.concatenate([q['kap'], q['lmv'][:T]], axis=1)
        for lvl in range(6):
            mm = _dot3 if lvl < 2 else _dot1
            for q in chs:
                if lvl < 5:
                    nx = mm(q['n'], jnp.concatenate([q['x'], q['n']], axis=1))
                    q['n'], q['x'] = nx[:, 2 * RW_HEAD:], q['x'] + nx[:, :2 * RW_HEAD]
                else:
                    q['x'] = q['x'] + _dot1(q['n'], q['x'])
        for q in chs:
            q['p'] = _dot1(q['m_b'], q['x'])
            xb = _dot1(q['x'], q['bet'], TN)
            q['a_m'] = xb[:RW_HEAD]
            q['d_m'] = _dot1(q['v'], q['khat'], TN) - xb[RW_HEAD:]
        for q in chs:
            rho_p = q['rho'] - q['p'][:, :RW_HEAD]
            y_v = q['lmv'][T:] - q['p'][:, RW_HEAD:]
            s_old = s_scr[q['d'], q['h']]
            q['y_ref'][0, :, q['sl']] = _dot1(rho_p, s_old, NT) + y_v
            s_scr[q['d'], q['h']] = (s_old - _dot1(s_old, q['a_m']) + q['d_m']) * q['w_tot']

    for g0 in range(0, len(ch), RW_GROUP):
        solve(ch[g0:g0 + RW_GROUP])

    @pl.when(c == pl.num_programs(1) - 1)
    def _():
        sf_ref[0] = s_scr[...]


def _rw_core(st, r, v, kk, kd, bd, lw, s0):
    nb, L = st.nb, st.L
    nc = L // RW_T
    f1 = pl.BlockSpec((1, RW_T, D_GROUP), lambda b, c: (b, c, 0))
    b1 = pl.BlockSpec((1, RW_T, D_GROUP), lambda b, c: (b, nc - 1 - c, 0))
    f2 = pl.BlockSpec((1, 1, RW_T, D_GROUP), lambda b, c: (0, b, c, 0))
    b2 = pl.BlockSpec((1, 1, RW_T, D_GROUP), lambda b, c: (1, b, nc - 1 - c, 0))
    s_spec = pl.BlockSpec((1, 2, RW_HEADS, RW_HEAD, RW_HEAD), lambda b, c: (b, 0, 0, 0, 0))
    ys = jax.ShapeDtypeStruct((nb, L, D_GROUP), F32)
    return pl.pallas_call(
        _rw_core_kernel,
        grid=(nb, nc),
        in_specs=[f1, f1, f1, f2, f2, f2, b1, b1, b1, b2, b2, b2, s_spec],
        out_specs=[f1, b1, s_spec],
        out_shape=[ys, ys, jax.ShapeDtypeStruct((nb, 2, RW_HEADS, RW_HEAD, RW_HEAD), F32)],
        scratch_shapes=[pltpu.VMEM((2, RW_HEADS, RW_HEAD, RW_HEAD), F32)],
        compiler_params=_cparams("parallel", "arbitrary"),
        name="rwkv_core",
    )(r, v, kk, kd, bd, lw, r, v, kk, kd, bd, lw, s0)


RWO_TM = 512


def _rw_out_kernel(yf_ref, yb_ref, r_ref, v_ref, kd0_ref, kd1_ref, g_ref, lng_ref, lnb_ref, rk_ref, hones_ref, o_ref):
    y = yf_ref[...] + yb_ref[...]
    inv = 1.0 / RW_HEAD
    mu = _dot_sel_r(y, hones_ref[...]) * inv
    dv = y - mu
    var = _dot_sel_r(dv * dv, hones_ref[...]) * inv
    yn = dv * lax.rsqrt(var + RW_GN_EPS) * lng_ref[0] + lnb_ref[0]
    kmean = 0.5 * (kd0_ref[0] + kd1_ref[0])
    bonus = _dot_sel_r(r_ref[...] * kmean * rk_ref[0], hones_ref[...]) * v_ref[...]
    o_ref[...] = ((yn + bonus) * g_ref[...]).astype(BF16)


def _rw_out(st, yf, yb, r, v, kd, g, ln_g, ln_b, r_k, l):
    rows = st.rows
    flat = lambda a: a.reshape(rows, D_GROUP)
    row = pl.BlockSpec((RWO_TM, D_GROUP), lambda i: (i, 0))
    lay = pl.BlockSpec((1, 1, D_GROUP), lambda i: (l, 0, 0))
    kd2 = kd.reshape(2, rows, D_GROUP)
    return pl.pallas_call(
        _rw_out_kernel,
        grid=(rows // RWO_TM,),
        in_specs=[row, row, row, row,
                  pl.BlockSpec((1, RWO_TM, D_GROUP), lambda i: (0, i, 0)),
                  pl.BlockSpec((1, RWO_TM, D_GROUP), lambda i: (1, i, 0)),
                  row, lay, lay, lay, pl.BlockSpec((D_GROUP, D_GROUP), lambda i: (0, 0))],
        out_specs=row,
        out_shape=jax.ShapeDtypeStruct((rows, D_GROUP), BF16),
        compiler_params=_cparams("parallel"),
        name="rwkv_out",
    )(flat(yf), flat(yb), flat(r), flat(v), kd2, kd2, flat(g), ln_g.reshape(DEPTH, 1, D_GROUP),
      ln_b.reshape(DEPTH, 1, D_GROUP), r_k.reshape(DEPTH, 1, D_GROUP), _head_ones())


def _trunk_layer(st, x, l, mod4, p, states, ctx_kv):
    nb, L = st.nb, st.L
    z, zg = _zproj(st, x, mod4, p['w_in'], p['w_gate_cols'], l)
    z3 = z.reshape(nb, L, D_Z)

    ydir, sfr, sfi = _s5_scan(st, z3, p['s5_bq'], p['s5_cq'], p['s5_lamr'], p['s5_lami'],
                              states['s5_re'], states['s5_im'], l)
    y_s5 = _s5_out(st, z, ydir.reshape(2, st.rows, D_GROUP), p['s5_d'], p['s5_w_glu'], l)

    r, v, kk, g, kd, bd, lw = _rw_front(st, z3, p['rw_mu_prev'], p['rw_mu_next'], p['rw_w0'], p['rw_w2'],
                                        p['rw_a0'], p['rw_a2'], p['rw_g2'], p['rw_k_k'], p['rw_k_a'], l)
    yf, yb, rw_s = _rw_core(st, r, v, kk, kd, bd, lw, states['rw'])
    y_rw = _rw_out(st, yf, yb, r, v, kd, g, p['rw_ln_g'], p['rw_ln_b'], p['rw_r_k'], l)

    if ctx_kv is None:
        y_na, nk, nv = _na_ctx(st, z3)
    else:
        y_na = _na_nbr(st, z3, ctx_kv[0], ctx_kv[1], p['na_bias'], l)
        nk = nv = None

    y_ml, ml_c, ml_n, ml_m = _ml_mixer(st, z3, zg, p['ml_i_bias'], p['ml_f_bias'], p['ml_ln_g'], p['ml_ln_b'],
                                       states['ml_c'], states['ml_n'], states['ml_m'], l)

    ys = (y_s5, y_rw, y_na.reshape(st.rows, D_GROUP), y_ml.reshape(st.rows, D_GROUP))
    tail = _oproj(st, ys, x, mod4, p['w_out'], p['ln1_g'], p['ln1_b'], p['w_router'], p['b_router'], l)
    return tail, (nk, nv, sfr, sfi, rw_s, ml_c, ml_n, ml_m)


def _moe_and_ln2(tails, mod4, p, l):
    xs = []
    for st, (x1, h2, gates, keep) in zip((PROMPT, LATENT), tails):
        f = _moe_sorted(st, h2, gates, keep, p['moe_w_gate'], p['moe_w_up'], p['moe_w_down'], l)
        xs.append(_ln2f(st, x1, f, mod4, p['ln2_g'], p['ln2_b'], l))
    return xs


def kernel(x_prompt, x_sample, cache_nat_k, cache_nat_v, state_s5_re, state_s5_im, state_rwkv, state_mlstm_c, state_mlstm_n, state_mlstm_m, c, c_ctx, w_mod, b_mod, w_in, w_out, s5_lam_re, s5_lam_im, s5_log_step, s5_b_re, s5_b_im, s5_c_re, s5_c_im, s5_d, s5_w_glu, rw_mu_prev, rw_mu_next, rw_w0, rw_w2, rw_a0, rw_a2, rw_g2, rw_k_k, rw_k_a, rw_r_k, rw_ln_g, rw_ln_b, na_rpb, ml_i_bias, ml_f_bias, ml_ln_g, ml_ln_b, ln1_g, ln1_b, ln2_g, ln2_b, w_router, b_router, moe_w_gate, moe_w_up, moe_w_down):
    dt = x_prompt.dtype
    cond = jnp.concatenate([c_ctx[None, :], c, jnp.zeros((MOD_ROWS - 1 - DEC_BATCH, D_MODEL), F32)], axis=0)
    mod4 = _modulation(cond, w_mod, b_mod).reshape(DEPTH, MOD_ROWS, 1, 6 * D_MODEL)

    lbr, lbi, bbr, bbi = _s5_prep(s5_lam_re, s5_lam_im, s5_log_step, s5_b_re, s5_b_im)
    s5_bq, s5_cq, s5_lamr, s5_lami = _s5_block_params(lbr, lbi, bbr, bbi, s5_c_re, s5_c_im)
    w_in_b = w_in.astype(BF16)
    p = dict(w_in=w_in_b, w_gate_cols=w_in_b[:, :, D_Z:], w_out=w_out.astype(BF16),
             s5_bq=s5_bq, s5_cq=s5_cq, s5_lamr=s5_lamr, s5_lami=s5_lami, s5_d=s5_d, s5_w_glu=s5_w_glu,
             rw_mu_prev=rw_mu_prev, rw_mu_next=rw_mu_next, rw_w0=rw_w0, rw_w2=rw_w2, rw_a0=rw_a0, rw_a2=rw_a2,
             rw_g2=rw_g2, rw_k_k=rw_k_k, rw_k_a=rw_k_a, rw_r_k=rw_r_k, rw_ln_g=rw_ln_g, rw_ln_b=rw_ln_b,
             na_bias=_na_bias_table(na_rpb), ml_i_bias=ml_i_bias, ml_f_bias=ml_f_bias, ml_ln_g=ml_ln_g,
             ml_ln_b=ml_ln_b, ln1_g=ln1_g, ln1_b=ln1_b, ln2_g=ln2_g, ln2_b=ln2_b, w_router=w_router,
             b_router=b_router, moe_w_gate=moe_w_gate.astype(BF16), moe_w_up=moe_w_up.astype(BF16),
             moe_w_down=moe_w_down.astype(BF16))

    gp = BATCH // S5_SEQS
    zero_states = dict(
        s5_re=jnp.zeros((gp, 2, S5_SEQS, N_S5), F32), s5_im=jnp.zeros((gp, 2, S5_SEQS, N_S5), F32),
        rw=jnp.zeros((BATCH, 2, RW_HEADS, RW_HEAD, RW_HEAD), F32),
        ml_c=jnp.zeros((BATCH, 2, ML_HEADS, ML_HEAD, ML_HEAD), F32),
        ml_n=jnp.zeros((BATCH, ML_HEADS, 2, ML_HEAD), F32), ml_m=jnp.zeros((BATCH, ML_HEADS, 2, 1), F32))

    xp = x_prompt.reshape(PROMPT.rows, D_MODEL)
    xs = x_sample.reshape(LATENT.rows, D_MODEL)
    outs = [[] for _ in range(8)]
    for l in range(DEPTH):
        tail_p, ctx_t = _trunk_layer(PROMPT, xp, l, mod4, p, zero_states, None)
        for acc, t in zip(outs, ctx_t):
            acc.append(t)
        lat_states = dict(
            s5_re=state_s5_re[:, l].reshape(DEC_BATCH, 2, N_S5).transpose(1, 0, 2)[None],
            s5_im=state_s5_im[:, l].reshape(DEC_BATCH, 2, N_S5).transpose(1, 0, 2)[None],
            rw=state_rwkv[:, l], ml_c=state_mlstm_c[:, l],
            ml_n=state_mlstm_n[:, l].transpose(0, 2, 1, 3), ml_m=state_mlstm_m[:, l].transpose(0, 2, 1)[..., None])
        tail_s, _ = _trunk_layer(LATENT, xs, l, mod4, p, lat_states, (cache_nat_k, cache_nat_v))
        xp, xs = _moe_and_ln2((tail_p, tail_s), mod4, p, l)

    nk, nv, s5r, s5i, rw, mc, mn, mm = [jnp.stack(t, axis=1) for t in outs]

    def s5_state(t):
        return t.transpose(0, 3, 1, 2, 4).reshape(BATCH, DEPTH, 2, S5_GROUPS, S5_STATE)

    return (xp.reshape(BATCH, SEQ, D_MODEL), xs.reshape(DEC_BATCH, DEC_SEQ, D_MODEL),
            nk, nv, s5_state(s5r).astype(dt), s5_state(s5i).astype(dt), rw.astype(dt), mc.astype(dt),
            mn.transpose(0, 1, 3, 2, 4).astype(dt), mm[..., 0].transpose(0, 1, 3, 2).astype(dt))
```

```python
import functools
import math

import numpy as np
import jax
import jax.numpy as jnp
from jax import lax
from jax.experimental import pallas as pl
from jax.experimental.pallas import tpu as pltpu

F32 = jnp.float32
BF16 = jnp.bfloat16

D_MODEL = 2048
BATCH = 16
SEQ = 256
DEPTH = 4
DEC_BATCH = 8
DEC_SEQ = 1024
PAST_LEN = 256
GRID_W = 64
D_GROUP = D_MODEL // 4
S5_CH = 16
S5_GROUPS = D_GROUP // S5_CH
S5_STATE = 64
RW_HEAD = 64
RW_HEADS = D_GROUP // RW_HEAD
RW_LORA_W = 64
RW_LORA_A = 64
RW_LORA_G = 128
RW_GN_EPS = 64e-5
NA_HEAD = 64
NA_HEADS = D_GROUP // NA_HEAD
NA_WIN_H = 8
NA_WIN_W = 16
NEG_INF = -1e30
ML_HEAD = 128
ML_HEADS = D_GROUP // ML_HEAD
ML_CHUNK = 256
ML_GN_EPS = 1e-5
ROPE_BASE = 10000.0
N_EXPERTS = 16
N_EXPERT_GROUPS = 4
EXPERTS_PER_GROUP = N_EXPERTS // N_EXPERT_GROUPS
D_EXPERT = 512
DEEPNORM_ALPHA = (2 * DEPTH) ** 0.25
LN_EPS = 1e-5
S5_IN = D_GROUP
RW_IN = 3 * D_GROUP + RW_LORA_W + RW_LORA_A + RW_LORA_G
NA_IN = 3 * D_GROUP
ML_IN = 4 * D_GROUP + 4 * ML_HEADS
D_IN = S5_IN + RW_IN + NA_IN + ML_IN
N_GATE = 4 * ML_HEADS
D_Z = D_IN - N_GATE
RW_OFF = S5_IN
NA_OFF = S5_IN + RW_IN
ML_OFF = NA_OFF + NA_IN
MOD_ROWS = 16
LANE = 128
SUB = 8

VMEM_LIMIT = 56 * 1024 * 1024


def _cparams(*sem):
    return pltpu.CompilerParams(dimension_semantics=sem, vmem_limit_bytes=VMEM_LIMIT)


def _dg(a, b, dims):
    return lax.dot_general(a, b, (dims, ((), ())), preferred_element_type=F32)


NN = ((1,), (0,))
NT = ((1,), (1,))
TN = ((0,), (0,))


def _dot1(a, b, dims=NN):
    return _dg(a.astype(BF16), b.astype(BF16), dims)


def _split(x):
    hi = x.astype(BF16)
    lo = (x - hi.astype(F32)).astype(BF16)
    return hi, lo


def _split3(x):
    hi = x.astype(BF16)
    r = x - hi.astype(F32)
    mid = r.astype(BF16)
    lo = (r - mid.astype(F32)).astype(BF16)
    return hi, mid, lo


def _dot3(a, b, dims=NN):
    ah, al = _split(a)
    bh, bl = _split(b)
    return _dg(ah, bh, dims) + (_dg(ah, bl, dims) + _dg(al, bh, dims))


def _dot_sel_l(sel, x, dims=NN):
    s = sel.astype(BF16)
    hi, mid, lo = _split3(x)
    return _dg(s, hi, dims) + (_dg(s, mid, dims) + _dg(s, lo, dims))


def _dot_sel_r(x, sel, dims=NN):
    s = sel.astype(BF16)
    hi, mid, lo = _split3(x)
    return _dg(hi, s, dims) + (_dg(mid, s, dims) + _dg(lo, s, dims))


def _sigmoid(x):
    return 1.0 / (1.0 + jnp.exp(-x))


def _silu(x):
    return x * _sigmoid(x)


def _softplus(x):
    return jnp.maximum(x, 0.0) + jnp.log(1.0 + jnp.exp(-jnp.abs(x)))


def _iota(shape, dim):
    return lax.broadcasted_iota(jnp.int32, shape, dim)


class _Stream:
    def __init__(self, nb, L, latent):
        self.nb, self.L, self.latent = nb, L, latent
        self.rows = nb * L

    def mod_row(self, tile, tile_rows):
        if not self.latent:
            return 0
        return 1 + (tile * tile_rows) // self.L


PROMPT = _Stream(BATCH, SEQ, False)
LATENT = _Stream(DEC_BATCH, DEC_SEQ, True)


MOD_TN = 768


def _mod_kernel(cond_ref, w_ref, b_ref, o_ref):
    c = _silu(cond_ref[...])
    o_ref[0] = _dot3(c, w_ref[0]) + b_ref[0]


def _modulation(cond, w_mod, b_mod):
    n = 6 * D_MODEL
    return pl.pallas_call(
        _mod_kernel,
        grid=(DEPTH, n // MOD_TN),
        in_specs=[pl.BlockSpec((MOD_ROWS, D_MODEL), lambda l, j: (0, 0)),
                  pl.BlockSpec((1, D_MODEL, MOD_TN), lambda l, j: (l, 0, j)),
                  pl.BlockSpec((1, 1, MOD_TN), lambda l, j: (l, 0, j))],
        out_specs=pl.BlockSpec((1, MOD_ROWS, MOD_TN), lambda l, j: (l, 0, j)),
        out_shape=jax.ShapeDtypeStruct((DEPTH, MOD_ROWS, n), F32),
        compiler_params=_cparams("parallel", "parallel"),
        name="modulation",
    )(cond, w_mod, b_mod.reshape(DEPTH, 1, n))


ZP_TM = 1024
ZP_TN = 256


def _zproj_kernel(x_ref, mod_ref, w_ref, wg_ref, z_ref, zg_ref, h_scr):
    j = pl.program_id(1)

    @pl.when(j == 0)
    def _():
        m = mod_ref[0, 0]
        shift1 = m[:, 0:D_MODEL]
        scale1 = m[:, D_MODEL:2 * D_MODEL]
        h = (x_ref[...] * (1.0 + scale1) + shift1).astype(BF16)
        h_scr[...] = h
        zg_ref[...] = jnp.dot(h, wg_ref[0], preferred_element_type=F32)

    z_ref[...] = jnp.dot(h_scr[...], w_ref[0], preferred_element_type=F32)


def _zproj(st, x, mod4, w_in, w_gate_cols, l):
    return pl.pallas_call(
        _zproj_kernel,
        grid=(st.rows // ZP_TM, D_Z // ZP_TN),
        in_specs=[pl.BlockSpec((ZP_TM, D_MODEL), lambda i, j: (i, 0)),
                  pl.BlockSpec((1, 1, 1, 6 * D_MODEL), lambda i, j: (l, st.mod_row(i, ZP_TM), 0, 0)),
                  pl.BlockSpec((1, D_MODEL, ZP_TN), lambda i, j: (l, 0, j)),
                  pl.BlockSpec((1, D_MODEL, N_GATE), lambda i, j: (l, 0, 0))],
        out_specs=[pl.BlockSpec((ZP_TM, ZP_TN), lambda i, j: (i, j)),
                   pl.BlockSpec((ZP_TM, N_GATE), lambda i, j: (i, 0))],
        out_shape=[jax.ShapeDtypeStruct((st.rows, D_Z), F32),
                   jax.ShapeDtypeStruct((st.rows, N_GATE), F32)],
        scratch_shapes=[pltpu.VMEM((ZP_TM, D_MODEL), BF16)],
        compiler_params=_cparams("parallel", "arbitrary"),
        name="zproj",
    )(x, mod4, w_in, w_gate_cols)


OP_TM = 512


def _layer_norm(v, g, b):
    mu = jnp.mean(v, axis=-1, keepdims=True)
    d = v - mu
    var = jnp.mean(d * d, axis=-1, keepdims=True)
    return d * lax.rsqrt(var + LN_EPS) * g + b


def _route(scores, b_router):
    sel = scores + b_router
    s = [sel[e:e + 1, :] for e in range(N_EXPERTS)]
    in_top2 = []
    for g in range(N_EXPERT_GROUPS):
        for i in range(EXPERTS_PER_GROUP):
            e = g * EXPERTS_PER_GROUP + i
            cnt = jnp.zeros_like(s[e])
            for jj in range(EXPERTS_PER_GROUP):
                if jj == i:
                    continue
                o = g * EXPERTS_PER_GROUP + jj
                beats = (s[o] > s[e]) if jj > i else (s[o] >= s[e])
                cnt = cnt + jnp.where(beats, 1.0, 0.0)
            in_top2.append(cnt < 2.0)
    grp = []
    for g in range(N_EXPERT_GROUPS):
        tot = jnp.zeros_like(s[0])
        for i in range(EXPERTS_PER_GROUP):
            e = g * EXPERTS_PER_GROUP + i
            tot = tot + jnp.where(in_top2[e], s[e], 0.0)
        grp.append(tot)
    keep_rows = []
    for g in range(N_EXPERT_GROUPS):
        cnt = jnp.zeros_like(s[0])
        for o in range(N_EXPERT_GROUPS):
            if o == g:
                continue
            beats = (grp[o] > grp[g]) if o > g else (grp[o] >= grp[g])
            cnt = cnt + jnp.where(beats, 1.0, 0.0)
        best = cnt < 1.0
        for i in range(EXPERTS_PER_GROUP):
            e = g * EXPERTS_PER_GROUP + i
            keep_rows.append(jnp.where(best, jnp.where(in_top2[e], 1.0, 0.0), 0.0))
    keep = jnp.concatenate(keep_rows, axis=0)
    picked = scores * keep
    return picked / jnp.sum(picked, axis=0, keepdims=True), keep


def _oproj_kernel(y0_ref, y1_ref, y2_ref, y3_ref, w_ref, x_ref, mod_ref, g_ref, b_ref, wr_ref, br_ref,
                  x1_ref, h2_ref, gate_ref, keep_ref):
    acc = None
    for k, yr in enumerate((y0_ref, y1_ref, y2_ref, y3_ref)):
        part = jnp.dot(yr[...], w_ref[0, k * D_GROUP:(k + 1) * D_GROUP, :], preferred_element_type=F32)
        acc = part if acc is None else acc + part
    m = mod_ref[0, 0]
    gate1 = m[:, 2 * D_MODEL:3 * D_MODEL]
    shift2 = m[:, 3 * D_MODEL:4 * D_MODEL]
    scale2 = m[:, 4 * D_MODEL:5 * D_MODEL]
    x1 = _layer_norm(DEEPNORM_ALPHA * x_ref[...] + gate1 * acc, g_ref[0], b_ref[0])
    x1_ref[...] = x1
    h2 = x1 * (1.0 + scale2) + shift2
    h2_ref[...] = h2.astype(BF16)
    scores_t = _sigmoid(_dot3(wr_ref[...], h2, NT))
    gates_t, keep_t = _route(scores_t, br_ref[...])
    gate_ref[...] = gates_t.T
    keep_ref[...] = keep_t.T


def _oproj(st, ys, x, mod4, w_out_b, ln_g, ln_b, w_router, b_router, l):
    row = lambda i: (i, 0)
    return pl.pallas_call(
        _oproj_kernel,
        grid=(st.rows // OP_TM,),
        in_specs=[pl.BlockSpec((OP_TM, D_GROUP), row)] * 4 + [
            pl.BlockSpec((1, D_MODEL, D_MODEL), lambda i: (l, 0, 0)),
            pl.BlockSpec((OP_TM, D_MODEL), row),
            pl.BlockSpec((1, 1, 1, 6 * D_MODEL), lambda i: (l, st.mod_row(i, OP_TM), 0, 0)),
            pl.BlockSpec((1, 1, D_MODEL), lambda i: (l, 0, 0)),
            pl.BlockSpec((1, 1, D_MODEL), lambda i: (l, 0, 0)),
            pl.BlockSpec((N_EXPERTS, D_MODEL), lambda i: (0, 0)),
            pl.BlockSpec((N_EXPERTS, 1), lambda i: (0, 0))],
        out_specs=[pl.BlockSpec((OP_TM, D_MODEL), row),
                   pl.BlockSpec((OP_TM, D_MODEL), row),
                   pl.BlockSpec((OP_TM, N_EXPERTS), row),
                   pl.BlockSpec((OP_TM, N_EXPERTS), row)],
        out_shape=[jax.ShapeDtypeStruct((st.rows, D_MODEL), F32),
                   jax.ShapeDtypeStruct((st.rows, D_MODEL), BF16),
                   jax.ShapeDtypeStruct((st.rows, N_EXPERTS), F32),
                   jax.ShapeDtypeStruct((st.rows, N_EXPERTS), F32)],
        compiler_params=_cparams("parallel"),
        name="oproj_ln1_router",
    )(*ys, w_out_b, x, mod4, ln_g.reshape(DEPTH, 1, D_MODEL), ln_b.reshape(DEPTH, 1, D_MODEL),
      w_router.T, b_router.reshape(N_EXPERTS, 1))


LN_TM = 512


GS_BLK = 1024
GS_CAP = 288
GS_ALIGN = 2 * SUB
GS_COLS = 512
GS_STEP = 256
GS_VMEM_LIMIT = 60 * 1024 * 1024
GS_RCH = 3 * LANE
GS_ROWS = pl.cdiv(GS_BLK + N_EXPERT_GROUPS * GS_ALIGN, GS_RCH) * GS_RCH


def _moe_sorted_kernel(h_ref, gate_ref, keep_ref, wg_ref, wu_ref, wd_ref, f_ref, xs_scr, gs_scr, pt_scr, acc_scr,
                       seg_smem):
    e = pl.program_id(1)

    @pl.when(e == 0)
    def _():
        keep = keep_ref[...]
        lane = _iota(keep.shape, 1)
        member = []
        for g in range(N_EXPERT_GROUPS):
            in_g = (lane >= g * EXPERTS_PER_GROUP) & (lane < (g + 1) * EXPERTS_PER_GROUP)
            member.append(jnp.minimum(jnp.sum(jnp.where(in_g, keep, 0.0), axis=-1, keepdims=True), 1.0))
        lane4 = _iota((GS_BLK, LANE), 1)
        onehot = jnp.zeros((GS_BLK, LANE), F32)
        for g in range(N_EXPERT_GROUPS):
            onehot = onehot + jnp.where(lane4 == g, member[g], 0.0)
        counts = jnp.sum(onehot, axis=0, keepdims=True)
        starts = []
        start = jnp.zeros((1, 1), F32)
        for g in range(N_EXPERT_GROUPS):
            cnt_g = counts[:, g:g + 1]
            starts.append(start)
            seg_smem[2 * g] = jnp.sum(start).astype(jnp.int32)
            seg_smem[2 * g + 1] = jnp.sum(cnt_g).astype(jnp.int32)
            start = start + jnp.ceil(cnt_g * (1.0 / GS_ALIGN)) * GS_ALIGN
        onehot_b = onehot.astype(BF16)
        for r0 in range(0, GS_BLK, GS_STEP):
            tri = jnp.where(_iota((GS_STEP, GS_BLK), 1) <= _iota((GS_STEP, GS_BLK), 0) + r0, 1.0, 0.0).astype(BF16)
            csum = jnp.dot(tri, onehot_b, preferred_element_type=F32)
            dest = jnp.zeros((GS_STEP, 1), F32)
            for g in range(N_EXPERT_GROUPS):
                dest = dest + member[g][r0:r0 + GS_STEP] * (starts[g] + csum[:, g:g + 1] - 1.0)
            pt_scr[r0:r0 + GS_STEP, :] = jnp.where(
                _iota((GS_STEP, GS_ROWS), 1) == dest.astype(jnp.int32), 1.0, 0.0).astype(BF16)
        for c0 in range(0, GS_ROWS, GS_RCH):
            pt_c = pt_scr[:, c0:c0 + GS_RCH]
            xs_scr[c0:c0 + GS_RCH, :] = _dg(pt_c, h_ref[...], TN).astype(BF16)
            gs_scr[c0:c0 + GS_RCH, :] = _dot_sel_l(pt_c, gate_ref[...], TN)
        acc_scr[...] = jnp.zeros_like(acc_scr)

    g = e // EXPERTS_PER_GROUP
    seg_start = seg_smem[2 * g]
    seg_len = seg_smem[2 * g + 1]
    wg = wg_ref[0, 0]
    wu = wu_ref[0, 0]
    wd = wd_ref[0, 0]

    def window(c, carry):
        first = seg_start + c * GS_CAP
        start = jnp.minimum(first, GS_ROWS - GS_CAP)
        rows = pl.ds(pl.multiple_of(start, GS_ALIGN), GS_CAP)
        x = xs_scr[rows, :]
        a = jnp.dot(x, wg, preferred_element_type=F32)
        u = jnp.dot(x, wu, preferred_element_type=F32)
        gates = gs_scr[rows, :]
        ge = jnp.sum(jnp.where(_iota(gates.shape, 1) == e, gates, 0.0), axis=-1, keepdims=True)
        ge = jnp.where(_iota((GS_CAP, 1), 0) + start >= first, ge, 0.0)
        hid = (_silu(a) * u * ge).astype(BF16)
        acc_scr[rows, :] += jnp.dot(hid, wd, preferred_element_type=F32)
        return carry

    lax.fori_loop(0, (seg_len + GS_CAP - 1) // GS_CAP, window, 0)

    @pl.when(e == N_EXPERTS - 1)
    def _():
        pt = pt_scr[...]
        for c0 in range(0, D_MODEL, GS_COLS):
            acc_b = acc_scr[:, c0:c0 + GS_COLS].astype(BF16)
            f_ref[:, c0:c0 + GS_COLS] = jnp.dot(pt, acc_b, preferred_element_type=F32).astype(BF16)


def _moe_sorted(st, h2, gates, keep, wg_b, wu_b, wd_b, l):
    row = lambda i, e: (i, 0)
    wspec = lambda a, b: pl.BlockSpec((1, 1, a, b), lambda i, e: (l, e, 0, 0))
    return pl.pallas_call(
        _moe_sorted_kernel,
        grid=(st.rows // GS_BLK, N_EXPERTS),
        in_specs=[pl.BlockSpec((GS_BLK, D_MODEL), row, pipeline_mode=pl.Buffered(1)),
                  pl.BlockSpec((GS_BLK, N_EXPERTS), row), pl.BlockSpec((GS_BLK, N_EXPERTS), row),
                  wspec(D_MODEL, D_EXPERT), wspec(D_MODEL, D_EXPERT), wspec(D_EXPERT, D_MODEL)],
        out_specs=pl.BlockSpec((GS_BLK, D_MODEL), row),
        out_shape=jax.ShapeDtypeStruct((st.rows, D_MODEL), BF16),
        scratch_shapes=[pltpu.VMEM((GS_ROWS, D_MODEL), BF16), pltpu.VMEM((GS_ROWS, N_EXPERTS), F32),
                        pltpu.VMEM((GS_BLK, GS_ROWS), BF16), pltpu.VMEM((GS_ROWS, D_MODEL), F32),
                        pltpu.SMEM((2 * N_EXPERT_GROUPS,), jnp.int32)],
        compiler_params=pltpu.CompilerParams(dimension_semantics=("parallel", "arbitrary"),
                                             vmem_limit_bytes=GS_VMEM_LIMIT),
        name="moe_group_sorted",
    )(h2, gates, keep, wg_b, wu_b, wd_b)


def _ln2f_kernel(x_ref, f_ref, mod_ref, g_ref, b_ref, o_ref):
    gate2 = mod_ref[0, 0][:, 5 * D_MODEL:6 * D_MODEL]
    o_ref[...] = _layer_norm(DEEPNORM_ALPHA * x_ref[...] + gate2 * f_ref[...].astype(F32), g_ref[0], b_ref[0])


def _ln2f(st, x1, f, mod4, ln_g, ln_b, l):
    row = lambda i: (i, 0)
    return pl.pallas_call(
        _ln2f_kernel,
        grid=(st.rows // LN_TM,),
        in_specs=[pl.BlockSpec((LN_TM, D_MODEL), row),
                  pl.BlockSpec((LN_TM, D_MODEL), row),
                  pl.BlockSpec((1, 1, 1, 6 * D_MODEL), lambda i: (l, st.mod_row(i, LN_TM), 0, 0)),
                  pl.BlockSpec((1, 1, D_MODEL), lambda i: (l, 0, 0)),
                  pl.BlockSpec((1, 1, D_MODEL), lambda i: (l, 0, 0))],
        out_specs=pl.BlockSpec((LN_TM, D_MODEL), row),
        out_shape=jax.ShapeDtypeStruct((st.rows, D_MODEL), F32),
        compiler_params=_cparams("parallel"),
        name="ln2",
    )(x1, f, mod4, ln_g.reshape(DEPTH, 1, D_MODEL), ln_b.reshape(DEPTH, 1, D_MODEL))


S5_TC = 128
S5_SEQS = SUB
S5_Q = 4
S5_QS = S5_GROUPS // S5_Q * S5_STATE
N_S5 = S5_GROUPS * S5_STATE


def _s5_prep_kernel(lr_ref, li_ref, ls_ref, br_ref, bi_ref, e_ref, lbr_ref, lbi_ref, bbr_ref, bbi_ref):
    lr = lr_ref[0]
    li = li_ref[0]
    dt = jnp.exp(ls_ref[0])
    mag = jnp.exp(lr * dt)
    ang = li * dt
    ar = mag * jnp.cos(ang)
    ai = mag * jnp.sin(ang)
    lbr_ref[0] = ar
    lbi_ref[0] = ai
    den = lr * lr + li * li
    nr = ar - 1.0
    cr = (nr * lr + ai * li) / den
    ci = (ai * lr - nr * li) / den
    cr = _dot_sel_l(e_ref[...], cr)
    ci = _dot_sel_l(e_ref[...], ci)
    bre = br_ref[0]
    bim = bi_ref[0]
    bbr_ref[0] = cr * bre - ci * bim
    bbi_ref[0] = cr * bim + ci * bre


def _s5_prep(lam_re, lam_im, log_step, b_re, b_im):
    d2 = DEPTH * 2
    g, p, h = S5_GROUPS, S5_STATE, S5_CH
    bt_re = jnp.swapaxes(b_re, -1, -2).reshape(d2, g * h, p)
    bt_im = jnp.swapaxes(b_im, -1, -2).reshape(d2, g * h, p)
    expand = jnp.asarray(np.kron(np.eye(g, dtype=np.float32), np.ones((h, 1), np.float32)))
    spec_gp = pl.BlockSpec((1, g, p), lambda i: (i, 0, 0))
    spec_b = pl.BlockSpec((1, g * h, p), lambda i: (i, 0, 0))
    lbr, lbi, bbr, bbi = pl.pallas_call(
        _s5_prep_kernel,
        grid=(d2,),
        in_specs=[spec_gp, spec_gp, pl.BlockSpec((1, g, 1), lambda i: (i, 0, 0)), spec_b, spec_b,
                  pl.BlockSpec((g * h, g), lambda i: (0, 0))],
        out_specs=[spec_gp, spec_gp, spec_b, spec_b],
        out_shape=[jax.ShapeDtypeStruct((d2, g, p), F32)] * 2 + [jax.ShapeDtypeStruct((d2, g * h, p), F32)] * 2,
        compiler_params=_cparams("parallel"),
        name="s5_prep",
    )(lam_re.reshape(d2, g, p), lam_im.reshape(d2, g, p), log_step.reshape(d2, g, 1), bt_re, bt_im, expand)
    return lbr, lbi, bbr, bbi


def _s5_block_params(lbr, lbi, bbr, bbi, c_re, c_im):
    d2 = DEPTH * 2
    gq = S5_GROUPS // S5_Q
    eye = jnp.eye(gq, dtype=F32)

    def b_blocks(b):
        b = b.reshape(d2, S5_Q, gq, S5_CH, S5_STATE)
        return jnp.einsum('djghp,gk->djghkp', b, eye).reshape(d2, S5_Q, gq * S5_CH, gq * S5_STATE)

    def c_blocks(c):
        c = c.reshape(d2, S5_Q, gq, S5_CH, S5_STATE)
        return jnp.einsum('djghp,gk->djgpkh', c, eye).reshape(d2, S5_Q, gq * S5_STATE, gq * S5_CH)

    bq = jnp.concatenate([b_blocks(bbr), b_blocks(bbi)], axis=-1)
    cq = jnp.concatenate([c_blocks(c_re), -c_blocks(c_im)], axis=-2)
    return bq, cq, lbr.reshape(d2, 1, N_S5), lbi.reshape(d2, 1, N_S5)


def _s5_scan_kernel(u_ref, bq_ref, cq_ref, lr_ref, li_ref, s0r_ref, s0i_ref, y_ref, sfr_ref, sfi_ref,
                    utb, bur, bui, ytb, sr_scr, si_scr):
    d = pl.program_id(1)
    c = pl.program_id(2)
    nrow = S5_TC * S5_SEQS
    cw = D_GROUP // S5_Q

    @pl.when(c == 0)
    def _():
        sr_scr[...] = s0r_ref[0, 0]
        si_scr[...] = s0i_ref[0, 0]

    for b in range(S5_SEQS):
        for j in range(S5_Q):
            utb[j, pl.ds(b, S5_TC, stride=S5_SEQS), :] = u_ref[b, :, j * cw:(j + 1) * cw]
    for j in range(S5_Q):
        bu = _dot1(utb[j], bq_ref[0, j])
        bur[:, j * S5_QS:(j + 1) * S5_QS] = bu[:, :S5_QS]
        bui[:, j * S5_QS:(j + 1) * S5_QS] = bu[:, S5_QS:]
    for j in range(S5_Q):
        sl = slice(j * S5_QS, (j + 1) * S5_QS)
        lam_r = jnp.broadcast_to(lr_ref[0][:, sl], (S5_SEQS, S5_QS))
        lam_i = jnp.broadcast_to(li_ref[0][:, sl], (S5_SEQS, S5_QS))

        def step(t, carry, sl=sl, lam_r=lam_r, lam_i=lam_i):
            sr, si = carry
            te = jnp.where(d == 0, t, S5_TC - 1 - t)
            rows = pl.ds(pl.multiple_of(te * S5_SEQS, S5_SEQS), S5_SEQS)
            nr = lam_r * sr - lam_i * si + bur[rows, sl]
            ni = lam_r * si + lam_i * sr + bui[rows, sl]
            bur[rows, sl] = nr
            bui[rows, sl] = ni
            return nr, ni

        sr, si = lax.fori_loop(0, S5_TC, step, (sr_scr[:, sl], si_scr[:, sl]))
        sr_scr[:, sl] = sr
        si_scr[:, sl] = si
    for j in range(S5_Q):
        sl = slice(j * S5_QS, (j + 1) * S5_QS)
        yj = _dot1(bur[:, sl], cq_ref[0, j, :S5_QS, :]) + _dot1(bui[:, sl], cq_ref[0, j, S5_QS:, :])
        ytb[j] = yj
    for b in range(S5_SEQS):
        for j in range(S5_Q):
            y_ref[0, b, :, j * cw:(j + 1) * cw] = ytb[j, pl.ds(b, S5_TC, stride=S5_SEQS), :]

    @pl.when(c == pl.num_programs(2) - 1)
    def _():
        sfr_ref[0, 0] = sr_scr[...]
        sfi_ref[0, 0] = si_scr[...]


def _s5_scan(st, z3, bq, cq, lamr, lami, s0r, s0i, l):
    ng, nc = st.nb // S5_SEQS, st.L // S5_TC
    nrow = S5_TC * S5_SEQS
    chunk = lambda d, c: c + d * (nc - 1 - 2 * c)
    par = lambda g, d, c: (2 * l + d, 0, 0, 0)
    st_spec = pl.BlockSpec((1, 1, S5_SEQS, N_S5), lambda g, d, c: (g, d, 0, 0))
    return pl.pallas_call(
        _s5_scan_kernel,
        grid=(ng, 2, nc),
        in_specs=[pl.BlockSpec((S5_SEQS, S5_TC, D_GROUP), lambda g, d, c: (g, chunk(d, c), 0)),
                  pl.BlockSpec((1, S5_Q, D_GROUP // S5_Q, 2 * S5_QS), par),
                  pl.BlockSpec((1, S5_Q, 2 * S5_QS, D_GROUP // S5_Q), par),
                  pl.BlockSpec((1, 1, N_S5), lambda g, d, c: (2 * l + d, 0, 0)),
                  pl.BlockSpec((1, 1, N_S5), lambda g, d, c: (2 * l + d, 0, 0)),
                  st_spec, st_spec],
        out_specs=[pl.BlockSpec((1, S5_SEQS, S5_TC, D_GROUP), lambda g, d, c: (d, g, chunk(d, c), 0)),
                   st_spec, st_spec],
        out_shape=[jax.ShapeDtypeStruct((2, st.nb, st.L, D_GROUP), F32),
                   jax.ShapeDtypeStruct((ng, 2, S5_SEQS, N_S5), F32),
                   jax.ShapeDtypeStruct((ng, 2, S5_SEQS, N_S5), F32)],
        scratch_shapes=[pltpu.VMEM((S5_Q, nrow, LANE), F32), pltpu.VMEM((nrow, N_S5), F32),
                        pltpu.VMEM((nrow, N_S5), F32), pltpu.VMEM((S5_Q, nrow, LANE), F32),
                        pltpu.VMEM((S5_SEQS, N_S5), F32), pltpu.VMEM((S5_SEQS, N_S5), F32)],
        compiler_params=_cparams("parallel", "arbitrary", "arbitrary"),
        name="s5_scan",
    )(z3, bq, cq, lamr, lami, s0r, s0i)


S5_TM = 512


def _gelu_tanh(x):
    return 0.5 * x * (1.0 + jnp.tanh(math.sqrt(2.0 / math.pi) * (x + 0.044715 * (x * x * x))))


def _s5_out_kernel(u_ref, yf_ref, yb_ref, d_ref, w_ref, o_ref):
    y = yf_ref[0] + yb_ref[0] + d_ref[0] * u_ref[...]
    y = _gelu_tanh(y)
    o_ref[...] = (y * _sigmoid(_dot1(y, w_ref[0]))).astype(BF16)


def _s5_out(st, z, ydir, d_skip, w_glu, l):
    return pl.pallas_call(
        _s5_out_kernel,
        grid=(st.rows // S5_TM,),
        in_specs=[pl.BlockSpec((S5_TM, D_GROUP), lambda i: (i, 0)),
                  pl.BlockSpec((1, S5_TM, D_GROUP), lambda i: (0, i, 0)),
                  pl.BlockSpec((1, S5_TM, D_GROUP), lambda i: (1, i, 0)),
                  pl.BlockSpec((1, 1, D_GROUP), lambda i: (l, 0, 0)),
                  pl.BlockSpec((1, D_GROUP, D_GROUP), lambda i: (l, 0, 0))],
        out_specs=pl.BlockSpec((S5_TM, D_GROUP), lambda i: (i, 0)),
        out_shape=jax.ShapeDtypeStruct((st.rows, D_GROUP), BF16),
        compiler_params=_cparams("parallel"),
        name="s5_out",
    )(z, ydir, ydir, d_skip.reshape(DEPTH, 1, D_GROUP), w_glu)


NA_SCALE = NA_HEAD ** -0.5
NA_ROWS = DEC_SEQ // GRID_W
NA_KH = min(NA_WIN_H, NA_ROWS)
NA_QCOL = NA_OFF // LANE
NA_KCOL = (NA_OFF + D_GROUP) // LANE
NA_VCOL = (NA_OFF + 2 * D_GROUP) // LANE
NA_NDR = 2 * NA_WIN_H - 1
NA_AHEAD = 1


def _na_ctx_kernel(q_ref, k_ref, v_ref, o_ref, nk_ref, nv_ref):
    scores = []
    for hh in range(2):
        sl = slice(hh * NA_HEAD, (hh + 1) * NA_HEAD)
        k = k_ref[0, :, sl]
        nk_ref[0, hh] = k
        scores.append(_dot1(q_ref[0, :, sl], k, NT) * NA_SCALE)
    for hh in range(2):
        sl = slice(hh * NA_HEAD, (hh + 1) * NA_HEAD)
        v = v_ref[0, :, sl]
        nv_ref[0, hh] = v
        s = scores[hh]
        e = jnp.exp(s - jnp.max(s, axis=-1, keepdims=True))
        o = _dot1(e, v) / jnp.sum(e, axis=-1, keepdims=True)
        o_ref[0, :, sl] = o.astype(BF16)


def _na_ctx(st, z3):
    blk = lambda col: pl.BlockSpec((1, st.L, LANE), lambda b, p: (b, 0, col + p))
    kv_spec = pl.BlockSpec((1, 2, st.L, NA_HEAD), lambda b, p: (b, p, 0, 0))
    kv_shape = jax.ShapeDtypeStruct((st.nb, NA_HEADS, st.L, NA_HEAD), F32)
    return pl.pallas_call(
        _na_ctx_kernel,
        grid=(st.nb, NA_HEADS // 2),
        in_specs=[blk(NA_QCOL), blk(NA_KCOL), blk(NA_VCOL)],
        out_specs=[pl.BlockSpec((1, st.L, LANE), lambda b, p: (b, 0, p)), kv_spec, kv_spec],
        out_shape=[jax.ShapeDtypeStruct((st.nb, st.L, D_GROUP), BF16), kv_shape, kv_shape],
        compiler_params=_cparams("parallel", "parallel"),
        name="na_context",
    )(z3, z3, z3)


def _na_nbr_kernel(q_ref, k_ref, v_ref, ck_ref, cv_ref, bias_ref, o_ref):
    nloc = NA_KH * GRID_W
    qc = _iota((GRID_W, nloc), 0)
    kc = _iota((GRID_W, nloc), 1) % GRID_W
    cs = jnp.clip(qc - NA_WIN_W // 2, 0, GRID_W - NA_WIN_W)
    col_in = (kc >= cs) & (kc < cs + NA_WIN_W)

    def scores(hh, r):
        sl = slice(hh * NA_HEAD, (hh + 1) * NA_HEAD)
        rs = min(max(r - NA_KH // 2, 0), NA_ROWS - NA_KH)
        q = q_ref[0, r * GRID_W:(r + 1) * GRID_W, sl]
        k = k_ref[0, rs * GRID_W:rs * GRID_W + nloc, sl]
        off = (rs - r + NA_WIN_H - 1) * GRID_W
        s_loc = _dot1(q, k, NT) * NA_SCALE + bias_ref[hh, :, off:off + nloc]
        s_loc = jnp.where(col_in, s_loc, NEG_INF)
        s_ctx = _dot1(q, ck_ref[0, 0, hh], NT) * NA_SCALE
        return s_loc, s_ctx

    def finish(hh, r, s_loc, s_ctx):
        sl = slice(hh * NA_HEAD, (hh + 1) * NA_HEAD)
        rs = min(max(r - NA_KH // 2, 0), NA_ROWS - NA_KH)
        v = v_ref[0, rs * GRID_W:rs * GRID_W + nloc, sl]
        m = jnp.maximum(jnp.max(s_loc, axis=-1, keepdims=True), jnp.max(s_ctx, axis=-1, keepdims=True))
        e_loc = jnp.exp(s_loc - m)
        e_ctx = jnp.exp(s_ctx - m)
        den = jnp.sum(e_loc, axis=-1, keepdims=True) + jnp.sum(e_ctx, axis=-1, keepdims=True)
        o = (_dot1(e_loc, v) + _dot1(e_ctx, cv_ref[0, 0, hh])) / den
        o_ref[0, r * GRID_W:(r + 1) * GRID_W, sl] = o.astype(BF16)

    blocks = [(hh, r) for hh in range(2) for r in range(NA_ROWS)]
    pending = []
    for i, blk in enumerate(blocks):
        pending.append(scores(*blk))
        if i >= NA_AHEAD:
            finish(*blocks[i - NA_AHEAD], *pending.pop(0))
    for j in range(len(blocks) - NA_AHEAD, len(blocks)):
        finish(*blocks[j], *pending.pop(0))


def _na_bias_table(rpb):
    qc = np.arange(GRID_W)[:, None]
    kc = np.arange(GRID_W)[None, :]
    dc = np.clip(kc - qc, -(NA_WIN_W - 1), NA_WIN_W - 1) + (NA_WIN_W - 1)
    t = rpb[:, :, :, dc]
    return jnp.transpose(t, (0, 1, 3, 2, 4)).reshape(DEPTH, NA_HEADS, GRID_W, NA_NDR * GRID_W)


def _na_nbr(st, z3, cache_k, cache_v, bias_tab, l):
    blk = lambda col: pl.BlockSpec((1, st.L, LANE), lambda b, p: (b, 0, col + p))
    cspec = pl.BlockSpec((1, 1, 2, PAST_LEN, NA_HEAD), lambda b, p: (b, l, p, 0, 0))
    return pl.pallas_call(
        _na_nbr_kernel,
        grid=(st.nb, NA_HEADS // 2),
        in_specs=[blk(NA_QCOL), blk(NA_KCOL), blk(NA_VCOL), cspec, cspec,
                  pl.BlockSpec((None, 2, GRID_W, NA_NDR * GRID_W), lambda b, p: (l, p, 0, 0))],
        out_specs=pl.BlockSpec((1, st.L, LANE), lambda b, p: (b, 0, p)),
        out_shape=jax.ShapeDtypeStruct((st.nb, st.L, D_GROUP), BF16),
        compiler_params=_cparams("parallel", "parallel"),
        name="na_neighbourhood",
    )(z3, z3, z3, cache_k, cache_v, bias_tab)


ML_COL = ML_OFF // LANE
ML_SCALE = ML_HEAD ** -0.5


def _log_sigmoid(x):
    return -_softplus(-x)


def _rope_tables(L):
    half = ML_HEAD // 2
    quarter = half // 2
    t = np.arange(L)
    inv_freq = ROPE_BASE ** (-np.arange(quarter, dtype=np.float32) / quarter)

    def tabs(pos):
        ang = pos.astype(np.float32)[:, None] * inv_freq[None, :].astype(np.float32)
        c, s = np.cos(ang), np.sin(ang)
        return np.concatenate([c, c], axis=-1), np.concatenate([-s, s], axis=-1)

    c1, s1 = tabs(t // GRID_W)
    c2, s2 = tabs(t % GRID_W)
    return (jnp.asarray(np.concatenate([c1, c2], axis=-1), F32),
            jnp.asarray(np.concatenate([s1, s2], axis=-1), F32))


def _ml_kernel(*refs, L, rotary, layer):
    nh = ML_HEADS
    ib_ref, fb_ref = refs[0], refs[1]
    q_refs, k_refs, v_refs, o_refs = (refs[2 + i * nh:2 + (i + 1) * nh] for i in range(4))
    (g_ref, cos_ref, sin_ref, c0_ref, n0_ref, m0_ref, lng_ref, lnb_ref,
     y_ref, cf_ref, nf_ref, mf_ref, qs, ks, hf, hb, c_scr, n_scr, m_scr) = refs[2 + 4 * nh:]
    T = ML_CHUNK
    nc = L // T

    if rotary:
        first = (_iota((L, ML_HEAD), 1) % (ML_HEAD // 2)) < ML_HEAD // 4

        def rope(x):
            quarter = ML_HEAD // 4
            partner = jnp.where(first, pltpu.roll(x, ML_HEAD - quarter, axis=1), pltpu.roll(x, quarter, axis=1))
            return x * cos_ref[...] + partner * sin_ref[...]
    else:
        rope = lambda x: x
    for h in range(nh):
        qs[h] = rope(q_refs[h][0]) * ML_SCALE
        ks[h] = rope(k_refs[h][0])
        for d in range(2):
            c_scr[d, h] = c0_ref[0, d, h]
            n_scr[d, h] = n0_ref[0, h, d:d + 1, :]
            m_scr[d, h] = m0_ref[0, h, d:d + 1, :]

    tj = _iota((T, T), 0)
    ts = _iota((T, T), 1)
    ones = jnp.ones((T, T), F32)
    upto_row = (tj <= ts, tj >= ts)
    upto_col = (ts <= tj, ts >= tj)

    def body(ci, carry):
        ch = []
        for d in range(2):
            cd = ci if d == 0 else nc - 1 - ci
            rows = pl.ds(pl.multiple_of(cd * T, T), T)
            for h in range(nh):
                g = g_ref[0, h, cd]
                ig = g[:, d:d + 1] + ib_ref[layer * 2 * nh + d * nh + h]
                fg = g[:, 2 + d:3 + d] + fb_ref[layer * 2 * nh + d * nh + h]
                lf = jnp.broadcast_to(_log_sigmoid(fg), (T, T))
                igb = jnp.broadcast_to(ig, (T, T))
                q = dict(d=d, h=h, rows=rows, ig=ig, qc=qs[h, rows, :], kc=ks[h, rows, :], vc=v_refs[h][0, rows, :])
                q['bcol'] = _dot_sel_l(jnp.where(upto_col[d], 1.0, 0.0), lf)
                q['brow_i'] = _dot_sel_l(ones, jnp.where(upto_row[d], lf, 0.0) - jnp.where(tj == ts, igb, 0.0))
                ch.append(q)
        for q in ch:
            q['qk'] = _dot1(q['qc'], q['kc'], NT)
            q['c_old'] = c_scr[q['d'], q['h']]
            q['qc_c'] = _dot1(q['qc'], q['c_old'])
        for q in ch:
            d, h = q['d'], q['h']
            dmat = jnp.where(upto_col[d], q['bcol'] - q['brow_i'], -jnp.inf)
            b1 = q['bcol'][:, 0:1]
            m_prev = m_scr[d, h]
            inter = b1 + m_prev
            m_t = jnp.maximum(inter, jnp.max(dmat, axis=-1, keepdims=True))
            qk = q['qk'] * jnp.exp(dmat - m_t)
            w_inter = jnp.exp(inter - m_t)
            n_old = n_scr[d, h]
            den = w_inter * jnp.sum(q['qc'] * n_old, axis=-1, keepdims=True) + jnp.sum(qk, axis=-1, keepdims=True)
            q['scale'] = 1.0 / jnp.maximum(jnp.abs(den), jnp.exp(-m_t))
            q['inter_part'] = w_inter * q['qc_c']
            q['intra'] = _dot1(qk, q['vc'])
            b_last = b1[T - 1:T, :] if d == 0 else b1[0:1, :]
            g_s = b_last - b1 + q['ig']
            m_new = jnp.maximum(b_last + m_prev, jnp.max(g_s, axis=0, keepdims=True))
            w_old = jnp.exp(b_last + m_prev - m_new)
            w_s = jnp.exp(g_s - m_new)
            q['c_new'] = _dot1(q['kc'], w_s * q['vc'], TN)
            q['w_old'] = w_old
            n_scr[d, h] = w_old * n_old + jnp.sum(w_s * q['kc'], axis=0, keepdims=True)
            m_scr[d, h] = m_new
        for q in ch:
            d, h = q['d'], q['h']
            hcur = (q['inter_part'] + q['intra']) * q['scale']
            if d == 0:
                hf[h, q['rows'], :] = hcur
            else:
                hb[h, q['rows'], :] = hcur
            c_scr[d, h] = q['w_old'] * q['c_old'] + q['c_new']
        return carry

    lax.fori_loop(0, nc, body, 0)

    for h in range(nh):
        hsum = hf[h] + hb[h]
        mu = jnp.mean(hsum, axis=-1, keepdims=True)
        dv = hsum - mu
        var = jnp.mean(dv * dv, axis=-1, keepdims=True)
        hn = dv * lax.rsqrt(var + ML_GN_EPS)
        y_ref[0, :, h * ML_HEAD:(h + 1) * ML_HEAD] = (
            _sigmoid(o_refs[h][0]) * (hn * lng_ref[0, :, h * ML_HEAD:(h + 1) * ML_HEAD]
                                      + lnb_ref[0, :, h * ML_HEAD:(h + 1) * ML_HEAD])).astype(BF16)
        for d in range(2):
            cf_ref[0, d, h] = c_scr[d, h]
            nf_ref[0, h, d:d + 1, :] = n_scr[d, h]
            mf_ref[0, h, d:d + 1, :] = m_scr[d, h]


def _ml_mixer(st, z3, zg, i_bias, f_bias, ln_g, ln_b, c0, n0, m0, l):
    L, nb, nh = st.L, st.nb, ML_HEADS
    nc = L // ML_CHUNK
    gates = zg.reshape(nb, L, 2, 2, nh).transpose(0, 4, 1, 2, 3).reshape(nb, nh, nc, ML_CHUNK, 4)
    cos_t, sin_t = _rope_tables(L)
    blk = lambda col: pl.BlockSpec((1, L, LANE), lambda b: (b, 0, col))
    head_blocks = [blk(ML_COL + part * nh + h) for part in range(4) for h in range(nh)]
    smem = pl.BlockSpec(memory_space=pltpu.SMEM)
    tab = pl.BlockSpec((L, ML_HEAD), lambda b: (0, 0))
    c_spec = pl.BlockSpec((1, 2, nh, ML_HEAD, ML_HEAD), lambda b: (b, 0, 0, 0, 0))
    n_spec = pl.BlockSpec((1, nh, 2, ML_HEAD), lambda b: (b, 0, 0, 0))
    m_spec = pl.BlockSpec((1, nh, 2, 1), lambda b: (b, 0, 0, 0))
    par = pl.BlockSpec((1, 1, D_GROUP), lambda b: (l, 0, 0))
    return pl.pallas_call(
        functools.partial(_ml_kernel, L=L, rotary=st.latent, layer=l),
        grid=(nb,),
        in_specs=[smem, smem] + head_blocks + [
            pl.BlockSpec((1, nh, nc, ML_CHUNK, 4), lambda b: (b, 0, 0, 0, 0)),
            tab, tab, c_spec, n_spec, m_spec, par, par],
        out_specs=[pl.BlockSpec((1, L, D_GROUP), lambda b: (b, 0, 0)), c_spec, n_spec, m_spec],
        out_shape=[jax.ShapeDtypeStruct((nb, L, D_GROUP), BF16),
                   jax.ShapeDtypeStruct((nb, 2, nh, ML_HEAD, ML_HEAD), F32),
                   jax.ShapeDtypeStruct((nb, nh, 2, ML_HEAD), F32),
                   jax.ShapeDtypeStruct((nb, nh, 2, 1), F32)],
        scratch_shapes=[pltpu.VMEM((nh, L, ML_HEAD), F32), pltpu.VMEM((nh, L, ML_HEAD), F32),
                        pltpu.VMEM((nh, L, ML_HEAD), F32), pltpu.VMEM((nh, L, ML_HEAD), F32),
                        pltpu.VMEM((2, nh, ML_HEAD, ML_HEAD), F32), pltpu.VMEM((2, nh, 1, ML_HEAD), F32),
                        pltpu.VMEM((2, nh, 1, 1), F32)],
        compiler_params=_cparams("parallel"),
        name="mlstm",
    )(i_bias.reshape(-1), f_bias.reshape(-1), *([z3] * (4 * nh)), gates, cos_t, sin_t, c0, n0, m0,
      ln_g.reshape(DEPTH, 1, D_GROUP), ln_b.reshape(DEPTH, 1, D_GROUP))


RW_TL = 256
RW_T = 64
RW_GROUP = 16
RW_RCOL = RW_OFF // D_GROUP
RW_LCOL = (RW_OFF + 3 * D_GROUP) // (2 * LANE)
N_LORA = RW_LORA_W + RW_LORA_A + RW_LORA_G


def _head_ones():
    return jnp.asarray(np.kron(np.eye(RW_HEADS, dtype=np.float32), np.ones((RW_HEAD, RW_HEAD), np.float32)))


def _rw_front_kernel(r_ref, k_ref, v_ref, lo_ref, rp_ref, kp_ref, vp_ref, lp_ref, rn_ref, kn_ref, vn_ref, ln_ref,
                     mup_ref, mun_ref, w0_ref, w2_ref, a0_ref, a2_ref, g2_ref, kk_ref, ka_ref, hones_ref,
                     ro_ref, vo_ref, kko_ref, go_ref, kd_ref, bd_ref, lw_ref):
    t = pl.program_id(1)
    first = t == 0
    last = t == pl.num_programs(1) - 1
    row = _iota((RW_TL, 1), 0)

    def shift(x_ref, p_ref, n_ref, lo, hi):
        x = x_ref[0]
        prev_edge = jnp.where(first, 0.0, p_ref[0, SUB - 1:SUB, :])
        next_edge = jnp.where(last, 0.0, n_ref[0, 0:1, :])
        prev = jnp.where(row == 0, prev_edge, pltpu.roll(x, 1, axis=0))
        nxt = jnp.where(row == RW_TL - 1, next_edge, pltpu.roll(x, RW_TL - 1, axis=0))
        return x + mup_ref[0][:, lo:hi] * (prev - x) + mun_ref[0][:, lo:hi] * (nxt - x)

    g = D_GROUP
    r = shift(r_ref, rp_ref, rn_ref, 0, g)
    k = shift(k_ref, kp_ref, kn_ref, g, 2 * g)
    v = shift(v_ref, vp_ref, vn_ref, 2 * g, 3 * g)
    lo = shift(lo_ref, lp_ref, ln_ref, 3 * g, 3 * g + N_LORA)
    zw = lo[:, :RW_LORA_W]
    za = lo[:, RW_LORA_W:RW_LORA_W + RW_LORA_A]
    zg = lo[:, RW_LORA_W + RW_LORA_A:]
    ro_ref[0] = r
    vo_ref[0] = v
    kk = k * kk_ref[0]
    ssq = _dot_sel_r(kk * kk, hones_ref[...])
    kk = kk * lax.rsqrt(ssq + 1e-12)
    kko_ref[0] = kk
    go_ref[0] = _dot3(_sigmoid(zg), g2_ref[0])
    tw = jnp.tanh(zw)
    for d in range(2):
        w_log = -_softplus(-(w0_ref[0, d:d + 1, :] + _dot3(tw, w2_ref[0, d]))) - 0.5
        lw_ref[d, 0] = -jnp.exp(w_log)
        a = _sigmoid(a0_ref[0, d:d + 1, :] + _dot3(za, a2_ref[0, d]))
        kd_ref[d, 0] = k * (1.0 + (a - 1.0) * ka_ref[0])
        bd_ref[d, 0] = kk * a


def _rw_front(st, z3, mu_prev, mu_next, w0, w2, a0, a2, g2, k_k, k_a, l):
    nb, L = st.nb, st.L
    nt = L // RW_TL
    tpb = RW_TL // SUB
    main = lambda w, col: pl.BlockSpec((1, RW_TL, w), lambda b, t: (b, t, col))
    prev = lambda w, col: pl.BlockSpec((1, SUB, w), lambda b, t: (b, jnp.maximum(t * tpb - 1, 0), col))
    nxt = lambda w, col: pl.BlockSpec((1, SUB, w), lambda b, t: (b, jnp.minimum((t + 1) * tpb, L // SUB - 1), col))
    cols = [(D_GROUP, RW_RCOL), (D_GROUP, RW_RCOL + 1), (D_GROUP, RW_RCOL + 2), (2 * LANE, RW_LCOL)]
    lay = lambda *shape: pl.BlockSpec((1,) + shape, lambda b, t: (l,) + (0,) * len(shape))
    out1 = pl.BlockSpec((1, RW_TL, D_GROUP), lambda b, t: (b, t, 0))
    out2 = pl.BlockSpec((2, 1, RW_TL, D_GROUP), lambda b, t: (0, b, t, 0))
    s1 = jax.ShapeDtypeStruct((nb, L, D_GROUP), F32)
    s2 = jax.ShapeDtypeStruct((2, nb, L, D_GROUP), F32)
    return pl.pallas_call(
        _rw_front_kernel,
        grid=(nb, nt),
        in_specs=[main(*c) for c in cols] + [prev(*c) for c in cols] + [nxt(*c) for c in cols] + [
            lay(1, RW_IN), lay(1, RW_IN), lay(2, D_GROUP), lay(2, RW_LORA_W, D_GROUP), lay(2, D_GROUP),
            lay(2, RW_LORA_A, D_GROUP), lay(RW_LORA_G, D_GROUP), lay(1, D_GROUP), lay(1, D_GROUP),
            pl.BlockSpec((D_GROUP, D_GROUP), lambda b, t: (0, 0))],
        out_specs=[out1, out1, out1, out1, out2, out2, out2],
        out_shape=[s1, s1, s1, s1, s2, s2, s2],
        compiler_params=_cparams("parallel", "parallel"),
        name="rwkv_front",
    )(*([z3] * 12), mu_prev.reshape(DEPTH, 1, RW_IN), mu_next.reshape(DEPTH, 1, RW_IN), w0, w2, a0, a2, g2,
      k_k.reshape(DEPTH, 1, D_GROUP), k_a.reshape(DEPTH, 1, D_GROUP), _head_ones())


def _rw_core_kernel(rf_ref, vf_ref, kkf_ref, kdf_ref, bdf_ref, lwf_ref,
                    rb_ref, vb_ref, kkb_ref, kdb_ref, bdb_ref, lwb_ref, s0_ref,
                    yf_ref, yb_ref, sf_ref, s_scr):
    c = pl.program_id(1)
    T = RW_T

    @pl.when(c == 0)
    def _():
        s_scr[...] = s0_ref[0]

    tj = _iota((T, T), 0)
    ts = _iota((T, T), 1)
    dirs = ((rf_ref, vf_ref, kkf_ref, kdf_ref, bdf_ref, lwf_ref, yf_ref),
            (rb_ref, vb_ref, kkb_ref, kdb_ref, bdb_ref, lwb_ref, yb_ref))
    ch = []
    for d, (r_ref, v_ref, kk_ref, kd_ref, bd_ref, lw_ref, y_ref) in enumerate(dirs):
        if d == 0:
            incl = ts <= tj
            strict = ts < tj
            last = T - 1
        else:
            incl = ts >= tj
            strict = ts > tj
            last = 0
        lw = lw_ref[0, 0]
        cum = _dot_sel_l(jnp.where(incl, 1.0, 0.0), lw)
        w_in = jnp.exp(cum)
        w_inv = jnp.exp(-cum)
        w_ex = jnp.exp(cum - lw)
        kap_a = kk_ref[0] * w_ex
        bet_a = bd_ref[0, 0] * w_inv
        khat_a = kd_ref[0, 0] * w_inv
        rho_a = r_ref[0] * w_in
        w_tot = w_in[last:last + 1, :]
        v_a = v_ref[0]
        for h in range(RW_HEADS):
            sl = slice(h * RW_HEAD, (h + 1) * RW_HEAD)
            ch.append(dict(d=d, h=h, sl=sl, incl=incl, strict=strict, y_ref=y_ref, w_tot=w_tot[:, sl],
                           kap=kap_a[:, sl], bet=bet_a[:, sl], khat=khat_a[:, sl], rho=rho_a[:, sl], v=v_a[:, sl]))
    def solve(chs):
        for q in chs:
            kr = jnp.concatenate([q['kap'], q['rho']], axis=0)
            bk = jnp.concatenate([q['bet'], q['khat']], axis=0)
            q['gram'] = _dot1(kr, bk, NT)
        for q in chs:
            gram = q.pop('gram')
            q['n'] = -jnp.where(q['strict'], gram[:T, :T], 0.0)
            l_k = jnp.where(q['strict'], gram[:T, T:], 0.0)
            q['m_b'] = jnp.where(q['incl'], gram[T:, :T], 0.0)
            m_k = jnp.where(q['incl'], gram[T:, T:], 0.0)
            q['lmv'] = _dot1(jnp.concatenate([l_k, m_k], axis=0), q['v'])
        for q in chs:
            q['x'] = jnp.concatenate([q['kap'], q['lmv'][:T]], axis=1)
        for lvl in range(6):
            mm = _dot3 if lvl < 2 else _dot1
            for q in chs:
                if lvl < 5:
                    nx = mm(q['n'], jnp.concatenate([q['x'], q['n']], axis=1))
                    q['n'], q['x'] = nx[:, 2 * RW_HEAD:], q['x'] + nx[:, :2 * RW_HEAD]
                else:
                    q['x'] = q['x'] + _dot1(q['n'], q['x'])
        for q in chs:
            q['p'] = _dot1(q['m_b'], q['x'])
            xb = _dot1(q['x'], q['bet'], TN)
            q['a_m'] = xb[:RW_HEAD]
            q['d_m'] = _dot1(q['v'], q['khat'], TN) - xb[RW_HEAD:]
        for q in chs:
            rho_p = q['rho'] - q['p'][:, :RW_HEAD]
            y_v = q['lmv'][T:] - q['p'][:, RW_HEAD:]
            s_old = s_scr[q['d'], q['h']]
            q['y_ref'][0, :, q['sl']] = _dot1(rho_p, s_old, NT) + y_v
            s_scr[q['d'], q['h']] = (s_old - _dot1(s_old, q['a_m']) + q['d_m']) * q['w_tot']

    for g0 in range(0, len(ch), RW_GROUP):
        solve(ch[g0:g0 + RW_GROUP])

    @pl.when(c == pl.num_programs(1) - 1)
    def _():
        sf_ref[0] = s_scr[...]


def _rw_core(st, r, v, kk, kd, bd, lw, s0):
    nb, L = st.nb, st.L
    nc = L // RW_T
    f1 = pl.BlockSpec((1, RW_T, D_GROUP), lambda b, c: (b, c, 0))
    b1 = pl.BlockSpec((1, RW_T, D_GROUP), lambda b, c: (b, nc - 1 - c, 0))
    f2 = pl.BlockSpec((1, 1, RW_T, D_GROUP), lambda b, c: (0, b, c, 0))
    b2 = pl.BlockSpec((1, 1, RW_T, D_GROUP), lambda b, c: (1, b, nc - 1 - c, 0))
    s_spec = pl.BlockSpec((1, 2, RW_HEADS, RW_HEAD, RW_HEAD), lambda b, c: (b, 0, 0, 0, 0))
    ys = jax.ShapeDtypeStruct((nb, L, D_GROUP), F32)
    return pl.pallas_call(
        _rw_core_kernel,
        grid=(nb, nc),
        in_specs=[f1, f1, f1, f2, f2, f2, b1, b1, b1, b2, b2, b2, s_spec],
        out_specs=[f1, b1, s_spec],
        out_shape=[ys, ys, jax.ShapeDtypeStruct((nb, 2, RW_HEADS, RW_HEAD, RW_HEAD), F32)],
        scratch_shapes=[pltpu.VMEM((2, RW_HEADS, RW_HEAD, RW_HEAD), F32)],
        compiler_params=_cparams("parallel", "arbitrary"),
        name="rwkv_core",
    )(r, v, kk, kd, bd, lw, r, v, kk, kd, bd, lw, s0)


RWO_TM = 512


def _rw_out_kernel(yf_ref, yb_ref, r_ref, v_ref, kd0_ref, kd1_ref, g_ref, lng_ref, lnb_ref, rk_ref, hones_ref, o_ref):
    y = yf_ref[...] + yb_ref[...]
    inv = 1.0 / RW_HEAD
    mu = _dot_sel_r(y, hones_ref[...]) * inv
    dv = y - mu
    var = _dot_sel_r(dv * dv, hones_ref[...]) * inv
    yn = dv * lax.rsqrt(var + RW_GN_EPS) * lng_ref[0] + lnb_ref[0]
    kmean = 0.5 * (kd0_ref[0] + kd1_ref[0])
    bonus = _dot_sel_r(r_ref[...] * kmean * rk_ref[0], hones_ref[...]) * v_ref[...]
    o_ref[...] = ((yn + bonus) * g_ref[...]).astype(BF16)


def _rw_out(st, yf, yb, r, v, kd, g, ln_g, ln_b, r_k, l):
    rows = st.rows
    flat = lambda a: a.reshape(rows, D_GROUP)
    row = pl.BlockSpec((RWO_TM, D_GROUP), lambda i: (i, 0))
    lay = pl.BlockSpec((1, 1, D_GROUP), lambda i: (l, 0, 0))
    kd2 = kd.reshape(2, rows, D_GROUP)
    return pl.pallas_call(
        _rw_out_kernel,
        grid=(rows // RWO_TM,),
        in_specs=[row, row, row, row,
                  pl.BlockSpec((1, RWO_TM, D_GROUP), lambda i: (0, i, 0)),
                  pl.BlockSpec((1, RWO_TM, D_GROUP), lambda i: (1, i, 0)),
                  row, lay, lay, lay, pl.BlockSpec((D_GROUP, D_GROUP), lambda i: (0, 0))],
        out_specs=row,
        out_shape=jax.ShapeDtypeStruct((rows, D_GROUP), BF16),
        compiler_params=_cparams("parallel"),
        name="rwkv_out",
    )(flat(yf), flat(yb), flat(r), flat(v), kd2, kd2, flat(g), ln_g.reshape(DEPTH, 1, D_GROUP),
      ln_b.reshape(DEPTH, 1, D_GROUP), r_k.reshape(DEPTH, 1, D_GROUP), _head_ones())


def _trunk_layer(st, x, l, mod4, p, states, ctx_kv):
    nb, L = st.nb, st.L
    z, zg = _zproj(st, x, mod4, p['w_in'], p['w_gate_cols'], l)
    z3 = z.reshape(nb, L, D_Z)

    ydir, sfr, sfi = _s5_scan(st, z3, p['s5_bq'], p['s5_cq'], p['s5_lamr'], p['s5_lami'],
                              states['s5_re'], states['s5_im'], l)
    y_s5 = _s5_out(st, z, ydir.reshape(2, st.rows, D_GROUP), p['s5_d'], p['s5_w_glu'], l)

    r, v, kk, g, kd, bd, lw = _rw_front(st, z3, p['rw_mu_prev'], p['rw_mu_next'], p['rw_w0'], p['rw_w2'],
                                        p['rw_a0'], p['rw_a2'], p['rw_g2'], p['rw_k_k'], p['rw_k_a'], l)
    yf, yb, rw_s = _rw_core(st, r, v, kk, kd, bd, lw, states['rw'])
    y_rw = _rw_out(st, yf, yb, r, v, kd, g, p['rw_ln_g'], p['rw_ln_b'], p['rw_r_k'], l)

    if ctx_kv is None:
        y_na, nk, nv = _na_ctx(st, z3)
    else:
        y_na = _na_nbr(st, z3, ctx_kv[0], ctx_kv[1], p['na_bias'], l)
        nk = nv = None

    y_ml, ml_c, ml_n, ml_m = _ml_mixer(st, z3, zg, p['ml_i_bias'], p['ml_f_bias'], p['ml_ln_g'], p['ml_ln_b'],
                                       states['ml_c'], states['ml_n'], states['ml_m'], l)

    ys = (y_s5, y_rw, y_na.reshape(st.rows, D_GROUP), y_ml.reshape(st.rows, D_GROUP))
    tail = _oproj(st, ys, x, mod4, p['w_out'], p['ln1_g'], p['ln1_b'], p['w_router'], p['b_router'], l)
    return tail, (nk, nv, sfr, sfi, rw_s, ml_c, ml_n, ml_m)


def _moe_and_ln2(tails, mod4, p, l):
    xs = []
    for st, (x1, h2, gates, keep) in zip((PROMPT, LATENT), tails):
        f = _moe_sorted(st, h2, gates, keep, p['moe_w_gate'], p['moe_w_up'], p['moe_w_down'], l)
        xs.append(_ln2f(st, x1, f, mod4, p['ln2_g'], p['ln2_b'], l))
    return xs


def kernel(x_prompt, x_sample, cache_nat_k, cache_nat_v, state_s5_re, state_s5_im, state_rwkv, state_mlstm_c, state_mlstm_n, state_mlstm_m, c, c_ctx, w_mod, b_mod, w_in, w_out, s5_lam_re, s5_lam_im, s5_log_step, s5_b_re, s5_b_im, s5_c_re, s5_c_im, s5_d, s5_w_glu, rw_mu_prev, rw_mu_next, rw_w0, rw_w2, rw_a0, rw_a2, rw_g2, rw_k_k, rw_k_a, rw_r_k, rw_ln_g, rw_ln_b, na_rpb, ml_i_bias, ml_f_bias, ml_ln_g, ml_ln_b, ln1_g, ln1_b, ln2_g, ln2_b, w_router, b_router, moe_w_gate, moe_w_up, moe_w_down):
    dt = x_prompt.dtype
    cond = jnp.concatenate([c_ctx[None, :], c, jnp.zeros((MOD_ROWS - 1 - DEC_BATCH, D_MODEL), F32)], axis=0)
    mod4 = _modulation(cond, w_mod, b_mod).reshape(DEPTH, MOD_ROWS, 1, 6 * D_MODEL)

    lbr, lbi, bbr, bbi = _s5_prep(s5_lam_re, s5_lam_im, s5_log_step, s5_b_re, s5_b_im)
    s5_bq, s5_cq, s5_lamr, s5_lami = _s5_block_params(lbr, lbi, bbr, bbi, s5_c_re, s5_c_im)
    w_in_b = w_in.astype(BF16)
    p = dict(w_in=w_in_b, w_gate_cols=w_in_b[:, :, D_Z:], w_out=w_out.astype(BF16),
             s5_bq=s5_bq, s5_cq=s5_cq, s5_lamr=s5_lamr, s5_lami=s5_lami, s5_d=s5_d, s5_w_glu=s5_w_glu,
             rw_mu_prev=rw_mu_prev, rw_mu_next=rw_mu_next, rw_w0=rw_w0, rw_w2=rw_w2, rw_a0=rw_a0, rw_a2=rw_a2,
             rw_g2=rw_g2, rw_k_k=rw_k_k, rw_k_a=rw_k_a, rw_r_k=rw_r_k, rw_ln_g=rw_ln_g, rw_ln_b=rw_ln_b,
             na_bias=_na_bias_table(na_rpb), ml_i_bias=ml_i_bias, ml_f_bias=ml_f_bias, ml_ln_g=ml_ln_g,
             ml_ln_b=ml_ln_b, ln1_g=ln1_g, ln1_b=ln1_b, ln2_g=ln2_g, ln2_b=ln2_b, w_router=w_router,
             b_router=b_router, moe_w_gate=moe_w_gate.astype(BF16), moe_w_up=moe_w_up.astype(BF16),
             moe_w_down=moe_w_down.astype(BF16))

    gp = BATCH // S5_SEQS
    zero_states = dict(
        s5_re=jnp.zeros((gp, 2, S5_SEQS, N_S5), F32), s5_im=jnp.zeros((gp, 2, S5_SEQS, N_S5), F32),
        rw=jnp.zeros((BATCH, 2, RW_HEADS, RW_HEAD, RW_HEAD), F32),
        ml_c=jnp.zeros((BATCH, 2, ML_HEADS, ML_HEAD, ML_HEAD), F32),
        ml_n=jnp.zeros((BATCH, ML_HEADS, 2, ML_HEAD), F32), ml_m=jnp.zeros((BATCH, ML_HEADS, 2, 1), F32))

    xp = x_prompt.reshape(PROMPT.rows, D_MODEL)
    xs = x_sample.reshape(LATENT.rows, D_MODEL)
    outs = [[] for _ in range(8)]
    for l in range(DEPTH):
        tail_p, ctx_t = _trunk_layer(PROMPT, xp, l, mod4, p, zero_states, None)
        for acc, t in zip(outs, ctx_t):
            acc.append(t)
        lat_states = dict(
            s5_re=state_s5_re[:, l].reshape(DEC_BATCH, 2, N_S5).transpose(1, 0, 2)[None],
            s5_im=state_s5_im[:, l].reshape(DEC_BATCH, 2, N_S5).transpose(1, 0, 2)[None],
            rw=state_rwkv[:, l], ml_c=state_mlstm_c[:, l],
            ml_n=state_mlstm_n[:, l].transpose(0, 2, 1, 3), ml_m=state_mlstm_m[:, l].transpose(0, 2, 1)[..., None])
        tail_s, _ = _trunk_layer(LATENT, xs, l, mod4, p, lat_states, (cache_nat_k, cache_nat_v))
        xp, xs = _moe_and_ln2((tail_p, tail_s), mod4, p, l)

    nk, nv, s5r, s5i, rw, mc, mn, mm = [jnp.stack(t, axis=1) for t in outs]

    def s5_state(t):
        return t.transpose(0, 3, 1, 2, 4).reshape(BATCH, DEPTH, 2, S5_GROUPS, S5_STATE)

    return (xp.reshape(BATCH, SEQ, D_MODEL), xs.reshape(DEC_BATCH, DEC_SEQ, D_MODEL),
            nk, nv, s5_state(s5r).astype(dt), s5_state(s5i).astype(dt), rw.astype(dt), mc.astype(dt),
            mn.transpose(0, 1, 3, 2, 4).astype(dt), mm[..., 0].transpose(0, 1, 3, 2).astype(dt))
```

```python
import functools
import math

import numpy as np
import jax
import jax.numpy as jnp
from jax import lax
from jax.experimental import pallas as pl
from jax.experimental.pallas import tpu as pltpu

F32 = jnp.float32
BF16 = jnp.bfloat16

D_MODEL = 2048
BATCH = 16
SEQ = 256
DEPTH = 4
DEC_BATCH = 8
DEC_SEQ = 1024
PAST_LEN = 256
GRID_W = 64
D_GROUP = D_MODEL // 4
S5_CH = 16
S5_GROUPS = D_GROUP // S5_CH
S5_STATE = 64
RW_HEAD = 64
RW_HEADS = D_GROUP // RW_HEAD
RW_LORA_W = 64
RW_LORA_A = 64
RW_LORA_G = 128
RW_GN_EPS = 64e-5
NA_HEAD = 64
NA_HEADS = D_GROUP // NA_HEAD
NA_WIN_H = 8
NA_WIN_W = 16
NEG_INF = -1e30
ML_HEAD = 128
ML_HEADS = D_GROUP // ML_HEAD
ML_CHUNK = 256
ML_GN_EPS = 1e-5
ROPE_BASE = 10000.0
N_EXPERTS = 16
N_EXPERT_GROUPS = 4
EXPERTS_PER_GROUP = N_EXPERTS // N_EXPERT_GROUPS
D_EXPERT = 512
DEEPNORM_ALPHA = (2 * DEPTH) ** 0.25
LN_EPS = 1e-5
S5_IN = D_GROUP
RW_IN = 3 * D_GROUP + RW_LORA_W + RW_LORA_A + RW_LORA_G
NA_IN = 3 * D_GROUP
ML_IN = 4 * D_GROUP + 4 * ML_HEADS
D_IN = S5_IN + RW_IN + NA_IN + ML_IN
N_GATE = 4 * ML_HEADS
D_Z = D_IN - N_GATE
RW_OFF = S5_IN
NA_OFF = S5_IN + RW_IN
ML_OFF = NA_OFF + NA_IN
MOD_ROWS = 16
LANE = 128
SUB = 8

VMEM_LIMIT = 56 * 1024 * 1024


def _cparams(*sem):
    return pltpu.CompilerParams(dimension_semantics=sem, vmem_limit_bytes=VMEM_LIMIT)


def _dg(a, b, dims):
    return lax.dot_general(a, b, (dims, ((), ())), preferred_element_type=F32)


NN = ((1,), (0,))
NT = ((1,), (1,))
TN = ((0,), (0,))


def _dot1(a, b, dims=NN):
    return _dg(a.astype(BF16), b.astype(BF16), dims)


def _split(x):
    hi = x.astype(BF16)
    lo = (x - hi.astype(F32)).astype(BF16)
    return hi, lo


def _split3(x):
    hi = x.astype(BF16)
    r = x - hi.astype(F32)
    mid = r.astype(BF16)
    lo = (r - mid.astype(F32)).astype(BF16)
    return hi, mid, lo


def _dot3(a, b, dims=NN):
    ah, al = _split(a)
    bh, bl = _split(b)
    return _dg(ah, bh, dims) + (_dg(ah, bl, dims) + _dg(al, bh, dims))


def _dot_sel_l(sel, x, dims=NN):
    s = sel.astype(BF16)
    hi, mid, lo = _split3(x)
    return _dg(s, hi, dims) + (_dg(s, mid, dims) + _dg(s, lo, dims))


def _seg_sum(x, sel):
    s = sel.astype(BF16)
    hi, lo = _split(x)
    return _dg(hi, s, NN) + _dg(lo, s, NN)


def _sigmoid(x):
    return 1.0 / (1.0 + jnp.exp(-x))


def _silu(x):
    return x * _sigmoid(x)


def _softplus(x):
    return jnp.maximum(x, 0.0) + jnp.log(1.0 + jnp.exp(-jnp.abs(x)))


def _iota(shape, dim):
    return lax.broadcasted_iota(jnp.int32, shape, dim)


class _Stream:
    def __init__(self, nb, L, latent):
        self.nb, self.L, self.latent = nb, L, latent
        self.rows = nb * L

    def mod_row(self, tile, tile_rows):
        if not self.latent:
            return 0
        return 1 + (tile * tile_rows) // self.L


PROMPT = _Stream(BATCH, SEQ, False)
LATENT = _Stream(DEC_BATCH, DEC_SEQ, True)


MOD_TN = 768


def _mod_kernel(cond_ref, w_ref, b_ref, o_ref):
    c = _silu(cond_ref[...])
    o_ref[0] = _dot3(c, w_ref[0]) + b_ref[0]


def _modulation(cond, w_mod, b_mod):
    n = 6 * D_MODEL
    return pl.pallas_call(
        _mod_kernel,
        grid=(DEPTH, n // MOD_TN),
        in_specs=[pl.BlockSpec((MOD_ROWS, D_MODEL), lambda l, j: (0, 0)),
                  pl.BlockSpec((1, D_MODEL, MOD_TN), lambda l, j: (l, 0, j)),
                  pl.BlockSpec((1, 1, MOD_TN), lambda l, j: (l, 0, j))],
        out_specs=pl.BlockSpec((1, MOD_ROWS, MOD_TN), lambda l, j: (l, 0, j)),
        out_shape=jax.ShapeDtypeStruct((DEPTH, MOD_ROWS, n), F32),
        compiler_params=_cparams("parallel", "parallel"),
        name="modulation",
    )(cond, w_mod, b_mod.reshape(DEPTH, 1, n))


ZP_TM = 1024
ZP_TN = 256


def _zproj_kernel(x_ref, mod_ref, w_ref, wg_ref, z_ref, zg_ref, h_scr):
    j = pl.program_id(1)

    @pl.when(j == 0)
    def _():
        m = mod_ref[0, 0]
        shift1 = m[:, 0:D_MODEL]
        scale1 = m[:, D_MODEL:2 * D_MODEL]
        h = (x_ref[...] * (1.0 + scale1) + shift1).astype(BF16)
        h_scr[...] = h
        zg_ref[...] = jnp.dot(h, wg_ref[0], preferred_element_type=F32)

    z_ref[...] = jnp.dot(h_scr[...], w_ref[0], preferred_element_type=F32)


def _zproj(st, x, mod4, w_in, w_gate_cols, l):
    return pl.pallas_call(
        _zproj_kernel,
        grid=(st.rows // ZP_TM, D_Z // ZP_TN),
        in_specs=[pl.BlockSpec((ZP_TM, D_MODEL), lambda i, j: (i, 0)),
                  pl.BlockSpec((1, 1, 1, 6 * D_MODEL), lambda i, j: (l, st.mod_row(i, ZP_TM), 0, 0)),
                  pl.BlockSpec((1, D_MODEL, ZP_TN), lambda i, j: (l, 0, j)),
                  pl.BlockSpec((1, D_MODEL, N_GATE), lambda i, j: (l, 0, 0))],
        out_specs=[pl.BlockSpec((ZP_TM, ZP_TN), lambda i, j: (i, j)),
                   pl.BlockSpec((ZP_TM, N_GATE), lambda i, j: (i, 0))],
        out_shape=[jax.ShapeDtypeStruct((st.rows, D_Z), F32),
                   jax.ShapeDtypeStruct((st.rows, N_GATE), F32)],
        scratch_shapes=[pltpu.VMEM((ZP_TM, D_MODEL), BF16)],
        compiler_params=_cparams("parallel", "arbitrary"),
        name="zproj",
    )(x, mod4, w_in, w_gate_cols)


OP_TM = 512


def _layer_norm(v, g, b):
    mu = jnp.mean(v, axis=-1, keepdims=True)
    d = v - mu
    var = jnp.mean(d * d, axis=-1, keepdims=True)
    return d * lax.rsqrt(var + LN_EPS) * g + b


def _route(scores, b_router):
    sel = scores + b_router
    s = [sel[e:e + 1, :] for e in range(N_EXPERTS)]
    in_top2 = []
    for g in range(N_EXPERT_GROUPS):
        for i in range(EXPERTS_PER_GROUP):
            e = g * EXPERTS_PER_GROUP + i
            cnt = jnp.zeros_like(s[e])
            for jj in range(EXPERTS_PER_GROUP):
                if jj == i:
                    continue
                o = g * EXPERTS_PER_GROUP + jj
                beats = (s[o] > s[e]) if jj > i else (s[o] >= s[e])
                cnt = cnt + jnp.where(beats, 1.0, 0.0)
            in_top2.append(cnt < 2.0)
    grp = []
    for g in range(N_EXPERT_GROUPS):
        tot = jnp.zeros_like(s[0])
        for i in range(EXPERTS_PER_GROUP):
            e = g * EXPERTS_PER_GROUP + i
            tot = tot + jnp.where(in_top2[e], s[e], 0.0)
        grp.append(tot)
    keep_rows = []
    for g in range(N_EXPERT_GROUPS):
        cnt = jnp.zeros_like(s[0])
        for o in range(N_EXPERT_GROUPS):
            if o == g:
                continue
            beats = (grp[o] > grp[g]) if o > g else (grp[o] >= grp[g])
            cnt = cnt + jnp.where(beats, 1.0, 0.0)
        best = cnt < 1.0
        for i in range(EXPERTS_PER_GROUP):
            e = g * EXPERTS_PER_GROUP + i
            keep_rows.append(jnp.where(best, jnp.where(in_top2[e], 1.0, 0.0), 0.0))
    keep = jnp.concatenate(keep_rows, axis=0)
    picked = scores * keep
    return picked / jnp.sum(picked, axis=0, keepdims=True), keep


def _oproj_kernel(y0_ref, y1_ref, y2_ref, y3_ref, w_ref, x_ref, mod_ref, g_ref, b_ref, wr_ref, br_ref,
                  x1_ref, h2_ref, gate_ref, keep_ref):
    acc = None
    for k, yr in enumerate((y0_ref, y1_ref, y2_ref, y3_ref)):
        part = jnp.dot(yr[...], w_ref[0, k * D_GROUP:(k + 1) * D_GROUP, :], preferred_element_type=F32)
        acc = part if acc is None else acc + part
    m = mod_ref[0, 0]
    gate1 = m[:, 2 * D_MODEL:3 * D_MODEL]
    shift2 = m[:, 3 * D_MODEL:4 * D_MODEL]
    scale2 = m[:, 4 * D_MODEL:5 * D_MODEL]
    x1 = _layer_norm(DEEPNORM_ALPHA * x_ref[...] + gate1 * acc, g_ref[0], b_ref[0])
    x1_ref[...] = x1
    h2 = x1 * (1.0 + scale2) + shift2
    h2_ref[...] = h2.astype(BF16)
    scores_t = _sigmoid(_dot3(wr_ref[...], h2, NT))
    gates_t, keep_t = _route(scores_t, br_ref[...])
    gate_ref[...] = gates_t.T
    keep_ref[...] = keep_t.T


def _oproj(st, ys, x, mod4, w_out_b, ln_g, ln_b, w_router, b_router, l):
    row = lambda i: (i, 0)
    return pl.pallas_call(
        _oproj_kernel,
        grid=(st.rows // OP_TM,),
        in_specs=[pl.BlockSpec((OP_TM, D_GROUP), row)] * 4 + [
            pl.BlockSpec((1, D_MODEL, D_MODEL), lambda i: (l, 0, 0)),
            pl.BlockSpec((OP_TM, D_MODEL), row),
            pl.BlockSpec((1, 1, 1, 6 * D_MODEL), lambda i: (l, st.mod_row(i, OP_TM), 0, 0)),
            pl.BlockSpec((1, 1, D_MODEL), lambda i: (l, 0, 0)),
            pl.BlockSpec((1, 1, D_MODEL), lambda i: (l, 0, 0)),
            pl.BlockSpec((N_EXPERTS, D_MODEL), lambda i: (0, 0)),
            pl.BlockSpec((N_EXPERTS, 1), lambda i: (0, 0))],
        out_specs=[pl.BlockSpec((OP_TM, D_MODEL), row),
                   pl.BlockSpec((OP_TM, D_MODEL), row),
                   pl.BlockSpec((OP_TM, N_EXPERTS), row),
                   pl.BlockSpec((OP_TM, N_EXPERTS), row)],
        out_shape=[jax.ShapeDtypeStruct((st.rows, D_MODEL), F32),
                   jax.ShapeDtypeStruct((st.rows, D_MODEL), BF16),
                   jax.ShapeDtypeStruct((st.rows, N_EXPERTS), F32),
                   jax.ShapeDtypeStruct((st.rows, N_EXPERTS), F32)],
        compiler_params=_cparams("parallel"),
        name="oproj_ln1_router",
    )(*ys, w_out_b, x, mod4, ln_g.reshape(DEPTH, 1, D_MODEL), ln_b.reshape(DEPTH, 1, D_MODEL),
      w_router.T, b_router.reshape(N_EXPERTS, 1))


LN_TM = 512


GS_BLK = 1024
GS_CAP = 320
GS_ALIGN = 2 * SUB
GS_COLS = 512
GS_STEP = 256
GS_VMEM_LIMIT = 60 * 1024 * 1024
GS_RCH = 3 * LANE
GS_ROWS = pl.cdiv(GS_BLK + N_EXPERT_GROUPS * GS_ALIGN, GS_RCH) * GS_RCH


def _moe_sorted_kernel(h_ref, gate_ref, keep_ref, wg_ref, wu_ref, wd_ref, f_ref, xs_scr, gs_scr, pt_scr, acc_scr,
                       seg_smem):
    e = pl.program_id(1)

    @pl.when(e == 0)
    def _():
        keep = keep_ref[...]
        lane = _iota(keep.shape, 1)
        member = []
        for g in range(N_EXPERT_GROUPS):
            in_g = (lane >= g * EXPERTS_PER_GROUP) & (lane < (g + 1) * EXPERTS_PER_GROUP)
            member.append(jnp.minimum(jnp.sum(jnp.where(in_g, keep, 0.0), axis=-1, keepdims=True), 1.0))
        lane4 = _iota((GS_BLK, LANE), 1)
        onehot = jnp.zeros((GS_BLK, LANE), F32)
        for g in range(N_EXPERT_GROUPS):
            onehot = onehot + jnp.where(lane4 == g, member[g], 0.0)
        counts = jnp.sum(onehot, axis=0, keepdims=True)
        starts = []
        start = jnp.zeros((1, 1), F32)
        for g in range(N_EXPERT_GROUPS):
            cnt_g = counts[:, g:g + 1]
            starts.append(start)
            seg_smem[2 * g] = jnp.sum(start).astype(jnp.int32)
            seg_smem[2 * g + 1] = jnp.sum(cnt_g).astype(jnp.int32)
            start = start + jnp.ceil(cnt_g * (1.0 / GS_ALIGN)) * GS_ALIGN
        onehot_b = onehot.astype(BF16)
        for r0 in range(0, GS_BLK, GS_STEP):
            tri = jnp.where(_iota((GS_STEP, GS_BLK), 1) <= _iota((GS_STEP, GS_BLK), 0) + r0, 1.0, 0.0).astype(BF16)
            csum = jnp.dot(tri, onehot_b, preferred_element_type=F32)
            dest = jnp.zeros((GS_STEP, 1), F32)
            for g in range(N_EXPERT_GROUPS):
                dest = dest + member[g][r0:r0 + GS_STEP] * (starts[g] + csum[:, g:g + 1] - 1.0)
            pt_scr[r0:r0 + GS_STEP, :] = jnp.where(
                _iota((GS_STEP, GS_ROWS), 1) == dest.astype(jnp.int32), 1.0, 0.0).astype(BF16)
        for c0 in range(0, GS_ROWS, GS_RCH):
            pt_c = pt_scr[:, c0:c0 + GS_RCH]
            xs_scr[c0:c0 + GS_RCH, :] = _dg(pt_c, h_ref[...], TN).astype(BF16)
            gs_scr[c0:c0 + GS_RCH, :] = _dot_sel_l(pt_c, gate_ref[...], TN)
        acc_scr[...] = jnp.zeros_like(acc_scr)

    g = e // EXPERTS_PER_GROUP
    seg_start = seg_smem[2 * g]
    seg_len = seg_smem[2 * g + 1]
    wg = wg_ref[0, 0]
    wu = wu_ref[0, 0]
    wd = wd_ref[0, 0]

    def window(c, carry):
        first = seg_start + c * GS_CAP
        start = jnp.minimum(first, GS_ROWS - GS_CAP)
        rows = pl.ds(pl.multiple_of(start, GS_ALIGN), GS_CAP)
        x = xs_scr[rows, :]
        a = jnp.dot(x, wg, preferred_element_type=F32)
        u = jnp.dot(x, wu, preferred_element_type=F32)
        gates = gs_scr[rows, :]
        ge = jnp.sum(jnp.where(_iota(gates.shape, 1) == e, gates, 0.0), axis=-1, keepdims=True)
        ge = jnp.where(_iota((GS_CAP, 1), 0) + start >= first, ge, 0.0)
        hid = (_silu(a) * u * ge).astype(BF16)
        acc_scr[rows, :] += jnp.dot(hid, wd, preferred_element_type=F32)
        return carry

    lax.fori_loop(0, (seg_len + GS_CAP - 1) // GS_CAP, window, 0)

    @pl.when(e == N_EXPERTS - 1)
    def _():
        pt = pt_scr[...]
        for c0 in range(0, D_MODEL, GS_COLS):
            acc_b = acc_scr[:, c0:c0 + GS_COLS].astype(BF16)
            f_ref[:, c0:c0 + GS_COLS] = jnp.dot(pt, acc_b, preferred_element_type=F32).astype(BF16)


def _moe_sorted(st, h2, gates, keep, wg_b, wu_b, wd_b, l):
    row = lambda i, e: (i, 0)
    wspec = lambda a, b: pl.BlockSpec((1, 1, a, b), lambda i, e: (l, e, 0, 0))
    return pl.pallas_call(
        _moe_sorted_kernel,
        grid=(st.rows // GS_BLK, N_EXPERTS),
        in_specs=[pl.BlockSpec((GS_BLK, D_MODEL), row, pipeline_mode=pl.Buffered(1)),
                  pl.BlockSpec((GS_BLK, N_EXPERTS), row), pl.BlockSpec((GS_BLK, N_EXPERTS), row),
                  wspec(D_MODEL, D_EXPERT), wspec(D_MODEL, D_EXPERT), wspec(D_EXPERT, D_MODEL)],
        out_specs=pl.BlockSpec((GS_BLK, D_MODEL), row),
        out_shape=jax.ShapeDtypeStruct((st.rows, D_MODEL), BF16),
        scratch_shapes=[pltpu.VMEM((GS_ROWS, D_MODEL), BF16), pltpu.VMEM((GS_ROWS, N_EXPERTS), F32),
                        pltpu.VMEM((GS_BLK, GS_ROWS), BF16), pltpu.VMEM((GS_ROWS, D_MODEL), F32),
                        pltpu.SMEM((2 * N_EXPERT_GROUPS,), jnp.int32)],
        compiler_params=pltpu.CompilerParams(dimension_semantics=("parallel", "arbitrary"),
                                             vmem_limit_bytes=GS_VMEM_LIMIT),
        name="moe_group_sorted",
    )(h2, gates, keep, wg_b, wu_b, wd_b)


def _ln2f_kernel(x_ref, f_ref, mod_ref, g_ref, b_ref, o_ref):
    gate2 = mod_ref[0, 0][:, 5 * D_MODEL:6 * D_MODEL]
    o_ref[...] = _layer_norm(DEEPNORM_ALPHA * x_ref[...] + gate2 * f_ref[...].astype(F32), g_ref[0], b_ref[0])


def _ln2f(st, x1, f, mod4, ln_g, ln_b, l):
    row = lambda i: (i, 0)
    return pl.pallas_call(
        _ln2f_kernel,
        grid=(st.rows // LN_TM,),
        in_specs=[pl.BlockSpec((LN_TM, D_MODEL), row),
                  pl.BlockSpec((LN_TM, D_MODEL), row),
                  pl.BlockSpec((1, 1, 1, 6 * D_MODEL), lambda i: (l, st.mod_row(i, LN_TM), 0, 0)),
                  pl.BlockSpec((1, 1, D_MODEL), lambda i: (l, 0, 0)),
                  pl.BlockSpec((1, 1, D_MODEL), lambda i: (l, 0, 0))],
        out_specs=pl.BlockSpec((LN_TM, D_MODEL), row),
        out_shape=jax.ShapeDtypeStruct((st.rows, D_MODEL), F32),
        compiler_params=_cparams("parallel"),
        name="ln2",
    )(x1, f, mod4, ln_g.reshape(DEPTH, 1, D_MODEL), ln_b.reshape(DEPTH, 1, D_MODEL))


S5_TC = 128
S5_SEQS = SUB
S5_Q = 4
S5_QS = S5_GROUPS // S5_Q * S5_STATE
N_S5 = S5_GROUPS * S5_STATE


def _s5_prep_kernel(lr_ref, li_ref, ls_ref, br_ref, bi_ref, e_ref, lbr_ref, lbi_ref, bbr_ref, bbi_ref):
    lr = lr_ref[0]
    li = li_ref[0]
    dt = jnp.exp(ls_ref[0])
    mag = jnp.exp(lr * dt)
    ang = li * dt
    ar = mag * jnp.cos(ang)
    ai = mag * jnp.sin(ang)
    lbr_ref[0] = ar
    lbi_ref[0] = ai
    den = lr * lr + li * li
    nr = ar - 1.0
    cr = (nr * lr + ai * li) / den
    ci = (ai * lr - nr * li) / den
    cr = _dot_sel_l(e_ref[...], cr)
    ci = _dot_sel_l(e_ref[...], ci)
    bre = br_ref[0]
    bim = bi_ref[0]
    bbr_ref[0] = cr * bre - ci * bim
    bbi_ref[0] = cr * bim + ci * bre


def _s5_prep(lam_re, lam_im, log_step, b_re, b_im):
    d2 = DEPTH * 2
    g, p, h = S5_GROUPS, S5_STATE, S5_CH
    bt_re = jnp.swapaxes(b_re, -1, -2).reshape(d2, g * h, p)
    bt_im = jnp.swapaxes(b_im, -1, -2).reshape(d2, g * h, p)
    expand = jnp.asarray(np.kron(np.eye(g, dtype=np.float32), np.ones((h, 1), np.float32)))
    spec_gp = pl.BlockSpec((1, g, p), lambda i: (i, 0, 0))
    spec_b = pl.BlockSpec((1, g * h, p), lambda i: (i, 0, 0))
    lbr, lbi, bbr, bbi = pl.pallas_call(
        _s5_prep_kernel,
        grid=(d2,),
        in_specs=[spec_gp, spec_gp, pl.BlockSpec((1, g, 1), lambda i: (i, 0, 0)), spec_b, spec_b,
                  pl.BlockSpec((g * h, g), lambda i: (0, 0))],
        out_specs=[spec_gp, spec_gp, spec_b, spec_b],
        out_shape=[jax.ShapeDtypeStruct((d2, g, p), F32)] * 2 + [jax.ShapeDtypeStruct((d2, g * h, p), F32)] * 2,
        compiler_params=_cparams("parallel"),
        name="s5_prep",
    )(lam_re.reshape(d2, g, p), lam_im.reshape(d2, g, p), log_step.reshape(d2, g, 1), bt_re, bt_im, expand)
    return lbr, lbi, bbr, bbi


def _s5_block_params(lbr, lbi, bbr, bbi, c_re, c_im):
    d2 = DEPTH * 2
    gq = S5_GROUPS // S5_Q
    eye = jnp.eye(gq, dtype=F32)

    def b_blocks(b):
        b = b.reshape(d2, S5_Q, gq, S5_CH, S5_STATE)
        return jnp.einsum('djghp,gk->djghkp', b, eye).reshape(d2, S5_Q, gq * S5_CH, gq * S5_STATE)

    def c_blocks(c):
        c = c.reshape(d2, S5_Q, gq, S5_CH, S5_STATE)
        return jnp.einsum('djghp,gk->djgpkh', c, eye).reshape(d2, S5_Q, gq * S5_STATE, gq * S5_CH)

    bq = jnp.concatenate([b_blocks(bbr), b_blocks(bbi)], axis=-1)
    cq = jnp.concatenate([c_blocks(c_re), -c_blocks(c_im)], axis=-2)
    return bq, cq, lbr.reshape(d2, 1, N_S5), lbi.reshape(d2, 1, N_S5)


def _s5_scan_kernel(u_ref, bq_ref, cq_ref, lr_ref, li_ref, s0r_ref, s0i_ref, y_ref, sfr_ref, sfi_ref,
                    utb, bur, bui, ytb, sr_scr, si_scr):
    d = pl.program_id(1)
    c = pl.program_id(2)
    nrow = S5_TC * S5_SEQS
    cw = D_GROUP // S5_Q

    @pl.when(c == 0)
    def _():
        sr_scr[...] = s0r_ref[0, 0]
        si_scr[...] = s0i_ref[0, 0]

    for b in range(S5_SEQS):
        for j in range(S5_Q):
            utb[j, pl.ds(b, S5_TC, stride=S5_SEQS), :] = u_ref[b, :, j * cw:(j + 1) * cw]
    for j in range(S5_Q):
        bu = _dot1(utb[j], bq_ref[0, j])
        bur[:, j * S5_QS:(j + 1) * S5_QS] = bu[:, :S5_QS]
        bui[:, j * S5_QS:(j + 1) * S5_QS] = bu[:, S5_QS:]
    for j in range(S5_Q):
        sl = slice(j * S5_QS, (j + 1) * S5_QS)
        lam_r = jnp.broadcast_to(lr_ref[0][:, sl], (S5_SEQS, S5_QS))
        lam_i = jnp.broadcast_to(li_ref[0][:, sl], (S5_SEQS, S5_QS))

        def step(t, carry, sl=sl, lam_r=lam_r, lam_i=lam_i):
            sr, si = carry
            te = jnp.where(d == 0, t, S5_TC - 1 - t)
            rows = pl.ds(pl.multiple_of(te * S5_SEQS, S5_SEQS), S5_SEQS)
            nr = lam_r * sr - lam_i * si + bur[rows, sl]
            ni = lam_r * si + lam_i * sr + bui[rows, sl]
            bur[rows, sl] = nr
            bui[rows, sl] = ni
            return nr, ni

        sr, si = lax.fori_loop(0, S5_TC, step, (sr_scr[:, sl], si_scr[:, sl]))
        sr_scr[:, sl] = sr
        si_scr[:, sl] = si
    for j in range(S5_Q):
        sl = slice(j * S5_QS, (j + 1) * S5_QS)
        yj = _dot1(bur[:, sl], cq_ref[0, j, :S5_QS, :]) + _dot1(bui[:, sl], cq_ref[0, j, S5_QS:, :])
        ytb[j] = yj
    for b in range(S5_SEQS):
        for j in range(S5_Q):
            y_ref[0, b, :, j * cw:(j + 1) * cw] = ytb[j, pl.ds(b, S5_TC, stride=S5_SEQS), :]

    @pl.when(c == pl.num_programs(2) - 1)
    def _():
        sfr_ref[0, 0] = sr_scr[...]
        sfi_ref[0, 0] = si_scr[...]


def _s5_scan(st, z3, bq, cq, lamr, lami, s0r, s0i, l):
    ng, nc = st.nb // S5_SEQS, st.L // S5_TC
    nrow = S5_TC * S5_SEQS
    chunk = lambda d, c: c + d * (nc - 1 - 2 * c)
    par = lambda g, d, c: (2 * l + d, 0, 0, 0)
    st_spec = pl.BlockSpec((1, 1, S5_SEQS, N_S5), lambda g, d, c: (g, d, 0, 0))
    return pl.pallas_call(
        _s5_scan_kernel,
        grid=(ng, 2, nc),
        in_specs=[pl.BlockSpec((S5_SEQS, S5_TC, D_GROUP), lambda g, d, c: (g, chunk(d, c), 0)),
                  pl.BlockSpec((1, S5_Q, D_GROUP // S5_Q, 2 * S5_QS), par),
                  pl.BlockSpec((1, S5_Q, 2 * S5_QS, D_GROUP // S5_Q), par),
                  pl.BlockSpec((1, 1, N_S5), lambda g, d, c: (2 * l + d, 0, 0)),
                  pl.BlockSpec((1, 1, N_S5), lambda g, d, c: (2 * l + d, 0, 0)),
                  st_spec, st_spec],
        out_specs=[pl.BlockSpec((1, S5_SEQS, S5_TC, D_GROUP), lambda g, d, c: (d, g, chunk(d, c), 0)),
                   st_spec, st_spec],
        out_shape=[jax.ShapeDtypeStruct((2, st.nb, st.L, D_GROUP), F32),
                   jax.ShapeDtypeStruct((ng, 2, S5_SEQS, N_S5), F32),
                   jax.ShapeDtypeStruct((ng, 2, S5_SEQS, N_S5), F32)],
        scratch_shapes=[pltpu.VMEM((S5_Q, nrow, LANE), F32), pltpu.VMEM((nrow, N_S5), F32),
                        pltpu.VMEM((nrow, N_S5), F32), pltpu.VMEM((S5_Q, nrow, LANE), F32),
                        pltpu.VMEM((S5_SEQS, N_S5), F32), pltpu.VMEM((S5_SEQS, N_S5), F32)],
        compiler_params=_cparams("parallel", "arbitrary", "arbitrary"),
        name="s5_scan",
    )(z3, bq, cq, lamr, lami, s0r, s0i)


S5_TM = 512


def _gelu_tanh(x):
    return 0.5 * x * (1.0 + jnp.tanh(math.sqrt(2.0 / math.pi) * (x + 0.044715 * (x * x * x))))


def _s5_out_kernel(u_ref, yf_ref, yb_ref, d_ref, w_ref, o_ref):
    y = yf_ref[0] + yb_ref[0] + d_ref[0] * u_ref[...]
    y = _gelu_tanh(y)
    o_ref[...] = (y * _sigmoid(_dot1(y, w_ref[0]))).astype(BF16)


def _s5_out(st, z, ydir, d_skip, w_glu, l):
    return pl.pallas_call(
        _s5_out_kernel,
        grid=(st.rows // S5_TM,),
        in_specs=[pl.BlockSpec((S5_TM, D_GROUP), lambda i: (i, 0)),
                  pl.BlockSpec((1, S5_TM, D_GROUP), lambda i: (0, i, 0)),
                  pl.BlockSpec((1, S5_TM, D_GROUP), lambda i: (1, i, 0)),
                  pl.BlockSpec((1, 1, D_GROUP), lambda i: (l, 0, 0)),
                  pl.BlockSpec((1, D_GROUP, D_GROUP), lambda i: (l, 0, 0))],
        out_specs=pl.BlockSpec((S5_TM, D_GROUP), lambda i: (i, 0)),
        out_shape=jax.ShapeDtypeStruct((st.rows, D_GROUP), BF16),
        compiler_params=_cparams("parallel"),
        name="s5_out",
    )(z, ydir, ydir, d_skip.reshape(DEPTH, 1, D_GROUP), w_glu)


NA_SCALE = NA_HEAD ** -0.5
NA_ROWS = DEC_SEQ // GRID_W
NA_KH = min(NA_WIN_H, NA_ROWS)
NA_QCOL = NA_OFF // LANE
NA_KCOL = (NA_OFF + D_GROUP) // LANE
NA_VCOL = (NA_OFF + 2 * D_GROUP) // LANE
NA_NDR = 2 * NA_WIN_H - 1
NA_AHEAD = 1


def _na_ctx_kernel(q_ref, k_ref, v_ref, o_ref, nk_ref, nv_ref):
    scores = []
    for hh in range(2):
        sl = slice(hh * NA_HEAD, (hh + 1) * NA_HEAD)
        k = k_ref[0, :, sl]
        nk_ref[0, hh] = k
        scores.append(_dot1(q_ref[0, :, sl], k, NT) * NA_SCALE)
    for hh in range(2):
        sl = slice(hh * NA_HEAD, (hh + 1) * NA_HEAD)
        v = v_ref[0, :, sl]
        nv_ref[0, hh] = v
        s = scores[hh]
        e = jnp.exp(s - jnp.max(s, axis=-1, keepdims=True))
        o = _dot1(e, v) / jnp.sum(e, axis=-1, keepdims=True)
        o_ref[0, :, sl] = o.astype(BF16)


def _na_ctx(st, z3):
    blk = lambda col: pl.BlockSpec((1, st.L, LANE), lambda b, p: (b, 0, col + p))
    kv_spec = pl.BlockSpec((1, 2, st.L, NA_HEAD), lambda b, p: (b, p, 0, 0))
    kv_shape = jax.ShapeDtypeStruct((st.nb, NA_HEADS, st.L, NA_HEAD), F32)
    return pl.pallas_call(
        _na_ctx_kernel,
        grid=(st.nb, NA_HEADS // 2),
        in_specs=[blk(NA_QCOL), blk(NA_KCOL), blk(NA_VCOL)],
        out_specs=[pl.BlockSpec((1, st.L, LANE), lambda b, p: (b, 0, p)), kv_spec, kv_spec],
        out_shape=[jax.ShapeDtypeStruct((st.nb, st.L, D_GROUP), BF16), kv_shape, kv_shape],
        compiler_params=_cparams("parallel", "parallel"),
        name="na_context",
    )(z3, z3, z3)


def _na_nbr_kernel(q_ref, k_ref, v_ref, ck_ref, cv_ref, bias_ref, o_ref):
    nloc = NA_KH * GRID_W
    qc = _iota((GRID_W, nloc), 0)
    kc = _iota((GRID_W, nloc), 1) % GRID_W
    cs = jnp.clip(qc - NA_WIN_W // 2, 0, GRID_W - NA_WIN_W)
    col_in = (kc >= cs) & (kc < cs + NA_WIN_W)

    def scores(hh, r):
        sl = slice(hh * NA_HEAD, (hh + 1) * NA_HEAD)
        rs = min(max(r - NA_KH // 2, 0), NA_ROWS - NA_KH)
        q = q_ref[0, r * GRID_W:(r + 1) * GRID_W, sl]
        k = k_ref[0, rs * GRID_W:rs * GRID_W + nloc, sl]
        off = (rs - r + NA_WIN_H - 1) * GRID_W
        s_loc = _dot1(q, k, NT) * NA_SCALE + bias_ref[hh, :, off:off + nloc]
        s_loc = jnp.where(col_in, s_loc, NEG_INF)
        s_ctx = _dot1(q, ck_ref[0, 0, hh], NT) * NA_SCALE
        return s_loc, s_ctx

    def finish(hh, r, s_loc, s_ctx):
        sl = slice(hh * NA_HEAD, (hh + 1) * NA_HEAD)
        rs = min(max(r - NA_KH // 2, 0), NA_ROWS - NA_KH)
        v = v_ref[0, rs * GRID_W:rs * GRID_W + nloc, sl]
        m = jnp.maximum(jnp.max(s_loc, axis=-1, keepdims=True), jnp.max(s_ctx, axis=-1, keepdims=True))
        e_loc = jnp.exp(s_loc - m)
        e_ctx = jnp.exp(s_ctx - m)
        den = jnp.sum(e_loc, axis=-1, keepdims=True) + jnp.sum(e_ctx, axis=-1, keepdims=True)
        o = (_dot1(e_loc, v) + _dot1(e_ctx, cv_ref[0, 0, hh])) / den
        o_ref[0, r * GRID_W:(r + 1) * GRID_W, sl] = o.astype(BF16)

    blocks = [(hh, r) for hh in range(2) for r in range(NA_ROWS)]
    pending = []
    for i, blk in enumerate(blocks):
        pending.append(scores(*blk))
        if i >= NA_AHEAD:
            finish(*blocks[i - NA_AHEAD], *pending.pop(0))
    for j in range(len(blocks) - NA_AHEAD, len(blocks)):
        finish(*blocks[j], *pending.pop(0))


def _na_bias_table(rpb):
    qc = np.arange(GRID_W)[:, None]
    kc = np.arange(GRID_W)[None, :]
    dc = np.clip(kc - qc, -(NA_WIN_W - 1), NA_WIN_W - 1) + (NA_WIN_W - 1)
    t = rpb[:, :, :, dc]
    return jnp.transpose(t, (0, 1, 3, 2, 4)).reshape(DEPTH, NA_HEADS, GRID_W, NA_NDR * GRID_W)


def _na_nbr(st, z3, cache_k, cache_v, bias_tab, l):
    blk = lambda col: pl.BlockSpec((1, st.L, LANE), lambda b, p: (b, 0, col + p))
    cspec = pl.BlockSpec((1, 1, 2, PAST_LEN, NA_HEAD), lambda b, p: (b, l, p, 0, 0))
    return pl.pallas_call(
        _na_nbr_kernel,
        grid=(st.nb, NA_HEADS // 2),
        in_specs=[blk(NA_QCOL), blk(NA_KCOL), blk(NA_VCOL), cspec, cspec,
                  pl.BlockSpec((None, 2, GRID_W, NA_NDR * GRID_W), lambda b, p: (l, p, 0, 0))],
        out_specs=pl.BlockSpec((1, st.L, LANE), lambda b, p: (b, 0, p)),
        out_shape=jax.ShapeDtypeStruct((st.nb, st.L, D_GROUP), BF16),
        compiler_params=_cparams("parallel", "parallel"),
        name="na_neighbourhood",
    )(z3, z3, z3, cache_k, cache_v, bias_tab)


ML_COL = ML_OFF // LANE
ML_SCALE = ML_HEAD ** -0.5


def _log_sigmoid(x):
    return -_softplus(-x)


def _rope_tables(L):
    half = ML_HEAD // 2
    quarter = half // 2
    t = np.arange(L)
    inv_freq = ROPE_BASE ** (-np.arange(quarter, dtype=np.float32) / quarter)

    def tabs(pos):
        ang = pos.astype(np.float32)[:, None] * inv_freq[None, :].astype(np.float32)
        c, s = np.cos(ang), np.sin(ang)
        return np.concatenate([c, c], axis=-1), np.concatenate([-s, s], axis=-1)

    c1, s1 = tabs(t // GRID_W)
    c2, s2 = tabs(t % GRID_W)
    return (jnp.asarray(np.concatenate([c1, c2], axis=-1), F32),
            jnp.asarray(np.concatenate([s1, s2], axis=-1), F32))


def _ml_kernel(*refs, L, rotary, layer):
    nh = ML_HEADS
    ib_ref, fb_ref = refs[0], refs[1]
    q_refs, k_refs, v_refs, o_refs = (refs[2 + i * nh:2 + (i + 1) * nh] for i in range(4))
    (g_ref, cos_ref, sin_ref, c0_ref, n0_ref, m0_ref, lng_ref, lnb_ref,
     y_ref, cf_ref, nf_ref, mf_ref, qs, ks, hf, hb, c_scr, n_scr, m_scr) = refs[2 + 4 * nh:]
    T = ML_CHUNK
    nc = L // T

    if rotary:
        first = (_iota((L, ML_HEAD), 1) % (ML_HEAD // 2)) < ML_HEAD // 4

        def rope(x):
            quarter = ML_HEAD // 4
            partner = jnp.where(first, pltpu.roll(x, ML_HEAD - quarter, axis=1), pltpu.roll(x, quarter, axis=1))
            return x * cos_ref[...] + partner * sin_ref[...]
    else:
        rope = lambda x: x
    for h in range(nh):
        qs[h] = rope(q_refs[h][0]) * ML_SCALE
        ks[h] = rope(k_refs[h][0])
        for d in range(2):
            c_scr[d, h] = c0_ref[0, d, h]
            n_scr[d, h] = n0_ref[0, h, d:d + 1, :]
            m_scr[d, h] = m0_ref[0, h, d:d + 1, :]

    tj = _iota((T, T), 0)
    ts = _iota((T, T), 1)
    ones = jnp.ones((T, T), F32)
    upto_row = (tj <= ts, tj >= ts)
    upto_col = (ts <= tj, ts >= tj)

    def body(ci, carry):
        ch = []
        for d in range(2):
            cd = ci if d == 0 else nc - 1 - ci
            rows = pl.ds(pl.multiple_of(cd * T, T), T)
            for h in range(nh):
                g = g_ref[0, h, cd]
                ig = g[:, d:d + 1] + ib_ref[layer * 2 * nh + d * nh + h]
                fg = g[:, 2 + d:3 + d] + fb_ref[layer * 2 * nh + d * nh + h]
                lf = jnp.broadcast_to(_log_sigmoid(fg), (T, T))
                igb = jnp.broadcast_to(ig, (T, T))
                q = dict(d=d, h=h, rows=rows, ig=ig, qc=qs[h, rows, :], kc=ks[h, rows, :], vc=v_refs[h][0, rows, :])
                q['bcol'] = _dot_sel_l(jnp.where(upto_col[d], 1.0, 0.0), lf)
                q['brow_i'] = _dot_sel_l(ones, jnp.where(upto_row[d], lf, 0.0) - jnp.where(tj == ts, igb, 0.0))
                ch.append(q)
        for q in ch:
            q['qk'] = _dot1(q['qc'], q['kc'], NT)
            q['c_old'] = c_scr[q['d'], q['h']]
            q['qc_c'] = _dot1(q['qc'], q['c_old'])
        for q in ch:
            d, h = q['d'], q['h']
            dmat = jnp.where(upto_col[d], q['bcol'] - q['brow_i'], -jnp.inf)
            b1 = q['bcol'][:, 0:1]
            m_prev = m_scr[d, h]
            inter = b1 + m_prev
            m_t = jnp.maximum(inter, jnp.max(dmat, axis=-1, keepdims=True))
            qk = q['qk'] * jnp.exp(dmat - m_t)
            w_inter = jnp.exp(inter - m_t)
            n_old = n_scr[d, h]
            den = w_inter * jnp.sum(q['qc'] * n_old, axis=-1, keepdims=True) + jnp.sum(qk, axis=-1, keepdims=True)
            q['scale'] = 1.0 / jnp.maximum(jnp.abs(den), jnp.exp(-m_t))
            q['inter_part'] = w_inter * q['qc_c']
            q['intra'] = _dot1(qk, q['vc'])
            b_last = b1[T - 1:T, :] if d == 0 else b1[0:1, :]
            g_s = b_last - b1 + q['ig']
            m_new = jnp.maximum(b_last + m_prev, jnp.max(g_s, axis=0, keepdims=True))
            w_old = jnp.exp(b_last + m_prev - m_new)
            w_s = jnp.exp(g_s - m_new)
            q['c_new'] = _dot1(q['kc'], w_s * q['vc'], TN)
            q['w_old'] = w_old
            n_scr[d, h] = w_old * n_old + jnp.sum(w_s * q['kc'], axis=0, keepdims=True)
            m_scr[d, h] = m_new
        for q in ch:
            d, h = q['d'], q['h']
            hcur = (q['inter_part'] + q['intra']) * q['scale']
            if d == 0:
                hf[h, q['rows'], :] = hcur
            else:
                hb[h, q['rows'], :] = hcur
            c_scr[d, h] = q['w_old'] * q['c_old'] + q['c_new']
        return carry

    lax.fori_loop(0, nc, body, 0)

    for h in range(nh):
        hsum = hf[h] + hb[h]
        mu = jnp.mean(hsum, axis=-1, keepdims=True)
        dv = hsum - mu
        var = jnp.mean(dv * dv, axis=-1, keepdims=True)
        hn = dv * lax.rsqrt(var + ML_GN_EPS)
        y_ref[0, :, h * ML_HEAD:(h + 1) * ML_HEAD] = (
            _sigmoid(o_refs[h][0]) * (hn * lng_ref[0, :, h * ML_HEAD:(h + 1) * ML_HEAD]
                                      + lnb_ref[0, :, h * ML_HEAD:(h + 1) * ML_HEAD])).astype(BF16)
        for d in range(2):
            cf_ref[0, d, h] = c_scr[d, h]
            nf_ref[0, h, d:d + 1, :] = n_scr[d, h]
            mf_ref[0, h, d:d + 1, :] = m_scr[d, h]


def _ml_mixer(st, z3, zg, i_bias, f_bias, ln_g, ln_b, c0, n0, m0, l):
    L, nb, nh = st.L, st.nb, ML_HEADS
    nc = L // ML_CHUNK
    gates = zg.reshape(nb, L, 2, 2, nh).transpose(0, 4, 1, 2, 3).reshape(nb, nh, nc, ML_CHUNK, 4)
    cos_t, sin_t = _rope_tables(L)
    blk = lambda col: pl.BlockSpec((1, L, LANE), lambda b: (b, 0, col))
    head_blocks = [blk(ML_COL + part * nh + h) for part in range(4) for h in range(nh)]
    smem = pl.BlockSpec(memory_space=pltpu.SMEM)
    tab = pl.BlockSpec((L, ML_HEAD), lambda b: (0, 0))
    c_spec = pl.BlockSpec((1, 2, nh, ML_HEAD, ML_HEAD), lambda b: (b, 0, 0, 0, 0))
    n_spec = pl.BlockSpec((1, nh, 2, ML_HEAD), lambda b: (b, 0, 0, 0))
    m_spec = pl.BlockSpec((1, nh, 2, 1), lambda b: (b, 0, 0, 0))
    par = pl.BlockSpec((1, 1, D_GROUP), lambda b: (l, 0, 0))
    return pl.pallas_call(
        functools.partial(_ml_kernel, L=L, rotary=st.latent, layer=l),
        grid=(nb,),
        in_specs=[smem, smem] + head_blocks + [
            pl.BlockSpec((1, nh, nc, ML_CHUNK, 4), lambda b: (b, 0, 0, 0, 0)),
            tab, tab, c_spec, n_spec, m_spec, par, par],
        out_specs=[pl.BlockSpec((1, L, D_GROUP), lambda b: (b, 0, 0)), c_spec, n_spec, m_spec],
        out_shape=[jax.ShapeDtypeStruct((nb, L, D_GROUP), BF16),
                   jax.ShapeDtypeStruct((nb, 2, nh, ML_HEAD, ML_HEAD), F32),
                   jax.ShapeDtypeStruct((nb, nh, 2, ML_HEAD), F32),
                   jax.ShapeDtypeStruct((nb, nh, 2, 1), F32)],
        scratch_shapes=[pltpu.VMEM((nh, L, ML_HEAD), F32), pltpu.VMEM((nh, L, ML_HEAD), F32),
                        pltpu.VMEM((nh, L, ML_HEAD), F32), pltpu.VMEM((nh, L, ML_HEAD), F32),
                        pltpu.VMEM((2, nh, ML_HEAD, ML_HEAD), F32), pltpu.VMEM((2, nh, 1, ML_HEAD), F32),
                        pltpu.VMEM((2, nh, 1, 1), F32)],
        compiler_params=_cparams("parallel"),
        name="mlstm",
    )(i_bias.reshape(-1), f_bias.reshape(-1), *([z3] * (4 * nh)), gates, cos_t, sin_t, c0, n0, m0,
      ln_g.reshape(DEPTH, 1, D_GROUP), ln_b.reshape(DEPTH, 1, D_GROUP))


RW_TL = 256
RW_T = 64
RW_GROUP = 16
RW_RCOL = RW_OFF // D_GROUP
RW_LCOL = (RW_OFF + 3 * D_GROUP) // (2 * LANE)
N_LORA = RW_LORA_W + RW_LORA_A + RW_LORA_G


def _head_ones():
    return jnp.asarray(np.kron(np.eye(RW_HEADS, dtype=np.float32), np.ones((RW_HEAD, RW_HEAD), np.float32)))


def _rw_front_kernel(r_ref, k_ref, v_ref, lo_ref, rp_ref, kp_ref, vp_ref, lp_ref, rn_ref, kn_ref, vn_ref, ln_ref,
                     mup_ref, mun_ref, w0_ref, w2_ref, a0_ref, a2_ref, g2_ref, kk_ref, ka_ref, hones_ref,
                     ro_ref, vo_ref, kko_ref, go_ref, kd_ref, bd_ref, lw_ref):
    t = pl.program_id(1)
    first = t == 0
    last = t == pl.num_programs(1) - 1
    row = _iota((RW_TL, 1), 0)

    def shift(x_ref, p_ref, n_ref, lo, hi):
        x = x_ref[0]
        prev_edge = jnp.where(first, 0.0, p_ref[0, SUB - 1:SUB, :])
        next_edge = jnp.where(last, 0.0, n_ref[0, 0:1, :])
        prev = jnp.where(row == 0, prev_edge, pltpu.roll(x, 1, axis=0))
        nxt = jnp.where(row == RW_TL - 1, next_edge, pltpu.roll(x, RW_TL - 1, axis=0))
        return x + mup_ref[0][:, lo:hi] * (prev - x) + mun_ref[0][:, lo:hi] * (nxt - x)

    g = D_GROUP
    r = shift(r_ref, rp_ref, rn_ref, 0, g)
    k = shift(k_ref, kp_ref, kn_ref, g, 2 * g)
    v = shift(v_ref, vp_ref, vn_ref, 2 * g, 3 * g)
    lo = shift(lo_ref, lp_ref, ln_ref, 3 * g, 3 * g + N_LORA)
    zw = lo[:, :RW_LORA_W]
    za = lo[:, RW_LORA_W:RW_LORA_W + RW_LORA_A]
    zg = lo[:, RW_LORA_W + RW_LORA_A:]
    ro_ref[0] = r
    vo_ref[0] = v
    kk = k * kk_ref[0]
    ssq = _seg_sum(kk * kk, hones_ref[...])
    kk = kk * lax.rsqrt(ssq + 1e-12)
    kko_ref[0] = kk
    go_ref[0] = _dot3(_sigmoid(zg), g2_ref[0])
    tw = jnp.tanh(zw)
    for d in range(2):
        w_log = -_softplus(-(w0_ref[0, d:d + 1, :] + _dot3(tw, w2_ref[0, d]))) - 0.5
        lw_ref[d, 0] = -jnp.exp(w_log)
        a = _sigmoid(a0_ref[0, d:d + 1, :] + _dot3(za, a2_ref[0, d]))
        kd_ref[d, 0] = k * (1.0 + (a - 1.0) * ka_ref[0])
        bd_ref[d, 0] = kk * a


def _rw_front(st, z3, mu_prev, mu_next, w0, w2, a0, a2, g2, k_k, k_a, l):
    nb, L = st.nb, st.L
    nt = L // RW_TL
    tpb = RW_TL // SUB
    main = lambda w, col: pl.BlockSpec((1, RW_TL, w), lambda b, t: (b, t, col))
    prev = lambda w, col: pl.BlockSpec((1, SUB, w), lambda b, t: (b, jnp.maximum(t * tpb - 1, 0), col))
    nxt = lambda w, col: pl.BlockSpec((1, SUB, w), lambda b, t: (b, jnp.minimum((t + 1) * tpb, L // SUB - 1), col))
    cols = [(D_GROUP, RW_RCOL), (D_GROUP, RW_RCOL + 1), (D_GROUP, RW_RCOL + 2), (2 * LANE, RW_LCOL)]
    lay = lambda *shape: pl.BlockSpec((1,) + shape, lambda b, t: (l,) + (0,) * len(shape))
    out1 = pl.BlockSpec((1, RW_TL, D_GROUP), lambda b, t: (b, t, 0))
    out2 = pl.BlockSpec((2, 1, RW_TL, D_GROUP), lambda b, t: (0, b, t, 0))
    s1 = jax.ShapeDtypeStruct((nb, L, D_GROUP), F32)
    s2 = jax.ShapeDtypeStruct((2, nb, L, D_GROUP), F32)
    return pl.pallas_call(
        _rw_front_kernel,
        grid=(nb, nt),
        in_specs=[main(*c) for c in cols] + [prev(*c) for c in cols] + [nxt(*c) for c in cols] + [
            lay(1, RW_IN), lay(1, RW_IN), lay(2, D_GROUP), lay(2, RW_LORA_W, D_GROUP), lay(2, D_GROUP),
            lay(2, RW_LORA_A, D_GROUP), lay(RW_LORA_G, D_GROUP), lay(1, D_GROUP), lay(1, D_GROUP),
            pl.BlockSpec((D_GROUP, D_GROUP), lambda b, t: (0, 0))],
        out_specs=[out1, out1, out1, out1, out2, out2, out2],
        out_shape=[s1, s1, s1, s1, s2, s2, s2],
        compiler_params=_cparams("parallel", "parallel"),
        name="rwkv_front",
    )(*([z3] * 12), mu_prev.reshape(DEPTH, 1, RW_IN), mu_next.reshape(DEPTH, 1, RW_IN), w0, w2, a0, a2, g2,
      k_k.reshape(DEPTH, 1, D_GROUP), k_a.reshape(DEPTH, 1, D_GROUP), _head_ones())


def _rw_core_kernel(rf_ref, vf_ref, kkf_ref, kdf_ref, bdf_ref, lwf_ref,
                    rb_ref, vb_ref, kkb_ref, kdb_ref, bdb_ref, lwb_ref, s0_ref,
                    yf_ref, yb_ref, sf_ref, s_scr):
    c = pl.program_id(1)
    T = RW_T

    @pl.when(c == 0)
    def _():
        s_scr[...] = s0_ref[0]

    tj = _iota((T, T), 0)
    ts = _iota((T, T), 1)
    dirs = ((rf_ref, vf_ref, kkf_ref, kdf_ref, bdf_ref, lwf_ref, yf_ref),
            (rb_ref, vb_ref, kkb_ref, kdb_ref, bdb_ref, lwb_ref, yb_ref))
    ch = []
    for d, (r_ref, v_ref, kk_ref, kd_ref, bd_ref, lw_ref, y_ref) in enumerate(dirs):
        if d == 0:
            incl = ts <= tj
            strict = ts < tj
            last = T - 1
        else:
            incl = ts >= tj
            strict = ts > tj
            last = 0
        lw = lw_ref[0, 0]
        cum = _dot_sel_l(jnp.where(incl, 1.0, 0.0), lw)
        w_in = jnp.exp(cum)
        w_inv = jnp.exp(-cum)
        w_ex = jnp.exp(cum - lw)
        kap_a = kk_ref[0] * w_ex
        bet_a = bd_ref[0, 0] * w_inv
        khat_a = kd_ref[0, 0] * w_inv
        rho_a = r_ref[0] * w_in
        w_tot = w_in[last:last + 1, :]
        v_a = v_ref[0]
        for h in range(RW_HEADS):
            sl = slice(h * RW_HEAD, (h + 1) * RW_HEAD)
            ch.append(dict(d=d, h=h, sl=sl, incl=incl, strict=strict, y_ref=y_ref, w_tot=w_tot[:, sl],
                           kap=kap_a[:, sl], bet=bet_a[:, sl], khat=khat_a[:, sl], rho=rho_a[:, sl], v=v_a[:, sl]))
    def solve(chs):
        for q in chs:
            kr = jnp.concatenate([q['kap'], q['rho']], axis=0)
            bk = jnp.concatenate([q['bet'], q['khat']], axis=0)
            q['gram'] = _dot1(kr, bk, NT)
        for q in chs:
            gram = q.pop('gram')
            q['n'] = -jnp.where(q['strict'], gram[:T, :T], 0.0)
            l_k = jnp.where(q['strict'], gram[:T, T:], 0.0)
            q['m_b'] = jnp.where(q['incl'], gram[T:, :T], 0.0)
            m_k = jnp.where(q['incl'], gram[T:, T:], 0.0)
            q['lmv'] = _dot1(jnp.concatenate([l_k, m_k], axis=0), q['v'])
        for q in chs:
            q['x'] = jnp.concatenate([q['kap'], q['lmv'][:T]], axis=1)
        for lvl in range(6):
            mm = _dot3 if lvl < 2 else _dot1
            for q in chs:
                if lvl < 5:
                    nx = mm(q['n'], jnp.concatenate([q['x'], q['n']], axis=1))
                    q['n'], q['x'] = nx[:, 2 * RW_HEAD:], q['x'] + nx[:, :2 * RW_HEAD]
                else:
                    q['x'] = q['x'] + _dot1(q['n'], q['x'])
        for q in chs:
            q['p'] = _dot1(q['m_b'], q['x'])
            xb = _dot1(q['x'], q['bet'], TN)
            q['a_m'] = xb[:RW_HEAD]
            q['d_m'] = _dot1(q['v'], q['khat'], TN) - xb[RW_HEAD:]
        for q in chs:
            rho_p = q['rho'] - q['p'][:, :RW_HEAD]
            y_v = q['lmv'][T:] - q['p'][:, RW_HEAD:]
            s_old = s_scr[q['d'], q['h']]
            q['y_ref'][0, :, q['sl']] = _dot1(rho_p, s_old, NT) + y_v
            s_scr[q['d'], q['h']] = (s_old - _dot1(s_old, q['a_m']) + q['d_m']) * q['w_tot']

    for g0 in range(0, len(ch), RW_GROUP):
        solve(ch[g0:g0 + RW_GROUP])

    @pl.when(c == pl.num_programs(1) - 1)
    def _():
        sf_ref[0] = s_scr[...]


def _rw_core(st, r, v, kk, kd, bd, lw, s0):
    nb, L = st.nb, st.L
    nc = L // RW_T
    f1 = pl.BlockSpec((1, RW_T, D_GROUP), lambda b, c: (b, c, 0))
    b1 = pl.BlockSpec((1, RW_T, D_GROUP), lambda b, c: (b, nc - 1 - c, 0))
    f2 = pl.BlockSpec((1, 1, RW_T, D_GROUP), lambda b, c: (0, b, c, 0))
    b2 = pl.BlockSpec((1, 1, RW_T, D_GROUP), lambda b, c: (1, b, nc - 1 - c, 0))
    s_spec = pl.BlockSpec((1, 2, RW_HEADS, RW_HEAD, RW_HEAD), lambda b, c: (b, 0, 0, 0, 0))
    ys = jax.ShapeDtypeStruct((nb, L, D_GROUP), F32)
    return pl.pallas_call(
        _rw_core_kernel,
        grid=(nb, nc),
        in_specs=[f1, f1, f1, f2, f2, f2, b1, b1, b1, b2, b2, b2, s_spec],
        out_specs=[f1, b1, s_spec],
        out_shape=[ys, ys, jax.ShapeDtypeStruct((nb, 2, RW_HEADS, RW_HEAD, RW_HEAD), F32)],
        scratch_shapes=[pltpu.VMEM((2, RW_HEADS, RW_HEAD, RW_HEAD), F32)],
        compiler_params=_cparams("parallel", "arbitrary"),
        name="rwkv_core",
    )(r, v, kk, kd, bd, lw, r, v, kk, kd, bd, lw, s0)


RWO_TM = 512


def _rw_out_kernel(yf_ref, yb_ref, r_ref, v_ref, kd0_ref, kd1_ref, g_ref, lng_ref, lnb_ref, rk_ref, hones_ref, o_ref):
    y = yf_ref[...] + yb_ref[...]
    inv = 1.0 / RW_HEAD
    mu = _seg_sum(y, hones_ref[...]) * inv
    dv = y - mu
    var = _seg_sum(dv * dv, hones_ref[...]) * inv
    yn = dv * lax.rsqrt(var + RW_GN_EPS) * lng_ref[0] + lnb_ref[0]
    kmean = 0.5 * (kd0_ref[0] + kd1_ref[0])
    bonus = _seg_sum(r_ref[...] * kmean * rk_ref[0], hones_ref[...]) * v_ref[...]
    o_ref[...] = ((yn + bonus) * g_ref[...]).astype(BF16)


def _rw_out(st, yf, yb, r, v, kd, g, ln_g, ln_b, r_k, l):
    rows = st.rows
    flat = lambda a: a.reshape(rows, D_GROUP)
    row = pl.BlockSpec((RWO_TM, D_GROUP), lambda i: (i, 0))
    lay = pl.BlockSpec((1, 1, D_GROUP), lambda i: (l, 0, 0))
    kd2 = kd.reshape(2, rows, D_GROUP)
    return pl.pallas_call(
        _rw_out_kernel,
        grid=(rows // RWO_TM,),
        in_specs=[row, row, row, row,
                  pl.BlockSpec((1, RWO_TM, D_GROUP), lambda i: (0, i, 0)),
                  pl.BlockSpec((1, RWO_TM, D_GROUP), lambda i: (1, i, 0)),
                  row, lay, lay, lay, pl.BlockSpec((D_GROUP, D_GROUP), lambda i: (0, 0))],
        out_specs=row,
        out_shape=jax.ShapeDtypeStruct((rows, D_GROUP), BF16),
        compiler_params=_cparams("parallel"),
        name="rwkv_out",
    )(flat(yf), flat(yb), flat(r), flat(v), kd2, kd2, flat(g), ln_g.reshape(DEPTH, 1, D_GROUP),
      ln_b.reshape(DEPTH, 1, D_GROUP), r_k.reshape(DEPTH, 1, D_GROUP), _head_ones())


def _trunk_layer(st, x, l, mod4, p, states, ctx_kv):
    nb, L = st.nb, st.L
    z, zg = _zproj(st, x, mod4, p['w_in'], p['w_gate_cols'], l)
    z3 = z.reshape(nb, L, D_Z)

    ydir, sfr, sfi = _s5_scan(st, z3, p['s5_bq'], p['s5_cq'], p['s5_lamr'], p['s5_lami'],
                              states['s5_re'], states['s5_im'], l)
    y_s5 = _s5_out(st, z, ydir.reshape(2, st.rows, D_GROUP), p['s5_d'], p['s5_w_glu'], l)

    r, v, kk, g, kd, bd, lw = _rw_front(st, z3, p['rw_mu_prev'], p['rw_mu_next'], p['rw_w0'], p['rw_w2'],
                                        p['rw_a0'], p['rw_a2'], p['rw_g2'], p['rw_k_k'], p['rw_k_a'], l)
    yf, yb, rw_s = _rw_core(st, r, v, kk, kd, bd, lw, states['rw'])
    y_rw = _rw_out(st, yf, yb, r, v, kd, g, p['rw_ln_g'], p['rw_ln_b'], p['rw_r_k'], l)

    if ctx_kv is None:
        y_na, nk, nv = _na_ctx(st, z3)
    else:
        y_na = _na_nbr(st, z3, ctx_kv[0], ctx_kv[1], p['na_bias'], l)
        nk = nv = None

    y_ml, ml_c, ml_n, ml_m = _ml_mixer(st, z3, zg, p['ml_i_bias'], p['ml_f_bias'], p['ml_ln_g'], p['ml_ln_b'],
                                       states['ml_c'], states['ml_n'], states['ml_m'], l)

    ys = (y_s5, y_rw, y_na.reshape(st.rows, D_GROUP), y_ml.reshape(st.rows, D_GROUP))
    tail = _oproj(st, ys, x, mod4, p['w_out'], p['ln1_g'], p['ln1_b'], p['w_router'], p['b_router'], l)
    return tail, (nk, nv, sfr, sfi, rw_s, ml_c, ml_n, ml_m)


def _moe_and_ln2(tails, mod4, p, l):
    xs = []
    for st, (x1, h2, gates, keep) in zip((PROMPT, LATENT), tails):
        f = _moe_sorted(st, h2, gates, keep, p['moe_w_gate'], p['moe_w_up'], p['moe_w_down'], l)
        xs.append(_ln2f(st, x1, f, mod4, p['ln2_g'], p['ln2_b'], l))
    return xs


def kernel(x_prompt, x_sample, cache_nat_k, cache_nat_v, state_s5_re, state_s5_im, state_rwkv, state_mlstm_c, state_mlstm_n, state_mlstm_m, c, c_ctx, w_mod, b_mod, w_in, w_out, s5_lam_re, s5_lam_im, s5_log_step, s5_b_re, s5_b_im, s5_c_re, s5_c_im, s5_d, s5_w_glu, rw_mu_prev, rw_mu_next, rw_w0, rw_w2, rw_a0, rw_a2, rw_g2, rw_k_k, rw_k_a, rw_r_k, rw_ln_g, rw_ln_b, na_rpb, ml_i_bias, ml_f_bias, ml_ln_g, ml_ln_b, ln1_g, ln1_b, ln2_g, ln2_b, w_router, b_router, moe_w_gate, moe_w_up, moe_w_down):
    dt = x_prompt.dtype
    cond = jnp.concatenate([c_ctx[None, :], c, jnp.zeros((MOD_ROWS - 1 - DEC_BATCH, D_MODEL), F32)], axis=0)
    mod4 = _modulation(cond, w_mod, b_mod).reshape(DEPTH, MOD_ROWS, 1, 6 * D_MODEL)

    lbr, lbi, bbr, bbi = _s5_prep(s5_lam_re, s5_lam_im, s5_log_step, s5_b_re, s5_b_im)
    s5_bq, s5_cq, s5_lamr, s5_lami = _s5_block_params(lbr, lbi, bbr, bbi, s5_c_re, s5_c_im)
    w_in_b = w_in.astype(BF16)
    p = dict(w_in=w_in_b, w_gate_cols=w_in_b[:, :, D_Z:], w_out=w_out.astype(BF16),
             s5_bq=s5_bq, s5_cq=s5_cq, s5_lamr=s5_lamr, s5_lami=s5_lami, s5_d=s5_d, s5_w_glu=s5_w_glu,
             rw_mu_prev=rw_mu_prev, rw_mu_next=rw_mu_next, rw_w0=rw_w0, rw_w2=rw_w2, rw_a0=rw_a0, rw_a2=rw_a2,
             rw_g2=rw_g2, rw_k_k=rw_k_k, rw_k_a=rw_k_a, rw_r_k=rw_r_k, rw_ln_g=rw_ln_g, rw_ln_b=rw_ln_b,
             na_bias=_na_bias_table(na_rpb), ml_i_bias=ml_i_bias, ml_f_bias=ml_f_bias, ml_ln_g=ml_ln_g,
             ml_ln_b=ml_ln_b, ln1_g=ln1_g, ln1_b=ln1_b, ln2_g=ln2_g, ln2_b=ln2_b, w_router=w_router,
             b_router=b_router, moe_w_gate=moe_w_gate.astype(BF16), moe_w_up=moe_w_up.astype(BF16),
             moe_w_down=moe_w_down.astype(BF16))

    gp = BATCH // S5_SEQS
    zero_states = dict(
        s5_re=jnp.zeros((gp, 2, S5_SEQS, N_S5), F32), s5_im=jnp.zeros((gp, 2, S5_SEQS, N_S5), F32),
        rw=jnp.zeros((BATCH, 2, RW_HEADS, RW_HEAD, RW_HEAD), F32),
        ml_c=jnp.zeros((BATCH, 2, ML_HEADS, ML_HEAD, ML_HEAD), F32),
        ml_n=jnp.zeros((BATCH, ML_HEADS, 2, ML_HEAD), F32), ml_m=jnp.zeros((BATCH, ML_HEADS, 2, 1), F32))

    xp = x_prompt.reshape(PROMPT.rows, D_MODEL)
    xs = x_sample.reshape(LATENT.rows, D_MODEL)
    outs = [[] for _ in range(8)]
    for l in range(DEPTH):
        tail_p, ctx_t = _trunk_layer(PROMPT, xp, l, mod4, p, zero_states, None)
        for acc, t in zip(outs, ctx_t):
            acc.append(t)
        lat_states = dict(
            s5_re=state_s5_re[:, l].reshape(DEC_BATCH, 2, N_S5).transpose(1, 0, 2)[None],
            s5_im=state_s5_im[:, l].reshape(DEC_BATCH, 2, N_S5).transpose(1, 0, 2)[None],
            rw=state_rwkv[:, l], ml_c=state_mlstm_c[:, l],
            ml_n=state_mlstm_n[:, l].transpose(0, 2, 1, 3), ml_m=state_mlstm_m[:, l].transpose(0, 2, 1)[..., None])
        tail_s, _ = _trunk_layer(LATENT, xs, l, mod4, p, lat_states, (cache_nat_k, cache_nat_v))
        xp, xs = _moe_and_ln2((tail_p, tail_s), mod4, p, l)

    nk, nv, s5r, s5i, rw, mc, mn, mm = [jnp.stack(t, axis=1) for t in outs]

    def s5_state(t):
        return t.transpose(0, 3, 1, 2, 4).reshape(BATCH, DEPTH, 2, S5_GROUPS, S5_STATE)

    return (xp.reshape(BATCH, SEQ, D_MODEL), xs.reshape(DEC_BATCH, DEC_SEQ, D_MODEL),
            nk, nv, s5_state(s5r).astype(dt), s5_state(s5i).astype(dt), rw.astype(dt), mc.astype(dt),
            mn.transpose(0, 1, 3, 2, 4).astype(dt), mm[..., 0].transpose(0, 1, 3, 2).astype(dt))
```

```python
import functools
import math

import numpy as np
import jax
import jax.numpy as jnp
from jax import lax
from jax.experimental import pallas as pl
from jax.experimental.pallas import tpu as pltpu

F32 = jnp.float32
BF16 = jnp.bfloat16

D_MODEL = 2048
BATCH = 16
SEQ = 256
DEPTH = 4
DEC_BATCH = 8
DEC_SEQ = 1024
PAST_LEN = 256
GRID_W = 64
D_GROUP = D_MODEL // 4
S5_CH = 16
S5_GROUPS = D_GROUP // S5_CH
S5_STATE = 64
RW_HEAD = 64
RW_HEADS = D_GROUP // RW_HEAD
RW_LORA_W = 64
RW_LORA_A = 64
RW_LORA_G = 128
RW_GN_EPS = 64e-5
NA_HEAD = 64
NA_HEADS = D_GROUP // NA_HEAD
NA_WIN_H = 8
NA_WIN_W = 16
NEG_INF = -1e30
ML_HEAD = 128
ML_HEADS = D_GROUP // ML_HEAD
ML_CHUNK = 256
ML_GN_EPS = 1e-5
ROPE_BASE = 10000.0
N_EXPERTS = 16
N_EXPERT_GROUPS = 4
EXPERTS_PER_GROUP = N_EXPERTS // N_EXPERT_GROUPS
D_EXPERT = 512
DEEPNORM_ALPHA = (2 * DEPTH) ** 0.25
LN_EPS = 1e-5
S5_IN = D_GROUP
RW_IN = 3 * D_GROUP + RW_LORA_W + RW_LORA_A + RW_LORA_G
NA_IN = 3 * D_GROUP
ML_IN = 4 * D_GROUP + 4 * ML_HEADS
D_IN = S5_IN + RW_IN + NA_IN + ML_IN
N_GATE = 4 * ML_HEADS
D_Z = D_IN - N_GATE
RW_OFF = S5_IN
NA_OFF = S5_IN + RW_IN
ML_OFF = NA_OFF + NA_IN
MOD_ROWS = 16
LANE = 128
SUB = 8

VMEM_LIMIT = 56 * 1024 * 1024


def _cparams(*sem):
    return pltpu.CompilerParams(dimension_semantics=sem, vmem_limit_bytes=VMEM_LIMIT)


def _dg(a, b, dims):
    return lax.dot_general(a, b, (dims, ((), ())), preferred_element_type=F32)


NN = ((1,), (0,))
NT = ((1,), (1,))
TN = ((0,), (0,))


def _dot1(a, b, dims=NN):
    return _dg(a.astype(BF16), b.astype(BF16), dims)


def _split(x):
    hi = x.astype(BF16)
    lo = (x - hi.astype(F32)).astype(BF16)
    return hi, lo


def _dot3(a, b, dims=NN):
    ah, al = _split(a)
    bh, bl = _split(b)
    return _dg(ah, bh, dims) + (_dg(ah, bl, dims) + _dg(al, bh, dims))


def _dot_sel_l(sel, x, dims=NN):
    s = sel.astype(BF16)
    hi, lo = _split(x)
    return _dg(s, hi, dims) + _dg(s, lo, dims)


def _seg_sum(x, sel):
    s = sel.astype(BF16)
    hi, lo = _split(x)
    return _dg(hi, s, NN) + _dg(lo, s, NN)


def _sigmoid(x):
    return 1.0 / (1.0 + jnp.exp(-x))


def _silu(x):
    return x * _sigmoid(x)


def _softplus(x):
    return jnp.maximum(x, 0.0) + jnp.log(1.0 + jnp.exp(-jnp.abs(x)))


def _iota(shape, dim):
    return lax.broadcasted_iota(jnp.int32, shape, dim)


class _Stream:
    def __init__(self, nb, L, latent):
        self.nb, self.L, self.latent = nb, L, latent
        self.rows = nb * L

    def mod_row(self, tile, tile_rows):
        if not self.latent:
            return 0
        return 1 + (tile * tile_rows) // self.L


PROMPT = _Stream(BATCH, SEQ, False)
LATENT = _Stream(DEC_BATCH, DEC_SEQ, True)


MOD_TN = 768


def _mod_kernel(cond_ref, w_ref, b_ref, o_ref):
    c = _silu(cond_ref[...])
    o_ref[0] = _dot3(c, w_ref[0]) + b_ref[0]


def _modulation(cond, w_mod, b_mod):
    n = 6 * D_MODEL
    return pl.pallas_call(
        _mod_kernel,
        grid=(DEPTH, n // MOD_TN),
        in_specs=[pl.BlockSpec((MOD_ROWS, D_MODEL), lambda l, j: (0, 0)),
                  pl.BlockSpec((1, D_MODEL, MOD_TN), lambda l, j: (l, 0, j)),
                  pl.BlockSpec((1, 1, MOD_TN), lambda l, j: (l, 0, j))],
        out_specs=pl.BlockSpec((1, MOD_ROWS, MOD_TN), lambda l, j: (l, 0, j)),
        out_shape=jax.ShapeDtypeStruct((DEPTH, MOD_ROWS, n), F32),
        compiler_params=_cparams("parallel", "parallel"),
        name="modulation",
    )(cond, w_mod, b_mod.reshape(DEPTH, 1, n))


ZP_TM = 1024
ZP_TN = 256


def _zproj_kernel(x_ref, mod_ref, w_ref, wg_ref, z_ref, zg_ref, h_scr):
    j = pl.program_id(1)

    @pl.when(j == 0)
    def _():
        m = mod_ref[0, 0]
        shift1 = m[:, 0:D_MODEL]
        scale1 = m[:, D_MODEL:2 * D_MODEL]
        h = (x_ref[...] * (1.0 + scale1) + shift1).astype(BF16)
        h_scr[...] = h
        zg_ref[...] = jnp.dot(h, wg_ref[0], preferred_element_type=F32)

    z_ref[...] = jnp.dot(h_scr[...], w_ref[0], preferred_element_type=F32)


def _zproj(st, x, mod4, w_in, w_gate_cols, l):
    return pl.pallas_call(
        _zproj_kernel,
        grid=(st.rows // ZP_TM, D_Z // ZP_TN),
        in_specs=[pl.BlockSpec((ZP_TM, D_MODEL), lambda i, j: (i, 0)),
                  pl.BlockSpec((1, 1, 1, 6 * D_MODEL), lambda i, j: (l, st.mod_row(i, ZP_TM), 0, 0)),
                  pl.BlockSpec((1, D_MODEL, ZP_TN), lambda i, j: (l, 0, j)),
                  pl.BlockSpec((1, D_MODEL, N_GATE), lambda i, j: (l, 0, 0))],
        out_specs=[pl.BlockSpec((ZP_TM, ZP_TN), lambda i, j: (i, j)),
                   pl.BlockSpec((ZP_TM, N_GATE), lambda i, j: (i, 0))],
        out_shape=[jax.ShapeDtypeStruct((st.rows, D_Z), F32),
                   jax.ShapeDtypeStruct((st.rows, N_GATE), F32)],
        scratch_shapes=[pltpu.VMEM((ZP_TM, D_MODEL), BF16)],
        compiler_params=_cparams("parallel", "arbitrary"),
        name="zproj",
    )(x, mod4, w_in, w_gate_cols)


OP_TM = 512


def _layer_norm(v, g, b):
    mu = jnp.mean(v, axis=-1, keepdims=True)
    d = v - mu
    var = jnp.mean(d * d, axis=-1, keepdims=True)
    return d * lax.rsqrt(var + LN_EPS) * g + b


def _route(scores, b_router):
    sel = scores + b_router
    s = [sel[e:e + 1, :] for e in range(N_EXPERTS)]
    in_top2 = []
    for g in range(N_EXPERT_GROUPS):
        for i in range(EXPERTS_PER_GROUP):
            e = g * EXPERTS_PER_GROUP + i
            cnt = jnp.zeros_like(s[e])
            for jj in range(EXPERTS_PER_GROUP):
                if jj == i:
                    continue
                o = g * EXPERTS_PER_GROUP + jj
                beats = (s[o] > s[e]) if jj > i else (s[o] >= s[e])
                cnt = cnt + jnp.where(beats, 1.0, 0.0)
            in_top2.append(cnt < 2.0)
    grp = []
    for g in range(N_EXPERT_GROUPS):
        tot = jnp.zeros_like(s[0])
        for i in range(EXPERTS_PER_GROUP):
            e = g * EXPERTS_PER_GROUP + i
            tot = tot + jnp.where(in_top2[e], s[e], 0.0)
        grp.append(tot)
    keep_rows = []
    for g in range(N_EXPERT_GROUPS):
        cnt = jnp.zeros_like(s[0])
        for o in range(N_EXPERT_GROUPS):
            if o == g:
                continue
            beats = (grp[o] > grp[g]) if o > g else (grp[o] >= grp[g])
            cnt = cnt + jnp.where(beats, 1.0, 0.0)
        best = cnt < 1.0
        for i in range(EXPERTS_PER_GROUP):
            e = g * EXPERTS_PER_GROUP + i
            keep_rows.append(jnp.where(best, jnp.where(in_top2[e], 1.0, 0.0), 0.0))
    keep = jnp.concatenate(keep_rows, axis=0)
    picked = scores * keep
    return picked / jnp.sum(picked, axis=0, keepdims=True), keep


def _oproj_kernel(y0_ref, y1_ref, y2_ref, y3_ref, w_ref, x_ref, mod_ref, g_ref, b_ref, wr_ref, br_ref,
                  x1_ref, h2_ref, gate_ref, keep_ref):
    acc = None
    for k, yr in enumerate((y0_ref, y1_ref, y2_ref, y3_ref)):
        part = jnp.dot(yr[...], w_ref[0, k * D_GROUP:(k + 1) * D_GROUP, :], preferred_element_type=F32)
        acc = part if acc is None else acc + part
    m = mod_ref[0, 0]
    gate1 = m[:, 2 * D_MODEL:3 * D_MODEL]
    shift2 = m[:, 3 * D_MODEL:4 * D_MODEL]
    scale2 = m[:, 4 * D_MODEL:5 * D_MODEL]
    x1 = _layer_norm(DEEPNORM_ALPHA * x_ref[...] + gate1 * acc, g_ref[0], b_ref[0])
    x1_ref[...] = x1
    h2 = x1 * (1.0 + scale2) + shift2
    h2_ref[...] = h2.astype(BF16)
    scores_t = _sigmoid(_dot3(wr_ref[...], h2, NT))
    gates_t, keep_t = _route(scores_t, br_ref[...])
    gate_ref[...] = gates_t.T
    keep_ref[...] = keep_t.T


def _oproj(st, ys, x, mod4, w_out_b, ln_g, ln_b, w_router, b_router, l):
    row = lambda i: (i, 0)
    return pl.pallas_call(
        _oproj_kernel,
        grid=(st.rows // OP_TM,),
        in_specs=[pl.BlockSpec((OP_TM, D_GROUP), row)] * 4 + [
            pl.BlockSpec((1, D_MODEL, D_MODEL), lambda i: (l, 0, 0)),
            pl.BlockSpec((OP_TM, D_MODEL), row),
            pl.BlockSpec((1, 1, 1, 6 * D_MODEL), lambda i: (l, st.mod_row(i, OP_TM), 0, 0)),
            pl.BlockSpec((1, 1, D_MODEL), lambda i: (l, 0, 0)),
            pl.BlockSpec((1, 1, D_MODEL), lambda i: (l, 0, 0)),
            pl.BlockSpec((N_EXPERTS, D_MODEL), lambda i: (0, 0)),
            pl.BlockSpec((N_EXPERTS, 1), lambda i: (0, 0))],
        out_specs=[pl.BlockSpec((OP_TM, D_MODEL), row),
                   pl.BlockSpec((OP_TM, D_MODEL), row),
                   pl.BlockSpec((OP_TM, N_EXPERTS), row),
                   pl.BlockSpec((OP_TM, N_EXPERTS), row)],
        out_shape=[jax.ShapeDtypeStruct((st.rows, D_MODEL), F32),
                   jax.ShapeDtypeStruct((st.rows, D_MODEL), BF16),
                   jax.ShapeDtypeStruct((st.rows, N_EXPERTS), F32),
                   jax.ShapeDtypeStruct((st.rows, N_EXPERTS), F32)],
        compiler_params=_cparams("parallel"),
        name="oproj_ln1_router",
    )(*ys, w_out_b, x, mod4, ln_g.reshape(DEPTH, 1, D_MODEL), ln_b.reshape(DEPTH, 1, D_MODEL),
      w_router.T, b_router.reshape(N_EXPERTS, 1))


LN_TM = 512


GS_BLK = 1024
GS_CAP = 320
GS_ALIGN = 2 * SUB
GS_COLS = 512
GS_STEP = 256
GS_VMEM_LIMIT = 60 * 1024 * 1024
GS_RCH = 3 * LANE
GS_ROWS = pl.cdiv(GS_BLK + N_EXPERT_GROUPS * GS_ALIGN, GS_RCH) * GS_RCH


def _moe_sorted_kernel(h_ref, gate_ref, keep_ref, wg_ref, wu_ref, wd_ref, f_ref, xs_scr, gs_scr, pt_scr, acc_scr,
                       seg_smem):
    e = pl.program_id(1)

    @pl.when(e == 0)
    def _():
        keep = keep_ref[...]
        lane = _iota(keep.shape, 1)
        member = []
        for g in range(N_EXPERT_GROUPS):
            in_g = (lane >= g * EXPERTS_PER_GROUP) & (lane < (g + 1) * EXPERTS_PER_GROUP)
            member.append(jnp.minimum(jnp.sum(jnp.where(in_g, keep, 0.0), axis=-1, keepdims=True), 1.0))
        lane4 = _iota((GS_BLK, LANE), 1)
        onehot = jnp.zeros((GS_BLK, LANE), F32)
        for g in range(N_EXPERT_GROUPS):
            onehot = onehot + jnp.where(lane4 == g, member[g], 0.0)
        counts = jnp.sum(onehot, axis=0, keepdims=True)
        starts = []
        start = jnp.zeros((1, 1), F32)
        for g in range(N_EXPERT_GROUPS):
            cnt_g = counts[:, g:g + 1]
            starts.append(start)
            seg_smem[2 * g] = jnp.sum(start).astype(jnp.int32)
            seg_smem[2 * g + 1] = jnp.sum(cnt_g).astype(jnp.int32)
            start = start + jnp.ceil(cnt_g * (1.0 / GS_ALIGN)) * GS_ALIGN
        onehot_b = onehot.astype(BF16)
        for r0 in range(0, GS_BLK, GS_STEP):
            tri = jnp.where(_iota((GS_STEP, GS_BLK), 1) <= _iota((GS_STEP, GS_BLK), 0) + r0, 1.0, 0.0).astype(BF16)
            csum = jnp.dot(tri, onehot_b, preferred_element_type=F32)
            dest = jnp.zeros((GS_STEP, 1), F32)
            for g in range(N_EXPERT_GROUPS):
                dest = dest + member[g][r0:r0 + GS_STEP] * (starts[g] + csum[:, g:g + 1] - 1.0)
            pt_scr[r0:r0 + GS_STEP, :] = jnp.where(
                _iota((GS_STEP, GS_ROWS), 1) == dest.astype(jnp.int32), 1.0, 0.0).astype(BF16)
        for c0 in range(0, GS_ROWS, GS_RCH):
            pt_c = pt_scr[:, c0:c0 + GS_RCH]
            xs_scr[c0:c0 + GS_RCH, :] = _dg(pt_c, h_ref[...], TN).astype(BF16)
            gs_scr[c0:c0 + GS_RCH, :] = _dot_sel_l(pt_c, gate_ref[...], TN)
        acc_scr[...] = jnp.zeros_like(acc_scr)

    g = e // EXPERTS_PER_GROUP
    seg_start = seg_smem[2 * g]
    seg_len = seg_smem[2 * g + 1]
    wg = wg_ref[0, 0]
    wu = wu_ref[0, 0]
    wd = wd_ref[0, 0]

    def window(c, carry):
        first = seg_start + c * GS_CAP
        start = jnp.minimum(first, GS_ROWS - GS_CAP)
        rows = pl.ds(pl.multiple_of(start, GS_ALIGN), GS_CAP)
        x = xs_scr[rows, :]
        a = jnp.dot(x, wg, preferred_element_type=F32)
        u = jnp.dot(x, wu, preferred_element_type=F32)
        gates = gs_scr[rows, :]
        ge = jnp.sum(jnp.where(_iota(gates.shape, 1) == e, gates, 0.0), axis=-1, keepdims=True)
        ge = jnp.where(_iota((GS_CAP, 1), 0) + start >= first, ge, 0.0)
        hid = (_silu(a) * u * ge).astype(BF16)
        acc_scr[rows, :] += jnp.dot(hid, wd, preferred_element_type=F32)
        return carry

    lax.fori_loop(0, (seg_len + GS_CAP - 1) // GS_CAP, window, 0)

    @pl.when(e == N_EXPERTS - 1)
    def _():
        pt = pt_scr[...]
        for c0 in range(0, D_MODEL, GS_COLS):
            acc_b = acc_scr[:, c0:c0 + GS_COLS].astype(BF16)
            f_ref[:, c0:c0 + GS_COLS] = jnp.dot(pt, acc_b, preferred_element_type=F32).astype(BF16)


def _moe_sorted(st, h2, gates, keep, wg_b, wu_b, wd_b, l):
    row = lambda i, e: (i, 0)
    wspec = lambda a, b: pl.BlockSpec((1, 1, a, b), lambda i, e: (l, e, 0, 0))
    return pl.pallas_call(
        _moe_sorted_kernel,
        grid=(st.rows // GS_BLK, N_EXPERTS),
        in_specs=[pl.BlockSpec((GS_BLK, D_MODEL), row, pipeline_mode=pl.Buffered(1)),
                  pl.BlockSpec((GS_BLK, N_EXPERTS), row), pl.BlockSpec((GS_BLK, N_EXPERTS), row),
                  wspec(D_MODEL, D_EXPERT), wspec(D_MODEL, D_EXPERT), wspec(D_EXPERT, D_MODEL)],
        out_specs=pl.BlockSpec((GS_BLK, D_MODEL), row),
        out_shape=jax.ShapeDtypeStruct((st.rows, D_MODEL), BF16),
        scratch_shapes=[pltpu.VMEM((GS_ROWS, D_MODEL), BF16), pltpu.VMEM((GS_ROWS, N_EXPERTS), F32),
                        pltpu.VMEM((GS_BLK, GS_ROWS), BF16), pltpu.VMEM((GS_ROWS, D_MODEL), F32),
                        pltpu.SMEM((2 * N_EXPERT_GROUPS,), jnp.int32)],
        compiler_params=pltpu.CompilerParams(dimension_semantics=("parallel", "arbitrary"),
                                             vmem_limit_bytes=GS_VMEM_LIMIT),
        name="moe_group_sorted",
    )(h2, gates, keep, wg_b, wu_b, wd_b)


def _ln2f_kernel(x_ref, f_ref, mod_ref, g_ref, b_ref, o_ref):
    gate2 = mod_ref[0, 0][:, 5 * D_MODEL:6 * D_MODEL]
    o_ref[...] = _layer_norm(DEEPNORM_ALPHA * x_ref[...] + gate2 * f_ref[...].astype(F32), g_ref[0], b_ref[0])


def _ln2f(st, x1, f, mod4, ln_g, ln_b, l):
    row = lambda i: (i, 0)
    return pl.pallas_call(
        _ln2f_kernel,
        grid=(st.rows // LN_TM,),
        in_specs=[pl.BlockSpec((LN_TM, D_MODEL), row),
                  pl.BlockSpec((LN_TM, D_MODEL), row),
                  pl.BlockSpec((1, 1, 1, 6 * D_MODEL), lambda i: (l, st.mod_row(i, LN_TM), 0, 0)),
                  pl.BlockSpec((1, 1, D_MODEL), lambda i: (l, 0, 0)),
                  pl.BlockSpec((1, 1, D_MODEL), lambda i: (l, 0, 0))],
        out_specs=pl.BlockSpec((LN_TM, D_MODEL), row),
        out_shape=jax.ShapeDtypeStruct((st.rows, D_MODEL), F32),
        compiler_params=_cparams("parallel"),
        name="ln2",
    )(x1, f, mod4, ln_g.reshape(DEPTH, 1, D_MODEL), ln_b.reshape(DEPTH, 1, D_MODEL))


S5_TC = 128
S5_SEQS = SUB
S5_Q = 4
S5_QS = S5_GROUPS // S5_Q * S5_STATE
N_S5 = S5_GROUPS * S5_STATE


def _s5_prep_kernel(lr_ref, li_ref, ls_ref, br_ref, bi_ref, e_ref, lbr_ref, lbi_ref, bbr_ref, bbi_ref):
    lr = lr_ref[0]
    li = li_ref[0]
    dt = jnp.exp(ls_ref[0])
    mag = jnp.exp(lr * dt)
    ang = li * dt
    ar = mag * jnp.cos(ang)
    ai = mag * jnp.sin(ang)
    lbr_ref[0] = ar
    lbi_ref[0] = ai
    den = lr * lr + li * li
    nr = ar - 1.0
    cr = (nr * lr + ai * li) / den
    ci = (ai * lr - nr * li) / den
    cr = _dot_sel_l(e_ref[...], cr)
    ci = _dot_sel_l(e_ref[...], ci)
    bre = br_ref[0]
    bim = bi_ref[0]
    bbr_ref[0] = cr * bre - ci * bim
    bbi_ref[0] = cr * bim + ci * bre


def _s5_prep(lam_re, lam_im, log_step, b_re, b_im):
    d2 = DEPTH * 2
    g, p, h = S5_GROUPS, S5_STATE, S5_CH
    bt_re = jnp.swapaxes(b_re, -1, -2).reshape(d2, g * h, p)
    bt_im = jnp.swapaxes(b_im, -1, -2).reshape(d2, g * h, p)
    expand = jnp.asarray(np.kron(np.eye(g, dtype=np.float32), np.ones((h, 1), np.float32)))
    spec_gp = pl.BlockSpec((1, g, p), lambda i: (i, 0, 0))
    spec_b = pl.BlockSpec((1, g * h, p), lambda i: (i, 0, 0))
    lbr, lbi, bbr, bbi = pl.pallas_call(
        _s5_prep_kernel,
        grid=(d2,),
        in_specs=[spec_gp, spec_gp, pl.BlockSpec((1, g, 1), lambda i: (i, 0, 0)), spec_b, spec_b,
                  pl.BlockSpec((g * h, g), lambda i: (0, 0))],
        out_specs=[spec_gp, spec_gp, spec_b, spec_b],
        out_shape=[jax.ShapeDtypeStruct((d2, g, p), F32)] * 2 + [jax.ShapeDtypeStruct((d2, g * h, p), F32)] * 2,
        compiler_params=_cparams("parallel"),
        name="s5_prep",
    )(lam_re.reshape(d2, g, p), lam_im.reshape(d2, g, p), log_step.reshape(d2, g, 1), bt_re, bt_im, expand)
    return lbr, lbi, bbr, bbi


def _s5_block_params(lbr, lbi, bbr, bbi, c_re, c_im):
    d2 = DEPTH * 2
    gq = S5_GROUPS // S5_Q
    eye = jnp.eye(gq, dtype=F32)

    def b_blocks(b):
        b = b.reshape(d2, S5_Q, gq, S5_CH, S5_STATE)
        return jnp.einsum('djghp,gk->djghkp', b, eye).reshape(d2, S5_Q, gq * S5_CH, gq * S5_STATE)

    def c_blocks(c):
        c = c.reshape(d2, S5_Q, gq, S5_CH, S5_STATE)
        return jnp.einsum('djghp,gk->djgpkh', c, eye).reshape(d2, S5_Q, gq * S5_STATE, gq * S5_CH)

    bq = jnp.concatenate([b_blocks(bbr), b_blocks(bbi)], axis=-1)
    cq = jnp.concatenate([c_blocks(c_re), -c_blocks(c_im)], axis=-2)
    return bq, cq, lbr.reshape(d2, 1, N_S5), lbi.reshape(d2, 1, N_S5)


def _s5_scan_kernel(u_ref, bq_ref, cq_ref, lr_ref, li_ref, s0r_ref, s0i_ref, y_ref, sfr_ref, sfi_ref,
                    utb, bur, bui, ytb, sr_scr, si_scr):
    d = pl.program_id(1)
    c = pl.program_id(2)
    nrow = S5_TC * S5_SEQS
    cw = D_GROUP // S5_Q

    @pl.when(c == 0)
    def _():
        sr_scr[...] = s0r_ref[0, 0]
        si_scr[...] = s0i_ref[0, 0]

    for b in range(S5_SEQS):
        for j in range(S5_Q):
            utb[j, pl.ds(b, S5_TC, stride=S5_SEQS), :] = u_ref[b, :, j * cw:(j + 1) * cw]
    for j in range(S5_Q):
        bu = _dot1(utb[j], bq_ref[0, j])
        bur[:, j * S5_QS:(j + 1) * S5_QS] = bu[:, :S5_QS]
        bui[:, j * S5_QS:(j + 1) * S5_QS] = bu[:, S5_QS:]
    for j in range(S5_Q):
        sl = slice(j * S5_QS, (j + 1) * S5_QS)
        lam_r = jnp.broadcast_to(lr_ref[0][:, sl], (S5_SEQS, S5_QS))
        lam_i = jnp.broadcast_to(li_ref[0][:, sl], (S5_SEQS, S5_QS))

        def step(t, carry, sl=sl, lam_r=lam_r, lam_i=lam_i):
            sr, si = carry
            te = jnp.where(d == 0, t, S5_TC - 1 - t)
            rows = pl.ds(pl.multiple_of(te * S5_SEQS, S5_SEQS), S5_SEQS)
            nr = lam_r * sr - lam_i * si + bur[rows, sl]
            ni = lam_r * si + lam_i * sr + bui[rows, sl]
            bur[rows, sl] = nr
            bui[rows, sl] = ni
            return nr, ni

        sr, si = lax.fori_loop(0, S5_TC, step, (sr_scr[:, sl], si_scr[:, sl]))
        sr_scr[:, sl] = sr
        si_scr[:, sl] = si
    for j in range(S5_Q):
        sl = slice(j * S5_QS, (j + 1) * S5_QS)
        yj = _dot1(bur[:, sl], cq_ref[0, j, :S5_QS, :]) + _dot1(bui[:, sl], cq_ref[0, j, S5_QS:, :])
        ytb[j] = yj
    for b in range(S5_SEQS):
        for j in range(S5_Q):
            y_ref[0, b, :, j * cw:(j + 1) * cw] = ytb[j, pl.ds(b, S5_TC, stride=S5_SEQS), :]

    @pl.when(c == pl.num_programs(2) - 1)
    def _():
        sfr_ref[0, 0] = sr_scr[...]
        sfi_ref[0, 0] = si_scr[...]


def _s5_scan(st, z3, bq, cq, lamr, lami, s0r, s0i, l):
    ng, nc = st.nb // S5_SEQS, st.L // S5_TC
    nrow = S5_TC * S5_SEQS
    chunk = lambda d, c: c + d * (nc - 1 - 2 * c)
    par = lambda g, d, c: (2 * l + d, 0, 0, 0)
    st_spec = pl.BlockSpec((1, 1, S5_SEQS, N_S5), lambda g, d, c: (g, d, 0, 0))
    return pl.pallas_call(
        _s5_scan_kernel,
        grid=(ng, 2, nc),
        in_specs=[pl.BlockSpec((S5_SEQS, S5_TC, D_GROUP), lambda g, d, c: (g, chunk(d, c), 0)),
                  pl.BlockSpec((1, S5_Q, D_GROUP // S5_Q, 2 * S5_QS), par),
                  pl.BlockSpec((1, S5_Q, 2 * S5_QS, D_GROUP // S5_Q), par),
                  pl.BlockSpec((1, 1, N_S5), lambda g, d, c: (2 * l + d, 0, 0)),
                  pl.BlockSpec((1, 1, N_S5), lambda g, d, c: (2 * l + d, 0, 0)),
                  st_spec, st_spec],
        out_specs=[pl.BlockSpec((1, S5_SEQS, S5_TC, D_GROUP), lambda g, d, c: (d, g, chunk(d, c), 0)),
                   st_spec, st_spec],
        out_shape=[jax.ShapeDtypeStruct((2, st.nb, st.L, D_GROUP), F32),
                   jax.ShapeDtypeStruct((ng, 2, S5_SEQS, N_S5), F32),
                   jax.ShapeDtypeStruct((ng, 2, S5_SEQS, N_S5), F32)],
        scratch_shapes=[pltpu.VMEM((S5_Q, nrow, LANE), F32), pltpu.VMEM((nrow, N_S5), F32),
                        pltpu.VMEM((nrow, N_S5), F32), pltpu.VMEM((S5_Q, nrow, LANE), F32),
                        pltpu.VMEM((S5_SEQS, N_S5), F32), pltpu.VMEM((S5_SEQS, N_S5), F32)],
        compiler_params=_cparams("parallel", "arbitrary", "arbitrary"),
        name="s5_scan",
    )(z3, bq, cq, lamr, lami, s0r, s0i)


S5_TM = 512


def _gelu_tanh(x):
    return 0.5 * x * (1.0 + jnp.tanh(math.sqrt(2.0 / math.pi) * (x + 0.044715 * (x * x * x))))


def _s5_out_kernel(u_ref, yf_ref, yb_ref, d_ref, w_ref, o_ref):
    y = yf_ref[0] + yb_ref[0] + d_ref[0] * u_ref[...]
    y = _gelu_tanh(y)
    o_ref[...] = (y * _sigmoid(_dot1(y, w_ref[0]))).astype(BF16)


def _s5_out(st, z, ydir, d_skip, w_glu, l):
    return pl.pallas_call(
        _s5_out_kernel,
        grid=(st.rows // S5_TM,),
        in_specs=[pl.BlockSpec((S5_TM, D_GROUP), lambda i: (i, 0)),
                  pl.BlockSpec((1, S5_TM, D_GROUP), lambda i: (0, i, 0)),
                  pl.BlockSpec((1, S5_TM, D_GROUP), lambda i: (1, i, 0)),
                  pl.BlockSpec((1, 1, D_GROUP), lambda i: (l, 0, 0)),
                  pl.BlockSpec((1, D_GROUP, D_GROUP), lambda i: (l, 0, 0))],
        out_specs=pl.BlockSpec((S5_TM, D_GROUP), lambda i: (i, 0)),
        out_shape=jax.ShapeDtypeStruct((st.rows, D_GROUP), BF16),
        compiler_params=_cparams("parallel"),
        name="s5_out",
    )(z, ydir, ydir, d_skip.reshape(DEPTH, 1, D_GROUP), w_glu)


NA_SCALE = NA_HEAD ** -0.5
NA_ROWS = DEC_SEQ // GRID_W
NA_KH = min(NA_WIN_H, NA_ROWS)
NA_QCOL = NA_OFF // LANE
NA_KCOL = (NA_OFF + D_GROUP) // LANE
NA_VCOL = (NA_OFF + 2 * D_GROUP) // LANE
NA_NDR = 2 * NA_WIN_H - 1
NA_AHEAD = 1


def _na_ctx_kernel(q_ref, k_ref, v_ref, o_ref, nk_ref, nv_ref):
    scores = []
    for hh in range(2):
        sl = slice(hh * NA_HEAD, (hh + 1) * NA_HEAD)
        k = k_ref[0, :, sl]
        nk_ref[0, hh] = k
        scores.append(_dot1(q_ref[0, :, sl], k, NT) * NA_SCALE)
    for hh in range(2):
        sl = slice(hh * NA_HEAD, (hh + 1) * NA_HEAD)
        v = v_ref[0, :, sl]
        nv_ref[0, hh] = v
        s = scores[hh]
        e = jnp.exp(s - jnp.max(s, axis=-1, keepdims=True))
        o = _dot1(e, v) / jnp.sum(e, axis=-1, keepdims=True)
        o_ref[0, :, sl] = o.astype(BF16)


def _na_ctx(st, z3):
    blk = lambda col: pl.BlockSpec((1, st.L, LANE), lambda b, p: (b, 0, col + p))
    kv_spec = pl.BlockSpec((1, 2, st.L, NA_HEAD), lambda b, p: (b, p, 0, 0))
    kv_shape = jax.ShapeDtypeStruct((st.nb, NA_HEADS, st.L, NA_HEAD), F32)
    return pl.pallas_call(
        _na_ctx_kernel,
        grid=(st.nb, NA_HEADS // 2),
        in_specs=[blk(NA_QCOL), blk(NA_KCOL), blk(NA_VCOL)],
        out_specs=[pl.BlockSpec((1, st.L, LANE), lambda b, p: (b, 0, p)), kv_spec, kv_spec],
        out_shape=[jax.ShapeDtypeStruct((st.nb, st.L, D_GROUP), BF16), kv_shape, kv_shape],
        compiler_params=_cparams("parallel", "parallel"),
        name="na_context",
    )(z3, z3, z3)


def _na_nbr_kernel(q_ref, k_ref, v_ref, ck_ref, cv_ref, bias_ref, o_ref):
    nloc = NA_KH * GRID_W
    qc = _iota((GRID_W, nloc), 0)
    kc = _iota((GRID_W, nloc), 1) % GRID_W
    cs = jnp.clip(qc - NA_WIN_W // 2, 0, GRID_W - NA_WIN_W)
    col_in = (kc >= cs) & (kc < cs + NA_WIN_W)

    def scores(hh, r):
        sl = slice(hh * NA_HEAD, (hh + 1) * NA_HEAD)
        rs = min(max(r - NA_KH // 2, 0), NA_ROWS - NA_KH)
        q = q_ref[0, r * GRID_W:(r + 1) * GRID_W, sl]
        k = k_ref[0, rs * GRID_W:rs * GRID_W + nloc, sl]
        off = (rs - r + NA_WIN_H - 1) * GRID_W
        s_loc = _dot1(q, k, NT) * NA_SCALE + bias_ref[hh, :, off:off + nloc]
        s_loc = jnp.where(col_in, s_loc, NEG_INF)
        s_ctx = _dot1(q, ck_ref[0, 0, hh], NT) * NA_SCALE
        return s_loc, s_ctx

    def finish(hh, r, s_loc, s_ctx):
        sl = slice(hh * NA_HEAD, (hh + 1) * NA_HEAD)
        rs = min(max(r - NA_KH // 2, 0), NA_ROWS - NA_KH)
        v = v_ref[0, rs * GRID_W:rs * GRID_W + nloc, sl]
        m = jnp.maximum(jnp.max(s_loc, axis=-1, keepdims=True), jnp.max(s_ctx, axis=-1, keepdims=True))
        e_loc = jnp.exp(s_loc - m)
        e_ctx = jnp.exp(s_ctx - m)
        den = jnp.sum(e_loc, axis=-1, keepdims=True) + jnp.sum(e_ctx, axis=-1, keepdims=True)
        o = (_dot1(e_loc, v) + _dot1(e_ctx, cv_ref[0, 0, hh])) / den
        o_ref[0, r * GRID_W:(r + 1) * GRID_W, sl] = o.astype(BF16)

    blocks = [(hh, r) for hh in range(2) for r in range(NA_ROWS)]
    pending = []
    for i, blk in enumerate(blocks):
        pending.append(scores(*blk))
        if i >= NA_AHEAD:
            finish(*blocks[i - NA_AHEAD], *pending.pop(0))
    for j in range(len(blocks) - NA_AHEAD, len(blocks)):
        finish(*blocks[j], *pending.pop(0))


def _na_bias_table(rpb):
    qc = np.arange(GRID_W)[:, None]
    kc = np.arange(GRID_W)[None, :]
    dc = np.clip(kc - qc, -(NA_WIN_W - 1), NA_WIN_W - 1) + (NA_WIN_W - 1)
    t = rpb[:, :, :, dc]
    return jnp.transpose(t, (0, 1, 3, 2, 4)).reshape(DEPTH, NA_HEADS, GRID_W, NA_NDR * GRID_W)


def _na_nbr(st, z3, cache_k, cache_v, bias_tab, l):
    blk = lambda col: pl.BlockSpec((1, st.L, LANE), lambda b, p: (b, 0, col + p))
    cspec = pl.BlockSpec((1, 1, 2, PAST_LEN, NA_HEAD), lambda b, p: (b, l, p, 0, 0))
    return pl.pallas_call(
        _na_nbr_kernel,
        grid=(st.nb, NA_HEADS // 2),
        in_specs=[blk(NA_QCOL), blk(NA_KCOL), blk(NA_VCOL), cspec, cspec,
                  pl.BlockSpec((None, 2, GRID_W, NA_NDR * GRID_W), lambda b, p: (l, p, 0, 0))],
        out_specs=pl.BlockSpec((1, st.L, LANE), lambda b, p: (b, 0, p)),
        out_shape=jax.ShapeDtypeStruct((st.nb, st.L, D_GROUP), BF16),
        compiler_params=_cparams("parallel", "parallel"),
        name="na_neighbourhood",
    )(z3, z3, z3, cache_k, cache_v, bias_tab)


ML_COL = ML_OFF // LANE
ML_SCALE = ML_HEAD ** -0.5


def _log_sigmoid(x):
    return -_softplus(-x)


def _rope_tables(L):
    half = ML_HEAD // 2
    quarter = half // 2
    t = np.arange(L)
    inv_freq = ROPE_BASE ** (-np.arange(quarter, dtype=np.float32) / quarter)

    def tabs(pos):
        ang = pos.astype(np.float32)[:, None] * inv_freq[None, :].astype(np.float32)
        c, s = np.cos(ang), np.sin(ang)
        return np.concatenate([c, c], axis=-1), np.concatenate([-s, s], axis=-1)

    c1, s1 = tabs(t // GRID_W)
    c2, s2 = tabs(t % GRID_W)
    return (jnp.asarray(np.concatenate([c1, c2], axis=-1), F32),
            jnp.asarray(np.concatenate([s1, s2], axis=-1), F32))


def _ml_kernel(*refs, L, rotary, layer):
    nh = ML_HEADS
    ib_ref, fb_ref = refs[0], refs[1]
    q_refs, k_refs, v_refs, o_refs = (refs[2 + i * nh:2 + (i + 1) * nh] for i in range(4))
    (g_ref, cos_ref, sin_ref, c0_ref, n0_ref, m0_ref, lng_ref, lnb_ref,
     y_ref, cf_ref, nf_ref, mf_ref, qs, ks, hf, hb, c_scr, n_scr, m_scr) = refs[2 + 4 * nh:]
    T = ML_CHUNK
    nc = L // T

    if rotary:
        first = (_iota((L, ML_HEAD), 1) % (ML_HEAD // 2)) < ML_HEAD // 4

        def rope(x):
            quarter = ML_HEAD // 4
            partner = jnp.where(first, pltpu.roll(x, ML_HEAD - quarter, axis=1), pltpu.roll(x, quarter, axis=1))
            return x * cos_ref[...] + partner * sin_ref[...]
    else:
        rope = lambda x: x
    for h in range(nh):
        qs[h] = rope(q_refs[h][0]) * ML_SCALE
        ks[h] = rope(k_refs[h][0])
        for d in range(2):
            c_scr[d, h] = c0_ref[0, d, h]
            n_scr[d, h] = n0_ref[0, h, d:d + 1, :]
            m_scr[d, h] = m0_ref[0, h, d:d + 1, :]

    tj = _iota((T, T), 0)
    ts = _iota((T, T), 1)
    ones = jnp.ones((T, T), F32)
    upto_row = (tj <= ts, tj >= ts)
    upto_col = (ts <= tj, ts >= tj)

    def body(ci, carry):
        ch = []
        for d in range(2):
            cd = ci if d == 0 else nc - 1 - ci
            rows = pl.ds(pl.multiple_of(cd * T, T), T)
            for h in range(nh):
                g = g_ref[0, h, cd]
                ig = g[:, d:d + 1] + ib_ref[layer * 2 * nh + d * nh + h]
                fg = g[:, 2 + d:3 + d] + fb_ref[layer * 2 * nh + d * nh + h]
                lf = jnp.broadcast_to(_log_sigmoid(fg), (T, T))
                igb = jnp.broadcast_to(ig, (T, T))
                q = dict(d=d, h=h, rows=rows, ig=ig, qc=qs[h, rows, :], kc=ks[h, rows, :], vc=v_refs[h][0, rows, :])
                q['bcol'] = _dot_sel_l(jnp.where(upto_col[d], 1.0, 0.0), lf)
                q['brow_i'] = _dot_sel_l(ones, jnp.where(upto_row[d], lf, 0.0) - jnp.where(tj == ts, igb, 0.0))
                ch.append(q)
        for q in ch:
            q['qk'] = _dot1(q['qc'], q['kc'], NT)
            q['c_old'] = c_scr[q['d'], q['h']]
            q['qc_c'] = _dot1(q['qc'], q['c_old'])
        for q in ch:
            d, h = q['d'], q['h']
            dmat = jnp.where(upto_col[d], q['bcol'] - q['brow_i'], -jnp.inf)
            b1 = q['bcol'][:, 0:1]
            m_prev = m_scr[d, h]
            inter = b1 + m_prev
            m_t = jnp.maximum(inter, jnp.max(dmat, axis=-1, keepdims=True))
            qk = q['qk'] * jnp.exp(dmat - m_t)
            w_inter = jnp.exp(inter - m_t)
            n_old = n_scr[d, h]
            den = w_inter * jnp.sum(q['qc'] * n_old, axis=-1, keepdims=True) + jnp.sum(qk, axis=-1, keepdims=True)
            q['scale'] = 1.0 / jnp.maximum(jnp.abs(den), jnp.exp(-m_t))
            q['inter_part'] = w_inter * q['qc_c']
            q['intra'] = _dot1(qk, q['vc'])
            b_last = b1[T - 1:T, :] if d == 0 else b1[0:1, :]
            g_s = b_last - b1 + q['ig']
            m_new = jnp.maximum(b_last + m_prev, jnp.max(g_s, axis=0, keepdims=True))
            w_old = jnp.exp(b_last + m_prev - m_new)
            w_s = jnp.exp(g_s - m_new)
            q['c_new'] = _dot1(q['kc'], w_s * q['vc'], TN)
            q['w_old'] = w_old
            n_scr[d, h] = w_old * n_old + jnp.sum(w_s * q['kc'], axis=0, keepdims=True)
            m_scr[d, h] = m_new
        for q in ch:
            d, h = q['d'], q['h']
            hcur = (q['inter_part'] + q['intra']) * q['scale']
            if d == 0:
                hf[h, q['rows'], :] = hcur
            else:
                hb[h, q['rows'], :] = hcur
            c_scr[d, h] = q['w_old'] * q['c_old'] + q['c_new']
        return carry

    lax.fori_loop(0, nc, body, 0)

    for h in range(nh):
        hsum = hf[h] + hb[h]
        mu = jnp.mean(hsum, axis=-1, keepdims=True)
        dv = hsum - mu
        var = jnp.mean(dv * dv, axis=-1, keepdims=True)
        hn = dv * lax.rsqrt(var + ML_GN_EPS)
        y_ref[0, :, h * ML_HEAD:(h + 1) * ML_HEAD] = (
            _sigmoid(o_refs[h][0]) * (hn * lng_ref[0, :, h * ML_HEAD:(h + 1) * ML_HEAD]
                                      + lnb_ref[0, :, h * ML_HEAD:(h + 1) * ML_HEAD])).astype(BF16)
        for d in range(2):
            cf_ref[0, d, h] = c_scr[d, h]
            nf_ref[0, h, d:d + 1, :] = n_scr[d, h]
            mf_ref[0, h, d:d + 1, :] = m_scr[d, h]


def _ml_mixer(st, z3, zg, i_bias, f_bias, ln_g, ln_b, c0, n0, m0, l):
    L, nb, nh = st.L, st.nb, ML_HEADS
    nc = L // ML_CHUNK
    gates = zg.reshape(nb, L, 2, 2, nh).transpose(0, 4, 1, 2, 3).reshape(nb, nh, nc, ML_CHUNK, 4)
    cos_t, sin_t = _rope_tables(L)
    blk = lambda col: pl.BlockSpec((1, L, LANE), lambda b: (b, 0, col))
    head_blocks = [blk(ML_COL + part * nh + h) for part in range(4) for h in range(nh)]
    smem = pl.BlockSpec(memory_space=pltpu.SMEM)
    tab = pl.BlockSpec((L, ML_HEAD), lambda b: (0, 0))
    c_spec = pl.BlockSpec((1, 2, nh, ML_HEAD, ML_HEAD), lambda b: (b, 0, 0, 0, 0))
    n_spec = pl.BlockSpec((1, nh, 2, ML_HEAD), lambda b: (b, 0, 0, 0))
    m_spec = pl.BlockSpec((1, nh, 2, 1), lambda b: (b, 0, 0, 0))
    par = pl.BlockSpec((1, 1, D_GROUP), lambda b: (l, 0, 0))
    return pl.pallas_call(
        functools.partial(_ml_kernel, L=L, rotary=st.latent, layer=l),
        grid=(nb,),
        in_specs=[smem, smem] + head_blocks + [
            pl.BlockSpec((1, nh, nc, ML_CHUNK, 4), lambda b: (b, 0, 0, 0, 0)),
            tab, tab, c_spec, n_spec, m_spec, par, par],
        out_specs=[pl.BlockSpec((1, L, D_GROUP), lambda b: (b, 0, 0)), c_spec, n_spec, m_spec],
        out_shape=[jax.ShapeDtypeStruct((nb, L, D_GROUP), BF16),
                   jax.ShapeDtypeStruct((nb, 2, nh, ML_HEAD, ML_HEAD), F32),
                   jax.ShapeDtypeStruct((nb, nh, 2, ML_HEAD), F32),
                   jax.ShapeDtypeStruct((nb, nh, 2, 1), F32)],
        scratch_shapes=[pltpu.VMEM((nh, L, ML_HEAD), F32), pltpu.VMEM((nh, L, ML_HEAD), F32),
                        pltpu.VMEM((nh, L, ML_HEAD), F32), pltpu.VMEM((nh, L, ML_HEAD), F32),
                        pltpu.VMEM((2, nh, ML_HEAD, ML_HEAD), F32), pltpu.VMEM((2, nh, 1, ML_HEAD), F32),
                        pltpu.VMEM((2, nh, 1, 1), F32)],
        compiler_params=_cparams("parallel"),
        name="mlstm",
    )(i_bias.reshape(-1), f_bias.reshape(-1), *([z3] * (4 * nh)), gates, cos_t, sin_t, c0, n0, m0,
      ln_g.reshape(DEPTH, 1, D_GROUP), ln_b.reshape(DEPTH, 1, D_GROUP))


RW_TL = 256
RW_T = 64
RW_GROUP = 16
RW_RCOL = RW_OFF // D_GROUP
RW_LCOL = (RW_OFF + 3 * D_GROUP) // (2 * LANE)
N_LORA = RW_LORA_W + RW_LORA_A + RW_LORA_G


def _head_ones():
    return jnp.asarray(np.kron(np.eye(RW_HEADS, dtype=np.float32), np.ones((RW_HEAD, RW_HEAD), np.float32)))


def _rw_front_kernel(r_ref, k_ref, v_ref, lo_ref, rp_ref, kp_ref, vp_ref, lp_ref, rn_ref, kn_ref, vn_ref, ln_ref,
                     mup_ref, mun_ref, w0_ref, w2_ref, a0_ref, a2_ref, g2_ref, kk_ref, ka_ref, hones_ref,
                     ro_ref, vo_ref, kko_ref, go_ref, kd_ref, bd_ref, lw_ref):
    t = pl.program_id(1)
    first = t == 0
    last = t == pl.num_programs(1) - 1
    row = _iota((RW_TL, 1), 0)

    def shift(x_ref, p_ref, n_ref, lo, hi):
        x = x_ref[0]
        prev_edge = jnp.where(first, 0.0, p_ref[0, SUB - 1:SUB, :])
        next_edge = jnp.where(last, 0.0, n_ref[0, 0:1, :])
        prev = jnp.where(row == 0, prev_edge, pltpu.roll(x, 1, axis=0))
        nxt = jnp.where(row == RW_TL - 1, next_edge, pltpu.roll(x, RW_TL - 1, axis=0))
        return x + mup_ref[0][:, lo:hi] * (prev - x) + mun_ref[0][:, lo:hi] * (nxt - x)

    g = D_GROUP
    r = shift(r_ref, rp_ref, rn_ref, 0, g)
    k = shift(k_ref, kp_ref, kn_ref, g, 2 * g)
    v = shift(v_ref, vp_ref, vn_ref, 2 * g, 3 * g)
    lo = shift(lo_ref, lp_ref, ln_ref, 3 * g, 3 * g + N_LORA)
    zw = lo[:, :RW_LORA_W]
    za = lo[:, RW_LORA_W:RW_LORA_W + RW_LORA_A]
    zg = lo[:, RW_LORA_W + RW_LORA_A:]
    ro_ref[0] = r
    vo_ref[0] = v
    kk = k * kk_ref[0]
    ssq = _seg_sum(kk * kk, hones_ref[...])
    kk = kk * lax.rsqrt(ssq + 1e-12)
    kko_ref[0] = kk
    go_ref[0] = _dot3(_sigmoid(zg), g2_ref[0])
    tw = jnp.tanh(zw)
    for d in range(2):
        w_log = -_softplus(-(w0_ref[0, d:d + 1, :] + _dot3(tw, w2_ref[0, d]))) - 0.5
        lw_ref[d, 0] = -jnp.exp(w_log)
        a = _sigmoid(a0_ref[0, d:d + 1, :] + _dot3(za, a2_ref[0, d]))
        kd_ref[d, 0] = k * (1.0 + (a - 1.0) * ka_ref[0])
        bd_ref[d, 0] = kk * a


def _rw_front(st, z3, mu_prev, mu_next, w0, w2, a0, a2, g2, k_k, k_a, l):
    nb, L = st.nb, st.L
    nt = L // RW_TL
    tpb = RW_TL // SUB
    main = lambda w, col: pl.BlockSpec((1, RW_TL, w), lambda b, t: (b, t, col))
    prev = lambda w, col: pl.BlockSpec((1, SUB, w), lambda b, t: (b, jnp.maximum(t * tpb - 1, 0), col))
    nxt = lambda w, col: pl.BlockSpec((1, SUB, w), lambda b, t: (b, jnp.minimum((t + 1) * tpb, L // SUB - 1), col))
    cols = [(D_GROUP, RW_RCOL), (D_GROUP, RW_RCOL + 1), (D_GROUP, RW_RCOL + 2), (2 * LANE, RW_LCOL)]
    lay = lambda *shape: pl.BlockSpec((1,) + shape, lambda b, t: (l,) + (0,) * len(shape))
    out1 = pl.BlockSpec((1, RW_TL, D_GROUP), lambda b, t: (b, t, 0))
    out2 = pl.BlockSpec((2, 1, RW_TL, D_GROUP), lambda b, t: (0, b, t, 0))
    s1 = jax.ShapeDtypeStruct((nb, L, D_GROUP), F32)
    s2 = jax.ShapeDtypeStruct((2, nb, L, D_GROUP), F32)
    return pl.pallas_call(
        _rw_front_kernel,
        grid=(nb, nt),
        in_specs=[main(*c) for c in cols] + [prev(*c) for c in cols] + [nxt(*c) for c in cols] + [
            lay(1, RW_IN), lay(1, RW_IN), lay(2, D_GROUP), lay(2, RW_LORA_W, D_GROUP), lay(2, D_GROUP),
            lay(2, RW_LORA_A, D_GROUP), lay(RW_LORA_G, D_GROUP), lay(1, D_GROUP), lay(1, D_GROUP),
            pl.BlockSpec((D_GROUP, D_GROUP), lambda b, t: (0, 0))],
        out_specs=[out1, out1, out1, out1, out2, out2, out2],
        out_shape=[s1, s1, s1, s1, s2, s2, s2],
        compiler_params=_cparams("parallel", "parallel"),
        name="rwkv_front",
    )(*([z3] * 12), mu_prev.reshape(DEPTH, 1, RW_IN), mu_next.reshape(DEPTH, 1, RW_IN), w0, w2, a0, a2, g2,
      k_k.reshape(DEPTH, 1, D_GROUP), k_a.reshape(DEPTH, 1, D_GROUP), _head_ones())


def _rw_core_kernel(rf_ref, vf_ref, kkf_ref, kdf_ref, bdf_ref, lwf_ref,
                    rb_ref, vb_ref, kkb_ref, kdb_ref, bdb_ref, lwb_ref, s0_ref,
                    yf_ref, yb_ref, sf_ref, s_scr):
    c = pl.program_id(1)
    T = RW_T

    @pl.when(c == 0)
    def _():
        s_scr[...] = s0_ref[0]

    tj = _iota((T, T), 0)
    ts = _iota((T, T), 1)
    dirs = ((rf_ref, vf_ref, kkf_ref, kdf_ref, bdf_ref, lwf_ref, yf_ref),
            (rb_ref, vb_ref, kkb_ref, kdb_ref, bdb_ref, lwb_ref, yb_ref))
    ch = []
    for d, (r_ref, v_ref, kk_ref, kd_ref, bd_ref, lw_ref, y_ref) in enumerate(dirs):
        if d == 0:
            incl = ts <= tj
            strict = ts < tj
            last = T - 1
        else:
            incl = ts >= tj
            strict = ts > tj
            last = 0
        lw = lw_ref[0, 0]
        cum = _dot_sel_l(jnp.where(incl, 1.0, 0.0), lw)
        w_in = jnp.exp(cum)
        w_inv = jnp.exp(-cum)
        w_ex = jnp.exp(cum - lw)
        kap_a = kk_ref[0] * w_ex
        bet_a = bd_ref[0, 0] * w_inv
        khat_a = kd_ref[0, 0] * w_inv
        rho_a = r_ref[0] * w_in
        w_tot = w_in[last:last + 1, :]
        v_a = v_ref[0]
        for h in range(RW_HEADS):
            sl = slice(h * RW_HEAD, (h + 1) * RW_HEAD)
            ch.append(dict(d=d, h=h, sl=sl, incl=incl, strict=strict, y_ref=y_ref, w_tot=w_tot[:, sl],
                           kap=kap_a[:, sl], bet=bet_a[:, sl], khat=khat_a[:, sl], rho=rho_a[:, sl], v=v_a[:, sl]))
    def solve(chs):
        for q in chs:
            kr = jnp.concatenate([q['kap'], q['rho']], axis=0)
            bk = jnp.concatenate([q['bet'], q['khat']], axis=0)
            q['gram'] = _dot1(kr, bk, NT)
        for q in chs:
            gram = q.pop('gram')
            q['n'] = -jnp.where(q['strict'], gram[:T, :T], 0.0)
            l_k = jnp.where(q['strict'], gram[:T, T:], 0.0)
            q['m_b'] = jnp.where(q['incl'], gram[T:, :T], 0.0)
            m_k = jnp.where(q['incl'], gram[T:, T:], 0.0)
            q['lmv'] = _dot1(jnp.concatenate([l_k, m_k], axis=0), q['v'])
        for q in chs:
            q['x'] = jnp.concatenate([q['kap'], q['lmv'][:T]], axis=1)
        for lvl in range(6):
            mm = _dot3 if lvl < 2 else _dot1
            for q in chs:
                if lvl < 5:
                    nx = mm(q['n'], jnp.concatenate([q['x'], q['n']], axis=1))
                    q['n'], q['x'] = nx[:, 2 * RW_HEAD:], q['x'] + nx[:, :2 * RW_HEAD]
                else:
                    q['x'] = q['x'] + _dot1(q['n'], q['x'])
        for q in chs:
            q['p'] = _dot1(q['m_b'], q['x'])
            xb = _dot1(q['x'], q['bet'], TN)
            q['a_m'] = xb[:RW_HEAD]
            q['d_m'] = _dot1(q['v'], q['khat'], TN) - xb[RW_HEAD:]
        for q in chs:
            rho_p = q['rho'] - q['p'][:, :RW_HEAD]
            y_v = q['lmv'][T:] - q['p'][:, RW_HEAD:]
            s_old = s_scr[q['d'], q['h']]
            q['y_ref'][0, :, q['sl']] = _dot1(rho_p, s_old, NT) + y_v
            s_scr[q['d'], q['h']] = (s_old - _dot1(s_old, q['a_m']) + q['d_m']) * q['w_tot']

    for g0 in range(0, len(ch), RW_GROUP):
        solve(ch[g0:g0 + RW_GROUP])

    @pl.when(c == pl.num_programs(1) - 1)
    def _():
        sf_ref[0] = s_scr[...]


def _rw_core(st, r, v, kk, kd, bd, lw, s0):
    nb, L = st.nb, st.L
    nc = L // RW_T
    f1 = pl.BlockSpec((1, RW_T, D_GROUP), lambda b, c: (b, c, 0))
    b1 = pl.BlockSpec((1, RW_T, D_GROUP), lambda b, c: (b, nc - 1 - c, 0))
    f2 = pl.BlockSpec((1, 1, RW_T, D_GROUP), lambda b, c: (0, b, c, 0))
    b2 = pl.BlockSpec((1, 1, RW_T, D_GROUP), lambda b, c: (1, b, nc - 1 - c, 0))
    s_spec = pl.BlockSpec((1, 2, RW_HEADS, RW_HEAD, RW_HEAD), lambda b, c: (b, 0, 0, 0, 0))
    ys = jax.ShapeDtypeStruct((nb, L, D_GROUP), F32)
    return pl.pallas_call(
        _rw_core_kernel,
        grid=(nb, nc),
        in_specs=[f1, f1, f1, f2, f2, f2, b1, b1, b1, b2, b2, b2, s_spec],
        out_specs=[f1, b1, s_spec],
        out_shape=[ys, ys, jax.ShapeDtypeStruct((nb, 2, RW_HEADS, RW_HEAD, RW_HEAD), F32)],
        scratch_shapes=[pltpu.VMEM((2, RW_HEADS, RW_HEAD, RW_HEAD), F32)],
        compiler_params=_cparams("parallel", "arbitrary"),
        name="rwkv_core",
    )(r, v, kk, kd, bd, lw, r, v, kk, kd, bd, lw, s0)


RWO_TM = 512


def _rw_out_kernel(yf_ref, yb_ref, r_ref, v_ref, kd0_ref, kd1_ref, g_ref, lng_ref, lnb_ref, rk_ref, hones_ref, o_ref):
    y = yf_ref[...] + yb_ref[...]
    inv = 1.0 / RW_HEAD
    mu = _seg_sum(y, hones_ref[...]) * inv
    dv = y - mu
    var = _seg_sum(dv * dv, hones_ref[...]) * inv
    yn = dv * lax.rsqrt(var + RW_GN_EPS) * lng_ref[0] + lnb_ref[0]
    kmean = 0.5 * (kd0_ref[0] + kd1_ref[0])
    bonus = _seg_sum(r_ref[...] * kmean * rk_ref[0], hones_ref[...]) * v_ref[...]
    o_ref[...] = ((yn + bonus) * g_ref[...]).astype(BF16)


def _rw_out(st, yf, yb, r, v, kd, g, ln_g, ln_b, r_k, l):
    rows = st.rows
    flat = lambda a: a.reshape(rows, D_GROUP)
    row = pl.BlockSpec((RWO_TM, D_GROUP), lambda i: (i, 0))
    lay = pl.BlockSpec((1, 1, D_GROUP), lambda i: (l, 0, 0))
    kd2 = kd.reshape(2, rows, D_GROUP)
    return pl.pallas_call(
        _rw_out_kernel,
        grid=(rows // RWO_TM,),
        in_specs=[row, row, row, row,
                  pl.BlockSpec((1, RWO_TM, D_GROUP), lambda i: (0, i, 0)),
                  pl.BlockSpec((1, RWO_TM, D_GROUP), lambda i: (1, i, 0)),
                  row, lay, lay, lay, pl.BlockSpec((D_GROUP, D_GROUP), lambda i: (0, 0))],
        out_specs=row,
        out_shape=jax.ShapeDtypeStruct((rows, D_GROUP), BF16),
        compiler_params=_cparams("parallel"),
        name="rwkv_out",
    )(flat(yf), flat(yb), flat(r), flat(v), kd2, kd2, flat(g), ln_g.reshape(DEPTH, 1, D_GROUP),
      ln_b.reshape(DEPTH, 1, D_GROUP), r_k.reshape(DEPTH, 1, D_GROUP), _head_ones())


def _trunk_layer(st, x, l, mod4, p, states, ctx_kv):
    nb, L = st.nb, st.L
    z, zg = _zproj(st, x, mod4, p['w_in'], p['w_gate_cols'], l)
    z3 = z.reshape(nb, L, D_Z)

    ydir, sfr, sfi = _s5_scan(st, z3, p['s5_bq'], p['s5_cq'], p['s5_lamr'], p['s5_lami'],
                              states['s5_re'], states['s5_im'], l)
    y_s5 = _s5_out(st, z, ydir.reshape(2, st.rows, D_GROUP), p['s5_d'], p['s5_w_glu'], l)

    r, v, kk, g, kd, bd, lw = _rw_front(st, z3, p['rw_mu_prev'], p['rw_mu_next'], p['rw_w0'], p['rw_w2'],
                                        p['rw_a0'], p['rw_a2'], p['rw_g2'], p['rw_k_k'], p['rw_k_a'], l)
    yf, yb, rw_s = _rw_core(st, r, v, kk, kd, bd, lw, states['rw'])
    y_rw = _rw_out(st, yf, yb, r, v, kd, g, p['rw_ln_g'], p['rw_ln_b'], p['rw_r_k'], l)

    if ctx_kv is None:
        y_na, nk, nv = _na_ctx(st, z3)
    else:
        y_na = _na_nbr(st, z3, ctx_kv[0], ctx_kv[1], p['na_bias'], l)
        nk = nv = None

    y_ml, ml_c, ml_n, ml_m = _ml_mixer(st, z3, zg, p['ml_i_bias'], p['ml_f_bias'], p['ml_ln_g'], p['ml_ln_b'],
                                       states['ml_c'], states['ml_n'], states['ml_m'], l)

    ys = (y_s5, y_rw, y_na.reshape(st.rows, D_GROUP), y_ml.reshape(st.rows, D_GROUP))
    tail = _oproj(st, ys, x, mod4, p['w_out'], p['ln1_g'], p['ln1_b'], p['w_router'], p['b_router'], l)
    return tail, (nk, nv, sfr, sfi, rw_s, ml_c, ml_n, ml_m)


def _moe_and_ln2(tails, mod4, p, l):
    xs = []
    for st, (x1, h2, gates, keep) in zip((PROMPT, LATENT), tails):
        f = _moe_sorted(st, h2, gates, keep, p['moe_w_gate'], p['moe_w_up'], p['moe_w_down'], l)
        xs.append(_ln2f(st, x1, f, mod4, p['ln2_g'], p['ln2_b'], l))
    return xs


def kernel(x_prompt, x_sample, cache_nat_k, cache_nat_v, state_s5_re, state_s5_im, state_rwkv, state_mlstm_c, state_mlstm_n, state_mlstm_m, c, c_ctx, w_mod, b_mod, w_in, w_out, s5_lam_re, s5_lam_im, s5_log_step, s5_b_re, s5_b_im, s5_c_re, s5_c_im, s5_d, s5_w_glu, rw_mu_prev, rw_mu_next, rw_w0, rw_w2, rw_a0, rw_a2, rw_g2, rw_k_k, rw_k_a, rw_r_k, rw_ln_g, rw_ln_b, na_rpb, ml_i_bias, ml_f_bias, ml_ln_g, ml_ln_b, ln1_g, ln1_b, ln2_g, ln2_b, w_router, b_router, moe_w_gate, moe_w_up, moe_w_down):
    dt = x_prompt.dtype
    cond = jnp.concatenate([c_ctx[None, :], c, jnp.zeros((MOD_ROWS - 1 - DEC_BATCH, D_MODEL), F32)], axis=0)
    mod4 = _modulation(cond, w_mod, b_mod).reshape(DEPTH, MOD_ROWS, 1, 6 * D_MODEL)

    lbr, lbi, bbr, bbi = _s5_prep(s5_lam_re, s5_lam_im, s5_log_step, s5_b_re, s5_b_im)
    s5_bq, s5_cq, s5_lamr, s5_lami = _s5_block_params(lbr, lbi, bbr, bbi, s5_c_re, s5_c_im)
    w_in_b = w_in.astype(BF16)
    p = dict(w_in=w_in_b, w_gate_cols=w_in_b[:, :, D_Z:], w_out=w_out.astype(BF16),
             s5_bq=s5_bq, s5_cq=s5_cq, s5_lamr=s5_lamr, s5_lami=s5_lami, s5_d=s5_d, s5_w_glu=s5_w_glu,
             rw_mu_prev=rw_mu_prev, rw_mu_next=rw_mu_next, rw_w0=rw_w0, rw_w2=rw_w2, rw_a0=rw_a0, rw_a2=rw_a2,
             rw_g2=rw_g2, rw_k_k=rw_k_k, rw_k_a=rw_k_a, rw_r_k=rw_r_k, rw_ln_g=rw_ln_g, rw_ln_b=rw_ln_b,
             na_bias=_na_bias_table(na_rpb), ml_i_bias=ml_i_bias, ml_f_bias=ml_f_bias, ml_ln_g=ml_ln_g,
             ml_ln_b=ml_ln_b, ln1_g=ln1_g, ln1_b=ln1_b, ln2_g=ln2_g, ln2_b=ln2_b, w_router=w_router,
             b_router=b_router, moe_w_gate=moe_w_gate.astype(BF16), moe_w_up=moe_w_up.astype(BF16),
             moe_w_down=moe_w_down.astype(BF16))

    gp = BATCH // S5_SEQS
    zero_states = dict(
        s5_re=jnp.zeros((gp, 2, S5_SEQS, N_S5), F32), s5_im=jnp.zeros((gp, 2, S5_SEQS, N_S5), F32),
        rw=jnp.zeros((BATCH, 2, RW_HEADS, RW_HEAD, RW_HEAD), F32),
        ml_c=jnp.zeros((BATCH, 2, ML_HEADS, ML_HEAD, ML_HEAD), F32),
        ml_n=jnp.zeros((BATCH, ML_HEADS, 2, ML_HEAD), F32), ml_m=jnp.zeros((BATCH, ML_HEADS, 2, 1), F32))

    xp = x_prompt.reshape(PROMPT.rows, D_MODEL)
    xs = x_sample.reshape(LATENT.rows, D_MODEL)
    outs = [[] for _ in range(8)]
    for l in range(DEPTH):
        tail_p, ctx_t = _trunk_layer(PROMPT, xp, l, mod4, p, zero_states, None)
        for acc, t in zip(outs, ctx_t):
            acc.append(t)
        lat_states = dict(
            s5_re=state_s5_re[:, l].reshape(DEC_BATCH, 2, N_S5).transpose(1, 0, 2)[None],
            s5_im=state_s5_im[:, l].reshape(DEC_BATCH, 2, N_S5).transpose(1, 0, 2)[None],
            rw=state_rwkv[:, l], ml_c=state_mlstm_c[:, l],
            ml_n=state_mlstm_n[:, l].transpose(0, 2, 1, 3), ml_m=state_mlstm_m[:, l].transpose(0, 2, 1)[..., None])
        tail_s, _ = _trunk_layer(LATENT, xs, l, mod4, p, lat_states, (cache_nat_k, cache_nat_v))
        xp, xs = _moe_and_ln2((tail_p, tail_s), mod4, p, l)

    nk, nv, s5r, s5i, rw, mc, mn, mm = [jnp.stack(t, axis=1) for t in outs]

    def s5_state(t):
        return t.transpose(0, 3, 1, 2, 4).reshape(BATCH, DEPTH, 2, S5_GROUPS, S5_STATE)

    return (xp.reshape(BATCH, SEQ, D_MODEL), xs.reshape(DEC_BATCH, DEC_SEQ, D_MODEL),
            nk, nv, s5_state(s5r).astype(dt), s5_state(s5i).astype(dt), rw.astype(dt), mc.astype(dt),
            mn.transpose(0, 1, 3, 2, 4).astype(dt), mm[..., 0].transpose(0, 1, 3, 2).astype(dt))
```

```python
import functools
import math

import numpy as np
import jax
import jax.numpy as jnp
from jax import lax
from jax.experimental import pallas as pl
from jax.experimental.pallas import tpu as pltpu

F32 = jnp.float32
BF16 = jnp.bfloat16

D_MODEL = 2048
BATCH = 16
SEQ = 256
DEPTH = 4
DEC_BATCH = 8
DEC_SEQ = 1024
PAST_LEN = 256
GRID_W = 64
D_GROUP = D_MODEL // 4
S5_CH = 16
S5_GROUPS = D_GROUP // S5_CH
S5_STATE = 64
RW_HEAD = 64
RW_HEADS = D_GROUP // RW_HEAD
RW_LORA_W = 64
RW_LORA_A = 64
RW_LORA_G = 128
RW_GN_EPS = 64e-5
NA_HEAD = 64
NA_HEADS = D_GROUP // NA_HEAD
NA_WIN_H = 8
NA_WIN_W = 16
NEG_INF = -1e30
ML_HEAD = 128
ML_HEADS = D_GROUP // ML_HEAD
ML_CHUNK = 256
ML_GN_EPS = 1e-5
ROPE_BASE = 10000.0
N_EXPERTS = 16
N_EXPERT_GROUPS = 4
EXPERTS_PER_GROUP = N_EXPERTS // N_EXPERT_GROUPS
D_EXPERT = 512
DEEPNORM_ALPHA = (2 * DEPTH) ** 0.25
LN_EPS = 1e-5
S5_IN = D_GROUP
RW_IN = 3 * D_GROUP + RW_LORA_W + RW_LORA_A + RW_LORA_G
NA_IN = 3 * D_GROUP
ML_IN = 4 * D_GROUP + 4 * ML_HEADS
D_IN = S5_IN + RW_IN + NA_IN + ML_IN
N_GATE = 4 * ML_HEADS
D_Z = D_IN - N_GATE
RW_OFF = S5_IN
NA_OFF = S5_IN + RW_IN
ML_OFF = NA_OFF + NA_IN
MOD_ROWS = 16
LANE = 128
SUB = 8

VMEM_LIMIT = 56 * 1024 * 1024


def _cparams(*sem):
    return pltpu.CompilerParams(dimension_semantics=sem, vmem_limit_bytes=VMEM_LIMIT)


def _dg(a, b, dims):
    return lax.dot_general(a, b, (dims, ((), ())), preferred_element_type=F32)


NN = ((1,), (0,))
NT = ((1,), (1,))
TN = ((0,), (0,))


def _dot1(a, b, dims=NN):
    return _dg(a.astype(BF16), b.astype(BF16), dims)


def _split(x):
    hi = x.astype(BF16)
    lo = (x - hi.astype(F32)).astype(BF16)
    return hi, lo


def _dot3(a, b, dims=NN):
    ah, al = _split(a)
    bh, bl = _split(b)
    return _dg(ah, bh, dims) + (_dg(ah, bl, dims) + _dg(al, bh, dims))


def _dot_sel_l(sel, x, dims=NN):
    s = sel.astype(BF16)
    hi, lo = _split(x)
    return _dg(s, hi, dims) + _dg(s, lo, dims)


def _seg_sum(x, sel):
    s = sel.astype(BF16)
    hi, lo = _split(x)
    return _dg(hi, s, NN) + _dg(lo, s, NN)


def _sigmoid(x):
    return 1.0 / (1.0 + jnp.exp(-x))


def _silu(x):
    return x * _sigmoid(x)


def _softplus(x):
    return jnp.maximum(x, 0.0) + jnp.log(1.0 + jnp.exp(-jnp.abs(x)))


def _iota(shape, dim):
    return lax.broadcasted_iota(jnp.int32, shape, dim)


class _Stream:
    def __init__(self, nb, L, latent):
        self.nb, self.L, self.latent = nb, L, latent
        self.rows = nb * L

    def mod_row(self, tile, tile_rows):
        if not self.latent:
            return 0
        return 1 + (tile * tile_rows) // self.L


PROMPT = _Stream(BATCH, SEQ, False)
LATENT = _Stream(DEC_BATCH, DEC_SEQ, True)


MOD_TN = 768


def _mod_kernel(cond_ref, w_ref, b_ref, o_ref):
    c = _silu(cond_ref[...])
    o_ref[0] = _dot3(c, w_ref[0]) + b_ref[0]


def _modulation(cond, w_mod, b_mod):
    n = 6 * D_MODEL
    return pl.pallas_call(
        _mod_kernel,
        grid=(DEPTH, n // MOD_TN),
        in_specs=[pl.BlockSpec((MOD_ROWS, D_MODEL), lambda l, j: (0, 0)),
                  pl.BlockSpec((1, D_MODEL, MOD_TN), lambda l, j: (l, 0, j)),
                  pl.BlockSpec((1, 1, MOD_TN), lambda l, j: (l, 0, j))],
        out_specs=pl.BlockSpec((1, MOD_ROWS, MOD_TN), lambda l, j: (l, 0, j)),
        out_shape=jax.ShapeDtypeStruct((DEPTH, MOD_ROWS, n), F32),
        compiler_params=_cparams("parallel", "parallel"),
        name="modulation",
    )(cond, w_mod, b_mod.reshape(DEPTH, 1, n))


ZP_TM = 1024
ZP_TN = 256


def _zproj_kernel(x_ref, mod_ref, w_ref, wg_ref, z_ref, zg_ref, h_scr):
    j = pl.program_id(1)

    @pl.when(j == 0)
    def _():
        m = mod_ref[0, 0]
        shift1 = m[:, 0:D_MODEL]
        scale1 = m[:, D_MODEL:2 * D_MODEL]
        h = (x_ref[...] * (1.0 + scale1) + shift1).astype(BF16)
        h_scr[...] = h
        zg_ref[...] = jnp.dot(h, wg_ref[0], preferred_element_type=F32)

    z_ref[...] = jnp.dot(h_scr[...], w_ref[0], preferred_element_type=F32)


def _zproj(st, x, mod4, w_in, w_gate_cols, l):
    return pl.pallas_call(
        _zproj_kernel,
        grid=(st.rows // ZP_TM, D_Z // ZP_TN),
        in_specs=[pl.BlockSpec((ZP_TM, D_MODEL), lambda i, j: (i, 0)),
                  pl.BlockSpec((1, 1, 1, 6 * D_MODEL), lambda i, j: (l, st.mod_row(i, ZP_TM), 0, 0)),
                  pl.BlockSpec((1, D_MODEL, ZP_TN), lambda i, j: (l, 0, j)),
                  pl.BlockSpec((1, D_MODEL, N_GATE), lambda i, j: (l, 0, 0))],
        out_specs=[pl.BlockSpec((ZP_TM, ZP_TN), lambda i, j: (i, j)),
                   pl.BlockSpec((ZP_TM, N_GATE), lambda i, j: (i, 0))],
        out_shape=[jax.ShapeDtypeStruct((st.rows, D_Z), F32),
                   jax.ShapeDtypeStruct((st.rows, N_GATE), F32)],
        scratch_shapes=[pltpu.VMEM((ZP_TM, D_MODEL), BF16)],
        compiler_params=_cparams("parallel", "arbitrary"),
        name="zproj",
    )(x, mod4, w_in, w_gate_cols)


OP_TM = 512


def _layer_norm(v, g, b):
    mu = jnp.mean(v, axis=-1, keepdims=True)
    d = v - mu
    var = jnp.mean(d * d, axis=-1, keepdims=True)
    return d * lax.rsqrt(var + LN_EPS) * g + b


def _route(scores, b_router):
    sel = scores + b_router
    s = [sel[e:e + 1, :] for e in range(N_EXPERTS)]
    in_top2 = []
    for g in range(N_EXPERT_GROUPS):
        for i in range(EXPERTS_PER_GROUP):
            e = g * EXPERTS_PER_GROUP + i
            cnt = jnp.zeros_like(s[e])
            for jj in range(EXPERTS_PER_GROUP):
                if jj == i:
                    continue
                o = g * EXPERTS_PER_GROUP + jj
                beats = (s[o] > s[e]) if jj > i else (s[o] >= s[e])
                cnt = cnt + jnp.where(beats, 1.0, 0.0)
            in_top2.append(cnt < 2.0)
    grp = []
    for g in range(N_EXPERT_GROUPS):
        tot = jnp.zeros_like(s[0])
        for i in range(EXPERTS_PER_GROUP):
            e = g * EXPERTS_PER_GROUP + i
            tot = tot + jnp.where(in_top2[e], s[e], 0.0)
        grp.append(tot)
    keep_rows = []
    for g in range(N_EXPERT_GROUPS):
        cnt = jnp.zeros_like(s[0])
        for o in range(N_EXPERT_GROUPS):
            if o == g:
                continue
            beats = (grp[o] > grp[g]) if o > g else (grp[o] >= grp[g])
            cnt = cnt + jnp.where(beats, 1.0, 0.0)
        best = cnt < 1.0
        for i in range(EXPERTS_PER_GROUP):
            e = g * EXPERTS_PER_GROUP + i
            keep_rows.append(jnp.where(best, jnp.where(in_top2[e], 1.0, 0.0), 0.0))
    keep = jnp.concatenate(keep_rows, axis=0)
    picked = scores * keep
    return picked / jnp.sum(picked, axis=0, keepdims=True), keep


def _oproj_kernel(y0_ref, y1_ref, y2_ref, y3_ref, w_ref, x_ref, mod_ref, g_ref, b_ref, wr_ref, br_ref,
                  x1_ref, h2_ref, gate_ref, keep_ref):
    acc = None
    for k, yr in enumerate((y0_ref, y1_ref, y2_ref, y3_ref)):
        part = jnp.dot(yr[...], w_ref[0, k * D_GROUP:(k + 1) * D_GROUP, :], preferred_element_type=F32)
        acc = part if acc is None else acc + part
    m = mod_ref[0, 0]
    gate1 = m[:, 2 * D_MODEL:3 * D_MODEL]
    shift2 = m[:, 3 * D_MODEL:4 * D_MODEL]
    scale2 = m[:, 4 * D_MODEL:5 * D_MODEL]
    x1 = _layer_norm(DEEPNORM_ALPHA * x_ref[...] + gate1 * acc, g_ref[0], b_ref[0])
    x1_ref[...] = x1
    h2 = x1 * (1.0 + scale2) + shift2
    h2_ref[...] = h2.astype(BF16)
    scores_t = _sigmoid(_dot3(wr_ref[...], h2, NT))
    gates_t, keep_t = _route(scores_t, br_ref[...])
    gate_ref[...] = gates_t.T
    keep_ref[...] = keep_t.T


def _oproj(st, ys, x, mod4, w_out_b, ln_g, ln_b, w_router, b_router, l):
    row = lambda i: (i, 0)
    return pl.pallas_call(
        _oproj_kernel,
        grid=(st.rows // OP_TM,),
        in_specs=[pl.BlockSpec((OP_TM, D_GROUP), row)] * 4 + [
            pl.BlockSpec((1, D_MODEL, D_MODEL), lambda i: (l, 0, 0)),
            pl.BlockSpec((OP_TM, D_MODEL), row),
            pl.BlockSpec((1, 1, 1, 6 * D_MODEL), lambda i: (l, st.mod_row(i, OP_TM), 0, 0)),
            pl.BlockSpec((1, 1, D_MODEL), lambda i: (l, 0, 0)),
            pl.BlockSpec((1, 1, D_MODEL), lambda i: (l, 0, 0)),
            pl.BlockSpec((N_EXPERTS, D_MODEL), lambda i: (0, 0)),
            pl.BlockSpec((N_EXPERTS, 1), lambda i: (0, 0))],
        out_specs=[pl.BlockSpec((OP_TM, D_MODEL), row),
                   pl.BlockSpec((OP_TM, D_MODEL), row),
                   pl.BlockSpec((OP_TM, N_EXPERTS), row),
                   pl.BlockSpec((OP_TM, N_EXPERTS), row)],
        out_shape=[jax.ShapeDtypeStruct((st.rows, D_MODEL), F32),
                   jax.ShapeDtypeStruct((st.rows, D_MODEL), BF16),
                   jax.ShapeDtypeStruct((st.rows, N_EXPERTS), F32),
                   jax.ShapeDtypeStruct((st.rows, N_EXPERTS), F32)],
        compiler_params=_cparams("parallel"),
        name="oproj_ln1_router",
    )(*ys, w_out_b, x, mod4, ln_g.reshape(DEPTH, 1, D_MODEL), ln_b.reshape(DEPTH, 1, D_MODEL),
      w_router.T, b_router.reshape(N_EXPERTS, 1))


LN_TM = 512


GS_BLK = 1024
GS_CAP = 320
GS_ALIGN = 2 * SUB
GS_COLS = 512
GS_STEP = 256
GS_VMEM_LIMIT = 60 * 1024 * 1024
GS_RCH = 3 * LANE
GS_ROWS = pl.cdiv(GS_BLK + N_EXPERT_GROUPS * GS_ALIGN, GS_RCH) * GS_RCH


def _moe_sorted_kernel(h_ref, gate_ref, keep_ref, wg_ref, wu_ref, wd_ref, f_ref, xs_scr, gs_scr, pt_scr, acc_scr,
                       seg_smem):
    e = pl.program_id(1)

    @pl.when(e == 0)
    def _():
        keep = keep_ref[...]
        lane = _iota(keep.shape, 1)
        member = []
        for g in range(N_EXPERT_GROUPS):
            in_g = (lane >= g * EXPERTS_PER_GROUP) & (lane < (g + 1) * EXPERTS_PER_GROUP)
            member.append(jnp.minimum(jnp.sum(jnp.where(in_g, keep, 0.0), axis=-1, keepdims=True), 1.0))
        lane4 = _iota((GS_BLK, LANE), 1)
        onehot = jnp.zeros((GS_BLK, LANE), F32)
        for g in range(N_EXPERT_GROUPS):
            onehot = onehot + jnp.where(lane4 == g, member[g], 0.0)
        counts = jnp.sum(onehot, axis=0, keepdims=True)
        starts = []
        start = jnp.zeros((1, 1), F32)
        for g in range(N_EXPERT_GROUPS):
            cnt_g = counts[:, g:g + 1]
            starts.append(start)
            seg_smem[2 * g] = jnp.sum(start).astype(jnp.int32)
            seg_smem[2 * g + 1] = jnp.sum(cnt_g).astype(jnp.int32)
            start = start + jnp.ceil(cnt_g * (1.0 / GS_ALIGN)) * GS_ALIGN
        onehot_b = onehot.astype(BF16)
        for r0 in range(0, GS_BLK, GS_STEP):
            tri = jnp.where(_iota((GS_STEP, GS_BLK), 1) <= _iota((GS_STEP, GS_BLK), 0) + r0, 1.0, 0.0).astype(BF16)
            csum = jnp.dot(tri, onehot_b, preferred_element_type=F32)
            dest = jnp.zeros((GS_STEP, 1), F32)
            for g in range(N_EXPERT_GROUPS):
                dest = dest + member[g][r0:r0 + GS_STEP] * (starts[g] + csum[:, g:g + 1] - 1.0)
            pt_scr[r0:r0 + GS_STEP, :] = jnp.where(
                _iota((GS_STEP, GS_ROWS), 1) == dest.astype(jnp.int32), 1.0, 0.0).astype(BF16)
        for c0 in range(0, GS_ROWS, GS_RCH):
            pt_c = pt_scr[:, c0:c0 + GS_RCH]
            xs_scr[c0:c0 + GS_RCH, :] = _dg(pt_c, h_ref[...], TN).astype(BF16)
            gs_scr[c0:c0 + GS_RCH, :] = _dot_sel_l(pt_c, gate_ref[...], TN)
        acc_scr[...] = jnp.zeros_like(acc_scr)

    g = e // EXPERTS_PER_GROUP
    seg_start = seg_smem[2 * g]
    seg_len = seg_smem[2 * g + 1]
    wg = wg_ref[0, 0]
    wu = wu_ref[0, 0]
    wd = wd_ref[0, 0]

    def window(c, carry):
        first = seg_start + c * GS_CAP
        start = jnp.minimum(first, GS_ROWS - GS_CAP)
        rows = pl.ds(pl.multiple_of(start, GS_ALIGN), GS_CAP)
        x = xs_scr[rows, :]
        a = jnp.dot(x, wg, preferred_element_type=F32)
        u = jnp.dot(x, wu, preferred_element_type=F32)
        gates = gs_scr[rows, :]
        ge = jnp.sum(jnp.where(_iota(gates.shape, 1) == e, gates, 0.0), axis=-1, keepdims=True)
        ge = jnp.where(_iota((GS_CAP, 1), 0) + start >= first, ge, 0.0)
        hid = (_silu(a) * u * ge).astype(BF16)
        acc_scr[rows, :] += jnp.dot(hid, wd, preferred_element_type=F32)
        return carry

    lax.fori_loop(0, (seg_len + GS_CAP - 1) // GS_CAP, window, 0)

    @pl.when(e == N_EXPERTS - 1)
    def _():
        pt = pt_scr[...]
        for c0 in range(0, D_MODEL, GS_COLS):
            acc_b = acc_scr[:, c0:c0 + GS_COLS].astype(BF16)
            f_ref[:, c0:c0 + GS_COLS] = jnp.dot(pt, acc_b, preferred_element_type=F32).astype(BF16)


def _moe_sorted(st, h2, gates, keep, wg_b, wu_b, wd_b, l):
    row = lambda i, e: (i, 0)
    wspec = lambda a, b: pl.BlockSpec((1, 1, a, b), lambda i, e: (l, e, 0, 0))
    return pl.pallas_call(
        _moe_sorted_kernel,
        grid=(st.rows // GS_BLK, N_EXPERTS),
        in_specs=[pl.BlockSpec((GS_BLK, D_MODEL), row, pipeline_mode=pl.Buffered(1)),
                  pl.BlockSpec((GS_BLK, N_EXPERTS), row), pl.BlockSpec((GS_BLK, N_EXPERTS), row),
                  wspec(D_MODEL, D_EXPERT), wspec(D_MODEL, D_EXPERT), wspec(D_EXPERT, D_MODEL)],
        out_specs=pl.BlockSpec((GS_BLK, D_MODEL), row),
        out_shape=jax.ShapeDtypeStruct((st.rows, D_MODEL), BF16),
        scratch_shapes=[pltpu.VMEM((GS_ROWS, D_MODEL), BF16), pltpu.VMEM((GS_ROWS, N_EXPERTS), F32),
                        pltpu.VMEM((GS_BLK, GS_ROWS), BF16), pltpu.VMEM((GS_ROWS, D_MODEL), F32),
                        pltpu.SMEM((2 * N_EXPERT_GROUPS,), jnp.int32)],
        compiler_params=pltpu.CompilerParams(dimension_semantics=("parallel", "arbitrary"),
                                             vmem_limit_bytes=GS_VMEM_LIMIT),
        name="moe_group_sorted",
    )(h2, gates, keep, wg_b, wu_b, wd_b)


def _ln2f_kernel(x_ref, f_ref, mod_ref, g_ref, b_ref, o_ref):
    gate2 = mod_ref[0, 0][:, 5 * D_MODEL:6 * D_MODEL]
    o_ref[...] = _layer_norm(DEEPNORM_ALPHA * x_ref[...] + gate2 * f_ref[...].astype(F32), g_ref[0], b_ref[0])


def _ln2f(st, x1, f, mod4, ln_g, ln_b, l):
    row = lambda i: (i, 0)
    return pl.pallas_call(
        _ln2f_kernel,
        grid=(st.rows // LN_TM,),
        in_specs=[pl.BlockSpec((LN_TM, D_MODEL), row),
                  pl.BlockSpec((LN_TM, D_MODEL), row),
                  pl.BlockSpec((1, 1, 1, 6 * D_MODEL), lambda i: (l, st.mod_row(i, LN_TM), 0, 0)),
                  pl.BlockSpec((1, 1, D_MODEL), lambda i: (l, 0, 0)),
                  pl.BlockSpec((1, 1, D_MODEL), lambda i: (l, 0, 0))],
        out_specs=pl.BlockSpec((LN_TM, D_MODEL), row),
        out_shape=jax.ShapeDtypeStruct((st.rows, D_MODEL), F32),
        compiler_params=_cparams("parallel"),
        name="ln2",
    )(x1, f, mod4, ln_g.reshape(DEPTH, 1, D_MODEL), ln_b.reshape(DEPTH, 1, D_MODEL))


S5_TC = 128
S5_SEQS = SUB
S5_Q = 4
S5_QS = S5_GROUPS // S5_Q * S5_STATE
N_S5 = S5_GROUPS * S5_STATE


def _s5_prep_kernel(lr_ref, li_ref, ls_ref, br_ref, bi_ref, e_ref, lbr_ref, lbi_ref, bbr_ref, bbi_ref):
    lr = lr_ref[0]
    li = li_ref[0]
    dt = jnp.exp(ls_ref[0])
    mag = jnp.exp(lr * dt)
    ang = li * dt
    ar = mag * jnp.cos(ang)
    ai = mag * jnp.sin(ang)
    lbr_ref[0] = ar
    lbi_ref[0] = ai
    den = lr * lr + li * li
    nr = ar - 1.0
    cr = (nr * lr + ai * li) / den
    ci = (ai * lr - nr * li) / den
    cr = _dot_sel_l(e_ref[...], cr)
    ci = _dot_sel_l(e_ref[...], ci)
    bre = br_ref[0]
    bim = bi_ref[0]
    bbr_ref[0] = cr * bre - ci * bim
    bbi_ref[0] = cr * bim + ci * bre


def _s5_prep(lam_re, lam_im, log_step, b_re, b_im):
    d2 = DEPTH * 2
    g, p, h = S5_GROUPS, S5_STATE, S5_CH
    bt_re = jnp.swapaxes(b_re, -1, -2).reshape(d2, g * h, p)
    bt_im = jnp.swapaxes(b_im, -1, -2).reshape(d2, g * h, p)
    expand = jnp.asarray(np.kron(np.eye(g, dtype=np.float32), np.ones((h, 1), np.float32)))
    spec_gp = pl.BlockSpec((1, g, p), lambda i: (i, 0, 0))
    spec_b = pl.BlockSpec((1, g * h, p), lambda i: (i, 0, 0))
    lbr, lbi, bbr, bbi = pl.pallas_call(
        _s5_prep_kernel,
        grid=(d2,),
        in_specs=[spec_gp, spec_gp, pl.BlockSpec((1, g, 1), lambda i: (i, 0, 0)), spec_b, spec_b,
                  pl.BlockSpec((g * h, g), lambda i: (0, 0))],
        out_specs=[spec_gp, spec_gp, spec_b, spec_b],
        out_shape=[jax.ShapeDtypeStruct((d2, g, p), F32)] * 2 + [jax.ShapeDtypeStruct((d2, g * h, p), F32)] * 2,
        compiler_params=_cparams("parallel"),
        name="s5_prep",
    )(lam_re.reshape(d2, g, p), lam_im.reshape(d2, g, p), log_step.reshape(d2, g, 1), bt_re, bt_im, expand)
    return lbr, lbi, bbr, bbi


def _s5_block_params(lbr, lbi, bbr, bbi, c_re, c_im):
    d2 = DEPTH * 2
    gq = S5_GROUPS // S5_Q
    eye = jnp.eye(gq, dtype=F32)

    def b_blocks(b):
        b = b.reshape(d2, S5_Q, gq, S5_CH, S5_STATE)
        return jnp.einsum('djghp,gk->djghkp', b, eye).reshape(d2, S5_Q, gq * S5_CH, gq * S5_STATE)

    def c_blocks(c):
        c = c.reshape(d2, S5_Q, gq, S5_CH, S5_STATE)
        return jnp.einsum('djghp,gk->djgpkh', c, eye).reshape(d2, S5_Q, gq * S5_STATE, gq * S5_CH)

    bq = jnp.concatenate([b_blocks(bbr), b_blocks(bbi)], axis=-1)
    cq = jnp.concatenate([c_blocks(c_re), -c_blocks(c_im)], axis=-2)
    return bq, cq, lbr.reshape(d2, 1, N_S5), lbi.reshape(d2, 1, N_S5)


def _s5_scan_kernel(u_ref, bq_ref, cq_ref, lr_ref, li_ref, s0r_ref, s0i_ref, y_ref, sfr_ref, sfi_ref,
                    utb, bur, bui, ytb, sr_scr, si_scr):
    d = pl.program_id(1)
    c = pl.program_id(2)
    nrow = S5_TC * S5_SEQS
    cw = D_GROUP // S5_Q

    @pl.when(c == 0)
    def _():
        sr_scr[...] = s0r_ref[0, 0]
        si_scr[...] = s0i_ref[0, 0]

    for b in range(S5_SEQS):
        for j in range(S5_Q):
            utb[j, pl.ds(b, S5_TC, stride=S5_SEQS), :] = u_ref[b, :, j * cw:(j + 1) * cw]
    for j in range(S5_Q):
        bu = _dot1(utb[j], bq_ref[0, j])
        bur[:, j * S5_QS:(j + 1) * S5_QS] = bu[:, :S5_QS]
        bui[:, j * S5_QS:(j + 1) * S5_QS] = bu[:, S5_QS:]
    for j in range(S5_Q):
        sl = slice(j * S5_QS, (j + 1) * S5_QS)
        lam_r = jnp.broadcast_to(lr_ref[0][:, sl], (S5_SEQS, S5_QS))
        lam_i = jnp.broadcast_to(li_ref[0][:, sl], (S5_SEQS, S5_QS))

        def step(t, carry, sl=sl, lam_r=lam_r, lam_i=lam_i):
            sr, si = carry
            te = jnp.where(d == 0, t, S5_TC - 1 - t)
            rows = pl.ds(pl.multiple_of(te * S5_SEQS, S5_SEQS), S5_SEQS)
            nr = lam_r * sr - lam_i * si + bur[rows, sl]
            ni = lam_r * si + lam_i * sr + bui[rows, sl]
            bur[rows, sl] = nr
            bui[rows, sl] = ni
            return nr, ni

        sr, si = lax.fori_loop(0, S5_TC, step, (sr_scr[:, sl], si_scr[:, sl]))
        sr_scr[:, sl] = sr
        si_scr[:, sl] = si
    for j in range(S5_Q):
        sl = slice(j * S5_QS, (j + 1) * S5_QS)
        yj = _dot1(bur[:, sl], cq_ref[0, j, :S5_QS, :]) + _dot1(bui[:, sl], cq_ref[0, j, S5_QS:, :])
        ytb[j] = yj
    for b in range(S5_SEQS):
        for j in range(S5_Q):
            y_ref[0, b, :, j * cw:(j + 1) * cw] = ytb[j, pl.ds(b, S5_TC, stride=S5_SEQS), :]

    @pl.when(c == pl.num_programs(2) - 1)
    def _():
        sfr_ref[0, 0] = sr_scr[...]
        sfi_ref[0, 0] = si_scr[...]


def _s5_scan(st, z3, bq, cq, lamr, lami, s0r, s0i, l):
    ng, nc = st.nb // S5_SEQS, st.L // S5_TC
    nrow = S5_TC * S5_SEQS
    chunk = lambda d, c: c + d * (nc - 1 - 2 * c)
    par = lambda g, d, c: (2 * l + d, 0, 0, 0)
    st_spec = pl.BlockSpec((1, 1, S5_SEQS, N_S5), lambda g, d, c: (g, d, 0, 0))
    return pl.pallas_call(
        _s5_scan_kernel,
        grid=(ng, 2, nc),
        in_specs=[pl.BlockSpec((S5_SEQS, S5_TC, D_GROUP), lambda g, d, c: (g, chunk(d, c), 0)),
                  pl.BlockSpec((1, S5_Q, D_GROUP // S5_Q, 2 * S5_QS), par),
                  pl.BlockSpec((1, S5_Q, 2 * S5_QS, D_GROUP // S5_Q), par),
                  pl.BlockSpec((1, 1, N_S5), lambda g, d, c: (2 * l + d, 0, 0)),
                  pl.BlockSpec((1, 1, N_S5), lambda g, d, c: (2 * l + d, 0, 0)),
                  st_spec, st_spec],
        out_specs=[pl.BlockSpec((1, S5_SEQS, S5_TC, D_GROUP), lambda g, d, c: (d, g, chunk(d, c), 0)),
                   st_spec, st_spec],
        out_shape=[jax.ShapeDtypeStruct((2, st.nb, st.L, D_GROUP), F32),
                   jax.ShapeDtypeStruct((ng, 2, S5_SEQS, N_S5), F32),
                   jax.ShapeDtypeStruct((ng, 2, S5_SEQS, N_S5), F32)],
        scratch_shapes=[pltpu.VMEM((S5_Q, nrow, LANE), F32), pltpu.VMEM((nrow, N_S5), F32),
                        pltpu.VMEM((nrow, N_S5), F32), pltpu.VMEM((S5_Q, nrow, LANE), F32),
                        pltpu.VMEM((S5_SEQS, N_S5), F32), pltpu.VMEM((S5_SEQS, N_S5), F32)],
        compiler_params=_cparams("parallel", "arbitrary", "arbitrary"),
        name="s5_scan",
    )(z3, bq, cq, lamr, lami, s0r, s0i)


S5_TM = 512


def _gelu_tanh(x):
    return 0.5 * x * (1.0 + jnp.tanh(math.sqrt(2.0 / math.pi) * (x + 0.044715 * (x * x * x))))


def _s5_out_kernel(u_ref, yf_ref, yb_ref, d_ref, w_ref, o_ref):
    y = yf_ref[0] + yb_ref[0] + d_ref[0] * u_ref[...]
    y = _gelu_tanh(y)
    o_ref[...] = (y * _sigmoid(_dot1(y, w_ref[0]))).astype(BF16)


def _s5_out(st, z, ydir, d_skip, w_glu, l):
    return pl.pallas_call(
        _s5_out_kernel,
        grid=(st.rows // S5_TM,),
        in_specs=[pl.BlockSpec((S5_TM, D_GROUP), lambda i: (i, 0)),
                  pl.BlockSpec((1, S5_TM, D_GROUP), lambda i: (0, i, 0)),
                  pl.BlockSpec((1, S5_TM, D_GROUP), lambda i: (1, i, 0)),
                  pl.BlockSpec((1, 1, D_GROUP), lambda i: (l, 0, 0)),
                  pl.BlockSpec((1, D_GROUP, D_GROUP), lambda i: (l, 0, 0))],
        out_specs=pl.BlockSpec((S5_TM, D_GROUP), lambda i: (i, 0)),
        out_shape=jax.ShapeDtypeStruct((st.rows, D_GROUP), BF16),
        compiler_params=_cparams("parallel"),
        name="s5_out",
    )(z, ydir, ydir, d_skip.reshape(DEPTH, 1, D_GROUP), w_glu)


NA_SCALE = NA_HEAD ** -0.5
NA_ROWS = DEC_SEQ // GRID_W
NA_KH = min(NA_WIN_H, NA_ROWS)
NA_QCOL = NA_OFF // LANE
NA_KCOL = (NA_OFF + D_GROUP) // LANE
NA_VCOL = (NA_OFF + 2 * D_GROUP) // LANE
NA_NDR = 2 * NA_WIN_H - 1
NA_AHEAD = 1


def _na_ctx_kernel(q_ref, k_ref, v_ref, o_ref, nk_ref, nv_ref):
    scores = []
    for hh in range(2):
        sl = slice(hh * NA_HEAD, (hh + 1) * NA_HEAD)
        k = k_ref[0, :, sl]
        nk_ref[0, hh] = k
        scores.append(_dot1(q_ref[0, :, sl], k, NT) * NA_SCALE)
    for hh in range(2):
        sl = slice(hh * NA_HEAD, (hh + 1) * NA_HEAD)
        v = v_ref[0, :, sl]
        nv_ref[0, hh] = v
        s = scores[hh]
        e = jnp.exp(s - jnp.max(s, axis=-1, keepdims=True))
        o = _dot1(e, v) / jnp.sum(e, axis=-1, keepdims=True)
        o_ref[0, :, sl] = o.astype(BF16)


def _na_ctx(st, z3):
    blk = lambda col: pl.BlockSpec((1, st.L, LANE), lambda b, p: (b, 0, col + p))
    kv_spec = pl.BlockSpec((1, 2, st.L, NA_HEAD), lambda b, p: (b, p, 0, 0))
    kv_shape = jax.ShapeDtypeStruct((st.nb, NA_HEADS, st.L, NA_HEAD), F32)
    return pl.pallas_call(
        _na_ctx_kernel,
        grid=(st.nb, NA_HEADS // 2),
        in_specs=[blk(NA_QCOL), blk(NA_KCOL), blk(NA_VCOL)],
        out_specs=[pl.BlockSpec((1, st.L, LANE), lambda b, p: (b, 0, p)), kv_spec, kv_spec],
        out_shape=[jax.ShapeDtypeStruct((st.nb, st.L, D_GROUP), BF16), kv_shape, kv_shape],
        compiler_params=_cparams("parallel", "parallel"),
        name="na_context",
    )(z3, z3, z3)


def _na_nbr_kernel(q_ref, k_ref, v_ref, ck_ref, cv_ref, bias_ref, o_ref):
    nloc = NA_KH * GRID_W
    qc = _iota((GRID_W, nloc), 0)
    kc = _iota((GRID_W, nloc), 1) % GRID_W
    cs = jnp.clip(qc - NA_WIN_W // 2, 0, GRID_W - NA_WIN_W)
    col_in = (kc >= cs) & (kc < cs + NA_WIN_W)

    def scores(hh, r):
        sl = slice(hh * NA_HEAD, (hh + 1) * NA_HEAD)
        rs = min(max(r - NA_KH // 2, 0), NA_ROWS - NA_KH)
        q = q_ref[0, r * GRID_W:(r + 1) * GRID_W, sl]
        k = k_ref[0, rs * GRID_W:rs * GRID_W + nloc, sl]
        off = (rs - r + NA_WIN_H - 1) * GRID_W
        s_loc = _dot1(q, k, NT) * NA_SCALE + bias_ref[hh, :, off:off + nloc]
        s_loc = jnp.where(col_in, s_loc, NEG_INF)
        s_ctx = _dot1(q, ck_ref[0, 0, hh], NT) * NA_SCALE
        return s_loc, s_ctx

    def finish(hh, r, s_loc, s_ctx):
        sl = slice(hh * NA_HEAD, (hh + 1) * NA_HEAD)
        rs = min(max(r - NA_KH // 2, 0), NA_ROWS - NA_KH)
        v = v_ref[0, rs * GRID_W:rs * GRID_W + nloc, sl]
        m = jnp.maximum(jnp.max(s_loc, axis=-1, keepdims=True), jnp.max(s_ctx, axis=-1, keepdims=True))
        e_loc = jnp.exp(s_loc - m)
        e_ctx = jnp.exp(s_ctx - m)
        den = jnp.sum(e_loc, axis=-1, keepdims=True) + jnp.sum(e_ctx, axis=-1, keepdims=True)
        o = (_dot1(e_loc, v) + _dot1(e_ctx, cv_ref[0, 0, hh])) / den
        o_ref[0, r * GRID_W:(r + 1) * GRID_W, sl] = o.astype(BF16)

    blocks = [(hh, r) for hh in range(2) for r in range(NA_ROWS)]
    pending = []
    for i, blk in enumerate(blocks):
        pending.append(scores(*blk))
        if i >= NA_AHEAD:
            finish(*blocks[i - NA_AHEAD], *pending.pop(0))
    for j in range(len(blocks) - NA_AHEAD, len(blocks)):
        finish(*blocks[j], *pending.pop(0))


def _na_bias_table(rpb):
    qc = np.arange(GRID_W)[:, None]
    kc = np.arange(GRID_W)[None, :]
    dc = np.clip(kc - qc, -(NA_WIN_W - 1), NA_WIN_W - 1) + (NA_WIN_W - 1)
    t = rpb[:, :, :, dc]
    return jnp.transpose(t, (0, 1, 3, 2, 4)).reshape(DEPTH, NA_HEADS, GRID_W, NA_NDR * GRID_W)


def _na_nbr(st, z3, cache_k, cache_v, bias_tab, l):
    blk = lambda col: pl.BlockSpec((1, st.L, LANE), lambda b, p: (b, 0, col + p))
    cspec = pl.BlockSpec((1, 1, 2, PAST_LEN, NA_HEAD), lambda b, p: (b, l, p, 0, 0))
    return pl.pallas_call(
        _na_nbr_kernel,
        grid=(st.nb, NA_HEADS // 2),
        in_specs=[blk(NA_QCOL), blk(NA_KCOL), blk(NA_VCOL), cspec, cspec,
                  pl.BlockSpec((None, 2, GRID_W, NA_NDR * GRID_W), lambda b, p: (l, p, 0, 0))],
        out_specs=pl.BlockSpec((1, st.L, LANE), lambda b, p: (b, 0, p)),
        out_shape=jax.ShapeDtypeStruct((st.nb, st.L, D_GROUP), BF16),
        compiler_params=_cparams("parallel", "parallel"),
        name="na_neighbourhood",
    )(z3, z3, z3, cache_k, cache_v, bias_tab)


ML_COL = ML_OFF // LANE
ML_SCALE = ML_HEAD ** -0.5


def _log_sigmoid(x):
    return -_softplus(-x)


def _rope_tables(L):
    half = ML_HEAD // 2
    quarter = half // 2
    t = np.arange(L)
    inv_freq = ROPE_BASE ** (-np.arange(quarter, dtype=np.float32) / quarter)

    def tabs(pos):
        ang = pos.astype(np.float32)[:, None] * inv_freq[None, :].astype(np.float32)
        c, s = np.cos(ang), np.sin(ang)
        return np.concatenate([c, c], axis=-1), np.concatenate([-s, s], axis=-1)

    c1, s1 = tabs(t // GRID_W)
    c2, s2 = tabs(t % GRID_W)
    return (jnp.asarray(np.concatenate([c1, c2], axis=-1), F32),
            jnp.asarray(np.concatenate([s1, s2], axis=-1), F32))


def _ml_kernel(*refs, L, rotary, layer):
    nh = ML_HEADS
    ib_ref, fb_ref = refs[0], refs[1]
    q_refs, k_refs, v_refs, o_refs = (refs[2 + i * nh:2 + (i + 1) * nh] for i in range(4))
    (g_ref, cos_ref, sin_ref, c0_ref, n0_ref, m0_ref, lng_ref, lnb_ref,
     y_ref, cf_ref, nf_ref, mf_ref, qs, ks, hf, hb, c_scr, n_scr, m_scr) = refs[2 + 4 * nh:]
    T = ML_CHUNK
    nc = L // T

    if rotary:
        first = (_iota((L, ML_HEAD), 1) % (ML_HEAD // 2)) < ML_HEAD // 4

        def rope(x):
            quarter = ML_HEAD // 4
            partner = jnp.where(first, pltpu.roll(x, ML_HEAD - quarter, axis=1), pltpu.roll(x, quarter, axis=1))
            return x * cos_ref[...] + partner * sin_ref[...]
    else:
        rope = lambda x: x
    for h in range(nh):
        qs[h] = rope(q_refs[h][0]) * ML_SCALE
        ks[h] = rope(k_refs[h][0])
        for d in range(2):
            c_scr[d, h] = c0_ref[0, d, h]
            n_scr[d, h] = n0_ref[0, h, d:d + 1, :]
            m_scr[d, h] = m0_ref[0, h, d:d + 1, :]

    tj = _iota((T, T), 0)
    ts = _iota((T, T), 1)
    ones = jnp.ones((T, T), F32)
    upto_row = (tj <= ts, tj >= ts)
    upto_col = (ts <= tj, ts >= tj)

    def body(ci, carry):
        ch = []
        for d in range(2):
            cd = ci if d == 0 else nc - 1 - ci
            rows = pl.ds(pl.multiple_of(cd * T, T), T)
            for h in range(nh):
                g = g_ref[0, h, cd]
                ig = g[:, d:d + 1] + ib_ref[layer * 2 * nh + d * nh + h]
                fg = g[:, 2 + d:3 + d] + fb_ref[layer * 2 * nh + d * nh + h]
                lf = jnp.broadcast_to(_log_sigmoid(fg), (T, T))
                igb = jnp.broadcast_to(ig, (T, T))
                q = dict(d=d, h=h, rows=rows, ig=ig, qc=qs[h, rows, :], kc=ks[h, rows, :], vc=v_refs[h][0, rows, :])
                q['bcol'] = _dot_sel_l(jnp.where(upto_col[d], 1.0, 0.0), lf)
                q['brow_i'] = _dot_sel_l(ones, jnp.where(upto_row[d], lf, 0.0) - jnp.where(tj == ts, igb, 0.0))
                ch.append(q)
        for q in ch:
            q['qk'] = _dot1(q['qc'], q['kc'], NT)
            q['c_old'] = c_scr[q['d'], q['h']]
            q['qc_c'] = _dot1(q['qc'], q['c_old'])
        for q in ch:
            d, h = q['d'], q['h']
            dmat = jnp.where(upto_col[d], q['bcol'] - q['brow_i'], -jnp.inf)
            b1 = q['bcol'][:, 0:1]
            m_prev = m_scr[d, h]
            inter = b1 + m_prev
            m_t = jnp.maximum(inter, jnp.max(dmat, axis=-1, keepdims=True))
            qk = q['qk'] * jnp.exp(dmat - m_t)
            w_inter = jnp.exp(inter - m_t)
            n_old = n_scr[d, h]
            den = w_inter * jnp.sum(q['qc'] * n_old, axis=-1, keepdims=True) + jnp.sum(qk, axis=-1, keepdims=True)
            q['scale'] = 1.0 / jnp.maximum(jnp.abs(den), jnp.exp(-m_t))
            q['inter_part'] = w_inter * q['qc_c']
            q['intra'] = _dot1(qk, q['vc'])
            b_last = b1[T - 1:T, :] if d == 0 else b1[0:1, :]
            g_s = b_last - b1 + q['ig']
            m_new = jnp.maximum(b_last + m_prev, jnp.max(g_s, axis=0, keepdims=True))
            w_old = jnp.exp(b_last + m_prev - m_new)
            w_s = jnp.exp(g_s - m_new)
            q['c_new'] = _dot1(q['kc'], w_s * q['vc'], TN)
            q['w_old'] = w_old
            n_scr[d, h] = w_old * n_old + jnp.sum(w_s * q['kc'], axis=0, keepdims=True)
            m_scr[d, h] = m_new
        for q in ch:
            d, h = q['d'], q['h']
            hcur = (q['inter_part'] + q['intra']) * q['scale']
            if d == 0:
                hf[h, q['rows'], :] = hcur
            else:
                hb[h, q['rows'], :] = hcur
            c_scr[d, h] = q['w_old'] * q['c_old'] + q['c_new']
        return carry

    lax.fori_loop(0, nc, body, 0)

    for h in range(nh):
        hsum = hf[h] + hb[h]
        mu = jnp.mean(hsum, axis=-1, keepdims=True)
        dv = hsum - mu
        var = jnp.mean(dv * dv, axis=-1, keepdims=True)
        hn = dv * lax.rsqrt(var + ML_GN_EPS)
        y_ref[0, :, h * ML_HEAD:(h + 1) * ML_HEAD] = (
            _sigmoid(o_refs[h][0]) * (hn * lng_ref[0, :, h * ML_HEAD:(h + 1) * ML_HEAD]
                                      + lnb_ref[0, :, h * ML_HEAD:(h + 1) * ML_HEAD])).astype(BF16)
        for d in range(2):
            cf_ref[0, d, h] = c_scr[d, h]
            nf_ref[0, h, d:d + 1, :] = n_scr[d, h]
            mf_ref[0, h, d:d + 1, :] = m_scr[d, h]


def _ml_mixer(st, z3, zg, i_bias, f_bias, ln_g, ln_b, c0, n0, m0, l):
    L, nb, nh = st.L, st.nb, ML_HEADS
    nc = L // ML_CHUNK
    gates = zg.reshape(nb, L, 2, 2, nh).transpose(0, 4, 1, 2, 3).reshape(nb, nh, nc, ML_CHUNK, 4)
    cos_t, sin_t = _rope_tables(L)
    blk = lambda col: pl.BlockSpec((1, L, LANE), lambda b: (b, 0, col))
    head_blocks = [blk(ML_COL + part * nh + h) for part in range(4) for h in range(nh)]
    smem = pl.BlockSpec(memory_space=pltpu.SMEM)
    tab = pl.BlockSpec((L, ML_HEAD), lambda b: (0, 0))
    c_spec = pl.BlockSpec((1, 2, nh, ML_HEAD, ML_HEAD), lambda b: (b, 0, 0, 0, 0))
    n_spec = pl.BlockSpec((1, nh, 2, ML_HEAD), lambda b: (b, 0, 0, 0))
    m_spec = pl.BlockSpec((1, nh, 2, 1), lambda b: (b, 0, 0, 0))
    par = pl.BlockSpec((1, 1, D_GROUP), lambda b: (l, 0, 0))
    return pl.pallas_call(
        functools.partial(_ml_kernel, L=L, rotary=st.latent, layer=l),
        grid=(nb,),
        in_specs=[smem, smem] + head_blocks + [
            pl.BlockSpec((1, nh, nc, ML_CHUNK, 4), lambda b: (b, 0, 0, 0, 0)),
            tab, tab, c_spec, n_spec, m_spec, par, par],
        out_specs=[pl.BlockSpec((1, L, D_GROUP), lambda b: (b, 0, 0)), c_spec, n_spec, m_spec],
        out_shape=[jax.ShapeDtypeStruct((nb, L, D_GROUP), BF16),
                   jax.ShapeDtypeStruct((nb, 2, nh, ML_HEAD, ML_HEAD), F32),
                   jax.ShapeDtypeStruct((nb, nh, 2, ML_HEAD), F32),
                   jax.ShapeDtypeStruct((nb, nh, 2, 1), F32)],
        scratch_shapes=[pltpu.VMEM((nh, L, ML_HEAD), F32), pltpu.VMEM((nh, L, ML_HEAD), F32),
                        pltpu.VMEM((nh, L, ML_HEAD), F32), pltpu.VMEM((nh, L, ML_HEAD), F32),
                        pltpu.VMEM((2, nh, ML_HEAD, ML_HEAD), F32), pltpu.VMEM((2, nh, 1, ML_HEAD), F32),
                        pltpu.VMEM((2, nh, 1, 1), F32)],
        compiler_params=_cparams("parallel"),
        name="mlstm",
    )(i_bias.reshape(-1), f_bias.reshape(-1), *([z3] * (4 * nh)), gates, cos_t, sin_t, c0, n0, m0,
      ln_g.reshape(DEPTH, 1, D_GROUP), ln_b.reshape(DEPTH, 1, D_GROUP))


RW_TL = 256
RW_T = 64
RW_GROUP = 16
RW_RCOL = RW_OFF // D_GROUP
RW_LCOL = (RW_OFF + 3 * D_GROUP) // (2 * LANE)
N_LORA = RW_LORA_W + RW_LORA_A + RW_LORA_G


def _head_ones():
    return jnp.asarray(np.kron(np.eye(RW_HEADS, dtype=np.float32), np.ones((RW_HEAD, RW_HEAD), np.float32)))


def _rw_front_kernel(r_ref, k_ref, v_ref, lo_ref, rp_ref, kp_ref, vp_ref, lp_ref, rn_ref, kn_ref, vn_ref, ln_ref,
                     mup_ref, mun_ref, w0_ref, w2_ref, a0_ref, a2_ref, g2_ref, kk_ref, ka_ref, hones_ref,
                     ro_ref, vo_ref, kko_ref, go_ref, kd_ref, bd_ref, lw_ref):
    t = pl.program_id(1)
    first = t == 0
    last = t == pl.num_programs(1) - 1
    row = _iota((RW_TL, 1), 0)

    def shift(x_ref, p_ref, n_ref, lo, hi):
        x = x_ref[0]
        prev_edge = jnp.where(first, 0.0, p_ref[0, SUB - 1:SUB, :])
        next_edge = jnp.where(last, 0.0, n_ref[0, 0:1, :])
        prev = jnp.where(row == 0, prev_edge, pltpu.roll(x, 1, axis=0))
        nxt = jnp.where(row == RW_TL - 1, next_edge, pltpu.roll(x, RW_TL - 1, axis=0))
        return x + mup_ref[0][:, lo:hi] * (prev - x) + mun_ref[0][:, lo:hi] * (nxt - x)

    g = D_GROUP
    r = shift(r_ref, rp_ref, rn_ref, 0, g)
    k = shift(k_ref, kp_ref, kn_ref, g, 2 * g)
    v = shift(v_ref, vp_ref, vn_ref, 2 * g, 3 * g)
    lo = shift(lo_ref, lp_ref, ln_ref, 3 * g, 3 * g + N_LORA)
    zw = lo[:, :RW_LORA_W]
    za = lo[:, RW_LORA_W:RW_LORA_W + RW_LORA_A]
    zg = lo[:, RW_LORA_W + RW_LORA_A:]
    ro_ref[0] = r
    vo_ref[0] = v
    kk = k * kk_ref[0]
    ssq = _seg_sum(kk * kk, hones_ref[...])
    kk = kk * lax.rsqrt(ssq + 1e-12)
    kko_ref[0] = kk
    go_ref[0] = _dot3(_sigmoid(zg), g2_ref[0])
    tw = jnp.tanh(zw)
    for d in range(2):
        w_log = -_softplus(-(w0_ref[0, d:d + 1, :] + _dot3(tw, w2_ref[0, d]))) - 0.5
        lw_ref[d, 0] = -jnp.exp(w_log)
        a = _sigmoid(a0_ref[0, d:d + 1, :] + _dot3(za, a2_ref[0, d]))
        kd_ref[d, 0] = k * (1.0 + (a - 1.0) * ka_ref[0])
        bd_ref[d, 0] = kk * a


def _rw_front(st, z3, mu_prev, mu_next, w0, w2, a0, a2, g2, k_k, k_a, l):
    nb, L = st.nb, st.L
    nt = L // RW_TL
    tpb = RW_TL // SUB
    main = lambda w, col: pl.BlockSpec((1, RW_TL, w), lambda b, t: (b, t, col))
    prev = lambda w, col: pl.BlockSpec((1, SUB, w), lambda b, t: (b, jnp.maximum(t * tpb - 1, 0), col))
    nxt = lambda w, col: pl.BlockSpec((1, SUB, w), lambda b, t: (b, jnp.minimum((t + 1) * tpb, L // SUB - 1), col))
    cols = [(D_GROUP, RW_RCOL), (D_GROUP, RW_RCOL + 1), (D_GROUP, RW_RCOL + 2), (2 * LANE, RW_LCOL)]
    lay = lambda *shape: pl.BlockSpec((1,) + shape, lambda b, t: (l,) + (0,) * len(shape))
    out1 = pl.BlockSpec((1, RW_TL, D_GROUP), lambda b, t: (b, t, 0))
    out2 = pl.BlockSpec((2, 1, RW_TL, D_GROUP), lambda b, t: (0, b, t, 0))
    s1 = jax.ShapeDtypeStruct((nb, L, D_GROUP), F32)
    s2 = jax.ShapeDtypeStruct((2, nb, L, D_GROUP), F32)
    return pl.pallas_call(
        _rw_front_kernel,
        grid=(nb, nt),
        in_specs=[main(*c) for c in cols] + [prev(*c) for c in cols] + [nxt(*c) for c in cols] + [
            lay(1, RW_IN), lay(1, RW_IN), lay(2, D_GROUP), lay(2, RW_LORA_W, D_GROUP), lay(2, D_GROUP),
            lay(2, RW_LORA_A, D_GROUP), lay(RW_LORA_G, D_GROUP), lay(1, D_GROUP), lay(1, D_GROUP),
            pl.BlockSpec((D_GROUP, D_GROUP), lambda b, t: (0, 0))],
        out_specs=[out1, out1, out1, out1, out2, out2, out2],
        out_shape=[s1, s1, s1, s1, s2, s2, s2],
        compiler_params=_cparams("parallel", "parallel"),
        name="rwkv_front",
    )(*([z3] * 12), mu_prev.reshape(DEPTH, 1, RW_IN), mu_next.reshape(DEPTH, 1, RW_IN), w0, w2, a0, a2, g2,
      k_k.reshape(DEPTH, 1, D_GROUP), k_a.reshape(DEPTH, 1, D_GROUP), _head_ones())


def _rw_core_kernel(rf_ref, vf_ref, kkf_ref, kdf_ref, bdf_ref, lwf_ref,
                    rb_ref, vb_ref, kkb_ref, kdb_ref, bdb_ref, lwb_ref, s0_ref,
                    yf_ref, yb_ref, sf_ref, s_scr):
    c = pl.program_id(1)
    T = RW_T

    @pl.when(c == 0)
    def _():
        s_scr[...] = s0_ref[0]

    tj = _iota((T, T), 0)
    ts = _iota((T, T), 1)
    dirs = ((rf_ref, vf_ref, kkf_ref, kdf_ref, bdf_ref, lwf_ref, yf_ref),
            (rb_ref, vb_ref, kkb_ref, kdb_ref, bdb_ref, lwb_ref, yb_ref))
    ch = []
    for d, (r_ref, v_ref, kk_ref, kd_ref, bd_ref, lw_ref, y_ref) in enumerate(dirs):
        if d == 0:
            incl = ts <= tj
            strict = ts < tj
            last = T - 1
        else:
            incl = ts >= tj
            strict = ts > tj
            last = 0
        lw = lw_ref[0, 0]
        cum = _dot_sel_l(jnp.where(incl, 1.0, 0.0), lw)
        w_in = jnp.exp(cum)
        w_inv = jnp.exp(-cum)
        w_ex = jnp.exp(cum - lw)
        kap_a = kk_ref[0] * w_ex
        bet_a = bd_ref[0, 0] * w_inv
        khat_a = kd_ref[0, 0] * w_inv
        rho_a = r_ref[0] * w_in
        w_tot = w_in[last:last + 1, :]
        v_a = v_ref[0]
        for h in range(RW_HEADS):
            sl = slice(h * RW_HEAD, (h + 1) * RW_HEAD)
            ch.append(dict(d=d, h=h, sl=sl, incl=incl, strict=strict, y_ref=y_ref, w_tot=w_tot[:, sl],
                           kap=kap_a[:, sl], bet=bet_a[:, sl], khat=khat_a[:, sl], rho=rho_a[:, sl], v=v_a[:, sl]))
    def solve(chs):
        for q in chs:
            kr = jnp.concatenate([q['kap'], q['rho']], axis=0)
            bk = jnp.concatenate([q['bet'], q['khat']], axis=0)
            q['gram'] = _dot1(kr, bk, NT)
        for q in chs:
            gram = q.pop('gram')
            q['n'] = -jnp.where(q['strict'], gram[:T, :T], 0.0)
            l_k = jnp.where(q['strict'], gram[:T, T:], 0.0)
            q['m_b'] = jnp.where(q['incl'], gram[T:, :T], 0.0)
            m_k = jnp.where(q['incl'], gram[T:, T:], 0.0)
            q['lmv'] = _dot1(jnp.concatenate([l_k, m_k], axis=0), q['v'])
        for q in chs:
            q['x'] = jnp.concatenate([q['kap'], q['lmv'][:T]], axis=1)
        for lvl in range(6):
            mm = _dot3 if lvl < 1 else _dot1
            for q in chs:
                if lvl < 5:
                    nx = mm(q['n'], jnp.concatenate([q['x'], q['n']], axis=1))
                    q['n'], q['x'] = nx[:, 2 * RW_HEAD:], q['x'] + nx[:, :2 * RW_HEAD]
                else:
                    q['x'] = q['x'] + _dot1(q['n'], q['x'])
        for q in chs:
            q['p'] = _dot1(q['m_b'], q['x'])
            xb = _dot1(q['x'], q['bet'], TN)
            q['a_m'] = xb[:RW_HEAD]
            q['d_m'] = _dot1(q['v'], q['khat'], TN) - xb[RW_HEAD:]
        for q in chs:
            rho_p = q['rho'] - q['p'][:, :RW_HEAD]
            y_v = q['lmv'][T:] - q['p'][:, RW_HEAD:]
            s_old = s_scr[q['d'], q['h']]
            q['y_ref'][0, :, q['sl']] = _dot1(rho_p, s_old, NT) + y_v
            s_scr[q['d'], q['h']] = (s_old - _dot1(s_old, q['a_m']) + q['d_m']) * q['w_tot']

    for g0 in range(0, len(ch), RW_GROUP):
        solve(ch[g0:g0 + RW_GROUP])

    @pl.when(c == pl.num_programs(1) - 1)
    def _():
        sf_ref[0] = s_scr[...]


def _rw_core(st, r, v, kk, kd, bd, lw, s0):
    nb, L = st.nb, st.L
    nc = L // RW_T
    f1 = pl.BlockSpec((1, RW_T, D_GROUP), lambda b, c: (b, c, 0))
    b1 = pl.BlockSpec((1, RW_T, D_GROUP), lambda b, c: (b, nc - 1 - c, 0))
    f2 = pl.BlockSpec((1, 1, RW_T, D_GROUP), lambda b, c: (0, b, c, 0))
    b2 = pl.BlockSpec((1, 1, RW_T, D_GROUP), lambda b, c: (1, b, nc - 1 - c, 0))
    s_spec = pl.BlockSpec((1, 2, RW_HEADS, RW_HEAD, RW_HEAD), lambda b, c: (b, 0, 0, 0, 0))
    ys = jax.ShapeDtypeStruct((nb, L, D_GROUP), F32)
    return pl.pallas_call(
        _rw_core_kernel,
        grid=(nb, nc),
        in_specs=[f1, f1, f1, f2, f2, f2, b1, b1, b1, b2, b2, b2, s_spec],
        out_specs=[f1, b1, s_spec],
        out_shape=[ys, ys, jax.ShapeDtypeStruct((nb, 2, RW_HEADS, RW_HEAD, RW_HEAD), F32)],
        scratch_shapes=[pltpu.VMEM((2, RW_HEADS, RW_HEAD, RW_HEAD), F32)],
        compiler_params=_cparams("parallel", "arbitrary"),
        name="rwkv_core",
    )(r, v, kk, kd, bd, lw, r, v, kk, kd, bd, lw, s0)


RWO_TM = 512


def _rw_out_kernel(yf_ref, yb_ref, r_ref, v_ref, kd0_ref, kd1_ref, g_ref, lng_ref, lnb_ref, rk_ref, hones_ref, o_ref):
    y = yf_ref[...] + yb_ref[...]
    inv = 1.0 / RW_HEAD
    mu = _seg_sum(y, hones_ref[...]) * inv
    dv = y - mu
    var = _seg_sum(dv * dv, hones_ref[...]) * inv
    yn = dv * lax.rsqrt(var + RW_GN_EPS) * lng_ref[0] + lnb_ref[0]
    kmean = 0.5 * (kd0_ref[0] + kd1_ref[0])
    bonus = _seg_sum(r_ref[...] * kmean * rk_ref[0], hones_ref[...]) * v_ref[...]
    o_ref[...] = ((yn + bonus) * g_ref[...]).astype(BF16)


def _rw_out(st, yf, yb, r, v, kd, g, ln_g, ln_b, r_k, l):
    rows = st.rows
    flat = lambda a: a.reshape(rows, D_GROUP)
    row = pl.BlockSpec((RWO_TM, D_GROUP), lambda i: (i, 0))
    lay = pl.BlockSpec((1, 1, D_GROUP), lambda i: (l, 0, 0))
    kd2 = kd.reshape(2, rows, D_GROUP)
    return pl.pallas_call(
        _rw_out_kernel,
        grid=(rows // RWO_TM,),
        in_specs=[row, row, row, row,
                  pl.BlockSpec((1, RWO_TM, D_GROUP), lambda i: (0, i, 0)),
                  pl.BlockSpec((1, RWO_TM, D_GROUP), lambda i: (1, i, 0)),
                  row, lay, lay, lay, pl.BlockSpec((D_GROUP, D_GROUP), lambda i: (0, 0))],
        out_specs=row,
        out_shape=jax.ShapeDtypeStruct((rows, D_GROUP), BF16),
        compiler_params=_cparams("parallel"),
        name="rwkv_out",
    )(flat(yf), flat(yb), flat(r), flat(v), kd2, kd2, flat(g), ln_g.reshape(DEPTH, 1, D_GROUP),
      ln_b.reshape(DEPTH, 1, D_GROUP), r_k.reshape(DEPTH, 1, D_GROUP), _head_ones())


def _trunk_layer(st, x, l, mod4, p, states, ctx_kv):
    nb, L = st.nb, st.L
    z, zg = _zproj(st, x, mod4, p['w_in'], p['w_gate_cols'], l)
    z3 = z.reshape(nb, L, D_Z)

    ydir, sfr, sfi = _s5_scan(st, z3, p['s5_bq'], p['s5_cq'], p['s5_lamr'], p['s5_lami'],
                              states['s5_re'], states['s5_im'], l)
    y_s5 = _s5_out(st, z, ydir.reshape(2, st.rows, D_GROUP), p['s5_d'], p['s5_w_glu'], l)

    r, v, kk, g, kd, bd, lw = _rw_front(st, z3, p['rw_mu_prev'], p['rw_mu_next'], p['rw_w0'], p['rw_w2'],
                                        p['rw_a0'], p['rw_a2'], p['rw_g2'], p['rw_k_k'], p['rw_k_a'], l)
    yf, yb, rw_s = _rw_core(st, r, v, kk, kd, bd, lw, states['rw'])
    y_rw = _rw_out(st, yf, yb, r, v, kd, g, p['rw_ln_g'], p['rw_ln_b'], p['rw_r_k'], l)

    if ctx_kv is None:
        y_na, nk, nv = _na_ctx(st, z3)
    else:
        y_na = _na_nbr(st, z3, ctx_kv[0], ctx_kv[1], p['na_bias'], l)
        nk = nv = None

    y_ml, ml_c, ml_n, ml_m = _ml_mixer(st, z3, zg, p['ml_i_bias'], p['ml_f_bias'], p['ml_ln_g'], p['ml_ln_b'],
                                       states['ml_c'], states['ml_n'], states['ml_m'], l)

    ys = (y_s5, y_rw, y_na.reshape(st.rows, D_GROUP), y_ml.reshape(st.rows, D_GROUP))
    tail = _oproj(st, ys, x, mod4, p['w_out'], p['ln1_g'], p['ln1_b'], p['w_router'], p['b_router'], l)
    return tail, (nk, nv, sfr, sfi, rw_s, ml_c, ml_n, ml_m)


def _moe_and_ln2(tails, mod4, p, l):
    xs = []
    for st, (x1, h2, gates, keep) in zip((PROMPT, LATENT), tails):
        f = _moe_sorted(st, h2, gates, keep, p['moe_w_gate'], p['moe_w_up'], p['moe_w_down'], l)
        xs.append(_ln2f(st, x1, f, mod4, p['ln2_g'], p['ln2_b'], l))
    return xs


def kernel(x_prompt, x_sample, cache_nat_k, cache_nat_v, state_s5_re, state_s5_im, state_rwkv, state_mlstm_c, state_mlstm_n, state_mlstm_m, c, c_ctx, w_mod, b_mod, w_in, w_out, s5_lam_re, s5_lam_im, s5_log_step, s5_b_re, s5_b_im, s5_c_re, s5_c_im, s5_d, s5_w_glu, rw_mu_prev, rw_mu_next, rw_w0, rw_w2, rw_a0, rw_a2, rw_g2, rw_k_k, rw_k_a, rw_r_k, rw_ln_g, rw_ln_b, na_rpb, ml_i_bias, ml_f_bias, ml_ln_g, ml_ln_b, ln1_g, ln1_b, ln2_g, ln2_b, w_router, b_router, moe_w_gate, moe_w_up, moe_w_down):
    dt = x_prompt.dtype
    cond = jnp.concatenate([c_ctx[None, :], c, jnp.zeros((MOD_ROWS - 1 - DEC_BATCH, D_MODEL), F32)], axis=0)
    mod4 = _modulation(cond, w_mod, b_mod).reshape(DEPTH, MOD_ROWS, 1, 6 * D_MODEL)

    lbr, lbi, bbr, bbi = _s5_prep(s5_lam_re, s5_lam_im, s5_log_step, s5_b_re, s5_b_im)
    s5_bq, s5_cq, s5_lamr, s5_lami = _s5_block_params(lbr, lbi, bbr, bbi, s5_c_re, s5_c_im)
    w_in_b = w_in.astype(BF16)
    p = dict(w_in=w_in_b, w_gate_cols=w_in_b[:, :, D_Z:], w_out=w_out.astype(BF16),
             s5_bq=s5_bq, s5_cq=s5_cq, s5_lamr=s5_lamr, s5_lami=s5_lami, s5_d=s5_d, s5_w_glu=s5_w_glu,
             rw_mu_prev=rw_mu_prev, rw_mu_next=rw_mu_next, rw_w0=rw_w0, rw_w2=rw_w2, rw_a0=rw_a0, rw_a2=rw_a2,
             rw_g2=rw_g2, rw_k_k=rw_k_k, rw_k_a=rw_k_a, rw_r_k=rw_r_k, rw_ln_g=rw_ln_g, rw_ln_b=rw_ln_b,
             na_bias=_na_bias_table(na_rpb), ml_i_bias=ml_i_bias, ml_f_bias=ml_f_bias, ml_ln_g=ml_ln_g,
             ml_ln_b=ml_ln_b, ln1_g=ln1_g, ln1_b=ln1_b, ln2_g=ln2_g, ln2_b=ln2_b, w_router=w_router,
             b_router=b_router, moe_w_gate=moe_w_gate.astype(BF16), moe_w_up=moe_w_up.astype(BF16),
             moe_w_down=moe_w_down.astype(BF16))

    gp = BATCH // S5_SEQS
    zero_states = dict(
        s5_re=jnp.zeros((gp, 2, S5_SEQS, N_S5), F32), s5_im=jnp.zeros((gp, 2, S5_SEQS, N_S5), F32),
        rw=jnp.zeros((BATCH, 2, RW_HEADS, RW_HEAD, RW_HEAD), F32),
        ml_c=jnp.zeros((BATCH, 2, ML_HEADS, ML_HEAD, ML_HEAD), F32),
        ml_n=jnp.zeros((BATCH, ML_HEADS, 2, ML_HEAD), F32), ml_m=jnp.zeros((BATCH, ML_HEADS, 2, 1), F32))

    xp = x_prompt.reshape(PROMPT.rows, D_MODEL)
    xs = x_sample.reshape(LATENT.rows, D_MODEL)
    outs = [[] for _ in range(8)]
    for l in range(DEPTH):
        tail_p, ctx_t = _trunk_layer(PROMPT, xp, l, mod4, p, zero_states, None)
        for acc, t in zip(outs, ctx_t):
            acc.append(t)
        lat_states = dict(
            s5_re=state_s5_re[:, l].reshape(DEC_BATCH, 2, N_S5).transpose(1, 0, 2)[None],
            s5_im=state_s5_im[:, l].reshape(DEC_BATCH, 2, N_S5).transpose(1, 0, 2)[None],
            rw=state_rwkv[:, l], ml_c=state_mlstm_c[:, l],
            ml_n=state_mlstm_n[:, l].transpose(0, 2, 1, 3), ml_m=state_mlstm_m[:, l].transpose(0, 2, 1)[..., None])
        tail_s, _ = _trunk_layer(LATENT, xs, l, mod4, p, lat_states, (cache_nat_k, cache_nat_v))
        xp, xs = _moe_and_ln2((tail_p, tail_s), mod4, p, l)

    nk, nv, s5r, s5i, rw, mc, mn, mm = [jnp.stack(t, axis=1) for t in outs]

    def s5_state(t):
        return t.transpose(0, 3, 1, 2, 4).reshape(BATCH, DEPTH, 2, S5_GROUPS, S5_STATE)

    return (xp.reshape(BATCH, SEQ, D_MODEL), xs.reshape(DEC_BATCH, DEC_SEQ, D_MODEL),
            nk, nv, s5_state(s5r).astype(dt), s5_state(s5i).astype(dt), rw.astype(dt), mc.astype(dt),
            mn.transpose(0, 1, 3, 2, 4).astype(dt), mm[..., 0].transpose(0, 1, 3, 2).astype(dt))
```
